```python
import jax
import jax.numpy as jnp
from jax import lax
import numpy as np

D_MODEL = 1024
BATCH = 16
SEQ = 2048
DEPTH = 1

CHUNK = 64
GM_WIDTH = D_MODEL // 2
GM_HEADS = 4
GM_HEAD_DIM = GM_WIDTH // GM_HEADS
GM_CHUNK = 128
MLA_HEADS = 4
QK_NOPE_DIM = 128
QK_ROPE_DIM = 64
V_HEAD_DIM = (D_MODEL - GM_WIDTH) // MLA_HEADS
Q_LORA_RANK = 3 * D_MODEL // 8
KV_LORA_RANK = D_MODEL // 4
ROPE_THETA = 10000.0
Q_BLOCK = 128
MIX_WIDTH = GM_WIDTH + MLA_HEADS * V_HEAD_DIM
IN_WIDTH = 2 * GM_WIDTH + Q_LORA_RANK + KV_LORA_RANK + QK_ROPE_DIM
N_GROUPS = 4
EXPERTS_PER_GROUP = 8
N_EXPERTS = N_GROUPS * EXPERTS_PER_GROUP
TOP_K = 2
EXPERT_FF = D_MODEL // 4
MOE_BLOCK = 128
ALPHA = (2 * DEPTH) ** 0.25
BETA = (8 * DEPTH) ** -0.25
MAX_POS_OFFSET = 4096

kernel_name = 'hybrid_gmlp_mla_hmoe_deepnorm'


def layer_norm(x, g, b, eps=1e-5):
    xf = x.astype(jnp.float32)
    mu = jnp.mean(xf, axis=-1, keepdims=True)
    var = jnp.mean(jnp.square(xf - mu), axis=-1, keepdims=True)
    return ((xf - mu) * lax.rsqrt(var + eps) * g + b).astype(x.dtype)


def rms_norm(x, g, eps=1e-6):
    xf = x.astype(jnp.float32)
    return (xf * lax.rsqrt(jnp.mean(jnp.square(xf), axis=-1, keepdims=True) + eps) * g).astype(x.dtype)


def rope_tables(positions):
    inv_freq = ROPE_THETA ** (-jnp.arange(0, QK_ROPE_DIM, 2, dtype=jnp.float32) / QK_ROPE_DIM)
    ang = positions.astype(jnp.float32)[..., None] * inv_freq
    return jnp.cos(ang), jnp.sin(ang)


def apply_rope(x, cos, sin):
    half = x.shape[-1] // 2
    xf = x.astype(jnp.float32)
    x1, x2 = xf[..., :half], xf[..., half:]
    return jnp.concatenate([x1 * cos - x2 * sin, x2 * cos + x1 * sin], axis=-1).astype(x.dtype)


def spatial_gating(u, v, ln_g, ln_b, w_s, b_s):
    B, S, _ = u.shape
    v = v.reshape(B, S, GM_HEADS, GM_HEAD_DIM)
    v = layer_norm(v, ln_g.reshape(GM_HEADS, GM_HEAD_DIM), ln_b.reshape(GM_HEADS, GM_HEAD_DIM))
    v = v.reshape(B, S // GM_CHUNK, GM_CHUNK, GM_HEADS, GM_HEAD_DIM)
    frame_chunk = jnp.arange(GM_CHUNK) // CHUNK
    allowed = frame_chunk[None, :] <= frame_chunk[:, None]
    w_masked = jnp.where(allowed[None], w_s, jnp.zeros((), w_s.dtype))
    f = jnp.einsum('hij,bnjhc->bnihc', w_masked, v) + b_s.T[None, None, :, :, None]
    return u * f.reshape(B, S, GM_WIDTH)


def mla_attention(q_nope, q_rope, k_nope, k_rope, v):
    B, S, H, _ = q_nope.shape
    nb = S // Q_BLOCK
    scale = (QK_NOPE_DIM + QK_ROPE_DIM) ** -0.5
    key_chunk = jnp.arange(S) // CHUNK
    qn = q_nope.reshape(B, nb, Q_BLOCK, H, QK_NOPE_DIM).transpose(1, 0, 2, 3, 4)
    qr = q_rope.reshape(B, nb, Q_BLOCK, H, QK_ROPE_DIM).transpose(1, 0, 2, 3, 4)

    def one_block(args):
        qn_b, qr_b, bi = args
        q_chunk = (bi * Q_BLOCK + jnp.arange(Q_BLOCK)) // CHUNK
        s = (jnp.einsum('bqhd,bkhd->bhqk', qn_b, k_nope)
             + jnp.einsum('bqhr,bkr->bhqk', qr_b, k_rope)).astype(jnp.float32) * scale
        allowed = key_chunk[None, :] <= q_chunk[:, None]
        s = jnp.where(allowed[None, None], s, jnp.float32(-1e30))
        p = jax.nn.softmax(s, axis=-1).astype(v.dtype)
        return jnp.einsum('bhqk,bkhd->bqhd', p, v)

    o = lax.map(one_block, (qn, qr, jnp.arange(nb)))
    return o.transpose(1, 0, 2, 3, 4).reshape(B, S, H * V_HEAD_DIM)


def token_mixer(h, cos, sin, w_in, gm_ln_g, gm_ln_b, w_spatial, b_spatial,
                q_norm_g, w_uq, kv_norm_g, w_ukv, w_out):
    B, S, _ = h.shape
    z = jnp.einsum('bsd,de->bse', h, w_in)
    o1 = GM_WIDTH
    o2 = 2 * GM_WIDTH
    o3 = o2 + Q_LORA_RANK
    o4 = o3 + KV_LORA_RANK
    out_a = spatial_gating(jax.nn.gelu(z[..., :o1]), jax.nn.gelu(z[..., o1:o2]),
                           gm_ln_g, gm_ln_b, w_spatial, b_spatial)
    q = jnp.einsum('bsr,re->bse', rms_norm(z[..., o2:o3], q_norm_g), w_uq)
    q = q.reshape(B, S, MLA_HEADS, QK_NOPE_DIM + QK_ROPE_DIM)
    q_nope = q[..., :QK_NOPE_DIM]
    q_rope = apply_rope(q[..., QK_NOPE_DIM:], cos[:, :, None, :], sin[:, :, None, :])
    kv = jnp.einsum('bsr,re->bse', rms_norm(z[..., o3:o4], kv_norm_g), w_ukv)
    kv = kv.reshape(B, S, MLA_HEADS, QK_NOPE_DIM + V_HEAD_DIM)
    k_nope = kv[..., :QK_NOPE_DIM]
    v = kv[..., QK_NOPE_DIM:]
    k_rope = apply_rope(z[..., o4:], cos, sin)
    out_b = mla_attention(q_nope, q_rope, k_nope, k_rope, v)
    mix = jnp.concatenate([out_a, out_b], axis=-1)
    return jnp.einsum('bse,ed->bsd', mix, w_out)


def hierarchical_moe(h, w_router_group, b_router_group, w_router_expert, b_router_expert,
                     w_gate, w_up, w_down):
    B, S, D = h.shape
    T = B * S
    xf = h.reshape(T, D)
    xr = xf.astype(jnp.float32)
    tok_idx = jnp.arange(T)
    g_logits = xr @ w_router_group.astype(jnp.float32) + b_router_group.astype(jnp.float32)
    g_prob = jax.nn.softmax(g_logits, axis=-1)
    g_top = jnp.argmax(g_logits, axis=-1)
    p_group = g_prob[tok_idx, g_top]
    e_logits = (xr @ w_router_expert.astype(jnp.float32)
                + b_router_expert.astype(jnp.float32)).reshape(T, N_GROUPS, EXPERTS_PER_GROUP)
    e_in_group = e_logits[tok_idx, g_top]
    top_val, top_idx = lax.top_k(e_in_group, TOP_K)
    gates = p_group[:, None] * jax.nn.softmax(top_val, axis=-1)
    expert_id = g_top[:, None] * EXPERTS_PER_GROUP + top_idx
    TK = T * TOP_K
    flat_e = expert_id.reshape(TK)
    order = jnp.argsort(flat_e)
    sorted_e = flat_e[order]
    counts = jnp.bincount(flat_e, length=N_EXPERTS)
    padded = (counts + MOE_BLOCK - 1) // MOE_BLOCK * MOE_BLOCK
    pad_end = jnp.cumsum(padded)
    pad_start = pad_end - padded
    cnt_start = jnp.cumsum(counts) - counts
    dest = pad_start[sorted_e] + jnp.arange(TK) - cnt_start[sorted_e]
    n_rows = TK + N_EXPERTS * MOE_BLOCK
    n_blocks = n_rows // MOE_BLOCK
    src_tok = order // TOP_K
    buf = jnp.zeros((n_rows, D), h.dtype).at[dest].set(xf[src_tok])
    blk_expert = jnp.minimum(jnp.searchsorted(pad_end, jnp.arange(n_blocks) * MOE_BLOCK, side='right'),
                             N_EXPERTS - 1)

    def expert_block(args):
        xb, e = args
        return (jax.nn.silu(xb @ w_gate[e]) * (xb @ w_up[e])) @ w_down[e]

    out = lax.map(expert_block, (buf.reshape(n_blocks, MOE_BLOCK, D), blk_expert)).reshape(n_rows, D)
    contrib = out[dest] * gates.reshape(TK)[order][:, None].astype(out.dtype)
    y = jax.ops.segment_sum(contrib, src_tok, num_segments=T)
    return y.reshape(B, S, D)


def setup_inputs(seed: int = 0) -> dict:
    key = jax.random.key(seed)
    ks = iter(jax.random.split(key, 26))
    L = DEPTH
    f32 = jnp.float32

    def nrm(shape, scale):
        return jax.random.normal(next(ks), shape, f32) * scale

    def gain(shape):
        return 1.0 + 0.02 * jax.random.normal(next(ks), shape, f32)

    x = jax.random.normal(next(ks), (BATCH, SEQ, D_MODEL), f32)
    offsets = jax.random.randint(next(ks), (BATCH, 1), 0, MAX_POS_OFFSET, dtype=jnp.int32)
    positions = offsets + jnp.arange(SEQ, dtype=jnp.int32)[None, :]
    return {
        'x': x,
        'positions': positions,
        'ln0_g': gain((D_MODEL,)),
        'ln0_b': nrm((D_MODEL,), 0.02),
        'w_in': nrm((L, D_MODEL, IN_WIDTH), D_MODEL ** -0.5),
        'gm_ln_g': gain((L, GM_WIDTH)),
        'gm_ln_b': nrm((L, GM_WIDTH), 0.02),
        'w_spatial': nrm((L, GM_HEADS, GM_CHUNK, GM_CHUNK), 0.5 * GM_CHUNK ** -0.5),
        'b_spatial': gain((L, GM_HEADS, GM_CHUNK)),
        'q_norm_g': gain((L, Q_LORA_RANK)),
        'w_uq': nrm((L, Q_LORA_RANK, MLA_HEADS * (QK_NOPE_DIM + QK_ROPE_DIM)), Q_LORA_RANK ** -0.5),
        'kv_norm_g': gain((L, KV_LORA_RANK)),
        'w_ukv': nrm((L, KV_LORA_RANK, MLA_HEADS * (QK_NOPE_DIM + V_HEAD_DIM)), KV_LORA_RANK ** -0.5),
        'w_out': nrm((L, MIX_WIDTH, D_MODEL), BETA * MIX_WIDTH ** -0.5),
        'ln1_g': gain((L, D_MODEL)),
        'ln1_b': nrm((L, D_MODEL), 0.02),
        'w_router_group': nrm((L, D_MODEL, N_GROUPS), D_MODEL ** -0.5),
        'b_router_group': nrm((L, N_GROUPS), 0.01),
        'w_router_expert': nrm((L, D_MODEL, N_EXPERTS), D_MODEL ** -0.5),
        'b_router_expert': nrm((L, N_EXPERTS), 0.01),
        'w_gate': nrm((L, N_EXPERTS, D_MODEL, EXPERT_FF), D_MODEL ** -0.5),
        'w_up': nrm((L, N_EXPERTS, D_MODEL, EXPERT_FF), D_MODEL ** -0.5),
        'w_down': nrm((L, N_EXPERTS, EXPERT_FF, D_MODEL), BETA * EXPERT_FF ** -0.5),
        'ln2_g': gain((L, D_MODEL)),
        'ln2_b': nrm((L, D_MODEL), 0.02),
    }


def reference(x, positions, ln0_g, ln0_b, w_in, gm_ln_g, gm_ln_b, w_spatial, b_spatial,
              q_norm_g, w_uq, kv_norm_g, w_ukv, w_out, ln1_g, ln1_b,
              w_router_group, b_router_group, w_router_expert, b_router_expert,
              w_gate, w_up, w_down, ln2_g, ln2_b):
    cos, sin = rope_tables(positions)
    h = layer_norm(x, ln0_g, ln0_b)
    for l in range(DEPTH):
        mix = token_mixer(h, cos, sin, w_in[l], gm_ln_g[l], gm_ln_b[l], w_spatial[l], b_spatial[l],
                          q_norm_g[l], w_uq[l], kv_norm_g[l], w_ukv[l], w_out[l])
        h = layer_norm(ALPHA * h + mix, ln1_g[l], ln1_b[l])
        ffn = hierarchical_moe(h, w_router_group[l], b_router_group[l], w_router_expert[l],
                               b_router_expert[l], w_gate[l], w_up[l], w_down[l])
        h = layer_norm(ALPHA * h + ffn, ln2_g[l], ln2_b[l])
    return h
```

```python
import functools
import math

import jax
import jax.numpy as jnp
from jax import lax
from jax.experimental import pallas as pl
from jax.experimental.pallas import tpu as pltpu

D_MODEL = 1024
BATCH = 16
SEQ = 2048
N_TOK = BATCH * SEQ
CHUNK = 64
GM_WIDTH = 512
GM_HEADS = 4
GM_HEAD_DIM = 128
GM_CHUNK = 128
MLA_HEADS = 4
QK_NOPE_DIM = 128
QK_ROPE_DIM = 64
V_HEAD_DIM = 128
Q_LORA_RANK = 384
KV_LORA_RANK = 256
ROPE_THETA = 10000.0
N_GROUPS = 4
EXPERTS_PER_GROUP = 8
N_EXPERTS = 32
TOP_K = 2
EXPERT_FF = 256
ALPHA = 2.0 ** 0.25
QK_SCALE = (QK_NOPE_DIM + QK_ROPE_DIM) ** -0.5

LANES = 128
SUBLANES = 8
FEAT_TILES = D_MODEL // LANES

PREP_TOKENS = 512
ATT_Q = 256
ATT_K = 256
N_QBLK = SEQ // ATT_Q
PLAN_TOKENS = 512
DISPATCH_TOKENS = 512
EXPERT_ROWS = 256
COMBINE_TOKENS = 256
N_ROWS = N_TOK * TOP_K + N_EXPERTS * EXPERT_ROWS
N_ROW_BLOCKS = N_ROWS // EXPERT_ROWS
META_LANES = 384
IN_COLS = 2 * GM_WIDTH + Q_LORA_RANK + KV_LORA_RANK + 2 * QK_ROPE_DIM
O_Q = 2 * GM_WIDTH
O_KV = O_Q + Q_LORA_RANK
O_KR = O_KV + KV_LORA_RANK
ROUTER_ROWS = 40
VMEM_LIMIT = 48 * 1024 * 1024

assert N_ROW_BLOCKS <= META_LANES


def _layer_norm(x, g, b, eps=1e-5):
    mu = jnp.mean(x, axis=-1, keepdims=True)
    xc = x - mu
    var = jnp.mean(xc * xc, axis=-1, keepdims=True)
    return xc * lax.rsqrt(var + eps) * g + b


def _rms_norm(x, g, eps=1e-6):
    return x * lax.rsqrt(jnp.mean(x * x, axis=-1, keepdims=True) + eps) * g


def _gelu_tanh(x):
    c = math.sqrt(2.0 / math.pi)
    return 0.5 * x * (1.0 + jnp.tanh(c * (x + 0.044715 * (x * x * x))))


def _to_row_tiles(ref, x):
    n = x.shape[0]
    for s in range(FEAT_TILES):
        ref[pl.ds(s, n, stride=FEAT_TILES), :] = x[:, s * LANES:(s + 1) * LANES]


def _from_row_tiles(ref, n):
    return jnp.concatenate(
        [ref[pl.ds(s, n, stride=FEAT_TILES), :] for s in range(FEAT_TILES)], axis=-1)


def _prep_kernel(x_ref, pos_ref, ln0g_ref, ln0b_ref, win_ref, gmg_ref, gmb_ref, ws_ref, bs_ref,
                 qg_ref, wuq_ref, kvg_ref, wukv_ref, freq_ref, phase_ref, sign_ref,
                 h_ref, outa_ref, q_ref, k_ref, v_ref):
    tb = x_ref.shape[0]
    h = _layer_norm(x_ref[...], ln0g_ref[...], ln0b_ref[...])
    h_ref[...] = h
    z = jnp.dot(h.astype(jnp.bfloat16), win_ref[...], preferred_element_type=jnp.float32)

    ang = pos_ref[...].astype(jnp.float32) * freq_ref[...]
    rot = jnp.cos(ang - phase_ref[...]) * sign_ref[...]

    u = _gelu_tanh(z[:, :GM_WIDTH])
    v = _gelu_tanh(z[:, GM_WIDTH:2 * GM_WIDTH])
    row_chunk = lax.broadcasted_iota(jnp.int32, (GM_CHUNK, GM_CHUNK), 0) // CHUNK
    col_chunk = lax.broadcasted_iota(jnp.int32, (GM_CHUNK, GM_CHUNK), 1) // CHUNK
    allowed = col_chunk <= row_chunk
    for hd in range(GM_HEADS):
        lo, hi = hd * GM_HEAD_DIM, (hd + 1) * GM_HEAD_DIM
        vln = _layer_norm(v[:, lo:hi], gmg_ref[:, lo:hi], gmb_ref[:, lo:hi]).astype(jnp.bfloat16)
        wm = jnp.where(allowed, ws_ref[hd], 0.0).astype(jnp.bfloat16)
        for c in range(tb // GM_CHUNK):
            r0, r1 = c * GM_CHUNK, (c + 1) * GM_CHUNK
            f = jnp.dot(wm, vln[r0:r1], preferred_element_type=jnp.float32) + bs_ref[hd]
            outa_ref[r0:r1, lo:hi] = (u[r0:r1, lo:hi] * f).astype(jnp.bfloat16)

    ql = _rms_norm(z[:, O_Q:O_KV], qg_ref[...]).astype(jnp.bfloat16)
    qf = jnp.dot(ql, wuq_ref[...], preferred_element_type=jnp.float32)
    rot_s = rot * QK_SCALE
    q_parts = []
    for hd in range(MLA_HEADS):
        base = hd * 2 * LANES
        q_parts.append(qf[:, base:base + LANES] * QK_SCALE)
        q_parts.append(qf[:, base + LANES:base + 2 * LANES] * rot_s)
    q_ref[...] = jnp.concatenate(q_parts, axis=-1).astype(jnp.bfloat16)

    kvl = _rms_norm(z[:, O_KV:O_KR], kvg_ref[...]).astype(jnp.bfloat16)
    kv = jnp.dot(kvl, wukv_ref[...], preferred_element_type=jnp.float32)
    t = z[:, O_KR:O_KR + LANES] * rot
    krr = t + pltpu.roll(t, 2 * QK_ROPE_DIM // 2, axis=1)
    k_parts, v_parts = [], []
    for hd in range(MLA_HEADS):
        base = hd * 2 * LANES
        k_parts.append(kv[:, base:base + LANES])
        k_parts.append(krr)
        v_parts.append(kv[:, base + LANES:base + 2 * LANES])
    k_ref[...] = jnp.concatenate(k_parts, axis=-1).astype(jnp.bfloat16)
    v_ref[...] = jnp.concatenate(v_parts, axis=-1).astype(jnp.bfloat16)


def _prep(x2, pos2, ln0g, ln0b, win, gmg, gmb, ws, bs, qg, wuq, kvg, wukv, freq, phase, sign):
    tb = PREP_TOKENS
    full = lambda shape: pl.BlockSpec(shape, lambda i: (0,) * len(shape))
    tok = lambda cols: pl.BlockSpec((tb, cols), lambda i: (i, 0))
    return pl.pallas_call(
        _prep_kernel,
        grid=(N_TOK // tb,),
        in_specs=[tok(D_MODEL), tok(1), full((1, D_MODEL)), full((1, D_MODEL)),
                  full((D_MODEL, IN_COLS)), full((1, GM_WIDTH)), full((1, GM_WIDTH)),
                  full((GM_HEADS, GM_CHUNK, GM_CHUNK)), full((GM_HEADS, GM_CHUNK, GM_HEAD_DIM)),
                  full((1, Q_LORA_RANK)), full((Q_LORA_RANK, D_MODEL)),
                  full((1, KV_LORA_RANK)), full((KV_LORA_RANK, D_MODEL)),
                  full((1, LANES)), full((1, LANES)), full((1, LANES))],
        out_specs=[tok(D_MODEL), tok(GM_WIDTH), tok(D_MODEL), tok(D_MODEL), tok(GM_WIDTH)],
        out_shape=[jax.ShapeDtypeStruct((N_TOK, D_MODEL), jnp.float32),
                   jax.ShapeDtypeStruct((N_TOK, GM_WIDTH), jnp.bfloat16),
                   jax.ShapeDtypeStruct((N_TOK, D_MODEL), jnp.bfloat16),
                   jax.ShapeDtypeStruct((N_TOK, D_MODEL), jnp.bfloat16),
                   jax.ShapeDtypeStruct((N_TOK, GM_WIDTH), jnp.bfloat16)],
        compiler_params=pltpu.CompilerParams(
            dimension_semantics=("arbitrary",), vmem_limit_bytes=VMEM_LIMIT),
        name="prep",
    )(x2, pos2, ln0g, ln0b, win, gmg, gmb, ws, bs, qg, wuq, kvg, wukv, freq, phase, sign)


def _attn_kernel(q_ref, k_ref, v_ref, outa_ref, h_ref, wout_ref, ln1g_ref, ln1b_ref,
                 wr_ref, br_ref, h1r_ref, ri_ref, rf_ref):
    qi = pl.program_id(1)
    tq = q_ref.shape[0]
    nt = (((1,), (1,)), ((), ()))

    row_chunk = (lax.broadcasted_iota(jnp.int32, (tq, ATT_K), 0)) // CHUNK
    col_chunk = (lax.broadcasted_iota(jnp.int32, (tq, ATT_K), 1)) // CHUNK
    diag_allowed = col_chunk <= row_chunk

    heads = []
    for hd in range(MLA_HEADS):
        q_h = q_ref[:, hd * 2 * LANES:(hd + 1) * 2 * LANES]

        def scores(j, q_h=q_h, hd=hd):
            start = pl.multiple_of(j * ATT_K, ATT_K)
            kb = k_ref[pl.ds(start, ATT_K), hd * 2 * LANES:(hd + 1) * 2 * LANES]
            vb = v_ref[pl.ds(start, ATT_K), hd * LANES:(hd + 1) * LANES]
            return lax.dot_general(q_h, kb, nt, preferred_element_type=jnp.float32), vb

        def update(carry, s, vb):
            m, l, acc = carry
            m_new = jnp.maximum(m, jnp.max(s, axis=-1, keepdims=True))
            a = jnp.exp(m - m_new)
            p = jnp.exp(s - m_new)
            l = a * l + jnp.sum(p, axis=-1, keepdims=True)
            acc = a * acc + jnp.dot(p.astype(jnp.bfloat16), vb, preferred_element_type=jnp.float32)
            return m_new, l, acc

        def body(j, carry):
            s, vb = scores(j)
            return update(carry, s, vb)

        init = (jnp.full((tq, 1), -jnp.inf, jnp.float32), jnp.zeros((tq, 1), jnp.float32),
                jnp.zeros((tq, V_HEAD_DIM), jnp.float32))
        carry = lax.fori_loop(0, qi, body, init)
        s, vb = scores(qi)
        s = jnp.where(diag_allowed, s, jnp.float32(-1e30))
        m, l, acc = update(carry, s, vb)
        heads.append((acc / l).astype(jnp.bfloat16))

    mix_b = jnp.concatenate(heads, axis=-1)
    proj = (jnp.dot(outa_ref[...], wout_ref[:GM_WIDTH, :], preferred_element_type=jnp.float32)
            + jnp.dot(mix_b, wout_ref[GM_WIDTH:, :], preferred_element_type=jnp.float32))
    h1 = _layer_norm(ALPHA * h_ref[...] + proj, ln1g_ref[...], ln1b_ref[...])
    _to_row_tiles(h1r_ref, h1)

    h_hi = h1.astype(jnp.bfloat16)
    h_lo = (h1 - h_hi.astype(jnp.float32)).astype(jnp.bfloat16)
    w = wr_ref[...]
    w_hi = w.astype(jnp.bfloat16)
    w_lo = (w - w_hi.astype(jnp.float32)).astype(jnp.bfloat16)
    logits = (lax.dot_general(w_hi, h_hi, nt, preferred_element_type=jnp.float32)
              + lax.dot_general(w_lo, h_hi, nt, preferred_element_type=jnp.float32)
              + lax.dot_general(w_hi, h_lo, nt, preferred_element_type=jnp.float32)
              + br_ref[...])

    sub_i = lax.broadcasted_iota(jnp.int32, (SUBLANES, tq), 0)
    sub = sub_i.astype(jnp.float32)
    neg = jnp.float32(-jnp.inf)
    g = jnp.where(sub_i < N_GROUPS, logits[0:SUBLANES], neg)
    gmax = jnp.max(g, axis=0, keepdims=True)
    g_top = jnp.min(jnp.where(g == gmax, sub, float(SUBLANES)), axis=0, keepdims=True)
    p_group = 1.0 / jnp.sum(jnp.exp(g - gmax), axis=0, keepdims=True)
    sel = logits[SUBLANES:2 * SUBLANES]
    for grp in range(1, N_GROUPS):
        sel = jnp.where(g_top == float(grp), logits[(grp + 1) * SUBLANES:(grp + 2) * SUBLANES], sel)
    v1 = jnp.max(sel, axis=0, keepdims=True)
    i1 = jnp.min(jnp.where(sel == v1, sub, float(SUBLANES)), axis=0, keepdims=True)
    sel2 = jnp.where(sub == i1, neg, sel)
    v2 = jnp.max(sel2, axis=0, keepdims=True)
    i2 = jnp.min(jnp.where(sel2 == v2, sub, float(SUBLANES)), axis=0, keepdims=True)
    e21 = jnp.exp(v2 - v1)
    w1 = 1.0 / (1.0 + e21)
    gate1 = p_group * w1
    gate2 = p_group * (e21 * w1)
    e1 = g_top * EXPERTS_PER_GROUP + i1
    e2 = g_top * EXPERTS_PER_GROUP + i2
    ri_ref[...] = jnp.where(sub_i == 0, e1, jnp.where(sub_i == 1, e2, 0.0)).astype(jnp.int32)
    rf_ref[...] = jnp.where(sub_i == 0, gate1, jnp.where(sub_i == 1, gate2, 0.0))


def _attn(q, k, v, outa, h, wout, ln1g, ln1b, wr, br):
    tq = ATT_Q
    tokblk = lambda cols: pl.BlockSpec((tq, cols), lambda b, i: (b * N_QBLK + i, 0))
    seqblk = lambda cols: pl.BlockSpec((SEQ, cols), lambda b, i: (b, 0))
    full = lambda shape: pl.BlockSpec(shape, lambda b, i: (0,) * len(shape))
    return pl.pallas_call(
        _attn_kernel,
        grid=(BATCH, N_QBLK),
        in_specs=[tokblk(D_MODEL), seqblk(D_MODEL), seqblk(GM_WIDTH), tokblk(GM_WIDTH),
                  tokblk(D_MODEL), full((D_MODEL, D_MODEL)), full((1, D_MODEL)), full((1, D_MODEL)),
                  full((ROUTER_ROWS, D_MODEL)), full((ROUTER_ROWS, 1))],
        out_specs=[pl.BlockSpec((tq * FEAT_TILES, LANES), lambda b, i: (b * N_QBLK + i, 0)),
                   pl.BlockSpec((SUBLANES, tq), lambda b, i: (0, b * N_QBLK + i)),
                   pl.BlockSpec((SUBLANES, tq), lambda b, i: (0, b * N_QBLK + i))],
        out_shape=[jax.ShapeDtypeStruct((N_TOK * FEAT_TILES, LANES), jnp.float32),
                   jax.ShapeDtypeStruct((SUBLANES, N_TOK), jnp.int32),
                   jax.ShapeDtypeStruct((SUBLANES, N_TOK), jnp.float32)],
        compiler_params=pltpu.CompilerParams(
            dimension_semantics=("arbitrary", "arbitrary"), vmem_limit_bytes=VMEM_LIMIT),
        name="attn",
    )(q, k, v, outa, h, wout, ln1g, ln1b, wr, br)


def _plan_kernel(ri_ref, dest_ref, meta_ref, cnt_ref, run_ref, start_ref):
    phase = pl.program_id(0)
    blk = pl.program_id(1)
    tb = ri_ref.shape[1]
    f32 = jnp.float32

    e_sub = lax.broadcasted_iota(jnp.int32, (N_EXPERTS, tb), 0)
    oh1 = e_sub == ri_ref[0:1, :]
    oh2 = e_sub == ri_ref[1:2, :]
    oh = jnp.where(oh1 | oh2, 1.0, 0.0).astype(f32)
    blk_count = jnp.sum(oh, axis=1, keepdims=True)

    @pl.when((phase == 0) & (blk == 0))
    def _():
        cnt_ref[...] = jnp.zeros_like(cnt_ref)

    @pl.when(phase == 0)
    def _():
        cnt_ref[...] = cnt_ref[...] + blk_count

    @pl.when((phase == 1) & (blk == 0))
    def _():
        counts = cnt_ref[:, 0:1]
        padded = jnp.floor((counts + (EXPERT_ROWS - 1)) * (1.0 / EXPERT_ROWS)) * EXPERT_ROWS
        er = lax.broadcasted_iota(jnp.int32, (N_EXPERTS, LANES), 0)
        ec = lax.broadcasted_iota(jnp.int32, (N_EXPERTS, LANES), 1)
        padded_row = jnp.sum(jnp.where(er == ec, padded, 0.0), axis=0, keepdims=True)
        counts_row = jnp.sum(jnp.where(er == ec, counts, 0.0), axis=0, keepdims=True)
        pad_end = jnp.sum(jnp.where(ec <= er, padded_row, 0.0), axis=1, keepdims=True)
        pad_end_row = jnp.sum(jnp.where(er == ec, pad_end, 0.0), axis=0, keepdims=True)
        start_ref[...] = jnp.broadcast_to(pad_end - padded, start_ref.shape)
        run_ref[...] = jnp.zeros_like(run_ref)
        bstart = (lax.broadcasted_iota(jnp.int32, (N_EXPERTS, META_LANES), 1) * EXPERT_ROWS).astype(f32)
        blk_e = jnp.sum(jnp.where(pad_end <= bstart, 1.0, 0.0), axis=0, keepdims=True)
        blk_e = jnp.minimum(blk_e, N_EXPERTS - 1.0)
        n_used = pad_end[N_EXPERTS - 1:N_EXPERTS, :] * (1.0 / EXPERT_ROWS)
        pad3 = lambda r: jnp.concatenate(
            [r, jnp.zeros((1, META_LANES - LANES), f32)], axis=1)
        msub = lax.broadcasted_iota(jnp.int32, (SUBLANES, META_LANES), 0)
        meta = jnp.where(msub == 0, blk_e,
                         jnp.where(msub == 1, pad3(pad_end_row),
                                   jnp.where(msub == 2, pad3(counts_row),
                                             jnp.where(msub == 3, n_used, 0.0))))
        meta_ref[...] = meta.astype(jnp.int32)

    @pl.when(phase == 1)
    def _():
        tr = lax.broadcasted_iota(jnp.int32, (tb, tb), 0)
        tc = lax.broadcasted_iota(jnp.int32, (tb, tb), 1)
        upper = jnp.where(tr < tc, 1.0, 0.0).astype(jnp.bfloat16)
        prefix = jnp.dot(oh.astype(jnp.bfloat16), upper, preferred_element_type=f32)
        base = prefix + run_ref[:, 0:1] + start_ref[:, 0:1]
        d1 = jnp.sum(jnp.where(oh1, base, 0.0), axis=0, keepdims=True)
        d2 = jnp.sum(jnp.where(oh2, base, 0.0), axis=0, keepdims=True)
        sub = lax.broadcasted_iota(jnp.int32, (SUBLANES, tb), 0)
        dest_ref[...] = jnp.where(sub == 0, d1, jnp.where(sub == 1, d2, 0.0)).astype(jnp.int32)
        run_ref[...] = run_ref[...] + blk_count


def _plan(ri):
    tb = PLAN_TOKENS
    return pl.pallas_call(
        _plan_kernel,
        grid=(2, N_TOK // tb),
        in_specs=[pl.BlockSpec((SUBLANES, tb), lambda p, i: (0, i))],
        out_specs=[pl.BlockSpec((SUBLANES, tb), lambda p, i: (0, i * p)),
                   pl.BlockSpec((SUBLANES, META_LANES), lambda p, i: (0, 0))],
        out_shape=[jax.ShapeDtypeStruct((SUBLANES, N_TOK), jnp.int32),
                   jax.ShapeDtypeStruct((SUBLANES, META_LANES), jnp.int32)],
        scratch_shapes=[pltpu.VMEM((N_EXPERTS, LANES), jnp.float32),
                        pltpu.VMEM((N_EXPERTS, LANES), jnp.float32),
                        pltpu.VMEM((N_EXPERTS, LANES), jnp.float32)],
        compiler_params=pltpu.CompilerParams(dimension_semantics=("arbitrary", "arbitrary")),
        name="plan",
    )(ri)


def _row_copy(src, src_row, dst, dst_row, sem):
    return pltpu.make_async_copy(
        src.at[pl.ds(pl.multiple_of(src_row * FEAT_TILES, FEAT_TILES), FEAT_TILES)],
        dst.at[pl.ds(pl.multiple_of(dst_row * FEAT_TILES, FEAT_TILES), FEAT_TILES)], sem)


def _dispatch_kernel(meta_ref, dest_ref, h1r_ref, buf_ref, zero_ref, sem0, sem1, zsem):
    tb = dest_ref.shape[1]

    @pl.when(pl.program_id(0) == 0)
    def _():
        zero_ref[...] = jnp.zeros_like(zero_ref)

        def zero_copy(e):
            start = pl.multiple_of((meta_ref[1, e] - EXPERT_ROWS) * FEAT_TILES, EXPERT_ROWS * FEAT_TILES)
            return pltpu.make_async_copy(
                zero_ref, buf_ref.at[pl.ds(start, EXPERT_ROWS * FEAT_TILES)], zsem)

        def start_zero(e, c):
            @pl.when(meta_ref[2, e] > 0)
            def _():
                zero_copy(e).start()
            return c

        def wait_zero(e, c):
            @pl.when(meta_ref[2, e] > 0)
            def _():
                zero_copy(e).wait()
            return c

        def tail_copy(b):
            start = pl.multiple_of(b * (EXPERT_ROWS * FEAT_TILES), EXPERT_ROWS * FEAT_TILES)
            return pltpu.make_async_copy(
                zero_ref, buf_ref.at[pl.ds(start, EXPERT_ROWS * FEAT_TILES)], zsem)

        def start_tail(b, c):
            tail_copy(b).start()
            return c

        def wait_tail(b, c):
            tail_copy(b).wait()
            return c

        lax.fori_loop(0, N_EXPERTS, start_zero, 0)
        lax.fori_loop(meta_ref[3, 0], N_ROW_BLOCKS, start_tail, 0)
        lax.fori_loop(0, N_EXPERTS, wait_zero, 0)
        lax.fori_loop(meta_ref[3, 0], N_ROW_BLOCKS, wait_tail, 0)

    def issue(t, c):
        _row_copy(h1r_ref, t, buf_ref, dest_ref[0, t], sem0).start()
        _row_copy(h1r_ref, t, buf_ref, dest_ref[1, t], sem1).start()
        return c

    def drain(t, c):
        _row_copy(h1r_ref, t, buf_ref, dest_ref[0, t], sem0).wait()
        _row_copy(h1r_ref, t, buf_ref, dest_ref[1, t], sem1).wait()
        return c

    lax.fori_loop(0, tb, issue, 0)
    lax.fori_loop(0, tb, drain, 0)


def _dispatch(meta, dest, h1r):
    tb = DISPATCH_TOKENS
    return pl.pallas_call(
        _dispatch_kernel,
        grid_spec=pltpu.PrefetchScalarGridSpec(
            num_scalar_prefetch=1,
            grid=(N_TOK // tb,),
            in_specs=[pl.BlockSpec((SUBLANES, tb), lambda i, m: (0, i), memory_space=pltpu.SMEM),
                      pl.BlockSpec((tb * FEAT_TILES, LANES), lambda i, m: (i, 0))],
            out_specs=pl.BlockSpec(memory_space=pl.ANY),
            scratch_shapes=[pltpu.VMEM((EXPERT_ROWS * FEAT_TILES, LANES), jnp.float32),
                            pltpu.SemaphoreType.DMA, pltpu.SemaphoreType.DMA,
                            pltpu.SemaphoreType.DMA]),
        out_shape=jax.ShapeDtypeStruct((N_ROWS * FEAT_TILES, LANES), jnp.float32),
        compiler_params=pltpu.CompilerParams(dimension_semantics=("arbitrary",)),
        name="dispatch",
    )(meta, dest, h1r)


def _experts_kernel(meta_ref, x_ref, wg_ref, wu_ref, wd_ref, o_ref):
    n = EXPERT_ROWS

    @pl.when(pl.program_id(0) < meta_ref[3, 0])
    def _():
        x = _from_row_tiles(x_ref, n).astype(jnp.bfloat16)
        gate = jnp.dot(x, wg_ref[...], preferred_element_type=jnp.float32)
        up = jnp.dot(x, wu_ref[...], preferred_element_type=jnp.float32)
        act = (gate * jax.nn.sigmoid(gate) * up).astype(jnp.bfloat16)
        _to_row_tiles(o_ref, jnp.dot(act, wd_ref[...], preferred_element_type=jnp.float32))

    @pl.when(pl.program_id(0) >= meta_ref[3, 0])
    def _():
        o_ref[...] = jnp.zeros_like(o_ref)


def _experts(meta, buf, wg, wu, wd):
    used = lambda i, m: jnp.minimum(i, m[3, 0] - 1)
    return pl.pallas_call(
        _experts_kernel,
        grid_spec=pltpu.PrefetchScalarGridSpec(
            num_scalar_prefetch=1,
            grid=(N_ROW_BLOCKS,),
            in_specs=[pl.BlockSpec((EXPERT_ROWS * FEAT_TILES, LANES), lambda i, m: (used(i, m), 0)),
                      pl.BlockSpec((None, D_MODEL, EXPERT_FF), lambda i, m: (m[0, used(i, m)], 0, 0)),
                      pl.BlockSpec((None, D_MODEL, EXPERT_FF), lambda i, m: (m[0, used(i, m)], 0, 0)),
                      pl.BlockSpec((None, EXPERT_FF, D_MODEL), lambda i, m: (m[0, used(i, m)], 0, 0))],
            out_specs=pl.BlockSpec((EXPERT_ROWS * FEAT_TILES, LANES), lambda i, m: (i, 0))),
        out_shape=jax.ShapeDtypeStruct((N_ROWS * FEAT_TILES, LANES), jnp.float32),
        compiler_params=pltpu.CompilerParams(
            dimension_semantics=("arbitrary",), vmem_limit_bytes=VMEM_LIMIT),
        name="experts",
    )(meta, buf, wg, wu, wd)


def _combine_kernel(dest_ref, rf_ref, h1r_ref, eout_ref, ln2g_ref, ln2b_ref, o_ref,
                    y0_ref, y1_ref, sem0, sem1):
    tb = dest_ref.shape[1]

    def issue(t, c):
        _row_copy(eout_ref, dest_ref[0, t], y0_ref, t, sem0).start()
        _row_copy(eout_ref, dest_ref[1, t], y1_ref, t, sem1).start()
        return c

    def drain(t, c):
        _row_copy(eout_ref, dest_ref[0, t], y0_ref, t, sem0).wait()
        _row_copy(eout_ref, dest_ref[1, t], y1_ref, t, sem1).wait()
        return c

    lax.fori_loop(0, tb, issue, 0)
    lax.fori_loop(0, tb, drain, 0)

    gates = rf_ref[...].T
    y = (gates[:, 0:1] * _from_row_tiles(y0_ref, tb)
         + gates[:, 1:2] * _from_row_tiles(y1_ref, tb))
    h1 = _from_row_tiles(h1r_ref, tb)
    o_ref[...] = _layer_norm(ALPHA * h1 + y, ln2g_ref[...], ln2b_ref[...])


def _combine(dest, rf, h1r, eout, ln2g, ln2b):
    tb = COMBINE_TOKENS
    return pl.pallas_call(
        _combine_kernel,
        grid=(N_TOK // tb,),
        in_specs=[pl.BlockSpec((SUBLANES, tb), lambda i: (0, i), memory_space=pltpu.SMEM),
                  pl.BlockSpec((SUBLANES, tb), lambda i: (0, i)),
                  pl.BlockSpec((tb * FEAT_TILES, LANES), lambda i: (i, 0)),
                  pl.BlockSpec(memory_space=pl.ANY),
                  pl.BlockSpec((1, D_MODEL), lambda i: (0, 0)),
                  pl.BlockSpec((1, D_MODEL), lambda i: (0, 0))],
        out_specs=pl.BlockSpec((tb, D_MODEL), lambda i: (i, 0)),
        out_shape=jax.ShapeDtypeStruct((N_TOK, D_MODEL), jnp.float32),
        scratch_shapes=[pltpu.VMEM((tb * FEAT_TILES, LANES), jnp.float32),
                        pltpu.VMEM((tb * FEAT_TILES, LANES), jnp.float32),
                        pltpu.SemaphoreType.DMA, pltpu.SemaphoreType.DMA],
        compiler_params=pltpu.CompilerParams(
            dimension_semantics=("arbitrary",), vmem_limit_bytes=VMEM_LIMIT),
        name="combine",
    )(dest, rf, h1r, eout, ln2g, ln2b)


def _swap_halves(w):
    half = w.shape[-1] // 2
    return jnp.concatenate([w[..., half:], w[..., :half]], axis=-1)


def kernel(x, positions, ln0_g, ln0_b, w_in, gm_ln_g, gm_ln_b, w_spatial, b_spatial, q_norm_g, w_uq, kv_norm_g, w_ukv, w_out, ln1_g, ln1_b, w_router_group, b_router_group, w_router_expert, b_router_expert, w_gate, w_up, w_down, ln2_g, ln2_b):
    bf16 = jnp.bfloat16
    row = lambda a: a.reshape(1, -1)

    w_in0 = w_in[0]
    kr_cols = w_in0[:, O_KR:O_KR + QK_ROPE_DIM]
    win = jnp.concatenate([w_in0, _swap_halves(kr_cols)], axis=1).astype(bf16)
    wuq3 = w_uq[0].reshape(Q_LORA_RANK, MLA_HEADS, QK_NOPE_DIM + QK_ROPE_DIM)
    rope_cols = wuq3[:, :, QK_NOPE_DIM:]
    wuq = jnp.concatenate([wuq3, _swap_halves(rope_cols)], axis=-1).reshape(Q_LORA_RANK, D_MODEL).astype(bf16)
    wukv = w_ukv[0].astype(bf16)
    wout = w_out[0].astype(bf16)
    bs = jnp.broadcast_to(b_spatial[0][:, :, None], (GM_HEADS, GM_CHUNK, GM_HEAD_DIM))
    wr = jnp.concatenate([w_router_group[0].T, jnp.zeros((SUBLANES - N_GROUPS, D_MODEL), jnp.float32),
                          w_router_expert[0].T], axis=0)
    br = jnp.concatenate([b_router_group[0], jnp.zeros((SUBLANES - N_GROUPS,), jnp.float32),
                          b_router_expert[0]]).reshape(ROUTER_ROWS, 1)
    wg = w_gate[0].astype(bf16)
    wu = w_up[0].astype(bf16)
    wd = w_down[0].astype(bf16)

    inv_freq = ROPE_THETA ** (-jnp.arange(0, QK_ROPE_DIM, 2, dtype=jnp.float32) / QK_ROPE_DIM)
    freq = jnp.tile(inv_freq, 4).reshape(1, LANES)
    quarter = QK_ROPE_DIM // 2
    phase = jnp.concatenate([jnp.zeros((2 * quarter,), jnp.float32),
                             jnp.full((2 * quarter,), math.pi / 2, jnp.float32)]).reshape(1, LANES)
    sign = jnp.concatenate([jnp.ones((2 * quarter,), jnp.float32), -jnp.ones((quarter,), jnp.float32),
                            jnp.ones((quarter,), jnp.float32)]).reshape(1, LANES)

    x2 = x.reshape(N_TOK, D_MODEL)
    pos2 = positions.reshape(N_TOK, 1)

    h, outa, q, k, v = _prep(x2, pos2, row(ln0_g), row(ln0_b), win, row(gm_ln_g[0]), row(gm_ln_b[0]),
                             w_spatial[0], bs, row(q_norm_g[0]), wuq, row(kv_norm_g[0]), wukv,
                             freq, phase, sign)
    h1r, ri, rf = _attn(q, k, v, outa, h, wout, row(ln1_g[0]), row(ln1_b[0]), wr, br)
    dest, meta = _plan(ri)
    buf = _dispatch(meta, dest, h1r)
    eout = _experts(meta, buf, wg, wu, wd)
    out = _combine(dest, rf, h1r, eout, row(ln2_g[0]), row(ln2_b[0]))
    return out.reshape(BATCH, SEQ, D_MODEL)
```

```python
import functools
import math

import jax
import jax.numpy as jnp
from jax import lax
from jax.experimental import pallas as pl
from jax.experimental.pallas import tpu as pltpu

D_MODEL = 1024
BATCH = 16
SEQ = 2048
N_TOK = BATCH * SEQ
CHUNK = 64
GM_WIDTH = 512
GM_HEADS = 4
GM_HEAD_DIM = 128
GM_CHUNK = 128
MLA_HEADS = 4
QK_NOPE_DIM = 128
QK_ROPE_DIM = 64
V_HEAD_DIM = 128
Q_LORA_RANK = 384
KV_LORA_RANK = 256
ROPE_THETA = 10000.0
N_GROUPS = 4
EXPERTS_PER_GROUP = 8
N_EXPERTS = 32
TOP_K = 2
EXPERT_FF = 256
ALPHA = 2.0 ** 0.25
QK_SCALE = (QK_NOPE_DIM + QK_ROPE_DIM) ** -0.5 * math.log2(math.e)

LANES = 128
SUBLANES = 8
FEAT_TILES = D_MODEL // LANES

PREP_TOKENS = 512
ATT_Q = 256
ATT_K = 256
N_QBLK = SEQ // ATT_Q
PLAN_TOKENS = 512
DISPATCH_TOKENS = 512
EXPERT_ROWS = 256
COMBINE_TOKENS = 256
N_ROWS = N_TOK * TOP_K + N_EXPERTS * EXPERT_ROWS
N_ROW_BLOCKS = N_ROWS // EXPERT_ROWS
META_LANES = 384
IN_COLS = 2 * GM_WIDTH + Q_LORA_RANK + KV_LORA_RANK + 2 * QK_ROPE_DIM
O_Q = 2 * GM_WIDTH
O_KV = O_Q + Q_LORA_RANK
O_KR = O_KV + KV_LORA_RANK
ROUTER_ROWS = 40
VMEM_LIMIT = 48 * 1024 * 1024

assert N_ROW_BLOCKS <= META_LANES


def _layer_norm(x, g, b, eps=1e-5):
    mu = jnp.mean(x, axis=-1, keepdims=True)
    xc = x - mu
    var = jnp.mean(xc * xc, axis=-1, keepdims=True)
    return xc * lax.rsqrt(var + eps) * g + b


def _rms_norm(x, g, eps=1e-6):
    return x * lax.rsqrt(jnp.mean(x * x, axis=-1, keepdims=True) + eps) * g


def _gelu_tanh(x):
    c = math.sqrt(2.0 / math.pi)
    return 0.5 * x * (1.0 + jnp.tanh(c * (x + 0.044715 * (x * x * x))))


def _to_row_tiles(ref, x):
    n = x.shape[0]
    for s in range(FEAT_TILES):
        ref[pl.ds(s, n, stride=FEAT_TILES), :] = x[:, s * LANES:(s + 1) * LANES]


def _from_row_tiles(ref, n):
    return jnp.concatenate(
        [ref[pl.ds(s, n, stride=FEAT_TILES), :] for s in range(FEAT_TILES)], axis=-1)


def _prep_kernel(x_ref, pos_ref, ln0g_ref, ln0b_ref, win_ref, gmg_ref, gmb_ref, ws_ref, bs_ref,
                 qg_ref, wuq_ref, kvg_ref, wukv_ref, freq_ref, phase_ref, sign_ref,
                 h_ref, outa_ref, q_ref, k_ref, v_ref):
    tb = x_ref.shape[0]
    h = _layer_norm(x_ref[...], ln0g_ref[...], ln0b_ref[...])
    h_ref[...] = h
    z = jnp.dot(h.astype(jnp.bfloat16), win_ref[...], preferred_element_type=jnp.float32)

    ang = pos_ref[...].astype(jnp.float32) * freq_ref[...]
    rot = jnp.cos(ang - phase_ref[...]) * sign_ref[...]

    u = _gelu_tanh(z[:, :GM_WIDTH])
    v = _gelu_tanh(z[:, GM_WIDTH:2 * GM_WIDTH])
    row_chunk = lax.broadcasted_iota(jnp.int32, (GM_CHUNK, GM_CHUNK), 0) // CHUNK
    col_chunk = lax.broadcasted_iota(jnp.int32, (GM_CHUNK, GM_CHUNK), 1) // CHUNK
    allowed = col_chunk <= row_chunk
    for hd in range(GM_HEADS):
        lo, hi = hd * GM_HEAD_DIM, (hd + 1) * GM_HEAD_DIM
        vln = _layer_norm(v[:, lo:hi], gmg_ref[:, lo:hi], gmb_ref[:, lo:hi]).astype(jnp.bfloat16)
        wm = jnp.where(allowed, ws_ref[hd], 0.0).astype(jnp.bfloat16)
        for c in range(tb // GM_CHUNK):
            r0, r1 = c * GM_CHUNK, (c + 1) * GM_CHUNK
            f = jnp.dot(wm, vln[r0:r1], preferred_element_type=jnp.float32) + bs_ref[hd]
            outa_ref[r0:r1, lo:hi] = (u[r0:r1, lo:hi] * f).astype(jnp.bfloat16)

    ql = _rms_norm(z[:, O_Q:O_KV], qg_ref[...]).astype(jnp.bfloat16)
    qf = jnp.dot(ql, wuq_ref[...], preferred_element_type=jnp.float32)
    rot_s = rot * QK_SCALE
    q_parts = []
    for hd in range(MLA_HEADS):
        base = hd * 2 * LANES
        q_parts.append(qf[:, base:base + LANES] * QK_SCALE)
        q_parts.append(qf[:, base + LANES:base + 2 * LANES] * rot_s)
    q_ref[...] = jnp.concatenate(q_parts, axis=-1).astype(jnp.bfloat16)

    kvl = _rms_norm(z[:, O_KV:O_KR], kvg_ref[...]).astype(jnp.bfloat16)
    kv = jnp.dot(kvl, wukv_ref[...], preferred_element_type=jnp.float32)
    t = z[:, O_KR:O_KR + LANES] * rot
    krr = t + pltpu.roll(t, 2 * QK_ROPE_DIM // 2, axis=1)
    k_parts, v_parts = [], []
    for hd in range(MLA_HEADS):
        base = hd * 2 * LANES
        k_parts.append(kv[:, base:base + LANES])
        k_parts.append(krr)
        v_parts.append(kv[:, base + LANES:base + 2 * LANES])
    k_ref[...] = jnp.concatenate(k_parts, axis=-1).astype(jnp.bfloat16)
    v_ref[...] = jnp.concatenate(v_parts, axis=-1).astype(jnp.bfloat16)


def _prep(x2, pos2, ln0g, ln0b, win, gmg, gmb, ws, bs, qg, wuq, kvg, wukv, freq, phase, sign):
    tb = PREP_TOKENS
    full = lambda shape: pl.BlockSpec(shape, lambda i: (0,) * len(shape))
    tok = lambda cols: pl.BlockSpec((tb, cols), lambda i: (i, 0))
    return pl.pallas_call(
        _prep_kernel,
        grid=(N_TOK // tb,),
        in_specs=[tok(D_MODEL), tok(1), full((1, D_MODEL)), full((1, D_MODEL)),
                  full((D_MODEL, IN_COLS)), full((1, GM_WIDTH)), full((1, GM_WIDTH)),
                  full((GM_HEADS, GM_CHUNK, GM_CHUNK)), full((GM_HEADS, GM_CHUNK, GM_HEAD_DIM)),
                  full((1, Q_LORA_RANK)), full((Q_LORA_RANK, D_MODEL)),
                  full((1, KV_LORA_RANK)), full((KV_LORA_RANK, D_MODEL)),
                  full((1, LANES)), full((1, LANES)), full((1, LANES))],
        out_specs=[tok(D_MODEL), tok(GM_WIDTH), tok(D_MODEL), tok(D_MODEL), tok(GM_WIDTH)],
        out_shape=[jax.ShapeDtypeStruct((N_TOK, D_MODEL), jnp.float32),
                   jax.ShapeDtypeStruct((N_TOK, GM_WIDTH), jnp.bfloat16),
                   jax.ShapeDtypeStruct((N_TOK, D_MODEL), jnp.bfloat16),
                   jax.ShapeDtypeStruct((N_TOK, D_MODEL), jnp.bfloat16),
                   jax.ShapeDtypeStruct((N_TOK, GM_WIDTH), jnp.bfloat16)],
        compiler_params=pltpu.CompilerParams(
            dimension_semantics=("arbitrary",), vmem_limit_bytes=VMEM_LIMIT),
        name="prep",
    )(x2, pos2, ln0g, ln0b, win, gmg, gmb, ws, bs, qg, wuq, kvg, wukv, freq, phase, sign)


def _attn_kernel(q_ref, k_ref, v_ref, outa_ref, h_ref, wout_ref, ln1g_ref, ln1b_ref,
                 wr_ref, br_ref, h1r_ref, ri_ref, rf_ref, s_ref, mx_ref, ls_ref, acc_ref):
    qi = pl.program_id(1)
    tq = q_ref.shape[0]
    nt = (((1,), (1,)), ((), ()))
    n_kv = SEQ // ATT_K

    row_chunk = (lax.broadcasted_iota(jnp.int32, (tq, ATT_K), 0)) // CHUNK
    col_chunk = (lax.broadcasted_iota(jnp.int32, (tq, ATT_K), 1)) // CHUNK
    diag_allowed = col_chunk <= row_chunk

    def lane_fold(x, op):
        return op(x[:, :LANES], x[:, LANES:])

    def scores(hd, j):
        start = pl.multiple_of(j * ATT_K, ATT_K)
        q_h = q_ref[:, hd * 2 * LANES:(hd + 1) * 2 * LANES]
        kb = k_ref[pl.ds(start, ATT_K), hd * 2 * LANES:(hd + 1) * 2 * LANES]
        return lax.dot_general(q_h, kb, nt, preferred_element_type=jnp.float32)

    for hd in range(MLA_HEADS):
        s = jnp.where(diag_allowed, scores(hd, qi), jnp.float32(-1e30))
        s_ref[hd * n_kv + qi] = s
        mx_ref[hd] = lane_fold(s, jnp.maximum)

    def pass_a(j, c):
        for hd in range(MLA_HEADS):
            s = scores(hd, j)
            s_ref[hd * n_kv + j] = s
            mx_ref[hd] = jnp.maximum(mx_ref[hd], lane_fold(s, jnp.maximum))
        return c

    lax.fori_loop(0, qi, pass_a, 0)

    for hd in range(MLA_HEADS):
        mx_ref[hd] = jnp.broadcast_to(jnp.max(mx_ref[hd], axis=-1, keepdims=True), (tq, LANES))

    def probs(hd, j):
        s = s_ref[hd * n_kv + j]
        mb = mx_ref[hd]
        p = jnp.exp2(jnp.concatenate([s[:, :LANES] - mb, s[:, LANES:] - mb], axis=-1))
        start = pl.multiple_of(j * ATT_K, ATT_K)
        vb = v_ref[pl.ds(start, ATT_K), hd * LANES:(hd + 1) * LANES]
        pv = jnp.dot(p.astype(jnp.bfloat16), vb, preferred_element_type=jnp.float32)
        return lane_fold(p, jnp.add), pv

    for hd in range(MLA_HEADS):
        ls, pv = probs(hd, qi)
        ls_ref[hd] = ls
        acc_ref[hd] = pv

    def pass_b(j, c):
        for hd in range(MLA_HEADS):
            ls, pv = probs(hd, j)
            ls_ref[hd] = ls_ref[hd] + ls
            acc_ref[hd] = acc_ref[hd] + pv
        return c

    lax.fori_loop(0, qi, pass_b, 0)

    heads = []
    for hd in range(MLA_HEADS):
        l = jnp.sum(ls_ref[hd], axis=-1, keepdims=True)
        heads.append((acc_ref[hd] / l).astype(jnp.bfloat16))

    mix_b = jnp.concatenate(heads, axis=-1)
    proj = (jnp.dot(outa_ref[...], wout_ref[:GM_WIDTH, :], preferred_element_type=jnp.float32)
            + jnp.dot(mix_b, wout_ref[GM_WIDTH:, :], preferred_element_type=jnp.float32))
    h1 = _layer_norm(ALPHA * h_ref[...] + proj, ln1g_ref[...], ln1b_ref[...])
    _to_row_tiles(h1r_ref, h1)

    h_hi = h1.astype(jnp.bfloat16)
    h_lo = (h1 - h_hi.astype(jnp.float32)).astype(jnp.bfloat16)
    w = wr_ref[...]
    w_hi = w.astype(jnp.bfloat16)
    w_lo = (w - w_hi.astype(jnp.float32)).astype(jnp.bfloat16)
    logits_tm = (jnp.dot(h_hi, w_hi, preferred_element_type=jnp.float32)
                 + jnp.dot(h_hi, w_lo, preferred_element_type=jnp.float32)
                 + jnp.dot(h_lo, w_hi, preferred_element_type=jnp.float32))
    logits = logits_tm.T[0:ROUTER_ROWS] + br_ref[...]

    sub_i = lax.broadcasted_iota(jnp.int32, (SUBLANES, tq), 0)
    sub = sub_i.astype(jnp.float32)
    neg = jnp.float32(-jnp.inf)
    g = jnp.where(sub_i < N_GROUPS, logits[0:SUBLANES], neg)
    gmax = jnp.max(g, axis=0, keepdims=True)
    g_top = jnp.min(jnp.where(g == gmax, sub, float(SUBLANES)), axis=0, keepdims=True)
    p_group = 1.0 / jnp.sum(jnp.exp(g - gmax), axis=0, keepdims=True)
    sel = logits[SUBLANES:2 * SUBLANES]
    for grp in range(1, N_GROUPS):
        sel = jnp.where(g_top == float(grp), logits[(grp + 1) * SUBLANES:(grp + 2) * SUBLANES], sel)
    v1 = jnp.max(sel, axis=0, keepdims=True)
    i1 = jnp.min(jnp.where(sel == v1, sub, float(SUBLANES)), axis=0, keepdims=True)
    sel2 = jnp.where(sub == i1, neg, sel)
    v2 = jnp.max(sel2, axis=0, keepdims=True)
    i2 = jnp.min(jnp.where(sel2 == v2, sub, float(SUBLANES)), axis=0, keepdims=True)
    e21 = jnp.exp(v2 - v1)
    w1 = 1.0 / (1.0 + e21)
    gate1 = p_group * w1
    gate2 = p_group * (e21 * w1)
    e1 = g_top * EXPERTS_PER_GROUP + i1
    e2 = g_top * EXPERTS_PER_GROUP + i2
    ri_ref[...] = jnp.where(sub_i == 0, e1, jnp.where(sub_i == 1, e2, 0.0)).astype(jnp.int32)
    rf_ref[...] = jnp.where(sub_i == 0, gate1, jnp.where(sub_i == 1, gate2, 0.0))


def _attn(q, k, v, outa, h, wout, ln1g, ln1b, wr, br):
    tq = ATT_Q
    tokblk = lambda cols: pl.BlockSpec((tq, cols), lambda b, i: (b * N_QBLK + i, 0))
    seqblk = lambda cols: pl.BlockSpec((SEQ, cols), lambda b, i: (b, 0))
    full = lambda shape: pl.BlockSpec(shape, lambda b, i: (0,) * len(shape))
    return pl.pallas_call(
        _attn_kernel,
        grid=(BATCH, N_QBLK),
        in_specs=[tokblk(D_MODEL), seqblk(D_MODEL), seqblk(GM_WIDTH), tokblk(GM_WIDTH),
                  tokblk(D_MODEL), full((D_MODEL, D_MODEL)), full((1, D_MODEL)), full((1, D_MODEL)),
                  full((D_MODEL, LANES)), full((ROUTER_ROWS, 1))],
        out_specs=[pl.BlockSpec((tq * FEAT_TILES, LANES), lambda b, i: (b * N_QBLK + i, 0)),
                   pl.BlockSpec((SUBLANES, tq), lambda b, i: (0, b * N_QBLK + i)),
                   pl.BlockSpec((SUBLANES, tq), lambda b, i: (0, b * N_QBLK + i))],
        out_shape=[jax.ShapeDtypeStruct((N_TOK * FEAT_TILES, LANES), jnp.float32),
                   jax.ShapeDtypeStruct((SUBLANES, N_TOK), jnp.int32),
                   jax.ShapeDtypeStruct((SUBLANES, N_TOK), jnp.float32)],
        scratch_shapes=[pltpu.VMEM((MLA_HEADS * (SEQ // ATT_K), tq, ATT_K), jnp.float32),
                        pltpu.VMEM((MLA_HEADS, tq, LANES), jnp.float32),
                        pltpu.VMEM((MLA_HEADS, tq, LANES), jnp.float32),
                        pltpu.VMEM((MLA_HEADS, tq, V_HEAD_DIM), jnp.float32)],
        compiler_params=pltpu.CompilerParams(
            dimension_semantics=("arbitrary", "arbitrary"), vmem_limit_bytes=VMEM_LIMIT),
        name="attn",
    )(q, k, v, outa, h, wout, ln1g, ln1b, wr, br)


def _plan_kernel(ri_ref, dest_ref, meta_ref, cnt_ref, run_ref, start_ref):
    phase = pl.program_id(0)
    blk = pl.program_id(1)
    tb = ri_ref.shape[1]
    f32 = jnp.float32

    e_sub = lax.broadcasted_iota(jnp.int32, (N_EXPERTS, tb), 0)
    oh1 = e_sub == ri_ref[0:1, :]
    oh2 = e_sub == ri_ref[1:2, :]
    oh = jnp.where(oh1 | oh2, 1.0, 0.0).astype(f32)
    blk_count = jnp.sum(oh, axis=1, keepdims=True)

    @pl.when((phase == 0) & (blk == 0))
    def _():
        cnt_ref[...] = jnp.zeros_like(cnt_ref)

    @pl.when(phase == 0)
    def _():
        cnt_ref[...] = cnt_ref[...] + blk_count

    @pl.when((phase == 1) & (blk == 0))
    def _():
        counts = cnt_ref[:, 0:1]
        padded = jnp.floor((counts + (EXPERT_ROWS - 1)) * (1.0 / EXPERT_ROWS)) * EXPERT_ROWS
        er = lax.broadcasted_iota(jnp.int32, (N_EXPERTS, LANES), 0)
        ec = lax.broadcasted_iota(jnp.int32, (N_EXPERTS, LANES), 1)
        padded_row = jnp.sum(jnp.where(er == ec, padded, 0.0), axis=0, keepdims=True)
        counts_row = jnp.sum(jnp.where(er == ec, counts, 0.0), axis=0, keepdims=True)
        pad_end = jnp.sum(jnp.where(ec <= er, padded_row, 0.0), axis=1, keepdims=True)
        pad_end_row = jnp.sum(jnp.where(er == ec, pad_end, 0.0), axis=0, keepdims=True)
        start_ref[...] = jnp.broadcast_to(pad_end - padded, start_ref.shape)
        run_ref[...] = jnp.zeros_like(run_ref)
        bstart = (lax.broadcasted_iota(jnp.int32, (N_EXPERTS, META_LANES), 1) * EXPERT_ROWS).astype(f32)
        blk_e = jnp.sum(jnp.where(pad_end <= bstart, 1.0, 0.0), axis=0, keepdims=True)
        blk_e = jnp.minimum(blk_e, N_EXPERTS - 1.0)
        n_used = pad_end[N_EXPERTS - 1:N_EXPERTS, :] * (1.0 / EXPERT_ROWS)
        pad3 = lambda r: jnp.concatenate(
            [r, jnp.zeros((1, META_LANES - LANES), f32)], axis=1)
        msub = lax.broadcasted_iota(jnp.int32, (SUBLANES, META_LANES), 0)
        meta = jnp.where(msub == 0, blk_e,
                         jnp.where(msub == 1, pad3(pad_end_row),
                                   jnp.where(msub == 2, pad3(counts_row),
                                             jnp.where(msub == 3, n_used, 0.0))))
        meta_ref[...] = meta.astype(jnp.int32)

    @pl.when(phase == 1)
    def _():
        tr = lax.broadcasted_iota(jnp.int32, (tb, tb), 0)
        tc = lax.broadcasted_iota(jnp.int32, (tb, tb), 1)
        upper = jnp.where(tr < tc, 1.0, 0.0).astype(jnp.bfloat16)
        prefix = jnp.dot(oh.astype(jnp.bfloat16), upper, preferred_element_type=f32)
        base = prefix + run_ref[:, 0:1] + start_ref[:, 0:1]
        d1 = jnp.sum(jnp.where(oh1, base, 0.0), axis=0, keepdims=True)
        d2 = jnp.sum(jnp.where(oh2, base, 0.0), axis=0, keepdims=True)
        sub = lax.broadcasted_iota(jnp.int32, (SUBLANES, tb), 0)
        dest_ref[...] = jnp.where(sub == 0, d1, jnp.where(sub == 1, d2, 0.0)).astype(jnp.int32)
        run_ref[...] = run_ref[...] + blk_count


def _plan(ri):
    tb = PLAN_TOKENS
    return pl.pallas_call(
        _plan_kernel,
        grid=(2, N_TOK // tb),
        in_specs=[pl.BlockSpec((SUBLANES, tb), lambda p, i: (0, i))],
        out_specs=[pl.BlockSpec((SUBLANES, tb), lambda p, i: (0, i * p)),
                   pl.BlockSpec((SUBLANES, META_LANES), lambda p, i: (0, 0))],
        out_shape=[jax.ShapeDtypeStruct((SUBLANES, N_TOK), jnp.int32),
                   jax.ShapeDtypeStruct((SUBLANES, META_LANES), jnp.int32)],
        scratch_shapes=[pltpu.VMEM((N_EXPERTS, LANES), jnp.float32),
                        pltpu.VMEM((N_EXPERTS, LANES), jnp.float32),
                        pltpu.VMEM((N_EXPERTS, LANES), jnp.float32)],
        compiler_params=pltpu.CompilerParams(dimension_semantics=("arbitrary", "arbitrary")),
        name="plan",
    )(ri)


def _row_copy(src, src_row, dst, dst_row, sem):
    return pltpu.make_async_copy(
        src.at[pl.ds(pl.multiple_of(src_row * FEAT_TILES, FEAT_TILES), FEAT_TILES)],
        dst.at[pl.ds(pl.multiple_of(dst_row * FEAT_TILES, FEAT_TILES), FEAT_TILES)], sem)


def _dispatch_kernel(meta_ref, dest_ref, h1r_ref, buf_ref, zero_ref, sem0, sem1, zsem):
    tb = dest_ref.shape[1]

    @pl.when(pl.program_id(0) == 0)
    def _():
        zero_ref[...] = jnp.zeros_like(zero_ref)

        def zero_copy(e):
            start = pl.multiple_of((meta_ref[1, e] - EXPERT_ROWS) * FEAT_TILES, EXPERT_ROWS * FEAT_TILES)
            return pltpu.make_async_copy(
                zero_ref, buf_ref.at[pl.ds(start, EXPERT_ROWS * FEAT_TILES)], zsem)

        def start_zero(e, c):
            @pl.when(meta_ref[2, e] > 0)
            def _():
                zero_copy(e).start()
            return c

        def wait_zero(e, c):
            @pl.when(meta_ref[2, e] > 0)
            def _():
                zero_copy(e).wait()
            return c

        def tail_copy(b):
            start = pl.multiple_of(b * (EXPERT_ROWS * FEAT_TILES), EXPERT_ROWS * FEAT_TILES)
            return pltpu.make_async_copy(
                zero_ref, buf_ref.at[pl.ds(start, EXPERT_ROWS * FEAT_TILES)], zsem)

        def start_tail(b, c):
            tail_copy(b).start()
            return c

        def wait_tail(b, c):
            tail_copy(b).wait()
            return c

        lax.fori_loop(0, N_EXPERTS, start_zero, 0)
        lax.fori_loop(meta_ref[3, 0], N_ROW_BLOCKS, start_tail, 0)
        lax.fori_loop(0, N_EXPERTS, wait_zero, 0)
        lax.fori_loop(meta_ref[3, 0], N_ROW_BLOCKS, wait_tail, 0)

    def issue(t, c):
        _row_copy(h1r_ref, t, buf_ref, dest_ref[0, t], sem0).start()
        _row_copy(h1r_ref, t, buf_ref, dest_ref[1, t], sem1).start()
        return c

    def drain(t, c):
        _row_copy(h1r_ref, t, buf_ref, dest_ref[0, t], sem0).wait()
        _row_copy(h1r_ref, t, buf_ref, dest_ref[1, t], sem1).wait()
        return c

    lax.fori_loop(0, tb, issue, 0)
    lax.fori_loop(0, tb, drain, 0)


def _dispatch(meta, dest, h1r):
    tb = DISPATCH_TOKENS
    return pl.pallas_call(
        _dispatch_kernel,
        grid_spec=pltpu.PrefetchScalarGridSpec(
            num_scalar_prefetch=1,
            grid=(N_TOK // tb,),
            in_specs=[pl.BlockSpec((SUBLANES, tb), lambda i, m: (0, i), memory_space=pltpu.SMEM),
                      pl.BlockSpec((tb * FEAT_TILES, LANES), lambda i, m: (i, 0))],
            out_specs=pl.BlockSpec(memory_space=pl.ANY),
            scratch_shapes=[pltpu.VMEM((EXPERT_ROWS * FEAT_TILES, LANES), jnp.float32),
                            pltpu.SemaphoreType.DMA, pltpu.SemaphoreType.DMA,
                            pltpu.SemaphoreType.DMA]),
        out_shape=jax.ShapeDtypeStruct((N_ROWS * FEAT_TILES, LANES), jnp.float32),
        compiler_params=pltpu.CompilerParams(dimension_semantics=("arbitrary",)),
        name="dispatch",
    )(meta, dest, h1r)


def _experts_kernel(meta_ref, x_ref, wg_ref, wu_ref, wd_ref, o_ref):
    n = EXPERT_ROWS

    @pl.when(pl.program_id(0) < meta_ref[3, 0])
    def _():
        x = _from_row_tiles(x_ref, n).astype(jnp.bfloat16)
        gate = jnp.dot(x, wg_ref[...], preferred_element_type=jnp.float32)
        up = jnp.dot(x, wu_ref[...], preferred_element_type=jnp.float32)
        act = (gate * jax.nn.sigmoid(gate) * up).astype(jnp.bfloat16)
        _to_row_tiles(o_ref, jnp.dot(act, wd_ref[...], preferred_element_type=jnp.float32))

    @pl.when(pl.program_id(0) >= meta_ref[3, 0])
    def _():
        o_ref[...] = jnp.zeros_like(o_ref)


def _experts(meta, buf, wg, wu, wd):
    used = lambda i, m: jnp.minimum(i, m[3, 0] - 1)
    return pl.pallas_call(
        _experts_kernel,
        grid_spec=pltpu.PrefetchScalarGridSpec(
            num_scalar_prefetch=1,
            grid=(N_ROW_BLOCKS,),
            in_specs=[pl.BlockSpec((EXPERT_ROWS * FEAT_TILES, LANES), lambda i, m: (used(i, m), 0)),
                      pl.BlockSpec((None, D_MODEL, EXPERT_FF), lambda i, m: (m[0, used(i, m)], 0, 0)),
                      pl.BlockSpec((None, D_MODEL, EXPERT_FF), lambda i, m: (m[0, used(i, m)], 0, 0)),
                      pl.BlockSpec((None, EXPERT_FF, D_MODEL), lambda i, m: (m[0, used(i, m)], 0, 0))],
            out_specs=pl.BlockSpec((EXPERT_ROWS * FEAT_TILES, LANES), lambda i, m: (i, 0))),
        out_shape=jax.ShapeDtypeStruct((N_ROWS * FEAT_TILES, LANES), jnp.float32),
        compiler_params=pltpu.CompilerParams(
            dimension_semantics=("arbitrary",), vmem_limit_bytes=VMEM_LIMIT),
        name="experts",
    )(meta, buf, wg, wu, wd)


def _combine_kernel(dest_ref, rf_ref, h1r_ref, eout_ref, ln2g_ref, ln2b_ref, o_ref,
                    y0_ref, y1_ref, sem0, sem1):
    tb = dest_ref.shape[1]

    def issue(t, c):
        _row_copy(eout_ref, dest_ref[0, t], y0_ref, t, sem0).start()
        _row_copy(eout_ref, dest_ref[1, t], y1_ref, t, sem1).start()
        return c

    def drain(t, c):
        _row_copy(eout_ref, dest_ref[0, t], y0_ref, t, sem0).wait()
        _row_copy(eout_ref, dest_ref[1, t], y1_ref, t, sem1).wait()
        return c

    lax.fori_loop(0, tb, issue, 0)
    lax.fori_loop(0, tb, drain, 0)

    gates = rf_ref[...].T
    y = (gates[:, 0:1] * _from_row_tiles(y0_ref, tb)
         + gates[:, 1:2] * _from_row_tiles(y1_ref, tb))
    h1 = _from_row_tiles(h1r_ref, tb)
    o_ref[...] = _layer_norm(ALPHA * h1 + y, ln2g_ref[...], ln2b_ref[...])


def _combine(dest, rf, h1r, eout, ln2g, ln2b):
    tb = COMBINE_TOKENS
    return pl.pallas_call(
        _combine_kernel,
        grid=(N_TOK // tb,),
        in_specs=[pl.BlockSpec((SUBLANES, tb), lambda i: (0, i), memory_space=pltpu.SMEM),
                  pl.BlockSpec((SUBLANES, tb), lambda i: (0, i)),
                  pl.BlockSpec((tb * FEAT_TILES, LANES), lambda i: (i, 0)),
                  pl.BlockSpec(memory_space=pl.ANY),
                  pl.BlockSpec((1, D_MODEL), lambda i: (0, 0)),
                  pl.BlockSpec((1, D_MODEL), lambda i: (0, 0))],
        out_specs=pl.BlockSpec((tb, D_MODEL), lambda i: (i, 0)),
        out_shape=jax.ShapeDtypeStruct((N_TOK, D_MODEL), jnp.float32),
        scratch_shapes=[pltpu.VMEM((tb * FEAT_TILES, LANES), jnp.float32),
                        pltpu.VMEM((tb * FEAT_TILES, LANES), jnp.float32),
                        pltpu.SemaphoreType.DMA, pltpu.SemaphoreType.DMA],
        compiler_params=pltpu.CompilerParams(
            dimension_semantics=("arbitrary",), vmem_limit_bytes=VMEM_LIMIT),
        name="combine",
    )(dest, rf, h1r, eout, ln2g, ln2b)


def _swap_halves(w):
    half = w.shape[-1] // 2
    return jnp.concatenate([w[..., half:], w[..., :half]], axis=-1)


def kernel(x, positions, ln0_g, ln0_b, w_in, gm_ln_g, gm_ln_b, w_spatial, b_spatial, q_norm_g, w_uq, kv_norm_g, w_ukv, w_out, ln1_g, ln1_b, w_router_group, b_router_group, w_router_expert, b_router_expert, w_gate, w_up, w_down, ln2_g, ln2_b):
    bf16 = jnp.bfloat16
    row = lambda a: a.reshape(1, -1)

    w_in0 = w_in[0]
    kr_cols = w_in0[:, O_KR:O_KR + QK_ROPE_DIM]
    win = jnp.concatenate([w_in0, _swap_halves(kr_cols)], axis=1).astype(bf16)
    wuq3 = w_uq[0].reshape(Q_LORA_RANK, MLA_HEADS, QK_NOPE_DIM + QK_ROPE_DIM)
    rope_cols = wuq3[:, :, QK_NOPE_DIM:]
    wuq = jnp.concatenate([wuq3, _swap_halves(rope_cols)], axis=-1).reshape(Q_LORA_RANK, D_MODEL).astype(bf16)
    wukv = w_ukv[0].astype(bf16)
    wout = w_out[0].astype(bf16)
    bs = jnp.broadcast_to(b_spatial[0][:, :, None], (GM_HEADS, GM_CHUNK, GM_HEAD_DIM))
    wr = jnp.concatenate([w_router_group[0], jnp.zeros((D_MODEL, SUBLANES - N_GROUPS), jnp.float32),
                          w_router_expert[0],
                          jnp.zeros((D_MODEL, LANES - ROUTER_ROWS), jnp.float32)], axis=1)
    br = jnp.concatenate([b_router_group[0], jnp.zeros((SUBLANES - N_GROUPS,), jnp.float32),
                          b_router_expert[0]]).reshape(ROUTER_ROWS, 1)
    wg = w_gate[0].astype(bf16)
    wu = w_up[0].astype(bf16)
    wd = w_down[0].astype(bf16)

    inv_freq = ROPE_THETA ** (-jnp.arange(0, QK_ROPE_DIM, 2, dtype=jnp.float32) / QK_ROPE_DIM)
    freq = jnp.tile(inv_freq, 4).reshape(1, LANES)
    quarter = QK_ROPE_DIM // 2
    phase = jnp.concatenate([jnp.zeros((2 * quarter,), jnp.float32),
                             jnp.full((2 * quarter,), math.pi / 2, jnp.float32)]).reshape(1, LANES)
    sign = jnp.concatenate([jnp.ones((2 * quarter,), jnp.float32), -jnp.ones((quarter,), jnp.float32),
                            jnp.ones((quarter,), jnp.float32)]).reshape(1, LANES)

    x2 = x.reshape(N_TOK, D_MODEL)
    pos2 = positions.reshape(N_TOK, 1)

    h, outa, q, k, v = _prep(x2, pos2, row(ln0_g), row(ln0_b), win, row(gm_ln_g[0]), row(gm_ln_b[0]),
                             w_spatial[0], bs, row(q_norm_g[0]), wuq, row(kv_norm_g[0]), wukv,
                             freq, phase, sign)
    h1r, ri, rf = _attn(q, k, v, outa, h, wout, row(ln1_g[0]), row(ln1_b[0]), wr, br)
    dest, meta = _plan(ri)
    buf = _dispatch(meta, dest, h1r)
    eout = _experts(meta, buf, wg, wu, wd)
    out = _combine(dest, rf, h1r, eout, row(ln2_g[0]), row(ln2_b[0]))
    return out.reshape(BATCH, SEQ, D_MODEL)
```

```python
import functools
import math

import jax
import jax.numpy as jnp
from jax import lax
from jax.experimental import pallas as pl
from jax.experimental.pallas import tpu as pltpu

D_MODEL = 1024
BATCH = 16
SEQ = 2048
N_TOK = BATCH * SEQ
CHUNK = 64
GM_WIDTH = 512
GM_HEADS = 4
GM_HEAD_DIM = 128
GM_CHUNK = 128
MLA_HEADS = 4
QK_NOPE_DIM = 128
QK_ROPE_DIM = 64
V_HEAD_DIM = 128
Q_LORA_RANK = 384
KV_LORA_RANK = 256
ROPE_THETA = 10000.0
N_GROUPS = 4
EXPERTS_PER_GROUP = 8
N_EXPERTS = 32
TOP_K = 2
EXPERT_FF = 256
ALPHA = 2.0 ** 0.25
QK_SCALE = (QK_NOPE_DIM + QK_ROPE_DIM) ** -0.5 * math.log2(math.e)

LANES = 128
SUBLANES = 8
FEAT_TILES = D_MODEL // LANES

PREP_TOKENS = 512
ATT_Q = 256
ATT_K = 256
N_QBLK = SEQ // ATT_Q
MOE_TOKENS = 512
N_MOE_BLOCKS = N_TOK // MOE_TOKENS
RUN_BITS = (TOP_K * MOE_TOKENS).bit_length()
SORT_CHUNK = 256
EXPERT_ROWS = 256
N_ROWS = N_TOK * TOP_K + N_EXPERTS * EXPERT_ROWS
N_ROW_BLOCKS = N_ROWS // EXPERT_ROWS
META_LANES = 384
IN_COLS = 2 * GM_WIDTH + Q_LORA_RANK + KV_LORA_RANK + 2 * QK_ROPE_DIM
O_Q = 2 * GM_WIDTH
O_KV = O_Q + Q_LORA_RANK
O_KR = O_KV + KV_LORA_RANK
ROUTER_ROWS = 40
VMEM_LIMIT = 48 * 1024 * 1024

assert N_ROW_BLOCKS <= META_LANES


def _layer_norm(x, g, b, eps=1e-5):
    mu = jnp.mean(x, axis=-1, keepdims=True)
    xc = x - mu
    var = jnp.mean(xc * xc, axis=-1, keepdims=True)
    return xc * lax.rsqrt(var + eps) * g + b


def _rms_norm(x, g, eps=1e-6):
    return x * lax.rsqrt(jnp.mean(x * x, axis=-1, keepdims=True) + eps) * g


def _gelu_tanh(x):
    c = math.sqrt(2.0 / math.pi)
    return 0.5 * x * (1.0 + jnp.tanh(c * (x + 0.044715 * (x * x * x))))


def _to_row_tiles(ref, x):
    n = x.shape[0]
    for s in range(FEAT_TILES):
        ref[pl.ds(s, n, stride=FEAT_TILES), :] = x[:, s * LANES:(s + 1) * LANES]


def _from_row_tiles(ref, n):
    return jnp.concatenate(
        [ref[pl.ds(s, n, stride=FEAT_TILES), :] for s in range(FEAT_TILES)], axis=-1)


def _prep_kernel(x_ref, pos_ref, ln0g_ref, ln0b_ref, win_ref, gmg_ref, gmb_ref, ws_ref, bs_ref,
                 qg_ref, wuq_ref, kvg_ref, wukv_ref, freq_ref, phase_ref, sign_ref,
                 h_ref, outa_ref, q_ref, k_ref, v_ref):
    tb = x_ref.shape[0]
    h = _layer_norm(x_ref[...], ln0g_ref[...], ln0b_ref[...])
    h_ref[...] = h
    z = jnp.dot(h.astype(jnp.bfloat16), win_ref[...], preferred_element_type=jnp.float32)

    ang = pos_ref[...].astype(jnp.float32) * freq_ref[...]
    rot = jnp.cos(ang - phase_ref[...]) * sign_ref[...]

    u = _gelu_tanh(z[:, :GM_WIDTH])
    v = _gelu_tanh(z[:, GM_WIDTH:2 * GM_WIDTH])
    row_chunk = lax.broadcasted_iota(jnp.int32, (GM_CHUNK, GM_CHUNK), 0) // CHUNK
    col_chunk = lax.broadcasted_iota(jnp.int32, (GM_CHUNK, GM_CHUNK), 1) // CHUNK
    allowed = col_chunk <= row_chunk
    for hd in range(GM_HEADS):
        lo, hi = hd * GM_HEAD_DIM, (hd + 1) * GM_HEAD_DIM
        vln = _layer_norm(v[:, lo:hi], gmg_ref[:, lo:hi], gmb_ref[:, lo:hi]).astype(jnp.bfloat16)
        wm = jnp.where(allowed, ws_ref[hd], 0.0).astype(jnp.bfloat16)
        for c in range(tb // GM_CHUNK):
            r0, r1 = c * GM_CHUNK, (c + 1) * GM_CHUNK
            f = jnp.dot(wm, vln[r0:r1], preferred_element_type=jnp.float32) + bs_ref[hd]
            outa_ref[r0:r1, lo:hi] = (u[r0:r1, lo:hi] * f).astype(jnp.bfloat16)

    ql = _rms_norm(z[:, O_Q:O_KV], qg_ref[...]).astype(jnp.bfloat16)
    qf = jnp.dot(ql, wuq_ref[...], preferred_element_type=jnp.float32)
    rot_s = rot * QK_SCALE
    q_parts = []
    for hd in range(MLA_HEADS):
        base = hd * 2 * LANES
        q_parts.append(qf[:, base:base + LANES] * QK_SCALE)
        q_parts.append(qf[:, base + LANES:base + 2 * LANES] * rot_s)
    q_ref[...] = jnp.concatenate(q_parts, axis=-1).astype(jnp.bfloat16)

    kvl = _rms_norm(z[:, O_KV:O_KR], kvg_ref[...]).astype(jnp.bfloat16)
    kv = jnp.dot(kvl, wukv_ref[...], preferred_element_type=jnp.float32)
    t = z[:, O_KR:O_KR + LANES] * rot
    krr = t + pltpu.roll(t, 2 * QK_ROPE_DIM // 2, axis=1)
    k_parts, v_parts = [], []
    for hd in range(MLA_HEADS):
        base = hd * 2 * LANES
        k_parts.append(kv[:, base:base + LANES])
        k_parts.append(krr)
        v_parts.append(kv[:, base + LANES:base + 2 * LANES])
    k_ref[...] = jnp.concatenate(k_parts, axis=-1).astype(jnp.bfloat16)
    v_ref[...] = jnp.concatenate(v_parts, axis=-1).astype(jnp.bfloat16)


def _prep(x2, pos2, ln0g, ln0b, win, gmg, gmb, ws, bs, qg, wuq, kvg, wukv, freq, phase, sign):
    tb = PREP_TOKENS
    full = lambda shape: pl.BlockSpec(shape, lambda i: (0,) * len(shape))
    tok = lambda cols: pl.BlockSpec((tb, cols), lambda i: (i, 0))
    return pl.pallas_call(
        _prep_kernel,
        grid=(N_TOK // tb,),
        in_specs=[tok(D_MODEL), tok(1), full((1, D_MODEL)), full((1, D_MODEL)),
                  full((D_MODEL, IN_COLS)), full((1, GM_WIDTH)), full((1, GM_WIDTH)),
                  full((GM_HEADS, GM_CHUNK, GM_CHUNK)), full((GM_HEADS, GM_CHUNK, GM_HEAD_DIM)),
                  full((1, Q_LORA_RANK)), full((Q_LORA_RANK, D_MODEL)),
                  full((1, KV_LORA_RANK)), full((KV_LORA_RANK, D_MODEL)),
                  full((1, LANES)), full((1, LANES)), full((1, LANES))],
        out_specs=[tok(D_MODEL), tok(GM_WIDTH), tok(D_MODEL), tok(D_MODEL), tok(GM_WIDTH)],
        out_shape=[jax.ShapeDtypeStruct((N_TOK, D_MODEL), jnp.float32),
                   jax.ShapeDtypeStruct((N_TOK, GM_WIDTH), jnp.bfloat16),
                   jax.ShapeDtypeStruct((N_TOK, D_MODEL), jnp.bfloat16),
                   jax.ShapeDtypeStruct((N_TOK, D_MODEL), jnp.bfloat16),
                   jax.ShapeDtypeStruct((N_TOK, GM_WIDTH), jnp.bfloat16)],
        compiler_params=pltpu.CompilerParams(
            dimension_semantics=("arbitrary",), vmem_limit_bytes=VMEM_LIMIT),
        name="prep",
    )(x2, pos2, ln0g, ln0b, win, gmg, gmb, ws, bs, qg, wuq, kvg, wukv, freq, phase, sign)


def _attn_kernel(q_ref, k_ref, v_ref, outa_ref, h_ref, wout_ref, ln1g_ref, ln1b_ref,
                 wr_ref, br_ref, h1r_ref, ri_ref, rf_ref, s_ref, mx_ref, ls_ref, acc_ref):
    qi = pl.program_id(1)
    tq = q_ref.shape[0]
    nt = (((1,), (1,)), ((), ()))
    n_kv = SEQ // ATT_K

    row_chunk = (lax.broadcasted_iota(jnp.int32, (tq, ATT_K), 0)) // CHUNK
    col_chunk = (lax.broadcasted_iota(jnp.int32, (tq, ATT_K), 1)) // CHUNK
    diag_allowed = col_chunk <= row_chunk

    def lane_fold(x, op):
        return op(x[:, :LANES], x[:, LANES:])

    def scores(hd, j):
        start = pl.multiple_of(j * ATT_K, ATT_K)
        q_h = q_ref[:, hd * 2 * LANES:(hd + 1) * 2 * LANES]
        kb = k_ref[pl.ds(start, ATT_K), hd * 2 * LANES:(hd + 1) * 2 * LANES]
        return lax.dot_general(q_h, kb, nt, preferred_element_type=jnp.float32)

    for hd in range(MLA_HEADS):
        s = jnp.where(diag_allowed, scores(hd, qi), jnp.float32(-1e30))
        s_ref[hd * n_kv + qi] = s
        mx_ref[hd] = lane_fold(s, jnp.maximum)

    def pass_a(j, c):
        for hd in range(MLA_HEADS):
            s = scores(hd, j)
            s_ref[hd * n_kv + j] = s
            mx_ref[hd] = jnp.maximum(mx_ref[hd], lane_fold(s, jnp.maximum))
        return c

    lax.fori_loop(0, qi, pass_a, 0)

    for hd in range(MLA_HEADS):
        mx_ref[hd] = jnp.broadcast_to(jnp.max(mx_ref[hd], axis=-1, keepdims=True), (tq, LANES))

    def probs(hd, j):
        s = s_ref[hd * n_kv + j]
        mb = mx_ref[hd]
        p = jnp.exp2(jnp.concatenate([s[:, :LANES] - mb, s[:, LANES:] - mb], axis=-1))
        start = pl.multiple_of(j * ATT_K, ATT_K)
        vb = v_ref[pl.ds(start, ATT_K), hd * LANES:(hd + 1) * LANES]
        pv = jnp.dot(p.astype(jnp.bfloat16), vb, preferred_element_type=jnp.float32)
        return lane_fold(p, jnp.add), pv

    for hd in range(MLA_HEADS):
        ls, pv = probs(hd, qi)
        ls_ref[hd] = ls
        acc_ref[hd] = pv

    def pass_b(j, c):
        for hd in range(MLA_HEADS):
            ls, pv = probs(hd, j)
            ls_ref[hd] = ls_ref[hd] + ls
            acc_ref[hd] = acc_ref[hd] + pv
        return c

    lax.fori_loop(0, qi, pass_b, 0)

    heads = []
    for hd in range(MLA_HEADS):
        l = jnp.sum(ls_ref[hd], axis=-1, keepdims=True)
        heads.append((acc_ref[hd] / l).astype(jnp.bfloat16))

    mix_b = jnp.concatenate(heads, axis=-1)
    proj = (jnp.dot(outa_ref[...], wout_ref[:GM_WIDTH, :], preferred_element_type=jnp.float32)
            + jnp.dot(mix_b, wout_ref[GM_WIDTH:, :], preferred_element_type=jnp.float32))
    h1 = _layer_norm(ALPHA * h_ref[...] + proj, ln1g_ref[...], ln1b_ref[...])
    _to_row_tiles(h1r_ref, h1)

    h_hi = h1.astype(jnp.bfloat16)
    h_lo = (h1 - h_hi.astype(jnp.float32)).astype(jnp.bfloat16)
    w = wr_ref[...]
    w_hi = w.astype(jnp.bfloat16)
    w_lo = (w - w_hi.astype(jnp.float32)).astype(jnp.bfloat16)
    logits_tm = (jnp.dot(h_hi, w_hi, preferred_element_type=jnp.float32)
                 + jnp.dot(h_hi, w_lo, preferred_element_type=jnp.float32)
                 + jnp.dot(h_lo, w_hi, preferred_element_type=jnp.float32))
    logits = logits_tm.T[0:ROUTER_ROWS] + br_ref[...]

    sub_i = lax.broadcasted_iota(jnp.int32, (SUBLANES, tq), 0)
    sub = sub_i.astype(jnp.float32)
    neg = jnp.float32(-jnp.inf)
    g = jnp.where(sub_i < N_GROUPS, logits[0:SUBLANES], neg)
    gmax = jnp.max(g, axis=0, keepdims=True)
    g_top = jnp.min(jnp.where(g == gmax, sub, float(SUBLANES)), axis=0, keepdims=True)
    p_group = 1.0 / jnp.sum(jnp.exp(g - gmax), axis=0, keepdims=True)
    sel = logits[SUBLANES:2 * SUBLANES]
    for grp in range(1, N_GROUPS):
        sel = jnp.where(g_top == float(grp), logits[(grp + 1) * SUBLANES:(grp + 2) * SUBLANES], sel)
    v1 = jnp.max(sel, axis=0, keepdims=True)
    i1 = jnp.min(jnp.where(sel == v1, sub, float(SUBLANES)), axis=0, keepdims=True)
    sel2 = jnp.where(sub == i1, neg, sel)
    v2 = jnp.max(sel2, axis=0, keepdims=True)
    i2 = jnp.min(jnp.where(sel2 == v2, sub, float(SUBLANES)), axis=0, keepdims=True)
    e21 = jnp.exp(v2 - v1)
    w1 = 1.0 / (1.0 + e21)
    gate1 = p_group * w1
    gate2 = p_group * (e21 * w1)
    e1 = g_top * EXPERTS_PER_GROUP + i1
    e2 = g_top * EXPERTS_PER_GROUP + i2
    ri_ref[...] = jnp.where(sub_i == 0, e1, jnp.where(sub_i == 1, e2, 0.0)).astype(jnp.int32)
    rf_ref[...] = jnp.where(sub_i == 0, gate1, jnp.where(sub_i == 1, gate2, 0.0))


def _attn(q, k, v, outa, h, wout, ln1g, ln1b, wr, br):
    tq = ATT_Q
    tokblk = lambda cols: pl.BlockSpec((tq, cols), lambda b, i: (b * N_QBLK + i, 0))
    seqblk = lambda cols: pl.BlockSpec((SEQ, cols), lambda b, i: (b, 0))
    full = lambda shape: pl.BlockSpec(shape, lambda b, i: (0,) * len(shape))
    return pl.pallas_call(
        _attn_kernel,
        grid=(BATCH, N_QBLK),
        in_specs=[tokblk(D_MODEL), seqblk(D_MODEL), seqblk(GM_WIDTH), tokblk(GM_WIDTH),
                  tokblk(D_MODEL), full((D_MODEL, D_MODEL)), full((1, D_MODEL)), full((1, D_MODEL)),
                  full((D_MODEL, LANES)), full((ROUTER_ROWS, 1))],
        out_specs=[pl.BlockSpec((tq * FEAT_TILES, LANES), lambda b, i: (b * N_QBLK + i, 0)),
                   pl.BlockSpec((SUBLANES, tq), lambda b, i: (0, b * N_QBLK + i)),
                   pl.BlockSpec((SUBLANES, tq), lambda b, i: (0, b * N_QBLK + i))],
        out_shape=[jax.ShapeDtypeStruct((N_TOK * FEAT_TILES, LANES), jnp.float32),
                   jax.ShapeDtypeStruct((SUBLANES, N_TOK), jnp.int32),
                   jax.ShapeDtypeStruct((SUBLANES, N_TOK), jnp.float32)],
        scratch_shapes=[pltpu.VMEM((MLA_HEADS * (SEQ // ATT_K), tq, ATT_K), jnp.float32),
                        pltpu.VMEM((MLA_HEADS, tq, LANES), jnp.float32),
                        pltpu.VMEM((MLA_HEADS, tq, LANES), jnp.float32),
                        pltpu.VMEM((MLA_HEADS, tq, V_HEAD_DIM), jnp.float32)],
        compiler_params=pltpu.CompilerParams(
            dimension_semantics=("arbitrary", "arbitrary"), vmem_limit_bytes=VMEM_LIMIT),
        name="attn",
    )(q, k, v, outa, h, wout, ln1g, ln1b, wr, br)


def _plan_kernel(ri_ref, ldest_ref, runs_ref, meta_ref, cnt_ref, run_ref, start_ref):
    phase = pl.program_id(0)
    blk = pl.program_id(1)
    tb = ri_ref.shape[1]
    f32 = jnp.float32

    e_sub = lax.broadcasted_iota(jnp.int32, (N_EXPERTS, tb), 0)
    oh1 = e_sub == ri_ref[0:1, :]
    oh2 = e_sub == ri_ref[1:2, :]
    oh = jnp.where(oh1 | oh2, 1.0, 0.0).astype(f32)
    blk_count = jnp.sum(oh, axis=1, keepdims=True)

    @pl.when((phase == 0) & (blk == 0))
    def _():
        cnt_ref[...] = jnp.zeros_like(cnt_ref)

    @pl.when(phase == 0)
    def _():
        cnt_ref[...] = cnt_ref[...] + blk_count

    @pl.when((phase == 1) & (blk == 0))
    def _():
        counts = cnt_ref[:, 0:1]
        padded = jnp.floor((counts + (EXPERT_ROWS - 1)) * (1.0 / EXPERT_ROWS)) * EXPERT_ROWS
        er = lax.broadcasted_iota(jnp.int32, (N_EXPERTS, LANES), 0)
        ec = lax.broadcasted_iota(jnp.int32, (N_EXPERTS, LANES), 1)
        padded_row = jnp.sum(jnp.where(er == ec, padded, 0.0), axis=0, keepdims=True)
        counts_row = jnp.sum(jnp.where(er == ec, counts, 0.0), axis=0, keepdims=True)
        pad_end = jnp.sum(jnp.where(ec <= er, padded_row, 0.0), axis=1, keepdims=True)
        pad_end_row = jnp.sum(jnp.where(er == ec, pad_end, 0.0), axis=0, keepdims=True)
        start_ref[...] = jnp.broadcast_to(pad_end - padded, start_ref.shape)
        run_ref[...] = jnp.zeros_like(run_ref)
        bstart = (lax.broadcasted_iota(jnp.int32, (N_EXPERTS, META_LANES), 1) * EXPERT_ROWS).astype(f32)
        blk_e = jnp.sum(jnp.where(pad_end <= bstart, 1.0, 0.0), axis=0, keepdims=True)
        blk_e = jnp.minimum(blk_e, N_EXPERTS - 1.0)
        n_used = pad_end[N_EXPERTS - 1:N_EXPERTS, :] * (1.0 / EXPERT_ROWS)
        pad3 = lambda r: jnp.concatenate(
            [r, jnp.zeros((1, META_LANES - LANES), f32)], axis=1)
        msub = lax.broadcasted_iota(jnp.int32, (SUBLANES, META_LANES), 0)
        meta = jnp.where(msub == 0, blk_e,
                         jnp.where(msub == 1, pad3(pad_end_row),
                                   jnp.where(msub == 2, pad3(counts_row),
                                             jnp.where(msub == 3, n_used, 0.0))))
        meta_ref[...] = meta.astype(jnp.int32)

    @pl.when(phase == 1)
    def _():
        tr = lax.broadcasted_iota(jnp.int32, (tb, tb), 0)
        tc = lax.broadcasted_iota(jnp.int32, (tb, tb), 1)
        upper = jnp.where(tr < tc, 1.0, 0.0).astype(jnp.bfloat16)
        prefix = jnp.dot(oh.astype(jnp.bfloat16), upper, preferred_element_type=f32)
        er = lax.broadcasted_iota(jnp.int32, (N_EXPERTS, LANES), 0)
        ec = lax.broadcasted_iota(jnp.int32, (N_EXPERTS, LANES), 1)
        to_row = lambda col: jnp.sum(jnp.where(er == ec, col, 0.0), axis=0, keepdims=True)
        cnt_row = to_row(blk_count)
        lstart = jnp.sum(jnp.where(ec < er, cnt_row, 0.0), axis=1, keepdims=True)
        base = prefix + lstart
        d1 = jnp.sum(jnp.where(oh1, base, 0.0), axis=0, keepdims=True)
        d2 = jnp.sum(jnp.where(oh2, base, 0.0), axis=0, keepdims=True)
        sub = lax.broadcasted_iota(jnp.int32, (SUBLANES, tb), 0)
        ldest_ref[...] = jnp.where(sub == 0, d1, jnp.where(sub == 1, d2, 0.0)).astype(jnp.int32)
        gstart = start_ref[:, 0:1] + run_ref[:, 0:1]
        rsub = lax.broadcasted_iota(jnp.int32, (SUBLANES, LANES), 0)
        runs = jnp.where(rsub == 0, cnt_row,
                         jnp.where(rsub == 1, to_row(lstart), jnp.where(rsub == 2, to_row(gstart), 0.0)))
        runs_ref[...] = runs.astype(jnp.int32)
        run_ref[...] = run_ref[...] + blk_count


def _plan(ri):
    tb = MOE_TOKENS
    return pl.pallas_call(
        _plan_kernel,
        grid=(2, N_TOK // tb),
        in_specs=[pl.BlockSpec((SUBLANES, tb), lambda p, i: (0, i))],
        out_specs=[pl.BlockSpec((SUBLANES, tb), lambda p, i: (0, i * p)),
                   pl.BlockSpec((SUBLANES, LANES), lambda p, i: (i * p, 0)),
                   pl.BlockSpec((SUBLANES, META_LANES), lambda p, i: (0, 0))],
        out_shape=[jax.ShapeDtypeStruct((SUBLANES, N_TOK), jnp.int32),
                   jax.ShapeDtypeStruct((N_MOE_BLOCKS * SUBLANES, LANES), jnp.int32),
                   jax.ShapeDtypeStruct((SUBLANES, META_LANES), jnp.int32)],
        scratch_shapes=[pltpu.VMEM((N_EXPERTS, LANES), jnp.float32),
                        pltpu.VMEM((N_EXPERTS, LANES), jnp.float32),
                        pltpu.VMEM((N_EXPERTS, LANES), jnp.float32)],
        compiler_params=pltpu.CompilerParams(dimension_semantics=("arbitrary", "arbitrary")),
        name="plan",
    )(ri)


def _for_each_run_piece(runs_ref, fn):
    for e in range(N_EXPERTS):
        n, lstart, gstart = runs_ref[0, e], runs_ref[1, e], runs_ref[2, e]
        for bit in range(RUN_BITS):
            @pl.when((n & (1 << bit)) != 0)
            def _(n=n, lstart=lstart, gstart=gstart, bit=bit):
                off = (n >> (bit + 1)) << (bit + 1)
                fn(lstart + off, gstart + off, 1 << bit)


def _tile_rows(ref, row, rows):
    return ref.at[pl.ds(pl.multiple_of(row * FEAT_TILES, FEAT_TILES), rows * FEAT_TILES)]


def _dispatch_kernel(meta_ref, runs_ref, ldest_ref, h1r_ref, buf_ref, sorted_ref, zero_ref, sems, zsem):
    i = pl.program_id(0)
    n_steps = pl.num_programs(0)
    tb = ldest_ref.shape[1]
    slot = i % 2
    block_tiles = 2 * tb * FEAT_TILES

    def wait_slot(s):
        pltpu.make_async_copy(sorted_ref.at[s], buf_ref.at[pl.ds(0, block_tiles)], sems.at[s]).wait()

    @pl.when(i == 0)
    def _():
        zero_ref[...] = jnp.zeros_like(zero_ref)

        def zero_copy(e):
            start = pl.multiple_of((meta_ref[1, e] - EXPERT_ROWS) * FEAT_TILES, EXPERT_ROWS * FEAT_TILES)
            return pltpu.make_async_copy(
                zero_ref, buf_ref.at[pl.ds(start, EXPERT_ROWS * FEAT_TILES)], zsem)

        def start_zero(e, c):
            @pl.when(meta_ref[2, e] > 0)
            def _():
                zero_copy(e).start()
            return c

        def wait_zero(e, c):
            @pl.when(meta_ref[2, e] > 0)
            def _():
                zero_copy(e).wait()
            return c

        def tail_copy(b):
            start = pl.multiple_of(b * (EXPERT_ROWS * FEAT_TILES), EXPERT_ROWS * FEAT_TILES)
            return pltpu.make_async_copy(
                zero_ref, buf_ref.at[pl.ds(start, EXPERT_ROWS * FEAT_TILES)], zsem)

        def start_tail(b, c):
            tail_copy(b).start()
            return c

        def wait_tail(b, c):
            tail_copy(b).wait()
            return c

        lax.fori_loop(0, N_EXPERTS, start_zero, 0)
        lax.fori_loop(meta_ref[3, 0], N_ROW_BLOCKS, start_tail, 0)
        lax.fori_loop(0, N_EXPERTS, wait_zero, 0)
        lax.fori_loop(meta_ref[3, 0], N_ROW_BLOCKS, wait_tail, 0)

    @pl.when(i >= 2)
    def _():
        wait_slot(slot)

    x = _from_row_tiles(h1r_ref, tb).astype(jnp.bfloat16)
    ld0 = ldest_ref[0:1, :]
    ld1 = ldest_ref[1:2, :]
    for c in range(2 * tb // SORT_CHUNK):
        r = lax.broadcasted_iota(jnp.int32, (SORT_CHUNK, tb), 0) + c * SORT_CHUNK
        perm = jnp.where((r == ld0) | (r == ld1), 1.0, 0.0).astype(jnp.bfloat16)
        rows = jnp.dot(perm, x, preferred_element_type=jnp.float32)
        _to_row_tiles(sorted_ref.at[slot, pl.ds(c * SORT_CHUNK * FEAT_TILES, SORT_CHUNK * FEAT_TILES)], rows)

    def send(lrow, grow, rows):
        pltpu.make_async_copy(_tile_rows(sorted_ref.at[slot], lrow, rows),
                              _tile_rows(buf_ref, grow, rows), sems.at[slot]).start()

    _for_each_run_piece(runs_ref, send)

    @pl.when(i == n_steps - 1)
    def _():
        wait_slot(slot)
        wait_slot(1 - slot)


def _dispatch(meta, runs, ldest, h1r):
    tb = MOE_TOKENS
    return pl.pallas_call(
        _dispatch_kernel,
        grid_spec=pltpu.PrefetchScalarGridSpec(
            num_scalar_prefetch=1,
            grid=(N_MOE_BLOCKS,),
            in_specs=[pl.BlockSpec((SUBLANES, LANES), lambda i, m: (i, 0), memory_space=pltpu.SMEM),
                      pl.BlockSpec((SUBLANES, tb), lambda i, m: (0, i)),
                      pl.BlockSpec((tb * FEAT_TILES, LANES), lambda i, m: (i, 0))],
            out_specs=pl.BlockSpec(memory_space=pl.ANY),
            scratch_shapes=[pltpu.VMEM((2, 2 * tb * FEAT_TILES, LANES), jnp.float32),
                            pltpu.VMEM((EXPERT_ROWS * FEAT_TILES, LANES), jnp.float32),
                            pltpu.SemaphoreType.DMA((2,)), pltpu.SemaphoreType.DMA]),
        out_shape=jax.ShapeDtypeStruct((N_ROWS * FEAT_TILES, LANES), jnp.float32),
        compiler_params=pltpu.CompilerParams(
            dimension_semantics=("arbitrary",), vmem_limit_bytes=VMEM_LIMIT),
        name="dispatch",
    )(meta, runs, ldest, h1r)


def _experts_kernel(meta_ref, x_ref, wg_ref, wu_ref, wd_ref, o_ref):
    n = EXPERT_ROWS

    @pl.when(pl.program_id(0) < meta_ref[3, 0])
    def _():
        x = _from_row_tiles(x_ref, n).astype(jnp.bfloat16)
        gate = jnp.dot(x, wg_ref[...], preferred_element_type=jnp.float32)
        up = jnp.dot(x, wu_ref[...], preferred_element_type=jnp.float32)
        act = (gate * jax.nn.sigmoid(gate) * up).astype(jnp.bfloat16)
        _to_row_tiles(o_ref, jnp.dot(act, wd_ref[...], preferred_element_type=jnp.float32))

    @pl.when(pl.program_id(0) >= meta_ref[3, 0])
    def _():
        o_ref[...] = jnp.zeros_like(o_ref)


def _experts(meta, buf, wg, wu, wd):
    used = lambda i, m: jnp.minimum(i, m[3, 0] - 1)
    return pl.pallas_call(
        _experts_kernel,
        grid_spec=pltpu.PrefetchScalarGridSpec(
            num_scalar_prefetch=1,
            grid=(N_ROW_BLOCKS,),
            in_specs=[pl.BlockSpec((EXPERT_ROWS * FEAT_TILES, LANES), lambda i, m: (used(i, m), 0)),
                      pl.BlockSpec((None, D_MODEL, EXPERT_FF), lambda i, m: (m[0, used(i, m)], 0, 0)),
                      pl.BlockSpec((None, D_MODEL, EXPERT_FF), lambda i, m: (m[0, used(i, m)], 0, 0)),
                      pl.BlockSpec((None, EXPERT_FF, D_MODEL), lambda i, m: (m[0, used(i, m)], 0, 0))],
            out_specs=pl.BlockSpec((EXPERT_ROWS * FEAT_TILES, LANES), lambda i, m: (i, 0))),
        out_shape=jax.ShapeDtypeStruct((N_ROWS * FEAT_TILES, LANES), jnp.float32),
        compiler_params=pltpu.CompilerParams(
            dimension_semantics=("arbitrary",), vmem_limit_bytes=VMEM_LIMIT),
        name="experts",
    )(meta, buf, wg, wu, wd)


def _combine_kernel(runs_ref, runs_next_ref, ldest_ref, rf_ref, h1r_ref, eout_ref, ln2g_ref, ln2b_ref,
                    o_ref, y_ref, sems):
    i = pl.program_id(0)
    n_steps = pl.num_programs(0)
    tb = ldest_ref.shape[1]
    slot = i % 2
    block_tiles = 2 * tb * FEAT_TILES

    def fetch(table_ref, s):
        def recv(lrow, grow, rows):
            pltpu.make_async_copy(_tile_rows(eout_ref, grow, rows),
                                  _tile_rows(y_ref.at[s], lrow, rows), sems.at[s]).start()
        _for_each_run_piece(table_ref, recv)

    @pl.when(i == 0)
    def _():
        fetch(runs_ref, slot)

    @pl.when(i + 1 < n_steps)
    def _():
        fetch(runs_next_ref, 1 - slot)

    pltpu.make_async_copy(eout_ref.at[pl.ds(0, block_tiles)], y_ref.at[slot], sems.at[slot]).wait()

    ld = ldest_ref[...].astype(jnp.float32).T
    gates = rf_ref[...].T
    y = None
    for c in range(2 * tb // SORT_CHUNK):
        col = (lax.broadcasted_iota(jnp.int32, (tb, SORT_CHUNK), 1) + c * SORT_CHUNK).astype(jnp.float32)
        g = (jnp.where(col == ld[:, 0:1], gates[:, 0:1], 0.0)
             + jnp.where(col == ld[:, 1:2], gates[:, 1:2], 0.0)).astype(jnp.bfloat16)
        rows = _from_row_tiles(
            y_ref.at[slot, pl.ds(c * SORT_CHUNK * FEAT_TILES, SORT_CHUNK * FEAT_TILES)], SORT_CHUNK)
        part = jnp.dot(g, rows.astype(jnp.bfloat16), preferred_element_type=jnp.float32)
        y = part if y is None else y + part
    h1 = _from_row_tiles(h1r_ref, tb)
    o_ref[...] = _layer_norm(ALPHA * h1 + y, ln2g_ref[...], ln2b_ref[...])


def _combine(runs, ldest, rf, h1r, eout, ln2g, ln2b):
    tb = MOE_TOKENS
    last = N_MOE_BLOCKS - 1
    return pl.pallas_call(
        _combine_kernel,
        grid=(N_MOE_BLOCKS,),
        in_specs=[pl.BlockSpec((SUBLANES, LANES), lambda i: (i, 0), memory_space=pltpu.SMEM),
                  pl.BlockSpec((SUBLANES, LANES), lambda i: (jnp.minimum(i + 1, last), 0),
                               memory_space=pltpu.SMEM),
                  pl.BlockSpec((SUBLANES, tb), lambda i: (0, i)),
                  pl.BlockSpec((SUBLANES, tb), lambda i: (0, i)),
                  pl.BlockSpec((tb * FEAT_TILES, LANES), lambda i: (i, 0)),
                  pl.BlockSpec(memory_space=pl.ANY),
                  pl.BlockSpec((1, D_MODEL), lambda i: (0, 0)),
                  pl.BlockSpec((1, D_MODEL), lambda i: (0, 0))],
        out_specs=pl.BlockSpec((tb, D_MODEL), lambda i: (i, 0)),
        out_shape=jax.ShapeDtypeStruct((N_TOK, D_MODEL), jnp.float32),
        scratch_shapes=[pltpu.VMEM((2, 2 * tb * FEAT_TILES, LANES), jnp.float32),
                        pltpu.SemaphoreType.DMA((2,))],
        compiler_params=pltpu.CompilerParams(
            dimension_semantics=("arbitrary",), vmem_limit_bytes=VMEM_LIMIT),
        name="combine",
    )(runs, runs, ldest, rf, h1r, eout, ln2g, ln2b)


def _swap_halves(w):
    half = w.shape[-1] // 2
    return jnp.concatenate([w[..., half:], w[..., :half]], axis=-1)


def kernel(x, positions, ln0_g, ln0_b, w_in, gm_ln_g, gm_ln_b, w_spatial, b_spatial, q_norm_g, w_uq, kv_norm_g, w_ukv, w_out, ln1_g, ln1_b, w_router_group, b_router_group, w_router_expert, b_router_expert, w_gate, w_up, w_down, ln2_g, ln2_b):
    bf16 = jnp.bfloat16
    row = lambda a: a.reshape(1, -1)

    w_in0 = w_in[0]
    kr_cols = w_in0[:, O_KR:O_KR + QK_ROPE_DIM]
    win = jnp.concatenate([w_in0, _swap_halves(kr_cols)], axis=1).astype(bf16)
    wuq3 = w_uq[0].reshape(Q_LORA_RANK, MLA_HEADS, QK_NOPE_DIM + QK_ROPE_DIM)
    rope_cols = wuq3[:, :, QK_NOPE_DIM:]
    wuq = jnp.concatenate([wuq3, _swap_halves(rope_cols)], axis=-1).reshape(Q_LORA_RANK, D_MODEL).astype(bf16)
    wukv = w_ukv[0].astype(bf16)
    wout = w_out[0].astype(bf16)
    bs = jnp.broadcast_to(b_spatial[0][:, :, None], (GM_HEADS, GM_CHUNK, GM_HEAD_DIM))
    wr = jnp.concatenate([w_router_group[0], jnp.zeros((D_MODEL, SUBLANES - N_GROUPS), jnp.float32),
                          w_router_expert[0],
                          jnp.zeros((D_MODEL, LANES - ROUTER_ROWS), jnp.float32)], axis=1)
    br = jnp.concatenate([b_router_group[0], jnp.zeros((SUBLANES - N_GROUPS,), jnp.float32),
                          b_router_expert[0]]).reshape(ROUTER_ROWS, 1)
    wg = w_gate[0].astype(bf16)
    wu = w_up[0].astype(bf16)
    wd = w_down[0].astype(bf16)

    inv_freq = ROPE_THETA ** (-jnp.arange(0, QK_ROPE_DIM, 2, dtype=jnp.float32) / QK_ROPE_DIM)
    freq = jnp.tile(inv_freq, 4).reshape(1, LANES)
    quarter = QK_ROPE_DIM // 2
    phase = jnp.concatenate([jnp.zeros((2 * quarter,), jnp.float32),
                             jnp.full((2 * quarter,), math.pi / 2, jnp.float32)]).reshape(1, LANES)
    sign = jnp.concatenate([jnp.ones((2 * quarter,), jnp.float32), -jnp.ones((quarter,), jnp.float32),
                            jnp.ones((quarter,), jnp.float32)]).reshape(1, LANES)

    x2 = x.reshape(N_TOK, D_MODEL)
    pos2 = positions.reshape(N_TOK, 1)

    h, outa, q, k, v = _prep(x2, pos2, row(ln0_g), row(ln0_b), win, row(gm_ln_g[0]), row(gm_ln_b[0]),
                             w_spatial[0], bs, row(q_norm_g[0]), wuq, row(kv_norm_g[0]), wukv,
                             freq, phase, sign)
    h1r, ri, rf = _attn(q, k, v, outa, h, wout, row(ln1_g[0]), row(ln1_b[0]), wr, br)
    ldest, runs, meta = _plan(ri)
    buf = _dispatch(meta, runs, ldest, h1r)
    eout = _experts(meta, buf, wg, wu, wd)
    out = _combine(runs, ldest, rf, h1r, eout, row(ln2_g[0]), row(ln2_b[0]))
    return out.reshape(BATCH, SEQ, D_MODEL)
```

```python
import functools
import math

import jax
import jax.numpy as jnp
from jax import lax
from jax.experimental import pallas as pl
from jax.experimental.pallas import tpu as pltpu

D_MODEL = 1024
BATCH = 16
SEQ = 2048
N_TOK = BATCH * SEQ
CHUNK = 64
GM_WIDTH = 512
GM_HEADS = 4
GM_HEAD_DIM = 128
GM_CHUNK = 128
MLA_HEADS = 4
QK_NOPE_DIM = 128
QK_ROPE_DIM = 64
V_HEAD_DIM = 128
Q_LORA_RANK = 384
KV_LORA_RANK = 256
ROPE_THETA = 10000.0
N_GROUPS = 4
EXPERTS_PER_GROUP = 8
N_EXPERTS = 32
TOP_K = 2
EXPERT_FF = 256
ALPHA = 2.0 ** 0.25
QK_SCALE = (QK_NOPE_DIM + QK_ROPE_DIM) ** -0.5 * math.log2(math.e)

LANES = 128
SUBLANES = 8
FEAT_TILES = D_MODEL // LANES
PACK_TILES = FEAT_TILES // 2

PREP_TOKENS = 512
ATT_Q = 256
ATT_K = 256
N_QBLK = SEQ // ATT_Q
MOE_TOKENS = 512
N_MOE_BLOCKS = N_TOK // MOE_TOKENS
RUN_BITS = (TOP_K * MOE_TOKENS).bit_length()
SORT_CHUNK = 256
EXPERT_ROWS = 256
N_ROWS = N_TOK * TOP_K + N_EXPERTS * EXPERT_ROWS
N_ROW_BLOCKS = N_ROWS // EXPERT_ROWS
META_LANES = 384
IN_COLS = 2 * GM_WIDTH + Q_LORA_RANK + KV_LORA_RANK + 2 * QK_ROPE_DIM
O_Q = 2 * GM_WIDTH
O_KV = O_Q + Q_LORA_RANK
O_KR = O_KV + KV_LORA_RANK
ROUTER_ROWS = 40
VMEM_LIMIT = 48 * 1024 * 1024

assert N_ROW_BLOCKS <= META_LANES


def _layer_norm(x, g, b, eps=1e-5):
    mu = jnp.mean(x, axis=-1, keepdims=True)
    xc = x - mu
    var = jnp.mean(xc * xc, axis=-1, keepdims=True)
    return xc * lax.rsqrt(var + eps) * g + b


def _rms_norm(x, g, eps=1e-6):
    return x * lax.rsqrt(jnp.mean(x * x, axis=-1, keepdims=True) + eps) * g


def _gelu_tanh(x):
    c = math.sqrt(2.0 / math.pi)
    return 0.5 * x * (1.0 + jnp.tanh(c * (x + 0.044715 * (x * x * x))))


def _to_row_tiles(ref, x):
    n = x.shape[0]
    for s in range(FEAT_TILES):
        ref[pl.ds(s, n, stride=FEAT_TILES), :] = x[:, s * LANES:(s + 1) * LANES]


def _from_row_tiles(ref, n):
    return jnp.concatenate(
        [ref[pl.ds(s, n, stride=FEAT_TILES), :] for s in range(FEAT_TILES)], axis=-1)


def _pack_rows(ref, x):
    n = x.shape[0]
    bits = lax.bitcast_convert_type(x, jnp.uint32)
    for s in range(PACK_TILES):
        lo = bits[:, s * LANES:(s + 1) * LANES] >> 16
        hi = bits[:, (s + PACK_TILES) * LANES:(s + PACK_TILES + 1) * LANES] & jnp.uint32(0xFFFF0000)
        ref[pl.ds(s, n, stride=PACK_TILES), :] = lo | hi


def _unpack_rows(ref, n):
    words = [ref[pl.ds(s, n, stride=PACK_TILES), :] for s in range(PACK_TILES)]
    lo = [lax.bitcast_convert_type(w << 16, jnp.float32) for w in words]
    hi = [lax.bitcast_convert_type(w & jnp.uint32(0xFFFF0000), jnp.float32) for w in words]
    return jnp.concatenate(lo + hi, axis=-1)


def _prep_kernel(x_ref, pos_ref, ln0g_ref, ln0b_ref, win_ref, gmg_ref, gmb_ref, ws_ref, bs_ref,
                 qg_ref, wuq_ref, kvg_ref, wukv_ref, freq_ref, phase_ref, sign_ref,
                 h_ref, outa_ref, q_ref, k_ref, v_ref):
    tb = x_ref.shape[0]
    h = _layer_norm(x_ref[...], ln0g_ref[...], ln0b_ref[...])
    h_ref[...] = h
    z = jnp.dot(h.astype(jnp.bfloat16), win_ref[...], preferred_element_type=jnp.float32)

    ang = pos_ref[...].astype(jnp.float32) * freq_ref[...]
    rot = jnp.cos(ang - phase_ref[...]) * sign_ref[...]

    u = _gelu_tanh(z[:, :GM_WIDTH])
    v = _gelu_tanh(z[:, GM_WIDTH:2 * GM_WIDTH])
    row_chunk = lax.broadcasted_iota(jnp.int32, (GM_CHUNK, GM_CHUNK), 0) // CHUNK
    col_chunk = lax.broadcasted_iota(jnp.int32, (GM_CHUNK, GM_CHUNK), 1) // CHUNK
    allowed = col_chunk <= row_chunk
    for hd in range(GM_HEADS):
        lo, hi = hd * GM_HEAD_DIM, (hd + 1) * GM_HEAD_DIM
        vln = _layer_norm(v[:, lo:hi], gmg_ref[:, lo:hi], gmb_ref[:, lo:hi]).astype(jnp.bfloat16)
        wm = jnp.where(allowed, ws_ref[hd], 0.0).astype(jnp.bfloat16)
        for c in range(tb // GM_CHUNK):
            r0, r1 = c * GM_CHUNK, (c + 1) * GM_CHUNK
            f = jnp.dot(wm, vln[r0:r1], preferred_element_type=jnp.float32) + bs_ref[hd]
            outa_ref[r0:r1, lo:hi] = (u[r0:r1, lo:hi] * f).astype(jnp.bfloat16)

    ql = _rms_norm(z[:, O_Q:O_KV], qg_ref[...]).astype(jnp.bfloat16)
    qf = jnp.dot(ql, wuq_ref[...], preferred_element_type=jnp.float32)
    rot_s = rot * QK_SCALE
    q_parts = []
    for hd in range(MLA_HEADS):
        base = hd * 2 * LANES
        q_parts.append(qf[:, base:base + LANES] * QK_SCALE)
        q_parts.append(qf[:, base + LANES:base + 2 * LANES] * rot_s)
    q_ref[...] = jnp.concatenate(q_parts, axis=-1).astype(jnp.bfloat16)

    kvl = _rms_norm(z[:, O_KV:O_KR], kvg_ref[...]).astype(jnp.bfloat16)
    kv = jnp.dot(kvl, wukv_ref[...], preferred_element_type=jnp.float32)
    t = z[:, O_KR:O_KR + LANES] * rot
    krr = t + pltpu.roll(t, 2 * QK_ROPE_DIM // 2, axis=1)
    k_parts, v_parts = [], []
    for hd in range(MLA_HEADS):
        base = hd * 2 * LANES
        k_parts.append(kv[:, base:base + LANES])
        k_parts.append(krr)
        v_parts.append(kv[:, base + LANES:base + 2 * LANES])
    k_ref[...] = jnp.concatenate(k_parts, axis=-1).astype(jnp.bfloat16)
    v_ref[...] = jnp.concatenate(v_parts, axis=-1).astype(jnp.bfloat16)


def _prep(x2, pos2, ln0g, ln0b, win, gmg, gmb, ws, bs, qg, wuq, kvg, wukv, freq, phase, sign):
    tb = PREP_TOKENS
    full = lambda shape: pl.BlockSpec(shape, lambda i: (0,) * len(shape))
    tok = lambda cols: pl.BlockSpec((tb, cols), lambda i: (i, 0))
    return pl.pallas_call(
        _prep_kernel,
        grid=(N_TOK // tb,),
        in_specs=[tok(D_MODEL), tok(1), full((1, D_MODEL)), full((1, D_MODEL)),
                  full((D_MODEL, IN_COLS)), full((1, GM_WIDTH)), full((1, GM_WIDTH)),
                  full((GM_HEADS, GM_CHUNK, GM_CHUNK)), full((GM_HEADS, GM_CHUNK, GM_HEAD_DIM)),
                  full((1, Q_LORA_RANK)), full((Q_LORA_RANK, D_MODEL)),
                  full((1, KV_LORA_RANK)), full((KV_LORA_RANK, D_MODEL)),
                  full((1, LANES)), full((1, LANES)), full((1, LANES))],
        out_specs=[tok(D_MODEL), tok(GM_WIDTH), tok(D_MODEL), tok(D_MODEL), tok(GM_WIDTH)],
        out_shape=[jax.ShapeDtypeStruct((N_TOK, D_MODEL), jnp.float32),
                   jax.ShapeDtypeStruct((N_TOK, GM_WIDTH), jnp.bfloat16),
                   jax.ShapeDtypeStruct((N_TOK, D_MODEL), jnp.bfloat16),
                   jax.ShapeDtypeStruct((N_TOK, D_MODEL), jnp.bfloat16),
                   jax.ShapeDtypeStruct((N_TOK, GM_WIDTH), jnp.bfloat16)],
        compiler_params=pltpu.CompilerParams(
            dimension_semantics=("arbitrary",), vmem_limit_bytes=VMEM_LIMIT),
        name="prep",
    )(x2, pos2, ln0g, ln0b, win, gmg, gmb, ws, bs, qg, wuq, kvg, wukv, freq, phase, sign)


def _attn_kernel(q_ref, k_ref, v_ref, outa_ref, h_ref, wout_ref, ln1g_ref, ln1b_ref,
                 wr_ref, br_ref, h1r_ref, ri_ref, rf_ref, s_ref, mx_ref, ls_ref, acc_ref):
    qi = pl.program_id(1)
    tq = q_ref.shape[0]
    nt = (((1,), (1,)), ((), ()))
    n_kv = SEQ // ATT_K

    row_chunk = (lax.broadcasted_iota(jnp.int32, (tq, ATT_K), 0)) // CHUNK
    col_chunk = (lax.broadcasted_iota(jnp.int32, (tq, ATT_K), 1)) // CHUNK
    diag_allowed = col_chunk <= row_chunk

    def lane_fold(x, op):
        return op(x[:, :LANES], x[:, LANES:])

    def scores(hd, j):
        start = pl.multiple_of(j * ATT_K, ATT_K)
        q_h = q_ref[:, hd * 2 * LANES:(hd + 1) * 2 * LANES]
        kb = k_ref[pl.ds(start, ATT_K), hd * 2 * LANES:(hd + 1) * 2 * LANES]
        return lax.dot_general(q_h, kb, nt, preferred_element_type=jnp.float32)

    for hd in range(MLA_HEADS):
        s = jnp.where(diag_allowed, scores(hd, qi), jnp.float32(-1e30))
        s_ref[hd * n_kv + qi] = s
        mx_ref[hd] = lane_fold(s, jnp.maximum)

    def pass_a(j, c):
        for hd in range(MLA_HEADS):
            s = scores(hd, j)
            s_ref[hd * n_kv + j] = s
            mx_ref[hd] = jnp.maximum(mx_ref[hd], lane_fold(s, jnp.maximum))
        return c

    lax.fori_loop(0, qi, pass_a, 0)

    for hd in range(MLA_HEADS):
        mx_ref[hd] = jnp.broadcast_to(jnp.max(mx_ref[hd], axis=-1, keepdims=True), (tq, LANES))

    def probs(hd, j):
        s = s_ref[hd * n_kv + j]
        mb = mx_ref[hd]
        p = jnp.exp2(jnp.concatenate([s[:, :LANES] - mb, s[:, LANES:] - mb], axis=-1))
        start = pl.multiple_of(j * ATT_K, ATT_K)
        vb = v_ref[pl.ds(start, ATT_K), hd * LANES:(hd + 1) * LANES]
        pv = jnp.dot(p.astype(jnp.bfloat16), vb, preferred_element_type=jnp.float32)
        return lane_fold(p, jnp.add), pv

    for hd in range(MLA_HEADS):
        ls, pv = probs(hd, qi)
        ls_ref[hd] = ls
        acc_ref[hd] = pv

    def pass_b(j, c):
        for hd in range(MLA_HEADS):
            ls, pv = probs(hd, j)
            ls_ref[hd] = ls_ref[hd] + ls
            acc_ref[hd] = acc_ref[hd] + pv
        return c

    lax.fori_loop(0, qi, pass_b, 0)

    heads = []
    for hd in range(MLA_HEADS):
        l = jnp.sum(ls_ref[hd], axis=-1, keepdims=True)
        heads.append((acc_ref[hd] / l).astype(jnp.bfloat16))

    mix_b = jnp.concatenate(heads, axis=-1)
    proj = (jnp.dot(outa_ref[...], wout_ref[:GM_WIDTH, :], preferred_element_type=jnp.float32)
            + jnp.dot(mix_b, wout_ref[GM_WIDTH:, :], preferred_element_type=jnp.float32))
    h1 = _layer_norm(ALPHA * h_ref[...] + proj, ln1g_ref[...], ln1b_ref[...])
    _to_row_tiles(h1r_ref, h1)

    h_hi = h1.astype(jnp.bfloat16)
    h_lo = (h1 - h_hi.astype(jnp.float32)).astype(jnp.bfloat16)
    w = wr_ref[...]
    w_hi = w.astype(jnp.bfloat16)
    w_lo = (w - w_hi.astype(jnp.float32)).astype(jnp.bfloat16)
    logits_tm = (jnp.dot(h_hi, w_hi, preferred_element_type=jnp.float32)
                 + jnp.dot(h_hi, w_lo, preferred_element_type=jnp.float32)
                 + jnp.dot(h_lo, w_hi, preferred_element_type=jnp.float32))
    logits = logits_tm.T[0:ROUTER_ROWS] + br_ref[...]

    sub_i = lax.broadcasted_iota(jnp.int32, (SUBLANES, tq), 0)
    sub = sub_i.astype(jnp.float32)
    neg = jnp.float32(-jnp.inf)
    g = jnp.where(sub_i < N_GROUPS, logits[0:SUBLANES], neg)
    gmax = jnp.max(g, axis=0, keepdims=True)
    g_top = jnp.min(jnp.where(g == gmax, sub, float(SUBLANES)), axis=0, keepdims=True)
    p_group = 1.0 / jnp.sum(jnp.exp(g - gmax), axis=0, keepdims=True)
    sel = logits[SUBLANES:2 * SUBLANES]
    for grp in range(1, N_GROUPS):
        sel = jnp.where(g_top == float(grp), logits[(grp + 1) * SUBLANES:(grp + 2) * SUBLANES], sel)
    v1 = jnp.max(sel, axis=0, keepdims=True)
    i1 = jnp.min(jnp.where(sel == v1, sub, float(SUBLANES)), axis=0, keepdims=True)
    sel2 = jnp.where(sub == i1, neg, sel)
    v2 = jnp.max(sel2, axis=0, keepdims=True)
    i2 = jnp.min(jnp.where(sel2 == v2, sub, float(SUBLANES)), axis=0, keepdims=True)
    e21 = jnp.exp(v2 - v1)
    w1 = 1.0 / (1.0 + e21)
    gate1 = p_group * w1
    gate2 = p_group * (e21 * w1)
    e1 = g_top * EXPERTS_PER_GROUP + i1
    e2 = g_top * EXPERTS_PER_GROUP + i2
    ri_ref[...] = jnp.where(sub_i == 0, e1, jnp.where(sub_i == 1, e2, 0.0)).astype(jnp.int32)
    rf_ref[...] = jnp.where(sub_i == 0, gate1, jnp.where(sub_i == 1, gate2, 0.0))


def _attn(q, k, v, outa, h, wout, ln1g, ln1b, wr, br):
    tq = ATT_Q
    tokblk = lambda cols: pl.BlockSpec((tq, cols), lambda b, i: (b * N_QBLK + i, 0))
    seqblk = lambda cols: pl.BlockSpec((SEQ, cols), lambda b, i: (b, 0))
    full = lambda shape: pl.BlockSpec(shape, lambda b, i: (0,) * len(shape))
    return pl.pallas_call(
        _attn_kernel,
        grid=(BATCH, N_QBLK),
        in_specs=[tokblk(D_MODEL), seqblk(D_MODEL), seqblk(GM_WIDTH), tokblk(GM_WIDTH),
                  tokblk(D_MODEL), full((D_MODEL, D_MODEL)), full((1, D_MODEL)), full((1, D_MODEL)),
                  full((D_MODEL, LANES)), full((ROUTER_ROWS, 1))],
        out_specs=[pl.BlockSpec((tq * FEAT_TILES, LANES), lambda b, i: (b * N_QBLK + i, 0)),
                   pl.BlockSpec((SUBLANES, tq), lambda b, i: (0, b * N_QBLK + i)),
                   pl.BlockSpec((SUBLANES, tq), lambda b, i: (0, b * N_QBLK + i))],
        out_shape=[jax.ShapeDtypeStruct((N_TOK * FEAT_TILES, LANES), jnp.float32),
                   jax.ShapeDtypeStruct((SUBLANES, N_TOK), jnp.int32),
                   jax.ShapeDtypeStruct((SUBLANES, N_TOK), jnp.float32)],
        scratch_shapes=[pltpu.VMEM((MLA_HEADS * (SEQ // ATT_K), tq, ATT_K), jnp.float32),
                        pltpu.VMEM((MLA_HEADS, tq, LANES), jnp.float32),
                        pltpu.VMEM((MLA_HEADS, tq, LANES), jnp.float32),
                        pltpu.VMEM((MLA_HEADS, tq, V_HEAD_DIM), jnp.float32)],
        compiler_params=pltpu.CompilerParams(
            dimension_semantics=("arbitrary", "arbitrary"), vmem_limit_bytes=VMEM_LIMIT),
        name="attn",
    )(q, k, v, outa, h, wout, ln1g, ln1b, wr, br)


def _plan_kernel(ri_ref, ldest_ref, runs_ref, meta_ref, cnt_ref, run_ref, start_ref):
    phase = pl.program_id(0)
    blk = pl.program_id(1)
    tb = ri_ref.shape[1]
    f32 = jnp.float32

    e_sub = lax.broadcasted_iota(jnp.int32, (N_EXPERTS, tb), 0)
    oh1 = e_sub == ri_ref[0:1, :]
    oh2 = e_sub == ri_ref[1:2, :]
    oh = jnp.where(oh1 | oh2, 1.0, 0.0).astype(f32)
    blk_count = jnp.sum(oh, axis=1, keepdims=True)

    @pl.when((phase == 0) & (blk == 0))
    def _():
        cnt_ref[...] = jnp.zeros_like(cnt_ref)

    @pl.when(phase == 0)
    def _():
        cnt_ref[...] = cnt_ref[...] + blk_count

    @pl.when((phase == 1) & (blk == 0))
    def _():
        counts = cnt_ref[:, 0:1]
        padded = jnp.floor((counts + (EXPERT_ROWS - 1)) * (1.0 / EXPERT_ROWS)) * EXPERT_ROWS
        er = lax.broadcasted_iota(jnp.int32, (N_EXPERTS, LANES), 0)
        ec = lax.broadcasted_iota(jnp.int32, (N_EXPERTS, LANES), 1)
        padded_row = jnp.sum(jnp.where(er == ec, padded, 0.0), axis=0, keepdims=True)
        counts_row = jnp.sum(jnp.where(er == ec, counts, 0.0), axis=0, keepdims=True)
        pad_end = jnp.sum(jnp.where(ec <= er, padded_row, 0.0), axis=1, keepdims=True)
        pad_end_row = jnp.sum(jnp.where(er == ec, pad_end, 0.0), axis=0, keepdims=True)
        start_ref[...] = jnp.broadcast_to(pad_end - padded, start_ref.shape)
        run_ref[...] = jnp.zeros_like(run_ref)
        bstart = (lax.broadcasted_iota(jnp.int32, (N_EXPERTS, META_LANES), 1) * EXPERT_ROWS).astype(f32)
        blk_e = jnp.sum(jnp.where(pad_end <= bstart, 1.0, 0.0), axis=0, keepdims=True)
        blk_e = jnp.minimum(blk_e, N_EXPERTS - 1.0)
        n_used = pad_end[N_EXPERTS - 1:N_EXPERTS, :] * (1.0 / EXPERT_ROWS)
        pad3 = lambda r: jnp.concatenate(
            [r, jnp.zeros((1, META_LANES - LANES), f32)], axis=1)
        msub = lax.broadcasted_iota(jnp.int32, (SUBLANES, META_LANES), 0)
        meta = jnp.where(msub == 0, blk_e,
                         jnp.where(msub == 1, pad3(pad_end_row),
                                   jnp.where(msub == 2, pad3(counts_row),
                                             jnp.where(msub == 3, n_used, 0.0))))
        meta_ref[...] = meta.astype(jnp.int32)

    @pl.when(phase == 1)
    def _():
        tr = lax.broadcasted_iota(jnp.int32, (tb, tb), 0)
        tc = lax.broadcasted_iota(jnp.int32, (tb, tb), 1)
        upper = jnp.where(tr < tc, 1.0, 0.0).astype(jnp.bfloat16)
        prefix = jnp.dot(oh.astype(jnp.bfloat16), upper, preferred_element_type=f32)
        er = lax.broadcasted_iota(jnp.int32, (N_EXPERTS, LANES), 0)
        ec = lax.broadcasted_iota(jnp.int32, (N_EXPERTS, LANES), 1)
        to_row = lambda col: jnp.sum(jnp.where(er == ec, col, 0.0), axis=0, keepdims=True)
        cnt_row = to_row(blk_count)
        lstart = jnp.sum(jnp.where(ec < er, cnt_row, 0.0), axis=1, keepdims=True)
        base = prefix + lstart
        d1 = jnp.sum(jnp.where(oh1, base, 0.0), axis=0, keepdims=True)
        d2 = jnp.sum(jnp.where(oh2, base, 0.0), axis=0, keepdims=True)
        sub = lax.broadcasted_iota(jnp.int32, (SUBLANES, tb), 0)
        ldest_ref[...] = jnp.where(sub == 0, d1, jnp.where(sub == 1, d2, 0.0)).astype(jnp.int32)
        gstart = start_ref[:, 0:1] + run_ref[:, 0:1]
        rsub = lax.broadcasted_iota(jnp.int32, (SUBLANES, LANES), 0)
        runs = jnp.where(rsub == 0, cnt_row,
                         jnp.where(rsub == 1, to_row(lstart), jnp.where(rsub == 2, to_row(gstart), 0.0)))
        runs_ref[...] = runs.astype(jnp.int32)
        run_ref[...] = run_ref[...] + blk_count


def _plan(ri):
    tb = MOE_TOKENS
    return pl.pallas_call(
        _plan_kernel,
        grid=(2, N_TOK // tb),
        in_specs=[pl.BlockSpec((SUBLANES, tb), lambda p, i: (0, i))],
        out_specs=[pl.BlockSpec((SUBLANES, tb), lambda p, i: (0, i * p)),
                   pl.BlockSpec((SUBLANES, LANES), lambda p, i: (i * p, 0)),
                   pl.BlockSpec((SUBLANES, META_LANES), lambda p, i: (0, 0))],
        out_shape=[jax.ShapeDtypeStruct((SUBLANES, N_TOK), jnp.int32),
                   jax.ShapeDtypeStruct((N_MOE_BLOCKS * SUBLANES, LANES), jnp.int32),
                   jax.ShapeDtypeStruct((SUBLANES, META_LANES), jnp.int32)],
        scratch_shapes=[pltpu.VMEM((N_EXPERTS, LANES), jnp.float32),
                        pltpu.VMEM((N_EXPERTS, LANES), jnp.float32),
                        pltpu.VMEM((N_EXPERTS, LANES), jnp.float32)],
        compiler_params=pltpu.CompilerParams(dimension_semantics=("arbitrary", "arbitrary")),
        name="plan",
    )(ri)


def _for_each_run_piece(runs_ref, fn):
    for e in range(N_EXPERTS):
        n, lstart, gstart = runs_ref[0, e], runs_ref[1, e], runs_ref[2, e]
        for bit in range(RUN_BITS):
            @pl.when((n & (1 << bit)) != 0)
            def _(n=n, lstart=lstart, gstart=gstart, bit=bit):
                off = (n >> (bit + 1)) << (bit + 1)
                fn(lstart + off, gstart + off, 1 << bit)


def _tile_rows(ref, row, rows):
    return ref.at[pl.ds(pl.multiple_of(row * PACK_TILES, PACK_TILES), rows * PACK_TILES)]


def _dispatch_kernel(meta_ref, runs_ref, ldest_ref, h1r_ref, buf_ref, sorted_ref, zero_ref, sems, zsem):
    i = pl.program_id(0)
    n_steps = pl.num_programs(0)
    tb = ldest_ref.shape[1]
    slot = i % 2
    block_tiles = 2 * tb * PACK_TILES

    def wait_slot(s):
        pltpu.make_async_copy(sorted_ref.at[s], buf_ref.at[pl.ds(0, block_tiles)], sems.at[s]).wait()

    @pl.when(i == 0)
    def _():
        zero_ref[...] = jnp.zeros_like(zero_ref)

        def zero_copy(e):
            start = pl.multiple_of((meta_ref[1, e] - EXPERT_ROWS) * PACK_TILES, EXPERT_ROWS * PACK_TILES)
            return pltpu.make_async_copy(
                zero_ref, buf_ref.at[pl.ds(start, EXPERT_ROWS * PACK_TILES)], zsem)

        def start_zero(e, c):
            @pl.when(meta_ref[2, e] > 0)
            def _():
                zero_copy(e).start()
            return c

        def wait_zero(e, c):
            @pl.when(meta_ref[2, e] > 0)
            def _():
                zero_copy(e).wait()
            return c

        def tail_copy(b):
            start = pl.multiple_of(b * (EXPERT_ROWS * PACK_TILES), EXPERT_ROWS * PACK_TILES)
            return pltpu.make_async_copy(
                zero_ref, buf_ref.at[pl.ds(start, EXPERT_ROWS * PACK_TILES)], zsem)

        def start_tail(b, c):
            tail_copy(b).start()
            return c

        def wait_tail(b, c):
            tail_copy(b).wait()
            return c

        lax.fori_loop(0, N_EXPERTS, start_zero, 0)
        lax.fori_loop(meta_ref[3, 0], N_ROW_BLOCKS, start_tail, 0)
        lax.fori_loop(0, N_EXPERTS, wait_zero, 0)
        lax.fori_loop(meta_ref[3, 0], N_ROW_BLOCKS, wait_tail, 0)

    @pl.when(i >= 2)
    def _():
        wait_slot(slot)

    x = _from_row_tiles(h1r_ref, tb).astype(jnp.bfloat16)
    ld0 = ldest_ref[0:1, :]
    ld1 = ldest_ref[1:2, :]
    for c in range(2 * tb // SORT_CHUNK):
        r = lax.broadcasted_iota(jnp.int32, (SORT_CHUNK, tb), 0) + c * SORT_CHUNK
        perm = jnp.where((r == ld0) | (r == ld1), 1.0, 0.0).astype(jnp.bfloat16)
        rows = jnp.dot(perm, x, preferred_element_type=jnp.float32)
        _pack_rows(sorted_ref.at[slot, pl.ds(c * SORT_CHUNK * PACK_TILES, SORT_CHUNK * PACK_TILES)], rows)

    def send(lrow, grow, rows):
        pltpu.make_async_copy(_tile_rows(sorted_ref.at[slot], lrow, rows),
                              _tile_rows(buf_ref, grow, rows), sems.at[slot]).start()

    _for_each_run_piece(runs_ref, send)

    @pl.when(i == n_steps - 1)
    def _():
        wait_slot(slot)
        wait_slot(1 - slot)


def _dispatch(meta, runs, ldest, h1r):
    tb = MOE_TOKENS
    return pl.pallas_call(
        _dispatch_kernel,
        grid_spec=pltpu.PrefetchScalarGridSpec(
            num_scalar_prefetch=1,
            grid=(N_MOE_BLOCKS,),
            in_specs=[pl.BlockSpec((SUBLANES, LANES), lambda i, m: (i, 0), memory_space=pltpu.SMEM),
                      pl.BlockSpec((SUBLANES, tb), lambda i, m: (0, i)),
                      pl.BlockSpec((tb * FEAT_TILES, LANES), lambda i, m: (i, 0))],
            out_specs=pl.BlockSpec(memory_space=pl.ANY),
            scratch_shapes=[pltpu.VMEM((2, 2 * tb * PACK_TILES, LANES), jnp.uint32),
                            pltpu.VMEM((EXPERT_ROWS * PACK_TILES, LANES), jnp.uint32),
                            pltpu.SemaphoreType.DMA((2,)), pltpu.SemaphoreType.DMA]),
        out_shape=jax.ShapeDtypeStruct((N_ROWS * PACK_TILES, LANES), jnp.uint32),
        compiler_params=pltpu.CompilerParams(
            dimension_semantics=("arbitrary",), vmem_limit_bytes=VMEM_LIMIT),
        name="dispatch",
    )(meta, runs, ldest, h1r)


def _experts_kernel(meta_ref, x_ref, wg_ref, wu_ref, wd_ref, o_ref):
    n = EXPERT_ROWS

    @pl.when(pl.program_id(0) < meta_ref[3, 0])
    def _():
        x = _unpack_rows(x_ref, n).astype(jnp.bfloat16)
        gate = jnp.dot(x, wg_ref[...], preferred_element_type=jnp.float32)
        up = jnp.dot(x, wu_ref[...], preferred_element_type=jnp.float32)
        act = (gate * jax.nn.sigmoid(gate) * up).astype(jnp.bfloat16)
        out = jnp.dot(act, wd_ref[...], preferred_element_type=jnp.float32)
        _pack_rows(o_ref, out.astype(jnp.bfloat16).astype(jnp.float32))

    @pl.when(pl.program_id(0) >= meta_ref[3, 0])
    def _():
        o_ref[...] = jnp.zeros_like(o_ref)


def _experts(meta, buf, wg, wu, wd):
    used = lambda i, m: jnp.minimum(i, m[3, 0] - 1)
    return pl.pallas_call(
        _experts_kernel,
        grid_spec=pltpu.PrefetchScalarGridSpec(
            num_scalar_prefetch=1,
            grid=(N_ROW_BLOCKS,),
            in_specs=[pl.BlockSpec((EXPERT_ROWS * PACK_TILES, LANES), lambda i, m: (used(i, m), 0)),
                      pl.BlockSpec((None, D_MODEL, EXPERT_FF), lambda i, m: (m[0, used(i, m)], 0, 0)),
                      pl.BlockSpec((None, D_MODEL, EXPERT_FF), lambda i, m: (m[0, used(i, m)], 0, 0)),
                      pl.BlockSpec((None, EXPERT_FF, D_MODEL), lambda i, m: (m[0, used(i, m)], 0, 0))],
            out_specs=pl.BlockSpec((EXPERT_ROWS * PACK_TILES, LANES), lambda i, m: (i, 0))),
        out_shape=jax.ShapeDtypeStruct((N_ROWS * PACK_TILES, LANES), jnp.uint32),
        compiler_params=pltpu.CompilerParams(
            dimension_semantics=("arbitrary",), vmem_limit_bytes=VMEM_LIMIT),
        name="experts",
    )(meta, buf, wg, wu, wd)


def _combine_kernel(runs_ref, runs_next_ref, ldest_ref, rf_ref, h1r_ref, eout_ref, ln2g_ref, ln2b_ref,
                    o_ref, y_ref, sems):
    i = pl.program_id(0)
    n_steps = pl.num_programs(0)
    tb = ldest_ref.shape[1]
    slot = i % 2
    block_tiles = 2 * tb * PACK_TILES

    def fetch(table_ref, s):
        def recv(lrow, grow, rows):
            pltpu.make_async_copy(_tile_rows(eout_ref, grow, rows),
                                  _tile_rows(y_ref.at[s], lrow, rows), sems.at[s]).start()
        _for_each_run_piece(table_ref, recv)

    @pl.when(i == 0)
    def _():
        fetch(runs_ref, slot)

    @pl.when(i + 1 < n_steps)
    def _():
        fetch(runs_next_ref, 1 - slot)

    pltpu.make_async_copy(eout_ref.at[pl.ds(0, block_tiles)], y_ref.at[slot], sems.at[slot]).wait()

    ld = ldest_ref[...].astype(jnp.float32).T
    gates = rf_ref[...].T
    y = None
    for c in range(2 * tb // SORT_CHUNK):
        col = (lax.broadcasted_iota(jnp.int32, (tb, SORT_CHUNK), 1) + c * SORT_CHUNK).astype(jnp.float32)
        g = (jnp.where(col == ld[:, 0:1], gates[:, 0:1], 0.0)
             + jnp.where(col == ld[:, 1:2], gates[:, 1:2], 0.0)).astype(jnp.bfloat16)
        rows = _unpack_rows(
            y_ref.at[slot, pl.ds(c * SORT_CHUNK * PACK_TILES, SORT_CHUNK * PACK_TILES)], SORT_CHUNK)
        part = jnp.dot(g, rows.astype(jnp.bfloat16), preferred_element_type=jnp.float32)
        y = part if y is None else y + part
    h1 = _from_row_tiles(h1r_ref, tb)
    o_ref[...] = _layer_norm(ALPHA * h1 + y, ln2g_ref[...], ln2b_ref[...])


def _combine(runs, ldest, rf, h1r, eout, ln2g, ln2b):
    tb = MOE_TOKENS
    last = N_MOE_BLOCKS - 1
    return pl.pallas_call(
        _combine_kernel,
        grid=(N_MOE_BLOCKS,),
        in_specs=[pl.BlockSpec((SUBLANES, LANES), lambda i: (i, 0), memory_space=pltpu.SMEM),
                  pl.BlockSpec((SUBLANES, LANES), lambda i: (jnp.minimum(i + 1, last), 0),
                               memory_space=pltpu.SMEM),
                  pl.BlockSpec((SUBLANES, tb), lambda i: (0, i)),
                  pl.BlockSpec((SUBLANES, tb), lambda i: (0, i)),
                  pl.BlockSpec((tb * FEAT_TILES, LANES), lambda i: (i, 0)),
                  pl.BlockSpec(memory_space=pl.ANY),
                  pl.BlockSpec((1, D_MODEL), lambda i: (0, 0)),
                  pl.BlockSpec((1, D_MODEL), lambda i: (0, 0))],
        out_specs=pl.BlockSpec((tb, D_MODEL), lambda i: (i, 0)),
        out_shape=jax.ShapeDtypeStruct((N_TOK, D_MODEL), jnp.float32),
        scratch_shapes=[pltpu.VMEM((2, 2 * tb * PACK_TILES, LANES), jnp.uint32),
                        pltpu.SemaphoreType.DMA((2,))],
        compiler_params=pltpu.CompilerParams(
            dimension_semantics=("arbitrary",), vmem_limit_bytes=VMEM_LIMIT),
        name="combine",
    )(runs, runs, ldest, rf, h1r, eout, ln2g, ln2b)


def _swap_halves(w):
    half = w.shape[-1] // 2
    return jnp.concatenate([w[..., half:], w[..., :half]], axis=-1)


def kernel(x, positions, ln0_g, ln0_b, w_in, gm_ln_g, gm_ln_b, w_spatial, b_spatial, q_norm_g, w_uq, kv_norm_g, w_ukv, w_out, ln1_g, ln1_b, w_router_group, b_router_group, w_router_expert, b_router_expert, w_gate, w_up, w_down, ln2_g, ln2_b):
    bf16 = jnp.bfloat16
    row = lambda a: a.reshape(1, -1)

    w_in0 = w_in[0]
    kr_cols = w_in0[:, O_KR:O_KR + QK_ROPE_DIM]
    win = jnp.concatenate([w_in0, _swap_halves(kr_cols)], axis=1).astype(bf16)
    wuq3 = w_uq[0].reshape(Q_LORA_RANK, MLA_HEADS, QK_NOPE_DIM + QK_ROPE_DIM)
    rope_cols = wuq3[:, :, QK_NOPE_DIM:]
    wuq = jnp.concatenate([wuq3, _swap_halves(rope_cols)], axis=-1).reshape(Q_LORA_RANK, D_MODEL).astype(bf16)
    wukv = w_ukv[0].astype(bf16)
    wout = w_out[0].astype(bf16)
    bs = jnp.broadcast_to(b_spatial[0][:, :, None], (GM_HEADS, GM_CHUNK, GM_HEAD_DIM))
    wr = jnp.concatenate([w_router_group[0], jnp.zeros((D_MODEL, SUBLANES - N_GROUPS), jnp.float32),
                          w_router_expert[0],
                          jnp.zeros((D_MODEL, LANES - ROUTER_ROWS), jnp.float32)], axis=1)
    br = jnp.concatenate([b_router_group[0], jnp.zeros((SUBLANES - N_GROUPS,), jnp.float32),
                          b_router_expert[0]]).reshape(ROUTER_ROWS, 1)
    wg = w_gate[0].astype(bf16)
    wu = w_up[0].astype(bf16)
    wd = w_down[0].astype(bf16)

    inv_freq = ROPE_THETA ** (-jnp.arange(0, QK_ROPE_DIM, 2, dtype=jnp.float32) / QK_ROPE_DIM)
    freq = jnp.tile(inv_freq, 4).reshape(1, LANES)
    quarter = QK_ROPE_DIM // 2
    phase = jnp.concatenate([jnp.zeros((2 * quarter,), jnp.float32),
                             jnp.full((2 * quarter,), math.pi / 2, jnp.float32)]).reshape(1, LANES)
    sign = jnp.concatenate([jnp.ones((2 * quarter,), jnp.float32), -jnp.ones((quarter,), jnp.float32),
                            jnp.ones((quarter,), jnp.float32)]).reshape(1, LANES)

    x2 = x.reshape(N_TOK, D_MODEL)
    pos2 = positions.reshape(N_TOK, 1)

    h, outa, q, k, v = _prep(x2, pos2, row(ln0_g), row(ln0_b), win, row(gm_ln_g[0]), row(gm_ln_b[0]),
                             w_spatial[0], bs, row(q_norm_g[0]), wuq, row(kv_norm_g[0]), wukv,
                             freq, phase, sign)
    h1r, ri, rf = _attn(q, k, v, outa, h, wout, row(ln1_g[0]), row(ln1_b[0]), wr, br)
    ldest, runs, meta = _plan(ri)
    buf = _dispatch(meta, runs, ldest, h1r)
    eout = _experts(meta, buf, wg, wu, wd)
    out = _combine(runs, ldest, rf, h1r, eout, row(ln2_g[0]), row(ln2_b[0]))
    return out.reshape(BATCH, SEQ, D_MODEL)
```

```python
import functools
import math

import jax
import jax.numpy as jnp
from jax import lax
from jax.experimental import pallas as pl
from jax.experimental.pallas import tpu as pltpu

D_MODEL = 1024
BATCH = 16
SEQ = 2048
N_TOK = BATCH * SEQ
CHUNK = 64
GM_WIDTH = 512
GM_HEADS = 4
GM_HEAD_DIM = 128
GM_CHUNK = 128
MLA_HEADS = 4
QK_NOPE_DIM = 128
QK_ROPE_DIM = 64
V_HEAD_DIM = 128
Q_LORA_RANK = 384
KV_LORA_RANK = 256
ROPE_THETA = 10000.0
N_GROUPS = 4
EXPERTS_PER_GROUP = 8
N_EXPERTS = 32
TOP_K = 2
EXPERT_FF = 256
ALPHA = 2.0 ** 0.25
QK_SCALE = (QK_NOPE_DIM + QK_ROPE_DIM) ** -0.5 * math.log2(math.e)

LANES = 128
SUBLANES = 8
FEAT_TILES = D_MODEL // LANES
PREP_TOKENS = 512
ATT_Q = 512
ATT_K = 512
N_QBLK = SEQ // ATT_Q
PROJ_TOKENS = 512
MOE_TOKENS = 512
N_MOE_BLOCKS = N_TOK // MOE_TOKENS
RUN_BITS = (TOP_K * MOE_TOKENS).bit_length()
SORT_CHUNK = 256
EXPERT_ROWS = 256
N_ROWS = N_TOK * TOP_K + N_EXPERTS * EXPERT_ROWS
N_ROW_BLOCKS = N_ROWS // EXPERT_ROWS
META_LANES = 384
IN_COLS = 2 * GM_WIDTH + Q_LORA_RANK + KV_LORA_RANK + 2 * QK_ROPE_DIM
O_Q = 2 * GM_WIDTH
O_KV = O_Q + Q_LORA_RANK
O_KR = O_KV + KV_LORA_RANK
ROUTER_ROWS = 40
VMEM_LIMIT = 48 * 1024 * 1024

assert N_ROW_BLOCKS <= META_LANES


def _layer_norm(x, g, b, eps=1e-5):
    mu = jnp.mean(x, axis=-1, keepdims=True)
    xc = x - mu
    var = jnp.mean(xc * xc, axis=-1, keepdims=True)
    return xc * lax.rsqrt(var + eps) * g + b


def _rms_norm(x, g, eps=1e-6):
    return x * lax.rsqrt(jnp.mean(x * x, axis=-1, keepdims=True) + eps) * g


def _gelu_tanh(x):
    c = math.sqrt(2.0 / math.pi)
    return 0.5 * x * (1.0 + jnp.tanh(c * (x + 0.044715 * (x * x * x))))


def _to_row_tiles(ref, x):
    n = x.shape[0]
    for s in range(FEAT_TILES):
        ref[pl.ds(s, n, stride=FEAT_TILES), :] = x[:, s * LANES:(s + 1) * LANES]


def _from_row_tiles(ref, n):
    return jnp.concatenate(
        [ref[pl.ds(s, n, stride=FEAT_TILES), :] for s in range(FEAT_TILES)], axis=-1)


def _prep_kernel(x_ref, pos_ref, ln0g_ref, ln0b_ref, win_ref, gmg_ref, gmb_ref, ws_ref, bs_ref,
                 qg_ref, wuq_ref, kvg_ref, wukv_ref, freq_ref, phase_ref, sign_ref,
                 h_ref, outa_ref, q_ref, k_ref, v_ref, tabc_ref, tabs_ref, rot_ref):
    tb = x_ref.shape[0]

    @pl.when(pl.program_id(0) == 0)
    def _():
        d = lax.broadcasted_iota(jnp.int32, (tb, LANES), 0).astype(jnp.float32) * freq_ref[...]
        tabc_ref[...] = jnp.cos(d)
        tabs_ref[...] = jnp.sin(d)

    pos = pos_ref[...]
    p0 = pos_ref[0:1, :]
    offset = lax.broadcasted_iota(jnp.int32, (tb, 1), 0)
    consecutive = jnp.max(jnp.abs((pos - p0 - offset).astype(jnp.float32))) == 0.0

    @pl.when(consecutive)
    def _():
        a0 = p0.astype(jnp.float32) * freq_ref[...]
        c0, s0 = jnp.cos(a0), jnp.sin(a0)
        lane = lax.broadcasted_iota(jnp.int32, (1, LANES), 1)
        coef_c = jnp.where(lane < 2 * 32, c0, jnp.where(lane < 3 * 32, -s0, s0))
        coef_s = jnp.where(lane < 2 * 32, -s0, jnp.where(lane < 3 * 32, -c0, c0))
        rot_ref[...] = coef_c * tabc_ref[...] + coef_s * tabs_ref[...]

    @pl.when(jnp.logical_not(consecutive))
    def _():
        ang = pos.astype(jnp.float32) * freq_ref[...]
        rot_ref[...] = jnp.cos(ang - phase_ref[...]) * sign_ref[...]

    rot = rot_ref[...]

    h = _layer_norm(x_ref[...], ln0g_ref[...], ln0b_ref[...])
    h_ref[...] = h
    z = jnp.dot(h.astype(jnp.bfloat16), win_ref[...], preferred_element_type=jnp.float32)

    u = _gelu_tanh(z[:, :GM_WIDTH])
    v = _gelu_tanh(z[:, GM_WIDTH:2 * GM_WIDTH])
    row_chunk = lax.broadcasted_iota(jnp.int32, (GM_CHUNK, GM_CHUNK), 0) // CHUNK
    col_chunk = lax.broadcasted_iota(jnp.int32, (GM_CHUNK, GM_CHUNK), 1) // CHUNK
    allowed = col_chunk <= row_chunk
    for hd in range(GM_HEADS):
        lo, hi = hd * GM_HEAD_DIM, (hd + 1) * GM_HEAD_DIM
        vln = _layer_norm(v[:, lo:hi], gmg_ref[:, lo:hi], gmb_ref[:, lo:hi]).astype(jnp.bfloat16)
        wm = jnp.where(allowed, ws_ref[hd], 0.0).astype(jnp.bfloat16)
        for c in range(tb // GM_CHUNK):
            r0, r1 = c * GM_CHUNK, (c + 1) * GM_CHUNK
            f = jnp.dot(wm, vln[r0:r1], preferred_element_type=jnp.float32) + bs_ref[hd]
            outa_ref[r0:r1, lo:hi] = (u[r0:r1, lo:hi] * f).astype(jnp.bfloat16)

    ql = _rms_norm(z[:, O_Q:O_KV], qg_ref[...]).astype(jnp.bfloat16)
    qf = jnp.dot(ql, wuq_ref[...], preferred_element_type=jnp.float32)
    rot_s = rot * QK_SCALE
    q_parts = []
    for hd in range(MLA_HEADS):
        base = hd * 2 * LANES
        q_parts.append(qf[:, base:base + LANES] * QK_SCALE)
        q_parts.append(qf[:, base + LANES:base + 2 * LANES] * rot_s)
    q_ref[...] = jnp.concatenate(q_parts, axis=-1).astype(jnp.bfloat16)

    kvl = _rms_norm(z[:, O_KV:O_KR], kvg_ref[...]).astype(jnp.bfloat16)
    kv = jnp.dot(kvl, wukv_ref[...], preferred_element_type=jnp.float32)
    t = z[:, O_KR:O_KR + LANES] * rot
    krr = t + pltpu.roll(t, 2 * QK_ROPE_DIM // 2, axis=1)
    k_parts, v_parts = [], []
    for hd in range(MLA_HEADS):
        base = hd * 2 * LANES
        k_parts.append(kv[:, base:base + LANES])
        k_parts.append(krr)
        v_parts.append(kv[:, base + LANES:base + 2 * LANES])
    k_ref[...] = jnp.concatenate(k_parts, axis=-1).astype(jnp.bfloat16)
    v_ref[...] = jnp.concatenate(v_parts, axis=-1).astype(jnp.bfloat16)


def _prep(x2, pos2, ln0g, ln0b, win, gmg, gmb, ws, bs, qg, wuq, kvg, wukv, freq, phase, sign):
    tb = PREP_TOKENS
    full = lambda shape: pl.BlockSpec(shape, lambda i: (0,) * len(shape))
    tok = lambda cols: pl.BlockSpec((tb, cols), lambda i: (i, 0))
    return pl.pallas_call(
        _prep_kernel,
        grid=(N_TOK // tb,),
        in_specs=[tok(D_MODEL), tok(1), full((1, D_MODEL)), full((1, D_MODEL)),
                  full((D_MODEL, IN_COLS)), full((1, GM_WIDTH)), full((1, GM_WIDTH)),
                  full((GM_HEADS, GM_CHUNK, GM_CHUNK)), full((GM_HEADS, GM_CHUNK, GM_HEAD_DIM)),
                  full((1, Q_LORA_RANK)), full((Q_LORA_RANK, D_MODEL)),
                  full((1, KV_LORA_RANK)), full((KV_LORA_RANK, D_MODEL)),
                  full((1, LANES)), full((1, LANES)), full((1, LANES))],
        out_specs=[tok(D_MODEL), tok(GM_WIDTH), tok(D_MODEL), tok(D_MODEL), tok(GM_WIDTH)],
        out_shape=[jax.ShapeDtypeStruct((N_TOK, D_MODEL), jnp.float32),
                   jax.ShapeDtypeStruct((N_TOK, GM_WIDTH), jnp.bfloat16),
                   jax.ShapeDtypeStruct((N_TOK, D_MODEL), jnp.bfloat16),
                   jax.ShapeDtypeStruct((N_TOK, D_MODEL), jnp.bfloat16),
                   jax.ShapeDtypeStruct((N_TOK, GM_WIDTH), jnp.bfloat16)],
        scratch_shapes=[pltpu.VMEM((tb, LANES), jnp.float32)] * 3,
        compiler_params=pltpu.CompilerParams(
            dimension_semantics=("arbitrary",), vmem_limit_bytes=VMEM_LIMIT),
        name="prep",
    )(x2, pos2, ln0g, ln0b, win, gmg, gmb, ws, bs, qg, wuq, kvg, wukv, freq, phase, sign)


def _attn_kernel(q_ref, k_ref, v_ref, o_ref, s_ref, mx_ref, ls_ref, acc_ref):
    qi = pl.program_id(1)
    tq = q_ref.shape[0]
    nt = (((1,), (1,)), ((), ()))
    n_kv = SEQ // ATT_K

    row_chunk = (lax.broadcasted_iota(jnp.int32, (tq, ATT_K), 0)) // CHUNK
    col_chunk = (lax.broadcasted_iota(jnp.int32, (tq, ATT_K), 1)) // CHUNK
    diag_allowed = col_chunk <= row_chunk

    def lane_tiles(x):
        return [x[:, t * LANES:(t + 1) * LANES] for t in range(x.shape[1] // LANES)]

    def lane_fold(x, op):
        return functools.reduce(op, lane_tiles(x))

    def scores(hd, j):
        start = pl.multiple_of(j * ATT_K, ATT_K)
        q_h = q_ref[:, hd * 2 * LANES:(hd + 1) * 2 * LANES]
        kb = k_ref[pl.ds(start, ATT_K), hd * 2 * LANES:(hd + 1) * 2 * LANES]
        return lax.dot_general(q_h, kb, nt, preferred_element_type=jnp.float32)

    for hd in range(MLA_HEADS):
        s = jnp.where(diag_allowed, scores(hd, qi), jnp.float32(-1e30))
        s_ref[hd * n_kv + qi] = s
        mx_ref[hd] = lane_fold(s, jnp.maximum)

    def pass_a(j, c):
        for hd in range(MLA_HEADS):
            s = scores(hd, j)
            s_ref[hd * n_kv + j] = s
            mx_ref[hd] = jnp.maximum(mx_ref[hd], lane_fold(s, jnp.maximum))
        return c

    lax.fori_loop(0, qi, pass_a, 0)

    for hd in range(MLA_HEADS):
        mx_ref[hd] = jnp.broadcast_to(jnp.max(mx_ref[hd], axis=-1, keepdims=True), (tq, LANES))

    def probs(hd, j):
        s = s_ref[hd * n_kv + j]
        mb = mx_ref[hd]
        p = jnp.exp2(jnp.concatenate([t - mb for t in lane_tiles(s)], axis=-1))
        start = pl.multiple_of(j * ATT_K, ATT_K)
        vb = v_ref[pl.ds(start, ATT_K), hd * LANES:(hd + 1) * LANES]
        pv = jnp.dot(p.astype(jnp.bfloat16), vb, preferred_element_type=jnp.float32)
        return lane_fold(p, jnp.add), pv

    for hd in range(MLA_HEADS):
        ls, pv = probs(hd, qi)
        ls_ref[hd] = ls
        acc_ref[hd] = pv

    def pass_b(j, c):
        for hd in range(MLA_HEADS):
            ls, pv = probs(hd, j)
            ls_ref[hd] = ls_ref[hd] + ls
            acc_ref[hd] = acc_ref[hd] + pv
        return c

    lax.fori_loop(0, qi, pass_b, 0)

    for hd in range(MLA_HEADS):
        l = jnp.sum(ls_ref[hd], axis=-1, keepdims=True)
        o_ref[:, hd * V_HEAD_DIM:(hd + 1) * V_HEAD_DIM] = (acc_ref[hd] / l).astype(jnp.bfloat16)


def _attn(q, k, v):
    tq = ATT_Q
    tokblk = lambda cols: pl.BlockSpec((tq, cols), lambda b, i: (b * N_QBLK + i, 0))
    seqblk = lambda cols: pl.BlockSpec((SEQ, cols), lambda b, i: (b, 0))
    return pl.pallas_call(
        _attn_kernel,
        grid=(BATCH, N_QBLK),
        in_specs=[tokblk(D_MODEL), seqblk(D_MODEL), seqblk(GM_WIDTH)],
        out_specs=tokblk(GM_WIDTH),
        out_shape=jax.ShapeDtypeStruct((N_TOK, MLA_HEADS * V_HEAD_DIM), jnp.bfloat16),
        scratch_shapes=[pltpu.VMEM((MLA_HEADS * (SEQ // ATT_K), tq, ATT_K), jnp.float32),
                        pltpu.VMEM((MLA_HEADS, tq, LANES), jnp.float32),
                        pltpu.VMEM((MLA_HEADS, tq, LANES), jnp.float32),
                        pltpu.VMEM((MLA_HEADS, tq, V_HEAD_DIM), jnp.float32)],
        compiler_params=pltpu.CompilerParams(
            dimension_semantics=("arbitrary", "arbitrary"), vmem_limit_bytes=VMEM_LIMIT),
        name="attn",
    )(q, k, v)


def _proj_kernel(outa_ref, ob_ref, h_ref, wout_ref, ln1g_ref, ln1b_ref, wr_ref, br_ref,
                 h1r_ref, ri_ref, rf_ref, proj_ref):
    i = pl.program_id(0)
    tb = outa_ref.shape[0]

    @pl.when(i == 0)
    def _():
        proj_ref[...] = jnp.zeros_like(proj_ref)

    h1 = _layer_norm(ALPHA * h_ref[...] + proj_ref[(i + 1) % 2], ln1g_ref[...], ln1b_ref[...])
    _to_row_tiles(h1r_ref, h1)

    h_hi = h1.astype(jnp.bfloat16)
    h_lo = (h1 - h_hi.astype(jnp.float32)).astype(jnp.bfloat16)
    w = wr_ref[...]
    w_hi = w.astype(jnp.bfloat16)
    w_lo = (w - w_hi.astype(jnp.float32)).astype(jnp.bfloat16)
    logits_tm = (jnp.dot(h_hi, w_hi, preferred_element_type=jnp.float32)
                 + jnp.dot(h_hi, w_lo, preferred_element_type=jnp.float32)
                 + jnp.dot(h_lo, w_hi, preferred_element_type=jnp.float32))
    logits = logits_tm.T[0:ROUTER_ROWS] + br_ref[...]

    sub_i = lax.broadcasted_iota(jnp.int32, (SUBLANES, tb), 0)
    sub = sub_i.astype(jnp.float32)
    neg = jnp.float32(-jnp.inf)
    g = jnp.where(sub_i < N_GROUPS, logits[0:SUBLANES], neg)
    gmax = jnp.max(g, axis=0, keepdims=True)
    g_top = jnp.min(jnp.where(g == gmax, sub, float(SUBLANES)), axis=0, keepdims=True)
    p_group = 1.0 / jnp.sum(jnp.exp(g - gmax), axis=0, keepdims=True)
    sel = logits[SUBLANES:2 * SUBLANES]
    for grp in range(1, N_GROUPS):
        sel = jnp.where(g_top == float(grp), logits[(grp + 1) * SUBLANES:(grp + 2) * SUBLANES], sel)
    v1 = jnp.max(sel, axis=0, keepdims=True)
    i1 = jnp.min(jnp.where(sel == v1, sub, float(SUBLANES)), axis=0, keepdims=True)
    sel2 = jnp.where(sub == i1, neg, sel)
    v2 = jnp.max(sel2, axis=0, keepdims=True)
    i2 = jnp.min(jnp.where(sel2 == v2, sub, float(SUBLANES)), axis=0, keepdims=True)
    e21 = jnp.exp(v2 - v1)
    w1 = 1.0 / (1.0 + e21)
    gate1 = p_group * w1
    gate2 = p_group * (e21 * w1)
    e1 = g_top * EXPERTS_PER_GROUP + i1
    e2 = g_top * EXPERTS_PER_GROUP + i2
    ri_ref[...] = jnp.where(sub_i == 0, e1, jnp.where(sub_i == 1, e2, 0.0)).astype(jnp.int32)
    rf_ref[...] = jnp.where(sub_i == 0, gate1, jnp.where(sub_i == 1, gate2, 0.0))

    proj_ref[i % 2] = (jnp.dot(outa_ref[...], wout_ref[:GM_WIDTH, :], preferred_element_type=jnp.float32)
                       + jnp.dot(ob_ref[...], wout_ref[GM_WIDTH:, :], preferred_element_type=jnp.float32))


def _proj(outa, ob, h, wout, ln1g, ln1b, wr, br):
    tb = PROJ_TOKENS
    n_blk = N_TOK // tb
    cur = lambda i: jnp.minimum(i, n_blk - 1)
    prev = lambda i: jnp.maximum(i - 1, 0)
    full = lambda shape: pl.BlockSpec(shape, lambda i: (0,) * len(shape))
    return pl.pallas_call(
        _proj_kernel,
        grid=(n_blk + 1,),
        in_specs=[pl.BlockSpec((tb, GM_WIDTH), lambda i: (cur(i), 0)),
                  pl.BlockSpec((tb, GM_WIDTH), lambda i: (cur(i), 0)),
                  pl.BlockSpec((tb, D_MODEL), lambda i: (prev(i), 0)),
                  full((D_MODEL, D_MODEL)), full((1, D_MODEL)), full((1, D_MODEL)),
                  full((D_MODEL, LANES)), full((ROUTER_ROWS, 1))],
        out_specs=[pl.BlockSpec((tb * FEAT_TILES, LANES), lambda i: (prev(i), 0)),
                   pl.BlockSpec((SUBLANES, tb), lambda i: (0, prev(i))),
                   pl.BlockSpec((SUBLANES, tb), lambda i: (0, prev(i)))],
        out_shape=[jax.ShapeDtypeStruct((N_TOK * FEAT_TILES, LANES), jnp.float32),
                   jax.ShapeDtypeStruct((SUBLANES, N_TOK), jnp.int32),
                   jax.ShapeDtypeStruct((SUBLANES, N_TOK), jnp.float32)],
        scratch_shapes=[pltpu.VMEM((2, tb, D_MODEL), jnp.float32)],
        compiler_params=pltpu.CompilerParams(
            dimension_semantics=("arbitrary",), vmem_limit_bytes=VMEM_LIMIT),
        name="proj",
    )(outa, ob, h, wout, ln1g, ln1b, wr, br)


def _plan_kernel(ri_ref, ldest_ref, runs_ref, meta_ref, cnt_ref, run_ref, start_ref):
    phase = pl.program_id(0)
    blk = pl.program_id(1)
    tb = ri_ref.shape[1]
    f32 = jnp.float32

    e_sub = lax.broadcasted_iota(jnp.int32, (N_EXPERTS, tb), 0)
    oh1 = e_sub == ri_ref[0:1, :]
    oh2 = e_sub == ri_ref[1:2, :]
    oh = jnp.where(oh1 | oh2, 1.0, 0.0).astype(f32)
    blk_count = jnp.sum(oh, axis=1, keepdims=True)

    @pl.when((phase == 0) & (blk == 0))
    def _():
        cnt_ref[...] = jnp.zeros_like(cnt_ref)

    @pl.when(phase == 0)
    def _():
        cnt_ref[...] = cnt_ref[...] + blk_count

    @pl.when((phase == 1) & (blk == 0))
    def _():
        counts = cnt_ref[:, 0:1]
        padded = jnp.floor((counts + (EXPERT_ROWS - 1)) * (1.0 / EXPERT_ROWS)) * EXPERT_ROWS
        er = lax.broadcasted_iota(jnp.int32, (N_EXPERTS, LANES), 0)
        ec = lax.broadcasted_iota(jnp.int32, (N_EXPERTS, LANES), 1)
        padded_row = jnp.sum(jnp.where(er == ec, padded, 0.0), axis=0, keepdims=True)
        counts_row = jnp.sum(jnp.where(er == ec, counts, 0.0), axis=0, keepdims=True)
        pad_end = jnp.sum(jnp.where(ec <= er, padded_row, 0.0), axis=1, keepdims=True)
        pad_end_row = jnp.sum(jnp.where(er == ec, pad_end, 0.0), axis=0, keepdims=True)
        start_ref[...] = jnp.broadcast_to(pad_end - padded, start_ref.shape)
        run_ref[...] = jnp.zeros_like(run_ref)
        bstart = (lax.broadcasted_iota(jnp.int32, (N_EXPERTS, META_LANES), 1) * EXPERT_ROWS).astype(f32)
        blk_e = jnp.sum(jnp.where(pad_end <= bstart, 1.0, 0.0), axis=0, keepdims=True)
        blk_e = jnp.minimum(blk_e, N_EXPERTS - 1.0)
        n_used = pad_end[N_EXPERTS - 1:N_EXPERTS, :] * (1.0 / EXPERT_ROWS)
        pad3 = lambda r: jnp.concatenate(
            [r, jnp.zeros((1, META_LANES - LANES), f32)], axis=1)
        msub = lax.broadcasted_iota(jnp.int32, (SUBLANES, META_LANES), 0)
        meta = jnp.where(msub == 0, blk_e,
                         jnp.where(msub == 1, pad3(pad_end_row),
                                   jnp.where(msub == 2, pad3(counts_row),
                                             jnp.where(msub == 3, n_used, 0.0))))
        meta_ref[...] = meta.astype(jnp.int32)

    @pl.when(phase == 1)
    def _():
        tr = lax.broadcasted_iota(jnp.int32, (tb, tb), 0)
        tc = lax.broadcasted_iota(jnp.int32, (tb, tb), 1)
        upper = jnp.where(tr < tc, 1.0, 0.0).astype(jnp.bfloat16)
        prefix = jnp.dot(oh.astype(jnp.bfloat16), upper, preferred_element_type=f32)
        er = lax.broadcasted_iota(jnp.int32, (N_EXPERTS, LANES), 0)
        ec = lax.broadcasted_iota(jnp.int32, (N_EXPERTS, LANES), 1)
        to_row = lambda col: jnp.sum(jnp.where(er == ec, col, 0.0), axis=0, keepdims=True)
        cnt_row = to_row(blk_count)
        lstart = jnp.sum(jnp.where(ec < er, cnt_row, 0.0), axis=1, keepdims=True)
        base = prefix + lstart
        d1 = jnp.sum(jnp.where(oh1, base, 0.0), axis=0, keepdims=True)
        d2 = jnp.sum(jnp.where(oh2, base, 0.0), axis=0, keepdims=True)
        sub = lax.broadcasted_iota(jnp.int32, (SUBLANES, tb), 0)
        ldest_ref[...] = jnp.where(sub == 0, d1, jnp.where(sub == 1, d2, 0.0)).astype(jnp.int32)
        gstart = start_ref[:, 0:1] + run_ref[:, 0:1]
        rsub = lax.broadcasted_iota(jnp.int32, (SUBLANES, LANES), 0)
        runs = jnp.where(rsub == 0, cnt_row,
                         jnp.where(rsub == 1, to_row(lstart), jnp.where(rsub == 2, to_row(gstart), 0.0)))
        runs_ref[...] = runs.astype(jnp.int32)
        run_ref[...] = run_ref[...] + blk_count


def _plan(ri):
    tb = MOE_TOKENS
    return pl.pallas_call(
        _plan_kernel,
        grid=(2, N_TOK // tb),
        in_specs=[pl.BlockSpec((SUBLANES, tb), lambda p, i: (0, i))],
        out_specs=[pl.BlockSpec((SUBLANES, tb), lambda p, i: (0, i * p)),
                   pl.BlockSpec((SUBLANES, LANES), lambda p, i: (i * p, 0)),
                   pl.BlockSpec((SUBLANES, META_LANES), lambda p, i: (0, 0))],
        out_shape=[jax.ShapeDtypeStruct((SUBLANES, N_TOK), jnp.int32),
                   jax.ShapeDtypeStruct((N_MOE_BLOCKS * SUBLANES, LANES), jnp.int32),
                   jax.ShapeDtypeStruct((SUBLANES, META_LANES), jnp.int32)],
        scratch_shapes=[pltpu.VMEM((N_EXPERTS, LANES), jnp.float32),
                        pltpu.VMEM((N_EXPERTS, LANES), jnp.float32),
                        pltpu.VMEM((N_EXPERTS, LANES), jnp.float32)],
        compiler_params=pltpu.CompilerParams(dimension_semantics=("arbitrary", "arbitrary")),
        name="plan",
    )(ri)


def _for_each_run_piece(runs_ref, fn):
    for e in range(N_EXPERTS):
        n, lstart, gstart = runs_ref[0, e], runs_ref[1, e], runs_ref[2, e]
        for bit in range(RUN_BITS):
            @pl.when((n & (1 << bit)) != 0)
            def _(n=n, lstart=lstart, gstart=gstart, bit=bit):
                off = (n >> (bit + 1)) << (bit + 1)
                fn(lstart + off, gstart + off, 1 << bit)


def _tile_rows(ref, row, rows):
    return ref.at[pl.ds(pl.multiple_of(row * FEAT_TILES, FEAT_TILES), rows * FEAT_TILES)]


def _dispatch_kernel(meta_ref, runs_ref, ldest_ref, h1r_ref, buf_ref, sorted_ref, zero_ref, sems, zsem):
    i = pl.program_id(0)
    n_steps = pl.num_programs(0)
    tb = ldest_ref.shape[1]
    slot = i % 2
    block_tiles = 2 * tb * FEAT_TILES

    def wait_slot(s):
        pltpu.make_async_copy(sorted_ref.at[s], buf_ref.at[pl.ds(0, block_tiles)], sems.at[s]).wait()

    @pl.when(i == 0)
    def _():
        zero_ref[...] = jnp.zeros_like(zero_ref)

        def zero_copy(e):
            start = pl.multiple_of((meta_ref[1, e] - EXPERT_ROWS) * FEAT_TILES, EXPERT_ROWS * FEAT_TILES)
            return pltpu.make_async_copy(
                zero_ref, buf_ref.at[pl.ds(start, EXPERT_ROWS * FEAT_TILES)], zsem)

        def start_zero(e, c):
            @pl.when(meta_ref[2, e] > 0)
            def _():
                zero_copy(e).start()
            return c

        def wait_zero(e, c):
            @pl.when(meta_ref[2, e] > 0)
            def _():
                zero_copy(e).wait()
            return c

        def tail_copy(b):
            start = pl.multiple_of(b * (EXPERT_ROWS * FEAT_TILES), EXPERT_ROWS * FEAT_TILES)
            return pltpu.make_async_copy(
                zero_ref, buf_ref.at[pl.ds(start, EXPERT_ROWS * FEAT_TILES)], zsem)

        def start_tail(b, c):
            tail_copy(b).start()
            return c

        def wait_tail(b, c):
            tail_copy(b).wait()
            return c

        lax.fori_loop(0, N_EXPERTS, start_zero, 0)
        lax.fori_loop(meta_ref[3, 0], N_ROW_BLOCKS, start_tail, 0)
        lax.fori_loop(0, N_EXPERTS, wait_zero, 0)
        lax.fori_loop(meta_ref[3, 0], N_ROW_BLOCKS, wait_tail, 0)

    @pl.when(i >= 2)
    def _():
        wait_slot(slot)

    x = _from_row_tiles(h1r_ref, tb).astype(jnp.bfloat16)
    ld0 = ldest_ref[0:1, :]
    ld1 = ldest_ref[1:2, :]
    for c in range(2 * tb // SORT_CHUNK):
        r = lax.broadcasted_iota(jnp.int32, (SORT_CHUNK, tb), 0) + c * SORT_CHUNK
        perm = jnp.where((r == ld0) | (r == ld1), 1.0, 0.0).astype(jnp.bfloat16)
        rows = jnp.dot(perm, x, preferred_element_type=jnp.float32)
        _to_row_tiles(sorted_ref.at[slot, pl.ds(c * SORT_CHUNK * FEAT_TILES, SORT_CHUNK * FEAT_TILES)], rows)

    def send(lrow, grow, rows):
        pltpu.make_async_copy(_tile_rows(sorted_ref.at[slot], lrow, rows),
                              _tile_rows(buf_ref, grow, rows), sems.at[slot]).start()

    _for_each_run_piece(runs_ref, send)

    @pl.when(i == n_steps - 1)
    def _():
        wait_slot(slot)
        wait_slot(1 - slot)


def _dispatch(meta, runs, ldest, h1r):
    tb = MOE_TOKENS
    return pl.pallas_call(
        _dispatch_kernel,
        grid_spec=pltpu.PrefetchScalarGridSpec(
            num_scalar_prefetch=1,
            grid=(N_MOE_BLOCKS,),
            in_specs=[pl.BlockSpec((SUBLANES, LANES), lambda i, m: (i, 0), memory_space=pltpu.SMEM),
                      pl.BlockSpec((SUBLANES, tb), lambda i, m: (0, i)),
                      pl.BlockSpec((tb * FEAT_TILES, LANES), lambda i, m: (i, 0))],
            out_specs=pl.BlockSpec(memory_space=pl.ANY),
            scratch_shapes=[pltpu.VMEM((2, 2 * tb * FEAT_TILES, LANES), jnp.float32),
                            pltpu.VMEM((EXPERT_ROWS * FEAT_TILES, LANES), jnp.float32),
                            pltpu.SemaphoreType.DMA((2,)), pltpu.SemaphoreType.DMA]),
        out_shape=jax.ShapeDtypeStruct((N_ROWS * FEAT_TILES, LANES), jnp.float32),
        compiler_params=pltpu.CompilerParams(
            dimension_semantics=("arbitrary",), vmem_limit_bytes=VMEM_LIMIT),
        name="dispatch",
    )(meta, runs, ldest, h1r)


def _experts_kernel(meta_ref, x_ref, wg_ref, wu_ref, wd_ref, o_ref):
    n = EXPERT_ROWS

    @pl.when(pl.program_id(0) < meta_ref[3, 0])
    def _():
        x = _from_row_tiles(x_ref, n).astype(jnp.bfloat16)
        gate = jnp.dot(x, wg_ref[...], preferred_element_type=jnp.float32)
        up = jnp.dot(x, wu_ref[...], preferred_element_type=jnp.float32)
        act = (gate * jax.nn.sigmoid(gate) * up).astype(jnp.bfloat16)
        _to_row_tiles(o_ref, jnp.dot(act, wd_ref[...], preferred_element_type=jnp.float32))

    @pl.when(pl.program_id(0) >= meta_ref[3, 0])
    def _():
        o_ref[...] = jnp.zeros_like(o_ref)


def _experts(meta, buf, wg, wu, wd):
    used = lambda i, m: jnp.minimum(i, m[3, 0] - 1)
    return pl.pallas_call(
        _experts_kernel,
        grid_spec=pltpu.PrefetchScalarGridSpec(
            num_scalar_prefetch=1,
            grid=(N_ROW_BLOCKS,),
            in_specs=[pl.BlockSpec((EXPERT_ROWS * FEAT_TILES, LANES), lambda i, m: (used(i, m), 0)),
                      pl.BlockSpec((None, D_MODEL, EXPERT_FF), lambda i, m: (m[0, used(i, m)], 0, 0)),
                      pl.BlockSpec((None, D_MODEL, EXPERT_FF), lambda i, m: (m[0, used(i, m)], 0, 0)),
                      pl.BlockSpec((None, EXPERT_FF, D_MODEL), lambda i, m: (m[0, used(i, m)], 0, 0))],
            out_specs=pl.BlockSpec((EXPERT_ROWS * FEAT_TILES, LANES), lambda i, m: (i, 0))),
        out_shape=jax.ShapeDtypeStruct((N_ROWS * FEAT_TILES, LANES), jnp.float32),
        compiler_params=pltpu.CompilerParams(
            dimension_semantics=("arbitrary",), vmem_limit_bytes=VMEM_LIMIT),
        name="experts",
    )(meta, buf, wg, wu, wd)


def _combine_kernel(runs_ref, runs_next_ref, ldest_ref, rf_ref, h1r_ref, eout_ref, ln2g_ref, ln2b_ref,
                    o_ref, y_ref, sems):
    i = pl.program_id(0)
    n_steps = pl.num_programs(0)
    tb = ldest_ref.shape[1]
    slot = i % 2
    block_tiles = 2 * tb * FEAT_TILES

    def fetch(table_ref, s):
        def recv(lrow, grow, rows):
            pltpu.make_async_copy(_tile_rows(eout_ref, grow, rows),
                                  _tile_rows(y_ref.at[s], lrow, rows), sems.at[s]).start()
        _for_each_run_piece(table_ref, recv)

    @pl.when(i == 0)
    def _():
        fetch(runs_ref, slot)

    @pl.when(i + 1 < n_steps)
    def _():
        fetch(runs_next_ref, 1 - slot)

    pltpu.make_async_copy(eout_ref.at[pl.ds(0, block_tiles)], y_ref.at[slot], sems.at[slot]).wait()

    ld = ldest_ref[...].astype(jnp.float32).T
    gates = rf_ref[...].T
    y = None
    for c in range(2 * tb // SORT_CHUNK):
        col = (lax.broadcasted_iota(jnp.int32, (tb, SORT_CHUNK), 1) + c * SORT_CHUNK).astype(jnp.float32)
        g = (jnp.where(col == ld[:, 0:1], gates[:, 0:1], 0.0)
             + jnp.where(col == ld[:, 1:2], gates[:, 1:2], 0.0)).astype(jnp.bfloat16)
        rows = _from_row_tiles(
            y_ref.at[slot, pl.ds(c * SORT_CHUNK * FEAT_TILES, SORT_CHUNK * FEAT_TILES)], SORT_CHUNK)
        part = jnp.dot(g, rows.astype(jnp.bfloat16), preferred_element_type=jnp.float32)
        y = part if y is None else y + part
    h1 = _from_row_tiles(h1r_ref, tb)
    o_ref[...] = _layer_norm(ALPHA * h1 + y, ln2g_ref[...], ln2b_ref[...])


def _combine(runs, ldest, rf, h1r, eout, ln2g, ln2b):
    tb = MOE_TOKENS
    last = N_MOE_BLOCKS - 1
    return pl.pallas_call(
        _combine_kernel,
        grid=(N_MOE_BLOCKS,),
        in_specs=[pl.BlockSpec((SUBLANES, LANES), lambda i: (i, 0), memory_space=pltpu.SMEM),
                  pl.BlockSpec((SUBLANES, LANES), lambda i: (jnp.minimum(i + 1, last), 0),
                               memory_space=pltpu.SMEM),
                  pl.BlockSpec((SUBLANES, tb), lambda i: (0, i)),
                  pl.BlockSpec((SUBLANES, tb), lambda i: (0, i)),
                  pl.BlockSpec((tb * FEAT_TILES, LANES), lambda i: (i, 0)),
                  pl.BlockSpec(memory_space=pl.ANY),
                  pl.BlockSpec((1, D_MODEL), lambda i: (0, 0)),
                  pl.BlockSpec((1, D_MODEL), lambda i: (0, 0))],
        out_specs=pl.BlockSpec((tb, D_MODEL), lambda i: (i, 0)),
        out_shape=jax.ShapeDtypeStruct((N_TOK, D_MODEL), jnp.float32),
        scratch_shapes=[pltpu.VMEM((2, 2 * tb * FEAT_TILES, LANES), jnp.float32),
                        pltpu.SemaphoreType.DMA((2,))],
        compiler_params=pltpu.CompilerParams(
            dimension_semantics=("arbitrary",), vmem_limit_bytes=VMEM_LIMIT),
        name="combine",
    )(runs, runs, ldest, rf, h1r, eout, ln2g, ln2b)


def _swap_halves(w):
    half = w.shape[-1] // 2
    return jnp.concatenate([w[..., half:], w[..., :half]], axis=-1)


def kernel(x, positions, ln0_g, ln0_b, w_in, gm_ln_g, gm_ln_b, w_spatial, b_spatial, q_norm_g, w_uq, kv_norm_g, w_ukv, w_out, ln1_g, ln1_b, w_router_group, b_router_group, w_router_expert, b_router_expert, w_gate, w_up, w_down, ln2_g, ln2_b):
    bf16 = jnp.bfloat16
    row = lambda a: a.reshape(1, -1)

    w_in0 = w_in[0]
    kr_cols = w_in0[:, O_KR:O_KR + QK_ROPE_DIM]
    win = jnp.concatenate([w_in0, _swap_halves(kr_cols)], axis=1).astype(bf16)
    wuq3 = w_uq[0].reshape(Q_LORA_RANK, MLA_HEADS, QK_NOPE_DIM + QK_ROPE_DIM)
    rope_cols = wuq3[:, :, QK_NOPE_DIM:]
    wuq = jnp.concatenate([wuq3, _swap_halves(rope_cols)], axis=-1).reshape(Q_LORA_RANK, D_MODEL).astype(bf16)
    wukv = w_ukv[0].astype(bf16)
    wout = w_out[0].astype(bf16)
    bs = jnp.broadcast_to(b_spatial[0][:, :, None], (GM_HEADS, GM_CHUNK, GM_HEAD_DIM))
    wr = jnp.concatenate([w_router_group[0], jnp.zeros((D_MODEL, SUBLANES - N_GROUPS), jnp.float32),
                          w_router_expert[0],
                          jnp.zeros((D_MODEL, LANES - ROUTER_ROWS), jnp.float32)], axis=1)
    br = jnp.concatenate([b_router_group[0], jnp.zeros((SUBLANES - N_GROUPS,), jnp.float32),
                          b_router_expert[0]]).reshape(ROUTER_ROWS, 1)
    wg = w_gate[0].astype(bf16)
    wu = w_up[0].astype(bf16)
    wd = w_down[0].astype(bf16)

    inv_freq = ROPE_THETA ** (-jnp.arange(0, QK_ROPE_DIM, 2, dtype=jnp.float32) / QK_ROPE_DIM)
    freq = jnp.tile(inv_freq, 4).reshape(1, LANES)
    quarter = QK_ROPE_DIM // 2
    phase = jnp.concatenate([jnp.zeros((2 * quarter,), jnp.float32),
                             jnp.full((2 * quarter,), math.pi / 2, jnp.float32)]).reshape(1, LANES)
    sign = jnp.concatenate([jnp.ones((2 * quarter,), jnp.float32), -jnp.ones((quarter,), jnp.float32),
                            jnp.ones((quarter,), jnp.float32)]).reshape(1, LANES)

    x2 = x.reshape(N_TOK, D_MODEL)
    pos2 = positions.reshape(N_TOK, 1)

    h, outa, q, k, v = _prep(x2, pos2, row(ln0_g), row(ln0_b), win, row(gm_ln_g[0]), row(gm_ln_b[0]),
                             w_spatial[0], bs, row(q_norm_g[0]), wuq, row(kv_norm_g[0]), wukv,
                             freq, phase, sign)
    ob = _attn(q, k, v)
    h1r, ri, rf = _proj(outa, ob, h, wout, row(ln1_g[0]), row(ln1_b[0]), wr, br)
    ldest, runs, meta = _plan(ri)
    buf = _dispatch(meta, runs, ldest, h1r)
    eout = _experts(meta, buf, wg, wu, wd)
    out = _combine(runs, ldest, rf, h1r, eout, row(ln2_g[0]), row(ln2_b[0]))
    return out.reshape(BATCH, SEQ, D_MODEL)
```

```python
import functools
import math

import jax
import jax.numpy as jnp
from jax import lax
from jax.experimental import pallas as pl
from jax.experimental.pallas import tpu as pltpu

D_MODEL = 1024
BATCH = 16
SEQ = 2048
N_TOK = BATCH * SEQ
CHUNK = 64
GM_WIDTH = 512
GM_HEADS = 4
GM_HEAD_DIM = 128
GM_CHUNK = 128
MLA_HEADS = 4
QK_NOPE_DIM = 128
QK_ROPE_DIM = 64
V_HEAD_DIM = 128
Q_LORA_RANK = 384
KV_LORA_RANK = 256
ROPE_THETA = 10000.0
N_GROUPS = 4
EXPERTS_PER_GROUP = 8
N_EXPERTS = 32
TOP_K = 2
EXPERT_FF = 256
ALPHA = 2.0 ** 0.25
QK_SCALE = (QK_NOPE_DIM + QK_ROPE_DIM) ** -0.5 * math.log2(math.e)

LANES = 128
SUBLANES = 8
FEAT_TILES = D_MODEL // LANES
PREP_TOKENS = 512
ATT_Q = 512
ATT_K = 512
N_QBLK = SEQ // ATT_Q
PROJ_TOKENS = 512
MOE_TOKENS = 512
N_MOE_BLOCKS = N_TOK // MOE_TOKENS
RUN_BITS = (TOP_K * MOE_TOKENS).bit_length()
SORT_CHUNK = 256
EXPERT_ROWS = 256
EXPERT_SUB = 2
N_ROWS = N_TOK * TOP_K + N_EXPERTS * EXPERT_ROWS
N_ROW_BLOCKS = N_ROWS // EXPERT_ROWS
META_LANES = 384
IN_COLS = 2 * GM_WIDTH + Q_LORA_RANK + KV_LORA_RANK + 2 * QK_ROPE_DIM
O_Q = 2 * GM_WIDTH
O_KV = O_Q + Q_LORA_RANK
O_KR = O_KV + KV_LORA_RANK
ROUTER_ROWS = 40
VMEM_LIMIT = 48 * 1024 * 1024

assert N_ROW_BLOCKS <= META_LANES and N_ROW_BLOCKS % EXPERT_SUB == 0


def _layer_norm(x, g, b, eps=1e-5):
    mu = jnp.mean(x, axis=-1, keepdims=True)
    xc = x - mu
    var = jnp.mean(xc * xc, axis=-1, keepdims=True)
    return xc * lax.rsqrt(var + eps) * g + b


def _rms_norm(x, g, eps=1e-6):
    return x * lax.rsqrt(jnp.mean(x * x, axis=-1, keepdims=True) + eps) * g


def _gelu_tanh(x):
    c = math.sqrt(2.0 / math.pi)
    return 0.5 * x * (1.0 + jnp.tanh(c * (x + 0.044715 * (x * x * x))))


def _to_row_tiles(ref, x):
    n = x.shape[0]
    for s in range(FEAT_TILES):
        ref[pl.ds(s, n, stride=FEAT_TILES), :] = x[:, s * LANES:(s + 1) * LANES]


def _from_row_tiles(ref, n):
    return jnp.concatenate(
        [ref[pl.ds(s, n, stride=FEAT_TILES), :] for s in range(FEAT_TILES)], axis=-1)


def _prep_kernel(x_ref, pos_ref, ln0g_ref, ln0b_ref, win_ref, gmg_ref, gmb_ref, ws_ref, bs_ref,
                 qg_ref, wuq_ref, kvg_ref, wukv_ref, freq_ref, phase_ref, sign_ref,
                 h_ref, outa_ref, q_ref, k_ref, v_ref, tabc_ref, tabs_ref, rot_ref):
    tb = x_ref.shape[0]

    @pl.when(pl.program_id(0) == 0)
    def _():
        d = lax.broadcasted_iota(jnp.int32, (tb, LANES), 0).astype(jnp.float32) * freq_ref[...]
        tabc_ref[...] = jnp.cos(d)
        tabs_ref[...] = jnp.sin(d)

    pos = pos_ref[...]
    p0 = pos_ref[0:1, :]
    offset = lax.broadcasted_iota(jnp.int32, (tb, 1), 0)
    consecutive = jnp.max(jnp.abs((pos - p0 - offset).astype(jnp.float32))) == 0.0

    @pl.when(consecutive)
    def _():
        a0 = p0.astype(jnp.float32) * freq_ref[...]
        c0, s0 = jnp.cos(a0), jnp.sin(a0)
        lane = lax.broadcasted_iota(jnp.int32, (1, LANES), 1)
        coef_c = jnp.where(lane < 2 * 32, c0, jnp.where(lane < 3 * 32, -s0, s0))
        coef_s = jnp.where(lane < 2 * 32, -s0, jnp.where(lane < 3 * 32, -c0, c0))
        rot_ref[...] = coef_c * tabc_ref[...] + coef_s * tabs_ref[...]

    @pl.when(jnp.logical_not(consecutive))
    def _():
        ang = pos.astype(jnp.float32) * freq_ref[...]
        rot_ref[...] = jnp.cos(ang - phase_ref[...]) * sign_ref[...]

    rot = rot_ref[...]

    h = _layer_norm(x_ref[...], ln0g_ref[...], ln0b_ref[...])
    h_ref[...] = h
    z = jnp.dot(h.astype(jnp.bfloat16), win_ref[...], preferred_element_type=jnp.float32)

    u = _gelu_tanh(z[:, :GM_WIDTH])
    v = _gelu_tanh(z[:, GM_WIDTH:2 * GM_WIDTH])
    row_chunk = lax.broadcasted_iota(jnp.int32, (GM_CHUNK, GM_CHUNK), 0) // CHUNK
    col_chunk = lax.broadcasted_iota(jnp.int32, (GM_CHUNK, GM_CHUNK), 1) // CHUNK
    allowed = col_chunk <= row_chunk
    for hd in range(GM_HEADS):
        lo, hi = hd * GM_HEAD_DIM, (hd + 1) * GM_HEAD_DIM
        vln = _layer_norm(v[:, lo:hi], gmg_ref[:, lo:hi], gmb_ref[:, lo:hi]).astype(jnp.bfloat16)
        wm = jnp.where(allowed, ws_ref[hd], 0.0).astype(jnp.bfloat16)
        for c in range(tb // GM_CHUNK):
            r0, r1 = c * GM_CHUNK, (c + 1) * GM_CHUNK
            f = jnp.dot(wm, vln[r0:r1], preferred_element_type=jnp.float32) + bs_ref[hd]
            outa_ref[r0:r1, lo:hi] = (u[r0:r1, lo:hi] * f).astype(jnp.bfloat16)

    ql = _rms_norm(z[:, O_Q:O_KV], qg_ref[...]).astype(jnp.bfloat16)
    qf = jnp.dot(ql, wuq_ref[...], preferred_element_type=jnp.float32)
    rot_s = rot * QK_SCALE
    q_parts = []
    for hd in range(MLA_HEADS):
        base = hd * 2 * LANES
        q_parts.append(qf[:, base:base + LANES] * QK_SCALE)
        q_parts.append(qf[:, base + LANES:base + 2 * LANES] * rot_s)
    q_ref[...] = jnp.concatenate(q_parts, axis=-1).astype(jnp.bfloat16)

    kvl = _rms_norm(z[:, O_KV:O_KR], kvg_ref[...]).astype(jnp.bfloat16)
    kv = jnp.dot(kvl, wukv_ref[...], preferred_element_type=jnp.float32)
    t = z[:, O_KR:O_KR + LANES] * rot
    krr = t + pltpu.roll(t, 2 * QK_ROPE_DIM // 2, axis=1)
    k_parts, v_parts = [], []
    for hd in range(MLA_HEADS):
        base = hd * 2 * LANES
        k_parts.append(kv[:, base:base + LANES])
        k_parts.append(krr)
        v_parts.append(kv[:, base + LANES:base + 2 * LANES])
    k_ref[...] = jnp.concatenate(k_parts, axis=-1).astype(jnp.bfloat16)
    v_ref[...] = jnp.concatenate(v_parts, axis=-1).astype(jnp.bfloat16)


def _prep(x2, pos2, ln0g, ln0b, win, gmg, gmb, ws, bs, qg, wuq, kvg, wukv, freq, phase, sign):
    tb = PREP_TOKENS
    full = lambda shape: pl.BlockSpec(shape, lambda i: (0,) * len(shape))
    tok = lambda cols: pl.BlockSpec((tb, cols), lambda i: (i, 0))
    return pl.pallas_call(
        _prep_kernel,
        grid=(N_TOK // tb,),
        in_specs=[tok(D_MODEL), tok(1), full((1, D_MODEL)), full((1, D_MODEL)),
                  full((D_MODEL, IN_COLS)), full((1, GM_WIDTH)), full((1, GM_WIDTH)),
                  full((GM_HEADS, GM_CHUNK, GM_CHUNK)), full((GM_HEADS, GM_CHUNK, GM_HEAD_DIM)),
                  full((1, Q_LORA_RANK)), full((Q_LORA_RANK, D_MODEL)),
                  full((1, KV_LORA_RANK)), full((KV_LORA_RANK, D_MODEL)),
                  full((1, LANES)), full((1, LANES)), full((1, LANES))],
        out_specs=[tok(D_MODEL), tok(GM_WIDTH), tok(D_MODEL), tok(D_MODEL), tok(GM_WIDTH)],
        out_shape=[jax.ShapeDtypeStruct((N_TOK, D_MODEL), jnp.float32),
                   jax.ShapeDtypeStruct((N_TOK, GM_WIDTH), jnp.bfloat16),
                   jax.ShapeDtypeStruct((N_TOK, D_MODEL), jnp.bfloat16),
                   jax.ShapeDtypeStruct((N_TOK, D_MODEL), jnp.bfloat16),
                   jax.ShapeDtypeStruct((N_TOK, GM_WIDTH), jnp.bfloat16)],
        scratch_shapes=[pltpu.VMEM((tb, LANES), jnp.float32)] * 3,
        compiler_params=pltpu.CompilerParams(
            dimension_semantics=("arbitrary",), vmem_limit_bytes=VMEM_LIMIT),
        name="prep",
    )(x2, pos2, ln0g, ln0b, win, gmg, gmb, ws, bs, qg, wuq, kvg, wukv, freq, phase, sign)


def _attn_kernel(q_ref, k_ref, v_ref, o_ref, s_ref, mx_ref, ls_ref, acc_ref):
    qi = pl.program_id(1)
    tq = q_ref.shape[0]
    nt = (((1,), (1,)), ((), ()))
    n_kv = SEQ // ATT_K

    row_chunk = (lax.broadcasted_iota(jnp.int32, (tq, ATT_K), 0)) // CHUNK
    col_chunk = (lax.broadcasted_iota(jnp.int32, (tq, ATT_K), 1)) // CHUNK
    diag_allowed = col_chunk <= row_chunk

    def lane_tiles(x):
        return [x[:, t * LANES:(t + 1) * LANES] for t in range(x.shape[1] // LANES)]

    def lane_fold(x, op):
        return functools.reduce(op, lane_tiles(x))

    def scores(hd, j):
        start = pl.multiple_of(j * ATT_K, ATT_K)
        q_h = q_ref[:, hd * 2 * LANES:(hd + 1) * 2 * LANES]
        kb = k_ref[pl.ds(start, ATT_K), hd * 2 * LANES:(hd + 1) * 2 * LANES]
        return lax.dot_general(q_h, kb, nt, preferred_element_type=jnp.float32)

    for hd in range(MLA_HEADS):
        s = jnp.where(diag_allowed, scores(hd, qi), jnp.float32(-1e30))
        s_ref[hd * n_kv + qi] = s
        mx_ref[hd] = lane_fold(s, jnp.maximum)

    def pass_a(j, c):
        for hd in range(MLA_HEADS):
            s = scores(hd, j)
            s_ref[hd * n_kv + j] = s
            mx_ref[hd] = jnp.maximum(mx_ref[hd], lane_fold(s, jnp.maximum))
        return c

    lax.fori_loop(0, qi, pass_a, 0)

    for hd in range(MLA_HEADS):
        mx_ref[hd] = jnp.broadcast_to(jnp.max(mx_ref[hd], axis=-1, keepdims=True), (tq, LANES))

    def probs(hd, j):
        s = s_ref[hd * n_kv + j]
        mb = mx_ref[hd]
        p = jnp.exp2(jnp.concatenate([t - mb for t in lane_tiles(s)], axis=-1))
        start = pl.multiple_of(j * ATT_K, ATT_K)
        vb = v_ref[pl.ds(start, ATT_K), hd * LANES:(hd + 1) * LANES]
        pv = jnp.dot(p.astype(jnp.bfloat16), vb, preferred_element_type=jnp.float32)
        return lane_fold(p, jnp.add), pv

    for hd in range(MLA_HEADS):
        ls, pv = probs(hd, qi)
        ls_ref[hd] = ls
        acc_ref[hd] = pv

    def pass_b(j, c):
        for hd in range(MLA_HEADS):
            ls, pv = probs(hd, j)
            ls_ref[hd] = ls_ref[hd] + ls
            acc_ref[hd] = acc_ref[hd] + pv
        return c

    lax.fori_loop(0, qi, pass_b, 0)

    for hd in range(MLA_HEADS):
        l = jnp.sum(ls_ref[hd], axis=-1, keepdims=True)
        o_ref[:, hd * V_HEAD_DIM:(hd + 1) * V_HEAD_DIM] = (acc_ref[hd] / l).astype(jnp.bfloat16)


def _attn(q, k, v):
    tq = ATT_Q
    tokblk = lambda cols: pl.BlockSpec((tq, cols), lambda b, i: (b * N_QBLK + i, 0))
    seqblk = lambda cols: pl.BlockSpec((SEQ, cols), lambda b, i: (b, 0))
    return pl.pallas_call(
        _attn_kernel,
        grid=(BATCH, N_QBLK),
        in_specs=[tokblk(D_MODEL), seqblk(D_MODEL), seqblk(GM_WIDTH)],
        out_specs=tokblk(GM_WIDTH),
        out_shape=jax.ShapeDtypeStruct((N_TOK, MLA_HEADS * V_HEAD_DIM), jnp.bfloat16),
        scratch_shapes=[pltpu.VMEM((MLA_HEADS * (SEQ // ATT_K), tq, ATT_K), jnp.float32),
                        pltpu.VMEM((MLA_HEADS, tq, LANES), jnp.float32),
                        pltpu.VMEM((MLA_HEADS, tq, LANES), jnp.float32),
                        pltpu.VMEM((MLA_HEADS, tq, V_HEAD_DIM), jnp.float32)],
        compiler_params=pltpu.CompilerParams(
            dimension_semantics=("arbitrary", "arbitrary"), vmem_limit_bytes=VMEM_LIMIT),
        name="attn",
    )(q, k, v)


def _proj_kernel(outa_ref, ob_ref, h_ref, wout_ref, ln1g_ref, ln1b_ref, wr_ref, br_ref,
                 h1r_ref, ri_ref, rf_ref, proj_ref):
    i = pl.program_id(0)
    tb = outa_ref.shape[0]

    @pl.when(i == 0)
    def _():
        proj_ref[...] = jnp.zeros_like(proj_ref)

    h1 = _layer_norm(ALPHA * h_ref[...] + proj_ref[(i + 1) % 2], ln1g_ref[...], ln1b_ref[...])
    _to_row_tiles(h1r_ref, h1)

    logits_tm = jnp.dot(h1.astype(jnp.bfloat16), wr_ref[...], preferred_element_type=jnp.float32)
    logits = logits_tm.T[0:ROUTER_ROWS] + br_ref[...]

    sub_i = lax.broadcasted_iota(jnp.int32, (SUBLANES, tb), 0)
    sub = sub_i.astype(jnp.float32)
    neg = jnp.float32(-jnp.inf)
    g = jnp.where(sub_i < N_GROUPS, logits[0:SUBLANES], neg)
    gmax = jnp.max(g, axis=0, keepdims=True)
    g_top = jnp.min(jnp.where(g == gmax, sub, float(SUBLANES)), axis=0, keepdims=True)
    p_group = 1.0 / jnp.sum(jnp.exp(g - gmax), axis=0, keepdims=True)
    sel = logits[SUBLANES:2 * SUBLANES]
    for grp in range(1, N_GROUPS):
        sel = jnp.where(g_top == float(grp), logits[(grp + 1) * SUBLANES:(grp + 2) * SUBLANES], sel)
    v1 = jnp.max(sel, axis=0, keepdims=True)
    i1 = jnp.min(jnp.where(sel == v1, sub, float(SUBLANES)), axis=0, keepdims=True)
    sel2 = jnp.where(sub == i1, neg, sel)
    v2 = jnp.max(sel2, axis=0, keepdims=True)
    i2 = jnp.min(jnp.where(sel2 == v2, sub, float(SUBLANES)), axis=0, keepdims=True)
    e21 = jnp.exp(v2 - v1)
    w1 = 1.0 / (1.0 + e21)
    gate1 = p_group * w1
    gate2 = p_group * (e21 * w1)
    e1 = g_top * EXPERTS_PER_GROUP + i1
    e2 = g_top * EXPERTS_PER_GROUP + i2
    ri_ref[...] = jnp.where(sub_i == 0, e1, jnp.where(sub_i == 1, e2, 0.0)).astype(jnp.int32)
    rf_ref[...] = jnp.where(sub_i == 0, gate1, jnp.where(sub_i == 1, gate2, 0.0))

    proj_ref[i % 2] = (jnp.dot(outa_ref[...], wout_ref[:GM_WIDTH, :], preferred_element_type=jnp.float32)
                       + jnp.dot(ob_ref[...], wout_ref[GM_WIDTH:, :], preferred_element_type=jnp.float32))


def _proj(outa, ob, h, wout, ln1g, ln1b, wr, br):
    tb = PROJ_TOKENS
    n_blk = N_TOK // tb
    cur = lambda i: jnp.minimum(i, n_blk - 1)
    prev = lambda i: jnp.maximum(i - 1, 0)
    full = lambda shape: pl.BlockSpec(shape, lambda i: (0,) * len(shape))
    return pl.pallas_call(
        _proj_kernel,
        grid=(n_blk + 1,),
        in_specs=[pl.BlockSpec((tb, GM_WIDTH), lambda i: (cur(i), 0)),
                  pl.BlockSpec((tb, GM_WIDTH), lambda i: (cur(i), 0)),
                  pl.BlockSpec((tb, D_MODEL), lambda i: (prev(i), 0)),
                  full((D_MODEL, D_MODEL)), full((1, D_MODEL)), full((1, D_MODEL)),
                  full((D_MODEL, LANES)), full((ROUTER_ROWS, 1))],
        out_specs=[pl.BlockSpec((tb * FEAT_TILES, LANES), lambda i: (prev(i), 0)),
                   pl.BlockSpec((SUBLANES, tb), lambda i: (0, prev(i))),
                   pl.BlockSpec((SUBLANES, tb), lambda i: (0, prev(i)))],
        out_shape=[jax.ShapeDtypeStruct((N_TOK * FEAT_TILES, LANES), jnp.float32),
                   jax.ShapeDtypeStruct((SUBLANES, N_TOK), jnp.int32),
                   jax.ShapeDtypeStruct((SUBLANES, N_TOK), jnp.float32)],
        scratch_shapes=[pltpu.VMEM((2, tb, D_MODEL), jnp.float32)],
        compiler_params=pltpu.CompilerParams(
            dimension_semantics=("arbitrary",), vmem_limit_bytes=VMEM_LIMIT),
        name="proj",
    )(outa, ob, h, wout, ln1g, ln1b, wr, br)


def _plan_kernel(ri_ref, ldest_ref, runs_ref, meta_ref, cnt_ref, run_ref, start_ref):
    phase = pl.program_id(0)
    blk = pl.program_id(1)
    tb = ri_ref.shape[1]
    f32 = jnp.float32

    e_sub = lax.broadcasted_iota(jnp.int32, (N_EXPERTS, tb), 0)
    oh1 = e_sub == ri_ref[0:1, :]
    oh2 = e_sub == ri_ref[1:2, :]
    oh = jnp.where(oh1 | oh2, 1.0, 0.0).astype(f32)
    blk_count = jnp.sum(oh, axis=1, keepdims=True)

    @pl.when((phase == 0) & (blk == 0))
    def _():
        cnt_ref[...] = jnp.zeros_like(cnt_ref)

    @pl.when(phase == 0)
    def _():
        cnt_ref[...] = cnt_ref[...] + blk_count

    @pl.when((phase == 1) & (blk == 0))
    def _():
        counts = cnt_ref[:, 0:1]
        padded = jnp.floor((counts + (EXPERT_ROWS - 1)) * (1.0 / EXPERT_ROWS)) * EXPERT_ROWS
        er = lax.broadcasted_iota(jnp.int32, (N_EXPERTS, LANES), 0)
        ec = lax.broadcasted_iota(jnp.int32, (N_EXPERTS, LANES), 1)
        padded_row = jnp.sum(jnp.where(er == ec, padded, 0.0), axis=0, keepdims=True)
        counts_row = jnp.sum(jnp.where(er == ec, counts, 0.0), axis=0, keepdims=True)
        pad_end = jnp.sum(jnp.where(ec <= er, padded_row, 0.0), axis=1, keepdims=True)
        pad_end_row = jnp.sum(jnp.where(er == ec, pad_end, 0.0), axis=0, keepdims=True)
        start_ref[...] = jnp.broadcast_to(pad_end - padded, start_ref.shape)
        run_ref[...] = jnp.zeros_like(run_ref)
        bstart = (lax.broadcasted_iota(jnp.int32, (N_EXPERTS, META_LANES), 1) * EXPERT_ROWS).astype(f32)
        blk_e = jnp.sum(jnp.where(pad_end <= bstart, 1.0, 0.0), axis=0, keepdims=True)
        blk_e = jnp.minimum(blk_e, N_EXPERTS - 1.0)
        n_used = pad_end[N_EXPERTS - 1:N_EXPERTS, :] * (1.0 / EXPERT_ROWS)
        pad3 = lambda r: jnp.concatenate(
            [r, jnp.zeros((1, META_LANES - LANES), f32)], axis=1)
        msub = lax.broadcasted_iota(jnp.int32, (SUBLANES, META_LANES), 0)
        meta = jnp.where(msub == 0, blk_e,
                         jnp.where(msub == 1, pad3(pad_end_row),
                                   jnp.where(msub == 2, pad3(counts_row),
                                             jnp.where(msub == 3, n_used, 0.0))))
        meta_ref[...] = meta.astype(jnp.int32)

    @pl.when(phase == 1)
    def _():
        tr = lax.broadcasted_iota(jnp.int32, (tb, tb), 0)
        tc = lax.broadcasted_iota(jnp.int32, (tb, tb), 1)
        upper = jnp.where(tr < tc, 1.0, 0.0).astype(jnp.bfloat16)
        prefix = jnp.dot(oh.astype(jnp.bfloat16), upper, preferred_element_type=f32)
        er = lax.broadcasted_iota(jnp.int32, (N_EXPERTS, LANES), 0)
        ec = lax.broadcasted_iota(jnp.int32, (N_EXPERTS, LANES), 1)
        to_row = lambda col: jnp.sum(jnp.where(er == ec, col, 0.0), axis=0, keepdims=True)
        cnt_row = to_row(blk_count)
        lstart = jnp.sum(jnp.where(ec < er, cnt_row, 0.0), axis=1, keepdims=True)
        base = prefix + lstart
        d1 = jnp.sum(jnp.where(oh1, base, 0.0), axis=0, keepdims=True)
        d2 = jnp.sum(jnp.where(oh2, base, 0.0), axis=0, keepdims=True)
        sub = lax.broadcasted_iota(jnp.int32, (SUBLANES, tb), 0)
        ldest_ref[...] = jnp.where(sub == 0, d1, jnp.where(sub == 1, d2, 0.0)).astype(jnp.int32)
        gstart = start_ref[:, 0:1] + run_ref[:, 0:1]
        rsub = lax.broadcasted_iota(jnp.int32, (SUBLANES, LANES), 0)
        runs = jnp.where(rsub == 0, cnt_row,
                         jnp.where(rsub == 1, to_row(lstart), jnp.where(rsub == 2, to_row(gstart), 0.0)))
        runs_ref[...] = runs.astype(jnp.int32)
        run_ref[...] = run_ref[...] + blk_count


def _plan(ri):
    tb = MOE_TOKENS
    return pl.pallas_call(
        _plan_kernel,
        grid=(2, N_TOK // tb),
        in_specs=[pl.BlockSpec((SUBLANES, tb), lambda p, i: (0, i))],
        out_specs=[pl.BlockSpec((SUBLANES, tb), lambda p, i: (0, i * p)),
                   pl.BlockSpec((SUBLANES, LANES), lambda p, i: (i * p, 0)),
                   pl.BlockSpec((SUBLANES, META_LANES), lambda p, i: (0, 0))],
        out_shape=[jax.ShapeDtypeStruct((SUBLANES, N_TOK), jnp.int32),
                   jax.ShapeDtypeStruct((N_MOE_BLOCKS * SUBLANES, LANES), jnp.int32),
                   jax.ShapeDtypeStruct((SUBLANES, META_LANES), jnp.int32)],
        scratch_shapes=[pltpu.VMEM((N_EXPERTS, LANES), jnp.float32),
                        pltpu.VMEM((N_EXPERTS, LANES), jnp.float32),
                        pltpu.VMEM((N_EXPERTS, LANES), jnp.float32)],
        compiler_params=pltpu.CompilerParams(dimension_semantics=("arbitrary", "arbitrary")),
        name="plan",
    )(ri)


def _for_each_run_piece(runs_ref, fn):
    for e in range(N_EXPERTS):
        n, lstart, gstart = runs_ref[0, e], runs_ref[1, e], runs_ref[2, e]
        for bit in range(RUN_BITS):
            @pl.when((n & (1 << bit)) != 0)
            def _(n=n, lstart=lstart, gstart=gstart, bit=bit):
                off = (n >> (bit + 1)) << (bit + 1)
                fn(lstart + off, gstart + off, 1 << bit)


def _tile_rows(ref, row, rows):
    return ref.at[pl.ds(pl.multiple_of(row * FEAT_TILES, FEAT_TILES), rows * FEAT_TILES)]


def _dispatch_kernel(meta_ref, runs_ref, ldest_ref, h1r_ref, buf_ref, sorted_ref, zero_ref, sems, zsem):
    i = pl.program_id(0)
    n_steps = pl.num_programs(0)
    tb = ldest_ref.shape[1]
    slot = i % 2
    block_tiles = 2 * tb * FEAT_TILES

    def wait_slot(s):
        pltpu.make_async_copy(sorted_ref.at[s], buf_ref.at[pl.ds(0, block_tiles)], sems.at[s]).wait()

    @pl.when(i == 0)
    def _():
        zero_ref[...] = jnp.zeros_like(zero_ref)

        def zero_copy(e):
            start = pl.multiple_of((meta_ref[1, e] - EXPERT_ROWS) * FEAT_TILES, EXPERT_ROWS * FEAT_TILES)
            return pltpu.make_async_copy(
                zero_ref, buf_ref.at[pl.ds(start, EXPERT_ROWS * FEAT_TILES)], zsem)

        def start_zero(e, c):
            @pl.when(meta_ref[2, e] > 0)
            def _():
                zero_copy(e).start()
            return c

        def wait_zero(e, c):
            @pl.when(meta_ref[2, e] > 0)
            def _():
                zero_copy(e).wait()
            return c

        def tail_copy(b):
            start = pl.multiple_of(b * (EXPERT_ROWS * FEAT_TILES), EXPERT_ROWS * FEAT_TILES)
            return pltpu.make_async_copy(
                zero_ref, buf_ref.at[pl.ds(start, EXPERT_ROWS * FEAT_TILES)], zsem)

        def start_tail(b, c):
            tail_copy(b).start()
            return c

        def wait_tail(b, c):
            tail_copy(b).wait()
            return c

        lax.fori_loop(0, N_EXPERTS, start_zero, 0)
        lax.fori_loop(meta_ref[3, 0], N_ROW_BLOCKS, start_tail, 0)
        lax.fori_loop(0, N_EXPERTS, wait_zero, 0)
        lax.fori_loop(meta_ref[3, 0], N_ROW_BLOCKS, wait_tail, 0)

    @pl.when(i >= 2)
    def _():
        wait_slot(slot)

    x = _from_row_tiles(h1r_ref, tb).astype(jnp.bfloat16)
    ld0 = ldest_ref[0:1, :]
    ld1 = ldest_ref[1:2, :]
    for c in range(2 * tb // SORT_CHUNK):
        r = lax.broadcasted_iota(jnp.int32, (SORT_CHUNK, tb), 0) + c * SORT_CHUNK
        perm = jnp.where((r == ld0) | (r == ld1), 1.0, 0.0).astype(jnp.bfloat16)
        rows = jnp.dot(perm, x, preferred_element_type=jnp.float32)
        _to_row_tiles(sorted_ref.at[slot, pl.ds(c * SORT_CHUNK * FEAT_TILES, SORT_CHUNK * FEAT_TILES)], rows)

    def send(lrow, grow, rows):
        pltpu.make_async_copy(_tile_rows(sorted_ref.at[slot], lrow, rows),
                              _tile_rows(buf_ref, grow, rows), sems.at[slot]).start()

    _for_each_run_piece(runs_ref, send)

    @pl.when(i == n_steps - 1)
    def _():
        wait_slot(slot)
        wait_slot(1 - slot)


def _dispatch(meta, runs, ldest, h1r):
    tb = MOE_TOKENS
    return pl.pallas_call(
        _dispatch_kernel,
        grid_spec=pltpu.PrefetchScalarGridSpec(
            num_scalar_prefetch=1,
            grid=(N_MOE_BLOCKS,),
            in_specs=[pl.BlockSpec((SUBLANES, LANES), lambda i, m: (i, 0), memory_space=pltpu.SMEM),
                      pl.BlockSpec((SUBLANES, tb), lambda i, m: (0, i)),
                      pl.BlockSpec((tb * FEAT_TILES, LANES), lambda i, m: (i, 0))],
            out_specs=pl.BlockSpec(memory_space=pl.ANY),
            scratch_shapes=[pltpu.VMEM((2, 2 * tb * FEAT_TILES, LANES), jnp.float32),
                            pltpu.VMEM((EXPERT_ROWS * FEAT_TILES, LANES), jnp.float32),
                            pltpu.SemaphoreType.DMA((2,)), pltpu.SemaphoreType.DMA]),
        out_shape=jax.ShapeDtypeStruct((N_ROWS * FEAT_TILES, LANES), jnp.float32),
        compiler_params=pltpu.CompilerParams(
            dimension_semantics=("arbitrary",), vmem_limit_bytes=VMEM_LIMIT),
        name="dispatch",
    )(meta, runs, ldest, h1r)


def _sub_block_expert(meta, i, sub):
    n_used = meta[3, 0]
    step = jnp.minimum(i, (n_used - 1) // EXPERT_SUB)
    return meta[0, jnp.minimum(EXPERT_SUB * step + sub, n_used - 1)]


def _experts_kernel(meta_ref, x_ref, wg0_ref, wu0_ref, wd0_ref, wg1_ref, wu1_ref, wd1_ref, o_ref,
                    wgb_ref, wub_ref, wdb_ref, cached_ref):
    n = EXPERT_ROWS
    i = pl.program_id(0)
    n_used = meta_ref[3, 0]
    weights = ((wg0_ref, wu0_ref, wd0_ref), (wg1_ref, wu1_ref, wd1_ref))

    @pl.when(i == 0)
    def _():
        for sub in range(EXPERT_SUB):
            cached_ref[sub] = -1

    for sub in range(EXPERT_SUB):
        expert = _sub_block_expert(meta_ref, i, sub)

        @pl.when(cached_ref[sub] != expert)
        def _(sub=sub, expert=expert):
            wg_ref, wu_ref, wd_ref = weights[sub]
            wgb_ref[sub] = wg_ref[...].astype(jnp.bfloat16)
            wub_ref[sub] = wu_ref[...].astype(jnp.bfloat16)
            wdb_ref[sub] = wd_ref[...].astype(jnp.bfloat16)
            cached_ref[sub] = expert

    @pl.when(EXPERT_SUB * i < n_used)
    def _():
        for sub in range(EXPERT_SUB):
            rows = pl.ds(sub * n * FEAT_TILES, n * FEAT_TILES)
            x = _from_row_tiles(x_ref.at[rows], n).astype(jnp.bfloat16)
            gate = jnp.dot(x, wgb_ref[sub], preferred_element_type=jnp.float32)
            up = jnp.dot(x, wub_ref[sub], preferred_element_type=jnp.float32)
            act = (gate * jax.nn.sigmoid(gate) * up).astype(jnp.bfloat16)
            _to_row_tiles(o_ref.at[rows], jnp.dot(act, wdb_ref[sub], preferred_element_type=jnp.float32))

    @pl.when(EXPERT_SUB * i >= n_used)
    def _():
        o_ref[...] = jnp.zeros_like(o_ref)


def _experts(meta, buf, wg, wu, wd):
    step_rows = EXPERT_SUB * EXPERT_ROWS
    used_step = lambda i, m: jnp.minimum(i, (m[3, 0] - 1) // EXPERT_SUB)

    def weight_spec(shape, sub):
        return pl.BlockSpec((None,) + shape, lambda i, m: (_sub_block_expert(m, i, sub), 0, 0))

    up_shape, down_shape = (D_MODEL, EXPERT_FF), (EXPERT_FF, D_MODEL)
    return pl.pallas_call(
        _experts_kernel,
        grid_spec=pltpu.PrefetchScalarGridSpec(
            num_scalar_prefetch=1,
            grid=(N_ROW_BLOCKS // EXPERT_SUB,),
            in_specs=[pl.BlockSpec((step_rows * FEAT_TILES, LANES), lambda i, m: (used_step(i, m), 0)),
                      weight_spec(up_shape, 0), weight_spec(up_shape, 0), weight_spec(down_shape, 0),
                      weight_spec(up_shape, 1), weight_spec(up_shape, 1), weight_spec(down_shape, 1)],
            out_specs=pl.BlockSpec((step_rows * FEAT_TILES, LANES), lambda i, m: (i, 0)),
            scratch_shapes=[pltpu.VMEM((EXPERT_SUB,) + up_shape, jnp.bfloat16),
                            pltpu.VMEM((EXPERT_SUB,) + up_shape, jnp.bfloat16),
                            pltpu.VMEM((EXPERT_SUB,) + down_shape, jnp.bfloat16),
                            pltpu.SMEM((EXPERT_SUB,), jnp.int32)]),
        out_shape=jax.ShapeDtypeStruct((N_ROWS * FEAT_TILES, LANES), jnp.float32),
        compiler_params=pltpu.CompilerParams(
            dimension_semantics=("arbitrary",), vmem_limit_bytes=VMEM_LIMIT),
        name="experts",
    )(meta, buf, wg, wu, wd, wg, wu, wd)


def _combine_kernel(runs_ref, runs_next_ref, ldest_ref, rf_ref, h1r_ref, eout_ref, ln2g_ref, ln2b_ref,
                    o_ref, y_ref, sems):
    i = pl.program_id(0)
    n_steps = pl.num_programs(0)
    tb = ldest_ref.shape[1]
    slot = i % 2
    block_tiles = 2 * tb * FEAT_TILES

    def fetch(table_ref, s):
        def recv(lrow, grow, rows):
            pltpu.make_async_copy(_tile_rows(eout_ref, grow, rows),
                                  _tile_rows(y_ref.at[s], lrow, rows), sems.at[s]).start()
        _for_each_run_piece(table_ref, recv)

    @pl.when(i == 0)
    def _():
        fetch(runs_ref, slot)

    @pl.when(i + 1 < n_steps)
    def _():
        fetch(runs_next_ref, 1 - slot)

    pltpu.make_async_copy(eout_ref.at[pl.ds(0, block_tiles)], y_ref.at[slot], sems.at[slot]).wait()

    ld = ldest_ref[...].astype(jnp.float32).T
    gates = rf_ref[...].T
    y = None
    for c in range(2 * tb // SORT_CHUNK):
        col = (lax.broadcasted_iota(jnp.int32, (tb, SORT_CHUNK), 1) + c * SORT_CHUNK).astype(jnp.float32)
        g = (jnp.where(col == ld[:, 0:1], gates[:, 0:1], 0.0)
             + jnp.where(col == ld[:, 1:2], gates[:, 1:2], 0.0)).astype(jnp.bfloat16)
        rows = _from_row_tiles(
            y_ref.at[slot, pl.ds(c * SORT_CHUNK * FEAT_TILES, SORT_CHUNK * FEAT_TILES)], SORT_CHUNK)
        part = jnp.dot(g, rows.astype(jnp.bfloat16), preferred_element_type=jnp.float32)
        y = part if y is None else y + part
    h1 = _from_row_tiles(h1r_ref, tb)
    o_ref[...] = _layer_norm(ALPHA * h1 + y, ln2g_ref[...], ln2b_ref[...])


def _combine(runs, ldest, rf, h1r, eout, ln2g, ln2b):
    tb = MOE_TOKENS
    last = N_MOE_BLOCKS - 1
    return pl.pallas_call(
        _combine_kernel,
        grid=(N_MOE_BLOCKS,),
        in_specs=[pl.BlockSpec((SUBLANES, LANES), lambda i: (i, 0), memory_space=pltpu.SMEM),
                  pl.BlockSpec((SUBLANES, LANES), lambda i: (jnp.minimum(i + 1, last), 0),
                               memory_space=pltpu.SMEM),
                  pl.BlockSpec((SUBLANES, tb), lambda i: (0, i)),
                  pl.BlockSpec((SUBLANES, tb), lambda i: (0, i)),
                  pl.BlockSpec((tb * FEAT_TILES, LANES), lambda i: (i, 0)),
                  pl.BlockSpec(memory_space=pl.ANY),
                  pl.BlockSpec((1, D_MODEL), lambda i: (0, 0)),
                  pl.BlockSpec((1, D_MODEL), lambda i: (0, 0))],
        out_specs=pl.BlockSpec((tb, D_MODEL), lambda i: (i, 0)),
        out_shape=jax.ShapeDtypeStruct((N_TOK, D_MODEL), jnp.float32),
        scratch_shapes=[pltpu.VMEM((2, 2 * tb * FEAT_TILES, LANES), jnp.float32),
                        pltpu.SemaphoreType.DMA((2,))],
        compiler_params=pltpu.CompilerParams(
            dimension_semantics=("arbitrary",), vmem_limit_bytes=VMEM_LIMIT),
        name="combine",
    )(runs, runs, ldest, rf, h1r, eout, ln2g, ln2b)


def _swap_halves(w):
    half = w.shape[-1] // 2
    return jnp.concatenate([w[..., half:], w[..., :half]], axis=-1)


def kernel(x, positions, ln0_g, ln0_b, w_in, gm_ln_g, gm_ln_b, w_spatial, b_spatial, q_norm_g, w_uq, kv_norm_g, w_ukv, w_out, ln1_g, ln1_b, w_router_group, b_router_group, w_router_expert, b_router_expert, w_gate, w_up, w_down, ln2_g, ln2_b):
    bf16 = jnp.bfloat16
    row = lambda a: a.reshape(1, -1)

    w_in0 = w_in[0]
    kr_cols = w_in0[:, O_KR:O_KR + QK_ROPE_DIM]
    win = jnp.concatenate([w_in0, _swap_halves(kr_cols)], axis=1).astype(bf16)
    wuq3 = w_uq[0].reshape(Q_LORA_RANK, MLA_HEADS, QK_NOPE_DIM + QK_ROPE_DIM)
    rope_cols = wuq3[:, :, QK_NOPE_DIM:]
    wuq = jnp.concatenate([wuq3, _swap_halves(rope_cols)], axis=-1).reshape(Q_LORA_RANK, D_MODEL).astype(bf16)
    wukv = w_ukv[0].astype(bf16)
    wout = w_out[0].astype(bf16)
    bs = jnp.broadcast_to(b_spatial[0][:, :, None], (GM_HEADS, GM_CHUNK, GM_HEAD_DIM))
    wr = jnp.concatenate([w_router_group[0], jnp.zeros((D_MODEL, SUBLANES - N_GROUPS), jnp.float32),
                          w_router_expert[0],
                          jnp.zeros((D_MODEL, LANES - ROUTER_ROWS), jnp.float32)],
                         axis=1).astype(bf16)
    br = jnp.concatenate([b_router_group[0], jnp.zeros((SUBLANES - N_GROUPS,), jnp.float32),
                          b_router_expert[0]]).reshape(ROUTER_ROWS, 1)

    inv_freq = ROPE_THETA ** (-jnp.arange(0, QK_ROPE_DIM, 2, dtype=jnp.float32) / QK_ROPE_DIM)
    freq = jnp.tile(inv_freq, 4).reshape(1, LANES)
    quarter = QK_ROPE_DIM // 2
    phase = jnp.concatenate([jnp.zeros((2 * quarter,), jnp.float32),
                             jnp.full((2 * quarter,), math.pi / 2, jnp.float32)]).reshape(1, LANES)
    sign = jnp.concatenate([jnp.ones((2 * quarter,), jnp.float32), -jnp.ones((quarter,), jnp.float32),
                            jnp.ones((quarter,), jnp.float32)]).reshape(1, LANES)

    x2 = x.reshape(N_TOK, D_MODEL)
    pos2 = positions.reshape(N_TOK, 1)

    h, outa, q, k, v = _prep(x2, pos2, row(ln0_g), row(ln0_b), win, row(gm_ln_g[0]), row(gm_ln_b[0]),
                             w_spatial[0], bs, row(q_norm_g[0]), wuq, row(kv_norm_g[0]), wukv,
                             freq, phase, sign)
    ob = _attn(q, k, v)
    h1r, ri, rf = _proj(outa, ob, h, wout, row(ln1_g[0]), row(ln1_b[0]), wr, br)
    ldest, runs, meta = _plan(ri)
    buf = _dispatch(meta, runs, ldest, h1r)
    eout = _experts(meta, buf, w_gate[0], w_up[0], w_down[0])
    out = _combine(runs, ldest, rf, h1r, eout, row(ln2_g[0]), row(ln2_b[0]))
    return out.reshape(BATCH, SEQ, D_MODEL)
```

```python
import functools
import math

import jax
import jax.numpy as jnp
from jax import lax
from jax.experimental import pallas as pl
from jax.experimental.pallas import tpu as pltpu

D_MODEL = 1024
BATCH = 16
SEQ = 2048
N_TOK = BATCH * SEQ
CHUNK = 64
GM_WIDTH = 512
GM_HEADS = 4
GM_HEAD_DIM = 128
GM_CHUNK = 128
MLA_HEADS = 4
QK_NOPE_DIM = 128
QK_ROPE_DIM = 64
V_HEAD_DIM = 128
Q_LORA_RANK = 384
KV_LORA_RANK = 256
ROPE_THETA = 10000.0
N_GROUPS = 4
EXPERTS_PER_GROUP = 8
N_EXPERTS = 32
TOP_K = 2
EXPERT_FF = 256
ALPHA = 2.0 ** 0.25
QK_SCALE = (QK_NOPE_DIM + QK_ROPE_DIM) ** -0.5 * math.log2(math.e)

LANES = 128
SUBLANES = 8
FEAT_TILES = D_MODEL // LANES
PREP_TOKENS = 512
ATT_Q = 512
ATT_K = 512
N_QBLK = SEQ // ATT_Q
PROJ_TOKENS = 512
MOE_TOKENS = 512
N_MOE_BLOCKS = N_TOK // MOE_TOKENS
BLOCK_ROWS = TOP_K * MOE_TOKENS + N_EXPERTS
RUN_BITS = BLOCK_ROWS.bit_length()
SORT_CHUNK = BLOCK_ROWS // 3
EXPERT_ROWS = 256
EXPERT_SUB = 2
N_ROWS = N_MOE_BLOCKS * BLOCK_ROWS + N_EXPERTS * EXPERT_ROWS
BF16_TILE_ROWS = 16
N_ROW_BLOCKS = N_ROWS // EXPERT_ROWS
META_LANES = 384
IN_COLS = 2 * GM_WIDTH + Q_LORA_RANK + KV_LORA_RANK + 2 * QK_ROPE_DIM
O_Q = 2 * GM_WIDTH
O_KV = O_Q + Q_LORA_RANK
O_KR = O_KV + KV_LORA_RANK
ROUTER_ROWS = 40
VMEM_LIMIT = 48 * 1024 * 1024

assert N_ROW_BLOCKS <= META_LANES and N_ROW_BLOCKS % EXPERT_SUB == 0
assert BLOCK_ROWS % 3 == 0 and (SORT_CHUNK * FEAT_TILES) % BF16_TILE_ROWS == 0 and N_ROWS % EXPERT_ROWS == 0


def _layer_norm(x, g, b, eps=1e-5):
    mu = jnp.mean(x, axis=-1, keepdims=True)
    xc = x - mu
    var = jnp.mean(xc * xc, axis=-1, keepdims=True)
    return xc * lax.rsqrt(var + eps) * g + b


def _rms_norm(x, g, eps=1e-6):
    return x * lax.rsqrt(jnp.mean(x * x, axis=-1, keepdims=True) + eps) * g


def _gelu_tanh(x):
    c = math.sqrt(2.0 / math.pi)
    return 0.5 * x * (1.0 + jnp.tanh(c * (x + 0.044715 * (x * x * x))))


def _to_row_tiles(ref, x):
    n = x.shape[0]
    for s in range(FEAT_TILES):
        ref[pl.ds(s, n, stride=FEAT_TILES), :] = x[:, s * LANES:(s + 1) * LANES]


def _from_row_tiles(ref, n):
    return jnp.concatenate(
        [ref[pl.ds(s, n, stride=FEAT_TILES), :] for s in range(FEAT_TILES)], axis=-1)


def _prep_kernel(x_ref, pos_ref, ln0g_ref, ln0b_ref, win_ref, gmg_ref, gmb_ref, ws_ref, bs_ref,
                 qg_ref, wuq_ref, kvg_ref, wukv_ref, freq_ref, phase_ref, sign_ref,
                 h_ref, outa_ref, q_ref, k_ref, v_ref, tabc_ref, tabs_ref, rot_ref):
    tb = x_ref.shape[0]

    @pl.when(pl.program_id(0) == 0)
    def _():
        d = lax.broadcasted_iota(jnp.int32, (tb, LANES), 0).astype(jnp.float32) * freq_ref[...]
        tabc_ref[...] = jnp.cos(d)
        tabs_ref[...] = jnp.sin(d)

    pos = pos_ref[...]
    p0 = pos_ref[0:1, :]
    offset = lax.broadcasted_iota(jnp.int32, (tb, 1), 0)
    consecutive = jnp.max(jnp.abs((pos - p0 - offset).astype(jnp.float32))) == 0.0

    @pl.when(consecutive)
    def _():
        a0 = p0.astype(jnp.float32) * freq_ref[...]
        c0, s0 = jnp.cos(a0), jnp.sin(a0)
        lane = lax.broadcasted_iota(jnp.int32, (1, LANES), 1)
        coef_c = jnp.where(lane < 2 * 32, c0, jnp.where(lane < 3 * 32, -s0, s0))
        coef_s = jnp.where(lane < 2 * 32, -s0, jnp.where(lane < 3 * 32, -c0, c0))
        rot_ref[...] = coef_c * tabc_ref[...] + coef_s * tabs_ref[...]

    @pl.when(jnp.logical_not(consecutive))
    def _():
        ang = pos.astype(jnp.float32) * freq_ref[...]
        rot_ref[...] = jnp.cos(ang - phase_ref[...]) * sign_ref[...]

    rot = rot_ref[...]

    h = _layer_norm(x_ref[...], ln0g_ref[...], ln0b_ref[...])
    h_ref[...] = h
    z = jnp.dot(h.astype(jnp.bfloat16), win_ref[...], preferred_element_type=jnp.float32)

    u = _gelu_tanh(z[:, :GM_WIDTH])
    v = _gelu_tanh(z[:, GM_WIDTH:2 * GM_WIDTH])
    row_chunk = lax.broadcasted_iota(jnp.int32, (GM_CHUNK, GM_CHUNK), 0) // CHUNK
    col_chunk = lax.broadcasted_iota(jnp.int32, (GM_CHUNK, GM_CHUNK), 1) // CHUNK
    allowed = col_chunk <= row_chunk
    for hd in range(GM_HEADS):
        lo, hi = hd * GM_HEAD_DIM, (hd + 1) * GM_HEAD_DIM
        vln = _layer_norm(v[:, lo:hi], gmg_ref[:, lo:hi], gmb_ref[:, lo:hi]).astype(jnp.bfloat16)
        wm = jnp.where(allowed, ws_ref[hd], 0.0).astype(jnp.bfloat16)
        for c in range(tb // GM_CHUNK):
            r0, r1 = c * GM_CHUNK, (c + 1) * GM_CHUNK
            f = jnp.dot(wm, vln[r0:r1], preferred_element_type=jnp.float32) + bs_ref[hd]
            outa_ref[r0:r1, lo:hi] = (u[r0:r1, lo:hi] * f).astype(jnp.bfloat16)

    ql = _rms_norm(z[:, O_Q:O_KV], qg_ref[...]).astype(jnp.bfloat16)
    qf = jnp.dot(ql, wuq_ref[...], preferred_element_type=jnp.float32)
    rot_s = rot * QK_SCALE
    q_parts = []
    for hd in range(MLA_HEADS):
        base = hd * 2 * LANES
        q_parts.append(qf[:, base:base + LANES] * QK_SCALE)
        q_parts.append(qf[:, base + LANES:base + 2 * LANES] * rot_s)
    q_ref[...] = jnp.concatenate(q_parts, axis=-1).astype(jnp.bfloat16)

    kvl = _rms_norm(z[:, O_KV:O_KR], kvg_ref[...]).astype(jnp.bfloat16)
    kv = jnp.dot(kvl, wukv_ref[...], preferred_element_type=jnp.float32)
    t = z[:, O_KR:O_KR + LANES] * rot
    krr = t + pltpu.roll(t, 2 * QK_ROPE_DIM // 2, axis=1)
    k_parts, v_parts = [], []
    for hd in range(MLA_HEADS):
        base = hd * 2 * LANES
        k_parts.append(kv[:, base:base + LANES])
        k_parts.append(krr)
        v_parts.append(kv[:, base + LANES:base + 2 * LANES])
    k_ref[...] = jnp.concatenate(k_parts, axis=-1).astype(jnp.bfloat16)
    v_ref[...] = jnp.concatenate(v_parts, axis=-1).astype(jnp.bfloat16)


def _prep(x2, pos2, ln0g, ln0b, win, gmg, gmb, ws, bs, qg, wuq, kvg, wukv, freq, phase, sign):
    tb = PREP_TOKENS
    full = lambda shape: pl.BlockSpec(shape, lambda i: (0,) * len(shape))
    tok = lambda cols: pl.BlockSpec((tb, cols), lambda i: (i, 0))
    return pl.pallas_call(
        _prep_kernel,
        grid=(N_TOK // tb,),
        in_specs=[tok(D_MODEL), tok(1), full((1, D_MODEL)), full((1, D_MODEL)),
                  full((D_MODEL, IN_COLS)), full((1, GM_WIDTH)), full((1, GM_WIDTH)),
                  full((GM_HEADS, GM_CHUNK, GM_CHUNK)), full((GM_HEADS, GM_CHUNK, GM_HEAD_DIM)),
                  full((1, Q_LORA_RANK)), full((Q_LORA_RANK, D_MODEL)),
                  full((1, KV_LORA_RANK)), full((KV_LORA_RANK, D_MODEL)),
                  full((1, LANES)), full((1, LANES)), full((1, LANES))],
        out_specs=[tok(D_MODEL), tok(GM_WIDTH), tok(D_MODEL), tok(D_MODEL), tok(GM_WIDTH)],
        out_shape=[jax.ShapeDtypeStruct((N_TOK, D_MODEL), jnp.float32),
                   jax.ShapeDtypeStruct((N_TOK, GM_WIDTH), jnp.bfloat16),
                   jax.ShapeDtypeStruct((N_TOK, D_MODEL), jnp.bfloat16),
                   jax.ShapeDtypeStruct((N_TOK, D_MODEL), jnp.bfloat16),
                   jax.ShapeDtypeStruct((N_TOK, GM_WIDTH), jnp.bfloat16)],
        scratch_shapes=[pltpu.VMEM((tb, LANES), jnp.float32)] * 3,
        compiler_params=pltpu.CompilerParams(
            dimension_semantics=("arbitrary",), vmem_limit_bytes=VMEM_LIMIT),
        name="prep",
    )(x2, pos2, ln0g, ln0b, win, gmg, gmb, ws, bs, qg, wuq, kvg, wukv, freq, phase, sign)


def _attn_kernel(q_ref, k_ref, v_ref, o_ref, s_ref, mx_ref, ls_ref, acc_ref):
    qi = pl.program_id(1)
    tq = q_ref.shape[0]
    nt = (((1,), (1,)), ((), ()))
    n_kv = SEQ // ATT_K

    row_chunk = (lax.broadcasted_iota(jnp.int32, (tq, ATT_K), 0)) // CHUNK
    col_chunk = (lax.broadcasted_iota(jnp.int32, (tq, ATT_K), 1)) // CHUNK
    diag_allowed = col_chunk <= row_chunk

    def lane_tiles(x):
        return [x[:, t * LANES:(t + 1) * LANES] for t in range(x.shape[1] // LANES)]

    def lane_fold(x, op):
        return functools.reduce(op, lane_tiles(x))

    def scores(hd, j):
        start = pl.multiple_of(j * ATT_K, ATT_K)
        q_h = q_ref[:, hd * 2 * LANES:(hd + 1) * 2 * LANES]
        kb = k_ref[pl.ds(start, ATT_K), hd * 2 * LANES:(hd + 1) * 2 * LANES]
        return lax.dot_general(q_h, kb, nt, preferred_element_type=jnp.float32)

    for hd in range(MLA_HEADS):
        s = jnp.where(diag_allowed, scores(hd, qi), jnp.float32(-1e30))
        s_ref[hd * n_kv + qi] = s
        mx_ref[hd] = lane_fold(s, jnp.maximum)

    def pass_a(j, c):
        for hd in range(MLA_HEADS):
            s = scores(hd, j)
            s_ref[hd * n_kv + j] = s
            mx_ref[hd] = jnp.maximum(mx_ref[hd], lane_fold(s, jnp.maximum))
        return c

    lax.fori_loop(0, qi, pass_a, 0)

    for hd in range(MLA_HEADS):
        mx_ref[hd] = jnp.broadcast_to(jnp.max(mx_ref[hd], axis=-1, keepdims=True), (tq, LANES))

    def probs(hd, j):
        s = s_ref[hd * n_kv + j]
        mb = mx_ref[hd]
        p = jnp.exp2(jnp.concatenate([t - mb for t in lane_tiles(s)], axis=-1))
        start = pl.multiple_of(j * ATT_K, ATT_K)
        vb = v_ref[pl.ds(start, ATT_K), hd * LANES:(hd + 1) * LANES]
        pv = jnp.dot(p.astype(jnp.bfloat16), vb, preferred_element_type=jnp.float32)
        return lane_fold(p, jnp.add), pv

    for hd in range(MLA_HEADS):
        ls, pv = probs(hd, qi)
        ls_ref[hd] = ls
        acc_ref[hd] = pv

    def pass_b(j, c):
        for hd in range(MLA_HEADS):
            ls, pv = probs(hd, j)
            ls_ref[hd] = ls_ref[hd] + ls
            acc_ref[hd] = acc_ref[hd] + pv
        return c

    lax.fori_loop(0, qi, pass_b, 0)

    for hd in range(MLA_HEADS):
        l = jnp.sum(ls_ref[hd], axis=-1, keepdims=True)
        o_ref[:, hd * V_HEAD_DIM:(hd + 1) * V_HEAD_DIM] = (acc_ref[hd] / l).astype(jnp.bfloat16)


def _attn(q, k, v):
    tq = ATT_Q
    tokblk = lambda cols: pl.BlockSpec((tq, cols), lambda b, i: (b * N_QBLK + i, 0))
    seqblk = lambda cols: pl.BlockSpec((SEQ, cols), lambda b, i: (b, 0))
    return pl.pallas_call(
        _attn_kernel,
        grid=(BATCH, N_QBLK),
        in_specs=[tokblk(D_MODEL), seqblk(D_MODEL), seqblk(GM_WIDTH)],
        out_specs=tokblk(GM_WIDTH),
        out_shape=jax.ShapeDtypeStruct((N_TOK, MLA_HEADS * V_HEAD_DIM), jnp.bfloat16),
        scratch_shapes=[pltpu.VMEM((MLA_HEADS * (SEQ // ATT_K), tq, ATT_K), jnp.float32),
                        pltpu.VMEM((MLA_HEADS, tq, LANES), jnp.float32),
                        pltpu.VMEM((MLA_HEADS, tq, LANES), jnp.float32),
                        pltpu.VMEM((MLA_HEADS, tq, V_HEAD_DIM), jnp.float32)],
        compiler_params=pltpu.CompilerParams(
            dimension_semantics=("arbitrary", "arbitrary"), vmem_limit_bytes=VMEM_LIMIT),
        name="attn",
    )(q, k, v)


def _proj_kernel(outa_ref, ob_ref, h_ref, wout_ref, ln1g_ref, ln1b_ref, wr_ref, br_ref,
                 h1r_ref, ri_ref, rf_ref, proj_ref):
    i = pl.program_id(0)
    tb = outa_ref.shape[0]

    @pl.when(i == 0)
    def _():
        proj_ref[...] = jnp.zeros_like(proj_ref)

    h1 = _layer_norm(ALPHA * h_ref[...] + proj_ref[(i + 1) % 2], ln1g_ref[...], ln1b_ref[...])
    _to_row_tiles(h1r_ref, h1)

    logits_tm = jnp.dot(h1.astype(jnp.bfloat16), wr_ref[...], preferred_element_type=jnp.float32)
    logits = logits_tm.T[0:ROUTER_ROWS] + br_ref[...]

    sub_i = lax.broadcasted_iota(jnp.int32, (SUBLANES, tb), 0)
    sub = sub_i.astype(jnp.float32)
    neg = jnp.float32(-jnp.inf)
    g = jnp.where(sub_i < N_GROUPS, logits[0:SUBLANES], neg)
    gmax = jnp.max(g, axis=0, keepdims=True)
    g_top = jnp.min(jnp.where(g == gmax, sub, float(SUBLANES)), axis=0, keepdims=True)
    p_group = 1.0 / jnp.sum(jnp.exp(g - gmax), axis=0, keepdims=True)
    sel = logits[SUBLANES:2 * SUBLANES]
    for grp in range(1, N_GROUPS):
        sel = jnp.where(g_top == float(grp), logits[(grp + 1) * SUBLANES:(grp + 2) * SUBLANES], sel)
    v1 = jnp.max(sel, axis=0, keepdims=True)
    i1 = jnp.min(jnp.where(sel == v1, sub, float(SUBLANES)), axis=0, keepdims=True)
    sel2 = jnp.where(sub == i1, neg, sel)
    v2 = jnp.max(sel2, axis=0, keepdims=True)
    i2 = jnp.min(jnp.where(sel2 == v2, sub, float(SUBLANES)), axis=0, keepdims=True)
    e21 = jnp.exp(v2 - v1)
    w1 = 1.0 / (1.0 + e21)
    gate1 = p_group * w1
    gate2 = p_group * (e21 * w1)
    e1 = g_top * EXPERTS_PER_GROUP + i1
    e2 = g_top * EXPERTS_PER_GROUP + i2
    ri_ref[...] = jnp.where(sub_i == 0, e1, jnp.where(sub_i == 1, e2, 0.0)).astype(jnp.int32)
    rf_ref[...] = jnp.where(sub_i == 0, gate1, jnp.where(sub_i == 1, gate2, 0.0))

    proj_ref[i % 2] = (jnp.dot(outa_ref[...], wout_ref[:GM_WIDTH, :], preferred_element_type=jnp.float32)
                       + jnp.dot(ob_ref[...], wout_ref[GM_WIDTH:, :], preferred_element_type=jnp.float32))


def _proj(outa, ob, h, wout, ln1g, ln1b, wr, br):
    tb = PROJ_TOKENS
    n_blk = N_TOK // tb
    cur = lambda i: jnp.minimum(i, n_blk - 1)
    prev = lambda i: jnp.maximum(i - 1, 0)
    full = lambda shape: pl.BlockSpec(shape, lambda i: (0,) * len(shape))
    return pl.pallas_call(
        _proj_kernel,
        grid=(n_blk + 1,),
        in_specs=[pl.BlockSpec((tb, GM_WIDTH), lambda i: (cur(i), 0)),
                  pl.BlockSpec((tb, GM_WIDTH), lambda i: (cur(i), 0)),
                  pl.BlockSpec((tb, D_MODEL), lambda i: (prev(i), 0)),
                  full((D_MODEL, D_MODEL)), full((1, D_MODEL)), full((1, D_MODEL)),
                  full((D_MODEL, LANES)), full((ROUTER_ROWS, 1))],
        out_specs=[pl.BlockSpec((tb * FEAT_TILES, LANES), lambda i: (prev(i), 0)),
                   pl.BlockSpec((SUBLANES, tb), lambda i: (0, prev(i))),
                   pl.BlockSpec((SUBLANES, tb), lambda i: (0, prev(i)))],
        out_shape=[jax.ShapeDtypeStruct((N_TOK * FEAT_TILES, LANES), jnp.float32),
                   jax.ShapeDtypeStruct((SUBLANES, N_TOK), jnp.int32),
                   jax.ShapeDtypeStruct((SUBLANES, N_TOK), jnp.float32)],
        scratch_shapes=[pltpu.VMEM((2, tb, D_MODEL), jnp.float32)],
        compiler_params=pltpu.CompilerParams(
            dimension_semantics=("arbitrary",), vmem_limit_bytes=VMEM_LIMIT),
        name="proj",
    )(outa, ob, h, wout, ln1g, ln1b, wr, br)


def _plan_kernel(ri_ref, ldest_ref, runs_ref, meta_ref, cnt_ref, run_ref, start_ref):
    phase = pl.program_id(0)
    blk = pl.program_id(1)
    tb = ri_ref.shape[1]
    f32 = jnp.float32

    e_sub = lax.broadcasted_iota(jnp.int32, (N_EXPERTS, tb), 0)
    oh1 = e_sub == ri_ref[0:1, :]
    oh2 = e_sub == ri_ref[1:2, :]
    oh = jnp.where(oh1 | oh2, 1.0, 0.0).astype(f32)
    raw_count = jnp.sum(oh, axis=1, keepdims=True)
    odd = raw_count - 2.0 * jnp.floor(raw_count * 0.5)
    spare = N_EXPERTS - jnp.sum(odd, axis=0, keepdims=True)
    is_e0 = lax.broadcasted_iota(jnp.int32, (N_EXPERTS, 1), 0) == 0
    blk_count = raw_count + odd + jnp.where(is_e0, spare, 0.0)

    @pl.when((phase == 0) & (blk == 0))
    def _():
        cnt_ref[...] = jnp.zeros_like(cnt_ref)

    @pl.when(phase == 0)
    def _():
        cnt_ref[...] = cnt_ref[...] + blk_count

    @pl.when((phase == 1) & (blk == 0))
    def _():
        counts = cnt_ref[:, 0:1]
        padded = jnp.floor((counts + (EXPERT_ROWS - 1)) * (1.0 / EXPERT_ROWS)) * EXPERT_ROWS
        er = lax.broadcasted_iota(jnp.int32, (N_EXPERTS, LANES), 0)
        ec = lax.broadcasted_iota(jnp.int32, (N_EXPERTS, LANES), 1)
        padded_row = jnp.sum(jnp.where(er == ec, padded, 0.0), axis=0, keepdims=True)
        counts_row = jnp.sum(jnp.where(er == ec, counts, 0.0), axis=0, keepdims=True)
        pad_end = jnp.sum(jnp.where(ec <= er, padded_row, 0.0), axis=1, keepdims=True)
        pad_end_row = jnp.sum(jnp.where(er == ec, pad_end, 0.0), axis=0, keepdims=True)
        start_ref[...] = jnp.broadcast_to(pad_end - padded, start_ref.shape)
        run_ref[...] = jnp.zeros_like(run_ref)
        bstart = (lax.broadcasted_iota(jnp.int32, (N_EXPERTS, META_LANES), 1) * EXPERT_ROWS).astype(f32)
        blk_e = jnp.sum(jnp.where(pad_end <= bstart, 1.0, 0.0), axis=0, keepdims=True)
        blk_e = jnp.minimum(blk_e, N_EXPERTS - 1.0)
        n_used = pad_end[N_EXPERTS - 1:N_EXPERTS, :] * (1.0 / EXPERT_ROWS)
        pad3 = lambda r: jnp.concatenate(
            [r, jnp.zeros((1, META_LANES - LANES), f32)], axis=1)
        msub = lax.broadcasted_iota(jnp.int32, (SUBLANES, META_LANES), 0)
        meta = jnp.where(msub == 0, blk_e,
                         jnp.where(msub == 1, pad3(pad_end_row),
                                   jnp.where(msub == 2, pad3(counts_row),
                                             jnp.where(msub == 3, n_used, 0.0))))
        meta_ref[...] = meta.astype(jnp.int32)

    @pl.when(phase == 1)
    def _():
        tr = lax.broadcasted_iota(jnp.int32, (tb, tb), 0)
        tc = lax.broadcasted_iota(jnp.int32, (tb, tb), 1)
        upper = jnp.where(tr < tc, 1.0, 0.0).astype(jnp.bfloat16)
        prefix = jnp.dot(oh.astype(jnp.bfloat16), upper, preferred_element_type=f32)
        er = lax.broadcasted_iota(jnp.int32, (N_EXPERTS, LANES), 0)
        ec = lax.broadcasted_iota(jnp.int32, (N_EXPERTS, LANES), 1)
        to_row = lambda col: jnp.sum(jnp.where(er == ec, col, 0.0), axis=0, keepdims=True)
        cnt_row = to_row(blk_count)
        lstart = jnp.sum(jnp.where(ec < er, cnt_row, 0.0), axis=1, keepdims=True)
        base = prefix + lstart
        d1 = jnp.sum(jnp.where(oh1, base, 0.0), axis=0, keepdims=True)
        d2 = jnp.sum(jnp.where(oh2, base, 0.0), axis=0, keepdims=True)
        sub = lax.broadcasted_iota(jnp.int32, (SUBLANES, tb), 0)
        ldest_ref[...] = jnp.where(sub == 0, d1, jnp.where(sub == 1, d2, 0.0)).astype(jnp.int32)
        gstart = start_ref[:, 0:1] + run_ref[:, 0:1]
        rsub = lax.broadcasted_iota(jnp.int32, (SUBLANES, LANES), 0)
        runs = jnp.where(rsub == 0, cnt_row,
                         jnp.where(rsub == 1, to_row(lstart), jnp.where(rsub == 2, to_row(gstart), 0.0)))
        runs_ref[...] = runs.astype(jnp.int32)
        run_ref[...] = run_ref[...] + blk_count


def _plan(ri):
    tb = MOE_TOKENS
    return pl.pallas_call(
        _plan_kernel,
        grid=(2, N_TOK // tb),
        in_specs=[pl.BlockSpec((SUBLANES, tb), lambda p, i: (0, i))],
        out_specs=[pl.BlockSpec((SUBLANES, tb), lambda p, i: (0, i * p)),
                   pl.BlockSpec((SUBLANES, LANES), lambda p, i: (i * p, 0)),
                   pl.BlockSpec((SUBLANES, META_LANES), lambda p, i: (0, 0))],
        out_shape=[jax.ShapeDtypeStruct((SUBLANES, N_TOK), jnp.int32),
                   jax.ShapeDtypeStruct((N_MOE_BLOCKS * SUBLANES, LANES), jnp.int32),
                   jax.ShapeDtypeStruct((SUBLANES, META_LANES), jnp.int32)],
        scratch_shapes=[pltpu.VMEM((N_EXPERTS, LANES), jnp.float32),
                        pltpu.VMEM((N_EXPERTS, LANES), jnp.float32),
                        pltpu.VMEM((N_EXPERTS, LANES), jnp.float32)],
        compiler_params=pltpu.CompilerParams(dimension_semantics=("arbitrary", "arbitrary")),
        name="plan",
    )(ri)


def _for_each_run_piece(runs_ref, fn):
    for e in range(N_EXPERTS):
        n, lstart, gstart = runs_ref[0, e], runs_ref[1, e], runs_ref[2, e]
        for bit in range(1, RUN_BITS):
            @pl.when((n & (1 << bit)) != 0)
            def _(n=n, lstart=lstart, gstart=gstart, bit=bit):
                off = (n >> (bit + 1)) << (bit + 1)
                fn(lstart + off, gstart + off, 1 << bit)


def _tile_rows(ref, row, rows):
    return ref.at[pl.ds(pl.multiple_of(row * FEAT_TILES, BF16_TILE_ROWS), rows * FEAT_TILES)]


def _dispatch_kernel(meta_ref, runs_ref, ldest_ref, h1r_ref, buf_ref, stage_ref, sorted_ref, zero_ref,
                     sems, zsem):
    i = pl.program_id(0)
    n_steps = pl.num_programs(0)
    tb = ldest_ref.shape[1]
    slot = i % 2
    block_tiles = BLOCK_ROWS * FEAT_TILES

    def wait_slot(s):
        pltpu.make_async_copy(sorted_ref.at[s], buf_ref.at[pl.ds(0, block_tiles)], sems.at[s]).wait()

    @pl.when(i == 0)
    def _():
        zero_ref[...] = jnp.zeros_like(zero_ref)

        def zero_copy(e):
            start = pl.multiple_of((meta_ref[1, e] - EXPERT_ROWS) * FEAT_TILES, EXPERT_ROWS * FEAT_TILES)
            return pltpu.make_async_copy(
                zero_ref, buf_ref.at[pl.ds(start, EXPERT_ROWS * FEAT_TILES)], zsem)

        def start_zero(e, c):
            @pl.when(meta_ref[2, e] > 0)
            def _():
                zero_copy(e).start()
            return c

        def wait_zero(e, c):
            @pl.when(meta_ref[2, e] > 0)
            def _():
                zero_copy(e).wait()
            return c

        def tail_copy(b):
            start = pl.multiple_of(b * (EXPERT_ROWS * FEAT_TILES), EXPERT_ROWS * FEAT_TILES)
            return pltpu.make_async_copy(
                zero_ref, buf_ref.at[pl.ds(start, EXPERT_ROWS * FEAT_TILES)], zsem)

        def start_tail(b, c):
            tail_copy(b).start()
            return c

        def wait_tail(b, c):
            tail_copy(b).wait()
            return c

        lax.fori_loop(0, N_EXPERTS, start_zero, 0)
        lax.fori_loop(meta_ref[3, 0], N_ROW_BLOCKS, start_tail, 0)
        lax.fori_loop(0, N_EXPERTS, wait_zero, 0)
        lax.fori_loop(meta_ref[3, 0], N_ROW_BLOCKS, wait_tail, 0)

    @pl.when(i >= 2)
    def _():
        wait_slot(slot)

    x = _from_row_tiles(h1r_ref, tb).astype(jnp.bfloat16)
    ld0 = ldest_ref[0:1, :]
    ld1 = ldest_ref[1:2, :]
    for c in range(BLOCK_ROWS // SORT_CHUNK):
        r = lax.broadcasted_iota(jnp.int32, (SORT_CHUNK, tb), 0) + c * SORT_CHUNK
        perm = jnp.where((r == ld0) | (r == ld1), 1.0, 0.0).astype(jnp.bfloat16)
        rows = jnp.dot(perm, x, preferred_element_type=jnp.float32)
        chunk = pl.ds(c * SORT_CHUNK * FEAT_TILES, SORT_CHUNK * FEAT_TILES)
        _to_row_tiles(stage_ref.at[chunk], rows)
        sorted_ref[slot, chunk] = stage_ref[chunk].astype(jnp.bfloat16)

    def send(lrow, grow, rows):
        pltpu.make_async_copy(_tile_rows(sorted_ref.at[slot], lrow, rows),
                              _tile_rows(buf_ref, grow, rows), sems.at[slot]).start()

    _for_each_run_piece(runs_ref, send)

    @pl.when(i == n_steps - 1)
    def _():
        wait_slot(slot)
        wait_slot(1 - slot)


def _dispatch(meta, runs, ldest, h1r):
    tb = MOE_TOKENS
    return pl.pallas_call(
        _dispatch_kernel,
        grid_spec=pltpu.PrefetchScalarGridSpec(
            num_scalar_prefetch=1,
            grid=(N_MOE_BLOCKS,),
            in_specs=[pl.BlockSpec((SUBLANES, LANES), lambda i, m: (i, 0), memory_space=pltpu.SMEM),
                      pl.BlockSpec((SUBLANES, tb), lambda i, m: (0, i)),
                      pl.BlockSpec((tb * FEAT_TILES, LANES), lambda i, m: (i, 0))],
            out_specs=pl.BlockSpec(memory_space=pl.ANY),
            scratch_shapes=[pltpu.VMEM((BLOCK_ROWS * FEAT_TILES, LANES), jnp.float32),
                            pltpu.VMEM((2, BLOCK_ROWS * FEAT_TILES, LANES), jnp.bfloat16),
                            pltpu.VMEM((EXPERT_ROWS * FEAT_TILES, LANES), jnp.bfloat16),
                            pltpu.SemaphoreType.DMA((2,)), pltpu.SemaphoreType.DMA]),
        out_shape=jax.ShapeDtypeStruct((N_ROWS * FEAT_TILES, LANES), jnp.bfloat16),
        compiler_params=pltpu.CompilerParams(
            dimension_semantics=("arbitrary",), vmem_limit_bytes=VMEM_LIMIT),
        name="dispatch",
    )(meta, runs, ldest, h1r)


def _sub_block_expert(meta, i, sub):
    n_used = meta[3, 0]
    step = jnp.minimum(i, (n_used - 1) // EXPERT_SUB)
    return meta[0, jnp.minimum(EXPERT_SUB * step + sub, n_used - 1)]


def _experts_kernel(meta_ref, x_ref, wg0_ref, wu0_ref, wd0_ref, wg1_ref, wu1_ref, wd1_ref, o_ref,
                    wgb_ref, wub_ref, wdb_ref, cached_ref, xin_ref, out_ref):
    n = EXPERT_ROWS
    i = pl.program_id(0)
    n_used = meta_ref[3, 0]
    weights = ((wg0_ref, wu0_ref, wd0_ref), (wg1_ref, wu1_ref, wd1_ref))

    @pl.when(i == 0)
    def _():
        for sub in range(EXPERT_SUB):
            cached_ref[sub] = -1

    for sub in range(EXPERT_SUB):
        expert = _sub_block_expert(meta_ref, i, sub)

        @pl.when(cached_ref[sub] != expert)
        def _(sub=sub, expert=expert):
            wg_ref, wu_ref, wd_ref = weights[sub]
            wgb_ref[sub] = wg_ref[...].astype(jnp.bfloat16)
            wub_ref[sub] = wu_ref[...].astype(jnp.bfloat16)
            wdb_ref[sub] = wd_ref[...].astype(jnp.bfloat16)
            cached_ref[sub] = expert

    @pl.when(EXPERT_SUB * i < n_used)
    def _():
        for sub in range(EXPERT_SUB):
            rows = pl.ds(sub * n * FEAT_TILES, n * FEAT_TILES)
            xin_ref[rows] = x_ref[rows].astype(jnp.float32)
            x = _from_row_tiles(xin_ref.at[rows], n).astype(jnp.bfloat16)
            gate = jnp.dot(x, wgb_ref[sub], preferred_element_type=jnp.float32)
            up = jnp.dot(x, wub_ref[sub], preferred_element_type=jnp.float32)
            act = (gate * jax.nn.sigmoid(gate) * up).astype(jnp.bfloat16)
            _to_row_tiles(out_ref.at[rows], jnp.dot(act, wdb_ref[sub], preferred_element_type=jnp.float32))
            o_ref[rows] = out_ref[rows].astype(jnp.bfloat16)

    @pl.when(EXPERT_SUB * i >= n_used)
    def _():
        o_ref[...] = jnp.zeros_like(o_ref)


def _experts(meta, buf, wg, wu, wd):
    step_rows = EXPERT_SUB * EXPERT_ROWS
    used_step = lambda i, m: jnp.minimum(i, (m[3, 0] - 1) // EXPERT_SUB)

    def weight_spec(shape, sub):
        return pl.BlockSpec((None,) + shape, lambda i, m: (_sub_block_expert(m, i, sub), 0, 0))

    up_shape, down_shape = (D_MODEL, EXPERT_FF), (EXPERT_FF, D_MODEL)
    return pl.pallas_call(
        _experts_kernel,
        grid_spec=pltpu.PrefetchScalarGridSpec(
            num_scalar_prefetch=1,
            grid=(N_ROW_BLOCKS // EXPERT_SUB,),
            in_specs=[pl.BlockSpec((step_rows * FEAT_TILES, LANES), lambda i, m: (used_step(i, m), 0)),
                      weight_spec(up_shape, 0), weight_spec(up_shape, 0), weight_spec(down_shape, 0),
                      weight_spec(up_shape, 1), weight_spec(up_shape, 1), weight_spec(down_shape, 1)],
            out_specs=pl.BlockSpec((step_rows * FEAT_TILES, LANES), lambda i, m: (i, 0)),
            scratch_shapes=[pltpu.VMEM((EXPERT_SUB,) + up_shape, jnp.bfloat16),
                            pltpu.VMEM((EXPERT_SUB,) + up_shape, jnp.bfloat16),
                            pltpu.VMEM((EXPERT_SUB,) + down_shape, jnp.bfloat16),
                            pltpu.SMEM((EXPERT_SUB,), jnp.int32),
                            pltpu.VMEM((step_rows * FEAT_TILES, LANES), jnp.float32),
                            pltpu.VMEM((step_rows * FEAT_TILES, LANES), jnp.float32)]),
        out_shape=jax.ShapeDtypeStruct((N_ROWS * FEAT_TILES, LANES), jnp.bfloat16),
        compiler_params=pltpu.CompilerParams(
            dimension_semantics=("arbitrary",), vmem_limit_bytes=VMEM_LIMIT),
        name="experts",
    )(meta, buf, wg, wu, wd, wg, wu, wd)


def _combine_kernel(runs_ref, runs_next_ref, ldest_ref, rf_ref, h1r_ref, eout_ref, ln2g_ref, ln2b_ref,
                    o_ref, y_ref, stage_ref, sems):
    i = pl.program_id(0)
    n_steps = pl.num_programs(0)
    tb = ldest_ref.shape[1]
    slot = i % 2
    block_tiles = BLOCK_ROWS * FEAT_TILES

    def fetch(table_ref, s):
        def recv(lrow, grow, rows):
            pltpu.make_async_copy(_tile_rows(eout_ref, grow, rows),
                                  _tile_rows(y_ref.at[s], lrow, rows), sems.at[s]).start()
        _for_each_run_piece(table_ref, recv)

    @pl.when(i == 0)
    def _():
        fetch(runs_ref, slot)

    @pl.when(i + 1 < n_steps)
    def _():
        fetch(runs_next_ref, 1 - slot)

    pltpu.make_async_copy(eout_ref.at[pl.ds(0, block_tiles)], y_ref.at[slot], sems.at[slot]).wait()

    ld = ldest_ref[...].astype(jnp.float32).T
    gates = rf_ref[...].T
    y = None
    for c in range(BLOCK_ROWS // SORT_CHUNK):
        col = (lax.broadcasted_iota(jnp.int32, (tb, SORT_CHUNK), 1) + c * SORT_CHUNK).astype(jnp.float32)
        g = (jnp.where(col == ld[:, 0:1], gates[:, 0:1], 0.0)
             + jnp.where(col == ld[:, 1:2], gates[:, 1:2], 0.0)).astype(jnp.bfloat16)
        chunk = pl.ds(c * SORT_CHUNK * FEAT_TILES, SORT_CHUNK * FEAT_TILES)
        stage_ref[chunk] = y_ref[slot, chunk].astype(jnp.float32)
        rows = _from_row_tiles(stage_ref.at[chunk], SORT_CHUNK)
        part = jnp.dot(g, rows.astype(jnp.bfloat16), preferred_element_type=jnp.float32)
        y = part if y is None else y + part
    h1 = _from_row_tiles(h1r_ref, tb)
    o_ref[...] = _layer_norm(ALPHA * h1 + y, ln2g_ref[...], ln2b_ref[...])


def _combine(runs, ldest, rf, h1r, eout, ln2g, ln2b):
    tb = MOE_TOKENS
    last = N_MOE_BLOCKS - 1
    return pl.pallas_call(
        _combine_kernel,
        grid=(N_MOE_BLOCKS,),
        in_specs=[pl.BlockSpec((SUBLANES, LANES), lambda i: (i, 0), memory_space=pltpu.SMEM),
                  pl.BlockSpec((SUBLANES, LANES), lambda i: (jnp.minimum(i + 1, last), 0),
                               memory_space=pltpu.SMEM),
                  pl.BlockSpec((SUBLANES, tb), lambda i: (0, i)),
                  pl.BlockSpec((SUBLANES, tb), lambda i: (0, i)),
                  pl.BlockSpec((tb * FEAT_TILES, LANES), lambda i: (i, 0)),
                  pl.BlockSpec(memory_space=pl.ANY),
                  pl.BlockSpec((1, D_MODEL), lambda i: (0, 0)),
                  pl.BlockSpec((1, D_MODEL), lambda i: (0, 0))],
        out_specs=pl.BlockSpec((tb, D_MODEL), lambda i: (i, 0)),
        out_shape=jax.ShapeDtypeStruct((N_TOK, D_MODEL), jnp.float32),
        scratch_shapes=[pltpu.VMEM((2, BLOCK_ROWS * FEAT_TILES, LANES), jnp.bfloat16),
                        pltpu.VMEM((BLOCK_ROWS * FEAT_TILES, LANES), jnp.float32),
                        pltpu.SemaphoreType.DMA((2,))],
        compiler_params=pltpu.CompilerParams(
            dimension_semantics=("arbitrary",), vmem_limit_bytes=VMEM_LIMIT),
        name="combine",
    )(runs, runs, ldest, rf, h1r, eout, ln2g, ln2b)


def _swap_halves(w):
    half = w.shape[-1] // 2
    return jnp.concatenate([w[..., half:], w[..., :half]], axis=-1)


def kernel(x, positions, ln0_g, ln0_b, w_in, gm_ln_g, gm_ln_b, w_spatial, b_spatial, q_norm_g, w_uq, kv_norm_g, w_ukv, w_out, ln1_g, ln1_b, w_router_group, b_router_group, w_router_expert, b_router_expert, w_gate, w_up, w_down, ln2_g, ln2_b):
    bf16 = jnp.bfloat16
    row = lambda a: a.reshape(1, -1)

    w_in0 = w_in[0]
    kr_cols = w_in0[:, O_KR:O_KR + QK_ROPE_DIM]
    win = jnp.concatenate([w_in0, _swap_halves(kr_cols)], axis=1).astype(bf16)
    wuq3 = w_uq[0].reshape(Q_LORA_RANK, MLA_HEADS, QK_NOPE_DIM + QK_ROPE_DIM)
    rope_cols = wuq3[:, :, QK_NOPE_DIM:]
    wuq = jnp.concatenate([wuq3, _swap_halves(rope_cols)], axis=-1).reshape(Q_LORA_RANK, D_MODEL).astype(bf16)
    wukv = w_ukv[0].astype(bf16)
    wout = w_out[0].astype(bf16)
    bs = jnp.broadcast_to(b_spatial[0][:, :, None], (GM_HEADS, GM_CHUNK, GM_HEAD_DIM))
    wr = jnp.concatenate([w_router_group[0], jnp.zeros((D_MODEL, SUBLANES - N_GROUPS), jnp.float32),
                          w_router_expert[0],
                          jnp.zeros((D_MODEL, LANES - ROUTER_ROWS), jnp.float32)],
                         axis=1).astype(bf16)
    br = jnp.concatenate([b_router_group[0], jnp.zeros((SUBLANES - N_GROUPS,), jnp.float32),
                          b_router_expert[0]]).reshape(ROUTER_ROWS, 1)

    inv_freq = ROPE_THETA ** (-jnp.arange(0, QK_ROPE_DIM, 2, dtype=jnp.float32) / QK_ROPE_DIM)
    freq = jnp.tile(inv_freq, 4).reshape(1, LANES)
    quarter = QK_ROPE_DIM // 2
    phase = jnp.concatenate([jnp.zeros((2 * quarter,), jnp.float32),
                             jnp.full((2 * quarter,), math.pi / 2, jnp.float32)]).reshape(1, LANES)
    sign = jnp.concatenate([jnp.ones((2 * quarter,), jnp.float32), -jnp.ones((quarter,), jnp.float32),
                            jnp.ones((quarter,), jnp.float32)]).reshape(1, LANES)

    x2 = x.reshape(N_TOK, D_MODEL)
    pos2 = positions.reshape(N_TOK, 1)

    h, outa, q, k, v = _prep(x2, pos2, row(ln0_g), row(ln0_b), win, row(gm_ln_g[0]), row(gm_ln_b[0]),
                             w_spatial[0], bs, row(q_norm_g[0]), wuq, row(kv_norm_g[0]), wukv,
                             freq, phase, sign)
    ob = _attn(q, k, v)
    h1r, ri, rf = _proj(outa, ob, h, wout, row(ln1_g[0]), row(ln1_b[0]), wr, br)
    ldest, runs, meta = _plan(ri)
    buf = _dispatch(meta, runs, ldest, h1r)
    eout = _experts(meta, buf, w_gate[0], w_up[0], w_down[0])
    out = _combine(runs, ldest, rf, h1r, eout, row(ln2_g[0]), row(ln2_b[0]))
    return out.reshape(BATCH, SEQ, D_MODEL)
```

```python
import functools
import math

import jax
import jax.numpy as jnp
from jax import lax
from jax.experimental import pallas as pl
from jax.experimental.pallas import tpu as pltpu

D_MODEL = 1024
BATCH = 16
SEQ = 2048
N_TOK = BATCH * SEQ
CHUNK = 64
GM_WIDTH = 512
GM_HEADS = 4
GM_HEAD_DIM = 128
GM_CHUNK = 128
MLA_HEADS = 4
QK_NOPE_DIM = 128
QK_ROPE_DIM = 64
V_HEAD_DIM = 128
Q_LORA_RANK = 384
KV_LORA_RANK = 256
ROPE_THETA = 10000.0
N_GROUPS = 4
EXPERTS_PER_GROUP = 8
N_EXPERTS = 32
TOP_K = 2
EXPERT_FF = 256
ALPHA = 2.0 ** 0.25
QK_SCALE = (QK_NOPE_DIM + QK_ROPE_DIM) ** -0.5 * math.log2(math.e)

LANES = 128
SUBLANES = 8
FEAT_TILES = D_MODEL // LANES
PREP_TOKENS = 512
ATT_Q = 512
ATT_K = 512
N_QBLK = SEQ // ATT_Q
PROJ_TOKENS = 512
MOE_TOKENS = 512
N_MOE_BLOCKS = N_TOK // MOE_TOKENS
RUN_BITS = (TOP_K * MOE_TOKENS).bit_length()
SORT_CHUNK = 256
EXPERT_ROWS = 256
EXPERT_SUB = 2
N_ROWS = N_TOK * TOP_K + N_EXPERTS * EXPERT_ROWS
N_ROW_BLOCKS = N_ROWS // EXPERT_ROWS
META_LANES = 384
IN_COLS = 2 * GM_WIDTH + Q_LORA_RANK + KV_LORA_RANK + 2 * QK_ROPE_DIM
O_Q = 2 * GM_WIDTH
O_KV = O_Q + Q_LORA_RANK
O_KR = O_KV + KV_LORA_RANK
ROUTER_ROWS = 40
VMEM_LIMIT = 48 * 1024 * 1024

assert N_ROW_BLOCKS <= META_LANES and N_ROW_BLOCKS % EXPERT_SUB == 0


def _layer_norm(x, g, b, eps=1e-5):
    mu = jnp.mean(x, axis=-1, keepdims=True)
    xc = x - mu
    var = jnp.mean(xc * xc, axis=-1, keepdims=True)
    return xc * lax.rsqrt(var + eps) * g + b


def _rms_norm(x, g, eps=1e-6):
    return x * lax.rsqrt(jnp.mean(x * x, axis=-1, keepdims=True) + eps) * g


def _gelu_tanh(x):
    c = math.sqrt(2.0 / math.pi)
    return 0.5 * x * (1.0 + jnp.tanh(c * (x + 0.044715 * (x * x * x))))


def _to_row_tiles(ref, x):
    n = x.shape[0]
    for s in range(FEAT_TILES):
        ref[pl.ds(s, n, stride=FEAT_TILES), :] = x[:, s * LANES:(s + 1) * LANES]


def _from_row_tiles(ref, n):
    return jnp.concatenate(
        [ref[pl.ds(s, n, stride=FEAT_TILES), :] for s in range(FEAT_TILES)], axis=-1)


def _prep_kernel(x_ref, pos_ref, ln0g_ref, ln0b_ref, win_ref, gmg_ref, gmb_ref, ws_ref, bs_ref,
                 qg_ref, wuq_ref, kvg_ref, wukv_ref, freq_ref, phase_ref, sign_ref,
                 h_ref, outa_ref, q_ref, k_ref, v_ref, tabc_ref, tabs_ref, rot_ref):
    tb = x_ref.shape[0]

    @pl.when(pl.program_id(0) == 0)
    def _():
        d = lax.broadcasted_iota(jnp.int32, (tb, LANES), 0).astype(jnp.float32) * freq_ref[...]
        tabc_ref[...] = jnp.cos(d)
        tabs_ref[...] = jnp.sin(d)

    pos = pos_ref[...]
    p0 = pos_ref[0:1, :]
    offset = lax.broadcasted_iota(jnp.int32, (tb, 1), 0)
    consecutive = jnp.max(jnp.abs((pos - p0 - offset).astype(jnp.float32))) == 0.0

    @pl.when(consecutive)
    def _():
        a0 = p0.astype(jnp.float32) * freq_ref[...]
        c0, s0 = jnp.cos(a0), jnp.sin(a0)
        lane = lax.broadcasted_iota(jnp.int32, (1, LANES), 1)
        coef_c = jnp.where(lane < 2 * 32, c0, jnp.where(lane < 3 * 32, -s0, s0))
        coef_s = jnp.where(lane < 2 * 32, -s0, jnp.where(lane < 3 * 32, -c0, c0))
        rot_ref[...] = coef_c * tabc_ref[...] + coef_s * tabs_ref[...]

    @pl.when(jnp.logical_not(consecutive))
    def _():
        ang = pos.astype(jnp.float32) * freq_ref[...]
        rot_ref[...] = jnp.cos(ang - phase_ref[...]) * sign_ref[...]

    rot = rot_ref[...]

    h = _layer_norm(x_ref[...], ln0g_ref[...], ln0b_ref[...])
    h_ref[...] = h
    z = jnp.dot(h.astype(jnp.bfloat16), win_ref[...], preferred_element_type=jnp.float32)

    u = _gelu_tanh(z[:, :GM_WIDTH])
    v = _gelu_tanh(z[:, GM_WIDTH:2 * GM_WIDTH])
    row_chunk = lax.broadcasted_iota(jnp.int32, (GM_CHUNK, GM_CHUNK), 0) // CHUNK
    col_chunk = lax.broadcasted_iota(jnp.int32, (GM_CHUNK, GM_CHUNK), 1) // CHUNK
    allowed = col_chunk <= row_chunk
    for hd in range(GM_HEADS):
        lo, hi = hd * GM_HEAD_DIM, (hd + 1) * GM_HEAD_DIM
        vln = _layer_norm(v[:, lo:hi], gmg_ref[:, lo:hi], gmb_ref[:, lo:hi]).astype(jnp.bfloat16)
        wm = jnp.where(allowed, ws_ref[hd], 0.0).astype(jnp.bfloat16)
        for c in range(tb // GM_CHUNK):
            r0, r1 = c * GM_CHUNK, (c + 1) * GM_CHUNK
            f = jnp.dot(wm, vln[r0:r1], preferred_element_type=jnp.float32) + bs_ref[hd]
            outa_ref[r0:r1, lo:hi] = (u[r0:r1, lo:hi] * f).astype(jnp.bfloat16)

    ql = _rms_norm(z[:, O_Q:O_KV], qg_ref[...]).astype(jnp.bfloat16)
    qf = jnp.dot(ql, wuq_ref[...], preferred_element_type=jnp.float32)
    rot_s = rot * QK_SCALE
    q_parts = []
    for hd in range(MLA_HEADS):
        base = hd * 2 * LANES
        q_parts.append(qf[:, base:base + LANES] * QK_SCALE)
        q_parts.append(qf[:, base + LANES:base + 2 * LANES] * rot_s)
    q_ref[...] = jnp.concatenate(q_parts, axis=-1).astype(jnp.bfloat16)

    kvl = _rms_norm(z[:, O_KV:O_KR], kvg_ref[...]).astype(jnp.bfloat16)
    kv = jnp.dot(kvl, wukv_ref[...], preferred_element_type=jnp.float32)
    t = z[:, O_KR:O_KR + LANES] * rot
    krr = t + pltpu.roll(t, 2 * QK_ROPE_DIM // 2, axis=1)
    k_parts, v_parts = [], []
    for hd in range(MLA_HEADS):
        base = hd * 2 * LANES
        k_parts.append(kv[:, base:base + LANES])
        k_parts.append(krr)
        v_parts.append(kv[:, base + LANES:base + 2 * LANES])
    k_ref[...] = jnp.concatenate(k_parts, axis=-1).astype(jnp.bfloat16)
    v_ref[...] = jnp.concatenate(v_parts, axis=-1).astype(jnp.bfloat16)


def _prep(x2, pos2, ln0g, ln0b, win, gmg, gmb, ws, bs, qg, wuq, kvg, wukv, freq, phase, sign):
    tb = PREP_TOKENS
    full = lambda shape: pl.BlockSpec(shape, lambda i: (0,) * len(shape))
    tok = lambda cols: pl.BlockSpec((tb, cols), lambda i: (i, 0))
    return pl.pallas_call(
        _prep_kernel,
        grid=(N_TOK // tb,),
        in_specs=[tok(D_MODEL), tok(1), full((1, D_MODEL)), full((1, D_MODEL)),
                  full((D_MODEL, IN_COLS)), full((1, GM_WIDTH)), full((1, GM_WIDTH)),
                  full((GM_HEADS, GM_CHUNK, GM_CHUNK)), full((GM_HEADS, GM_CHUNK, GM_HEAD_DIM)),
                  full((1, Q_LORA_RANK)), full((Q_LORA_RANK, D_MODEL)),
                  full((1, KV_LORA_RANK)), full((KV_LORA_RANK, D_MODEL)),
                  full((1, LANES)), full((1, LANES)), full((1, LANES))],
        out_specs=[tok(D_MODEL), tok(GM_WIDTH), tok(D_MODEL), tok(D_MODEL), tok(GM_WIDTH)],
        out_shape=[jax.ShapeDtypeStruct((N_TOK, D_MODEL), jnp.float32),
                   jax.ShapeDtypeStruct((N_TOK, GM_WIDTH), jnp.bfloat16),
                   jax.ShapeDtypeStruct((N_TOK, D_MODEL), jnp.bfloat16),
                   jax.ShapeDtypeStruct((N_TOK, D_MODEL), jnp.bfloat16),
                   jax.ShapeDtypeStruct((N_TOK, GM_WIDTH), jnp.bfloat16)],
        scratch_shapes=[pltpu.VMEM((tb, LANES), jnp.float32)] * 3,
        compiler_params=pltpu.CompilerParams(
            dimension_semantics=("arbitrary",), vmem_limit_bytes=VMEM_LIMIT),
        name="prep",
    )(x2, pos2, ln0g, ln0b, win, gmg, gmb, ws, bs, qg, wuq, kvg, wukv, freq, phase, sign)


def _attn_kernel(q_ref, k_ref, v_ref, o_ref, s_ref, mx_ref, ls_ref, acc_ref):
    qi = pl.program_id(1)
    tq = q_ref.shape[0]
    nt = (((1,), (1,)), ((), ()))
    n_kv = SEQ // ATT_K

    half = tq // 2
    row_chunk = (lax.broadcasted_iota(jnp.int32, (half, half), 0)) // CHUNK
    col_chunk = (lax.broadcasted_iota(jnp.int32, (half, half), 1)) // CHUNK
    quad_allowed = col_chunk <= row_chunk
    diag_start = pl.multiple_of(qi * ATT_K, ATT_K)
    masked = jnp.float32(-1e30)

    def lane_tiles(x):
        return [x[:, t * LANES:(t + 1) * LANES] for t in range(x.shape[1] // LANES)]

    def lane_fold(x, op):
        return functools.reduce(op, lane_tiles(x))

    def scores(hd, j):
        start = pl.multiple_of(j * ATT_K, ATT_K)
        q_h = q_ref[:, hd * 2 * LANES:(hd + 1) * 2 * LANES]
        kb = k_ref[pl.ds(start, ATT_K), hd * 2 * LANES:(hd + 1) * 2 * LANES]
        return lax.dot_general(q_h, kb, nt, preferred_element_type=jnp.float32)

    for hd in range(MLA_HEADS):
        cols = slice(hd * 2 * LANES, (hd + 1) * 2 * LANES)
        k_lo = k_ref[pl.ds(diag_start, half), cols]
        k_hi = k_ref[pl.ds(diag_start + half, half), cols]
        qk = lambda q, k: lax.dot_general(q, k, nt, preferred_element_type=jnp.float32)
        s_tl = jnp.where(quad_allowed, qk(q_ref[:half, cols], k_lo), masked)
        s_bl = qk(q_ref[half:, cols], k_lo)
        s_br = jnp.where(quad_allowed, qk(q_ref[half:, cols], k_hi), masked)
        diag = s_ref.at[hd * n_kv + qi]
        diag[:half, :half] = s_tl
        diag[half:, :half] = s_bl
        diag[half:, half:] = s_br
        mx_ref[hd, :half] = lane_fold(s_tl, jnp.maximum)
        mx_ref[hd, half:] = jnp.maximum(lane_fold(s_bl, jnp.maximum), lane_fold(s_br, jnp.maximum))

    def pass_a(j, c):
        for hd in range(MLA_HEADS):
            s = scores(hd, j)
            s_ref[hd * n_kv + j] = s
            mx_ref[hd] = jnp.maximum(mx_ref[hd], lane_fold(s, jnp.maximum))
        return c

    lax.fori_loop(0, qi, pass_a, 0)

    for hd in range(MLA_HEADS):
        mx_ref[hd] = jnp.broadcast_to(jnp.max(mx_ref[hd], axis=-1, keepdims=True), (tq, LANES))

    def probs(hd, j):
        s = s_ref[hd * n_kv + j]
        mb = mx_ref[hd]
        p = jnp.exp2(jnp.concatenate([t - mb for t in lane_tiles(s)], axis=-1))
        start = pl.multiple_of(j * ATT_K, ATT_K)
        vb = v_ref[pl.ds(start, ATT_K), hd * LANES:(hd + 1) * LANES]
        pv = jnp.dot(p.astype(jnp.bfloat16), vb, preferred_element_type=jnp.float32)
        return lane_fold(p, jnp.add), pv

    for hd in range(MLA_HEADS):
        diag = s_ref.at[hd * n_kv + qi]
        vcols = slice(hd * LANES, (hd + 1) * LANES)
        p_top = jnp.exp2(jnp.concatenate([t - mx_ref[hd, :half] for t in lane_tiles(diag[:half, :half])], axis=-1))
        p_bot = jnp.exp2(jnp.concatenate([t - mx_ref[hd, half:] for t in lane_tiles(diag[half:, :])], axis=-1))
        ls_ref[hd, :half] = lane_fold(p_top, jnp.add)
        ls_ref[hd, half:] = lane_fold(p_bot, jnp.add)
        acc_ref[hd, :half] = jnp.dot(p_top.astype(jnp.bfloat16), v_ref[pl.ds(diag_start, half), vcols],
                                     preferred_element_type=jnp.float32)
        acc_ref[hd, half:] = jnp.dot(p_bot.astype(jnp.bfloat16), v_ref[pl.ds(diag_start, ATT_K), vcols],
                                     preferred_element_type=jnp.float32)

    def pass_b(j, c):
        for hd in range(MLA_HEADS):
            ls, pv = probs(hd, j)
            ls_ref[hd] = ls_ref[hd] + ls
            acc_ref[hd] = acc_ref[hd] + pv
        return c

    lax.fori_loop(0, qi, pass_b, 0)

    for hd in range(MLA_HEADS):
        l = jnp.sum(ls_ref[hd], axis=-1, keepdims=True)
        o_ref[:, hd * V_HEAD_DIM:(hd + 1) * V_HEAD_DIM] = (acc_ref[hd] / l).astype(jnp.bfloat16)


def _attn(q, k, v):
    tq = ATT_Q
    tokblk = lambda cols: pl.BlockSpec((tq, cols), lambda b, i: (b * N_QBLK + i, 0))
    seqblk = lambda cols: pl.BlockSpec((SEQ, cols), lambda b, i: (b, 0))
    return pl.pallas_call(
        _attn_kernel,
        grid=(BATCH, N_QBLK),
        in_specs=[tokblk(D_MODEL), seqblk(D_MODEL), seqblk(GM_WIDTH)],
        out_specs=tokblk(GM_WIDTH),
        out_shape=jax.ShapeDtypeStruct((N_TOK, MLA_HEADS * V_HEAD_DIM), jnp.bfloat16),
        scratch_shapes=[pltpu.VMEM((MLA_HEADS * (SEQ // ATT_K), tq, ATT_K), jnp.float32),
                        pltpu.VMEM((MLA_HEADS, tq, LANES), jnp.float32),
                        pltpu.VMEM((MLA_HEADS, tq, LANES), jnp.float32),
                        pltpu.VMEM((MLA_HEADS, tq, V_HEAD_DIM), jnp.float32)],
        compiler_params=pltpu.CompilerParams(
            dimension_semantics=("arbitrary", "arbitrary"), vmem_limit_bytes=VMEM_LIMIT),
        name="attn",
    )(q, k, v)


def _proj_kernel(outa_ref, ob_ref, h_ref, wout_ref, ln1g_ref, ln1b_ref, wr_ref, br_ref,
                 h1r_ref, ri_ref, rf_ref, proj_ref):
    i = pl.program_id(0)
    tb = outa_ref.shape[0]

    @pl.when(i == 0)
    def _():
        proj_ref[...] = jnp.zeros_like(proj_ref)

    h1 = _layer_norm(ALPHA * h_ref[...] + proj_ref[(i + 1) % 2], ln1g_ref[...], ln1b_ref[...])
    _to_row_tiles(h1r_ref, h1)

    logits_tm = jnp.dot(h1.astype(jnp.bfloat16), wr_ref[...], preferred_element_type=jnp.float32)
    logits = logits_tm.T[0:ROUTER_ROWS] + br_ref[...]

    sub_i = lax.broadcasted_iota(jnp.int32, (SUBLANES, tb), 0)
    sub = sub_i.astype(jnp.float32)
    neg = jnp.float32(-jnp.inf)
    g = jnp.where(sub_i < N_GROUPS, logits[0:SUBLANES], neg)
    gmax = jnp.max(g, axis=0, keepdims=True)
    g_top = jnp.min(jnp.where(g == gmax, sub, float(SUBLANES)), axis=0, keepdims=True)
    p_group = 1.0 / jnp.sum(jnp.exp(g - gmax), axis=0, keepdims=True)
    sel = logits[SUBLANES:2 * SUBLANES]
    for grp in range(1, N_GROUPS):
        sel = jnp.where(g_top == float(grp), logits[(grp + 1) * SUBLANES:(grp + 2) * SUBLANES], sel)
    v1 = jnp.max(sel, axis=0, keepdims=True)
    i1 = jnp.min(jnp.where(sel == v1, sub, float(SUBLANES)), axis=0, keepdims=True)
    sel2 = jnp.where(sub == i1, neg, sel)
    v2 = jnp.max(sel2, axis=0, keepdims=True)
    i2 = jnp.min(jnp.where(sel2 == v2, sub, float(SUBLANES)), axis=0, keepdims=True)
    e21 = jnp.exp(v2 - v1)
    w1 = 1.0 / (1.0 + e21)
    gate1 = p_group * w1
    gate2 = p_group * (e21 * w1)
    e1 = g_top * EXPERTS_PER_GROUP + i1
    e2 = g_top * EXPERTS_PER_GROUP + i2
    ri_ref[...] = jnp.where(sub_i == 0, e1, jnp.where(sub_i == 1, e2, 0.0)).astype(jnp.int32)
    rf_ref[...] = jnp.where(sub_i == 0, gate1, jnp.where(sub_i == 1, gate2, 0.0))

    proj_ref[i % 2] = (jnp.dot(outa_ref[...], wout_ref[:GM_WIDTH, :], preferred_element_type=jnp.float32)
                       + jnp.dot(ob_ref[...], wout_ref[GM_WIDTH:, :], preferred_element_type=jnp.float32))


def _proj(outa, ob, h, wout, ln1g, ln1b, wr, br):
    tb = PROJ_TOKENS
    n_blk = N_TOK // tb
    cur = lambda i: jnp.minimum(i, n_blk - 1)
    prev = lambda i: jnp.maximum(i - 1, 0)
    full = lambda shape: pl.BlockSpec(shape, lambda i: (0,) * len(shape))
    return pl.pallas_call(
        _proj_kernel,
        grid=(n_blk + 1,),
        in_specs=[pl.BlockSpec((tb, GM_WIDTH), lambda i: (cur(i), 0)),
                  pl.BlockSpec((tb, GM_WIDTH), lambda i: (cur(i), 0)),
                  pl.BlockSpec((tb, D_MODEL), lambda i: (prev(i), 0)),
                  full((D_MODEL, D_MODEL)), full((1, D_MODEL)), full((1, D_MODEL)),
                  full((D_MODEL, LANES)), full((ROUTER_ROWS, 1))],
        out_specs=[pl.BlockSpec((tb * FEAT_TILES, LANES), lambda i: (prev(i), 0)),
                   pl.BlockSpec((SUBLANES, tb), lambda i: (0, prev(i))),
                   pl.BlockSpec((SUBLANES, tb), lambda i: (0, prev(i)))],
        out_shape=[jax.ShapeDtypeStruct((N_TOK * FEAT_TILES, LANES), jnp.float32),
                   jax.ShapeDtypeStruct((SUBLANES, N_TOK), jnp.int32),
                   jax.ShapeDtypeStruct((SUBLANES, N_TOK), jnp.float32)],
        scratch_shapes=[pltpu.VMEM((2, tb, D_MODEL), jnp.float32)],
        compiler_params=pltpu.CompilerParams(
            dimension_semantics=("arbitrary",), vmem_limit_bytes=VMEM_LIMIT),
        name="proj",
    )(outa, ob, h, wout, ln1g, ln1b, wr, br)


def _plan_kernel(ri_all_ref, ri_ref, ldest_ref, runs_ref, meta_ref, run_ref, start_ref, upper_ref):
    step = pl.program_id(0)
    tb = ri_ref.shape[1]
    f32 = jnp.float32
    er = lax.broadcasted_iota(jnp.int32, (N_EXPERTS, LANES), 0)
    ec = lax.broadcasted_iota(jnp.int32, (N_EXPERTS, LANES), 1)
    to_row = lambda col: jnp.sum(jnp.where(er == ec, col, 0.0), axis=0, keepdims=True)

    def expert_one_hot(ref):
        e_sub = lax.broadcasted_iota(jnp.int32, (N_EXPERTS, ref.shape[1]), 0)
        return e_sub == ref[0:1, :], e_sub == ref[1:2, :]

    @pl.when(step == 0)
    def _():
        oh1, oh2 = expert_one_hot(ri_all_ref)
        counts = jnp.sum(jnp.where(oh1 | oh2, 1.0, 0.0), axis=1, keepdims=True)
        padded = jnp.floor((counts + (EXPERT_ROWS - 1)) * (1.0 / EXPERT_ROWS)) * EXPERT_ROWS
        pad_end = jnp.sum(jnp.where(ec <= er, to_row(padded), 0.0), axis=1, keepdims=True)
        start_ref[...] = jnp.broadcast_to(pad_end - padded, start_ref.shape)
        run_ref[...] = jnp.zeros_like(run_ref)
        bstart = (lax.broadcasted_iota(jnp.int32, (N_EXPERTS, META_LANES), 1) * EXPERT_ROWS).astype(f32)
        blk_e = jnp.sum(jnp.where(pad_end <= bstart, 1.0, 0.0), axis=0, keepdims=True)
        blk_e = jnp.minimum(blk_e, N_EXPERTS - 1.0)
        n_used = pad_end[N_EXPERTS - 1:N_EXPERTS, :] * (1.0 / EXPERT_ROWS)
        pad3 = lambda r: jnp.concatenate(
            [r, jnp.zeros((1, META_LANES - LANES), f32)], axis=1)
        msub = lax.broadcasted_iota(jnp.int32, (SUBLANES, META_LANES), 0)
        meta = jnp.where(msub == 0, blk_e,
                         jnp.where(msub == 1, pad3(to_row(pad_end)),
                                   jnp.where(msub == 2, pad3(to_row(counts)),
                                             jnp.where(msub == 3, n_used, 0.0))))
        meta_ref[...] = meta.astype(jnp.int32)
        tr = lax.broadcasted_iota(jnp.int32, (tb, tb), 0)
        tc = lax.broadcasted_iota(jnp.int32, (tb, tb), 1)
        upper_ref[...] = jnp.where(tr < tc, 1.0, 0.0).astype(jnp.bfloat16)

    @pl.when(step > 0)
    def _():
        oh1, oh2 = expert_one_hot(ri_ref)
        oh = jnp.where(oh1 | oh2, 1.0, 0.0).astype(f32)
        blk_count = jnp.sum(oh, axis=1, keepdims=True)
        prefix = jnp.dot(oh.astype(jnp.bfloat16), upper_ref[...], preferred_element_type=f32)
        cnt_row = to_row(blk_count)
        lstart = jnp.sum(jnp.where(ec < er, cnt_row, 0.0), axis=1, keepdims=True)
        base = prefix + lstart
        d1 = jnp.sum(jnp.where(oh1, base, 0.0), axis=0, keepdims=True)
        d2 = jnp.sum(jnp.where(oh2, base, 0.0), axis=0, keepdims=True)
        sub = lax.broadcasted_iota(jnp.int32, (SUBLANES, tb), 0)
        ldest_ref[...] = jnp.where(sub == 0, d1, jnp.where(sub == 1, d2, 0.0)).astype(jnp.int32)
        gstart = start_ref[:, 0:1] + run_ref[:, 0:1]
        rsub = lax.broadcasted_iota(jnp.int32, (SUBLANES, LANES), 0)
        runs = jnp.where(rsub == 0, cnt_row,
                         jnp.where(rsub == 1, to_row(lstart), jnp.where(rsub == 2, to_row(gstart), 0.0)))
        runs_ref[...] = runs.astype(jnp.int32)
        run_ref[...] = run_ref[...] + blk_count


def _plan(ri):
    tb = MOE_TOKENS
    blk = lambda i: jnp.maximum(i - 1, 0)
    return pl.pallas_call(
        _plan_kernel,
        grid=(N_MOE_BLOCKS + 1,),
        in_specs=[pl.BlockSpec((SUBLANES, N_TOK), lambda i: (0, 0)),
                  pl.BlockSpec((SUBLANES, tb), lambda i: (0, blk(i)))],
        out_specs=[pl.BlockSpec((SUBLANES, tb), lambda i: (0, blk(i))),
                   pl.BlockSpec((SUBLANES, LANES), lambda i: (blk(i), 0)),
                   pl.BlockSpec((SUBLANES, META_LANES), lambda i: (0, 0))],
        out_shape=[jax.ShapeDtypeStruct((SUBLANES, N_TOK), jnp.int32),
                   jax.ShapeDtypeStruct((N_MOE_BLOCKS * SUBLANES, LANES), jnp.int32),
                   jax.ShapeDtypeStruct((SUBLANES, META_LANES), jnp.int32)],
        scratch_shapes=[pltpu.VMEM((N_EXPERTS, LANES), jnp.float32),
                        pltpu.VMEM((N_EXPERTS, LANES), jnp.float32),
                        pltpu.VMEM((tb, tb), jnp.bfloat16)],
        compiler_params=pltpu.CompilerParams(
            dimension_semantics=("arbitrary",), vmem_limit_bytes=VMEM_LIMIT),
        name="plan",
    )(ri, ri)


def _for_each_run_piece(runs_ref, fn):
    for e in range(N_EXPERTS):
        n, lstart, gstart = runs_ref[0, e], runs_ref[1, e], runs_ref[2, e]
        for bit in range(RUN_BITS):
            @pl.when((n & (1 << bit)) != 0)
            def _(n=n, lstart=lstart, gstart=gstart, bit=bit):
                off = (n >> (bit + 1)) << (bit + 1)
                fn(lstart + off, gstart + off, 1 << bit)


def _tile_rows(ref, row, rows):
    return ref.at[pl.ds(pl.multiple_of(row * FEAT_TILES, FEAT_TILES), rows * FEAT_TILES)]


def _dispatch_kernel(meta_ref, runs_ref, ldest_ref, h1r_ref, buf_ref, sorted_ref, zero_ref, sems, zsem):
    i = pl.program_id(0)
    n_steps = pl.num_programs(0)
    tb = ldest_ref.shape[1]
    slot = i % 2
    block_tiles = 2 * tb * FEAT_TILES

    def wait_slot(s):
        pltpu.make_async_copy(sorted_ref.at[s], buf_ref.at[pl.ds(0, block_tiles)], sems.at[s]).wait()

    @pl.when(i == 0)
    def _():
        zero_ref[...] = jnp.zeros_like(zero_ref)

        def zero_copy(e):
            start = pl.multiple_of((meta_ref[1, e] - EXPERT_ROWS) * FEAT_TILES, EXPERT_ROWS * FEAT_TILES)
            return pltpu.make_async_copy(
                zero_ref, buf_ref.at[pl.ds(start, EXPERT_ROWS * FEAT_TILES)], zsem)

        def start_zero(e, c):
            @pl.when(meta_ref[2, e] > 0)
            def _():
                zero_copy(e).start()
            return c

        def wait_zero(e, c):
            @pl.when(meta_ref[2, e] > 0)
            def _():
                zero_copy(e).wait()
            return c

        def tail_copy(b):
            start = pl.multiple_of(b * (EXPERT_ROWS * FEAT_TILES), EXPERT_ROWS * FEAT_TILES)
            return pltpu.make_async_copy(
                zero_ref, buf_ref.at[pl.ds(start, EXPERT_ROWS * FEAT_TILES)], zsem)

        def start_tail(b, c):
            tail_copy(b).start()
            return c

        def wait_tail(b, c):
            tail_copy(b).wait()
            return c

        lax.fori_loop(0, N_EXPERTS, start_zero, 0)
        lax.fori_loop(meta_ref[3, 0], N_ROW_BLOCKS, start_tail, 0)
        lax.fori_loop(0, N_EXPERTS, wait_zero, 0)
        lax.fori_loop(meta_ref[3, 0], N_ROW_BLOCKS, wait_tail, 0)

    @pl.when(i >= 2)
    def _():
        wait_slot(slot)

    x = _from_row_tiles(h1r_ref, tb).astype(jnp.bfloat16)
    ld0 = ldest_ref[0:1, :]
    ld1 = ldest_ref[1:2, :]
    for c in range(2 * tb // SORT_CHUNK):
        r = lax.broadcasted_iota(jnp.int32, (SORT_CHUNK, tb), 0) + c * SORT_CHUNK
        perm = jnp.where((r == ld0) | (r == ld1), 1.0, 0.0).astype(jnp.bfloat16)
        rows = jnp.dot(perm, x, preferred_element_type=jnp.float32)
        _to_row_tiles(sorted_ref.at[slot, pl.ds(c * SORT_CHUNK * FEAT_TILES, SORT_CHUNK * FEAT_TILES)], rows)

    def send(lrow, grow, rows):
        pltpu.make_async_copy(_tile_rows(sorted_ref.at[slot], lrow, rows),
                              _tile_rows(buf_ref, grow, rows), sems.at[slot]).start()

    _for_each_run_piece(runs_ref, send)

    @pl.when(i == n_steps - 1)
    def _():
        wait_slot(slot)
        wait_slot(1 - slot)


def _dispatch(meta, runs, ldest, h1r):
    tb = MOE_TOKENS
    return pl.pallas_call(
        _dispatch_kernel,
        grid_spec=pltpu.PrefetchScalarGridSpec(
            num_scalar_prefetch=1,
            grid=(N_MOE_BLOCKS,),
            in_specs=[pl.BlockSpec((SUBLANES, LANES), lambda i, m: (i, 0), memory_space=pltpu.SMEM),
                      pl.BlockSpec((SUBLANES, tb), lambda i, m: (0, i)),
                      pl.BlockSpec((tb * FEAT_TILES, LANES), lambda i, m: (i, 0))],
            out_specs=pl.BlockSpec(memory_space=pl.ANY),
            scratch_shapes=[pltpu.VMEM((2, 2 * tb * FEAT_TILES, LANES), jnp.float32),
                            pltpu.VMEM((EXPERT_ROWS * FEAT_TILES, LANES), jnp.float32),
                            pltpu.SemaphoreType.DMA((2,)), pltpu.SemaphoreType.DMA]),
        out_shape=jax.ShapeDtypeStruct((N_ROWS * FEAT_TILES, LANES), jnp.float32),
        compiler_params=pltpu.CompilerParams(
            dimension_semantics=("arbitrary",), vmem_limit_bytes=VMEM_LIMIT),
        name="dispatch",
    )(meta, runs, ldest, h1r)


def _sub_block_expert(meta, i, sub):
    n_used = meta[3, 0]
    step = jnp.minimum(i, (n_used - 1) // EXPERT_SUB)
    return meta[0, jnp.minimum(EXPERT_SUB * step + sub, n_used - 1)]


def _experts_kernel(meta_ref, x_ref, wg0_ref, wu0_ref, wd0_ref, wg1_ref, wu1_ref, wd1_ref, o_ref,
                    wgb_ref, wub_ref, wdb_ref, cached_ref):
    n = EXPERT_ROWS
    i = pl.program_id(0)
    n_used = meta_ref[3, 0]
    weights = ((wg0_ref, wu0_ref, wd0_ref), (wg1_ref, wu1_ref, wd1_ref))

    @pl.when(i == 0)
    def _():
        for sub in range(EXPERT_SUB):
            cached_ref[sub] = -1

    for sub in range(EXPERT_SUB):
        expert = _sub_block_expert(meta_ref, i, sub)

        @pl.when(cached_ref[sub] != expert)
        def _(sub=sub, expert=expert):
            wg_ref, wu_ref, wd_ref = weights[sub]
            wgb_ref[sub] = wg_ref[...].astype(jnp.bfloat16)
            wub_ref[sub] = wu_ref[...].astype(jnp.bfloat16)
            wdb_ref[sub] = wd_ref[...].astype(jnp.bfloat16)
            cached_ref[sub] = expert

    @pl.when(EXPERT_SUB * i < n_used)
    def _():
        for sub in range(EXPERT_SUB):
            rows = pl.ds(sub * n * FEAT_TILES, n * FEAT_TILES)
            x = _from_row_tiles(x_ref.at[rows], n).astype(jnp.bfloat16)
            gate = jnp.dot(x, wgb_ref[sub], preferred_element_type=jnp.float32)
            up = jnp.dot(x, wub_ref[sub], preferred_element_type=jnp.float32)
            act = (gate * jax.nn.sigmoid(gate) * up).astype(jnp.bfloat16)
            _to_row_tiles(o_ref.at[rows], jnp.dot(act, wdb_ref[sub], preferred_element_type=jnp.float32))

    @pl.when(EXPERT_SUB * i >= n_used)
    def _():
        o_ref[...] = jnp.zeros_like(o_ref)


def _experts(meta, buf, wg, wu, wd):
    step_rows = EXPERT_SUB * EXPERT_ROWS
    used_step = lambda i, m: jnp.minimum(i, (m[3, 0] - 1) // EXPERT_SUB)

    def weight_spec(shape, sub):
        return pl.BlockSpec((None,) + shape, lambda i, m: (_sub_block_expert(m, i, sub), 0, 0))

    up_shape, down_shape = (D_MODEL, EXPERT_FF), (EXPERT_FF, D_MODEL)
    return pl.pallas_call(
        _experts_kernel,
        grid_spec=pltpu.PrefetchScalarGridSpec(
            num_scalar_prefetch=1,
            grid=(N_ROW_BLOCKS // EXPERT_SUB,),
            in_specs=[pl.BlockSpec((step_rows * FEAT_TILES, LANES), lambda i, m: (used_step(i, m), 0)),
                      weight_spec(up_shape, 0), weight_spec(up_shape, 0), weight_spec(down_shape, 0),
                      weight_spec(up_shape, 1), weight_spec(up_shape, 1), weight_spec(down_shape, 1)],
            out_specs=pl.BlockSpec((step_rows * FEAT_TILES, LANES), lambda i, m: (i, 0)),
            scratch_shapes=[pltpu.VMEM((EXPERT_SUB,) + up_shape, jnp.bfloat16),
                            pltpu.VMEM((EXPERT_SUB,) + up_shape, jnp.bfloat16),
                            pltpu.VMEM((EXPERT_SUB,) + down_shape, jnp.bfloat16),
                            pltpu.SMEM((EXPERT_SUB,), jnp.int32)]),
        out_shape=jax.ShapeDtypeStruct((N_ROWS * FEAT_TILES, LANES), jnp.float32),
        compiler_params=pltpu.CompilerParams(
            dimension_semantics=("arbitrary",), vmem_limit_bytes=VMEM_LIMIT),
        name="experts",
    )(meta, buf, wg, wu, wd, wg, wu, wd)


def _combine_kernel(runs_ref, runs_next_ref, ldest_ref, rf_ref, h1r_ref, eout_ref, ln2g_ref, ln2b_ref,
                    o_ref, y_ref, sems):
    i = pl.program_id(0)
    n_steps = pl.num_programs(0)
    tb = ldest_ref.shape[1]
    slot = i % 2
    block_tiles = 2 * tb * FEAT_TILES

    def fetch(table_ref, s):
        def recv(lrow, grow, rows):
            pltpu.make_async_copy(_tile_rows(eout_ref, grow, rows),
                                  _tile_rows(y_ref.at[s], lrow, rows), sems.at[s]).start()
        _for_each_run_piece(table_ref, recv)

    @pl.when(i == 0)
    def _():
        fetch(runs_ref, slot)

    @pl.when(i + 1 < n_steps)
    def _():
        fetch(runs_next_ref, 1 - slot)

    pltpu.make_async_copy(eout_ref.at[pl.ds(0, block_tiles)], y_ref.at[slot], sems.at[slot]).wait()

    ld = ldest_ref[...].astype(jnp.float32).T
    gates = rf_ref[...].T
    y = None
    col = lax.broadcasted_iota(jnp.int32, (tb, SORT_CHUNK), 1).astype(jnp.float32)
    for c in range(2 * tb // SORT_CHUNK):
        ld_c = ld - float(c * SORT_CHUNK)
        g = jnp.where(col == ld_c[:, 0:1], gates[:, 0:1],
                      jnp.where(col == ld_c[:, 1:2], gates[:, 1:2], 0.0)).astype(jnp.bfloat16)
        rows = _from_row_tiles(
            y_ref.at[slot, pl.ds(c * SORT_CHUNK * FEAT_TILES, SORT_CHUNK * FEAT_TILES)], SORT_CHUNK)
        part = jnp.dot(g, rows.astype(jnp.bfloat16), preferred_element_type=jnp.float32)
        y = part if y is None else y + part
    h1 = _from_row_tiles(h1r_ref, tb)
    o_ref[...] = _layer_norm(ALPHA * h1 + y, ln2g_ref[...], ln2b_ref[...])


def _combine(runs, ldest, rf, h1r, eout, ln2g, ln2b):
    tb = MOE_TOKENS
    last = N_MOE_BLOCKS - 1
    return pl.pallas_call(
        _combine_kernel,
        grid=(N_MOE_BLOCKS,),
        in_specs=[pl.BlockSpec((SUBLANES, LANES), lambda i: (i, 0), memory_space=pltpu.SMEM),
                  pl.BlockSpec((SUBLANES, LANES), lambda i: (jnp.minimum(i + 1, last), 0),
                               memory_space=pltpu.SMEM),
                  pl.BlockSpec((SUBLANES, tb), lambda i: (0, i)),
                  pl.BlockSpec((SUBLANES, tb), lambda i: (0, i)),
                  pl.BlockSpec((tb * FEAT_TILES, LANES), lambda i: (i, 0)),
                  pl.BlockSpec(memory_space=pl.ANY),
                  pl.BlockSpec((1, D_MODEL), lambda i: (0, 0)),
                  pl.BlockSpec((1, D_MODEL), lambda i: (0, 0))],
        out_specs=pl.BlockSpec((tb, D_MODEL), lambda i: (i, 0)),
        out_shape=jax.ShapeDtypeStruct((N_TOK, D_MODEL), jnp.float32),
        scratch_shapes=[pltpu.VMEM((2, 2 * tb * FEAT_TILES, LANES), jnp.float32),
                        pltpu.SemaphoreType.DMA((2,))],
        compiler_params=pltpu.CompilerParams(
            dimension_semantics=("arbitrary",), vmem_limit_bytes=VMEM_LIMIT),
        name="combine",
    )(runs, runs, ldest, rf, h1r, eout, ln2g, ln2b)


def _swap_halves(w):
    half = w.shape[-1] // 2
    return jnp.concatenate([w[..., half:], w[..., :half]], axis=-1)


def kernel(x, positions, ln0_g, ln0_b, w_in, gm_ln_g, gm_ln_b, w_spatial, b_spatial, q_norm_g, w_uq, kv_norm_g, w_ukv, w_out, ln1_g, ln1_b, w_router_group, b_router_group, w_router_expert, b_router_expert, w_gate, w_up, w_down, ln2_g, ln2_b):
    bf16 = jnp.bfloat16
    row = lambda a: a.reshape(1, -1)

    w_in0 = w_in[0]
    kr_cols = w_in0[:, O_KR:O_KR + QK_ROPE_DIM]
    win = jnp.concatenate([w_in0, _swap_halves(kr_cols)], axis=1).astype(bf16)
    wuq3 = w_uq[0].reshape(Q_LORA_RANK, MLA_HEADS, QK_NOPE_DIM + QK_ROPE_DIM)
    rope_cols = wuq3[:, :, QK_NOPE_DIM:]
    wuq = jnp.concatenate([wuq3, _swap_halves(rope_cols)], axis=-1).reshape(Q_LORA_RANK, D_MODEL).astype(bf16)
    wukv = w_ukv[0].astype(bf16)
    wout = w_out[0].astype(bf16)
    bs = jnp.broadcast_to(b_spatial[0][:, :, None], (GM_HEADS, GM_CHUNK, GM_HEAD_DIM))
    wr = jnp.concatenate([w_router_group[0], jnp.zeros((D_MODEL, SUBLANES - N_GROUPS), jnp.float32),
                          w_router_expert[0],
                          jnp.zeros((D_MODEL, LANES - ROUTER_ROWS), jnp.float32)],
                         axis=1).astype(bf16)
    br = jnp.concatenate([b_router_group[0], jnp.zeros((SUBLANES - N_GROUPS,), jnp.float32),
                          b_router_expert[0]]).reshape(ROUTER_ROWS, 1)

    inv_freq = ROPE_THETA ** (-jnp.arange(0, QK_ROPE_DIM, 2, dtype=jnp.float32) / QK_ROPE_DIM)
    freq = jnp.tile(inv_freq, 4).reshape(1, LANES)
    quarter = QK_ROPE_DIM // 2
    phase = jnp.concatenate([jnp.zeros((2 * quarter,), jnp.float32),
                             jnp.full((2 * quarter,), math.pi / 2, jnp.float32)]).reshape(1, LANES)
    sign = jnp.concatenate([jnp.ones((2 * quarter,), jnp.float32), -jnp.ones((quarter,), jnp.float32),
                            jnp.ones((quarter,), jnp.float32)]).reshape(1, LANES)

    x2 = x.reshape(N_TOK, D_MODEL)
    pos2 = positions.reshape(N_TOK, 1)

    h, outa, q, k, v = _prep(x2, pos2, row(ln0_g), row(ln0_b), win, row(gm_ln_g[0]), row(gm_ln_b[0]),
                             w_spatial[0], bs, row(q_norm_g[0]), wuq, row(kv_norm_g[0]), wukv,
                             freq, phase, sign)
    ob = _attn(q, k, v)
    h1r, ri, rf = _proj(outa, ob, h, wout, row(ln1_g[0]), row(ln1_b[0]), wr, br)
    ldest, runs, meta = _plan(ri)
    buf = _dispatch(meta, runs, ldest, h1r)
    eout = _experts(meta, buf, w_gate[0], w_up[0], w_down[0])
    out = _combine(runs, ldest, rf, h1r, eout, row(ln2_g[0]), row(ln2_b[0]))
    return out.reshape(BATCH, SEQ, D_MODEL)
```

```python
import functools
import math

import jax
import jax.numpy as jnp
from jax import lax
from jax.experimental import pallas as pl
from jax.experimental.pallas import tpu as pltpu

D_MODEL = 1024
BATCH = 16
SEQ = 2048
N_TOK = BATCH * SEQ
CHUNK = 64
GM_WIDTH = 512
GM_HEADS = 4
GM_HEAD_DIM = 128
GM_CHUNK = 128
MLA_HEADS = 4
QK_NOPE_DIM = 128
QK_ROPE_DIM = 64
V_HEAD_DIM = 128
Q_LORA_RANK = 384
KV_LORA_RANK = 256
ROPE_THETA = 10000.0
N_GROUPS = 4
EXPERTS_PER_GROUP = 8
N_EXPERTS = 32
TOP_K = 2
EXPERT_FF = 256
ALPHA = 2.0 ** 0.25
QK_SCALE = (QK_NOPE_DIM + QK_ROPE_DIM) ** -0.5 * math.log2(math.e)

LANES = 128
SUBLANES = 8
FEAT_TILES = D_MODEL // LANES
PREP_TOKENS = 512
ATT_Q = 512
ATT_K = 512
N_QBLK = SEQ // ATT_Q
PROJ_TOKENS = 512
MOE_TOKENS = 512
N_MOE_BLOCKS = N_TOK // MOE_TOKENS
RUN_BITS = (TOP_K * MOE_TOKENS).bit_length()
SORT_CHUNK = 256
EXPERT_ROWS = 256
EXPERT_SUB = 2
N_ROWS = N_TOK * TOP_K + N_EXPERTS * EXPERT_ROWS
N_ROW_BLOCKS = N_ROWS // EXPERT_ROWS
META_LANES = 384
IN_COLS = 2 * GM_WIDTH + Q_LORA_RANK + KV_LORA_RANK + 2 * QK_ROPE_DIM
O_Q = 2 * GM_WIDTH
O_KV = O_Q + Q_LORA_RANK
O_KR = O_KV + KV_LORA_RANK
ROUTER_ROWS = 40
VMEM_LIMIT = 48 * 1024 * 1024

assert N_ROW_BLOCKS <= META_LANES and N_ROW_BLOCKS % EXPERT_SUB == 0


def _layer_norm(x, g, b, eps=1e-5):
    mu = jnp.mean(x, axis=-1, keepdims=True)
    xc = x - mu
    var = jnp.mean(xc * xc, axis=-1, keepdims=True)
    return xc * lax.rsqrt(var + eps) * g + b


def _rms_norm(x, g, eps=1e-6):
    return x * lax.rsqrt(jnp.mean(x * x, axis=-1, keepdims=True) + eps) * g


def _gelu_tanh(x):
    c = math.sqrt(2.0 / math.pi)
    return 0.5 * x * (1.0 + jnp.tanh(c * (x + 0.044715 * (x * x * x))))


def _to_row_tiles(ref, x):
    n = x.shape[0]
    for s in range(FEAT_TILES):
        ref[pl.ds(s, n, stride=FEAT_TILES), :] = x[:, s * LANES:(s + 1) * LANES]


def _from_row_tiles(ref, n):
    return jnp.concatenate(
        [ref[pl.ds(s, n, stride=FEAT_TILES), :] for s in range(FEAT_TILES)], axis=-1)


def _prep_kernel(x_ref, pos_ref, ln0g_ref, ln0b_ref, win_ref, gmg_ref, gmb_ref, ws_ref, bs_ref,
                 qg_ref, wuq_ref, kvg_ref, wukv_ref, freq_ref, phase_ref, sign_ref,
                 h_ref, outa_ref, q_ref, k_ref, v_ref, tabc_ref, tabs_ref, rot_ref):
    tb = x_ref.shape[0]

    @pl.when(pl.program_id(0) == 0)
    def _():
        d = lax.broadcasted_iota(jnp.int32, (tb, LANES), 0).astype(jnp.float32) * freq_ref[...]
        tabc_ref[...] = jnp.cos(d)
        tabs_ref[...] = jnp.sin(d)

    pos = pos_ref[...]
    p0 = pos_ref[0:1, :]
    offset = lax.broadcasted_iota(jnp.int32, (tb, 1), 0)
    consecutive = jnp.max(jnp.abs((pos - p0 - offset).astype(jnp.float32))) == 0.0

    @pl.when(consecutive)
    def _():
        a0 = p0.astype(jnp.float32) * freq_ref[...]
        c0, s0 = jnp.cos(a0), jnp.sin(a0)
        lane = lax.broadcasted_iota(jnp.int32, (1, LANES), 1)
        coef_c = jnp.where(lane < 2 * 32, c0, jnp.where(lane < 3 * 32, -s0, s0))
        coef_s = jnp.where(lane < 2 * 32, -s0, jnp.where(lane < 3 * 32, -c0, c0))
        rot_ref[...] = coef_c * tabc_ref[...] + coef_s * tabs_ref[...]

    @pl.when(jnp.logical_not(consecutive))
    def _():
        ang = pos.astype(jnp.float32) * freq_ref[...]
        rot_ref[...] = jnp.cos(ang - phase_ref[...]) * sign_ref[...]

    rot = rot_ref[...]

    h = _layer_norm(x_ref[...], ln0g_ref[...], ln0b_ref[...])
    h_ref[...] = h
    z = jnp.dot(h.astype(jnp.bfloat16), win_ref[...], preferred_element_type=jnp.float32)

    u = _gelu_tanh(z[:, :GM_WIDTH])
    v = _gelu_tanh(z[:, GM_WIDTH:2 * GM_WIDTH])
    row_chunk = lax.broadcasted_iota(jnp.int32, (GM_CHUNK, GM_CHUNK), 0) // CHUNK
    col_chunk = lax.broadcasted_iota(jnp.int32, (GM_CHUNK, GM_CHUNK), 1) // CHUNK
    allowed = col_chunk <= row_chunk
    for hd in range(GM_HEADS):
        lo, hi = hd * GM_HEAD_DIM, (hd + 1) * GM_HEAD_DIM
        vln = _layer_norm(v[:, lo:hi], gmg_ref[:, lo:hi], gmb_ref[:, lo:hi]).astype(jnp.bfloat16)
        wm = jnp.where(allowed, ws_ref[hd], 0.0).astype(jnp.bfloat16)
        for c in range(tb // GM_CHUNK):
            r0, r1 = c * GM_CHUNK, (c + 1) * GM_CHUNK
            f = jnp.dot(wm, vln[r0:r1], preferred_element_type=jnp.float32) + bs_ref[hd]
            outa_ref[r0:r1, lo:hi] = (u[r0:r1, lo:hi] * f).astype(jnp.bfloat16)

    ql = _rms_norm(z[:, O_Q:O_KV], qg_ref[...]).astype(jnp.bfloat16)
    qf = jnp.dot(ql, wuq_ref[...], preferred_element_type=jnp.float32)
    rot_s = rot * QK_SCALE
    q_parts = []
    for hd in range(MLA_HEADS):
        base = hd * 2 * LANES
        q_parts.append(qf[:, base:base + LANES] * QK_SCALE)
        q_parts.append(qf[:, base + LANES:base + 2 * LANES] * rot_s)
    q_ref[...] = jnp.concatenate(q_parts, axis=-1).astype(jnp.bfloat16)

    kvl = _rms_norm(z[:, O_KV:O_KR], kvg_ref[...]).astype(jnp.bfloat16)
    kv = jnp.dot(kvl, wukv_ref[...], preferred_element_type=jnp.float32)
    t = z[:, O_KR:O_KR + LANES] * rot
    krr = t + pltpu.roll(t, 2 * QK_ROPE_DIM // 2, axis=1)
    k_parts, v_parts = [], []
    for hd in range(MLA_HEADS):
        base = hd * 2 * LANES
        k_parts.append(kv[:, base:base + LANES])
        k_parts.append(krr)
        v_parts.append(kv[:, base + LANES:base + 2 * LANES])
    k_ref[...] = jnp.concatenate(k_parts, axis=-1).astype(jnp.bfloat16)
    v_ref[...] = jnp.concatenate(v_parts, axis=-1).astype(jnp.bfloat16)


def _prep(x2, pos2, ln0g, ln0b, win, gmg, gmb, ws, bs, qg, wuq, kvg, wukv, freq, phase, sign):
    tb = PREP_TOKENS
    full = lambda shape: pl.BlockSpec(shape, lambda i: (0,) * len(shape))
    tok = lambda cols: pl.BlockSpec((tb, cols), lambda i: (i, 0))
    return pl.pallas_call(
        _prep_kernel,
        grid=(N_TOK // tb,),
        in_specs=[tok(D_MODEL), tok(1), full((1, D_MODEL)), full((1, D_MODEL)),
                  full((D_MODEL, IN_COLS)), full((1, GM_WIDTH)), full((1, GM_WIDTH)),
                  full((GM_HEADS, GM_CHUNK, GM_CHUNK)), full((GM_HEADS, GM_CHUNK, GM_HEAD_DIM)),
                  full((1, Q_LORA_RANK)), full((Q_LORA_RANK, D_MODEL)),
                  full((1, KV_LORA_RANK)), full((KV_LORA_RANK, D_MODEL)),
                  full((1, LANES)), full((1, LANES)), full((1, LANES))],
        out_specs=[tok(D_MODEL), tok(GM_WIDTH), tok(D_MODEL), tok(D_MODEL), tok(GM_WIDTH)],
        out_shape=[jax.ShapeDtypeStruct((N_TOK, D_MODEL), jnp.float32),
                   jax.ShapeDtypeStruct((N_TOK, GM_WIDTH), jnp.bfloat16),
                   jax.ShapeDtypeStruct((N_TOK, D_MODEL), jnp.bfloat16),
                   jax.ShapeDtypeStruct((N_TOK, D_MODEL), jnp.bfloat16),
                   jax.ShapeDtypeStruct((N_TOK, GM_WIDTH), jnp.bfloat16)],
        scratch_shapes=[pltpu.VMEM((tb, LANES), jnp.float32)] * 3,
        compiler_params=pltpu.CompilerParams(
            dimension_semantics=("arbitrary",), vmem_limit_bytes=VMEM_LIMIT),
        name="prep",
    )(x2, pos2, ln0g, ln0b, win, gmg, gmb, ws, bs, qg, wuq, kvg, wukv, freq, phase, sign)


def _attn_kernel(q_ref, k_ref, v_ref, o_ref, s_ref, mx_ref, ls_ref, acc_ref):
    qi = pl.program_id(1)
    tq = q_ref.shape[0]
    nt = (((1,), (1,)), ((), ()))
    n_kv = SEQ // ATT_K

    half = tq // 2
    row_chunk = (lax.broadcasted_iota(jnp.int32, (half, half), 0)) // CHUNK
    col_chunk = (lax.broadcasted_iota(jnp.int32, (half, half), 1)) // CHUNK
    quad_allowed = col_chunk <= row_chunk
    diag_start = pl.multiple_of(qi * ATT_K, ATT_K)
    masked = jnp.float32(-1e30)

    def lane_tiles(x):
        return [x[:, t * LANES:(t + 1) * LANES] for t in range(x.shape[1] // LANES)]

    def lane_fold(x, op):
        return functools.reduce(op, lane_tiles(x))

    def scores(hd, j):
        start = pl.multiple_of(j * ATT_K, ATT_K)
        q_h = q_ref[:, hd * 2 * LANES:(hd + 1) * 2 * LANES]
        kb = k_ref[pl.ds(start, ATT_K), hd * 2 * LANES:(hd + 1) * 2 * LANES]
        return lax.dot_general(q_h, kb, nt, preferred_element_type=jnp.float32)

    for hd in range(MLA_HEADS):
        cols = slice(hd * 2 * LANES, (hd + 1) * 2 * LANES)
        k_lo = k_ref[pl.ds(diag_start, half), cols]
        k_hi = k_ref[pl.ds(diag_start + half, half), cols]
        qk = lambda q, k: lax.dot_general(q, k, nt, preferred_element_type=jnp.float32)
        s_tl = jnp.where(quad_allowed, qk(q_ref[:half, cols], k_lo), masked)
        s_bl = qk(q_ref[half:, cols], k_lo)
        s_br = jnp.where(quad_allowed, qk(q_ref[half:, cols], k_hi), masked)
        diag = s_ref.at[hd * n_kv + qi]
        diag[:half, :half] = s_tl
        diag[half:, :half] = s_bl
        diag[half:, half:] = s_br
        mx_ref[hd, :half] = lane_fold(s_tl, jnp.maximum)
        mx_ref[hd, half:] = jnp.maximum(lane_fold(s_bl, jnp.maximum), lane_fold(s_br, jnp.maximum))

    def pass_a(j, c):
        for hd in range(MLA_HEADS):
            s = scores(hd, j)
            s_ref[hd * n_kv + j] = s
            mx_ref[hd] = jnp.maximum(mx_ref[hd], lane_fold(s, jnp.maximum))
        return c

    lax.fori_loop(0, qi, pass_a, 0)

    for hd in range(MLA_HEADS):
        mx_ref[hd] = jnp.broadcast_to(jnp.max(mx_ref[hd], axis=-1, keepdims=True), (tq, LANES))

    def probs(hd, j):
        s = s_ref[hd * n_kv + j]
        mb = mx_ref[hd]
        p = jnp.exp2(jnp.concatenate([t - mb for t in lane_tiles(s)], axis=-1))
        start = pl.multiple_of(j * ATT_K, ATT_K)
        vb = v_ref[pl.ds(start, ATT_K), hd * LANES:(hd + 1) * LANES]
        pv = jnp.dot(p.astype(jnp.bfloat16), vb, preferred_element_type=jnp.float32)
        return lane_fold(p, jnp.add), pv

    for hd in range(MLA_HEADS):
        diag = s_ref.at[hd * n_kv + qi]
        vcols = slice(hd * LANES, (hd + 1) * LANES)
        p_top = jnp.exp2(jnp.concatenate([t - mx_ref[hd, :half] for t in lane_tiles(diag[:half, :half])], axis=-1))
        p_bot = jnp.exp2(jnp.concatenate([t - mx_ref[hd, half:] for t in lane_tiles(diag[half:, :])], axis=-1))
        ls_ref[hd, :half] = lane_fold(p_top, jnp.add)
        ls_ref[hd, half:] = lane_fold(p_bot, jnp.add)
        acc_ref[hd, :half] = jnp.dot(p_top.astype(jnp.bfloat16), v_ref[pl.ds(diag_start, half), vcols],
                                     preferred_element_type=jnp.float32)
        acc_ref[hd, half:] = jnp.dot(p_bot.astype(jnp.bfloat16), v_ref[pl.ds(diag_start, ATT_K), vcols],
                                     preferred_element_type=jnp.float32)

    def pass_b(j, c):
        for hd in range(MLA_HEADS):
            ls, pv = probs(hd, j)
            ls_ref[hd] = ls_ref[hd] + ls
            acc_ref[hd] = acc_ref[hd] + pv
        return c

    lax.fori_loop(0, qi, pass_b, 0)

    for hd in range(MLA_HEADS):
        l = jnp.sum(ls_ref[hd], axis=-1, keepdims=True)
        o_ref[:, hd * V_HEAD_DIM:(hd + 1) * V_HEAD_DIM] = (acc_ref[hd] / l).astype(jnp.bfloat16)


def _attn(q, k, v):
    tq = ATT_Q
    tokblk = lambda cols: pl.BlockSpec((tq, cols), lambda b, i: (b * N_QBLK + i, 0))
    seqblk = lambda cols: pl.BlockSpec((SEQ, cols), lambda b, i: (b, 0))
    return pl.pallas_call(
        _attn_kernel,
        grid=(BATCH, N_QBLK),
        in_specs=[tokblk(D_MODEL), seqblk(D_MODEL), seqblk(GM_WIDTH)],
        out_specs=tokblk(GM_WIDTH),
        out_shape=jax.ShapeDtypeStruct((N_TOK, MLA_HEADS * V_HEAD_DIM), jnp.bfloat16),
        scratch_shapes=[pltpu.VMEM((MLA_HEADS * (SEQ // ATT_K), tq, ATT_K), jnp.float32),
                        pltpu.VMEM((MLA_HEADS, tq, LANES), jnp.float32),
                        pltpu.VMEM((MLA_HEADS, tq, LANES), jnp.float32),
                        pltpu.VMEM((MLA_HEADS, tq, V_HEAD_DIM), jnp.float32)],
        compiler_params=pltpu.CompilerParams(
            dimension_semantics=("arbitrary", "arbitrary"), vmem_limit_bytes=VMEM_LIMIT),
        name="attn",
    )(q, k, v)


def _proj_kernel(outa_ref, ob_ref, h_ref, wout_ref, ln1g_ref, ln1b_ref, wr_ref, br_ref,
                 h1r_ref, ri_ref, rf_ref, proj_ref):
    i = pl.program_id(0)
    tb = outa_ref.shape[0]

    @pl.when(i == 0)
    def _():
        proj_ref[...] = jnp.zeros_like(proj_ref)

    h1 = _layer_norm(ALPHA * h_ref[...] + proj_ref[(i + 1) % 2], ln1g_ref[...], ln1b_ref[...])
    _to_row_tiles(h1r_ref, h1)

    logits_tm = jnp.dot(h1.astype(jnp.bfloat16), wr_ref[...], preferred_element_type=jnp.float32)
    logits = logits_tm.T[0:ROUTER_ROWS] + br_ref[...]

    sub_i = lax.broadcasted_iota(jnp.int32, (SUBLANES, tb), 0)
    sub = sub_i.astype(jnp.float32)
    neg = jnp.float32(-jnp.inf)
    g = jnp.where(sub_i < N_GROUPS, logits[0:SUBLANES], neg)
    gmax = jnp.max(g, axis=0, keepdims=True)
    g_top = jnp.min(jnp.where(g == gmax, sub, float(SUBLANES)), axis=0, keepdims=True)
    p_group = 1.0 / jnp.sum(jnp.exp(g - gmax), axis=0, keepdims=True)
    sel = logits[SUBLANES:2 * SUBLANES]
    for grp in range(1, N_GROUPS):
        sel = jnp.where(g_top == float(grp), logits[(grp + 1) * SUBLANES:(grp + 2) * SUBLANES], sel)
    v1 = jnp.max(sel, axis=0, keepdims=True)
    i1 = jnp.min(jnp.where(sel == v1, sub, float(SUBLANES)), axis=0, keepdims=True)
    sel2 = jnp.where(sub == i1, neg, sel)
    v2 = jnp.max(sel2, axis=0, keepdims=True)
    i2 = jnp.min(jnp.where(sel2 == v2, sub, float(SUBLANES)), axis=0, keepdims=True)
    e21 = jnp.exp(v2 - v1)
    w1 = 1.0 / (1.0 + e21)
    gate1 = p_group * w1
    gate2 = p_group * (e21 * w1)
    e1 = g_top * EXPERTS_PER_GROUP + i1
    e2 = g_top * EXPERTS_PER_GROUP + i2
    ri_ref[...] = jnp.where(sub_i == 0, e1, jnp.where(sub_i == 1, e2, 0.0)).astype(jnp.int32)
    rf_ref[...] = jnp.where(sub_i == 0, gate1, jnp.where(sub_i == 1, gate2, 0.0))

    proj_ref[i % 2] = (jnp.dot(outa_ref[...], wout_ref[:GM_WIDTH, :], preferred_element_type=jnp.float32)
                       + jnp.dot(ob_ref[...], wout_ref[GM_WIDTH:, :], preferred_element_type=jnp.float32))


def _proj(outa, ob, h, wout, ln1g, ln1b, wr, br):
    tb = PROJ_TOKENS
    n_blk = N_TOK // tb
    cur = lambda i: jnp.minimum(i, n_blk - 1)
    prev = lambda i: jnp.maximum(i - 1, 0)
    full = lambda shape: pl.BlockSpec(shape, lambda i: (0,) * len(shape))
    return pl.pallas_call(
        _proj_kernel,
        grid=(n_blk + 1,),
        in_specs=[pl.BlockSpec((tb, GM_WIDTH), lambda i: (cur(i), 0)),
                  pl.BlockSpec((tb, GM_WIDTH), lambda i: (cur(i), 0)),
                  pl.BlockSpec((tb, D_MODEL), lambda i: (prev(i), 0)),
                  full((D_MODEL, D_MODEL)), full((1, D_MODEL)), full((1, D_MODEL)),
                  full((D_MODEL, LANES)), full((ROUTER_ROWS, 1))],
        out_specs=[pl.BlockSpec((tb * FEAT_TILES, LANES), lambda i: (prev(i), 0)),
                   pl.BlockSpec((SUBLANES, tb), lambda i: (0, prev(i))),
                   pl.BlockSpec((SUBLANES, tb), lambda i: (0, prev(i)))],
        out_shape=[jax.ShapeDtypeStruct((N_TOK * FEAT_TILES, LANES), jnp.float32),
                   jax.ShapeDtypeStruct((SUBLANES, N_TOK), jnp.int32),
                   jax.ShapeDtypeStruct((SUBLANES, N_TOK), jnp.float32)],
        scratch_shapes=[pltpu.VMEM((2, tb, D_MODEL), jnp.float32)],
        compiler_params=pltpu.CompilerParams(
            dimension_semantics=("arbitrary",), vmem_limit_bytes=VMEM_LIMIT),
        name="proj",
    )(outa, ob, h, wout, ln1g, ln1b, wr, br)


def _plan_kernel(ri_all_ref, ri_ref, ldest_ref, runs_ref, meta_ref, run_ref, start_ref, upper_ref):
    step = pl.program_id(0)
    tb = ri_ref.shape[1]
    f32 = jnp.float32
    er = lax.broadcasted_iota(jnp.int32, (N_EXPERTS, LANES), 0)
    ec = lax.broadcasted_iota(jnp.int32, (N_EXPERTS, LANES), 1)
    to_row = lambda col: jnp.sum(jnp.where(er == ec, col, 0.0), axis=0, keepdims=True)

    def expert_one_hot(ref):
        e_sub = lax.broadcasted_iota(jnp.int32, (N_EXPERTS, ref.shape[1]), 0)
        return e_sub == ref[0:1, :], e_sub == ref[1:2, :]

    @pl.when(step == 0)
    def _():
        oh1, oh2 = expert_one_hot(ri_all_ref)
        counts = jnp.sum(jnp.where(oh1 | oh2, 1.0, 0.0), axis=1, keepdims=True)
        padded = jnp.floor((counts + (EXPERT_ROWS - 1)) * (1.0 / EXPERT_ROWS)) * EXPERT_ROWS
        pad_end = jnp.sum(jnp.where(ec <= er, to_row(padded), 0.0), axis=1, keepdims=True)
        start_ref[...] = jnp.broadcast_to(pad_end - padded, start_ref.shape)
        run_ref[...] = jnp.zeros_like(run_ref)
        bstart = (lax.broadcasted_iota(jnp.int32, (N_EXPERTS, META_LANES), 1) * EXPERT_ROWS).astype(f32)
        blk_e = jnp.sum(jnp.where(pad_end <= bstart, 1.0, 0.0), axis=0, keepdims=True)
        blk_e = jnp.minimum(blk_e, N_EXPERTS - 1.0)
        n_used = pad_end[N_EXPERTS - 1:N_EXPERTS, :] * (1.0 / EXPERT_ROWS)
        pad3 = lambda r: jnp.concatenate(
            [r, jnp.zeros((1, META_LANES - LANES), f32)], axis=1)
        msub = lax.broadcasted_iota(jnp.int32, (SUBLANES, META_LANES), 0)
        meta = jnp.where(msub == 0, blk_e,
                         jnp.where(msub == 1, pad3(to_row(pad_end)),
                                   jnp.where(msub == 2, pad3(to_row(counts)),
                                             jnp.where(msub == 3, n_used, 0.0))))
        meta_ref[...] = meta.astype(jnp.int32)
        tr = lax.broadcasted_iota(jnp.int32, (tb, tb), 0)
        tc = lax.broadcasted_iota(jnp.int32, (tb, tb), 1)
        upper_ref[...] = jnp.where(tr < tc, 1.0, 0.0).astype(jnp.bfloat16)

    @pl.when(step > 0)
    def _():
        oh1, oh2 = expert_one_hot(ri_ref)
        oh = jnp.where(oh1 | oh2, 1.0, 0.0).astype(f32)
        blk_count = jnp.sum(oh, axis=1, keepdims=True)
        prefix = jnp.dot(oh.astype(jnp.bfloat16), upper_ref[...], preferred_element_type=f32)
        cnt_row = to_row(blk_count)
        lstart = jnp.sum(jnp.where(ec < er, cnt_row, 0.0), axis=1, keepdims=True)
        base = prefix + lstart
        d1 = jnp.sum(jnp.where(oh1, base, 0.0), axis=0, keepdims=True)
        d2 = jnp.sum(jnp.where(oh2, base, 0.0), axis=0, keepdims=True)
        sub = lax.broadcasted_iota(jnp.int32, (SUBLANES, tb), 0)
        ldest_ref[...] = jnp.where(sub == 0, d1, jnp.where(sub == 1, d2, 0.0)).astype(jnp.int32)
        gstart = start_ref[:, 0:1] + run_ref[:, 0:1]
        rsub = lax.broadcasted_iota(jnp.int32, (SUBLANES, LANES), 0)
        runs = jnp.where(rsub == 0, cnt_row,
                         jnp.where(rsub == 1, to_row(lstart), jnp.where(rsub == 2, to_row(gstart), 0.0)))
        runs_ref[...] = runs.astype(jnp.int32)
        run_ref[...] = run_ref[...] + blk_count


def _plan(ri):
    tb = MOE_TOKENS
    blk = lambda i: jnp.maximum(i - 1, 0)
    return pl.pallas_call(
        _plan_kernel,
        grid=(N_MOE_BLOCKS + 1,),
        in_specs=[pl.BlockSpec((SUBLANES, N_TOK), lambda i: (0, 0)),
                  pl.BlockSpec((SUBLANES, tb), lambda i: (0, blk(i)))],
        out_specs=[pl.BlockSpec((SUBLANES, tb), lambda i: (0, blk(i))),
                   pl.BlockSpec((SUBLANES, LANES), lambda i: (blk(i), 0)),
                   pl.BlockSpec((SUBLANES, META_LANES), lambda i: (0, 0))],
        out_shape=[jax.ShapeDtypeStruct((SUBLANES, N_TOK), jnp.int32),
                   jax.ShapeDtypeStruct((N_MOE_BLOCKS * SUBLANES, LANES), jnp.int32),
                   jax.ShapeDtypeStruct((SUBLANES, META_LANES), jnp.int32)],
        scratch_shapes=[pltpu.VMEM((N_EXPERTS, LANES), jnp.float32),
                        pltpu.VMEM((N_EXPERTS, LANES), jnp.float32),
                        pltpu.VMEM((tb, tb), jnp.bfloat16)],
        compiler_params=pltpu.CompilerParams(
            dimension_semantics=("arbitrary",), vmem_limit_bytes=VMEM_LIMIT),
        name="plan",
    )(ri, ri)


def _for_each_run_piece(runs_ref, live, fn):
    for e in range(N_EXPERTS):
        n = jnp.where(live, runs_ref[0, e], 0)
        lstart, gstart = runs_ref[1, e], runs_ref[2, e]
        for bit in range(RUN_BITS):
            @pl.when((n & (1 << bit)) != 0)
            def _(n=n, lstart=lstart, gstart=gstart, bit=bit):
                off = (n >> (bit + 1)) << (bit + 1)
                fn(lstart + off, gstart + off, 1 << bit)


def _tile_rows(ref, row, rows):
    return ref.at[pl.ds(pl.multiple_of(row * FEAT_TILES, FEAT_TILES), rows * FEAT_TILES)]


def _dispatch_kernel(meta_ref, runs_prev_ref, ldest_ref, h1r_ref, buf_ref, sorted0_ref, sorted1_ref, zero_ref,
                     sems, zsem):
    i = pl.program_id(0)
    n_blocks = pl.num_programs(0) - 1
    tb = ldest_ref.shape[1]
    block_tiles = 2 * tb * FEAT_TILES

    def wait_rows(sorted_ref, sem):
        pltpu.make_async_copy(sorted_ref, buf_ref.at[pl.ds(0, block_tiles)], sem).wait()

    @pl.when(i == 0)
    def _():
        zero_ref[...] = jnp.zeros_like(zero_ref)

        def zero_copy(e):
            start = pl.multiple_of((meta_ref[1, e] - EXPERT_ROWS) * FEAT_TILES, EXPERT_ROWS * FEAT_TILES)
            return pltpu.make_async_copy(
                zero_ref, buf_ref.at[pl.ds(start, EXPERT_ROWS * FEAT_TILES)], zsem)

        def start_zero(e, c):
            @pl.when(meta_ref[2, e] > 0)
            def _():
                zero_copy(e).start()
            return c

        def wait_zero(e, c):
            @pl.when(meta_ref[2, e] > 0)
            def _():
                zero_copy(e).wait()
            return c

        def tail_copy(b):
            start = pl.multiple_of(b * (EXPERT_ROWS * FEAT_TILES), EXPERT_ROWS * FEAT_TILES)
            return pltpu.make_async_copy(
                zero_ref, buf_ref.at[pl.ds(start, EXPERT_ROWS * FEAT_TILES)], zsem)

        def start_tail(b, c):
            tail_copy(b).start()
            return c

        def wait_tail(b, c):
            tail_copy(b).wait()
            return c

        lax.fori_loop(0, N_EXPERTS, start_zero, 0)
        lax.fori_loop(meta_ref[3, 0], N_ROW_BLOCKS, start_tail, 0)
        lax.fori_loop(0, N_EXPERTS, wait_zero, 0)
        lax.fori_loop(meta_ref[3, 0], N_ROW_BLOCKS, wait_tail, 0)

    def step(cur_ref, cur_sem, prev_ref, prev_sem):
        @pl.when(i >= 2)
        def _():
            wait_rows(cur_ref, cur_sem)

        def send(lrow, grow, rows):
            pltpu.make_async_copy(_tile_rows(prev_ref, lrow, rows), _tile_rows(buf_ref, grow, rows),
                                  prev_sem).start()

        _for_each_run_piece(runs_prev_ref, i >= 1, send)

        x = _from_row_tiles(h1r_ref, tb).astype(jnp.bfloat16)
        ld0 = ldest_ref[0:1, :]
        ld1 = ldest_ref[1:2, :]
        for c in range(2 * tb // SORT_CHUNK):
            r = lax.broadcasted_iota(jnp.int32, (SORT_CHUNK, tb), 0) + c * SORT_CHUNK
            perm = jnp.where((r == ld0) | (r == ld1), 1.0, 0.0).astype(jnp.bfloat16)
            rows = jnp.dot(perm, x, preferred_element_type=jnp.float32)
            _to_row_tiles(cur_ref.at[pl.ds(c * SORT_CHUNK * FEAT_TILES, SORT_CHUNK * FEAT_TILES)], rows)

        @pl.when(i == n_blocks)
        def _():
            wait_rows(prev_ref, prev_sem)

    @pl.when(i % 2 == 0)
    def _():
        step(sorted0_ref, sems.at[0], sorted1_ref, sems.at[1])

    @pl.when(i % 2 == 1)
    def _():
        step(sorted1_ref, sems.at[1], sorted0_ref, sems.at[0])


def _dispatch(meta, runs, ldest, h1r):
    tb = MOE_TOKENS
    cur = lambda i: jnp.minimum(i, N_MOE_BLOCKS - 1)
    prev = lambda i: jnp.maximum(i - 1, 0)
    return pl.pallas_call(
        _dispatch_kernel,
        grid_spec=pltpu.PrefetchScalarGridSpec(
            num_scalar_prefetch=1,
            grid=(N_MOE_BLOCKS + 1,),
            in_specs=[pl.BlockSpec((SUBLANES, LANES), lambda i, m: (prev(i), 0), memory_space=pltpu.SMEM),
                      pl.BlockSpec((SUBLANES, tb), lambda i, m: (0, cur(i))),
                      pl.BlockSpec((tb * FEAT_TILES, LANES), lambda i, m: (cur(i), 0))],
            out_specs=pl.BlockSpec(memory_space=pl.ANY),
            scratch_shapes=[pltpu.VMEM((2 * tb * FEAT_TILES, LANES), jnp.float32),
                            pltpu.VMEM((2 * tb * FEAT_TILES, LANES), jnp.float32),
                            pltpu.VMEM((EXPERT_ROWS * FEAT_TILES, LANES), jnp.float32),
                            pltpu.SemaphoreType.DMA((2,)), pltpu.SemaphoreType.DMA]),
        out_shape=jax.ShapeDtypeStruct((N_ROWS * FEAT_TILES, LANES), jnp.float32),
        compiler_params=pltpu.CompilerParams(
            dimension_semantics=("arbitrary",), vmem_limit_bytes=VMEM_LIMIT),
        name="dispatch",
    )(meta, runs, ldest, h1r)


def _sub_block_expert(meta, i, sub):
    n_used = meta[3, 0]
    step = jnp.minimum(i, (n_used - 1) // EXPERT_SUB)
    return meta[0, jnp.minimum(EXPERT_SUB * step + sub, n_used - 1)]


def _experts_kernel(meta_ref, x_ref, wg0_ref, wu0_ref, wd0_ref, wg1_ref, wu1_ref, wd1_ref, o_ref,
                    wgb_ref, wub_ref, wdb_ref, cached_ref):
    n = EXPERT_ROWS
    i = pl.program_id(0)
    n_used = meta_ref[3, 0]
    weights = ((wg0_ref, wu0_ref, wd0_ref), (wg1_ref, wu1_ref, wd1_ref))

    @pl.when(i == 0)
    def _():
        for sub in range(EXPERT_SUB):
            cached_ref[sub] = -1

    for sub in range(EXPERT_SUB):
        expert = _sub_block_expert(meta_ref, i, sub)

        @pl.when(cached_ref[sub] != expert)
        def _(sub=sub, expert=expert):
            wg_ref, wu_ref, wd_ref = weights[sub]
            wgb_ref[sub] = wg_ref[...].astype(jnp.bfloat16)
            wub_ref[sub] = wu_ref[...].astype(jnp.bfloat16)
            wdb_ref[sub] = wd_ref[...].astype(jnp.bfloat16)
            cached_ref[sub] = expert

    @pl.when(EXPERT_SUB * i < n_used)
    def _():
        for sub in range(EXPERT_SUB):
            rows = pl.ds(sub * n * FEAT_TILES, n * FEAT_TILES)
            x = _from_row_tiles(x_ref.at[rows], n).astype(jnp.bfloat16)
            gate = jnp.dot(x, wgb_ref[sub], preferred_element_type=jnp.float32)
            up = jnp.dot(x, wub_ref[sub], preferred_element_type=jnp.float32)
            act = (gate * jax.nn.sigmoid(gate) * up).astype(jnp.bfloat16)
            _to_row_tiles(o_ref.at[rows], jnp.dot(act, wdb_ref[sub], preferred_element_type=jnp.float32))

    @pl.when(EXPERT_SUB * i >= n_used)
    def _():
        o_ref[...] = jnp.zeros_like(o_ref)


def _experts(meta, buf, wg, wu, wd):
    step_rows = EXPERT_SUB * EXPERT_ROWS
    used_step = lambda i, m: jnp.minimum(i, (m[3, 0] - 1) // EXPERT_SUB)

    def weight_spec(shape, sub):
        return pl.BlockSpec((None,) + shape, lambda i, m: (_sub_block_expert(m, i, sub), 0, 0))

    up_shape, down_shape = (D_MODEL, EXPERT_FF), (EXPERT_FF, D_MODEL)
    return pl.pallas_call(
        _experts_kernel,
        grid_spec=pltpu.PrefetchScalarGridSpec(
            num_scalar_prefetch=1,
            grid=(N_ROW_BLOCKS // EXPERT_SUB,),
            in_specs=[pl.BlockSpec((step_rows * FEAT_TILES, LANES), lambda i, m: (used_step(i, m), 0)),
                      weight_spec(up_shape, 0), weight_spec(up_shape, 0), weight_spec(down_shape, 0),
                      weight_spec(up_shape, 1), weight_spec(up_shape, 1), weight_spec(down_shape, 1)],
            out_specs=pl.BlockSpec((step_rows * FEAT_TILES, LANES), lambda i, m: (i, 0)),
            scratch_shapes=[pltpu.VMEM((EXPERT_SUB,) + up_shape, jnp.bfloat16),
                            pltpu.VMEM((EXPERT_SUB,) + up_shape, jnp.bfloat16),
                            pltpu.VMEM((EXPERT_SUB,) + down_shape, jnp.bfloat16),
                            pltpu.SMEM((EXPERT_SUB,), jnp.int32)]),
        out_shape=jax.ShapeDtypeStruct((N_ROWS * FEAT_TILES, LANES), jnp.float32),
        compiler_params=pltpu.CompilerParams(
            dimension_semantics=("arbitrary",), vmem_limit_bytes=VMEM_LIMIT),
        name="experts",
    )(meta, buf, wg, wu, wd, wg, wu, wd)


def _combine_kernel(runs_ref, runs_next_ref, ldest_ref, rf_ref, h1r_ref, eout_ref, ln2g_ref, ln2b_ref,
                    o_ref, y0_ref, y1_ref, sems):
    i = pl.program_id(0)
    n_steps = pl.num_programs(0)
    tb = ldest_ref.shape[1]
    block_tiles = 2 * tb * FEAT_TILES

    def fetch(table_ref, live, y_ref, sem):
        def recv(lrow, grow, rows):
            pltpu.make_async_copy(_tile_rows(eout_ref, grow, rows), _tile_rows(y_ref, lrow, rows), sem).start()
        _for_each_run_piece(table_ref, live, recv)

    @pl.when(i == 0)
    def _():
        fetch(runs_ref, True, y0_ref, sems.at[0])

    def step(cur_ref, cur_sem, next_ref, next_sem):
        pltpu.make_async_copy(eout_ref.at[pl.ds(0, block_tiles)], cur_ref, cur_sem).wait()
        fetch(runs_next_ref, i + 1 < n_steps, next_ref, next_sem)

        ld = ldest_ref[...].astype(jnp.float32).T
        gates = rf_ref[...].T
        y = None
        col = lax.broadcasted_iota(jnp.int32, (tb, SORT_CHUNK), 1).astype(jnp.float32)
        for c in range(2 * tb // SORT_CHUNK):
            ld_c = ld - float(c * SORT_CHUNK)
            g = jnp.where(col == ld_c[:, 0:1], gates[:, 0:1],
                          jnp.where(col == ld_c[:, 1:2], gates[:, 1:2], 0.0)).astype(jnp.bfloat16)
            rows = _from_row_tiles(
                cur_ref.at[pl.ds(c * SORT_CHUNK * FEAT_TILES, SORT_CHUNK * FEAT_TILES)], SORT_CHUNK)
            part = jnp.dot(g, rows.astype(jnp.bfloat16), preferred_element_type=jnp.float32)
            y = part if y is None else y + part
        h1 = _from_row_tiles(h1r_ref, tb)
        o_ref[...] = _layer_norm(ALPHA * h1 + y, ln2g_ref[...], ln2b_ref[...])

    @pl.when(i % 2 == 0)
    def _():
        step(y0_ref, sems.at[0], y1_ref, sems.at[1])

    @pl.when(i % 2 == 1)
    def _():
        step(y1_ref, sems.at[1], y0_ref, sems.at[0])


def _combine(runs, ldest, rf, h1r, eout, ln2g, ln2b):
    tb = MOE_TOKENS
    last = N_MOE_BLOCKS - 1
    return pl.pallas_call(
        _combine_kernel,
        grid=(N_MOE_BLOCKS,),
        in_specs=[pl.BlockSpec((SUBLANES, LANES), lambda i: (i, 0), memory_space=pltpu.SMEM),
                  pl.BlockSpec((SUBLANES, LANES), lambda i: (jnp.minimum(i + 1, last), 0),
                               memory_space=pltpu.SMEM),
                  pl.BlockSpec((SUBLANES, tb), lambda i: (0, i)),
                  pl.BlockSpec((SUBLANES, tb), lambda i: (0, i)),
                  pl.BlockSpec((tb * FEAT_TILES, LANES), lambda i: (i, 0)),
                  pl.BlockSpec(memory_space=pl.ANY),
                  pl.BlockSpec((1, D_MODEL), lambda i: (0, 0)),
                  pl.BlockSpec((1, D_MODEL), lambda i: (0, 0))],
        out_specs=pl.BlockSpec((tb, D_MODEL), lambda i: (i, 0)),
        out_shape=jax.ShapeDtypeStruct((N_TOK, D_MODEL), jnp.float32),
        scratch_shapes=[pltpu.VMEM((2 * tb * FEAT_TILES, LANES), jnp.float32),
                        pltpu.VMEM((2 * tb * FEAT_TILES, LANES), jnp.float32),
                        pltpu.SemaphoreType.DMA((2,))],
        compiler_params=pltpu.CompilerParams(
            dimension_semantics=("arbitrary",), vmem_limit_bytes=VMEM_LIMIT),
        name="combine",
    )(runs, runs, ldest, rf, h1r, eout, ln2g, ln2b)


def _swap_halves(w):
    half = w.shape[-1] // 2
    return jnp.concatenate([w[..., half:], w[..., :half]], axis=-1)


def kernel(x, positions, ln0_g, ln0_b, w_in, gm_ln_g, gm_ln_b, w_spatial, b_spatial, q_norm_g, w_uq, kv_norm_g, w_ukv, w_out, ln1_g, ln1_b, w_router_group, b_router_group, w_router_expert, b_router_expert, w_gate, w_up, w_down, ln2_g, ln2_b):
    bf16 = jnp.bfloat16
    row = lambda a: a.reshape(1, -1)

    w_in0 = w_in[0]
    kr_cols = w_in0[:, O_KR:O_KR + QK_ROPE_DIM]
    win = jnp.concatenate([w_in0, _swap_halves(kr_cols)], axis=1).astype(bf16)
    wuq3 = w_uq[0].reshape(Q_LORA_RANK, MLA_HEADS, QK_NOPE_DIM + QK_ROPE_DIM)
    rope_cols = wuq3[:, :, QK_NOPE_DIM:]
    wuq = jnp.concatenate([wuq3, _swap_halves(rope_cols)], axis=-1).reshape(Q_LORA_RANK, D_MODEL).astype(bf16)
    wukv = w_ukv[0].astype(bf16)
    wout = w_out[0].astype(bf16)
    bs = jnp.broadcast_to(b_spatial[0][:, :, None], (GM_HEADS, GM_CHUNK, GM_HEAD_DIM))
    wr = jnp.concatenate([w_router_group[0], jnp.zeros((D_MODEL, SUBLANES - N_GROUPS), jnp.float32),
                          w_router_expert[0],
                          jnp.zeros((D_MODEL, LANES - ROUTER_ROWS), jnp.float32)],
                         axis=1).astype(bf16)
    br = jnp.concatenate([b_router_group[0], jnp.zeros((SUBLANES - N_GROUPS,), jnp.float32),
                          b_router_expert[0]]).reshape(ROUTER_ROWS, 1)

    inv_freq = ROPE_THETA ** (-jnp.arange(0, QK_ROPE_DIM, 2, dtype=jnp.float32) / QK_ROPE_DIM)
    freq = jnp.tile(inv_freq, 4).reshape(1, LANES)
    quarter = QK_ROPE_DIM // 2
    phase = jnp.concatenate([jnp.zeros((2 * quarter,), jnp.float32),
                             jnp.full((2 * quarter,), math.pi / 2, jnp.float32)]).reshape(1, LANES)
    sign = jnp.concatenate([jnp.ones((2 * quarter,), jnp.float32), -jnp.ones((quarter,), jnp.float32),
                            jnp.ones((quarter,), jnp.float32)]).reshape(1, LANES)

    x2 = x.reshape(N_TOK, D_MODEL)
    pos2 = positions.reshape(N_TOK, 1)

    h, outa, q, k, v = _prep(x2, pos2, row(ln0_g), row(ln0_b), win, row(gm_ln_g[0]), row(gm_ln_b[0]),
                             w_spatial[0], bs, row(q_norm_g[0]), wuq, row(kv_norm_g[0]), wukv,
                             freq, phase, sign)
    ob = _attn(q, k, v)
    h1r, ri, rf = _proj(outa, ob, h, wout, row(ln1_g[0]), row(ln1_b[0]), wr, br)
    ldest, runs, meta = _plan(ri)
    buf = _dispatch(meta, runs, ldest, h1r)
    eout = _experts(meta, buf, w_gate[0], w_up[0], w_down[0])
    out = _combine(runs, ldest, rf, h1r, eout, row(ln2_g[0]), row(ln2_b[0]))
    return out.reshape(BATCH, SEQ, D_MODEL)
```

```python
import functools
import math

import jax
import jax.numpy as jnp
from jax import lax
from jax.experimental import pallas as pl
from jax.experimental.pallas import tpu as pltpu

D_MODEL = 1024
BATCH = 16
SEQ = 2048
N_TOK = BATCH * SEQ
CHUNK = 64
GM_WIDTH = 512
GM_HEADS = 4
GM_HEAD_DIM = 128
GM_CHUNK = 128
MLA_HEADS = 4
QK_NOPE_DIM = 128
QK_ROPE_DIM = 64
V_HEAD_DIM = 128
Q_LORA_RANK = 384
KV_LORA_RANK = 256
ROPE_THETA = 10000.0
N_GROUPS = 4
EXPERTS_PER_GROUP = 8
N_EXPERTS = 32
TOP_K = 2
EXPERT_FF = 256
ALPHA = 2.0 ** 0.25
QK_SCALE = (QK_NOPE_DIM + QK_ROPE_DIM) ** -0.5 * math.log2(math.e)

LANES = 128
SUBLANES = 8
FEAT_TILES = D_MODEL // LANES
PREP_TOKENS = 512
ATT_Q = 512
ATT_K = 512
N_QBLK = SEQ // ATT_Q
PROJ_TOKENS = 512
MOE_TOKENS = 512
N_MOE_BLOCKS = N_TOK // MOE_TOKENS
RUN_BITS = (TOP_K * MOE_TOKENS).bit_length()
SORT_CHUNK = 256
EXPERT_ROWS = 256
EXPERT_SUB = 2
N_ROWS = N_TOK * TOP_K + N_EXPERTS * EXPERT_ROWS
N_ROW_BLOCKS = N_ROWS // EXPERT_ROWS
META_LANES = 384
IN_COLS = 2 * GM_WIDTH + Q_LORA_RANK + KV_LORA_RANK + 2 * QK_ROPE_DIM
O_Q = 2 * GM_WIDTH
O_KV = O_Q + Q_LORA_RANK
O_KR = O_KV + KV_LORA_RANK
ROUTER_ROWS = 40
VMEM_LIMIT = 48 * 1024 * 1024

assert N_ROW_BLOCKS <= META_LANES and N_ROW_BLOCKS % EXPERT_SUB == 0


def _layer_norm(x, g, b, eps=1e-5):
    mu = jnp.mean(x, axis=-1, keepdims=True)
    xc = x - mu
    var = jnp.mean(xc * xc, axis=-1, keepdims=True)
    return xc * lax.rsqrt(var + eps) * g + b


def _rms_norm(x, g, eps=1e-6):
    return x * lax.rsqrt(jnp.mean(x * x, axis=-1, keepdims=True) + eps) * g


def _gelu_tanh(x):
    c = math.sqrt(2.0 / math.pi)
    return 0.5 * x * (1.0 + jnp.tanh(c * (x + 0.044715 * (x * x * x))))


def _to_row_tiles(ref, x):
    n = x.shape[0]
    for s in range(FEAT_TILES):
        ref[pl.ds(s, n, stride=FEAT_TILES), :] = x[:, s * LANES:(s + 1) * LANES]


def _from_row_tiles(ref, n):
    return jnp.concatenate(
        [ref[pl.ds(s, n, stride=FEAT_TILES), :] for s in range(FEAT_TILES)], axis=-1)


def _prep_kernel(x_ref, pos_ref, ln0g_ref, ln0b_ref, win_ref, gmg_ref, gmb_ref, ws_ref, bs_ref,
                 qg_ref, wuq_ref, kvg_ref, wukv_ref, freq_ref, phase_ref, sign_ref,
                 h_ref, outa_ref, q_ref, k_ref, v_ref, tabc_ref, tabs_ref, rot_ref):
    tb = x_ref.shape[0]

    @pl.when(pl.program_id(0) == 0)
    def _():
        d = lax.broadcasted_iota(jnp.int32, (tb, LANES), 0).astype(jnp.float32) * freq_ref[...]
        tabc_ref[...] = jnp.cos(d)
        tabs_ref[...] = jnp.sin(d)

    pos = pos_ref[...]
    p0 = pos_ref[0:1, :]
    offset = lax.broadcasted_iota(jnp.int32, (tb, 1), 0)
    consecutive = jnp.max(jnp.abs((pos - p0 - offset).astype(jnp.float32))) == 0.0

    @pl.when(consecutive)
    def _():
        a0 = p0.astype(jnp.float32) * freq_ref[...]
        c0, s0 = jnp.cos(a0), jnp.sin(a0)
        lane = lax.broadcasted_iota(jnp.int32, (1, LANES), 1)
        coef_c = jnp.where(lane < 2 * 32, c0, jnp.where(lane < 3 * 32, -s0, s0))
        coef_s = jnp.where(lane < 2 * 32, -s0, jnp.where(lane < 3 * 32, -c0, c0))
        rot_ref[...] = coef_c * tabc_ref[...] + coef_s * tabs_ref[...]

    @pl.when(jnp.logical_not(consecutive))
    def _():
        ang = pos.astype(jnp.float32) * freq_ref[...]
        rot_ref[...] = jnp.cos(ang - phase_ref[...]) * sign_ref[...]

    rot = rot_ref[...]

    h = _layer_norm(x_ref[...], ln0g_ref[...], ln0b_ref[...])
    h_ref[...] = h
    z = jnp.dot(h.astype(jnp.bfloat16), win_ref[...], preferred_element_type=jnp.float32)

    u = _gelu_tanh(z[:, :GM_WIDTH])
    v = _gelu_tanh(z[:, GM_WIDTH:2 * GM_WIDTH])
    row_chunk = lax.broadcasted_iota(jnp.int32, (GM_CHUNK, GM_CHUNK), 0) // CHUNK
    col_chunk = lax.broadcasted_iota(jnp.int32, (GM_CHUNK, GM_CHUNK), 1) // CHUNK
    allowed = col_chunk <= row_chunk
    for hd in range(GM_HEADS):
        lo, hi = hd * GM_HEAD_DIM, (hd + 1) * GM_HEAD_DIM
        vln = _layer_norm(v[:, lo:hi], gmg_ref[:, lo:hi], gmb_ref[:, lo:hi]).astype(jnp.bfloat16)
        wm = jnp.where(allowed, ws_ref[hd], 0.0).astype(jnp.bfloat16)
        for c in range(tb // GM_CHUNK):
            r0, r1 = c * GM_CHUNK, (c + 1) * GM_CHUNK
            f = jnp.dot(wm, vln[r0:r1], preferred_element_type=jnp.float32) + bs_ref[hd]
            outa_ref[r0:r1, lo:hi] = (u[r0:r1, lo:hi] * f).astype(jnp.bfloat16)

    ql = _rms_norm(z[:, O_Q:O_KV], qg_ref[...]).astype(jnp.bfloat16)
    qf = jnp.dot(ql, wuq_ref[...], preferred_element_type=jnp.float32)
    rot_s = rot * QK_SCALE
    q_parts = []
    for hd in range(MLA_HEADS):
        base = hd * 2 * LANES
        q_parts.append(qf[:, base:base + LANES] * QK_SCALE)
        q_parts.append(qf[:, base + LANES:base + 2 * LANES] * rot_s)
    q_ref[...] = jnp.concatenate(q_parts, axis=-1).astype(jnp.bfloat16)

    kvl = _rms_norm(z[:, O_KV:O_KR], kvg_ref[...]).astype(jnp.bfloat16)
    kv = jnp.dot(kvl, wukv_ref[...], preferred_element_type=jnp.float32)
    t = z[:, O_KR:O_KR + LANES] * rot
    krr = t + pltpu.roll(t, 2 * QK_ROPE_DIM // 2, axis=1)
    k_parts, v_parts = [], []
    for hd in range(MLA_HEADS):
        base = hd * 2 * LANES
        k_parts.append(kv[:, base:base + LANES])
        k_parts.append(krr)
        v_parts.append(kv[:, base + LANES:base + 2 * LANES])
    k_ref[...] = jnp.concatenate(k_parts, axis=-1).astype(jnp.bfloat16)
    v_ref[...] = jnp.concatenate(v_parts, axis=-1).astype(jnp.bfloat16)


def _prep(x2, pos2, ln0g, ln0b, win, gmg, gmb, ws, bs, qg, wuq, kvg, wukv, freq, phase, sign):
    tb = PREP_TOKENS
    full = lambda shape: pl.BlockSpec(shape, lambda i: (0,) * len(shape))
    tok = lambda cols: pl.BlockSpec((tb, cols), lambda i: (i, 0))
    return pl.pallas_call(
        _prep_kernel,
        grid=(N_TOK // tb,),
        in_specs=[tok(D_MODEL), tok(1), full((1, D_MODEL)), full((1, D_MODEL)),
                  full((D_MODEL, IN_COLS)), full((1, GM_WIDTH)), full((1, GM_WIDTH)),
                  full((GM_HEADS, GM_CHUNK, GM_CHUNK)), full((GM_HEADS, GM_CHUNK, GM_HEAD_DIM)),
                  full((1, Q_LORA_RANK)), full((Q_LORA_RANK, D_MODEL)),
                  full((1, KV_LORA_RANK)), full((KV_LORA_RANK, D_MODEL)),
                  full((1, LANES)), full((1, LANES)), full((1, LANES))],
        out_specs=[tok(D_MODEL), tok(GM_WIDTH), tok(D_MODEL), tok(D_MODEL), tok(GM_WIDTH)],
        out_shape=[jax.ShapeDtypeStruct((N_TOK, D_MODEL), jnp.float32),
                   jax.ShapeDtypeStruct((N_TOK, GM_WIDTH), jnp.bfloat16),
                   jax.ShapeDtypeStruct((N_TOK, D_MODEL), jnp.bfloat16),
                   jax.ShapeDtypeStruct((N_TOK, D_MODEL), jnp.bfloat16),
                   jax.ShapeDtypeStruct((N_TOK, GM_WIDTH), jnp.bfloat16)],
        scratch_shapes=[pltpu.VMEM((tb, LANES), jnp.float32)] * 3,
        compiler_params=pltpu.CompilerParams(
            dimension_semantics=("arbitrary",), vmem_limit_bytes=VMEM_LIMIT),
        name="prep",
    )(x2, pos2, ln0g, ln0b, win, gmg, gmb, ws, bs, qg, wuq, kvg, wukv, freq, phase, sign)


def _attn_kernel(q_ref, k_ref, v_ref, o_ref, s_ref, mx_ref, ls_ref, acc_ref):
    qi = pl.program_id(1)
    tq = q_ref.shape[0]
    nt = (((1,), (1,)), ((), ()))
    n_kv = SEQ // ATT_K

    half = tq // 2
    row_chunk = (lax.broadcasted_iota(jnp.int32, (half, half), 0)) // CHUNK
    col_chunk = (lax.broadcasted_iota(jnp.int32, (half, half), 1)) // CHUNK
    quad_allowed = col_chunk <= row_chunk
    diag_start = pl.multiple_of(qi * ATT_K, ATT_K)
    masked = jnp.float32(-1e30)

    def lane_tiles(x):
        return [x[:, t * LANES:(t + 1) * LANES] for t in range(x.shape[1] // LANES)]

    def lane_fold(x, op):
        return functools.reduce(op, lane_tiles(x))

    def scores(hd, j):
        start = pl.multiple_of(j * ATT_K, ATT_K)
        q_h = q_ref[:, hd * 2 * LANES:(hd + 1) * 2 * LANES]
        kb = k_ref[pl.ds(start, ATT_K), hd * 2 * LANES:(hd + 1) * 2 * LANES]
        return lax.dot_general(q_h, kb, nt, preferred_element_type=jnp.float32)

    for hd in range(MLA_HEADS):
        cols = slice(hd * 2 * LANES, (hd + 1) * 2 * LANES)
        k_lo = k_ref[pl.ds(diag_start, half), cols]
        k_hi = k_ref[pl.ds(diag_start + half, half), cols]
        qk = lambda q, k: lax.dot_general(q, k, nt, preferred_element_type=jnp.float32)
        s_tl = jnp.where(quad_allowed, qk(q_ref[:half, cols], k_lo), masked)
        s_bl = qk(q_ref[half:, cols], k_lo)
        s_br = jnp.where(quad_allowed, qk(q_ref[half:, cols], k_hi), masked)
        diag = s_ref.at[hd * n_kv + qi]
        diag[:half, :half] = s_tl
        diag[half:, :half] = s_bl
        diag[half:, half:] = s_br
        mx_ref[hd, :half] = lane_fold(s_tl, jnp.maximum)
        mx_ref[hd, half:] = jnp.maximum(lane_fold(s_bl, jnp.maximum), lane_fold(s_br, jnp.maximum))

    def pass_a(j, c):
        for hd in range(MLA_HEADS):
            s = scores(hd, j)
            s_ref[hd * n_kv + j] = s
            mx_ref[hd] = jnp.maximum(mx_ref[hd], lane_fold(s, jnp.maximum))
        return c

    lax.fori_loop(0, qi, pass_a, 0)

    for hd in range(MLA_HEADS):
        mx_ref[hd] = jnp.broadcast_to(jnp.max(mx_ref[hd], axis=-1, keepdims=True), (tq, LANES))

    def probs(hd, j):
        s = s_ref[hd * n_kv + j]
        mb = mx_ref[hd]
        p = jnp.exp2(jnp.concatenate([t - mb for t in lane_tiles(s)], axis=-1))
        start = pl.multiple_of(j * ATT_K, ATT_K)
        vb = v_ref[pl.ds(start, ATT_K), hd * LANES:(hd + 1) * LANES]
        pv = jnp.dot(p.astype(jnp.bfloat16), vb, preferred_element_type=jnp.float32)
        return lane_fold(p, jnp.add), pv

    for hd in range(MLA_HEADS):
        diag = s_ref.at[hd * n_kv + qi]
        vcols = slice(hd * LANES, (hd + 1) * LANES)
        p_top = jnp.exp2(jnp.concatenate([t - mx_ref[hd, :half] for t in lane_tiles(diag[:half, :half])], axis=-1))
        p_bot = jnp.exp2(jnp.concatenate([t - mx_ref[hd, half:] for t in lane_tiles(diag[half:, :])], axis=-1))
        ls_ref[hd, :half] = lane_fold(p_top, jnp.add)
        ls_ref[hd, half:] = lane_fold(p_bot, jnp.add)
        acc_ref[hd, :half] = jnp.dot(p_top.astype(jnp.bfloat16), v_ref[pl.ds(diag_start, half), vcols],
                                     preferred_element_type=jnp.float32)
        acc_ref[hd, half:] = jnp.dot(p_bot.astype(jnp.bfloat16), v_ref[pl.ds(diag_start, ATT_K), vcols],
                                     preferred_element_type=jnp.float32)

    def pass_b(j, c):
        for hd in range(MLA_HEADS):
            ls, pv = probs(hd, j)
            ls_ref[hd] = ls_ref[hd] + ls
            acc_ref[hd] = acc_ref[hd] + pv
        return c

    lax.fori_loop(0, qi, pass_b, 0)

    for hd in range(MLA_HEADS):
        l = jnp.sum(ls_ref[hd], axis=-1, keepdims=True)
        o_ref[:, hd * V_HEAD_DIM:(hd + 1) * V_HEAD_DIM] = (acc_ref[hd] / l).astype(jnp.bfloat16)


def _attn(q, k, v):
    tq = ATT_Q
    tokblk = lambda cols: pl.BlockSpec((tq, cols), lambda b, i: (b * N_QBLK + i, 0))
    seqblk = lambda cols: pl.BlockSpec((SEQ, cols), lambda b, i: (b, 0))
    return pl.pallas_call(
        _attn_kernel,
        grid=(BATCH, N_QBLK),
        in_specs=[tokblk(D_MODEL), seqblk(D_MODEL), seqblk(GM_WIDTH)],
        out_specs=tokblk(GM_WIDTH),
        out_shape=jax.ShapeDtypeStruct((N_TOK, MLA_HEADS * V_HEAD_DIM), jnp.bfloat16),
        scratch_shapes=[pltpu.VMEM((MLA_HEADS * (SEQ // ATT_K), tq, ATT_K), jnp.float32),
                        pltpu.VMEM((MLA_HEADS, tq, LANES), jnp.float32),
                        pltpu.VMEM((MLA_HEADS, tq, LANES), jnp.float32),
                        pltpu.VMEM((MLA_HEADS, tq, V_HEAD_DIM), jnp.float32)],
        compiler_params=pltpu.CompilerParams(
            dimension_semantics=("arbitrary", "arbitrary"), vmem_limit_bytes=VMEM_LIMIT),
        name="attn",
    )(q, k, v)


def _proj_kernel(outa_ref, ob_ref, h_ref, wout_ref, ln1g_ref, ln1b_ref, wr_ref, br_ref,
                 h1r_ref, ri_ref, rf_ref, proj_ref):
    i = pl.program_id(0)
    tb = outa_ref.shape[0]

    @pl.when(i == 0)
    def _():
        proj_ref[...] = jnp.zeros_like(proj_ref)

    h1 = _layer_norm(ALPHA * h_ref[...] + proj_ref[(i + 1) % 2], ln1g_ref[...], ln1b_ref[...])
    _to_row_tiles(h1r_ref, h1)

    logits_tm = jnp.dot(h1.astype(jnp.bfloat16), wr_ref[...], preferred_element_type=jnp.float32)
    logits = logits_tm.T[0:ROUTER_ROWS] + br_ref[...]

    sub_i = lax.broadcasted_iota(jnp.int32, (SUBLANES, tb), 0)
    sub = sub_i.astype(jnp.float32)
    neg = jnp.float32(-jnp.inf)
    g = jnp.where(sub_i < N_GROUPS, logits[0:SUBLANES], neg)
    gmax = jnp.max(g, axis=0, keepdims=True)
    g_top = jnp.min(jnp.where(g == gmax, sub, float(SUBLANES)), axis=0, keepdims=True)
    p_group = 1.0 / jnp.sum(jnp.exp(g - gmax), axis=0, keepdims=True)
    sel = logits[SUBLANES:2 * SUBLANES]
    for grp in range(1, N_GROUPS):
        sel = jnp.where(g_top == float(grp), logits[(grp + 1) * SUBLANES:(grp + 2) * SUBLANES], sel)
    v1 = jnp.max(sel, axis=0, keepdims=True)
    i1 = jnp.min(jnp.where(sel == v1, sub, float(SUBLANES)), axis=0, keepdims=True)
    sel2 = jnp.where(sub == i1, neg, sel)
    v2 = jnp.max(sel2, axis=0, keepdims=True)
    i2 = jnp.min(jnp.where(sel2 == v2, sub, float(SUBLANES)), axis=0, keepdims=True)
    e21 = jnp.exp(v2 - v1)
    w1 = 1.0 / (1.0 + e21)
    gate1 = p_group * w1
    gate2 = p_group * (e21 * w1)
    e1 = g_top * EXPERTS_PER_GROUP + i1
    e2 = g_top * EXPERTS_PER_GROUP + i2
    ri_ref[...] = jnp.where(sub_i == 0, e1, jnp.where(sub_i == 1, e2, 0.0)).astype(jnp.int32)
    rf_ref[...] = jnp.where(sub_i == 0, gate1, jnp.where(sub_i == 1, gate2, 0.0))

    proj_ref[i % 2] = (jnp.dot(outa_ref[...], wout_ref[:GM_WIDTH, :], preferred_element_type=jnp.float32)
                       + jnp.dot(ob_ref[...], wout_ref[GM_WIDTH:, :], preferred_element_type=jnp.float32))


def _proj(outa, ob, h, wout, ln1g, ln1b, wr, br):
    tb = PROJ_TOKENS
    n_blk = N_TOK // tb
    cur = lambda i: jnp.minimum(i, n_blk - 1)
    prev = lambda i: jnp.maximum(i - 1, 0)
    full = lambda shape: pl.BlockSpec(shape, lambda i: (0,) * len(shape))
    return pl.pallas_call(
        _proj_kernel,
        grid=(n_blk + 1,),
        in_specs=[pl.BlockSpec((tb, GM_WIDTH), lambda i: (cur(i), 0)),
                  pl.BlockSpec((tb, GM_WIDTH), lambda i: (cur(i), 0)),
                  pl.BlockSpec((tb, D_MODEL), lambda i: (prev(i), 0)),
                  full((D_MODEL, D_MODEL)), full((1, D_MODEL)), full((1, D_MODEL)),
                  full((D_MODEL, LANES)), full((ROUTER_ROWS, 1))],
        out_specs=[pl.BlockSpec((tb * FEAT_TILES, LANES), lambda i: (prev(i), 0)),
                   pl.BlockSpec((SUBLANES, tb), lambda i: (0, prev(i))),
                   pl.BlockSpec((SUBLANES, tb), lambda i: (0, prev(i)))],
        out_shape=[jax.ShapeDtypeStruct((N_TOK * FEAT_TILES, LANES), jnp.float32),
                   jax.ShapeDtypeStruct((SUBLANES, N_TOK), jnp.int32),
                   jax.ShapeDtypeStruct((SUBLANES, N_TOK), jnp.float32)],
        scratch_shapes=[pltpu.VMEM((2, tb, D_MODEL), jnp.float32)],
        compiler_params=pltpu.CompilerParams(
            dimension_semantics=("arbitrary",), vmem_limit_bytes=VMEM_LIMIT),
        name="proj",
    )(outa, ob, h, wout, ln1g, ln1b, wr, br)


def _plan_kernel(ri_all_ref, ri_ref, ldest_ref, runs_ref, meta_ref, run_ref, start_ref, upper_ref):
    step = pl.program_id(0)
    tb = ri_ref.shape[1]
    f32 = jnp.float32
    er = lax.broadcasted_iota(jnp.int32, (N_EXPERTS, LANES), 0)
    ec = lax.broadcasted_iota(jnp.int32, (N_EXPERTS, LANES), 1)
    to_row = lambda col: jnp.sum(jnp.where(er == ec, col, 0.0), axis=0, keepdims=True)

    def expert_one_hot(ref):
        e_sub = lax.broadcasted_iota(jnp.int32, (N_EXPERTS, ref.shape[1]), 0)
        return e_sub == ref[0:1, :], e_sub == ref[1:2, :]

    @pl.when(step == 0)
    def _():
        oh1, oh2 = expert_one_hot(ri_all_ref)
        counts = jnp.sum(jnp.where(oh1 | oh2, 1.0, 0.0), axis=1, keepdims=True)
        padded = jnp.floor((counts + (EXPERT_ROWS - 1)) * (1.0 / EXPERT_ROWS)) * EXPERT_ROWS
        pad_end = jnp.sum(jnp.where(ec <= er, to_row(padded), 0.0), axis=1, keepdims=True)
        start_ref[...] = jnp.broadcast_to(pad_end - padded, start_ref.shape)
        run_ref[...] = jnp.zeros_like(run_ref)
        bstart = (lax.broadcasted_iota(jnp.int32, (N_EXPERTS, META_LANES), 1) * EXPERT_ROWS).astype(f32)
        blk_e = jnp.sum(jnp.where(pad_end <= bstart, 1.0, 0.0), axis=0, keepdims=True)
        blk_e = jnp.minimum(blk_e, N_EXPERTS - 1.0)
        n_used = pad_end[N_EXPERTS - 1:N_EXPERTS, :] * (1.0 / EXPERT_ROWS)
        pad3 = lambda r: jnp.concatenate(
            [r, jnp.zeros((1, META_LANES - LANES), f32)], axis=1)
        msub = lax.broadcasted_iota(jnp.int32, (SUBLANES, META_LANES), 0)
        meta = jnp.where(msub == 0, blk_e,
                         jnp.where(msub == 1, pad3(to_row(pad_end)),
                                   jnp.where(msub == 2, pad3(to_row(counts)),
                                             jnp.where(msub == 3, n_used, 0.0))))
        meta_ref[...] = meta.astype(jnp.int32)
        tr = lax.broadcasted_iota(jnp.int32, (tb, tb), 0)
        tc = lax.broadcasted_iota(jnp.int32, (tb, tb), 1)
        upper_ref[...] = jnp.where(tr < tc, 1.0, 0.0).astype(jnp.bfloat16)

    @pl.when(step > 0)
    def _():
        oh1, oh2 = expert_one_hot(ri_ref)
        oh = jnp.where(oh1 | oh2, 1.0, 0.0).astype(f32)
        blk_count = jnp.sum(oh, axis=1, keepdims=True)
        prefix = jnp.dot(oh.astype(jnp.bfloat16), upper_ref[...], preferred_element_type=f32)
        cnt_row = to_row(blk_count)
        lstart = jnp.sum(jnp.where(ec < er, cnt_row, 0.0), axis=1, keepdims=True)
        base = prefix + lstart
        d1 = jnp.sum(jnp.where(oh1, base, 0.0), axis=0, keepdims=True)
        d2 = jnp.sum(jnp.where(oh2, base, 0.0), axis=0, keepdims=True)
        sub = lax.broadcasted_iota(jnp.int32, (SUBLANES, tb), 0)
        ldest_ref[...] = jnp.where(sub == 0, d1, jnp.where(sub == 1, d2, 0.0)).astype(jnp.int32)
        gstart = start_ref[:, 0:1] + run_ref[:, 0:1]
        rsub = lax.broadcasted_iota(jnp.int32, (SUBLANES, LANES), 0)
        runs = jnp.where(rsub == 0, cnt_row,
                         jnp.where(rsub == 1, to_row(lstart), jnp.where(rsub == 2, to_row(gstart), 0.0)))
        runs_ref[...] = runs.astype(jnp.int32)
        run_ref[...] = run_ref[...] + blk_count


def _plan(ri):
    tb = MOE_TOKENS
    blk = lambda i: jnp.maximum(i - 1, 0)
    return pl.pallas_call(
        _plan_kernel,
        grid=(N_MOE_BLOCKS + 1,),
        in_specs=[pl.BlockSpec((SUBLANES, N_TOK), lambda i: (0, 0)),
                  pl.BlockSpec((SUBLANES, tb), lambda i: (0, blk(i)))],
        out_specs=[pl.BlockSpec((SUBLANES, tb), lambda i: (0, blk(i))),
                   pl.BlockSpec((SUBLANES, LANES), lambda i: (blk(i), 0)),
                   pl.BlockSpec((SUBLANES, META_LANES), lambda i: (0, 0))],
        out_shape=[jax.ShapeDtypeStruct((SUBLANES, N_TOK), jnp.int32),
                   jax.ShapeDtypeStruct((N_MOE_BLOCKS * SUBLANES, LANES), jnp.int32),
                   jax.ShapeDtypeStruct((SUBLANES, META_LANES), jnp.int32)],
        scratch_shapes=[pltpu.VMEM((N_EXPERTS, LANES), jnp.float32),
                        pltpu.VMEM((N_EXPERTS, LANES), jnp.float32),
                        pltpu.VMEM((tb, tb), jnp.bfloat16)],
        compiler_params=pltpu.CompilerParams(
            dimension_semantics=("arbitrary",), vmem_limit_bytes=VMEM_LIMIT),
        name="plan",
    )(ri, ri)


def _for_each_run_piece(runs_ref, fn):
    for e in range(N_EXPERTS):
        n, lstart, gstart = runs_ref[0, e], runs_ref[1, e], runs_ref[2, e]
        for bit in range(RUN_BITS):
            @pl.when((n & (1 << bit)) != 0)
            def _(n=n, lstart=lstart, gstart=gstart, bit=bit):
                off = (n >> (bit + 1)) << (bit + 1)
                fn(lstart + off, gstart + off, 1 << bit)


def _tile_rows(ref, row, rows):
    return ref.at[pl.ds(pl.multiple_of(row * FEAT_TILES, FEAT_TILES), rows * FEAT_TILES)]


def _dispatch_kernel(meta_ref, runs_ref, ldest_ref, h1r_ref, buf_ref, sorted_ref, zero_ref, sems, zsem):
    i = pl.program_id(0)
    n_steps = pl.num_programs(0)
    tb = ldest_ref.shape[1]
    slot = i % 2
    block_tiles = 2 * tb * FEAT_TILES

    def wait_slot(s):
        pltpu.make_async_copy(sorted_ref.at[s], buf_ref.at[pl.ds(0, block_tiles)], sems.at[s]).wait()

    @pl.when(i == 0)
    def _():
        zero_ref[...] = jnp.zeros_like(zero_ref)

        def zero_copy(e):
            start = pl.multiple_of((meta_ref[1, e] - EXPERT_ROWS) * FEAT_TILES, EXPERT_ROWS * FEAT_TILES)
            return pltpu.make_async_copy(
                zero_ref, buf_ref.at[pl.ds(start, EXPERT_ROWS * FEAT_TILES)], zsem)

        def start_zero(e, c):
            @pl.when(meta_ref[2, e] > 0)
            def _():
                zero_copy(e).start()
            return c

        def wait_zero(e, c):
            @pl.when(meta_ref[2, e] > 0)
            def _():
                zero_copy(e).wait()
            return c

        def tail_copy(b):
            start = pl.multiple_of(b * (EXPERT_ROWS * FEAT_TILES), EXPERT_ROWS * FEAT_TILES)
            return pltpu.make_async_copy(
                zero_ref, buf_ref.at[pl.ds(start, EXPERT_ROWS * FEAT_TILES)], zsem)

        def start_tail(b, c):
            tail_copy(b).start()
            return c

        def wait_tail(b, c):
            tail_copy(b).wait()
            return c

        lax.fori_loop(0, N_EXPERTS, start_zero, 0)
        lax.fori_loop(meta_ref[3, 0], N_ROW_BLOCKS, start_tail, 0)
        lax.fori_loop(0, N_EXPERTS, wait_zero, 0)
        lax.fori_loop(meta_ref[3, 0], N_ROW_BLOCKS, wait_tail, 0)

    @pl.when(i >= 2)
    def _():
        wait_slot(slot)

    x = _from_row_tiles(h1r_ref, tb).astype(jnp.bfloat16)
    ld0 = ldest_ref[0:1, :]
    ld1 = ldest_ref[1:2, :]
    for c in range(2 * tb // SORT_CHUNK):
        r = lax.broadcasted_iota(jnp.int32, (SORT_CHUNK, tb), 0) + c * SORT_CHUNK
        perm = jnp.where((r == ld0) | (r == ld1), 1.0, 0.0).astype(jnp.bfloat16)
        rows = jnp.dot(perm, x, preferred_element_type=jnp.float32)
        _to_row_tiles(sorted_ref.at[slot, pl.ds(c * SORT_CHUNK * FEAT_TILES, SORT_CHUNK * FEAT_TILES)], rows)

    def send(lrow, grow, rows):
        pltpu.make_async_copy(_tile_rows(sorted_ref.at[slot], lrow, rows),
                              _tile_rows(buf_ref, grow, rows), sems.at[slot]).start()

    _for_each_run_piece(runs_ref, send)

    @pl.when(i == n_steps - 1)
    def _():
        wait_slot(slot)
        wait_slot(1 - slot)


def _dispatch(meta, runs, ldest, h1r):
    tb = MOE_TOKENS
    return pl.pallas_call(
        _dispatch_kernel,
        grid_spec=pltpu.PrefetchScalarGridSpec(
            num_scalar_prefetch=1,
            grid=(N_MOE_BLOCKS,),
            in_specs=[pl.BlockSpec((SUBLANES, LANES), lambda i, m: (i, 0), memory_space=pltpu.SMEM),
                      pl.BlockSpec((SUBLANES, tb), lambda i, m: (0, i)),
                      pl.BlockSpec((tb * FEAT_TILES, LANES), lambda i, m: (i, 0))],
            out_specs=pl.BlockSpec(memory_space=pl.ANY),
            scratch_shapes=[pltpu.VMEM((2, 2 * tb * FEAT_TILES, LANES), jnp.float32),
                            pltpu.VMEM((EXPERT_ROWS * FEAT_TILES, LANES), jnp.float32),
                            pltpu.SemaphoreType.DMA((2,)), pltpu.SemaphoreType.DMA]),
        out_shape=jax.ShapeDtypeStruct((N_ROWS * FEAT_TILES, LANES), jnp.float32),
        compiler_params=pltpu.CompilerParams(
            dimension_semantics=("arbitrary",), vmem_limit_bytes=VMEM_LIMIT),
        name="dispatch",
    )(meta, runs, ldest, h1r)


def _sub_block_expert(meta, i, sub):
    return meta[0, jnp.minimum(i + sub * (N_ROW_BLOCKS // EXPERT_SUB), meta[3, 0] - 1)]


def _experts_kernel(meta_ref, x_ref, wg0_ref, wu0_ref, wd0_ref, wg1_ref, wu1_ref, wd1_ref, o_ref,
                    wgb_ref, wub_ref, wdb_ref, cached_ref):
    n = EXPERT_ROWS
    i = pl.program_id(0)
    weights = ((wg0_ref, wu0_ref, wd0_ref), (wg1_ref, wu1_ref, wd1_ref))

    @pl.when(i == 0)
    def _():
        for sub in range(EXPERT_SUB):
            cached_ref[sub] = -1

    for sub in range(EXPERT_SUB):
        expert = _sub_block_expert(meta_ref, i, sub)

        @pl.when(cached_ref[sub] != expert)
        def _(sub=sub, expert=expert):
            wg_ref, wu_ref, wd_ref = weights[sub]
            wgb_ref[sub] = wg_ref[...].astype(jnp.bfloat16)
            wub_ref[sub] = wu_ref[...].astype(jnp.bfloat16)
            wdb_ref[sub] = wd_ref[...].astype(jnp.bfloat16)
            cached_ref[sub] = expert

    for sub in range(EXPERT_SUB):
        x = _from_row_tiles(x_ref.at[sub], n).astype(jnp.bfloat16)
        gate = jnp.dot(x, wgb_ref[sub], preferred_element_type=jnp.float32)
        up = jnp.dot(x, wub_ref[sub], preferred_element_type=jnp.float32)
        act = (gate * jax.nn.sigmoid(gate) * up).astype(jnp.bfloat16)
        _to_row_tiles(o_ref.at[sub], jnp.dot(act, wdb_ref[sub], preferred_element_type=jnp.float32))


def _experts(meta, buf, wg, wu, wd):
    sub_tiles = (N_ROW_BLOCKS // EXPERT_SUB) * EXPERT_ROWS * FEAT_TILES

    def weight_spec(shape, sub):
        return pl.BlockSpec((None,) + shape, lambda i, m: (_sub_block_expert(m, i, sub), 0, 0))

    rows_spec = pl.BlockSpec((EXPERT_SUB, EXPERT_ROWS * FEAT_TILES, LANES), lambda i, m: (0, i, 0))
    up_shape, down_shape = (D_MODEL, EXPERT_FF), (EXPERT_FF, D_MODEL)
    eout = pl.pallas_call(
        _experts_kernel,
        grid_spec=pltpu.PrefetchScalarGridSpec(
            num_scalar_prefetch=1,
            grid=(N_ROW_BLOCKS // EXPERT_SUB,),
            in_specs=[rows_spec,
                      weight_spec(up_shape, 0), weight_spec(up_shape, 0), weight_spec(down_shape, 0),
                      weight_spec(up_shape, 1), weight_spec(up_shape, 1), weight_spec(down_shape, 1)],
            out_specs=rows_spec,
            scratch_shapes=[pltpu.VMEM((EXPERT_SUB,) + up_shape, jnp.bfloat16),
                            pltpu.VMEM((EXPERT_SUB,) + up_shape, jnp.bfloat16),
                            pltpu.VMEM((EXPERT_SUB,) + down_shape, jnp.bfloat16),
                            pltpu.SMEM((EXPERT_SUB,), jnp.int32)]),
        out_shape=jax.ShapeDtypeStruct((EXPERT_SUB, sub_tiles, LANES), jnp.float32),
        compiler_params=pltpu.CompilerParams(
            dimension_semantics=("arbitrary",), vmem_limit_bytes=VMEM_LIMIT),
        name="experts",
    )(meta, buf.reshape(EXPERT_SUB, sub_tiles, LANES), wg, wu, wd, wg, wu, wd)
    return eout.reshape(N_ROWS * FEAT_TILES, LANES)


def _combine_kernel(runs_ref, runs_next_ref, ldest_ref, rf_ref, h1r_ref, eout_ref, ln2g_ref, ln2b_ref,
                    o_ref, y_ref, sems):
    i = pl.program_id(0)
    n_steps = pl.num_programs(0)
    tb = ldest_ref.shape[1]
    slot = i % 2
    block_tiles = 2 * tb * FEAT_TILES

    def fetch(table_ref, s):
        def recv(lrow, grow, rows):
            pltpu.make_async_copy(_tile_rows(eout_ref, grow, rows),
                                  _tile_rows(y_ref.at[s], lrow, rows), sems.at[s]).start()
        _for_each_run_piece(table_ref, recv)

    @pl.when(i == 0)
    def _():
        fetch(runs_ref, slot)

    @pl.when(i + 1 < n_steps)
    def _():
        fetch(runs_next_ref, 1 - slot)

    pltpu.make_async_copy(eout_ref.at[pl.ds(0, block_tiles)], y_ref.at[slot], sems.at[slot]).wait()

    ld = ldest_ref[...].astype(jnp.float32).T
    gates = rf_ref[...].T
    y = None
    col = lax.broadcasted_iota(jnp.int32, (tb, SORT_CHUNK), 1).astype(jnp.float32)
    for c in range(2 * tb // SORT_CHUNK):
        ld_c = ld - float(c * SORT_CHUNK)
        g = jnp.where(col == ld_c[:, 0:1], gates[:, 0:1],
                      jnp.where(col == ld_c[:, 1:2], gates[:, 1:2], 0.0)).astype(jnp.bfloat16)
        rows = _from_row_tiles(
            y_ref.at[slot, pl.ds(c * SORT_CHUNK * FEAT_TILES, SORT_CHUNK * FEAT_TILES)], SORT_CHUNK)
        part = jnp.dot(g, rows.astype(jnp.bfloat16), preferred_element_type=jnp.float32)
        y = part if y is None else y + part
    h1 = _from_row_tiles(h1r_ref, tb)
    o_ref[...] = _layer_norm(ALPHA * h1 + y, ln2g_ref[...], ln2b_ref[...])


def _combine(runs, ldest, rf, h1r, eout, ln2g, ln2b):
    tb = MOE_TOKENS
    last = N_MOE_BLOCKS - 1
    return pl.pallas_call(
        _combine_kernel,
        grid=(N_MOE_BLOCKS,),
        in_specs=[pl.BlockSpec((SUBLANES, LANES), lambda i: (i, 0), memory_space=pltpu.SMEM),
                  pl.BlockSpec((SUBLANES, LANES), lambda i: (jnp.minimum(i + 1, last), 0),
                               memory_space=pltpu.SMEM),
                  pl.BlockSpec((SUBLANES, tb), lambda i: (0, i)),
                  pl.BlockSpec((SUBLANES, tb), lambda i: (0, i)),
                  pl.BlockSpec((tb * FEAT_TILES, LANES), lambda i: (i, 0)),
                  pl.BlockSpec(memory_space=pl.ANY),
                  pl.BlockSpec((1, D_MODEL), lambda i: (0, 0)),
                  pl.BlockSpec((1, D_MODEL), lambda i: (0, 0))],
        out_specs=pl.BlockSpec((tb, D_MODEL), lambda i: (i, 0)),
        out_shape=jax.ShapeDtypeStruct((N_TOK, D_MODEL), jnp.float32),
        scratch_shapes=[pltpu.VMEM((2, 2 * tb * FEAT_TILES, LANES), jnp.float32),
                        pltpu.SemaphoreType.DMA((2,))],
        compiler_params=pltpu.CompilerParams(
            dimension_semantics=("arbitrary",), vmem_limit_bytes=VMEM_LIMIT),
        name="combine",
    )(runs, runs, ldest, rf, h1r, eout, ln2g, ln2b)


def _swap_halves(w):
    half = w.shape[-1] // 2
    return jnp.concatenate([w[..., half:], w[..., :half]], axis=-1)


def kernel(x, positions, ln0_g, ln0_b, w_in, gm_ln_g, gm_ln_b, w_spatial, b_spatial, q_norm_g, w_uq, kv_norm_g, w_ukv, w_out, ln1_g, ln1_b, w_router_group, b_router_group, w_router_expert, b_router_expert, w_gate, w_up, w_down, ln2_g, ln2_b):
    bf16 = jnp.bfloat16
    row = lambda a: a.reshape(1, -1)

    w_in0 = w_in[0]
    kr_cols = w_in0[:, O_KR:O_KR + QK_ROPE_DIM]
    win = jnp.concatenate([w_in0, _swap_halves(kr_cols)], axis=1).astype(bf16)
    wuq3 = w_uq[0].reshape(Q_LORA_RANK, MLA_HEADS, QK_NOPE_DIM + QK_ROPE_DIM)
    rope_cols = wuq3[:, :, QK_NOPE_DIM:]
    wuq = jnp.concatenate([wuq3, _swap_halves(rope_cols)], axis=-1).reshape(Q_LORA_RANK, D_MODEL).astype(bf16)
    wukv = w_ukv[0].astype(bf16)
    wout = w_out[0].astype(bf16)
    bs = jnp.broadcast_to(b_spatial[0][:, :, None], (GM_HEADS, GM_CHUNK, GM_HEAD_DIM))
    wr = jnp.concatenate([w_router_group[0], jnp.zeros((D_MODEL, SUBLANES - N_GROUPS), jnp.float32),
                          w_router_expert[0],
                          jnp.zeros((D_MODEL, LANES - ROUTER_ROWS), jnp.float32)],
                         axis=1).astype(bf16)
    br = jnp.concatenate([b_router_group[0], jnp.zeros((SUBLANES - N_GROUPS,), jnp.float32),
                          b_router_expert[0]]).reshape(ROUTER_ROWS, 1)

    inv_freq = ROPE_THETA ** (-jnp.arange(0, QK_ROPE_DIM, 2, dtype=jnp.float32) / QK_ROPE_DIM)
    freq = jnp.tile(inv_freq, 4).reshape(1, LANES)
    quarter = QK_ROPE_DIM // 2
    phase = jnp.concatenate([jnp.zeros((2 * quarter,), jnp.float32),
                             jnp.full((2 * quarter,), math.pi / 2, jnp.float32)]).reshape(1, LANES)
    sign = jnp.concatenate([jnp.ones((2 * quarter,), jnp.float32), -jnp.ones((quarter,), jnp.float32),
                            jnp.ones((quarter,), jnp.float32)]).reshape(1, LANES)

    x2 = x.reshape(N_TOK, D_MODEL)
    pos2 = positions.reshape(N_TOK, 1)

    h, outa, q, k, v = _prep(x2, pos2, row(ln0_g), row(ln0_b), win, row(gm_ln_g[0]), row(gm_ln_b[0]),
                             w_spatial[0], bs, row(q_norm_g[0]), wuq, row(kv_norm_g[0]), wukv,
                             freq, phase, sign)
    ob = _attn(q, k, v)
    h1r, ri, rf = _proj(outa, ob, h, wout, row(ln1_g[0]), row(ln1_b[0]), wr, br)
    ldest, runs, meta = _plan(ri)
    buf = _dispatch(meta, runs, ldest, h1r)
    eout = _experts(meta, buf, w_gate[0], w_up[0], w_down[0])
    out = _combine(runs, ldest, rf, h1r, eout, row(ln2_g[0]), row(ln2_b[0]))
    return out.reshape(BATCH, SEQ, D_MODEL)
```

```python
import functools
import math

import jax
import jax.numpy as jnp
from jax import lax
from jax.experimental import pallas as pl
from jax.experimental.pallas import tpu as pltpu

D_MODEL = 1024
BATCH = 16
SEQ = 2048
N_TOK = BATCH * SEQ
CHUNK = 64
GM_WIDTH = 512
GM_HEADS = 4
GM_HEAD_DIM = 128
GM_CHUNK = 128
MLA_HEADS = 4
QK_NOPE_DIM = 128
QK_ROPE_DIM = 64
V_HEAD_DIM = 128
Q_LORA_RANK = 384
KV_LORA_RANK = 256
ROPE_THETA = 10000.0
N_GROUPS = 4
EXPERTS_PER_GROUP = 8
N_EXPERTS = 32
TOP_K = 2
EXPERT_FF = 256
ALPHA = 2.0 ** 0.25
QK_SCALE = (QK_NOPE_DIM + QK_ROPE_DIM) ** -0.5 * math.log2(math.e)

LANES = 128
SUBLANES = 8
FEAT_TILES = D_MODEL // LANES
PREP_TOKENS = 512
ATT_Q = 512
ATT_K = 512
N_QBLK = SEQ // ATT_Q
PROJ_TOKENS = 512
MOE_TOKENS = 512
N_MOE_BLOCKS = N_TOK // MOE_TOKENS
RUN_BITS = (TOP_K * MOE_TOKENS).bit_length()
SORT_CHUNK = 256
EXPERT_ROWS = 256
EXPERT_SUB = 2
N_ROWS = N_TOK * TOP_K + N_EXPERTS * EXPERT_ROWS
N_ROW_BLOCKS = N_ROWS // EXPERT_ROWS
META_LANES = 384
IN_COLS = 2 * GM_WIDTH + Q_LORA_RANK + KV_LORA_RANK + 2 * QK_ROPE_DIM
O_Q = 2 * GM_WIDTH
O_KV = O_Q + Q_LORA_RANK
O_KR = O_KV + KV_LORA_RANK
ROUTER_ROWS = 40
VMEM_LIMIT = 48 * 1024 * 1024

assert N_ROW_BLOCKS <= META_LANES and N_ROW_BLOCKS % EXPERT_SUB == 0


def _layer_norm(x, g, b, eps=1e-5):
    mu = jnp.mean(x, axis=-1, keepdims=True)
    xc = x - mu
    var = jnp.mean(xc * xc, axis=-1, keepdims=True)
    return xc * lax.rsqrt(var + eps) * g + b


def _rms_norm(x, g, eps=1e-6):
    return x * lax.rsqrt(jnp.mean(x * x, axis=-1, keepdims=True) + eps) * g


def _gelu_tanh(x):
    c = math.sqrt(2.0 / math.pi)
    return 0.5 * x * (1.0 + jnp.tanh(c * (x + 0.044715 * (x * x * x))))


def _to_row_tiles(ref, x):
    n = x.shape[0]
    for s in range(FEAT_TILES):
        ref[pl.ds(s, n, stride=FEAT_TILES), :] = x[:, s * LANES:(s + 1) * LANES]


def _from_row_tiles(ref, n):
    return jnp.concatenate(
        [ref[pl.ds(s, n, stride=FEAT_TILES), :] for s in range(FEAT_TILES)], axis=-1)


def _prep_kernel(x_ref, pos_ref, ln0g_ref, ln0b_ref, win_ref, gmg_ref, gmb_ref, ws_ref, bs_ref,
                 qg_ref, wuq_ref, kvg_ref, wukv_ref, freq_ref, phase_ref, sign_ref,
                 h_ref, outa_ref, q_ref, k_ref, v_ref, tabc_ref, tabs_ref, rot_ref):
    tb = x_ref.shape[0]

    @pl.when(pl.program_id(0) == 0)
    def _():
        d = lax.broadcasted_iota(jnp.int32, (tb, LANES), 0).astype(jnp.float32) * freq_ref[...]
        tabc_ref[...] = jnp.cos(d)
        tabs_ref[...] = jnp.sin(d)

    pos = pos_ref[...]
    p0 = pos_ref[0:1, :]
    offset = lax.broadcasted_iota(jnp.int32, (tb, 1), 0)
    consecutive = jnp.max(jnp.abs((pos - p0 - offset).astype(jnp.float32))) == 0.0

    @pl.when(consecutive)
    def _():
        a0 = p0.astype(jnp.float32) * freq_ref[...]
        c0, s0 = jnp.cos(a0), jnp.sin(a0)
        lane = lax.broadcasted_iota(jnp.int32, (1, LANES), 1)
        coef_c = jnp.where(lane < 2 * 32, c0, jnp.where(lane < 3 * 32, -s0, s0))
        coef_s = jnp.where(lane < 2 * 32, -s0, jnp.where(lane < 3 * 32, -c0, c0))
        rot_ref[...] = coef_c * tabc_ref[...] + coef_s * tabs_ref[...]

    @pl.when(jnp.logical_not(consecutive))
    def _():
        ang = pos.astype(jnp.float32) * freq_ref[...]
        rot_ref[...] = jnp.cos(ang - phase_ref[...]) * sign_ref[...]

    rot = rot_ref[...]

    h = _layer_norm(x_ref[...], ln0g_ref[...], ln0b_ref[...])
    h_ref[...] = h
    z = jnp.dot(h.astype(jnp.bfloat16), win_ref[...], preferred_element_type=jnp.float32)

    u = _gelu_tanh(z[:, :GM_WIDTH])
    v = _gelu_tanh(z[:, GM_WIDTH:2 * GM_WIDTH])
    row_chunk = lax.broadcasted_iota(jnp.int32, (GM_CHUNK, GM_CHUNK), 0) // CHUNK
    col_chunk = lax.broadcasted_iota(jnp.int32, (GM_CHUNK, GM_CHUNK), 1) // CHUNK
    allowed = col_chunk <= row_chunk
    for hd in range(GM_HEADS):
        lo, hi = hd * GM_HEAD_DIM, (hd + 1) * GM_HEAD_DIM
        vln = _layer_norm(v[:, lo:hi], gmg_ref[:, lo:hi], gmb_ref[:, lo:hi]).astype(jnp.bfloat16)
        wm = jnp.where(allowed, ws_ref[hd], 0.0).astype(jnp.bfloat16)
        for c in range(tb // GM_CHUNK):
            r0, r1 = c * GM_CHUNK, (c + 1) * GM_CHUNK
            f = jnp.dot(wm, vln[r0:r1], preferred_element_type=jnp.float32) + bs_ref[hd]
            outa_ref[r0:r1, lo:hi] = (u[r0:r1, lo:hi] * f).astype(jnp.bfloat16)

    ql = _rms_norm(z[:, O_Q:O_KV], qg_ref[...]).astype(jnp.bfloat16)
    qf = jnp.dot(ql, wuq_ref[...], preferred_element_type=jnp.float32)
    rot_s = rot * QK_SCALE
    q_parts = []
    for hd in range(MLA_HEADS):
        base = hd * 2 * LANES
        q_parts.append(qf[:, base:base + LANES] * QK_SCALE)
        q_parts.append(qf[:, base + LANES:base + 2 * LANES] * rot_s)
    q_ref[...] = jnp.concatenate(q_parts, axis=-1).astype(jnp.bfloat16)

    kvl = _rms_norm(z[:, O_KV:O_KR], kvg_ref[...]).astype(jnp.bfloat16)
    kv = jnp.dot(kvl, wukv_ref[...], preferred_element_type=jnp.float32)
    t = z[:, O_KR:O_KR + LANES] * rot
    krr = t + pltpu.roll(t, 2 * QK_ROPE_DIM // 2, axis=1)
    k_parts, v_parts = [], []
    for hd in range(MLA_HEADS):
        base = hd * 2 * LANES
        k_parts.append(kv[:, base:base + LANES])
        k_parts.append(krr)
        v_parts.append(kv[:, base + LANES:base + 2 * LANES])
    k_ref[...] = jnp.concatenate(k_parts, axis=-1).astype(jnp.bfloat16)
    v_ref[...] = jnp.concatenate(v_parts, axis=-1).astype(jnp.bfloat16)


def _prep(x2, pos2, ln0g, ln0b, win, gmg, gmb, ws, bs, qg, wuq, kvg, wukv, freq, phase, sign):
    tb = PREP_TOKENS
    full = lambda shape: pl.BlockSpec(shape, lambda i: (0,) * len(shape))
    tok = lambda cols: pl.BlockSpec((tb, cols), lambda i: (i, 0))
    return pl.pallas_call(
        _prep_kernel,
        grid=(N_TOK // tb,),
        in_specs=[tok(D_MODEL), tok(1), full((1, D_MODEL)), full((1, D_MODEL)),
                  full((D_MODEL, IN_COLS)), full((1, GM_WIDTH)), full((1, GM_WIDTH)),
                  full((GM_HEADS, GM_CHUNK, GM_CHUNK)), full((GM_HEADS, GM_CHUNK, GM_HEAD_DIM)),
                  full((1, Q_LORA_RANK)), full((Q_LORA_RANK, D_MODEL)),
                  full((1, KV_LORA_RANK)), full((KV_LORA_RANK, D_MODEL)),
                  full((1, LANES)), full((1, LANES)), full((1, LANES))],
        out_specs=[tok(D_MODEL), tok(GM_WIDTH), tok(D_MODEL), tok(D_MODEL), tok(GM_WIDTH)],
        out_shape=[jax.ShapeDtypeStruct((N_TOK, D_MODEL), jnp.float32),
                   jax.ShapeDtypeStruct((N_TOK, GM_WIDTH), jnp.bfloat16),
                   jax.ShapeDtypeStruct((N_TOK, D_MODEL), jnp.bfloat16),
                   jax.ShapeDtypeStruct((N_TOK, D_MODEL), jnp.bfloat16),
                   jax.ShapeDtypeStruct((N_TOK, GM_WIDTH), jnp.bfloat16)],
        scratch_shapes=[pltpu.VMEM((tb, LANES), jnp.float32)] * 3,
        compiler_params=pltpu.CompilerParams(
            dimension_semantics=("arbitrary",), vmem_limit_bytes=VMEM_LIMIT),
        name="prep",
    )(x2, pos2, ln0g, ln0b, win, gmg, gmb, ws, bs, qg, wuq, kvg, wukv, freq, phase, sign)


def _attn_kernel(q_ref, k_ref, v_ref, o_ref, s_ref, mx_ref, ls_ref, acc_ref):
    qi = pl.program_id(1)
    tq = q_ref.shape[0]
    nt = (((1,), (1,)), ((), ()))
    n_kv = SEQ // ATT_K

    half = tq // 2
    row_chunk = (lax.broadcasted_iota(jnp.int32, (half, half), 0)) // CHUNK
    col_chunk = (lax.broadcasted_iota(jnp.int32, (half, half), 1)) // CHUNK
    quad_allowed = col_chunk <= row_chunk
    diag_start = pl.multiple_of(qi * ATT_K, ATT_K)
    masked = jnp.float32(-1e30)

    def lane_tiles(x):
        return [x[:, t * LANES:(t + 1) * LANES] for t in range(x.shape[1] // LANES)]

    def lane_fold(x, op):
        return functools.reduce(op, lane_tiles(x))

    def scores(hd, j):
        start = pl.multiple_of(j * ATT_K, ATT_K)
        q_h = q_ref[:, hd * 2 * LANES:(hd + 1) * 2 * LANES]
        kb = k_ref[pl.ds(start, ATT_K), hd * 2 * LANES:(hd + 1) * 2 * LANES]
        return lax.dot_general(q_h, kb, nt, preferred_element_type=jnp.float32)

    for hd in range(MLA_HEADS):
        cols = slice(hd * 2 * LANES, (hd + 1) * 2 * LANES)
        k_lo = k_ref[pl.ds(diag_start, half), cols]
        k_hi = k_ref[pl.ds(diag_start + half, half), cols]
        qk = lambda q, k: lax.dot_general(q, k, nt, preferred_element_type=jnp.float32)
        s_tl = jnp.where(quad_allowed, qk(q_ref[:half, cols], k_lo), masked)
        s_bl = qk(q_ref[half:, cols], k_lo)
        s_br = jnp.where(quad_allowed, qk(q_ref[half:, cols], k_hi), masked)
        diag = s_ref.at[hd * n_kv + qi]
        diag[:half, :half] = s_tl
        diag[half:, :half] = s_bl
        diag[half:, half:] = s_br
        mx_ref[hd, :half] = lane_fold(s_tl, jnp.maximum)
        mx_ref[hd, half:] = jnp.maximum(lane_fold(s_bl, jnp.maximum), lane_fold(s_br, jnp.maximum))

    def pass_a(j, c):
        for hd in range(MLA_HEADS):
            s = scores(hd, j)
            s_ref[hd * n_kv + j] = s
            mx_ref[hd] = jnp.maximum(mx_ref[hd], lane_fold(s, jnp.maximum))
        return c

    lax.fori_loop(0, qi, pass_a, 0)

    for hd in range(MLA_HEADS):
        mx_ref[hd] = jnp.broadcast_to(jnp.max(mx_ref[hd], axis=-1, keepdims=True), (tq, LANES))

    def probs(hd, j):
        s = s_ref[hd * n_kv + j]
        mb = mx_ref[hd]
        p = jnp.exp2(jnp.concatenate([t - mb for t in lane_tiles(s)], axis=-1))
        start = pl.multiple_of(j * ATT_K, ATT_K)
        vb = v_ref[pl.ds(start, ATT_K), hd * LANES:(hd + 1) * LANES]
        pv = jnp.dot(p.astype(jnp.bfloat16), vb, preferred_element_type=jnp.float32)
        return lane_fold(p, jnp.add), pv

    for hd in range(MLA_HEADS):
        diag = s_ref.at[hd * n_kv + qi]
        vcols = slice(hd * LANES, (hd + 1) * LANES)
        p_top = jnp.exp2(jnp.concatenate([t - mx_ref[hd, :half] for t in lane_tiles(diag[:half, :half])], axis=-1))
        p_bot = jnp.exp2(jnp.concatenate([t - mx_ref[hd, half:] for t in lane_tiles(diag[half:, :])], axis=-1))
        ls_ref[hd, :half] = lane_fold(p_top, jnp.add)
        ls_ref[hd, half:] = lane_fold(p_bot, jnp.add)
        acc_ref[hd, :half] = jnp.dot(p_top.astype(jnp.bfloat16), v_ref[pl.ds(diag_start, half), vcols],
                                     preferred_element_type=jnp.float32)
        acc_ref[hd, half:] = jnp.dot(p_bot.astype(jnp.bfloat16), v_ref[pl.ds(diag_start, ATT_K), vcols],
                                     preferred_element_type=jnp.float32)

    def pass_b(j, c):
        for hd in range(MLA_HEADS):
            ls, pv = probs(hd, j)
            ls_ref[hd] = ls_ref[hd] + ls
            acc_ref[hd] = acc_ref[hd] + pv
        return c

    lax.fori_loop(0, qi, pass_b, 0)

    for hd in range(MLA_HEADS):
        l = jnp.sum(ls_ref[hd], axis=-1, keepdims=True)
        o_ref[:, hd * V_HEAD_DIM:(hd + 1) * V_HEAD_DIM] = (acc_ref[hd] / l).astype(jnp.bfloat16)


def _attn(q, k, v):
    tq = ATT_Q
    tokblk = lambda cols: pl.BlockSpec((tq, cols), lambda b, i: (b * N_QBLK + i, 0))
    seqblk = lambda cols: pl.BlockSpec((SEQ, cols), lambda b, i: (b, 0))
    return pl.pallas_call(
        _attn_kernel,
        grid=(BATCH, N_QBLK),
        in_specs=[tokblk(D_MODEL), seqblk(D_MODEL), seqblk(GM_WIDTH)],
        out_specs=tokblk(GM_WIDTH),
        out_shape=jax.ShapeDtypeStruct((N_TOK, MLA_HEADS * V_HEAD_DIM), jnp.bfloat16),
        scratch_shapes=[pltpu.VMEM((MLA_HEADS * (SEQ // ATT_K), tq, ATT_K), jnp.float32),
                        pltpu.VMEM((MLA_HEADS, tq, LANES), jnp.float32),
                        pltpu.VMEM((MLA_HEADS, tq, LANES), jnp.float32),
                        pltpu.VMEM((MLA_HEADS, tq, V_HEAD_DIM), jnp.float32)],
        compiler_params=pltpu.CompilerParams(
            dimension_semantics=("arbitrary", "arbitrary"), vmem_limit_bytes=VMEM_LIMIT),
        name="attn",
    )(q, k, v)


def _proj_kernel(outa_ref, ob_ref, h_ref, wout_ref, ln1g_ref, ln1b_ref, wr_ref, br_ref,
                 h1_ref, h1b_ref, ri_ref, rf_ref, proj_ref):
    i = pl.program_id(0)
    tb = outa_ref.shape[0]

    @pl.when(i == 0)
    def _():
        proj_ref[...] = jnp.zeros_like(proj_ref)

    h1 = _layer_norm(ALPHA * h_ref[...] + proj_ref[(i + 1) % 2], ln1g_ref[...], ln1b_ref[...])
    h1_ref[...] = h1
    h1b_ref[...] = h1.astype(jnp.bfloat16)

    logits_tm = jnp.dot(h1b_ref[...], wr_ref[...], preferred_element_type=jnp.float32)
    logits = logits_tm.T[0:ROUTER_ROWS] + br_ref[...]

    sub_i = lax.broadcasted_iota(jnp.int32, (SUBLANES, tb), 0)
    sub = sub_i.astype(jnp.float32)
    neg = jnp.float32(-jnp.inf)
    g = jnp.where(sub_i < N_GROUPS, logits[0:SUBLANES], neg)
    gmax = jnp.max(g, axis=0, keepdims=True)
    g_top = jnp.min(jnp.where(g == gmax, sub, float(SUBLANES)), axis=0, keepdims=True)
    p_group = 1.0 / jnp.sum(jnp.exp(g - gmax), axis=0, keepdims=True)
    sel = logits[SUBLANES:2 * SUBLANES]
    for grp in range(1, N_GROUPS):
        sel = jnp.where(g_top == float(grp), logits[(grp + 1) * SUBLANES:(grp + 2) * SUBLANES], sel)
    v1 = jnp.max(sel, axis=0, keepdims=True)
    i1 = jnp.min(jnp.where(sel == v1, sub, float(SUBLANES)), axis=0, keepdims=True)
    sel2 = jnp.where(sub == i1, neg, sel)
    v2 = jnp.max(sel2, axis=0, keepdims=True)
    i2 = jnp.min(jnp.where(sel2 == v2, sub, float(SUBLANES)), axis=0, keepdims=True)
    e21 = jnp.exp(v2 - v1)
    w1 = 1.0 / (1.0 + e21)
    gate1 = p_group * w1
    gate2 = p_group * (e21 * w1)
    e1 = g_top * EXPERTS_PER_GROUP + i1
    e2 = g_top * EXPERTS_PER_GROUP + i2
    ri_ref[...] = jnp.where(sub_i == 0, e1, jnp.where(sub_i == 1, e2, 0.0)).astype(jnp.int32)
    rf_ref[...] = jnp.where(sub_i == 0, gate1, jnp.where(sub_i == 1, gate2, 0.0))

    proj_ref[i % 2] = (jnp.dot(outa_ref[...], wout_ref[:GM_WIDTH, :], preferred_element_type=jnp.float32)
                       + jnp.dot(ob_ref[...], wout_ref[GM_WIDTH:, :], preferred_element_type=jnp.float32))


def _proj(outa, ob, h, wout, ln1g, ln1b, wr, br):
    tb = PROJ_TOKENS
    n_blk = N_TOK // tb
    cur = lambda i: jnp.minimum(i, n_blk - 1)
    prev = lambda i: jnp.maximum(i - 1, 0)
    full = lambda shape: pl.BlockSpec(shape, lambda i: (0,) * len(shape))
    return pl.pallas_call(
        _proj_kernel,
        grid=(n_blk + 1,),
        in_specs=[pl.BlockSpec((tb, GM_WIDTH), lambda i: (cur(i), 0)),
                  pl.BlockSpec((tb, GM_WIDTH), lambda i: (cur(i), 0)),
                  pl.BlockSpec((tb, D_MODEL), lambda i: (prev(i), 0)),
                  full((D_MODEL, D_MODEL)), full((1, D_MODEL)), full((1, D_MODEL)),
                  full((D_MODEL, LANES)), full((ROUTER_ROWS, 1))],
        out_specs=[pl.BlockSpec((tb, D_MODEL), lambda i: (prev(i), 0)),
                   pl.BlockSpec((tb, D_MODEL), lambda i: (prev(i), 0)),
                   pl.BlockSpec((SUBLANES, tb), lambda i: (0, prev(i))),
                   pl.BlockSpec((SUBLANES, tb), lambda i: (0, prev(i)))],
        out_shape=[jax.ShapeDtypeStruct((N_TOK, D_MODEL), jnp.float32),
                   jax.ShapeDtypeStruct((N_TOK, D_MODEL), jnp.bfloat16),
                   jax.ShapeDtypeStruct((SUBLANES, N_TOK), jnp.int32),
                   jax.ShapeDtypeStruct((SUBLANES, N_TOK), jnp.float32)],
        scratch_shapes=[pltpu.VMEM((2, tb, D_MODEL), jnp.float32)],
        compiler_params=pltpu.CompilerParams(
            dimension_semantics=("arbitrary",), vmem_limit_bytes=VMEM_LIMIT),
        name="proj",
    )(outa, ob, h, wout, ln1g, ln1b, wr, br)


def _plan_kernel(ri_all_ref, ri_ref, ldest_ref, runs_ref, meta_ref, run_ref, start_ref, upper_ref):
    step = pl.program_id(0)
    tb = ri_ref.shape[1]
    f32 = jnp.float32
    er = lax.broadcasted_iota(jnp.int32, (N_EXPERTS, LANES), 0)
    ec = lax.broadcasted_iota(jnp.int32, (N_EXPERTS, LANES), 1)
    to_row = lambda col: jnp.sum(jnp.where(er == ec, col, 0.0), axis=0, keepdims=True)

    def expert_one_hot(ref):
        e_sub = lax.broadcasted_iota(jnp.int32, (N_EXPERTS, ref.shape[1]), 0)
        return e_sub == ref[0:1, :], e_sub == ref[1:2, :]

    @pl.when(step == 0)
    def _():
        oh1, oh2 = expert_one_hot(ri_all_ref)
        counts = jnp.sum(jnp.where(oh1 | oh2, 1.0, 0.0), axis=1, keepdims=True)
        padded = jnp.floor((counts + (EXPERT_ROWS - 1)) * (1.0 / EXPERT_ROWS)) * EXPERT_ROWS
        pad_end = jnp.sum(jnp.where(ec <= er, to_row(padded), 0.0), axis=1, keepdims=True)
        start_ref[...] = jnp.broadcast_to(pad_end - padded, start_ref.shape)
        run_ref[...] = jnp.zeros_like(run_ref)
        bstart = (lax.broadcasted_iota(jnp.int32, (N_EXPERTS, META_LANES), 1) * EXPERT_ROWS).astype(f32)
        blk_e = jnp.sum(jnp.where(pad_end <= bstart, 1.0, 0.0), axis=0, keepdims=True)
        blk_e = jnp.minimum(blk_e, N_EXPERTS - 1.0)
        n_used = pad_end[N_EXPERTS - 1:N_EXPERTS, :] * (1.0 / EXPERT_ROWS)
        pad3 = lambda r: jnp.concatenate(
            [r, jnp.zeros((1, META_LANES - LANES), f32)], axis=1)
        msub = lax.broadcasted_iota(jnp.int32, (SUBLANES, META_LANES), 0)
        meta = jnp.where(msub == 0, blk_e,
                         jnp.where(msub == 1, pad3(to_row(pad_end)),
                                   jnp.where(msub == 2, pad3(to_row(counts)),
                                             jnp.where(msub == 3, n_used, 0.0))))
        meta_ref[...] = meta.astype(jnp.int32)
        tr = lax.broadcasted_iota(jnp.int32, (tb, tb), 0)
        tc = lax.broadcasted_iota(jnp.int32, (tb, tb), 1)
        upper_ref[...] = jnp.where(tr < tc, 1.0, 0.0).astype(jnp.bfloat16)

    @pl.when(step > 0)
    def _():
        oh1, oh2 = expert_one_hot(ri_ref)
        oh = jnp.where(oh1 | oh2, 1.0, 0.0).astype(f32)
        blk_count = jnp.sum(oh, axis=1, keepdims=True)
        prefix = jnp.dot(oh.astype(jnp.bfloat16), upper_ref[...], preferred_element_type=f32)
        cnt_row = to_row(blk_count)
        lstart = jnp.sum(jnp.where(ec < er, cnt_row, 0.0), axis=1, keepdims=True)
        base = prefix + lstart
        d1 = jnp.sum(jnp.where(oh1, base, 0.0), axis=0, keepdims=True)
        d2 = jnp.sum(jnp.where(oh2, base, 0.0), axis=0, keepdims=True)
        sub = lax.broadcasted_iota(jnp.int32, (SUBLANES, tb), 0)
        ldest_ref[...] = jnp.where(sub == 0, d1, jnp.where(sub == 1, d2, 0.0)).astype(jnp.int32)
        gstart = start_ref[:, 0:1] + run_ref[:, 0:1]
        rsub = lax.broadcasted_iota(jnp.int32, (SUBLANES, LANES), 0)
        runs = jnp.where(rsub == 0, cnt_row,
                         jnp.where(rsub == 1, to_row(lstart), jnp.where(rsub == 2, to_row(gstart), 0.0)))
        runs_ref[...] = runs.astype(jnp.int32)
        run_ref[...] = run_ref[...] + blk_count


def _plan(ri):
    tb = MOE_TOKENS
    blk = lambda i: jnp.maximum(i - 1, 0)
    return pl.pallas_call(
        _plan_kernel,
        grid=(N_MOE_BLOCKS + 1,),
        in_specs=[pl.BlockSpec((SUBLANES, N_TOK), lambda i: (0, 0)),
                  pl.BlockSpec((SUBLANES, tb), lambda i: (0, blk(i)))],
        out_specs=[pl.BlockSpec((SUBLANES, tb), lambda i: (0, blk(i))),
                   pl.BlockSpec((SUBLANES, LANES), lambda i: (blk(i), 0)),
                   pl.BlockSpec((SUBLANES, META_LANES), lambda i: (0, 0))],
        out_shape=[jax.ShapeDtypeStruct((SUBLANES, N_TOK), jnp.int32),
                   jax.ShapeDtypeStruct((N_MOE_BLOCKS * SUBLANES, LANES), jnp.int32),
                   jax.ShapeDtypeStruct((SUBLANES, META_LANES), jnp.int32)],
        scratch_shapes=[pltpu.VMEM((N_EXPERTS, LANES), jnp.float32),
                        pltpu.VMEM((N_EXPERTS, LANES), jnp.float32),
                        pltpu.VMEM((tb, tb), jnp.bfloat16)],
        compiler_params=pltpu.CompilerParams(
            dimension_semantics=("arbitrary",), vmem_limit_bytes=VMEM_LIMIT),
        name="plan",
    )(ri, ri)


def _for_each_run_piece(runs_ref, fn):
    for e in range(N_EXPERTS):
        n, lstart, gstart = runs_ref[0, e], runs_ref[1, e], runs_ref[2, e]
        for bit in range(RUN_BITS):
            @pl.when((n & (1 << bit)) != 0)
            def _(n=n, lstart=lstart, gstart=gstart, bit=bit):
                off = (n >> (bit + 1)) << (bit + 1)
                fn(lstart + off, gstart + off, 1 << bit)


def _tile_rows(ref, row, rows):
    return ref.at[pl.ds(pl.multiple_of(row * FEAT_TILES, FEAT_TILES), rows * FEAT_TILES)]


def _dispatch_kernel(meta_ref, runs_ref, ldest_ref, h1b_ref, buf_ref, sorted_ref, zero_ref, sems, zsem):
    i = pl.program_id(0)
    n_steps = pl.num_programs(0)
    tb = ldest_ref.shape[1]
    slot = i % 2
    block_tiles = 2 * tb * FEAT_TILES

    def wait_slot(s):
        pltpu.make_async_copy(sorted_ref.at[s], buf_ref.at[pl.ds(0, block_tiles)], sems.at[s]).wait()

    @pl.when(i == 0)
    def _():
        zero_ref[...] = jnp.zeros_like(zero_ref)

        def zero_copy(e):
            start = pl.multiple_of((meta_ref[1, e] - EXPERT_ROWS) * FEAT_TILES, EXPERT_ROWS * FEAT_TILES)
            return pltpu.make_async_copy(
                zero_ref, buf_ref.at[pl.ds(start, EXPERT_ROWS * FEAT_TILES)], zsem)

        def start_zero(e, c):
            @pl.when(meta_ref[2, e] > 0)
            def _():
                zero_copy(e).start()
            return c

        def wait_zero(e, c):
            @pl.when(meta_ref[2, e] > 0)
            def _():
                zero_copy(e).wait()
            return c

        def tail_copy(b):
            start = pl.multiple_of(b * (EXPERT_ROWS * FEAT_TILES), EXPERT_ROWS * FEAT_TILES)
            return pltpu.make_async_copy(
                zero_ref, buf_ref.at[pl.ds(start, EXPERT_ROWS * FEAT_TILES)], zsem)

        def start_tail(b, c):
            tail_copy(b).start()
            return c

        def wait_tail(b, c):
            tail_copy(b).wait()
            return c

        lax.fori_loop(0, N_EXPERTS, start_zero, 0)
        lax.fori_loop(meta_ref[3, 0], N_ROW_BLOCKS, start_tail, 0)
        lax.fori_loop(0, N_EXPERTS, wait_zero, 0)
        lax.fori_loop(meta_ref[3, 0], N_ROW_BLOCKS, wait_tail, 0)

    @pl.when(i >= 2)
    def _():
        wait_slot(slot)

    x = h1b_ref[...]
    ld0 = ldest_ref[0:1, :]
    ld1 = ldest_ref[1:2, :]
    for c in range(2 * tb // SORT_CHUNK):
        r = lax.broadcasted_iota(jnp.int32, (SORT_CHUNK, tb), 0) + c * SORT_CHUNK
        perm = jnp.where((r == ld0) | (r == ld1), 1.0, 0.0).astype(jnp.bfloat16)
        rows = jnp.dot(perm, x, preferred_element_type=jnp.float32)
        _to_row_tiles(sorted_ref.at[slot, pl.ds(c * SORT_CHUNK * FEAT_TILES, SORT_CHUNK * FEAT_TILES)], rows)

    def send(lrow, grow, rows):
        pltpu.make_async_copy(_tile_rows(sorted_ref.at[slot], lrow, rows),
                              _tile_rows(buf_ref, grow, rows), sems.at[slot]).start()

    _for_each_run_piece(runs_ref, send)

    @pl.when(i == n_steps - 1)
    def _():
        wait_slot(slot)
        wait_slot(1 - slot)


def _dispatch(meta, runs, ldest, h1b):
    tb = MOE_TOKENS
    return pl.pallas_call(
        _dispatch_kernel,
        grid_spec=pltpu.PrefetchScalarGridSpec(
            num_scalar_prefetch=1,
            grid=(N_MOE_BLOCKS,),
            in_specs=[pl.BlockSpec((SUBLANES, LANES), lambda i, m: (i, 0), memory_space=pltpu.SMEM),
                      pl.BlockSpec((SUBLANES, tb), lambda i, m: (0, i)),
                      pl.BlockSpec((tb, D_MODEL), lambda i, m: (i, 0))],
            out_specs=pl.BlockSpec(memory_space=pl.ANY),
            scratch_shapes=[pltpu.VMEM((2, 2 * tb * FEAT_TILES, LANES), jnp.float32),
                            pltpu.VMEM((EXPERT_ROWS * FEAT_TILES, LANES), jnp.float32),
                            pltpu.SemaphoreType.DMA((2,)), pltpu.SemaphoreType.DMA]),
        out_shape=jax.ShapeDtypeStruct((N_ROWS * FEAT_TILES, LANES), jnp.float32),
        compiler_params=pltpu.CompilerParams(
            dimension_semantics=("arbitrary",), vmem_limit_bytes=VMEM_LIMIT),
        name="dispatch",
    )(meta, runs, ldest, h1b)


def _sub_block_expert(meta, i, sub):
    return meta[0, jnp.minimum(i + sub * (N_ROW_BLOCKS // EXPERT_SUB), meta[3, 0] - 1)]


def _experts_kernel(meta_ref, x_ref, wg0_ref, wu0_ref, wd0_ref, wg1_ref, wu1_ref, wd1_ref, o_ref,
                    wgb_ref, wub_ref, wdb_ref, cached_ref):
    n = EXPERT_ROWS
    i = pl.program_id(0)
    weights = ((wg0_ref, wu0_ref, wd0_ref), (wg1_ref, wu1_ref, wd1_ref))

    @pl.when(i == 0)
    def _():
        for sub in range(EXPERT_SUB):
            cached_ref[sub] = -1

    for sub in range(EXPERT_SUB):
        expert = _sub_block_expert(meta_ref, i, sub)

        @pl.when(cached_ref[sub] != expert)
        def _(sub=sub, expert=expert):
            wg_ref, wu_ref, wd_ref = weights[sub]
            wgb_ref[sub] = wg_ref[...].astype(jnp.bfloat16)
            wub_ref[sub] = wu_ref[...].astype(jnp.bfloat16)
            wdb_ref[sub] = wd_ref[...].astype(jnp.bfloat16)
            cached_ref[sub] = expert

    for sub in range(EXPERT_SUB):
        x = _from_row_tiles(x_ref.at[sub], n).astype(jnp.bfloat16)
        gate = jnp.dot(x, wgb_ref[sub], preferred_element_type=jnp.float32)
        up = jnp.dot(x, wub_ref[sub], preferred_element_type=jnp.float32)
        act = (gate * jax.nn.sigmoid(gate) * up).astype(jnp.bfloat16)
        _to_row_tiles(o_ref.at[sub], jnp.dot(act, wdb_ref[sub], preferred_element_type=jnp.float32))


def _experts(meta, buf, wg, wu, wd):
    sub_tiles = (N_ROW_BLOCKS // EXPERT_SUB) * EXPERT_ROWS * FEAT_TILES

    def weight_spec(shape, sub):
        return pl.BlockSpec((None,) + shape, lambda i, m: (_sub_block_expert(m, i, sub), 0, 0))

    rows_spec = pl.BlockSpec((EXPERT_SUB, EXPERT_ROWS * FEAT_TILES, LANES), lambda i, m: (0, i, 0))
    up_shape, down_shape = (D_MODEL, EXPERT_FF), (EXPERT_FF, D_MODEL)
    eout = pl.pallas_call(
        _experts_kernel,
        grid_spec=pltpu.PrefetchScalarGridSpec(
            num_scalar_prefetch=1,
            grid=(N_ROW_BLOCKS // EXPERT_SUB,),
            in_specs=[rows_spec,
                      weight_spec(up_shape, 0), weight_spec(up_shape, 0), weight_spec(down_shape, 0),
                      weight_spec(up_shape, 1), weight_spec(up_shape, 1), weight_spec(down_shape, 1)],
            out_specs=rows_spec,
            scratch_shapes=[pltpu.VMEM((EXPERT_SUB,) + up_shape, jnp.bfloat16),
                            pltpu.VMEM((EXPERT_SUB,) + up_shape, jnp.bfloat16),
                            pltpu.VMEM((EXPERT_SUB,) + down_shape, jnp.bfloat16),
                            pltpu.SMEM((EXPERT_SUB,), jnp.int32)]),
        out_shape=jax.ShapeDtypeStruct((EXPERT_SUB, sub_tiles, LANES), jnp.float32),
        compiler_params=pltpu.CompilerParams(
            dimension_semantics=("arbitrary",), vmem_limit_bytes=VMEM_LIMIT),
        name="experts",
    )(meta, buf.reshape(EXPERT_SUB, sub_tiles, LANES), wg, wu, wd, wg, wu, wd)
    return eout.reshape(N_ROWS * FEAT_TILES, LANES)


def _combine_kernel(runs_ref, runs_next_ref, ldest_ref, rf_ref, h1_ref, eout_ref, ln2g_ref, ln2b_ref,
                    o_ref, y_ref, sems):
    i = pl.program_id(0)
    n_steps = pl.num_programs(0)
    tb = ldest_ref.shape[1]
    slot = i % 2
    block_tiles = 2 * tb * FEAT_TILES

    def fetch(table_ref, s):
        def recv(lrow, grow, rows):
            pltpu.make_async_copy(_tile_rows(eout_ref, grow, rows),
                                  _tile_rows(y_ref.at[s], lrow, rows), sems.at[s]).start()
        _for_each_run_piece(table_ref, recv)

    @pl.when(i == 0)
    def _():
        fetch(runs_ref, slot)

    @pl.when(i + 1 < n_steps)
    def _():
        fetch(runs_next_ref, 1 - slot)

    pltpu.make_async_copy(eout_ref.at[pl.ds(0, block_tiles)], y_ref.at[slot], sems.at[slot]).wait()

    ld = ldest_ref[...].astype(jnp.float32).T
    gates = rf_ref[...].T
    y = None
    col = lax.broadcasted_iota(jnp.int32, (tb, SORT_CHUNK), 1).astype(jnp.float32)
    for c in range(2 * tb // SORT_CHUNK):
        ld_c = ld - float(c * SORT_CHUNK)
        g = jnp.where(col == ld_c[:, 0:1], gates[:, 0:1],
                      jnp.where(col == ld_c[:, 1:2], gates[:, 1:2], 0.0)).astype(jnp.bfloat16)
        rows = _from_row_tiles(
            y_ref.at[slot, pl.ds(c * SORT_CHUNK * FEAT_TILES, SORT_CHUNK * FEAT_TILES)], SORT_CHUNK)
        part = jnp.dot(g, rows.astype(jnp.bfloat16), preferred_element_type=jnp.float32)
        y = part if y is None else y + part
    o_ref[...] = _layer_norm(ALPHA * h1_ref[...] + y, ln2g_ref[...], ln2b_ref[...])


def _combine(runs, ldest, rf, h1, eout, ln2g, ln2b):
    tb = MOE_TOKENS
    last = N_MOE_BLOCKS - 1
    return pl.pallas_call(
        _combine_kernel,
        grid=(N_MOE_BLOCKS,),
        in_specs=[pl.BlockSpec((SUBLANES, LANES), lambda i: (i, 0), memory_space=pltpu.SMEM),
                  pl.BlockSpec((SUBLANES, LANES), lambda i: (jnp.minimum(i + 1, last), 0),
                               memory_space=pltpu.SMEM),
                  pl.BlockSpec((SUBLANES, tb), lambda i: (0, i)),
                  pl.BlockSpec((SUBLANES, tb), lambda i: (0, i)),
                  pl.BlockSpec((tb, D_MODEL), lambda i: (i, 0)),
                  pl.BlockSpec(memory_space=pl.ANY),
                  pl.BlockSpec((1, D_MODEL), lambda i: (0, 0)),
                  pl.BlockSpec((1, D_MODEL), lambda i: (0, 0))],
        out_specs=pl.BlockSpec((tb, D_MODEL), lambda i: (i, 0)),
        out_shape=jax.ShapeDtypeStruct((N_TOK, D_MODEL), jnp.float32),
        scratch_shapes=[pltpu.VMEM((2, 2 * tb * FEAT_TILES, LANES), jnp.float32),
                        pltpu.SemaphoreType.DMA((2,))],
        compiler_params=pltpu.CompilerParams(
            dimension_semantics=("arbitrary",), vmem_limit_bytes=VMEM_LIMIT),
        name="combine",
    )(runs, runs, ldest, rf, h1, eout, ln2g, ln2b)


def _swap_halves(w):
    half = w.shape[-1] // 2
    return jnp.concatenate([w[..., half:], w[..., :half]], axis=-1)


def kernel(x, positions, ln0_g, ln0_b, w_in, gm_ln_g, gm_ln_b, w_spatial, b_spatial, q_norm_g, w_uq, kv_norm_g, w_ukv, w_out, ln1_g, ln1_b, w_router_group, b_router_group, w_router_expert, b_router_expert, w_gate, w_up, w_down, ln2_g, ln2_b):
    bf16 = jnp.bfloat16
    row = lambda a: a.reshape(1, -1)

    w_in0 = w_in[0]
    kr_cols = w_in0[:, O_KR:O_KR + QK_ROPE_DIM]
    win = jnp.concatenate([w_in0, _swap_halves(kr_cols)], axis=1).astype(bf16)
    wuq3 = w_uq[0].reshape(Q_LORA_RANK, MLA_HEADS, QK_NOPE_DIM + QK_ROPE_DIM)
    rope_cols = wuq3[:, :, QK_NOPE_DIM:]
    wuq = jnp.concatenate([wuq3, _swap_halves(rope_cols)], axis=-1).reshape(Q_LORA_RANK, D_MODEL).astype(bf16)
    wukv = w_ukv[0].astype(bf16)
    wout = w_out[0].astype(bf16)
    bs = jnp.broadcast_to(b_spatial[0][:, :, None], (GM_HEADS, GM_CHUNK, GM_HEAD_DIM))
    wr = jnp.concatenate([w_router_group[0], jnp.zeros((D_MODEL, SUBLANES - N_GROUPS), jnp.float32),
                          w_router_expert[0],
                          jnp.zeros((D_MODEL, LANES - ROUTER_ROWS), jnp.float32)],
                         axis=1).astype(bf16)
    br = jnp.concatenate([b_router_group[0], jnp.zeros((SUBLANES - N_GROUPS,), jnp.float32),
                          b_router_expert[0]]).reshape(ROUTER_ROWS, 1)

    inv_freq = ROPE_THETA ** (-jnp.arange(0, QK_ROPE_DIM, 2, dtype=jnp.float32) / QK_ROPE_DIM)
    freq = jnp.tile(inv_freq, 4).reshape(1, LANES)
    quarter = QK_ROPE_DIM // 2
    phase = jnp.concatenate([jnp.zeros((2 * quarter,), jnp.float32),
                             jnp.full((2 * quarter,), math.pi / 2, jnp.float32)]).reshape(1, LANES)
    sign = jnp.concatenate([jnp.ones((2 * quarter,), jnp.float32), -jnp.ones((quarter,), jnp.float32),
                            jnp.ones((quarter,), jnp.float32)]).reshape(1, LANES)

    x2 = x.reshape(N_TOK, D_MODEL)
    pos2 = positions.reshape(N_TOK, 1)

    h, outa, q, k, v = _prep(x2, pos2, row(ln0_g), row(ln0_b), win, row(gm_ln_g[0]), row(gm_ln_b[0]),
                             w_spatial[0], bs, row(q_norm_g[0]), wuq, row(kv_norm_g[0]), wukv,
                             freq, phase, sign)
    ob = _attn(q, k, v)
    h1, h1b, ri, rf = _proj(outa, ob, h, wout, row(ln1_g[0]), row(ln1_b[0]), wr, br)
    ldest, runs, meta = _plan(ri)
    buf = _dispatch(meta, runs, ldest, h1b)
    eout = _experts(meta, buf, w_gate[0], w_up[0], w_down[0])
    out = _combine(runs, ldest, rf, h1, eout, row(ln2_g[0]), row(ln2_b[0]))
    return out.reshape(BATCH, SEQ, D_MODEL)
```

```python
import functools
import math

import jax
import jax.numpy as jnp
from jax import lax
from jax.experimental import pallas as pl
from jax.experimental.pallas import tpu as pltpu

D_MODEL = 1024
BATCH = 16
SEQ = 2048
N_TOK = BATCH * SEQ
CHUNK = 64
GM_WIDTH = 512
GM_HEADS = 4
GM_HEAD_DIM = 128
GM_CHUNK = 128
MLA_HEADS = 4
QK_NOPE_DIM = 128
QK_ROPE_DIM = 64
V_HEAD_DIM = 128
Q_LORA_RANK = 384
KV_LORA_RANK = 256
ROPE_THETA = 10000.0
N_GROUPS = 4
EXPERTS_PER_GROUP = 8
N_EXPERTS = 32
TOP_K = 2
EXPERT_FF = 256
ALPHA = 2.0 ** 0.25
QK_SCALE = (QK_NOPE_DIM + QK_ROPE_DIM) ** -0.5 * math.log2(math.e)

LANES = 128
SUBLANES = 8
FEAT_TILES = D_MODEL // LANES
PREP_TOKENS = 512
ATT_Q = 512
ATT_K = 512
N_QBLK = SEQ // ATT_Q
PROJ_TOKENS = 512
MOE_TOKENS = 512
N_MOE_BLOCKS = N_TOK // MOE_TOKENS
RUN_BITS = (TOP_K * MOE_TOKENS).bit_length()
SORT_CHUNK = 256
EXPERT_ROWS = 256
EXPERT_SUB = 2
N_ROWS = N_TOK * TOP_K + N_EXPERTS * EXPERT_ROWS
N_ROW_BLOCKS = N_ROWS // EXPERT_ROWS
META_LANES = 384
IN_COLS = 2 * GM_WIDTH + Q_LORA_RANK + KV_LORA_RANK + 2 * QK_ROPE_DIM
O_Q = 2 * GM_WIDTH
O_KV = O_Q + Q_LORA_RANK
O_KR = O_KV + KV_LORA_RANK
ROUTER_ROWS = 40
VMEM_LIMIT = 48 * 1024 * 1024

assert N_ROW_BLOCKS <= META_LANES and N_ROW_BLOCKS % EXPERT_SUB == 0


def _layer_norm(x, g, b, eps=1e-5):
    mu = jnp.mean(x, axis=-1, keepdims=True)
    xc = x - mu
    var = jnp.mean(xc * xc, axis=-1, keepdims=True)
    return xc * lax.rsqrt(var + eps) * g + b


def _rms_norm(x, g, eps=1e-6):
    return x * lax.rsqrt(jnp.mean(x * x, axis=-1, keepdims=True) + eps) * g


def _gelu_tanh(x):
    c = math.sqrt(2.0 / math.pi)
    return 0.5 * x * (1.0 + jnp.tanh(c * (x + 0.044715 * (x * x * x))))


def _to_row_tiles(ref, x):
    n = x.shape[0]
    for s in range(FEAT_TILES):
        ref[pl.ds(s, n, stride=FEAT_TILES), :] = x[:, s * LANES:(s + 1) * LANES]


def _from_row_tiles(ref, n):
    return jnp.concatenate(
        [ref[pl.ds(s, n, stride=FEAT_TILES), :] for s in range(FEAT_TILES)], axis=-1)


def _prep_kernel(x_ref, pos_ref, ln0g_ref, ln0b_ref, win_ref, gmg_ref, gmb_ref, ws_ref, bs_ref,
                 qg_ref, wuq_ref, kvg_ref, wukv_ref, freq_ref, phase_ref, sign_ref,
                 h_ref, outa_ref, q_ref, k_ref, v_ref, tabc_ref, tabs_ref, rot_ref):
    tb = x_ref.shape[0]

    @pl.when(pl.program_id(0) == 0)
    def _():
        d = lax.broadcasted_iota(jnp.int32, (tb, LANES), 0).astype(jnp.float32) * freq_ref[...]
        tabc_ref[...] = jnp.cos(d)
        tabs_ref[...] = jnp.sin(d)

    pos_row = pos_ref[0]
    p0 = pos_row[:, 0:1]
    offset = lax.broadcasted_iota(jnp.int32, (1, tb), 1)
    consecutive = jnp.max(jnp.abs((pos_row - p0 - offset).astype(jnp.float32))) == 0.0

    @pl.when(consecutive)
    def _():
        a0 = p0.astype(jnp.float32) * freq_ref[...]
        c0, s0 = jnp.cos(a0), jnp.sin(a0)
        lane = lax.broadcasted_iota(jnp.int32, (1, LANES), 1)
        coef_c = jnp.where(lane < 2 * 32, c0, jnp.where(lane < 3 * 32, -s0, s0))
        coef_s = jnp.where(lane < 2 * 32, -s0, jnp.where(lane < 3 * 32, -c0, c0))
        rot_ref[...] = coef_c * tabc_ref[...] + coef_s * tabs_ref[...]

    @pl.when(jnp.logical_not(consecutive))
    def _():
        pos_col = jnp.broadcast_to(pos_row.astype(jnp.float32), (SUBLANES, tb)).T[:, 0:1]
        ang = pos_col * freq_ref[...]
        rot_ref[...] = jnp.cos(ang - phase_ref[...]) * sign_ref[...]

    rot = rot_ref[...]

    h = _layer_norm(x_ref[...], ln0g_ref[...], ln0b_ref[...])
    h_ref[...] = h
    z = jnp.dot(h.astype(jnp.bfloat16), win_ref[...], preferred_element_type=jnp.float32)

    u = _gelu_tanh(z[:, :GM_WIDTH])
    v = _gelu_tanh(z[:, GM_WIDTH:2 * GM_WIDTH])
    row_chunk = lax.broadcasted_iota(jnp.int32, (GM_CHUNK, GM_CHUNK), 0) // CHUNK
    col_chunk = lax.broadcasted_iota(jnp.int32, (GM_CHUNK, GM_CHUNK), 1) // CHUNK
    allowed = col_chunk <= row_chunk
    for hd in range(GM_HEADS):
        lo, hi = hd * GM_HEAD_DIM, (hd + 1) * GM_HEAD_DIM
        vln = _layer_norm(v[:, lo:hi], gmg_ref[:, lo:hi], gmb_ref[:, lo:hi]).astype(jnp.bfloat16)
        wm = jnp.where(allowed, ws_ref[hd], 0.0).astype(jnp.bfloat16)
        for c in range(tb // GM_CHUNK):
            r0, r1 = c * GM_CHUNK, (c + 1) * GM_CHUNK
            f = jnp.dot(wm, vln[r0:r1], preferred_element_type=jnp.float32) + bs_ref[hd]
            outa_ref[r0:r1, lo:hi] = (u[r0:r1, lo:hi] * f).astype(jnp.bfloat16)

    ql = _rms_norm(z[:, O_Q:O_KV], qg_ref[...]).astype(jnp.bfloat16)
    qf = jnp.dot(ql, wuq_ref[...], preferred_element_type=jnp.float32)
    rot_s = rot * QK_SCALE
    q_parts = []
    for hd in range(MLA_HEADS):
        base = hd * 2 * LANES
        q_parts.append(qf[:, base:base + LANES] * QK_SCALE)
        q_parts.append(qf[:, base + LANES:base + 2 * LANES] * rot_s)
    q_ref[...] = jnp.concatenate(q_parts, axis=-1).astype(jnp.bfloat16)

    kvl = _rms_norm(z[:, O_KV:O_KR], kvg_ref[...]).astype(jnp.bfloat16)
    kv = jnp.dot(kvl, wukv_ref[...], preferred_element_type=jnp.float32)
    t = z[:, O_KR:O_KR + LANES] * rot
    krr = t + pltpu.roll(t, 2 * QK_ROPE_DIM // 2, axis=1)
    k_parts, v_parts = [], []
    for hd in range(MLA_HEADS):
        base = hd * 2 * LANES
        k_parts.append(kv[:, base:base + LANES])
        k_parts.append(krr)
        v_parts.append(kv[:, base + LANES:base + 2 * LANES])
    k_ref[...] = jnp.concatenate(k_parts, axis=-1).astype(jnp.bfloat16)
    v_ref[...] = jnp.concatenate(v_parts, axis=-1).astype(jnp.bfloat16)


def _prep(x2, pos2, ln0g, ln0b, win, gmg, gmb, ws, bs, qg, wuq, kvg, wukv, freq, phase, sign):
    tb = PREP_TOKENS
    full = lambda shape: pl.BlockSpec(shape, lambda i: (0,) * len(shape))
    tok = lambda cols: pl.BlockSpec((tb, cols), lambda i: (i, 0))
    return pl.pallas_call(
        _prep_kernel,
        grid=(N_TOK // tb,),
        in_specs=[tok(D_MODEL), pl.BlockSpec((1, 1, tb), lambda i: (i, 0, 0)), full((1, D_MODEL)), full((1, D_MODEL)),
                  full((D_MODEL, IN_COLS)), full((1, GM_WIDTH)), full((1, GM_WIDTH)),
                  full((GM_HEADS, GM_CHUNK, GM_CHUNK)), full((GM_HEADS, GM_CHUNK, GM_HEAD_DIM)),
                  full((1, Q_LORA_RANK)), full((Q_LORA_RANK, D_MODEL)),
                  full((1, KV_LORA_RANK)), full((KV_LORA_RANK, D_MODEL)),
                  full((1, LANES)), full((1, LANES)), full((1, LANES))],
        out_specs=[tok(D_MODEL), tok(GM_WIDTH), tok(D_MODEL), tok(D_MODEL), tok(GM_WIDTH)],
        out_shape=[jax.ShapeDtypeStruct((N_TOK, D_MODEL), jnp.float32),
                   jax.ShapeDtypeStruct((N_TOK, GM_WIDTH), jnp.bfloat16),
                   jax.ShapeDtypeStruct((N_TOK, D_MODEL), jnp.bfloat16),
                   jax.ShapeDtypeStruct((N_TOK, D_MODEL), jnp.bfloat16),
                   jax.ShapeDtypeStruct((N_TOK, GM_WIDTH), jnp.bfloat16)],
        scratch_shapes=[pltpu.VMEM((tb, LANES), jnp.float32)] * 3,
        compiler_params=pltpu.CompilerParams(
            dimension_semantics=("arbitrary",), vmem_limit_bytes=VMEM_LIMIT),
        name="prep",
    )(x2, pos2, ln0g, ln0b, win, gmg, gmb, ws, bs, qg, wuq, kvg, wukv, freq, phase, sign)


def _attn_kernel(q_ref, k_ref, v_ref, o_ref, s_ref, mx_ref, ls_ref, acc_ref):
    qi = pl.program_id(1)
    tq = q_ref.shape[0]
    nt = (((1,), (1,)), ((), ()))
    n_kv = SEQ // ATT_K

    half = tq // 2
    row_chunk = (lax.broadcasted_iota(jnp.int32, (half, half), 0)) // CHUNK
    col_chunk = (lax.broadcasted_iota(jnp.int32, (half, half), 1)) // CHUNK
    quad_allowed = col_chunk <= row_chunk
    diag_start = pl.multiple_of(qi * ATT_K, ATT_K)
    masked = jnp.float32(-1e30)

    def lane_tiles(x):
        return [x[:, t * LANES:(t + 1) * LANES] for t in range(x.shape[1] // LANES)]

    def lane_fold(x, op):
        return functools.reduce(op, lane_tiles(x))

    def scores(hd, j):
        start = pl.multiple_of(j * ATT_K, ATT_K)
        q_h = q_ref[:, hd * 2 * LANES:(hd + 1) * 2 * LANES]
        kb = k_ref[pl.ds(start, ATT_K), hd * 2 * LANES:(hd + 1) * 2 * LANES]
        return lax.dot_general(q_h, kb, nt, preferred_element_type=jnp.float32)

    for hd in range(MLA_HEADS):
        cols = slice(hd * 2 * LANES, (hd + 1) * 2 * LANES)
        k_lo = k_ref[pl.ds(diag_start, half), cols]
        k_hi = k_ref[pl.ds(diag_start + half, half), cols]
        qk = lambda q, k: lax.dot_general(q, k, nt, preferred_element_type=jnp.float32)
        s_tl = jnp.where(quad_allowed, qk(q_ref[:half, cols], k_lo), masked)
        s_bl = qk(q_ref[half:, cols], k_lo)
        s_br = jnp.where(quad_allowed, qk(q_ref[half:, cols], k_hi), masked)
        diag = s_ref.at[hd * n_kv + qi]
        diag[:half, :half] = s_tl
        diag[half:, :half] = s_bl
        diag[half:, half:] = s_br
        mx_ref[hd, :half] = lane_fold(s_tl, jnp.maximum)
        mx_ref[hd, half:] = jnp.maximum(lane_fold(s_bl, jnp.maximum), lane_fold(s_br, jnp.maximum))

    def pass_a(j, c):
        for hd in range(MLA_HEADS):
            s = scores(hd, j)
            s_ref[hd * n_kv + j] = s
            mx_ref[hd] = jnp.maximum(mx_ref[hd], lane_fold(s, jnp.maximum))
        return c

    lax.fori_loop(0, qi, pass_a, 0)

    for hd in range(MLA_HEADS):
        mx_ref[hd] = jnp.broadcast_to(jnp.max(mx_ref[hd], axis=-1, keepdims=True), (tq, LANES))

    def probs(hd, j):
        s = s_ref[hd * n_kv + j]
        mb = mx_ref[hd]
        p = jnp.exp2(jnp.concatenate([t - mb for t in lane_tiles(s)], axis=-1))
        start = pl.multiple_of(j * ATT_K, ATT_K)
        vb = v_ref[pl.ds(start, ATT_K), hd * LANES:(hd + 1) * LANES]
        pv = jnp.dot(p.astype(jnp.bfloat16), vb, preferred_element_type=jnp.float32)
        return lane_fold(p, jnp.add), pv

    for hd in range(MLA_HEADS):
        diag = s_ref.at[hd * n_kv + qi]
        vcols = slice(hd * LANES, (hd + 1) * LANES)
        p_top = jnp.exp2(jnp.concatenate([t - mx_ref[hd, :half] for t in lane_tiles(diag[:half, :half])], axis=-1))
        p_bot = jnp.exp2(jnp.concatenate([t - mx_ref[hd, half:] for t in lane_tiles(diag[half:, :])], axis=-1))
        ls_ref[hd, :half] = lane_fold(p_top, jnp.add)
        ls_ref[hd, half:] = lane_fold(p_bot, jnp.add)
        acc_ref[hd, :half] = jnp.dot(p_top.astype(jnp.bfloat16), v_ref[pl.ds(diag_start, half), vcols],
                                     preferred_element_type=jnp.float32)
        acc_ref[hd, half:] = jnp.dot(p_bot.astype(jnp.bfloat16), v_ref[pl.ds(diag_start, ATT_K), vcols],
                                     preferred_element_type=jnp.float32)

    def pass_b(j, c):
        for hd in range(MLA_HEADS):
            ls, pv = probs(hd, j)
            ls_ref[hd] = ls_ref[hd] + ls
            acc_ref[hd] = acc_ref[hd] + pv
        return c

    lax.fori_loop(0, qi, pass_b, 0)

    for hd in range(MLA_HEADS):
        l = jnp.sum(ls_ref[hd], axis=-1, keepdims=True)
        o_ref[:, hd * V_HEAD_DIM:(hd + 1) * V_HEAD_DIM] = (acc_ref[hd] / l).astype(jnp.bfloat16)


def _attn(q, k, v):
    tq = ATT_Q
    tokblk = lambda cols: pl.BlockSpec((tq, cols), lambda b, i: (b * N_QBLK + i, 0))
    seqblk = lambda cols: pl.BlockSpec((SEQ, cols), lambda b, i: (b, 0))
    return pl.pallas_call(
        _attn_kernel,
        grid=(BATCH, N_QBLK),
        in_specs=[tokblk(D_MODEL), seqblk(D_MODEL), seqblk(GM_WIDTH)],
        out_specs=tokblk(GM_WIDTH),
        out_shape=jax.ShapeDtypeStruct((N_TOK, MLA_HEADS * V_HEAD_DIM), jnp.bfloat16),
        scratch_shapes=[pltpu.VMEM((MLA_HEADS * (SEQ // ATT_K), tq, ATT_K), jnp.float32),
                        pltpu.VMEM((MLA_HEADS, tq, LANES), jnp.float32),
                        pltpu.VMEM((MLA_HEADS, tq, LANES), jnp.float32),
                        pltpu.VMEM((MLA_HEADS, tq, V_HEAD_DIM), jnp.float32)],
        compiler_params=pltpu.CompilerParams(
            dimension_semantics=("arbitrary", "arbitrary"), vmem_limit_bytes=VMEM_LIMIT),
        name="attn",
    )(q, k, v)


def _proj_kernel(outa_ref, ob_ref, h_ref, wout_ref, ln1g_ref, ln1b_ref, wr_ref, br_ref,
                 h1_ref, h1b_ref, ri_ref, rf_ref, proj_ref):
    i = pl.program_id(0)
    tb = outa_ref.shape[0]

    @pl.when(i == 0)
    def _():
        proj_ref[...] = jnp.zeros_like(proj_ref)

    h1 = _layer_norm(ALPHA * h_ref[...] + proj_ref[(i + 1) % 2], ln1g_ref[...], ln1b_ref[...])
    h1_ref[...] = h1
    h1b_ref[...] = h1.astype(jnp.bfloat16)

    logits_tm = jnp.dot(h1b_ref[...], wr_ref[...], preferred_element_type=jnp.float32)
    logits = logits_tm.T[0:ROUTER_ROWS] + br_ref[...]

    sub_i = lax.broadcasted_iota(jnp.int32, (SUBLANES, tb), 0)
    sub = sub_i.astype(jnp.float32)
    neg = jnp.float32(-jnp.inf)
    g = jnp.where(sub_i < N_GROUPS, logits[0:SUBLANES], neg)
    gmax = jnp.max(g, axis=0, keepdims=True)
    g_top = jnp.min(jnp.where(g == gmax, sub, float(SUBLANES)), axis=0, keepdims=True)
    p_group = 1.0 / jnp.sum(jnp.exp(g - gmax), axis=0, keepdims=True)
    sel = logits[SUBLANES:2 * SUBLANES]
    for grp in range(1, N_GROUPS):
        sel = jnp.where(g_top == float(grp), logits[(grp + 1) * SUBLANES:(grp + 2) * SUBLANES], sel)
    v1 = jnp.max(sel, axis=0, keepdims=True)
    i1 = jnp.min(jnp.where(sel == v1, sub, float(SUBLANES)), axis=0, keepdims=True)
    sel2 = jnp.where(sub == i1, neg, sel)
    v2 = jnp.max(sel2, axis=0, keepdims=True)
    i2 = jnp.min(jnp.where(sel2 == v2, sub, float(SUBLANES)), axis=0, keepdims=True)
    e21 = jnp.exp(v2 - v1)
    w1 = 1.0 / (1.0 + e21)
    gate1 = p_group * w1
    gate2 = p_group * (e21 * w1)
    e1 = g_top * EXPERTS_PER_GROUP + i1
    e2 = g_top * EXPERTS_PER_GROUP + i2
    ri_ref[...] = jnp.where(sub_i == 0, e1, jnp.where(sub_i == 1, e2, 0.0)).astype(jnp.int32)
    rf_ref[...] = jnp.where(sub_i == 0, gate1, jnp.where(sub_i == 1, gate2, 0.0))

    proj_ref[i % 2] = (jnp.dot(outa_ref[...], wout_ref[:GM_WIDTH, :], preferred_element_type=jnp.float32)
                       + jnp.dot(ob_ref[...], wout_ref[GM_WIDTH:, :], preferred_element_type=jnp.float32))


def _proj(outa, ob, h, wout, ln1g, ln1b, wr, br):
    tb = PROJ_TOKENS
    n_blk = N_TOK // tb
    cur = lambda i: jnp.minimum(i, n_blk - 1)
    prev = lambda i: jnp.maximum(i - 1, 0)
    full = lambda shape: pl.BlockSpec(shape, lambda i: (0,) * len(shape))
    return pl.pallas_call(
        _proj_kernel,
        grid=(n_blk + 1,),
        in_specs=[pl.BlockSpec((tb, GM_WIDTH), lambda i: (cur(i), 0)),
                  pl.BlockSpec((tb, GM_WIDTH), lambda i: (cur(i), 0)),
                  pl.BlockSpec((tb, D_MODEL), lambda i: (prev(i), 0)),
                  full((D_MODEL, D_MODEL)), full((1, D_MODEL)), full((1, D_MODEL)),
                  full((D_MODEL, LANES)), full((ROUTER_ROWS, 1))],
        out_specs=[pl.BlockSpec((tb, D_MODEL), lambda i: (prev(i), 0)),
                   pl.BlockSpec((tb, D_MODEL), lambda i: (prev(i), 0)),
                   pl.BlockSpec((SUBLANES, tb), lambda i: (0, prev(i))),
                   pl.BlockSpec((SUBLANES, tb), lambda i: (0, prev(i)))],
        out_shape=[jax.ShapeDtypeStruct((N_TOK, D_MODEL), jnp.float32),
                   jax.ShapeDtypeStruct((N_TOK, D_MODEL), jnp.bfloat16),
                   jax.ShapeDtypeStruct((SUBLANES, N_TOK), jnp.int32),
                   jax.ShapeDtypeStruct((SUBLANES, N_TOK), jnp.float32)],
        scratch_shapes=[pltpu.VMEM((2, tb, D_MODEL), jnp.float32)],
        compiler_params=pltpu.CompilerParams(
            dimension_semantics=("arbitrary",), vmem_limit_bytes=VMEM_LIMIT),
        name="proj",
    )(outa, ob, h, wout, ln1g, ln1b, wr, br)


def _plan_kernel(ri_all_ref, ri_ref, ldest_ref, runs_ref, meta_ref, run_ref, start_ref, upper_ref):
    step = pl.program_id(0)
    tb = ri_ref.shape[1]
    f32 = jnp.float32
    er = lax.broadcasted_iota(jnp.int32, (N_EXPERTS, LANES), 0)
    ec = lax.broadcasted_iota(jnp.int32, (N_EXPERTS, LANES), 1)
    to_row = lambda col: jnp.sum(jnp.where(er == ec, col, 0.0), axis=0, keepdims=True)

    def expert_one_hot(ref):
        e_sub = lax.broadcasted_iota(jnp.int32, (N_EXPERTS, ref.shape[1]), 0)
        return e_sub == ref[0:1, :], e_sub == ref[1:2, :]

    @pl.when(step == 0)
    def _():
        oh1, oh2 = expert_one_hot(ri_all_ref)
        counts = jnp.sum(jnp.where(oh1 | oh2, 1.0, 0.0), axis=1, keepdims=True)
        padded = jnp.floor((counts + (EXPERT_ROWS - 1)) * (1.0 / EXPERT_ROWS)) * EXPERT_ROWS
        pad_end = jnp.sum(jnp.where(ec <= er, to_row(padded), 0.0), axis=1, keepdims=True)
        start_ref[...] = jnp.broadcast_to(pad_end - padded, start_ref.shape)
        run_ref[...] = jnp.zeros_like(run_ref)
        bstart = (lax.broadcasted_iota(jnp.int32, (N_EXPERTS, META_LANES), 1) * EXPERT_ROWS).astype(f32)
        blk_e = jnp.sum(jnp.where(pad_end <= bstart, 1.0, 0.0), axis=0, keepdims=True)
        blk_e = jnp.minimum(blk_e, N_EXPERTS - 1.0)
        n_used = pad_end[N_EXPERTS - 1:N_EXPERTS, :] * (1.0 / EXPERT_ROWS)
        pad3 = lambda r: jnp.concatenate(
            [r, jnp.zeros((1, META_LANES - LANES), f32)], axis=1)
        msub = lax.broadcasted_iota(jnp.int32, (SUBLANES, META_LANES), 0)
        meta = jnp.where(msub == 0, blk_e,
                         jnp.where(msub == 1, pad3(to_row(pad_end)),
                                   jnp.where(msub == 2, pad3(to_row(counts)),
                                             jnp.where(msub == 3, n_used, 0.0))))
        meta_ref[...] = meta.astype(jnp.int32)
        tr = lax.broadcasted_iota(jnp.int32, (tb, tb), 0)
        tc = lax.broadcasted_iota(jnp.int32, (tb, tb), 1)
        upper_ref[...] = jnp.where(tr < tc, 1.0, 0.0).astype(jnp.bfloat16)

    @pl.when(step > 0)
    def _():
        oh1, oh2 = expert_one_hot(ri_ref)
        oh = jnp.where(oh1 | oh2, 1.0, 0.0).astype(f32)
        blk_count = jnp.sum(oh, axis=1, keepdims=True)
        prefix = jnp.dot(oh.astype(jnp.bfloat16), upper_ref[...], preferred_element_type=f32)
        cnt_row = to_row(blk_count)
        lstart = jnp.sum(jnp.where(ec < er, cnt_row, 0.0), axis=1, keepdims=True)
        base = prefix + lstart
        d1 = jnp.sum(jnp.where(oh1, base, 0.0), axis=0, keepdims=True)
        d2 = jnp.sum(jnp.where(oh2, base, 0.0), axis=0, keepdims=True)
        sub = lax.broadcasted_iota(jnp.int32, (SUBLANES, tb), 0)
        ldest_ref[...] = jnp.where(sub == 0, d1, jnp.where(sub == 1, d2, 0.0)).astype(jnp.int32)
        gstart = start_ref[:, 0:1] + run_ref[:, 0:1]
        rsub = lax.broadcasted_iota(jnp.int32, (SUBLANES, LANES), 0)
        runs = jnp.where(rsub == 0, cnt_row,
                         jnp.where(rsub == 1, to_row(lstart), jnp.where(rsub == 2, to_row(gstart), 0.0)))
        runs_ref[...] = runs.astype(jnp.int32)
        run_ref[...] = run_ref[...] + blk_count


def _plan(ri):
    tb = MOE_TOKENS
    blk = lambda i: jnp.maximum(i - 1, 0)
    return pl.pallas_call(
        _plan_kernel,
        grid=(N_MOE_BLOCKS + 1,),
        in_specs=[pl.BlockSpec((SUBLANES, N_TOK), lambda i: (0, 0)),
                  pl.BlockSpec((SUBLANES, tb), lambda i: (0, blk(i)))],
        out_specs=[pl.BlockSpec((SUBLANES, tb), lambda i: (0, blk(i))),
                   pl.BlockSpec((SUBLANES, LANES), lambda i: (blk(i), 0)),
                   pl.BlockSpec((SUBLANES, META_LANES), lambda i: (0, 0))],
        out_shape=[jax.ShapeDtypeStruct((SUBLANES, N_TOK), jnp.int32),
                   jax.ShapeDtypeStruct((N_MOE_BLOCKS * SUBLANES, LANES), jnp.int32),
                   jax.ShapeDtypeStruct((SUBLANES, META_LANES), jnp.int32)],
        scratch_shapes=[pltpu.VMEM((N_EXPERTS, LANES), jnp.float32),
                        pltpu.VMEM((N_EXPERTS, LANES), jnp.float32),
                        pltpu.VMEM((tb, tb), jnp.bfloat16)],
        compiler_params=pltpu.CompilerParams(
            dimension_semantics=("arbitrary",), vmem_limit_bytes=VMEM_LIMIT),
        name="plan",
    )(ri, ri)


def _for_each_run_piece(runs_ref, fn):
    for e in range(N_EXPERTS):
        n, lstart, gstart = runs_ref[0, e], runs_ref[1, e], runs_ref[2, e]
        for bit in range(RUN_BITS):
            @pl.when((n & (1 << bit)) != 0)
            def _(n=n, lstart=lstart, gstart=gstart, bit=bit):
                off = (n >> (bit + 1)) << (bit + 1)
                fn(lstart + off, gstart + off, 1 << bit)


def _tile_rows(ref, row, rows):
    return ref.at[pl.ds(pl.multiple_of(row * FEAT_TILES, FEAT_TILES), rows * FEAT_TILES)]


def _dispatch_kernel(meta_ref, runs_ref, ldest_ref, h1b_ref, buf_ref, sorted_ref, zero_ref, sems, zsem):
    i = pl.program_id(0)
    n_steps = pl.num_programs(0)
    tb = ldest_ref.shape[1]
    slot = i % 2
    block_tiles = 2 * tb * FEAT_TILES

    def wait_slot(s):
        pltpu.make_async_copy(sorted_ref.at[s], buf_ref.at[pl.ds(0, block_tiles)], sems.at[s]).wait()

    @pl.when(i == 0)
    def _():
        zero_ref[...] = jnp.zeros_like(zero_ref)

        def zero_copy(e):
            start = pl.multiple_of((meta_ref[1, e] - EXPERT_ROWS) * FEAT_TILES, EXPERT_ROWS * FEAT_TILES)
            return pltpu.make_async_copy(
                zero_ref, buf_ref.at[pl.ds(start, EXPERT_ROWS * FEAT_TILES)], zsem)

        def start_zero(e, c):
            @pl.when(meta_ref[2, e] > 0)
            def _():
                zero_copy(e).start()
            return c

        def wait_zero(e, c):
            @pl.when(meta_ref[2, e] > 0)
            def _():
                zero_copy(e).wait()
            return c

        def tail_copy(b):
            start = pl.multiple_of(b * (EXPERT_ROWS * FEAT_TILES), EXPERT_ROWS * FEAT_TILES)
            return pltpu.make_async_copy(
                zero_ref, buf_ref.at[pl.ds(start, EXPERT_ROWS * FEAT_TILES)], zsem)

        def start_tail(b, c):
            tail_copy(b).start()
            return c

        def wait_tail(b, c):
            tail_copy(b).wait()
            return c

        lax.fori_loop(0, N_EXPERTS, start_zero, 0)
        lax.fori_loop(meta_ref[3, 0], N_ROW_BLOCKS, start_tail, 0)
        lax.fori_loop(0, N_EXPERTS, wait_zero, 0)
        lax.fori_loop(meta_ref[3, 0], N_ROW_BLOCKS, wait_tail, 0)

    @pl.when(i >= 2)
    def _():
        wait_slot(slot)

    x = h1b_ref[...]
    ld0 = ldest_ref[0:1, :]
    ld1 = ldest_ref[1:2, :]
    for c in range(2 * tb // SORT_CHUNK):
        r = lax.broadcasted_iota(jnp.int32, (SORT_CHUNK, tb), 0) + c * SORT_CHUNK
        perm = jnp.where((r == ld0) | (r == ld1), 1.0, 0.0).astype(jnp.bfloat16)
        rows = jnp.dot(perm, x, preferred_element_type=jnp.float32)
        _to_row_tiles(sorted_ref.at[slot, pl.ds(c * SORT_CHUNK * FEAT_TILES, SORT_CHUNK * FEAT_TILES)], rows)

    def send(lrow, grow, rows):
        pltpu.make_async_copy(_tile_rows(sorted_ref.at[slot], lrow, rows),
                              _tile_rows(buf_ref, grow, rows), sems.at[slot]).start()

    _for_each_run_piece(runs_ref, send)

    @pl.when(i == n_steps - 1)
    def _():
        wait_slot(slot)
        wait_slot(1 - slot)


def _dispatch(meta, runs, ldest, h1b):
    tb = MOE_TOKENS
    return pl.pallas_call(
        _dispatch_kernel,
        grid_spec=pltpu.PrefetchScalarGridSpec(
            num_scalar_prefetch=1,
            grid=(N_MOE_BLOCKS,),
            in_specs=[pl.BlockSpec((SUBLANES, LANES), lambda i, m: (i, 0), memory_space=pltpu.SMEM),
                      pl.BlockSpec((SUBLANES, tb), lambda i, m: (0, i)),
                      pl.BlockSpec((tb, D_MODEL), lambda i, m: (i, 0))],
            out_specs=pl.BlockSpec(memory_space=pl.ANY),
            scratch_shapes=[pltpu.VMEM((2, 2 * tb * FEAT_TILES, LANES), jnp.float32),
                            pltpu.VMEM((EXPERT_ROWS * FEAT_TILES, LANES), jnp.float32),
                            pltpu.SemaphoreType.DMA((2,)), pltpu.SemaphoreType.DMA]),
        out_shape=jax.ShapeDtypeStruct((N_ROWS * FEAT_TILES, LANES), jnp.float32),
        compiler_params=pltpu.CompilerParams(
            dimension_semantics=("arbitrary",), vmem_limit_bytes=VMEM_LIMIT),
        name="dispatch",
    )(meta, runs, ldest, h1b)


def _sub_block_expert(meta, i, sub):
    return meta[0, jnp.minimum(i + sub * (N_ROW_BLOCKS // EXPERT_SUB), meta[3, 0] - 1)]


def _experts_kernel(meta_ref, x_ref, wg0_ref, wu0_ref, wd0_ref, wg1_ref, wu1_ref, wd1_ref, o_ref,
                    wgb_ref, wub_ref, wdb_ref, cached_ref):
    n = EXPERT_ROWS
    i = pl.program_id(0)
    weights = ((wg0_ref, wu0_ref, wd0_ref), (wg1_ref, wu1_ref, wd1_ref))

    @pl.when(i == 0)
    def _():
        for sub in range(EXPERT_SUB):
            cached_ref[sub] = -1

    for sub in range(EXPERT_SUB):
        expert = _sub_block_expert(meta_ref, i, sub)

        @pl.when(cached_ref[sub] != expert)
        def _(sub=sub, expert=expert):
            wg_ref, wu_ref, wd_ref = weights[sub]
            wgb_ref[sub] = wg_ref[...].astype(jnp.bfloat16)
            wub_ref[sub] = wu_ref[...].astype(jnp.bfloat16)
            wdb_ref[sub] = wd_ref[...].astype(jnp.bfloat16)
            cached_ref[sub] = expert

    for sub in range(EXPERT_SUB):
        x = _from_row_tiles(x_ref.at[sub], n).astype(jnp.bfloat16)
        gate = jnp.dot(x, wgb_ref[sub], preferred_element_type=jnp.float32)
        up = jnp.dot(x, wub_ref[sub], preferred_element_type=jnp.float32)
        act = (gate * jax.nn.sigmoid(gate) * up).astype(jnp.bfloat16)
        _to_row_tiles(o_ref.at[sub], jnp.dot(act, wdb_ref[sub], preferred_element_type=jnp.float32))


def _experts(meta, buf, wg, wu, wd):
    sub_tiles = (N_ROW_BLOCKS // EXPERT_SUB) * EXPERT_ROWS * FEAT_TILES

    def weight_spec(shape, sub):
        return pl.BlockSpec((None,) + shape, lambda i, m: (_sub_block_expert(m, i, sub), 0, 0))

    rows_spec = pl.BlockSpec((EXPERT_SUB, EXPERT_ROWS * FEAT_TILES, LANES), lambda i, m: (0, i, 0))
    up_shape, down_shape = (D_MODEL, EXPERT_FF), (EXPERT_FF, D_MODEL)
    eout = pl.pallas_call(
        _experts_kernel,
        grid_spec=pltpu.PrefetchScalarGridSpec(
            num_scalar_prefetch=1,
            grid=(N_ROW_BLOCKS // EXPERT_SUB,),
            in_specs=[rows_spec,
                      weight_spec(up_shape, 0), weight_spec(up_shape, 0), weight_spec(down_shape, 0),
                      weight_spec(up_shape, 1), weight_spec(up_shape, 1), weight_spec(down_shape, 1)],
            out_specs=rows_spec,
            scratch_shapes=[pltpu.VMEM((EXPERT_SUB,) + up_shape, jnp.bfloat16),
                            pltpu.VMEM((EXPERT_SUB,) + up_shape, jnp.bfloat16),
                            pltpu.VMEM((EXPERT_SUB,) + down_shape, jnp.bfloat16),
                            pltpu.SMEM((EXPERT_SUB,), jnp.int32)]),
        out_shape=jax.ShapeDtypeStruct((EXPERT_SUB, sub_tiles, LANES), jnp.float32),
        compiler_params=pltpu.CompilerParams(
            dimension_semantics=("arbitrary",), vmem_limit_bytes=VMEM_LIMIT),
        name="experts",
    )(meta, buf.reshape(EXPERT_SUB, sub_tiles, LANES), wg, wu, wd, wg, wu, wd)
    return eout.reshape(N_ROWS * FEAT_TILES, LANES)


def _combine_kernel(runs_ref, runs_next_ref, ldest_ref, rf_ref, h1_ref, eout_ref, ln2g_ref, ln2b_ref,
                    o_ref, y_ref, sems):
    i = pl.program_id(0)
    n_steps = pl.num_programs(0)
    tb = ldest_ref.shape[1]
    slot = i % 2
    block_tiles = 2 * tb * FEAT_TILES

    def fetch(table_ref, s):
        def recv(lrow, grow, rows):
            pltpu.make_async_copy(_tile_rows(eout_ref, grow, rows),
                                  _tile_rows(y_ref.at[s], lrow, rows), sems.at[s]).start()
        _for_each_run_piece(table_ref, recv)

    @pl.when(i == 0)
    def _():
        fetch(runs_ref, slot)

    @pl.when(i + 1 < n_steps)
    def _():
        fetch(runs_next_ref, 1 - slot)

    pltpu.make_async_copy(eout_ref.at[pl.ds(0, block_tiles)], y_ref.at[slot], sems.at[slot]).wait()

    ld = ldest_ref[...].astype(jnp.float32).T
    gates = rf_ref[...].T
    y = None
    col = lax.broadcasted_iota(jnp.int32, (tb, SORT_CHUNK), 1).astype(jnp.float32)
    for c in range(2 * tb // SORT_CHUNK):
        ld_c = ld - float(c * SORT_CHUNK)
        g = jnp.where(col == ld_c[:, 0:1], gates[:, 0:1],
                      jnp.where(col == ld_c[:, 1:2], gates[:, 1:2], 0.0)).astype(jnp.bfloat16)
        rows = _from_row_tiles(
            y_ref.at[slot, pl.ds(c * SORT_CHUNK * FEAT_TILES, SORT_CHUNK * FEAT_TILES)], SORT_CHUNK)
        part = jnp.dot(g, rows.astype(jnp.bfloat16), preferred_element_type=jnp.float32)
        y = part if y is None else y + part
    o_ref[...] = _layer_norm(ALPHA * h1_ref[...] + y, ln2g_ref[...], ln2b_ref[...])


def _combine(runs, ldest, rf, h1, eout, ln2g, ln2b):
    tb = MOE_TOKENS
    last = N_MOE_BLOCKS - 1
    return pl.pallas_call(
        _combine_kernel,
        grid=(N_MOE_BLOCKS,),
        in_specs=[pl.BlockSpec((SUBLANES, LANES), lambda i: (i, 0), memory_space=pltpu.SMEM),
                  pl.BlockSpec((SUBLANES, LANES), lambda i: (jnp.minimum(i + 1, last), 0),
                               memory_space=pltpu.SMEM),
                  pl.BlockSpec((SUBLANES, tb), lambda i: (0, i)),
                  pl.BlockSpec((SUBLANES, tb), lambda i: (0, i)),
                  pl.BlockSpec((tb, D_MODEL), lambda i: (i, 0)),
                  pl.BlockSpec(memory_space=pl.ANY),
                  pl.BlockSpec((1, D_MODEL), lambda i: (0, 0)),
                  pl.BlockSpec((1, D_MODEL), lambda i: (0, 0))],
        out_specs=pl.BlockSpec((tb, D_MODEL), lambda i: (i, 0)),
        out_shape=jax.ShapeDtypeStruct((N_TOK, D_MODEL), jnp.float32),
        scratch_shapes=[pltpu.VMEM((2, 2 * tb * FEAT_TILES, LANES), jnp.float32),
                        pltpu.SemaphoreType.DMA((2,))],
        compiler_params=pltpu.CompilerParams(
            dimension_semantics=("arbitrary",), vmem_limit_bytes=VMEM_LIMIT),
        name="combine",
    )(runs, runs, ldest, rf, h1, eout, ln2g, ln2b)


def _swap_halves(w):
    half = w.shape[-1] // 2
    return jnp.concatenate([w[..., half:], w[..., :half]], axis=-1)


def kernel(x, positions, ln0_g, ln0_b, w_in, gm_ln_g, gm_ln_b, w_spatial, b_spatial, q_norm_g, w_uq, kv_norm_g, w_ukv, w_out, ln1_g, ln1_b, w_router_group, b_router_group, w_router_expert, b_router_expert, w_gate, w_up, w_down, ln2_g, ln2_b):
    bf16 = jnp.bfloat16
    row = lambda a: a.reshape(1, -1)

    w_in0 = w_in[0]
    kr_cols = w_in0[:, O_KR:O_KR + QK_ROPE_DIM]
    win = jnp.concatenate([w_in0, _swap_halves(kr_cols)], axis=1).astype(bf16)
    wuq3 = w_uq[0].reshape(Q_LORA_RANK, MLA_HEADS, QK_NOPE_DIM + QK_ROPE_DIM)
    rope_cols = wuq3[:, :, QK_NOPE_DIM:]
    wuq = jnp.concatenate([wuq3, _swap_halves(rope_cols)], axis=-1).reshape(Q_LORA_RANK, D_MODEL).astype(bf16)
    wukv = w_ukv[0].astype(bf16)
    wout = w_out[0].astype(bf16)
    bs = jnp.broadcast_to(b_spatial[0][:, :, None], (GM_HEADS, GM_CHUNK, GM_HEAD_DIM))
    wr = jnp.concatenate([w_router_group[0], jnp.zeros((D_MODEL, SUBLANES - N_GROUPS), jnp.float32),
                          w_router_expert[0],
                          jnp.zeros((D_MODEL, LANES - ROUTER_ROWS), jnp.float32)],
                         axis=1).astype(bf16)
    br = jnp.concatenate([b_router_group[0], jnp.zeros((SUBLANES - N_GROUPS,), jnp.float32),
                          b_router_expert[0]]).reshape(ROUTER_ROWS, 1)

    inv_freq = ROPE_THETA ** (-jnp.arange(0, QK_ROPE_DIM, 2, dtype=jnp.float32) / QK_ROPE_DIM)
    freq = jnp.tile(inv_freq, 4).reshape(1, LANES)
    quarter = QK_ROPE_DIM // 2
    phase = jnp.concatenate([jnp.zeros((2 * quarter,), jnp.float32),
                             jnp.full((2 * quarter,), math.pi / 2, jnp.float32)]).reshape(1, LANES)
    sign = jnp.concatenate([jnp.ones((2 * quarter,), jnp.float32), -jnp.ones((quarter,), jnp.float32),
                            jnp.ones((quarter,), jnp.float32)]).reshape(1, LANES)

    x2 = x.reshape(N_TOK, D_MODEL)
    pos2 = positions.reshape(N_TOK // PREP_TOKENS, 1, PREP_TOKENS)

    h, outa, q, k, v = _prep(x2, pos2, row(ln0_g), row(ln0_b), win, row(gm_ln_g[0]), row(gm_ln_b[0]),
                             w_spatial[0], bs, row(q_norm_g[0]), wuq, row(kv_norm_g[0]), wukv,
                             freq, phase, sign)
    ob = _attn(q, k, v)
    h1, h1b, ri, rf = _proj(outa, ob, h, wout, row(ln1_g[0]), row(ln1_b[0]), wr, br)
    ldest, runs, meta = _plan(ri)
    buf = _dispatch(meta, runs, ldest, h1b)
    eout = _experts(meta, buf, w_gate[0], w_up[0], w_down[0])
    out = _combine(runs, ldest, rf, h1, eout, row(ln2_g[0]), row(ln2_b[0]))
    return out.reshape(BATCH, SEQ, D_MODEL)
```

```python
import functools
import math

import jax
import jax.numpy as jnp
from jax import lax
from jax.experimental import pallas as pl
from jax.experimental.pallas import tpu as pltpu

D_MODEL = 1024
BATCH = 16
SEQ = 2048
N_TOK = BATCH * SEQ
CHUNK = 64
GM_WIDTH = 512
GM_HEADS = 4
GM_HEAD_DIM = 128
GM_CHUNK = 128
MLA_HEADS = 4
QK_NOPE_DIM = 128
QK_ROPE_DIM = 64
V_HEAD_DIM = 128
Q_LORA_RANK = 384
KV_LORA_RANK = 256
ROPE_THETA = 10000.0
N_GROUPS = 4
EXPERTS_PER_GROUP = 8
N_EXPERTS = 32
TOP_K = 2
EXPERT_FF = 256
ALPHA = 2.0 ** 0.25
QK_SCALE = (QK_NOPE_DIM + QK_ROPE_DIM) ** -0.5 * math.log2(math.e)

LANES = 128
SUBLANES = 8
FEAT_TILES = D_MODEL // LANES
PREP_TOKENS = 512
ATT_Q = 512
ATT_K = 512
N_QBLK = SEQ // ATT_Q
PROJ_TOKENS = 512
MOE_TOKENS = 512
N_MOE_BLOCKS = N_TOK // MOE_TOKENS
RUN_BITS = (TOP_K * MOE_TOKENS).bit_length()
SORT_CHUNK = 256
EXPERT_ROWS = 256
EXPERT_SUB = 2
EXPERT_RING = 3
N_ROWS = N_TOK * TOP_K + N_EXPERTS * EXPERT_ROWS
N_ROW_BLOCKS = N_ROWS // EXPERT_ROWS
META_LANES = 384
IN_COLS = 2 * GM_WIDTH + Q_LORA_RANK + KV_LORA_RANK + 2 * QK_ROPE_DIM
O_Q = 2 * GM_WIDTH
O_KV = O_Q + Q_LORA_RANK
O_KR = O_KV + KV_LORA_RANK
ROUTER_ROWS = 40
VMEM_LIMIT = 48 * 1024 * 1024

assert N_ROW_BLOCKS <= META_LANES and N_ROW_BLOCKS % EXPERT_SUB == 0


def _layer_norm(x, g, b, eps=1e-5):
    mu = jnp.mean(x, axis=-1, keepdims=True)
    xc = x - mu
    var = jnp.mean(xc * xc, axis=-1, keepdims=True)
    return xc * lax.rsqrt(var + eps) * g + b


def _rms_norm(x, g, eps=1e-6):
    return x * lax.rsqrt(jnp.mean(x * x, axis=-1, keepdims=True) + eps) * g


def _gelu_tanh(x):
    c = math.sqrt(2.0 / math.pi)
    return 0.5 * x * (1.0 + jnp.tanh(c * (x + 0.044715 * (x * x * x))))


def _to_row_tiles(ref, x):
    n = x.shape[0]
    for s in range(FEAT_TILES):
        ref[pl.ds(s, n, stride=FEAT_TILES), :] = x[:, s * LANES:(s + 1) * LANES]


def _from_row_tiles(ref, n):
    return jnp.concatenate(
        [ref[pl.ds(s, n, stride=FEAT_TILES), :] for s in range(FEAT_TILES)], axis=-1)


def _prep_kernel(x_ref, pos_ref, ln0g_ref, ln0b_ref, win_ref, gmg_ref, gmb_ref, ws_ref, bs_ref,
                 qg_ref, wuq_ref, kvg_ref, wukv_ref, freq_ref, phase_ref, sign_ref,
                 h_ref, outa_ref, q_ref, k_ref, v_ref, tabc_ref, tabs_ref, rot_ref):
    tb = x_ref.shape[0]

    @pl.when(pl.program_id(0) == 0)
    def _():
        d = lax.broadcasted_iota(jnp.int32, (tb, LANES), 0).astype(jnp.float32) * freq_ref[...]
        tabc_ref[...] = jnp.cos(d)
        tabs_ref[...] = jnp.sin(d)

    pos_row = pos_ref[0]
    p0 = pos_row[:, 0:1]
    offset = lax.broadcasted_iota(jnp.int32, (1, tb), 1)
    consecutive = jnp.max(jnp.abs((pos_row - p0 - offset).astype(jnp.float32))) == 0.0

    @pl.when(consecutive)
    def _():
        a0 = p0.astype(jnp.float32) * freq_ref[...]
        c0, s0 = jnp.cos(a0), jnp.sin(a0)
        lane = lax.broadcasted_iota(jnp.int32, (1, LANES), 1)
        coef_c = jnp.where(lane < 2 * 32, c0, jnp.where(lane < 3 * 32, -s0, s0))
        coef_s = jnp.where(lane < 2 * 32, -s0, jnp.where(lane < 3 * 32, -c0, c0))
        rot_ref[...] = coef_c * tabc_ref[...] + coef_s * tabs_ref[...]

    @pl.when(jnp.logical_not(consecutive))
    def _():
        pos_col = jnp.broadcast_to(pos_row.astype(jnp.float32), (SUBLANES, tb)).T[:, 0:1]
        ang = pos_col * freq_ref[...]
        rot_ref[...] = jnp.cos(ang - phase_ref[...]) * sign_ref[...]

    rot = rot_ref[...]

    h = _layer_norm(x_ref[...], ln0g_ref[...], ln0b_ref[...])
    h_ref[...] = h
    z = jnp.dot(h.astype(jnp.bfloat16), win_ref[...], preferred_element_type=jnp.float32)

    u = _gelu_tanh(z[:, :GM_WIDTH])
    v = _gelu_tanh(z[:, GM_WIDTH:2 * GM_WIDTH])
    row_chunk = lax.broadcasted_iota(jnp.int32, (GM_CHUNK, GM_CHUNK), 0) // CHUNK
    col_chunk = lax.broadcasted_iota(jnp.int32, (GM_CHUNK, GM_CHUNK), 1) // CHUNK
    allowed = col_chunk <= row_chunk
    for hd in range(GM_HEADS):
        lo, hi = hd * GM_HEAD_DIM, (hd + 1) * GM_HEAD_DIM
        vln = _layer_norm(v[:, lo:hi], gmg_ref[:, lo:hi], gmb_ref[:, lo:hi]).astype(jnp.bfloat16)
        wm = jnp.where(allowed, ws_ref[hd], 0.0).astype(jnp.bfloat16)
        for c in range(tb // GM_CHUNK):
            r0, r1 = c * GM_CHUNK, (c + 1) * GM_CHUNK
            f = jnp.dot(wm, vln[r0:r1], preferred_element_type=jnp.float32) + bs_ref[hd]
            outa_ref[r0:r1, lo:hi] = (u[r0:r1, lo:hi] * f).astype(jnp.bfloat16)

    ql = _rms_norm(z[:, O_Q:O_KV], qg_ref[...]).astype(jnp.bfloat16)
    qf = jnp.dot(ql, wuq_ref[...], preferred_element_type=jnp.float32)
    rot_s = rot * QK_SCALE
    q_parts = []
    for hd in range(MLA_HEADS):
        base = hd * 2 * LANES
        q_parts.append(qf[:, base:base + LANES] * QK_SCALE)
        q_parts.append(qf[:, base + LANES:base + 2 * LANES] * rot_s)
    q_ref[...] = jnp.concatenate(q_parts, axis=-1).astype(jnp.bfloat16)

    kvl = _rms_norm(z[:, O_KV:O_KR], kvg_ref[...]).astype(jnp.bfloat16)
    kv = jnp.dot(kvl, wukv_ref[...], preferred_element_type=jnp.float32)
    t = z[:, O_KR:O_KR + LANES] * rot
    krr = t + pltpu.roll(t, 2 * QK_ROPE_DIM // 2, axis=1)
    k_parts, v_parts = [], []
    for hd in range(MLA_HEADS):
        base = hd * 2 * LANES
        k_parts.append(kv[:, base:base + LANES])
        k_parts.append(krr)
        v_parts.append(kv[:, base + LANES:base + 2 * LANES])
    k_ref[...] = jnp.concatenate(k_parts, axis=-1).astype(jnp.bfloat16)
    v_ref[...] = jnp.concatenate(v_parts, axis=-1).astype(jnp.bfloat16)


def _prep(x2, pos2, ln0g, ln0b, win, gmg, gmb, ws, bs, qg, wuq, kvg, wukv, freq, phase, sign):
    tb = PREP_TOKENS
    full = lambda shape: pl.BlockSpec(shape, lambda i: (0,) * len(shape))
    tok = lambda cols: pl.BlockSpec((tb, cols), lambda i: (i, 0))
    return pl.pallas_call(
        _prep_kernel,
        grid=(N_TOK // tb,),
        in_specs=[tok(D_MODEL), pl.BlockSpec((1, 1, tb), lambda i: (i, 0, 0)), full((1, D_MODEL)), full((1, D_MODEL)),
                  full((D_MODEL, IN_COLS)), full((1, GM_WIDTH)), full((1, GM_WIDTH)),
                  full((GM_HEADS, GM_CHUNK, GM_CHUNK)), full((GM_HEADS, GM_CHUNK, GM_HEAD_DIM)),
                  full((1, Q_LORA_RANK)), full((Q_LORA_RANK, D_MODEL)),
                  full((1, KV_LORA_RANK)), full((KV_LORA_RANK, D_MODEL)),
                  full((1, LANES)), full((1, LANES)), full((1, LANES))],
        out_specs=[tok(D_MODEL), tok(GM_WIDTH), tok(D_MODEL), tok(D_MODEL), tok(GM_WIDTH)],
        out_shape=[jax.ShapeDtypeStruct((N_TOK, D_MODEL), jnp.float32),
                   jax.ShapeDtypeStruct((N_TOK, GM_WIDTH), jnp.bfloat16),
                   jax.ShapeDtypeStruct((N_TOK, D_MODEL), jnp.bfloat16),
                   jax.ShapeDtypeStruct((N_TOK, D_MODEL), jnp.bfloat16),
                   jax.ShapeDtypeStruct((N_TOK, GM_WIDTH), jnp.bfloat16)],
        scratch_shapes=[pltpu.VMEM((tb, LANES), jnp.float32)] * 3,
        compiler_params=pltpu.CompilerParams(
            dimension_semantics=("arbitrary",), vmem_limit_bytes=VMEM_LIMIT),
        name="prep",
    )(x2, pos2, ln0g, ln0b, win, gmg, gmb, ws, bs, qg, wuq, kvg, wukv, freq, phase, sign)


def _attn_kernel(q_ref, k_ref, v_ref, o_ref, s_ref, mx_ref, ls_ref, acc_ref):
    qi = pl.program_id(1)
    tq = q_ref.shape[0]
    nt = (((1,), (1,)), ((), ()))
    n_kv = SEQ // ATT_K

    half = tq // 2
    row_chunk = (lax.broadcasted_iota(jnp.int32, (half, half), 0)) // CHUNK
    col_chunk = (lax.broadcasted_iota(jnp.int32, (half, half), 1)) // CHUNK
    quad_allowed = col_chunk <= row_chunk
    diag_start = pl.multiple_of(qi * ATT_K, ATT_K)
    masked = jnp.float32(-1e30)

    def lane_tiles(x):
        return [x[:, t * LANES:(t + 1) * LANES] for t in range(x.shape[1] // LANES)]

    def lane_fold(x, op):
        return functools.reduce(op, lane_tiles(x))

    def scores(hd, j):
        start = pl.multiple_of(j * ATT_K, ATT_K)
        q_h = q_ref[:, hd * 2 * LANES:(hd + 1) * 2 * LANES]
        kb = k_ref[pl.ds(start, ATT_K), hd * 2 * LANES:(hd + 1) * 2 * LANES]
        return lax.dot_general(q_h, kb, nt, preferred_element_type=jnp.float32)

    for hd in range(MLA_HEADS):
        cols = slice(hd * 2 * LANES, (hd + 1) * 2 * LANES)
        k_lo = k_ref[pl.ds(diag_start, half), cols]
        k_hi = k_ref[pl.ds(diag_start + half, half), cols]
        qk = lambda q, k: lax.dot_general(q, k, nt, preferred_element_type=jnp.float32)
        s_tl = jnp.where(quad_allowed, qk(q_ref[:half, cols], k_lo), masked)
        s_bl = qk(q_ref[half:, cols], k_lo)
        s_br = jnp.where(quad_allowed, qk(q_ref[half:, cols], k_hi), masked)
        diag = s_ref.at[hd * n_kv + qi]
        diag[:half, :half] = s_tl
        diag[half:, :half] = s_bl
        diag[half:, half:] = s_br
        mx_ref[hd, :half] = lane_fold(s_tl, jnp.maximum)
        mx_ref[hd, half:] = jnp.maximum(lane_fold(s_bl, jnp.maximum), lane_fold(s_br, jnp.maximum))

    def pass_a(j, c):
        for hd in range(MLA_HEADS):
            s = scores(hd, j)
            s_ref[hd * n_kv + j] = s
            mx_ref[hd] = jnp.maximum(mx_ref[hd], lane_fold(s, jnp.maximum))
        return c

    lax.fori_loop(0, qi, pass_a, 0)

    for hd in range(MLA_HEADS):
        mx_ref[hd] = jnp.broadcast_to(jnp.max(mx_ref[hd], axis=-1, keepdims=True), (tq, LANES))

    def probs(hd, j):
        s = s_ref[hd * n_kv + j]
        mb = mx_ref[hd]
        p = jnp.exp2(jnp.concatenate([t - mb for t in lane_tiles(s)], axis=-1))
        start = pl.multiple_of(j * ATT_K, ATT_K)
        vb = v_ref[pl.ds(start, ATT_K), hd * LANES:(hd + 1) * LANES]
        pv = jnp.dot(p.astype(jnp.bfloat16), vb, preferred_element_type=jnp.float32)
        return lane_fold(p, jnp.add), pv

    for hd in range(MLA_HEADS):
        diag = s_ref.at[hd * n_kv + qi]
        vcols = slice(hd * LANES, (hd + 1) * LANES)
        p_top = jnp.exp2(jnp.concatenate([t - mx_ref[hd, :half] for t in lane_tiles(diag[:half, :half])], axis=-1))
        p_bot = jnp.exp2(jnp.concatenate([t - mx_ref[hd, half:] for t in lane_tiles(diag[half:, :])], axis=-1))
        ls_ref[hd, :half] = lane_fold(p_top, jnp.add)
        ls_ref[hd, half:] = lane_fold(p_bot, jnp.add)
        acc_ref[hd, :half] = jnp.dot(p_top.astype(jnp.bfloat16), v_ref[pl.ds(diag_start, half), vcols],
                                     preferred_element_type=jnp.float32)
        acc_ref[hd, half:] = jnp.dot(p_bot.astype(jnp.bfloat16), v_ref[pl.ds(diag_start, ATT_K), vcols],
                                     preferred_element_type=jnp.float32)

    def pass_b(j, c):
        for hd in range(MLA_HEADS):
            ls, pv = probs(hd, j)
            ls_ref[hd] = ls_ref[hd] + ls
            acc_ref[hd] = acc_ref[hd] + pv
        return c

    lax.fori_loop(0, qi, pass_b, 0)

    for hd in range(MLA_HEADS):
        l = jnp.sum(ls_ref[hd], axis=-1, keepdims=True)
        o_ref[:, hd * V_HEAD_DIM:(hd + 1) * V_HEAD_DIM] = (acc_ref[hd] / l).astype(jnp.bfloat16)


def _attn(q, k, v):
    tq = ATT_Q
    tokblk = lambda cols: pl.BlockSpec((tq, cols), lambda b, i: (b * N_QBLK + i, 0))
    seqblk = lambda cols: pl.BlockSpec((SEQ, cols), lambda b, i: (b, 0))
    return pl.pallas_call(
        _attn_kernel,
        grid=(BATCH, N_QBLK),
        in_specs=[tokblk(D_MODEL), seqblk(D_MODEL), seqblk(GM_WIDTH)],
        out_specs=tokblk(GM_WIDTH),
        out_shape=jax.ShapeDtypeStruct((N_TOK, MLA_HEADS * V_HEAD_DIM), jnp.bfloat16),
        scratch_shapes=[pltpu.VMEM((MLA_HEADS * (SEQ // ATT_K), tq, ATT_K), jnp.float32),
                        pltpu.VMEM((MLA_HEADS, tq, LANES), jnp.float32),
                        pltpu.VMEM((MLA_HEADS, tq, LANES), jnp.float32),
                        pltpu.VMEM((MLA_HEADS, tq, V_HEAD_DIM), jnp.float32)],
        compiler_params=pltpu.CompilerParams(
            dimension_semantics=("arbitrary", "arbitrary"), vmem_limit_bytes=VMEM_LIMIT),
        name="attn",
    )(q, k, v)


def _proj_kernel(outa_ref, ob_ref, h_ref, wout_ref, ln1g_ref, ln1b_ref, wr_ref, br_ref,
                 h1_ref, h1b_ref, ri_ref, rf_ref, proj_ref):
    i = pl.program_id(0)
    tb = outa_ref.shape[0]

    @pl.when(i == 0)
    def _():
        proj_ref[...] = jnp.zeros_like(proj_ref)

    h1 = _layer_norm(ALPHA * h_ref[...] + proj_ref[(i + 1) % 2], ln1g_ref[...], ln1b_ref[...])
    h1_ref[...] = h1
    h1b_ref[...] = h1.astype(jnp.bfloat16)

    logits_tm = jnp.dot(h1b_ref[...], wr_ref[...], preferred_element_type=jnp.float32)
    logits = logits_tm.T[0:ROUTER_ROWS] + br_ref[...]

    sub_i = lax.broadcasted_iota(jnp.int32, (SUBLANES, tb), 0)
    sub = sub_i.astype(jnp.float32)
    neg = jnp.float32(-jnp.inf)
    g = jnp.where(sub_i < N_GROUPS, logits[0:SUBLANES], neg)
    gmax = jnp.max(g, axis=0, keepdims=True)
    g_top = jnp.min(jnp.where(g == gmax, sub, float(SUBLANES)), axis=0, keepdims=True)
    p_group = 1.0 / jnp.sum(jnp.exp(g - gmax), axis=0, keepdims=True)
    sel = logits[SUBLANES:2 * SUBLANES]
    for grp in range(1, N_GROUPS):
        sel = jnp.where(g_top == float(grp), logits[(grp + 1) * SUBLANES:(grp + 2) * SUBLANES], sel)
    v1 = jnp.max(sel, axis=0, keepdims=True)
    i1 = jnp.min(jnp.where(sel == v1, sub, float(SUBLANES)), axis=0, keepdims=True)
    sel2 = jnp.where(sub == i1, neg, sel)
    v2 = jnp.max(sel2, axis=0, keepdims=True)
    i2 = jnp.min(jnp.where(sel2 == v2, sub, float(SUBLANES)), axis=0, keepdims=True)
    e21 = jnp.exp(v2 - v1)
    w1 = 1.0 / (1.0 + e21)
    gate1 = p_group * w1
    gate2 = p_group * (e21 * w1)
    e1 = g_top * EXPERTS_PER_GROUP + i1
    e2 = g_top * EXPERTS_PER_GROUP + i2
    ri_ref[...] = jnp.where(sub_i == 0, e1, jnp.where(sub_i == 1, e2, 0.0)).astype(jnp.int32)
    rf_ref[...] = jnp.where(sub_i == 0, gate1, jnp.where(sub_i == 1, gate2, 0.0))

    proj_ref[i % 2] = (jnp.dot(outa_ref[...], wout_ref[:GM_WIDTH, :], preferred_element_type=jnp.float32)
                       + jnp.dot(ob_ref[...], wout_ref[GM_WIDTH:, :], preferred_element_type=jnp.float32))


def _proj(outa, ob, h, wout, ln1g, ln1b, wr, br):
    tb = PROJ_TOKENS
    n_blk = N_TOK // tb
    cur = lambda i: jnp.minimum(i, n_blk - 1)
    prev = lambda i: jnp.maximum(i - 1, 0)
    full = lambda shape: pl.BlockSpec(shape, lambda i: (0,) * len(shape))
    return pl.pallas_call(
        _proj_kernel,
        grid=(n_blk + 1,),
        in_specs=[pl.BlockSpec((tb, GM_WIDTH), lambda i: (cur(i), 0)),
                  pl.BlockSpec((tb, GM_WIDTH), lambda i: (cur(i), 0)),
                  pl.BlockSpec((tb, D_MODEL), lambda i: (prev(i), 0)),
                  full((D_MODEL, D_MODEL)), full((1, D_MODEL)), full((1, D_MODEL)),
                  full((D_MODEL, LANES)), full((ROUTER_ROWS, 1))],
        out_specs=[pl.BlockSpec((tb, D_MODEL), lambda i: (prev(i), 0)),
                   pl.BlockSpec((tb, D_MODEL), lambda i: (prev(i), 0)),
                   pl.BlockSpec((SUBLANES, tb), lambda i: (0, prev(i))),
                   pl.BlockSpec((SUBLANES, tb), lambda i: (0, prev(i)))],
        out_shape=[jax.ShapeDtypeStruct((N_TOK, D_MODEL), jnp.float32),
                   jax.ShapeDtypeStruct((N_TOK, D_MODEL), jnp.bfloat16),
                   jax.ShapeDtypeStruct((SUBLANES, N_TOK), jnp.int32),
                   jax.ShapeDtypeStruct((SUBLANES, N_TOK), jnp.float32)],
        scratch_shapes=[pltpu.VMEM((2, tb, D_MODEL), jnp.float32)],
        compiler_params=pltpu.CompilerParams(
            dimension_semantics=("arbitrary",), vmem_limit_bytes=VMEM_LIMIT),
        name="proj",
    )(outa, ob, h, wout, ln1g, ln1b, wr, br)


def _plan_kernel(ri_all_ref, ri_ref, ldest_ref, runs_ref, meta_ref, run_ref, start_ref, upper_ref):
    step = pl.program_id(0)
    tb = ri_ref.shape[1]
    f32 = jnp.float32
    er = lax.broadcasted_iota(jnp.int32, (N_EXPERTS, LANES), 0)
    ec = lax.broadcasted_iota(jnp.int32, (N_EXPERTS, LANES), 1)
    to_row = lambda col: jnp.sum(jnp.where(er == ec, col, 0.0), axis=0, keepdims=True)

    def expert_one_hot(ref):
        e_sub = lax.broadcasted_iota(jnp.int32, (N_EXPERTS, ref.shape[1]), 0)
        return e_sub == ref[0:1, :], e_sub == ref[1:2, :]

    @pl.when(step == 0)
    def _():
        oh1, oh2 = expert_one_hot(ri_all_ref)
        counts = jnp.sum(jnp.where(oh1 | oh2, 1.0, 0.0), axis=1, keepdims=True)
        padded = jnp.floor((counts + (EXPERT_ROWS - 1)) * (1.0 / EXPERT_ROWS)) * EXPERT_ROWS
        pad_end = jnp.sum(jnp.where(ec <= er, to_row(padded), 0.0), axis=1, keepdims=True)
        start_ref[...] = jnp.broadcast_to(pad_end - padded, start_ref.shape)
        run_ref[...] = jnp.zeros_like(run_ref)
        bstart = (lax.broadcasted_iota(jnp.int32, (N_EXPERTS, META_LANES), 1) * EXPERT_ROWS).astype(f32)
        blk_e = jnp.sum(jnp.where(pad_end <= bstart, 1.0, 0.0), axis=0, keepdims=True)
        blk_e = jnp.minimum(blk_e, N_EXPERTS - 1.0)
        n_used = pad_end[N_EXPERTS - 1:N_EXPERTS, :] * (1.0 / EXPERT_ROWS)
        pad3 = lambda r: jnp.concatenate(
            [r, jnp.zeros((1, META_LANES - LANES), f32)], axis=1)
        msub = lax.broadcasted_iota(jnp.int32, (SUBLANES, META_LANES), 0)
        meta = jnp.where(msub == 0, blk_e,
                         jnp.where(msub == 1, pad3(to_row(pad_end)),
                                   jnp.where(msub == 2, pad3(to_row(counts)),
                                             jnp.where(msub == 3, n_used, 0.0))))
        meta_ref[...] = meta.astype(jnp.int32)
        tr = lax.broadcasted_iota(jnp.int32, (tb, tb), 0)
        tc = lax.broadcasted_iota(jnp.int32, (tb, tb), 1)
        upper_ref[...] = jnp.where(tr < tc, 1.0, 0.0).astype(jnp.bfloat16)

    @pl.when(step > 0)
    def _():
        oh1, oh2 = expert_one_hot(ri_ref)
        oh = jnp.where(oh1 | oh2, 1.0, 0.0).astype(f32)
        blk_count = jnp.sum(oh, axis=1, keepdims=True)
        prefix = jnp.dot(oh.astype(jnp.bfloat16), upper_ref[...], preferred_element_type=f32)
        cnt_row = to_row(blk_count)
        lstart = jnp.sum(jnp.where(ec < er, cnt_row, 0.0), axis=1, keepdims=True)
        base = prefix + lstart
        d1 = jnp.sum(jnp.where(oh1, base, 0.0), axis=0, keepdims=True)
        d2 = jnp.sum(jnp.where(oh2, base, 0.0), axis=0, keepdims=True)
        sub = lax.broadcasted_iota(jnp.int32, (SUBLANES, tb), 0)
        ldest_ref[...] = jnp.where(sub == 0, d1, jnp.where(sub == 1, d2, 0.0)).astype(jnp.int32)
        gstart = start_ref[:, 0:1] + run_ref[:, 0:1]
        rsub = lax.broadcasted_iota(jnp.int32, (SUBLANES, LANES), 0)
        runs = jnp.where(rsub == 0, cnt_row,
                         jnp.where(rsub == 1, to_row(lstart), jnp.where(rsub == 2, to_row(gstart), 0.0)))
        runs_ref[...] = runs.astype(jnp.int32)
        run_ref[...] = run_ref[...] + blk_count


def _plan(ri):
    tb = MOE_TOKENS
    blk = lambda i: jnp.maximum(i - 1, 0)
    return pl.pallas_call(
        _plan_kernel,
        grid=(N_MOE_BLOCKS + 1,),
        in_specs=[pl.BlockSpec((SUBLANES, N_TOK), lambda i: (0, 0)),
                  pl.BlockSpec((SUBLANES, tb), lambda i: (0, blk(i)))],
        out_specs=[pl.BlockSpec((SUBLANES, tb), lambda i: (0, blk(i))),
                   pl.BlockSpec((SUBLANES, LANES), lambda i: (blk(i), 0)),
                   pl.BlockSpec((SUBLANES, META_LANES), lambda i: (0, 0))],
        out_shape=[jax.ShapeDtypeStruct((SUBLANES, N_TOK), jnp.int32),
                   jax.ShapeDtypeStruct((N_MOE_BLOCKS * SUBLANES, LANES), jnp.int32),
                   jax.ShapeDtypeStruct((SUBLANES, META_LANES), jnp.int32)],
        scratch_shapes=[pltpu.VMEM((N_EXPERTS, LANES), jnp.float32),
                        pltpu.VMEM((N_EXPERTS, LANES), jnp.float32),
                        pltpu.VMEM((tb, tb), jnp.bfloat16)],
        compiler_params=pltpu.CompilerParams(
            dimension_semantics=("arbitrary",), vmem_limit_bytes=VMEM_LIMIT),
        name="plan",
    )(ri, ri)


def _for_each_run_piece(runs_ref, fn):
    for e in range(N_EXPERTS):
        n, lstart, gstart = runs_ref[0, e], runs_ref[1, e], runs_ref[2, e]
        for bit in range(RUN_BITS):
            @pl.when((n & (1 << bit)) != 0)
            def _(n=n, lstart=lstart, gstart=gstart, bit=bit):
                off = (n >> (bit + 1)) << (bit + 1)
                fn(lstart + off, gstart + off, 1 << bit)


def _tile_rows(ref, row, rows):
    return ref.at[pl.ds(pl.multiple_of(row * FEAT_TILES, FEAT_TILES), rows * FEAT_TILES)]


def _dispatch_kernel(meta_ref, runs_ref, ldest_ref, h1b_ref, buf_ref, sorted_ref, zero_ref, sems, zsem):
    i = pl.program_id(0)
    n_steps = pl.num_programs(0)
    tb = ldest_ref.shape[1]
    slot = i % 2
    block_tiles = 2 * tb * FEAT_TILES

    def wait_slot(s):
        pltpu.make_async_copy(sorted_ref.at[s], buf_ref.at[pl.ds(0, block_tiles)], sems.at[s]).wait()

    @pl.when(i == 0)
    def _():
        zero_ref[...] = jnp.zeros_like(zero_ref)

        def zero_copy(e):
            start = pl.multiple_of((meta_ref[1, e] - EXPERT_ROWS) * FEAT_TILES, EXPERT_ROWS * FEAT_TILES)
            return pltpu.make_async_copy(
                zero_ref, buf_ref.at[pl.ds(start, EXPERT_ROWS * FEAT_TILES)], zsem)

        def start_zero(e, c):
            @pl.when(meta_ref[2, e] > 0)
            def _():
                zero_copy(e).start()
            return c

        def wait_zero(e, c):
            @pl.when(meta_ref[2, e] > 0)
            def _():
                zero_copy(e).wait()
            return c

        def tail_copy(b):
            start = pl.multiple_of(b * (EXPERT_ROWS * FEAT_TILES), EXPERT_ROWS * FEAT_TILES)
            return pltpu.make_async_copy(
                zero_ref, buf_ref.at[pl.ds(start, EXPERT_ROWS * FEAT_TILES)], zsem)

        def start_tail(b, c):
            tail_copy(b).start()
            return c

        def wait_tail(b, c):
            tail_copy(b).wait()
            return c

        lax.fori_loop(0, N_EXPERTS, start_zero, 0)
        lax.fori_loop(meta_ref[3, 0], N_ROW_BLOCKS, start_tail, 0)
        lax.fori_loop(0, N_EXPERTS, wait_zero, 0)
        lax.fori_loop(meta_ref[3, 0], N_ROW_BLOCKS, wait_tail, 0)

    @pl.when(i >= 2)
    def _():
        wait_slot(slot)

    x = h1b_ref[...]
    ld0 = ldest_ref[0:1, :]
    ld1 = ldest_ref[1:2, :]
    for c in range(2 * tb // SORT_CHUNK):
        r = lax.broadcasted_iota(jnp.int32, (SORT_CHUNK, tb), 0) + c * SORT_CHUNK
        perm = jnp.where((r == ld0) | (r == ld1), 1.0, 0.0).astype(jnp.bfloat16)
        rows = jnp.dot(perm, x, preferred_element_type=jnp.float32)
        _to_row_tiles(sorted_ref.at[slot, pl.ds(c * SORT_CHUNK * FEAT_TILES, SORT_CHUNK * FEAT_TILES)], rows)

    def send(lrow, grow, rows):
        pltpu.make_async_copy(_tile_rows(sorted_ref.at[slot], lrow, rows),
                              _tile_rows(buf_ref, grow, rows), sems.at[slot]).start()

    _for_each_run_piece(runs_ref, send)

    @pl.when(i == n_steps - 1)
    def _():
        wait_slot(slot)
        wait_slot(1 - slot)


def _dispatch(meta, runs, ldest, h1b):
    tb = MOE_TOKENS
    return pl.pallas_call(
        _dispatch_kernel,
        grid_spec=pltpu.PrefetchScalarGridSpec(
            num_scalar_prefetch=1,
            grid=(N_MOE_BLOCKS,),
            in_specs=[pl.BlockSpec((SUBLANES, LANES), lambda i, m: (i, 0), memory_space=pltpu.SMEM),
                      pl.BlockSpec((SUBLANES, tb), lambda i, m: (0, i)),
                      pl.BlockSpec((tb, D_MODEL), lambda i, m: (i, 0))],
            out_specs=pl.BlockSpec(memory_space=pl.ANY),
            scratch_shapes=[pltpu.VMEM((2, 2 * tb * FEAT_TILES, LANES), jnp.float32),
                            pltpu.VMEM((EXPERT_ROWS * FEAT_TILES, LANES), jnp.float32),
                            pltpu.SemaphoreType.DMA((2,)), pltpu.SemaphoreType.DMA]),
        out_shape=jax.ShapeDtypeStruct((N_ROWS * FEAT_TILES, LANES), jnp.float32),
        compiler_params=pltpu.CompilerParams(
            dimension_semantics=("arbitrary",), vmem_limit_bytes=VMEM_LIMIT),
        name="dispatch",
    )(meta, runs, ldest, h1b)


def _sub_block_expert(meta, i, sub):
    return meta[0, jnp.minimum(i + sub * (N_ROW_BLOCKS // EXPERT_SUB), meta[3, 0] - 1)]


def _experts_kernel(meta_ref, x_hbm_ref, wg0_ref, wu0_ref, wd0_ref, wg1_ref, wu1_ref, wd1_ref, o_ref,
                    wgb_ref, wub_ref, wdb_ref, cached_ref, xring_ref, xsems):
    n = EXPERT_ROWS
    i = pl.program_id(0)
    n_steps = pl.num_programs(0)
    weights = ((wg0_ref, wu0_ref, wd0_ref), (wg1_ref, wu1_ref, wd1_ref))

    def rows_copy(step):
        slot = step % EXPERT_RING
        start = pl.multiple_of(step * (n * FEAT_TILES), n * FEAT_TILES)
        return pltpu.make_async_copy(x_hbm_ref.at[:, pl.ds(start, n * FEAT_TILES), :], xring_ref.at[slot],
                                     xsems.at[slot])

    @pl.when(i == 0)
    def _():
        for step in range(EXPERT_RING - 1):
            rows_copy(step).start()

    @pl.when(i + (EXPERT_RING - 1) < n_steps)
    def _():
        rows_copy(i + (EXPERT_RING - 1)).start()

    rows_copy(i).wait()
    x_ref = xring_ref.at[i % EXPERT_RING]

    @pl.when(i == 0)
    def _():
        for sub in range(EXPERT_SUB):
            cached_ref[sub] = -1

    for sub in range(EXPERT_SUB):
        expert = _sub_block_expert(meta_ref, i, sub)

        @pl.when(cached_ref[sub] != expert)
        def _(sub=sub, expert=expert):
            wg_ref, wu_ref, wd_ref = weights[sub]
            wgb_ref[sub] = wg_ref[...].astype(jnp.bfloat16)
            wub_ref[sub] = wu_ref[...].astype(jnp.bfloat16)
            wdb_ref[sub] = wd_ref[...].astype(jnp.bfloat16)
            cached_ref[sub] = expert

    for sub in range(EXPERT_SUB):
        x = _from_row_tiles(x_ref.at[sub], n).astype(jnp.bfloat16)
        gate = jnp.dot(x, wgb_ref[sub], preferred_element_type=jnp.float32)
        up = jnp.dot(x, wub_ref[sub], preferred_element_type=jnp.float32)
        act = (gate * jax.nn.sigmoid(gate) * up).astype(jnp.bfloat16)
        _to_row_tiles(o_ref.at[sub], jnp.dot(act, wdb_ref[sub], preferred_element_type=jnp.float32))


def _experts(meta, buf, wg, wu, wd):
    sub_tiles = (N_ROW_BLOCKS // EXPERT_SUB) * EXPERT_ROWS * FEAT_TILES

    def weight_spec(shape, sub):
        return pl.BlockSpec((None,) + shape, lambda i, m: (_sub_block_expert(m, i, sub), 0, 0))

    rows_shape = (EXPERT_SUB, EXPERT_ROWS * FEAT_TILES, LANES)
    up_shape, down_shape = (D_MODEL, EXPERT_FF), (EXPERT_FF, D_MODEL)
    eout = pl.pallas_call(
        _experts_kernel,
        grid_spec=pltpu.PrefetchScalarGridSpec(
            num_scalar_prefetch=1,
            grid=(N_ROW_BLOCKS // EXPERT_SUB,),
            in_specs=[pl.BlockSpec(memory_space=pl.ANY),
                      weight_spec(up_shape, 0), weight_spec(up_shape, 0), weight_spec(down_shape, 0),
                      weight_spec(up_shape, 1), weight_spec(up_shape, 1), weight_spec(down_shape, 1)],
            out_specs=pl.BlockSpec(rows_shape, lambda i, m: (0, i, 0)),
            scratch_shapes=[pltpu.VMEM((EXPERT_SUB,) + up_shape, jnp.bfloat16),
                            pltpu.VMEM((EXPERT_SUB,) + up_shape, jnp.bfloat16),
                            pltpu.VMEM((EXPERT_SUB,) + down_shape, jnp.bfloat16),
                            pltpu.SMEM((EXPERT_SUB,), jnp.int32),
                            pltpu.VMEM((EXPERT_RING,) + rows_shape, jnp.float32),
                            pltpu.SemaphoreType.DMA((EXPERT_RING,))]),
        out_shape=jax.ShapeDtypeStruct((EXPERT_SUB, sub_tiles, LANES), jnp.float32),
        compiler_params=pltpu.CompilerParams(
            dimension_semantics=("arbitrary",), vmem_limit_bytes=VMEM_LIMIT),
        name="experts",
    )(meta, buf.reshape(EXPERT_SUB, sub_tiles, LANES), wg, wu, wd, wg, wu, wd)
    return eout.reshape(N_ROWS * FEAT_TILES, LANES)


def _combine_kernel(runs_ref, runs_next_ref, ldest_ref, rf_ref, h1_ref, eout_ref, ln2g_ref, ln2b_ref,
                    o_ref, y_ref, sems):
    i = pl.program_id(0)
    n_steps = pl.num_programs(0)
    tb = ldest_ref.shape[1]
    slot = i % 2
    block_tiles = 2 * tb * FEAT_TILES

    def fetch(table_ref, s):
        def recv(lrow, grow, rows):
            pltpu.make_async_copy(_tile_rows(eout_ref, grow, rows),
                                  _tile_rows(y_ref.at[s], lrow, rows), sems.at[s]).start()
        _for_each_run_piece(table_ref, recv)

    @pl.when(i == 0)
    def _():
        fetch(runs_ref, slot)

    @pl.when(i + 1 < n_steps)
    def _():
        fetch(runs_next_ref, 1 - slot)

    pltpu.make_async_copy(eout_ref.at[pl.ds(0, block_tiles)], y_ref.at[slot], sems.at[slot]).wait()

    ld = ldest_ref[...].astype(jnp.float32).T
    gates = rf_ref[...].T
    y = None
    col = lax.broadcasted_iota(jnp.int32, (tb, SORT_CHUNK), 1).astype(jnp.float32)
    for c in range(2 * tb // SORT_CHUNK):
        ld_c = ld - float(c * SORT_CHUNK)
        g = jnp.where(col == ld_c[:, 0:1], gates[:, 0:1],
                      jnp.where(col == ld_c[:, 1:2], gates[:, 1:2], 0.0)).astype(jnp.bfloat16)
        rows = _from_row_tiles(
            y_ref.at[slot, pl.ds(c * SORT_CHUNK * FEAT_TILES, SORT_CHUNK * FEAT_TILES)], SORT_CHUNK)
        part = jnp.dot(g, rows.astype(jnp.bfloat16), preferred_element_type=jnp.float32)
        y = part if y is None else y + part
    o_ref[...] = _layer_norm(ALPHA * h1_ref[...] + y, ln2g_ref[...], ln2b_ref[...])


def _combine(runs, ldest, rf, h1, eout, ln2g, ln2b):
    tb = MOE_TOKENS
    last = N_MOE_BLOCKS - 1
    return pl.pallas_call(
        _combine_kernel,
        grid=(N_MOE_BLOCKS,),
        in_specs=[pl.BlockSpec((SUBLANES, LANES), lambda i: (i, 0), memory_space=pltpu.SMEM),
                  pl.BlockSpec((SUBLANES, LANES), lambda i: (jnp.minimum(i + 1, last), 0),
                               memory_space=pltpu.SMEM),
                  pl.BlockSpec((SUBLANES, tb), lambda i: (0, i)),
                  pl.BlockSpec((SUBLANES, tb), lambda i: (0, i)),
                  pl.BlockSpec((tb, D_MODEL), lambda i: (i, 0)),
                  pl.BlockSpec(memory_space=pl.ANY),
                  pl.BlockSpec((1, D_MODEL), lambda i: (0, 0)),
                  pl.BlockSpec((1, D_MODEL), lambda i: (0, 0))],
        out_specs=pl.BlockSpec((tb, D_MODEL), lambda i: (i, 0)),
        out_shape=jax.ShapeDtypeStruct((N_TOK, D_MODEL), jnp.float32),
        scratch_shapes=[pltpu.VMEM((2, 2 * tb * FEAT_TILES, LANES), jnp.float32),
                        pltpu.SemaphoreType.DMA((2,))],
        compiler_params=pltpu.CompilerParams(
            dimension_semantics=("arbitrary",), vmem_limit_bytes=VMEM_LIMIT),
        name="combine",
    )(runs, runs, ldest, rf, h1, eout, ln2g, ln2b)


def _swap_halves(w):
    half = w.shape[-1] // 2
    return jnp.concatenate([w[..., half:], w[..., :half]], axis=-1)


def kernel(x, positions, ln0_g, ln0_b, w_in, gm_ln_g, gm_ln_b, w_spatial, b_spatial, q_norm_g, w_uq, kv_norm_g, w_ukv, w_out, ln1_g, ln1_b, w_router_group, b_router_group, w_router_expert, b_router_expert, w_gate, w_up, w_down, ln2_g, ln2_b):
    bf16 = jnp.bfloat16
    row = lambda a: a.reshape(1, -1)

    w_in0 = w_in[0]
    kr_cols = w_in0[:, O_KR:O_KR + QK_ROPE_DIM]
    win = jnp.concatenate([w_in0, _swap_halves(kr_cols)], axis=1).astype(bf16)
    wuq3 = w_uq[0].reshape(Q_LORA_RANK, MLA_HEADS, QK_NOPE_DIM + QK_ROPE_DIM)
    rope_cols = wuq3[:, :, QK_NOPE_DIM:]
    wuq = jnp.concatenate([wuq3, _swap_halves(rope_cols)], axis=-1).reshape(Q_LORA_RANK, D_MODEL).astype(bf16)
    wukv = w_ukv[0].astype(bf16)
    wout = w_out[0].astype(bf16)
    bs = jnp.broadcast_to(b_spatial[0][:, :, None], (GM_HEADS, GM_CHUNK, GM_HEAD_DIM))
    wr = jnp.concatenate([w_router_group[0], jnp.zeros((D_MODEL, SUBLANES - N_GROUPS), jnp.float32),
                          w_router_expert[0],
                          jnp.zeros((D_MODEL, LANES - ROUTER_ROWS), jnp.float32)],
                         axis=1).astype(bf16)
    br = jnp.concatenate([b_router_group[0], jnp.zeros((SUBLANES - N_GROUPS,), jnp.float32),
                          b_router_expert[0]]).reshape(ROUTER_ROWS, 1)

    inv_freq = ROPE_THETA ** (-jnp.arange(0, QK_ROPE_DIM, 2, dtype=jnp.float32) / QK_ROPE_DIM)
    freq = jnp.tile(inv_freq, 4).reshape(1, LANES)
    quarter = QK_ROPE_DIM // 2
    phase = jnp.concatenate([jnp.zeros((2 * quarter,), jnp.float32),
                             jnp.full((2 * quarter,), math.pi / 2, jnp.float32)]).reshape(1, LANES)
    sign = jnp.concatenate([jnp.ones((2 * quarter,), jnp.float32), -jnp.ones((quarter,), jnp.float32),
                            jnp.ones((quarter,), jnp.float32)]).reshape(1, LANES)

    x2 = x.reshape(N_TOK, D_MODEL)
    pos2 = positions.reshape(N_TOK // PREP_TOKENS, 1, PREP_TOKENS)

    h, outa, q, k, v = _prep(x2, pos2, row(ln0_g), row(ln0_b), win, row(gm_ln_g[0]), row(gm_ln_b[0]),
                             w_spatial[0], bs, row(q_norm_g[0]), wuq, row(kv_norm_g[0]), wukv,
                             freq, phase, sign)
    ob = _attn(q, k, v)
    h1, h1b, ri, rf = _proj(outa, ob, h, wout, row(ln1_g[0]), row(ln1_b[0]), wr, br)
    ldest, runs, meta = _plan(ri)
    buf = _dispatch(meta, runs, ldest, h1b)
    eout = _experts(meta, buf, w_gate[0], w_up[0], w_down[0])
    out = _combine(runs, ldest, rf, h1, eout, row(ln2_g[0]), row(ln2_b[0]))
    return out.reshape(BATCH, SEQ, D_MODEL)
```

```python
import functools
import math

import jax
import jax.numpy as jnp
from jax import lax
from jax.experimental import pallas as pl
from jax.experimental.pallas import tpu as pltpu

D_MODEL = 1024
BATCH = 16
SEQ = 2048
N_TOK = BATCH * SEQ
CHUNK = 64
GM_WIDTH = 512
GM_HEADS = 4
GM_HEAD_DIM = 128
GM_CHUNK = 128
MLA_HEADS = 4
QK_NOPE_DIM = 128
QK_ROPE_DIM = 64
V_HEAD_DIM = 128
Q_LORA_RANK = 384
KV_LORA_RANK = 256
ROPE_THETA = 10000.0
N_GROUPS = 4
EXPERTS_PER_GROUP = 8
N_EXPERTS = 32
TOP_K = 2
EXPERT_FF = 256
ALPHA = 2.0 ** 0.25
QK_SCALE = (QK_NOPE_DIM + QK_ROPE_DIM) ** -0.5 * math.log2(math.e)

LANES = 128
SUBLANES = 8
FEAT_TILES = D_MODEL // LANES
PREP_TOKENS = 512
ATT_Q = 512
ATT_K = 512
N_QBLK = SEQ // ATT_Q
PROJ_TOKENS = 512
MOE_TOKENS = 512
N_MOE_BLOCKS = N_TOK // MOE_TOKENS
RUN_BITS = (TOP_K * MOE_TOKENS).bit_length()
SORT_CHUNK = 256
EXPERT_ROWS = 256
EXPERT_SUB = 2
EXPERT_RING = 3
INPUT_RING = 3
N_ROWS = N_TOK * TOP_K + N_EXPERTS * EXPERT_ROWS
N_ROW_BLOCKS = N_ROWS // EXPERT_ROWS
META_LANES = 384
IN_COLS = 2 * GM_WIDTH + Q_LORA_RANK + KV_LORA_RANK + 2 * QK_ROPE_DIM
O_Q = 2 * GM_WIDTH
O_KV = O_Q + Q_LORA_RANK
O_KR = O_KV + KV_LORA_RANK
ROUTER_ROWS = 40
VMEM_LIMIT = 48 * 1024 * 1024

assert N_ROW_BLOCKS <= META_LANES and N_ROW_BLOCKS % EXPERT_SUB == 0


def _layer_norm(x, g, b, eps=1e-5):
    mu = jnp.mean(x, axis=-1, keepdims=True)
    xc = x - mu
    var = jnp.mean(xc * xc, axis=-1, keepdims=True)
    return xc * lax.rsqrt(var + eps) * g + b


def _rms_norm(x, g, eps=1e-6):
    return x * lax.rsqrt(jnp.mean(x * x, axis=-1, keepdims=True) + eps) * g


def _gelu_tanh(x):
    c = math.sqrt(2.0 / math.pi)
    return 0.5 * x * (1.0 + jnp.tanh(c * (x + 0.044715 * (x * x * x))))


def _to_row_tiles(ref, x):
    n = x.shape[0]
    for s in range(FEAT_TILES):
        ref[pl.ds(s, n, stride=FEAT_TILES), :] = x[:, s * LANES:(s + 1) * LANES]


def _from_row_tiles(ref, n):
    return jnp.concatenate(
        [ref[pl.ds(s, n, stride=FEAT_TILES), :] for s in range(FEAT_TILES)], axis=-1)


def _ring_block(hbm_ref, ring_ref, sems, block_of_step, rows):
    i = pl.program_id(0)
    n_steps = pl.num_programs(0)

    def copy(step):
        start = pl.multiple_of(block_of_step(step) * rows, rows)
        slot = step % INPUT_RING
        return pltpu.make_async_copy(hbm_ref.at[pl.ds(start, rows)], ring_ref.at[slot], sems.at[slot])

    @pl.when(i == 0)
    def _():
        for step in range(INPUT_RING - 1):
            copy(step).start()

    @pl.when(i + (INPUT_RING - 1) < n_steps)
    def _():
        copy(i + (INPUT_RING - 1)).start()

    copy(i).wait()
    return ring_ref.at[i % INPUT_RING]


def _ring_scratch(rows, cols, dtype):
    return [pltpu.VMEM((INPUT_RING, rows, cols), dtype), pltpu.SemaphoreType.DMA((INPUT_RING,))]


def _prep_kernel(x_hbm_ref, pos_ref, ln0g_ref, ln0b_ref, win_ref, gmg_ref, gmb_ref, ws_ref, bs_ref,
                 qg_ref, wuq_ref, kvg_ref, wukv_ref, freq_ref, phase_ref, sign_ref,
                 h_ref, outa_ref, q_ref, k_ref, v_ref, tabc_ref, tabs_ref, rot_ref, xring_ref, xsems):
    tb = h_ref.shape[0]
    x_ref = _ring_block(x_hbm_ref, xring_ref, xsems, lambda step: step, tb)

    @pl.when(pl.program_id(0) == 0)
    def _():
        d = lax.broadcasted_iota(jnp.int32, (tb, LANES), 0).astype(jnp.float32) * freq_ref[...]
        tabc_ref[...] = jnp.cos(d)
        tabs_ref[...] = jnp.sin(d)

    pos_row = pos_ref[0]
    p0 = pos_row[:, 0:1]
    offset = lax.broadcasted_iota(jnp.int32, (1, tb), 1)
    consecutive = jnp.max(jnp.abs((pos_row - p0 - offset).astype(jnp.float32))) == 0.0

    @pl.when(consecutive)
    def _():
        a0 = p0.astype(jnp.float32) * freq_ref[...]
        c0, s0 = jnp.cos(a0), jnp.sin(a0)
        lane = lax.broadcasted_iota(jnp.int32, (1, LANES), 1)
        coef_c = jnp.where(lane < 2 * 32, c0, jnp.where(lane < 3 * 32, -s0, s0))
        coef_s = jnp.where(lane < 2 * 32, -s0, jnp.where(lane < 3 * 32, -c0, c0))
        rot_ref[...] = coef_c * tabc_ref[...] + coef_s * tabs_ref[...]

    @pl.when(jnp.logical_not(consecutive))
    def _():
        pos_col = jnp.broadcast_to(pos_row.astype(jnp.float32), (SUBLANES, tb)).T[:, 0:1]
        ang = pos_col * freq_ref[...]
        rot_ref[...] = jnp.cos(ang - phase_ref[...]) * sign_ref[...]

    rot = rot_ref[...]

    h = _layer_norm(x_ref[...], ln0g_ref[...], ln0b_ref[...])
    h_ref[...] = h
    z = jnp.dot(h.astype(jnp.bfloat16), win_ref[...], preferred_element_type=jnp.float32)

    u = _gelu_tanh(z[:, :GM_WIDTH])
    v = _gelu_tanh(z[:, GM_WIDTH:2 * GM_WIDTH])
    row_chunk = lax.broadcasted_iota(jnp.int32, (GM_CHUNK, GM_CHUNK), 0) // CHUNK
    col_chunk = lax.broadcasted_iota(jnp.int32, (GM_CHUNK, GM_CHUNK), 1) // CHUNK
    allowed = col_chunk <= row_chunk
    for hd in range(GM_HEADS):
        lo, hi = hd * GM_HEAD_DIM, (hd + 1) * GM_HEAD_DIM
        vln = _layer_norm(v[:, lo:hi], gmg_ref[:, lo:hi], gmb_ref[:, lo:hi]).astype(jnp.bfloat16)
        wm = jnp.where(allowed, ws_ref[hd], 0.0).astype(jnp.bfloat16)
        for c in range(tb // GM_CHUNK):
            r0, r1 = c * GM_CHUNK, (c + 1) * GM_CHUNK
            f = jnp.dot(wm, vln[r0:r1], preferred_element_type=jnp.float32) + bs_ref[hd]
            outa_ref[r0:r1, lo:hi] = (u[r0:r1, lo:hi] * f).astype(jnp.bfloat16)

    ql = _rms_norm(z[:, O_Q:O_KV], qg_ref[...]).astype(jnp.bfloat16)
    qf = jnp.dot(ql, wuq_ref[...], preferred_element_type=jnp.float32)
    rot_s = rot * QK_SCALE
    q_parts = []
    for hd in range(MLA_HEADS):
        base = hd * 2 * LANES
        q_parts.append(qf[:, base:base + LANES] * QK_SCALE)
        q_parts.append(qf[:, base + LANES:base + 2 * LANES] * rot_s)
    q_ref[...] = jnp.concatenate(q_parts, axis=-1).astype(jnp.bfloat16)

    kvl = _rms_norm(z[:, O_KV:O_KR], kvg_ref[...]).astype(jnp.bfloat16)
    kv = jnp.dot(kvl, wukv_ref[...], preferred_element_type=jnp.float32)
    t = z[:, O_KR:O_KR + LANES] * rot
    krr = t + pltpu.roll(t, 2 * QK_ROPE_DIM // 2, axis=1)
    k_parts, v_parts = [], []
    for hd in range(MLA_HEADS):
        base = hd * 2 * LANES
        k_parts.append(kv[:, base:base + LANES])
        k_parts.append(krr)
        v_parts.append(kv[:, base + LANES:base + 2 * LANES])
    k_ref[...] = jnp.concatenate(k_parts, axis=-1).astype(jnp.bfloat16)
    v_ref[...] = jnp.concatenate(v_parts, axis=-1).astype(jnp.bfloat16)


def _prep(x2, pos2, ln0g, ln0b, win, gmg, gmb, ws, bs, qg, wuq, kvg, wukv, freq, phase, sign):
    tb = PREP_TOKENS
    full = lambda shape: pl.BlockSpec(shape, lambda i: (0,) * len(shape))
    tok = lambda cols: pl.BlockSpec((tb, cols), lambda i: (i, 0))
    return pl.pallas_call(
        _prep_kernel,
        grid=(N_TOK // tb,),
        in_specs=[pl.BlockSpec(memory_space=pl.ANY), pl.BlockSpec((1, 1, tb), lambda i: (i, 0, 0)),
                  full((1, D_MODEL)), full((1, D_MODEL)),
                  full((D_MODEL, IN_COLS)), full((1, GM_WIDTH)), full((1, GM_WIDTH)),
                  full((GM_HEADS, GM_CHUNK, GM_CHUNK)), full((GM_HEADS, GM_CHUNK, GM_HEAD_DIM)),
                  full((1, Q_LORA_RANK)), full((Q_LORA_RANK, D_MODEL)),
                  full((1, KV_LORA_RANK)), full((KV_LORA_RANK, D_MODEL)),
                  full((1, LANES)), full((1, LANES)), full((1, LANES))],
        out_specs=[tok(D_MODEL), tok(GM_WIDTH), tok(D_MODEL), tok(D_MODEL), tok(GM_WIDTH)],
        out_shape=[jax.ShapeDtypeStruct((N_TOK, D_MODEL), jnp.float32),
                   jax.ShapeDtypeStruct((N_TOK, GM_WIDTH), jnp.bfloat16),
                   jax.ShapeDtypeStruct((N_TOK, D_MODEL), jnp.bfloat16),
                   jax.ShapeDtypeStruct((N_TOK, D_MODEL), jnp.bfloat16),
                   jax.ShapeDtypeStruct((N_TOK, GM_WIDTH), jnp.bfloat16)],
        scratch_shapes=[pltpu.VMEM((tb, LANES), jnp.float32)] * 3 + _ring_scratch(tb, D_MODEL, jnp.float32),
        compiler_params=pltpu.CompilerParams(
            dimension_semantics=("arbitrary",), vmem_limit_bytes=VMEM_LIMIT),
        name="prep",
    )(x2, pos2, ln0g, ln0b, win, gmg, gmb, ws, bs, qg, wuq, kvg, wukv, freq, phase, sign)


def _attn_kernel(q_ref, k_ref, v_ref, o_ref, s_ref, mx_ref, ls_ref, acc_ref):
    qi = pl.program_id(1)
    tq = q_ref.shape[0]
    nt = (((1,), (1,)), ((), ()))
    n_kv = SEQ // ATT_K

    half = tq // 2
    row_chunk = (lax.broadcasted_iota(jnp.int32, (half, half), 0)) // CHUNK
    col_chunk = (lax.broadcasted_iota(jnp.int32, (half, half), 1)) // CHUNK
    quad_allowed = col_chunk <= row_chunk
    diag_start = pl.multiple_of(qi * ATT_K, ATT_K)
    masked = jnp.float32(-1e30)

    def lane_tiles(x):
        return [x[:, t * LANES:(t + 1) * LANES] for t in range(x.shape[1] // LANES)]

    def lane_fold(x, op):
        return functools.reduce(op, lane_tiles(x))

    def scores(hd, j):
        start = pl.multiple_of(j * ATT_K, ATT_K)
        q_h = q_ref[:, hd * 2 * LANES:(hd + 1) * 2 * LANES]
        kb = k_ref[pl.ds(start, ATT_K), hd * 2 * LANES:(hd + 1) * 2 * LANES]
        return lax.dot_general(q_h, kb, nt, preferred_element_type=jnp.float32)

    for hd in range(MLA_HEADS):
        cols = slice(hd * 2 * LANES, (hd + 1) * 2 * LANES)
        k_lo = k_ref[pl.ds(diag_start, half), cols]
        k_hi = k_ref[pl.ds(diag_start + half, half), cols]
        qk = lambda q, k: lax.dot_general(q, k, nt, preferred_element_type=jnp.float32)
        s_tl = jnp.where(quad_allowed, qk(q_ref[:half, cols], k_lo), masked)
        s_bl = qk(q_ref[half:, cols], k_lo)
        s_br = jnp.where(quad_allowed, qk(q_ref[half:, cols], k_hi), masked)
        diag = s_ref.at[hd * n_kv + qi]
        diag[:half, :half] = s_tl
        diag[half:, :half] = s_bl
        diag[half:, half:] = s_br
        mx_ref[hd, :half] = lane_fold(s_tl, jnp.maximum)
        mx_ref[hd, half:] = jnp.maximum(lane_fold(s_bl, jnp.maximum), lane_fold(s_br, jnp.maximum))

    def pass_a(j, c):
        for hd in range(MLA_HEADS):
            s = scores(hd, j)
            s_ref[hd * n_kv + j] = s
            mx_ref[hd] = jnp.maximum(mx_ref[hd], lane_fold(s, jnp.maximum))
        return c

    lax.fori_loop(0, qi, pass_a, 0)

    for hd in range(MLA_HEADS):
        mx_ref[hd] = jnp.broadcast_to(jnp.max(mx_ref[hd], axis=-1, keepdims=True), (tq, LANES))

    def probs(hd, j):
        s = s_ref[hd * n_kv + j]
        mb = mx_ref[hd]
        p = jnp.exp2(jnp.concatenate([t - mb for t in lane_tiles(s)], axis=-1))
        start = pl.multiple_of(j * ATT_K, ATT_K)
        vb = v_ref[pl.ds(start, ATT_K), hd * LANES:(hd + 1) * LANES]
        pv = jnp.dot(p.astype(jnp.bfloat16), vb, preferred_element_type=jnp.float32)
        return lane_fold(p, jnp.add), pv

    for hd in range(MLA_HEADS):
        diag = s_ref.at[hd * n_kv + qi]
        vcols = slice(hd * LANES, (hd + 1) * LANES)
        p_top = jnp.exp2(jnp.concatenate([t - mx_ref[hd, :half] for t in lane_tiles(diag[:half, :half])], axis=-1))
        p_bot = jnp.exp2(jnp.concatenate([t - mx_ref[hd, half:] for t in lane_tiles(diag[half:, :])], axis=-1))
        ls_ref[hd, :half] = lane_fold(p_top, jnp.add)
        ls_ref[hd, half:] = lane_fold(p_bot, jnp.add)
        acc_ref[hd, :half] = jnp.dot(p_top.astype(jnp.bfloat16), v_ref[pl.ds(diag_start, half), vcols],
                                     preferred_element_type=jnp.float32)
        acc_ref[hd, half:] = jnp.dot(p_bot.astype(jnp.bfloat16), v_ref[pl.ds(diag_start, ATT_K), vcols],
                                     preferred_element_type=jnp.float32)

    def pass_b(j, c):
        for hd in range(MLA_HEADS):
            ls, pv = probs(hd, j)
            ls_ref[hd] = ls_ref[hd] + ls
            acc_ref[hd] = acc_ref[hd] + pv
        return c

    lax.fori_loop(0, qi, pass_b, 0)

    for hd in range(MLA_HEADS):
        l = jnp.sum(ls_ref[hd], axis=-1, keepdims=True)
        o_ref[:, hd * V_HEAD_DIM:(hd + 1) * V_HEAD_DIM] = (acc_ref[hd] / l).astype(jnp.bfloat16)


def _attn(q, k, v):
    tq = ATT_Q
    tokblk = lambda cols: pl.BlockSpec((tq, cols), lambda b, i: (b * N_QBLK + i, 0))
    seqblk = lambda cols: pl.BlockSpec((SEQ, cols), lambda b, i: (b, 0))
    return pl.pallas_call(
        _attn_kernel,
        grid=(BATCH, N_QBLK),
        in_specs=[tokblk(D_MODEL), seqblk(D_MODEL), seqblk(GM_WIDTH)],
        out_specs=tokblk(GM_WIDTH),
        out_shape=jax.ShapeDtypeStruct((N_TOK, MLA_HEADS * V_HEAD_DIM), jnp.bfloat16),
        scratch_shapes=[pltpu.VMEM((MLA_HEADS * (SEQ // ATT_K), tq, ATT_K), jnp.float32),
                        pltpu.VMEM((MLA_HEADS, tq, LANES), jnp.float32),
                        pltpu.VMEM((MLA_HEADS, tq, LANES), jnp.float32),
                        pltpu.VMEM((MLA_HEADS, tq, V_HEAD_DIM), jnp.float32)],
        compiler_params=pltpu.CompilerParams(
            dimension_semantics=("arbitrary", "arbitrary"), vmem_limit_bytes=VMEM_LIMIT),
        name="attn",
    )(q, k, v)


def _proj_kernel(outa_ref, ob_ref, h_hbm_ref, wout_ref, ln1g_ref, ln1b_ref, wr_ref, br_ref,
                 h1_ref, h1b_ref, ri_ref, rf_ref, proj_ref, hring_ref, hsems):
    i = pl.program_id(0)
    tb = outa_ref.shape[0]
    h_ref = _ring_block(h_hbm_ref, hring_ref, hsems, lambda step: jnp.maximum(step - 1, 0), tb)

    @pl.when(i == 0)
    def _():
        proj_ref[...] = jnp.zeros_like(proj_ref)

    h1 = _layer_norm(ALPHA * h_ref[...] + proj_ref[(i + 1) % 2], ln1g_ref[...], ln1b_ref[...])
    h1_ref[...] = h1
    h1b_ref[...] = h1.astype(jnp.bfloat16)

    logits_tm = jnp.dot(h1b_ref[...], wr_ref[...], preferred_element_type=jnp.float32)
    logits = logits_tm.T[0:ROUTER_ROWS] + br_ref[...]

    sub_i = lax.broadcasted_iota(jnp.int32, (SUBLANES, tb), 0)
    sub = sub_i.astype(jnp.float32)
    neg = jnp.float32(-jnp.inf)
    g = jnp.where(sub_i < N_GROUPS, logits[0:SUBLANES], neg)
    gmax = jnp.max(g, axis=0, keepdims=True)
    g_top = jnp.min(jnp.where(g == gmax, sub, float(SUBLANES)), axis=0, keepdims=True)
    p_group = 1.0 / jnp.sum(jnp.exp(g - gmax), axis=0, keepdims=True)
    sel = logits[SUBLANES:2 * SUBLANES]
    for grp in range(1, N_GROUPS):
        sel = jnp.where(g_top == float(grp), logits[(grp + 1) * SUBLANES:(grp + 2) * SUBLANES], sel)
    v1 = jnp.max(sel, axis=0, keepdims=True)
    i1 = jnp.min(jnp.where(sel == v1, sub, float(SUBLANES)), axis=0, keepdims=True)
    sel2 = jnp.where(sub == i1, neg, sel)
    v2 = jnp.max(sel2, axis=0, keepdims=True)
    i2 = jnp.min(jnp.where(sel2 == v2, sub, float(SUBLANES)), axis=0, keepdims=True)
    e21 = jnp.exp(v2 - v1)
    w1 = 1.0 / (1.0 + e21)
    gate1 = p_group * w1
    gate2 = p_group * (e21 * w1)
    e1 = g_top * EXPERTS_PER_GROUP + i1
    e2 = g_top * EXPERTS_PER_GROUP + i2
    ri_ref[...] = jnp.where(sub_i == 0, e1, jnp.where(sub_i == 1, e2, 0.0)).astype(jnp.int32)
    rf_ref[...] = jnp.where(sub_i == 0, gate1, jnp.where(sub_i == 1, gate2, 0.0))

    proj_ref[i % 2] = (jnp.dot(outa_ref[...], wout_ref[:GM_WIDTH, :], preferred_element_type=jnp.float32)
                       + jnp.dot(ob_ref[...], wout_ref[GM_WIDTH:, :], preferred_element_type=jnp.float32))


def _proj(outa, ob, h, wout, ln1g, ln1b, wr, br):
    tb = PROJ_TOKENS
    n_blk = N_TOK // tb
    cur = lambda i: jnp.minimum(i, n_blk - 1)
    prev = lambda i: jnp.maximum(i - 1, 0)
    full = lambda shape: pl.BlockSpec(shape, lambda i: (0,) * len(shape))
    return pl.pallas_call(
        _proj_kernel,
        grid=(n_blk + 1,),
        in_specs=[pl.BlockSpec((tb, GM_WIDTH), lambda i: (cur(i), 0)),
                  pl.BlockSpec((tb, GM_WIDTH), lambda i: (cur(i), 0)),
                  pl.BlockSpec(memory_space=pl.ANY),
                  full((D_MODEL, D_MODEL)), full((1, D_MODEL)), full((1, D_MODEL)),
                  full((D_MODEL, LANES)), full((ROUTER_ROWS, 1))],
        out_specs=[pl.BlockSpec((tb, D_MODEL), lambda i: (prev(i), 0)),
                   pl.BlockSpec((tb, D_MODEL), lambda i: (prev(i), 0)),
                   pl.BlockSpec((SUBLANES, tb), lambda i: (0, prev(i))),
                   pl.BlockSpec((SUBLANES, tb), lambda i: (0, prev(i)))],
        out_shape=[jax.ShapeDtypeStruct((N_TOK, D_MODEL), jnp.float32),
                   jax.ShapeDtypeStruct((N_TOK, D_MODEL), jnp.bfloat16),
                   jax.ShapeDtypeStruct((SUBLANES, N_TOK), jnp.int32),
                   jax.ShapeDtypeStruct((SUBLANES, N_TOK), jnp.float32)],
        scratch_shapes=[pltpu.VMEM((2, tb, D_MODEL), jnp.float32)] + _ring_scratch(tb, D_MODEL, jnp.float32),
        compiler_params=pltpu.CompilerParams(
            dimension_semantics=("arbitrary",), vmem_limit_bytes=VMEM_LIMIT),
        name="proj",
    )(outa, ob, h, wout, ln1g, ln1b, wr, br)


def _plan_kernel(ri_all_ref, ri_ref, ldest_ref, runs_ref, meta_ref, run_ref, start_ref, upper_ref):
    step = pl.program_id(0)
    tb = ri_ref.shape[1]
    f32 = jnp.float32
    er = lax.broadcasted_iota(jnp.int32, (N_EXPERTS, LANES), 0)
    ec = lax.broadcasted_iota(jnp.int32, (N_EXPERTS, LANES), 1)
    to_row = lambda col: jnp.sum(jnp.where(er == ec, col, 0.0), axis=0, keepdims=True)

    def expert_one_hot(ref):
        e_sub = lax.broadcasted_iota(jnp.int32, (N_EXPERTS, ref.shape[1]), 0)
        return e_sub == ref[0:1, :], e_sub == ref[1:2, :]

    @pl.when(step == 0)
    def _():
        oh1, oh2 = expert_one_hot(ri_all_ref)
        counts = jnp.sum(jnp.where(oh1 | oh2, 1.0, 0.0), axis=1, keepdims=True)
        padded = jnp.floor((counts + (EXPERT_ROWS - 1)) * (1.0 / EXPERT_ROWS)) * EXPERT_ROWS
        pad_end = jnp.sum(jnp.where(ec <= er, to_row(padded), 0.0), axis=1, keepdims=True)
        start_ref[...] = jnp.broadcast_to(pad_end - padded, start_ref.shape)
        run_ref[...] = jnp.zeros_like(run_ref)
        bstart = (lax.broadcasted_iota(jnp.int32, (N_EXPERTS, META_LANES), 1) * EXPERT_ROWS).astype(f32)
        blk_e = jnp.sum(jnp.where(pad_end <= bstart, 1.0, 0.0), axis=0, keepdims=True)
        blk_e = jnp.minimum(blk_e, N_EXPERTS - 1.0)
        n_used = pad_end[N_EXPERTS - 1:N_EXPERTS, :] * (1.0 / EXPERT_ROWS)
        pad3 = lambda r: jnp.concatenate(
            [r, jnp.zeros((1, META_LANES - LANES), f32)], axis=1)
        msub = lax.broadcasted_iota(jnp.int32, (SUBLANES, META_LANES), 0)
        meta = jnp.where(msub == 0, blk_e,
                         jnp.where(msub == 1, pad3(to_row(pad_end)),
                                   jnp.where(msub == 2, pad3(to_row(counts)),
                                             jnp.where(msub == 3, n_used, 0.0))))
        meta_ref[...] = meta.astype(jnp.int32)
        tr = lax.broadcasted_iota(jnp.int32, (tb, tb), 0)
        tc = lax.broadcasted_iota(jnp.int32, (tb, tb), 1)
        upper_ref[...] = jnp.where(tr < tc, 1.0, 0.0).astype(jnp.bfloat16)

    @pl.when(step > 0)
    def _():
        oh1, oh2 = expert_one_hot(ri_ref)
        oh = jnp.where(oh1 | oh2, 1.0, 0.0).astype(f32)
        blk_count = jnp.sum(oh, axis=1, keepdims=True)
        prefix = jnp.dot(oh.astype(jnp.bfloat16), upper_ref[...], preferred_element_type=f32)
        cnt_row = to_row(blk_count)
        lstart = jnp.sum(jnp.where(ec < er, cnt_row, 0.0), axis=1, keepdims=True)
        base = prefix + lstart
        d1 = jnp.sum(jnp.where(oh1, base, 0.0), axis=0, keepdims=True)
        d2 = jnp.sum(jnp.where(oh2, base, 0.0), axis=0, keepdims=True)
        sub = lax.broadcasted_iota(jnp.int32, (SUBLANES, tb), 0)
        ldest_ref[...] = jnp.where(sub == 0, d1, jnp.where(sub == 1, d2, 0.0)).astype(jnp.int32)
        gstart = start_ref[:, 0:1] + run_ref[:, 0:1]
        rsub = lax.broadcasted_iota(jnp.int32, (SUBLANES, LANES), 0)
        runs = jnp.where(rsub == 0, cnt_row,
                         jnp.where(rsub == 1, to_row(lstart), jnp.where(rsub == 2, to_row(gstart), 0.0)))
        runs_ref[...] = runs.astype(jnp.int32)
        run_ref[...] = run_ref[...] + blk_count


def _plan(ri):
    tb = MOE_TOKENS
    blk = lambda i: jnp.maximum(i - 1, 0)
    return pl.pallas_call(
        _plan_kernel,
        grid=(N_MOE_BLOCKS + 1,),
        in_specs=[pl.BlockSpec((SUBLANES, N_TOK), lambda i: (0, 0)),
                  pl.BlockSpec((SUBLANES, tb), lambda i: (0, blk(i)))],
        out_specs=[pl.BlockSpec((SUBLANES, tb), lambda i: (0, blk(i))),
                   pl.BlockSpec((SUBLANES, LANES), lambda i: (blk(i), 0)),
                   pl.BlockSpec((SUBLANES, META_LANES), lambda i: (0, 0))],
        out_shape=[jax.ShapeDtypeStruct((SUBLANES, N_TOK), jnp.int32),
                   jax.ShapeDtypeStruct((N_MOE_BLOCKS * SUBLANES, LANES), jnp.int32),
                   jax.ShapeDtypeStruct((SUBLANES, META_LANES), jnp.int32)],
        scratch_shapes=[pltpu.VMEM((N_EXPERTS, LANES), jnp.float32),
                        pltpu.VMEM((N_EXPERTS, LANES), jnp.float32),
                        pltpu.VMEM((tb, tb), jnp.bfloat16)],
        compiler_params=pltpu.CompilerParams(
            dimension_semantics=("arbitrary",), vmem_limit_bytes=VMEM_LIMIT),
        name="plan",
    )(ri, ri)


def _for_each_run_piece(runs_ref, fn):
    for e in range(N_EXPERTS):
        n, lstart, gstart = runs_ref[0, e], runs_ref[1, e], runs_ref[2, e]
        for bit in range(RUN_BITS):
            @pl.when((n & (1 << bit)) != 0)
            def _(n=n, lstart=lstart, gstart=gstart, bit=bit):
                off = (n >> (bit + 1)) << (bit + 1)
                fn(lstart + off, gstart + off, 1 << bit)


def _tile_rows(ref, row, rows):
    return ref.at[pl.ds(pl.multiple_of(row * FEAT_TILES, FEAT_TILES), rows * FEAT_TILES)]


def _dispatch_kernel(meta_ref, runs_ref, ldest_ref, h1b_hbm_ref, buf_ref, sorted_ref, zero_ref, sems, zsem,
                     hring_ref, hsems):
    i = pl.program_id(0)
    n_steps = pl.num_programs(0)
    tb = ldest_ref.shape[1]
    h1b_ref = _ring_block(h1b_hbm_ref, hring_ref, hsems, lambda step: step, tb)
    slot = i % 2
    block_tiles = 2 * tb * FEAT_TILES

    def wait_slot(s):
        pltpu.make_async_copy(sorted_ref.at[s], buf_ref.at[pl.ds(0, block_tiles)], sems.at[s]).wait()

    @pl.when(i == 0)
    def _():
        zero_ref[...] = jnp.zeros_like(zero_ref)

        def zero_copy(e):
            start = pl.multiple_of((meta_ref[1, e] - EXPERT_ROWS) * FEAT_TILES, EXPERT_ROWS * FEAT_TILES)
            return pltpu.make_async_copy(
                zero_ref, buf_ref.at[pl.ds(start, EXPERT_ROWS * FEAT_TILES)], zsem)

        def start_zero(e, c):
            @pl.when(meta_ref[2, e] > 0)
            def _():
                zero_copy(e).start()
            return c

        def wait_zero(e, c):
            @pl.when(meta_ref[2, e] > 0)
            def _():
                zero_copy(e).wait()
            return c

        def tail_copy(b):
            start = pl.multiple_of(b * (EXPERT_ROWS * FEAT_TILES), EXPERT_ROWS * FEAT_TILES)
            return pltpu.make_async_copy(
                zero_ref, buf_ref.at[pl.ds(start, EXPERT_ROWS * FEAT_TILES)], zsem)

        def start_tail(b, c):
            tail_copy(b).start()
            return c

        def wait_tail(b, c):
            tail_copy(b).wait()
            return c

        lax.fori_loop(0, N_EXPERTS, start_zero, 0)
        lax.fori_loop(meta_ref[3, 0], N_ROW_BLOCKS, start_tail, 0)
        lax.fori_loop(0, N_EXPERTS, wait_zero, 0)
        lax.fori_loop(meta_ref[3, 0], N_ROW_BLOCKS, wait_tail, 0)

    @pl.when(i >= 2)
    def _():
        wait_slot(slot)

    x = h1b_ref[...]
    ld0 = ldest_ref[0:1, :]
    ld1 = ldest_ref[1:2, :]
    for c in range(2 * tb // SORT_CHUNK):
        r = lax.broadcasted_iota(jnp.int32, (SORT_CHUNK, tb), 0) + c * SORT_CHUNK
        perm = jnp.where((r == ld0) | (r == ld1), 1.0, 0.0).astype(jnp.bfloat16)
        rows = jnp.dot(perm, x, preferred_element_type=jnp.float32)
        _to_row_tiles(sorted_ref.at[slot, pl.ds(c * SORT_CHUNK * FEAT_TILES, SORT_CHUNK * FEAT_TILES)], rows)

    def send(lrow, grow, rows):
        pltpu.make_async_copy(_tile_rows(sorted_ref.at[slot], lrow, rows),
                              _tile_rows(buf_ref, grow, rows), sems.at[slot]).start()

    _for_each_run_piece(runs_ref, send)

    @pl.when(i == n_steps - 1)
    def _():
        wait_slot(slot)
        wait_slot(1 - slot)


def _dispatch(meta, runs, ldest, h1b):
    tb = MOE_TOKENS
    return pl.pallas_call(
        _dispatch_kernel,
        grid_spec=pltpu.PrefetchScalarGridSpec(
            num_scalar_prefetch=1,
            grid=(N_MOE_BLOCKS,),
            in_specs=[pl.BlockSpec((SUBLANES, LANES), lambda i, m: (i, 0), memory_space=pltpu.SMEM),
                      pl.BlockSpec((SUBLANES, tb), lambda i, m: (0, i)),
                      pl.BlockSpec(memory_space=pl.ANY)],
            out_specs=pl.BlockSpec(memory_space=pl.ANY),
            scratch_shapes=[pltpu.VMEM((2, 2 * tb * FEAT_TILES, LANES), jnp.float32),
                            pltpu.VMEM((EXPERT_ROWS * FEAT_TILES, LANES), jnp.float32),
                            pltpu.SemaphoreType.DMA((2,)), pltpu.SemaphoreType.DMA]
                           + _ring_scratch(tb, D_MODEL, jnp.bfloat16)),
        out_shape=jax.ShapeDtypeStruct((N_ROWS * FEAT_TILES, LANES), jnp.float32),
        compiler_params=pltpu.CompilerParams(
            dimension_semantics=("arbitrary",), vmem_limit_bytes=VMEM_LIMIT),
        name="dispatch",
    )(meta, runs, ldest, h1b)


def _sub_block_expert(meta, i, sub):
    return meta[0, jnp.minimum(i + sub * (N_ROW_BLOCKS // EXPERT_SUB), meta[3, 0] - 1)]


def _experts_kernel(meta_ref, x_hbm_ref, wg0_ref, wu0_ref, wd0_ref, wg1_ref, wu1_ref, wd1_ref, o_ref,
                    wgb_ref, wub_ref, wdb_ref, cached_ref, xring_ref, xsems):
    n = EXPERT_ROWS
    i = pl.program_id(0)
    n_steps = pl.num_programs(0)
    weights = ((wg0_ref, wu0_ref, wd0_ref), (wg1_ref, wu1_ref, wd1_ref))

    def rows_copy(step):
        slot = step % EXPERT_RING
        start = pl.multiple_of(step * (n * FEAT_TILES), n * FEAT_TILES)
        return pltpu.make_async_copy(x_hbm_ref.at[:, pl.ds(start, n * FEAT_TILES), :], xring_ref.at[slot],
                                     xsems.at[slot])

    @pl.when(i == 0)
    def _():
        for step in range(EXPERT_RING - 1):
            rows_copy(step).start()

    @pl.when(i + (EXPERT_RING - 1) < n_steps)
    def _():
        rows_copy(i + (EXPERT_RING - 1)).start()

    rows_copy(i).wait()
    x_ref = xring_ref.at[i % EXPERT_RING]

    @pl.when(i == 0)
    def _():
        for sub in range(EXPERT_SUB):
            cached_ref[sub] = -1

    for sub in range(EXPERT_SUB):
        expert = _sub_block_expert(meta_ref, i, sub)

        @pl.when(cached_ref[sub] != expert)
        def _(sub=sub, expert=expert):
            wg_ref, wu_ref, wd_ref = weights[sub]
            wgb_ref[sub] = wg_ref[...].astype(jnp.bfloat16)
            wub_ref[sub] = wu_ref[...].astype(jnp.bfloat16)
            wdb_ref[sub] = wd_ref[...].astype(jnp.bfloat16)
            cached_ref[sub] = expert

    for sub in range(EXPERT_SUB):
        x = _from_row_tiles(x_ref.at[sub], n).astype(jnp.bfloat16)
        gate = jnp.dot(x, wgb_ref[sub], preferred_element_type=jnp.float32)
        up = jnp.dot(x, wub_ref[sub], preferred_element_type=jnp.float32)
        act = (gate * jax.nn.sigmoid(gate) * up).astype(jnp.bfloat16)
        _to_row_tiles(o_ref.at[sub], jnp.dot(act, wdb_ref[sub], preferred_element_type=jnp.float32))


def _experts(meta, buf, wg, wu, wd):
    sub_tiles = (N_ROW_BLOCKS // EXPERT_SUB) * EXPERT_ROWS * FEAT_TILES

    def weight_spec(shape, sub):
        return pl.BlockSpec((None,) + shape, lambda i, m: (_sub_block_expert(m, i, sub), 0, 0))

    rows_shape = (EXPERT_SUB, EXPERT_ROWS * FEAT_TILES, LANES)
    up_shape, down_shape = (D_MODEL, EXPERT_FF), (EXPERT_FF, D_MODEL)
    eout = pl.pallas_call(
        _experts_kernel,
        grid_spec=pltpu.PrefetchScalarGridSpec(
            num_scalar_prefetch=1,
            grid=(N_ROW_BLOCKS // EXPERT_SUB,),
            in_specs=[pl.BlockSpec(memory_space=pl.ANY),
                      weight_spec(up_shape, 0), weight_spec(up_shape, 0), weight_spec(down_shape, 0),
                      weight_spec(up_shape, 1), weight_spec(up_shape, 1), weight_spec(down_shape, 1)],
            out_specs=pl.BlockSpec(rows_shape, lambda i, m: (0, i, 0)),
            scratch_shapes=[pltpu.VMEM((EXPERT_SUB,) + up_shape, jnp.bfloat16),
                            pltpu.VMEM((EXPERT_SUB,) + up_shape, jnp.bfloat16),
                            pltpu.VMEM((EXPERT_SUB,) + down_shape, jnp.bfloat16),
                            pltpu.SMEM((EXPERT_SUB,), jnp.int32),
                            pltpu.VMEM((EXPERT_RING,) + rows_shape, jnp.float32),
                            pltpu.SemaphoreType.DMA((EXPERT_RING,))]),
        out_shape=jax.ShapeDtypeStruct((EXPERT_SUB, sub_tiles, LANES), jnp.float32),
        compiler_params=pltpu.CompilerParams(
            dimension_semantics=("arbitrary",), vmem_limit_bytes=VMEM_LIMIT),
        name="experts",
    )(meta, buf.reshape(EXPERT_SUB, sub_tiles, LANES), wg, wu, wd, wg, wu, wd)
    return eout.reshape(N_ROWS * FEAT_TILES, LANES)


def _combine_kernel(runs_ref, runs_next_ref, ldest_ref, rf_ref, h1_hbm_ref, eout_ref, ln2g_ref, ln2b_ref,
                    o_ref, y_ref, sems, hring_ref, hsems):
    i = pl.program_id(0)
    n_steps = pl.num_programs(0)
    tb = ldest_ref.shape[1]
    h1_ref = _ring_block(h1_hbm_ref, hring_ref, hsems, lambda step: step, tb)
    slot = i % 2
    block_tiles = 2 * tb * FEAT_TILES

    def fetch(table_ref, s):
        def recv(lrow, grow, rows):
            pltpu.make_async_copy(_tile_rows(eout_ref, grow, rows),
                                  _tile_rows(y_ref.at[s], lrow, rows), sems.at[s]).start()
        _for_each_run_piece(table_ref, recv)

    @pl.when(i == 0)
    def _():
        fetch(runs_ref, slot)

    @pl.when(i + 1 < n_steps)
    def _():
        fetch(runs_next_ref, 1 - slot)

    pltpu.make_async_copy(eout_ref.at[pl.ds(0, block_tiles)], y_ref.at[slot], sems.at[slot]).wait()

    ld = ldest_ref[...].astype(jnp.float32).T
    gates = rf_ref[...].T
    y = None
    col = lax.broadcasted_iota(jnp.int32, (tb, SORT_CHUNK), 1).astype(jnp.float32)
    for c in range(2 * tb // SORT_CHUNK):
        ld_c = ld - float(c * SORT_CHUNK)
        g = jnp.where(col == ld_c[:, 0:1], gates[:, 0:1],
                      jnp.where(col == ld_c[:, 1:2], gates[:, 1:2], 0.0)).astype(jnp.bfloat16)
        rows = _from_row_tiles(
            y_ref.at[slot, pl.ds(c * SORT_CHUNK * FEAT_TILES, SORT_CHUNK * FEAT_TILES)], SORT_CHUNK)
        part = jnp.dot(g, rows.astype(jnp.bfloat16), preferred_element_type=jnp.float32)
        y = part if y is None else y + part
    o_ref[...] = _layer_norm(ALPHA * h1_ref[...] + y, ln2g_ref[...], ln2b_ref[...])


def _combine(runs, ldest, rf, h1, eout, ln2g, ln2b):
    tb = MOE_TOKENS
    last = N_MOE_BLOCKS - 1
    return pl.pallas_call(
        _combine_kernel,
        grid=(N_MOE_BLOCKS,),
        in_specs=[pl.BlockSpec((SUBLANES, LANES), lambda i: (i, 0), memory_space=pltpu.SMEM),
                  pl.BlockSpec((SUBLANES, LANES), lambda i: (jnp.minimum(i + 1, last), 0),
                               memory_space=pltpu.SMEM),
                  pl.BlockSpec((SUBLANES, tb), lambda i: (0, i)),
                  pl.BlockSpec((SUBLANES, tb), lambda i: (0, i)),
                  pl.BlockSpec(memory_space=pl.ANY),
                  pl.BlockSpec(memory_space=pl.ANY),
                  pl.BlockSpec((1, D_MODEL), lambda i: (0, 0)),
                  pl.BlockSpec((1, D_MODEL), lambda i: (0, 0))],
        out_specs=pl.BlockSpec((tb, D_MODEL), lambda i: (i, 0)),
        out_shape=jax.ShapeDtypeStruct((N_TOK, D_MODEL), jnp.float32),
        scratch_shapes=[pltpu.VMEM((2, 2 * tb * FEAT_TILES, LANES), jnp.float32),
                        pltpu.SemaphoreType.DMA((2,))] + _ring_scratch(tb, D_MODEL, jnp.float32),
        compiler_params=pltpu.CompilerParams(
            dimension_semantics=("arbitrary",), vmem_limit_bytes=VMEM_LIMIT),
        name="combine",
    )(runs, runs, ldest, rf, h1, eout, ln2g, ln2b)


def _swap_halves(w):
    half = w.shape[-1] // 2
    return jnp.concatenate([w[..., half:], w[..., :half]], axis=-1)


def kernel(x, positions, ln0_g, ln0_b, w_in, gm_ln_g, gm_ln_b, w_spatial, b_spatial, q_norm_g, w_uq, kv_norm_g, w_ukv, w_out, ln1_g, ln1_b, w_router_group, b_router_group, w_router_expert, b_router_expert, w_gate, w_up, w_down, ln2_g, ln2_b):
    bf16 = jnp.bfloat16
    row = lambda a: a.reshape(1, -1)

    w_in0 = w_in[0]
    kr_cols = w_in0[:, O_KR:O_KR + QK_ROPE_DIM]
    win = jnp.concatenate([w_in0, _swap_halves(kr_cols)], axis=1).astype(bf16)
    wuq3 = w_uq[0].reshape(Q_LORA_RANK, MLA_HEADS, QK_NOPE_DIM + QK_ROPE_DIM)
    rope_cols = wuq3[:, :, QK_NOPE_DIM:]
    wuq = jnp.concatenate([wuq3, _swap_halves(rope_cols)], axis=-1).reshape(Q_LORA_RANK, D_MODEL).astype(bf16)
    wukv = w_ukv[0].astype(bf16)
    wout = w_out[0].astype(bf16)
    bs = jnp.broadcast_to(b_spatial[0][:, :, None], (GM_HEADS, GM_CHUNK, GM_HEAD_DIM))
    wr = jnp.concatenate([w_router_group[0], jnp.zeros((D_MODEL, SUBLANES - N_GROUPS), jnp.float32),
                          w_router_expert[0],
                          jnp.zeros((D_MODEL, LANES - ROUTER_ROWS), jnp.float32)],
                         axis=1).astype(bf16)
    br = jnp.concatenate([b_router_group[0], jnp.zeros((SUBLANES - N_GROUPS,), jnp.float32),
                          b_router_expert[0]]).reshape(ROUTER_ROWS, 1)

    inv_freq = ROPE_THETA ** (-jnp.arange(0, QK_ROPE_DIM, 2, dtype=jnp.float32) / QK_ROPE_DIM)
    freq = jnp.tile(inv_freq, 4).reshape(1, LANES)
    quarter = QK_ROPE_DIM // 2
    phase = jnp.concatenate([jnp.zeros((2 * quarter,), jnp.float32),
                             jnp.full((2 * quarter,), math.pi / 2, jnp.float32)]).reshape(1, LANES)
    sign = jnp.concatenate([jnp.ones((2 * quarter,), jnp.float32), -jnp.ones((quarter,), jnp.float32),
                            jnp.ones((quarter,), jnp.float32)]).reshape(1, LANES)

    x2 = x.reshape(N_TOK, D_MODEL)
    pos2 = positions.reshape(N_TOK // PREP_TOKENS, 1, PREP_TOKENS)

    h, outa, q, k, v = _prep(x2, pos2, row(ln0_g), row(ln0_b), win, row(gm_ln_g[0]), row(gm_ln_b[0]),
                             w_spatial[0], bs, row(q_norm_g[0]), wuq, row(kv_norm_g[0]), wukv,
                             freq, phase, sign)
    ob = _attn(q, k, v)
    h1, h1b, ri, rf = _proj(outa, ob, h, wout, row(ln1_g[0]), row(ln1_b[0]), wr, br)
    ldest, runs, meta = _plan(ri)
    buf = _dispatch(meta, runs, ldest, h1b)
    eout = _experts(meta, buf, w_gate[0], w_up[0], w_down[0])
    out = _combine(runs, ldest, rf, h1, eout, row(ln2_g[0]), row(ln2_b[0]))
    return out.reshape(BATCH, SEQ, D_MODEL)
```

```python
import functools
import math

import jax
import jax.numpy as jnp
from jax import lax
from jax.experimental import pallas as pl
from jax.experimental.pallas import tpu as pltpu

D_MODEL = 1024
BATCH = 16
SEQ = 2048
N_TOK = BATCH * SEQ
CHUNK = 64
GM_WIDTH = 512
GM_HEADS = 4
GM_HEAD_DIM = 128
GM_CHUNK = 128
MLA_HEADS = 4
QK_NOPE_DIM = 128
QK_ROPE_DIM = 64
V_HEAD_DIM = 128
Q_LORA_RANK = 384
KV_LORA_RANK = 256
ROPE_THETA = 10000.0
N_GROUPS = 4
EXPERTS_PER_GROUP = 8
N_EXPERTS = 32
TOP_K = 2
EXPERT_FF = 256
ALPHA = 2.0 ** 0.25
QK_SCALE = (QK_NOPE_DIM + QK_ROPE_DIM) ** -0.5 * math.log2(math.e)

LANES = 128
SUBLANES = 8
FEAT_TILES = D_MODEL // LANES
PREP_TOKENS = 512
ATT_Q = 512
ATT_K = 512
N_QBLK = SEQ // ATT_Q
PROJ_TOKENS = 512
MOE_TOKENS = 512
N_MOE_BLOCKS = N_TOK // MOE_TOKENS
RUN_BITS = (TOP_K * MOE_TOKENS).bit_length()
SORT_CHUNK = 256
EXPERT_ROWS = 256
EXPERT_SUB = 2
EXPERT_RING = 3
INPUT_RING = 3
N_ROWS = N_TOK * TOP_K + N_EXPERTS * EXPERT_ROWS
N_ROW_BLOCKS = N_ROWS // EXPERT_ROWS
META_LANES = 384
IN_COLS = 2 * GM_WIDTH + Q_LORA_RANK + KV_LORA_RANK + 2 * QK_ROPE_DIM
O_Q = 2 * GM_WIDTH
O_KV = O_Q + Q_LORA_RANK
O_KR = O_KV + KV_LORA_RANK
ROUTER_ROWS = 40
VMEM_LIMIT = 48 * 1024 * 1024

assert N_ROW_BLOCKS <= META_LANES and N_ROW_BLOCKS % EXPERT_SUB == 0


def _layer_norm(x, g, b, eps=1e-5):
    mu = jnp.mean(x, axis=-1, keepdims=True)
    xc = x - mu
    var = jnp.mean(xc * xc, axis=-1, keepdims=True)
    return xc * lax.rsqrt(var + eps) * g + b


def _rms_norm(x, g, eps=1e-6):
    return x * lax.rsqrt(jnp.mean(x * x, axis=-1, keepdims=True) + eps) * g


def _gelu_tanh(x):
    c = math.sqrt(2.0 / math.pi)
    return 0.5 * x * (1.0 + jnp.tanh(c * (x + 0.044715 * (x * x * x))))


def _to_row_tiles(ref, x):
    n = x.shape[0]
    for s in range(FEAT_TILES):
        ref[pl.ds(s, n, stride=FEAT_TILES), :] = x[:, s * LANES:(s + 1) * LANES]


def _from_row_tiles(ref, n):
    return jnp.concatenate(
        [ref[pl.ds(s, n, stride=FEAT_TILES), :] for s in range(FEAT_TILES)], axis=-1)


def _ring_block(hbm_ref, ring_ref, sems, block_of_step, rows):
    i = pl.program_id(0)
    n_steps = pl.num_programs(0)

    def copy(step):
        start = pl.multiple_of(block_of_step(step) * rows, rows)
        slot = step % INPUT_RING
        return pltpu.make_async_copy(hbm_ref.at[pl.ds(start, rows)], ring_ref.at[slot], sems.at[slot])

    @pl.when(i == 0)
    def _():
        for step in range(INPUT_RING - 1):
            copy(step).start()

    @pl.when(i + (INPUT_RING - 1) < n_steps)
    def _():
        copy(i + (INPUT_RING - 1)).start()

    copy(i).wait()
    return ring_ref.at[i % INPUT_RING]


def _ring_scratch(rows, cols, dtype):
    return [pltpu.VMEM((INPUT_RING, rows, cols), dtype), pltpu.SemaphoreType.DMA((INPUT_RING,))]


def _prep_kernel(x_ref, pos_ref, ln0g_ref, ln0b_ref, win_ref, gmg_ref, gmb_ref, ws_ref, bs_ref,
                 qg_ref, wuq_ref, kvg_ref, wukv_ref, freq_ref, phase_ref, sign_ref,
                 h_ref, outa_ref, q_ref, k_ref, v_ref, tabc_ref, tabs_ref, rot_ref):
    tb = x_ref.shape[0]

    @pl.when(pl.program_id(0) == 0)
    def _():
        d = lax.broadcasted_iota(jnp.int32, (tb, LANES), 0).astype(jnp.float32) * freq_ref[...]
        tabc_ref[...] = jnp.cos(d)
        tabs_ref[...] = jnp.sin(d)

    pos_row = pos_ref[0]
    p0 = pos_row[:, 0:1]
    offset = lax.broadcasted_iota(jnp.int32, (1, tb), 1)
    consecutive = jnp.max(jnp.abs((pos_row - p0 - offset).astype(jnp.float32))) == 0.0

    @pl.when(consecutive)
    def _():
        a0 = p0.astype(jnp.float32) * freq_ref[...]
        c0, s0 = jnp.cos(a0), jnp.sin(a0)
        lane = lax.broadcasted_iota(jnp.int32, (1, LANES), 1)
        coef_c = jnp.where(lane < 2 * 32, c0, jnp.where(lane < 3 * 32, -s0, s0))
        coef_s = jnp.where(lane < 2 * 32, -s0, jnp.where(lane < 3 * 32, -c0, c0))
        rot_ref[...] = coef_c * tabc_ref[...] + coef_s * tabs_ref[...]

    @pl.when(jnp.logical_not(consecutive))
    def _():
        pos_col = jnp.broadcast_to(pos_row.astype(jnp.float32), (SUBLANES, tb)).T[:, 0:1]
        ang = pos_col * freq_ref[...]
        rot_ref[...] = jnp.cos(ang - phase_ref[...]) * sign_ref[...]

    rot = rot_ref[...]

    h = _layer_norm(x_ref[...], ln0g_ref[...], ln0b_ref[...])
    h_ref[...] = h
    z = jnp.dot(h.astype(jnp.bfloat16), win_ref[...], preferred_element_type=jnp.float32)

    u = _gelu_tanh(z[:, :GM_WIDTH])
    v = _gelu_tanh(z[:, GM_WIDTH:2 * GM_WIDTH])
    row_chunk = lax.broadcasted_iota(jnp.int32, (GM_CHUNK, GM_CHUNK), 0) // CHUNK
    col_chunk = lax.broadcasted_iota(jnp.int32, (GM_CHUNK, GM_CHUNK), 1) // CHUNK
    allowed = col_chunk <= row_chunk
    for hd in range(GM_HEADS):
        lo, hi = hd * GM_HEAD_DIM, (hd + 1) * GM_HEAD_DIM
        vln = _layer_norm(v[:, lo:hi], gmg_ref[:, lo:hi], gmb_ref[:, lo:hi]).astype(jnp.bfloat16)
        wm = jnp.where(allowed, ws_ref[hd], 0.0).astype(jnp.bfloat16)
        for c in range(tb // GM_CHUNK):
            r0, r1 = c * GM_CHUNK, (c + 1) * GM_CHUNK
            f = jnp.dot(wm, vln[r0:r1], preferred_element_type=jnp.float32) + bs_ref[hd]
            outa_ref[r0:r1, lo:hi] = (u[r0:r1, lo:hi] * f).astype(jnp.bfloat16)

    ql = _rms_norm(z[:, O_Q:O_KV], qg_ref[...]).astype(jnp.bfloat16)
    qf = jnp.dot(ql, wuq_ref[...], preferred_element_type=jnp.float32)
    rot_s = rot * QK_SCALE
    q_parts = []
    for hd in range(MLA_HEADS):
        base = hd * 2 * LANES
        q_parts.append(qf[:, base:base + LANES] * QK_SCALE)
        q_parts.append(qf[:, base + LANES:base + 2 * LANES] * rot_s)
    q_ref[...] = jnp.concatenate(q_parts, axis=-1).astype(jnp.bfloat16)

    kvl = _rms_norm(z[:, O_KV:O_KR], kvg_ref[...]).astype(jnp.bfloat16)
    kv = jnp.dot(kvl, wukv_ref[...], preferred_element_type=jnp.float32)
    t = z[:, O_KR:O_KR + LANES] * rot
    krr = t + pltpu.roll(t, 2 * QK_ROPE_DIM // 2, axis=1)
    k_parts, v_parts = [], []
    for hd in range(MLA_HEADS):
        base = hd * 2 * LANES
        k_parts.append(kv[:, base:base + LANES])
        k_parts.append(krr)
        v_parts.append(kv[:, base + LANES:base + 2 * LANES])
    k_ref[...] = jnp.concatenate(k_parts, axis=-1).astype(jnp.bfloat16)
    v_ref[...] = jnp.concatenate(v_parts, axis=-1).astype(jnp.bfloat16)


def _prep(x2, pos2, ln0g, ln0b, win, gmg, gmb, ws, bs, qg, wuq, kvg, wukv, freq, phase, sign):
    tb = PREP_TOKENS
    full = lambda shape: pl.BlockSpec(shape, lambda i: (0,) * len(shape))
    tok = lambda cols: pl.BlockSpec((tb, cols), lambda i: (i, 0))
    return pl.pallas_call(
        _prep_kernel,
        grid=(N_TOK // tb,),
        in_specs=[tok(D_MODEL), pl.BlockSpec((1, 1, tb), lambda i: (i, 0, 0)), full((1, D_MODEL)), full((1, D_MODEL)),
                  full((D_MODEL, IN_COLS)), full((1, GM_WIDTH)), full((1, GM_WIDTH)),
                  full((GM_HEADS, GM_CHUNK, GM_CHUNK)), full((GM_HEADS, GM_CHUNK, GM_HEAD_DIM)),
                  full((1, Q_LORA_RANK)), full((Q_LORA_RANK, D_MODEL)),
                  full((1, KV_LORA_RANK)), full((KV_LORA_RANK, D_MODEL)),
                  full((1, LANES)), full((1, LANES)), full((1, LANES))],
        out_specs=[tok(D_MODEL), tok(GM_WIDTH), tok(D_MODEL), tok(D_MODEL), tok(GM_WIDTH)],
        out_shape=[jax.ShapeDtypeStruct((N_TOK, D_MODEL), jnp.float32),
                   jax.ShapeDtypeStruct((N_TOK, GM_WIDTH), jnp.bfloat16),
                   jax.ShapeDtypeStruct((N_TOK, D_MODEL), jnp.bfloat16),
                   jax.ShapeDtypeStruct((N_TOK, D_MODEL), jnp.bfloat16),
                   jax.ShapeDtypeStruct((N_TOK, GM_WIDTH), jnp.bfloat16)],
        scratch_shapes=[pltpu.VMEM((tb, LANES), jnp.float32)] * 3,
        compiler_params=pltpu.CompilerParams(
            dimension_semantics=("arbitrary",), vmem_limit_bytes=VMEM_LIMIT),
        name="prep",
    )(x2, pos2, ln0g, ln0b, win, gmg, gmb, ws, bs, qg, wuq, kvg, wukv, freq, phase, sign)


def _attn_kernel(q_ref, k_ref, v_ref, o_ref, s_ref, mx_ref, ls_ref, acc_ref):
    qi = pl.program_id(1)
    tq = q_ref.shape[0]
    nt = (((1,), (1,)), ((), ()))
    n_kv = SEQ // ATT_K

    half = tq // 2
    row_chunk = (lax.broadcasted_iota(jnp.int32, (half, half), 0)) // CHUNK
    col_chunk = (lax.broadcasted_iota(jnp.int32, (half, half), 1)) // CHUNK
    quad_allowed = col_chunk <= row_chunk
    diag_start = pl.multiple_of(qi * ATT_K, ATT_K)
    masked = jnp.float32(-1e30)

    def lane_tiles(x):
        return [x[:, t * LANES:(t + 1) * LANES] for t in range(x.shape[1] // LANES)]

    def lane_fold(x, op):
        return functools.reduce(op, lane_tiles(x))

    def scores(hd, j):
        start = pl.multiple_of(j * ATT_K, ATT_K)
        q_h = q_ref[:, hd * 2 * LANES:(hd + 1) * 2 * LANES]
        kb = k_ref[pl.ds(start, ATT_K), hd * 2 * LANES:(hd + 1) * 2 * LANES]
        return lax.dot_general(q_h, kb, nt, preferred_element_type=jnp.float32)

    for hd in range(MLA_HEADS):
        cols = slice(hd * 2 * LANES, (hd + 1) * 2 * LANES)
        k_lo = k_ref[pl.ds(diag_start, half), cols]
        k_hi = k_ref[pl.ds(diag_start + half, half), cols]
        qk = lambda q, k: lax.dot_general(q, k, nt, preferred_element_type=jnp.float32)
        s_tl = jnp.where(quad_allowed, qk(q_ref[:half, cols], k_lo), masked)
        s_bl = qk(q_ref[half:, cols], k_lo)
        s_br = jnp.where(quad_allowed, qk(q_ref[half:, cols], k_hi), masked)
        diag = s_ref.at[hd * n_kv + qi]
        diag[:half, :half] = s_tl
        diag[half:, :half] = s_bl
        diag[half:, half:] = s_br
        mx_ref[hd, :half] = lane_fold(s_tl, jnp.maximum)
        mx_ref[hd, half:] = jnp.maximum(lane_fold(s_bl, jnp.maximum), lane_fold(s_br, jnp.maximum))

    def pass_a(j, c):
        for hd in range(MLA_HEADS):
            s = scores(hd, j)
            s_ref[hd * n_kv + j] = s
            mx_ref[hd] = jnp.maximum(mx_ref[hd], lane_fold(s, jnp.maximum))
        return c

    lax.fori_loop(0, qi, pass_a, 0)

    for hd in range(MLA_HEADS):
        mx_ref[hd] = jnp.broadcast_to(jnp.max(mx_ref[hd], axis=-1, keepdims=True), (tq, LANES))

    def probs(hd, j):
        s = s_ref[hd * n_kv + j]
        mb = mx_ref[hd]
        p = jnp.exp2(jnp.concatenate([t - mb for t in lane_tiles(s)], axis=-1))
        start = pl.multiple_of(j * ATT_K, ATT_K)
        vb = v_ref[pl.ds(start, ATT_K), hd * LANES:(hd + 1) * LANES]
        pv = jnp.dot(p.astype(jnp.bfloat16), vb, preferred_element_type=jnp.float32)
        return lane_fold(p, jnp.add), pv

    for hd in range(MLA_HEADS):
        diag = s_ref.at[hd * n_kv + qi]
        vcols = slice(hd * LANES, (hd + 1) * LANES)
        p_top = jnp.exp2(jnp.concatenate([t - mx_ref[hd, :half] for t in lane_tiles(diag[:half, :half])], axis=-1))
        p_bot = jnp.exp2(jnp.concatenate([t - mx_ref[hd, half:] for t in lane_tiles(diag[half:, :])], axis=-1))
        ls_ref[hd, :half] = lane_fold(p_top, jnp.add)
        ls_ref[hd, half:] = lane_fold(p_bot, jnp.add)
        acc_ref[hd, :half] = jnp.dot(p_top.astype(jnp.bfloat16), v_ref[pl.ds(diag_start, half), vcols],
                                     preferred_element_type=jnp.float32)
        acc_ref[hd, half:] = jnp.dot(p_bot.astype(jnp.bfloat16), v_ref[pl.ds(diag_start, ATT_K), vcols],
                                     preferred_element_type=jnp.float32)

    def pass_b(j, c):
        for hd in range(MLA_HEADS):
            ls, pv = probs(hd, j)
            ls_ref[hd] = ls_ref[hd] + ls
            acc_ref[hd] = acc_ref[hd] + pv
        return c

    lax.fori_loop(0, qi, pass_b, 0)

    for hd in range(MLA_HEADS):
        l = jnp.sum(ls_ref[hd], axis=-1, keepdims=True)
        o_ref[:, hd * V_HEAD_DIM:(hd + 1) * V_HEAD_DIM] = (acc_ref[hd] / l).astype(jnp.bfloat16)


def _attn(q, k, v):
    tq = ATT_Q
    tokblk = lambda cols: pl.BlockSpec((tq, cols), lambda b, i: (b * N_QBLK + i, 0))
    seqblk = lambda cols: pl.BlockSpec((SEQ, cols), lambda b, i: (b, 0))
    return pl.pallas_call(
        _attn_kernel,
        grid=(BATCH, N_QBLK),
        in_specs=[tokblk(D_MODEL), seqblk(D_MODEL), seqblk(GM_WIDTH)],
        out_specs=tokblk(GM_WIDTH),
        out_shape=jax.ShapeDtypeStruct((N_TOK, MLA_HEADS * V_HEAD_DIM), jnp.bfloat16),
        scratch_shapes=[pltpu.VMEM((MLA_HEADS * (SEQ // ATT_K), tq, ATT_K), jnp.float32),
                        pltpu.VMEM((MLA_HEADS, tq, LANES), jnp.float32),
                        pltpu.VMEM((MLA_HEADS, tq, LANES), jnp.float32),
                        pltpu.VMEM((MLA_HEADS, tq, V_HEAD_DIM), jnp.float32)],
        compiler_params=pltpu.CompilerParams(
            dimension_semantics=("arbitrary", "arbitrary"), vmem_limit_bytes=VMEM_LIMIT),
        name="attn",
    )(q, k, v)


def _proj_kernel(outa_ref, ob_ref, h_hbm_ref, wout_ref, ln1g_ref, ln1b_ref, wr_ref, br_ref,
                 h1_ref, h1b_ref, ri_ref, rf_ref, proj_ref, hring_ref, hsems):
    i = pl.program_id(0)
    tb = outa_ref.shape[0]
    h_ref = _ring_block(h_hbm_ref, hring_ref, hsems, lambda step: jnp.maximum(step - 1, 0), tb)

    @pl.when(i == 0)
    def _():
        proj_ref[...] = jnp.zeros_like(proj_ref)

    h1 = _layer_norm(ALPHA * h_ref[...] + proj_ref[(i + 1) % 2], ln1g_ref[...], ln1b_ref[...])
    h1_ref[...] = h1
    h1b_ref[...] = h1.astype(jnp.bfloat16)

    logits_tm = jnp.dot(h1b_ref[...], wr_ref[...], preferred_element_type=jnp.float32)
    logits = logits_tm.T[0:ROUTER_ROWS] + br_ref[...]

    sub_i = lax.broadcasted_iota(jnp.int32, (SUBLANES, tb), 0)
    sub = sub_i.astype(jnp.float32)
    neg = jnp.float32(-jnp.inf)
    g = jnp.where(sub_i < N_GROUPS, logits[0:SUBLANES], neg)
    gmax = jnp.max(g, axis=0, keepdims=True)
    g_top = jnp.min(jnp.where(g == gmax, sub, float(SUBLANES)), axis=0, keepdims=True)
    p_group = 1.0 / jnp.sum(jnp.exp(g - gmax), axis=0, keepdims=True)
    sel = logits[SUBLANES:2 * SUBLANES]
    for grp in range(1, N_GROUPS):
        sel = jnp.where(g_top == float(grp), logits[(grp + 1) * SUBLANES:(grp + 2) * SUBLANES], sel)
    v1 = jnp.max(sel, axis=0, keepdims=True)
    i1 = jnp.min(jnp.where(sel == v1, sub, float(SUBLANES)), axis=0, keepdims=True)
    sel2 = jnp.where(sub == i1, neg, sel)
    v2 = jnp.max(sel2, axis=0, keepdims=True)
    i2 = jnp.min(jnp.where(sel2 == v2, sub, float(SUBLANES)), axis=0, keepdims=True)
    e21 = jnp.exp(v2 - v1)
    w1 = 1.0 / (1.0 + e21)
    gate1 = p_group * w1
    gate2 = p_group * (e21 * w1)
    e1 = g_top * EXPERTS_PER_GROUP + i1
    e2 = g_top * EXPERTS_PER_GROUP + i2
    ri_ref[...] = jnp.where(sub_i == 0, e1, jnp.where(sub_i == 1, e2, 0.0)).astype(jnp.int32)
    rf_ref[...] = jnp.where(sub_i == 0, gate1, jnp.where(sub_i == 1, gate2, 0.0))

    proj_ref[i % 2] = (jnp.dot(outa_ref[...], wout_ref[:GM_WIDTH, :], preferred_element_type=jnp.float32)
                       + jnp.dot(ob_ref[...], wout_ref[GM_WIDTH:, :], preferred_element_type=jnp.float32))


def _proj(outa, ob, h, wout, ln1g, ln1b, wr, br):
    tb = PROJ_TOKENS
    n_blk = N_TOK // tb
    cur = lambda i: jnp.minimum(i, n_blk - 1)
    prev = lambda i: jnp.maximum(i - 1, 0)
    full = lambda shape: pl.BlockSpec(shape, lambda i: (0,) * len(shape))
    return pl.pallas_call(
        _proj_kernel,
        grid=(n_blk + 1,),
        in_specs=[pl.BlockSpec((tb, GM_WIDTH), lambda i: (cur(i), 0)),
                  pl.BlockSpec((tb, GM_WIDTH), lambda i: (cur(i), 0)),
                  pl.BlockSpec(memory_space=pl.ANY),
                  full((D_MODEL, D_MODEL)), full((1, D_MODEL)), full((1, D_MODEL)),
                  full((D_MODEL, LANES)), full((ROUTER_ROWS, 1))],
        out_specs=[pl.BlockSpec((tb, D_MODEL), lambda i: (prev(i), 0)),
                   pl.BlockSpec((tb, D_MODEL), lambda i: (prev(i), 0)),
                   pl.BlockSpec((SUBLANES, tb), lambda i: (0, prev(i))),
                   pl.BlockSpec((SUBLANES, tb), lambda i: (0, prev(i)))],
        out_shape=[jax.ShapeDtypeStruct((N_TOK, D_MODEL), jnp.float32),
                   jax.ShapeDtypeStruct((N_TOK, D_MODEL), jnp.bfloat16),
                   jax.ShapeDtypeStruct((SUBLANES, N_TOK), jnp.int32),
                   jax.ShapeDtypeStruct((SUBLANES, N_TOK), jnp.float32)],
        scratch_shapes=[pltpu.VMEM((2, tb, D_MODEL), jnp.float32)] + _ring_scratch(tb, D_MODEL, jnp.float32),
        compiler_params=pltpu.CompilerParams(
            dimension_semantics=("arbitrary",), vmem_limit_bytes=VMEM_LIMIT),
        name="proj",
    )(outa, ob, h, wout, ln1g, ln1b, wr, br)


def _plan_kernel(ri_all_ref, ri_ref, ldest_ref, runs_ref, meta_ref, run_ref, start_ref, upper_ref):
    step = pl.program_id(0)
    tb = ri_ref.shape[1]
    f32 = jnp.float32
    er = lax.broadcasted_iota(jnp.int32, (N_EXPERTS, LANES), 0)
    ec = lax.broadcasted_iota(jnp.int32, (N_EXPERTS, LANES), 1)
    to_row = lambda col: jnp.sum(jnp.where(er == ec, col, 0.0), axis=0, keepdims=True)

    def expert_one_hot(ref):
        e_sub = lax.broadcasted_iota(jnp.int32, (N_EXPERTS, ref.shape[1]), 0)
        return e_sub == ref[0:1, :], e_sub == ref[1:2, :]

    @pl.when(step == 0)
    def _():
        oh1, oh2 = expert_one_hot(ri_all_ref)
        counts = jnp.sum(jnp.where(oh1 | oh2, 1.0, 0.0), axis=1, keepdims=True)
        padded = jnp.floor((counts + (EXPERT_ROWS - 1)) * (1.0 / EXPERT_ROWS)) * EXPERT_ROWS
        pad_end = jnp.sum(jnp.where(ec <= er, to_row(padded), 0.0), axis=1, keepdims=True)
        start_ref[...] = jnp.broadcast_to(pad_end - padded, start_ref.shape)
        run_ref[...] = jnp.zeros_like(run_ref)
        bstart = (lax.broadcasted_iota(jnp.int32, (N_EXPERTS, META_LANES), 1) * EXPERT_ROWS).astype(f32)
        blk_e = jnp.sum(jnp.where(pad_end <= bstart, 1.0, 0.0), axis=0, keepdims=True)
        blk_e = jnp.minimum(blk_e, N_EXPERTS - 1.0)
        n_used = pad_end[N_EXPERTS - 1:N_EXPERTS, :] * (1.0 / EXPERT_ROWS)
        pad3 = lambda r: jnp.concatenate(
            [r, jnp.zeros((1, META_LANES - LANES), f32)], axis=1)
        msub = lax.broadcasted_iota(jnp.int32, (SUBLANES, META_LANES), 0)
        meta = jnp.where(msub == 0, blk_e,
                         jnp.where(msub == 1, pad3(to_row(pad_end)),
                                   jnp.where(msub == 2, pad3(to_row(counts)),
                                             jnp.where(msub == 3, n_used, 0.0))))
        meta_ref[...] = meta.astype(jnp.int32)
        tr = lax.broadcasted_iota(jnp.int32, (tb, tb), 0)
        tc = lax.broadcasted_iota(jnp.int32, (tb, tb), 1)
        upper_ref[...] = jnp.where(tr < tc, 1.0, 0.0).astype(jnp.bfloat16)

    @pl.when(step > 0)
    def _():
        oh1, oh2 = expert_one_hot(ri_ref)
        oh = jnp.where(oh1 | oh2, 1.0, 0.0).astype(f32)
        blk_count = jnp.sum(oh, axis=1, keepdims=True)
        prefix = jnp.dot(oh.astype(jnp.bfloat16), upper_ref[...], preferred_element_type=f32)
        cnt_row = to_row(blk_count)
        lstart = jnp.sum(jnp.where(ec < er, cnt_row, 0.0), axis=1, keepdims=True)
        base = prefix + lstart
        d1 = jnp.sum(jnp.where(oh1, base, 0.0), axis=0, keepdims=True)
        d2 = jnp.sum(jnp.where(oh2, base, 0.0), axis=0, keepdims=True)
        sub = lax.broadcasted_iota(jnp.int32, (SUBLANES, tb), 0)
        ldest_ref[...] = jnp.where(sub == 0, d1, jnp.where(sub == 1, d2, 0.0)).astype(jnp.int32)
        gstart = start_ref[:, 0:1] + run_ref[:, 0:1]
        rsub = lax.broadcasted_iota(jnp.int32, (SUBLANES, LANES), 0)
        runs = jnp.where(rsub == 0, cnt_row,
                         jnp.where(rsub == 1, to_row(lstart), jnp.where(rsub == 2, to_row(gstart), 0.0)))
        runs_ref[...] = runs.astype(jnp.int32)
        run_ref[...] = run_ref[...] + blk_count


def _plan(ri):
    tb = MOE_TOKENS
    blk = lambda i: jnp.maximum(i - 1, 0)
    return pl.pallas_call(
        _plan_kernel,
        grid=(N_MOE_BLOCKS + 1,),
        in_specs=[pl.BlockSpec((SUBLANES, N_TOK), lambda i: (0, 0)),
                  pl.BlockSpec((SUBLANES, tb), lambda i: (0, blk(i)))],
        out_specs=[pl.BlockSpec((SUBLANES, tb), lambda i: (0, blk(i))),
                   pl.BlockSpec((SUBLANES, LANES), lambda i: (blk(i), 0)),
                   pl.BlockSpec((SUBLANES, META_LANES), lambda i: (0, 0))],
        out_shape=[jax.ShapeDtypeStruct((SUBLANES, N_TOK), jnp.int32),
                   jax.ShapeDtypeStruct((N_MOE_BLOCKS * SUBLANES, LANES), jnp.int32),
                   jax.ShapeDtypeStruct((SUBLANES, META_LANES), jnp.int32)],
        scratch_shapes=[pltpu.VMEM((N_EXPERTS, LANES), jnp.float32),
                        pltpu.VMEM((N_EXPERTS, LANES), jnp.float32),
                        pltpu.VMEM((tb, tb), jnp.bfloat16)],
        compiler_params=pltpu.CompilerParams(
            dimension_semantics=("arbitrary",), vmem_limit_bytes=VMEM_LIMIT),
        name="plan",
    )(ri, ri)


def _for_each_run_piece(runs_ref, fn):
    for e in range(N_EXPERTS):
        n, lstart, gstart = runs_ref[0, e], runs_ref[1, e], runs_ref[2, e]
        for bit in range(RUN_BITS):
            @pl.when((n & (1 << bit)) != 0)
            def _(n=n, lstart=lstart, gstart=gstart, bit=bit):
                off = (n >> (bit + 1)) << (bit + 1)
                fn(lstart + off, gstart + off, 1 << bit)


def _tile_rows(ref, row, rows):
    return ref.at[pl.ds(pl.multiple_of(row * FEAT_TILES, FEAT_TILES), rows * FEAT_TILES)]


def _dispatch_kernel(meta_ref, runs_ref, ldest_ref, h1b_ref, buf_ref, sorted_ref, zero_ref, sems, zsem):
    i = pl.program_id(0)
    n_steps = pl.num_programs(0)
    tb = ldest_ref.shape[1]
    slot = i % 2
    block_tiles = 2 * tb * FEAT_TILES

    def wait_slot(s):
        pltpu.make_async_copy(sorted_ref.at[s], buf_ref.at[pl.ds(0, block_tiles)], sems.at[s]).wait()

    @pl.when(i == 0)
    def _():
        zero_ref[...] = jnp.zeros_like(zero_ref)

        def zero_copy(e):
            start = pl.multiple_of((meta_ref[1, e] - EXPERT_ROWS) * FEAT_TILES, EXPERT_ROWS * FEAT_TILES)
            return pltpu.make_async_copy(
                zero_ref, buf_ref.at[pl.ds(start, EXPERT_ROWS * FEAT_TILES)], zsem)

        def start_zero(e, c):
            @pl.when(meta_ref[2, e] > 0)
            def _():
                zero_copy(e).start()
            return c

        def wait_zero(e, c):
            @pl.when(meta_ref[2, e] > 0)
            def _():
                zero_copy(e).wait()
            return c

        def tail_copy(b):
            start = pl.multiple_of(b * (EXPERT_ROWS * FEAT_TILES), EXPERT_ROWS * FEAT_TILES)
            return pltpu.make_async_copy(
                zero_ref, buf_ref.at[pl.ds(start, EXPERT_ROWS * FEAT_TILES)], zsem)

        def start_tail(b, c):
            tail_copy(b).start()
            return c

        def wait_tail(b, c):
            tail_copy(b).wait()
            return c

        lax.fori_loop(0, N_EXPERTS, start_zero, 0)
        lax.fori_loop(meta_ref[3, 0], N_ROW_BLOCKS, start_tail, 0)
        lax.fori_loop(0, N_EXPERTS, wait_zero, 0)
        lax.fori_loop(meta_ref[3, 0], N_ROW_BLOCKS, wait_tail, 0)

    @pl.when(i >= 2)
    def _():
        wait_slot(slot)

    x = h1b_ref[...]
    ld0 = ldest_ref[0:1, :]
    ld1 = ldest_ref[1:2, :]
    for c in range(2 * tb // SORT_CHUNK):
        r = lax.broadcasted_iota(jnp.int32, (SORT_CHUNK, tb), 0) + c * SORT_CHUNK
        perm = jnp.where((r == ld0) | (r == ld1), 1.0, 0.0).astype(jnp.bfloat16)
        rows = jnp.dot(perm, x, preferred_element_type=jnp.float32)
        _to_row_tiles(sorted_ref.at[slot, pl.ds(c * SORT_CHUNK * FEAT_TILES, SORT_CHUNK * FEAT_TILES)], rows)

    def send(lrow, grow, rows):
        pltpu.make_async_copy(_tile_rows(sorted_ref.at[slot], lrow, rows),
                              _tile_rows(buf_ref, grow, rows), sems.at[slot]).start(priority=rows.bit_length() % 2)

    _for_each_run_piece(runs_ref, send)

    @pl.when(i == n_steps - 1)
    def _():
        wait_slot(slot)
        wait_slot(1 - slot)


def _dispatch(meta, runs, ldest, h1b):
    tb = MOE_TOKENS
    return pl.pallas_call(
        _dispatch_kernel,
        grid_spec=pltpu.PrefetchScalarGridSpec(
            num_scalar_prefetch=1,
            grid=(N_MOE_BLOCKS,),
            in_specs=[pl.BlockSpec((SUBLANES, LANES), lambda i, m: (i, 0), memory_space=pltpu.SMEM),
                      pl.BlockSpec((SUBLANES, tb), lambda i, m: (0, i)),
                      pl.BlockSpec((tb, D_MODEL), lambda i, m: (i, 0))],
            out_specs=pl.BlockSpec(memory_space=pl.ANY),
            scratch_shapes=[pltpu.VMEM((2, 2 * tb * FEAT_TILES, LANES), jnp.float32),
                            pltpu.VMEM((EXPERT_ROWS * FEAT_TILES, LANES), jnp.float32),
                            pltpu.SemaphoreType.DMA((2,)), pltpu.SemaphoreType.DMA]),
        out_shape=jax.ShapeDtypeStruct((N_ROWS * FEAT_TILES, LANES), jnp.float32),
        compiler_params=pltpu.CompilerParams(
            dimension_semantics=("arbitrary",), vmem_limit_bytes=VMEM_LIMIT),
        name="dispatch",
    )(meta, runs, ldest, h1b)


def _sub_block_expert(meta, i, sub):
    return meta[0, jnp.minimum(i + sub * (N_ROW_BLOCKS // EXPERT_SUB), meta[3, 0] - 1)]


def _experts_kernel(meta_ref, x_hbm_ref, wg0_ref, wu0_ref, wd0_ref, wg1_ref, wu1_ref, wd1_ref, o_ref,
                    wgb_ref, wub_ref, wdb_ref, cached_ref, xring_ref, xsems):
    n = EXPERT_ROWS
    i = pl.program_id(0)
    n_steps = pl.num_programs(0)
    weights = ((wg0_ref, wu0_ref, wd0_ref), (wg1_ref, wu1_ref, wd1_ref))

    def rows_copy(step):
        slot = step % EXPERT_RING
        start = pl.multiple_of(step * (n * FEAT_TILES), n * FEAT_TILES)
        return pltpu.make_async_copy(x_hbm_ref.at[:, pl.ds(start, n * FEAT_TILES), :], xring_ref.at[slot],
                                     xsems.at[slot])

    @pl.when(i == 0)
    def _():
        for step in range(EXPERT_RING - 1):
            rows_copy(step).start()

    @pl.when(i + (EXPERT_RING - 1) < n_steps)
    def _():
        rows_copy(i + (EXPERT_RING - 1)).start()

    rows_copy(i).wait()
    x_ref = xring_ref.at[i % EXPERT_RING]

    @pl.when(i == 0)
    def _():
        for sub in range(EXPERT_SUB):
            cached_ref[sub] = -1

    for sub in range(EXPERT_SUB):
        expert = _sub_block_expert(meta_ref, i, sub)

        @pl.when(cached_ref[sub] != expert)
        def _(sub=sub, expert=expert):
            wg_ref, wu_ref, wd_ref = weights[sub]
            wgb_ref[sub] = wg_ref[...].astype(jnp.bfloat16)
            wub_ref[sub] = wu_ref[...].astype(jnp.bfloat16)
            wdb_ref[sub] = wd_ref[...].astype(jnp.bfloat16)
            cached_ref[sub] = expert

    for sub in range(EXPERT_SUB):
        x = _from_row_tiles(x_ref.at[sub], n).astype(jnp.bfloat16)
        gate = jnp.dot(x, wgb_ref[sub], preferred_element_type=jnp.float32)
        up = jnp.dot(x, wub_ref[sub], preferred_element_type=jnp.float32)
        act = (gate * jax.nn.sigmoid(gate) * up).astype(jnp.bfloat16)
        _to_row_tiles(o_ref.at[sub], jnp.dot(act, wdb_ref[sub], preferred_element_type=jnp.float32))


def _experts(meta, buf, wg, wu, wd):
    sub_tiles = (N_ROW_BLOCKS // EXPERT_SUB) * EXPERT_ROWS * FEAT_TILES

    def weight_spec(shape, sub):
        return pl.BlockSpec((None,) + shape, lambda i, m: (_sub_block_expert(m, i, sub), 0, 0))

    rows_shape = (EXPERT_SUB, EXPERT_ROWS * FEAT_TILES, LANES)
    up_shape, down_shape = (D_MODEL, EXPERT_FF), (EXPERT_FF, D_MODEL)
    eout = pl.pallas_call(
        _experts_kernel,
        grid_spec=pltpu.PrefetchScalarGridSpec(
            num_scalar_prefetch=1,
            grid=(N_ROW_BLOCKS // EXPERT_SUB,),
            in_specs=[pl.BlockSpec(memory_space=pl.ANY),
                      weight_spec(up_shape, 0), weight_spec(up_shape, 0), weight_spec(down_shape, 0),
                      weight_spec(up_shape, 1), weight_spec(up_shape, 1), weight_spec(down_shape, 1)],
            out_specs=pl.BlockSpec(rows_shape, lambda i, m: (0, i, 0)),
            scratch_shapes=[pltpu.VMEM((EXPERT_SUB,) + up_shape, jnp.bfloat16),
                            pltpu.VMEM((EXPERT_SUB,) + up_shape, jnp.bfloat16),
                            pltpu.VMEM((EXPERT_SUB,) + down_shape, jnp.bfloat16),
                            pltpu.SMEM((EXPERT_SUB,), jnp.int32),
                            pltpu.VMEM((EXPERT_RING,) + rows_shape, jnp.float32),
                            pltpu.SemaphoreType.DMA((EXPERT_RING,))]),
        out_shape=jax.ShapeDtypeStruct((EXPERT_SUB, sub_tiles, LANES), jnp.float32),
        compiler_params=pltpu.CompilerParams(
            dimension_semantics=("arbitrary",), vmem_limit_bytes=VMEM_LIMIT),
        name="experts",
    )(meta, buf.reshape(EXPERT_SUB, sub_tiles, LANES), wg, wu, wd, wg, wu, wd)
    return eout.reshape(N_ROWS * FEAT_TILES, LANES)


def _combine_kernel(runs_ref, runs_next_ref, ldest_ref, rf_ref, h1_ref, eout_ref, ln2g_ref, ln2b_ref,
                    o_ref, y_ref, sems):
    i = pl.program_id(0)
    n_steps = pl.num_programs(0)
    tb = ldest_ref.shape[1]
    slot = i % 2
    block_tiles = 2 * tb * FEAT_TILES

    def fetch(table_ref, s):
        def recv(lrow, grow, rows):
            pltpu.make_async_copy(_tile_rows(eout_ref, grow, rows),
                                  _tile_rows(y_ref.at[s], lrow, rows), sems.at[s]).start()
        _for_each_run_piece(table_ref, recv)

    @pl.when(i == 0)
    def _():
        fetch(runs_ref, slot)

    @pl.when(i + 1 < n_steps)
    def _():
        fetch(runs_next_ref, 1 - slot)

    pltpu.make_async_copy(eout_ref.at[pl.ds(0, block_tiles)], y_ref.at[slot], sems.at[slot]).wait()

    ld = ldest_ref[...].astype(jnp.float32).T
    gates = rf_ref[...].T
    y = None
    col = lax.broadcasted_iota(jnp.int32, (tb, SORT_CHUNK), 1).astype(jnp.float32)
    for c in range(2 * tb // SORT_CHUNK):
        ld_c = ld - float(c * SORT_CHUNK)
        g = jnp.where(col == ld_c[:, 0:1], gates[:, 0:1],
                      jnp.where(col == ld_c[:, 1:2], gates[:, 1:2], 0.0)).astype(jnp.bfloat16)
        rows = _from_row_tiles(
            y_ref.at[slot, pl.ds(c * SORT_CHUNK * FEAT_TILES, SORT_CHUNK * FEAT_TILES)], SORT_CHUNK)
        part = jnp.dot(g, rows.astype(jnp.bfloat16), preferred_element_type=jnp.float32)
        y = part if y is None else y + part
    o_ref[...] = _layer_norm(ALPHA * h1_ref[...] + y, ln2g_ref[...], ln2b_ref[...])


def _combine(runs, ldest, rf, h1, eout, ln2g, ln2b):
    tb = MOE_TOKENS
    last = N_MOE_BLOCKS - 1
    return pl.pallas_call(
        _combine_kernel,
        grid=(N_MOE_BLOCKS,),
        in_specs=[pl.BlockSpec((SUBLANES, LANES), lambda i: (i, 0), memory_space=pltpu.SMEM),
                  pl.BlockSpec((SUBLANES, LANES), lambda i: (jnp.minimum(i + 1, last), 0),
                               memory_space=pltpu.SMEM),
                  pl.BlockSpec((SUBLANES, tb), lambda i: (0, i)),
                  pl.BlockSpec((SUBLANES, tb), lambda i: (0, i)),
                  pl.BlockSpec((tb, D_MODEL), lambda i: (i, 0)),
                  pl.BlockSpec(memory_space=pl.ANY),
                  pl.BlockSpec((1, D_MODEL), lambda i: (0, 0)),
                  pl.BlockSpec((1, D_MODEL), lambda i: (0, 0))],
        out_specs=pl.BlockSpec((tb, D_MODEL), lambda i: (i, 0)),
        out_shape=jax.ShapeDtypeStruct((N_TOK, D_MODEL), jnp.float32),
        scratch_shapes=[pltpu.VMEM((2, 2 * tb * FEAT_TILES, LANES), jnp.float32),
                        pltpu.SemaphoreType.DMA((2,))],
        compiler_params=pltpu.CompilerParams(
            dimension_semantics=("arbitrary",), vmem_limit_bytes=VMEM_LIMIT),
        name="combine",
    )(runs, runs, ldest, rf, h1, eout, ln2g, ln2b)


def _swap_halves(w):
    half = w.shape[-1] // 2
    return jnp.concatenate([w[..., half:], w[..., :half]], axis=-1)


def kernel(x, positions, ln0_g, ln0_b, w_in, gm_ln_g, gm_ln_b, w_spatial, b_spatial, q_norm_g, w_uq, kv_norm_g, w_ukv, w_out, ln1_g, ln1_b, w_router_group, b_router_group, w_router_expert, b_router_expert, w_gate, w_up, w_down, ln2_g, ln2_b):
    bf16 = jnp.bfloat16
    row = lambda a: a.reshape(1, -1)

    w_in0 = w_in[0]
    kr_cols = w_in0[:, O_KR:O_KR + QK_ROPE_DIM]
    win = jnp.concatenate([w_in0, _swap_halves(kr_cols)], axis=1).astype(bf16)
    wuq3 = w_uq[0].reshape(Q_LORA_RANK, MLA_HEADS, QK_NOPE_DIM + QK_ROPE_DIM)
    rope_cols = wuq3[:, :, QK_NOPE_DIM:]
    wuq = jnp.concatenate([wuq3, _swap_halves(rope_cols)], axis=-1).reshape(Q_LORA_RANK, D_MODEL).astype(bf16)
    wukv = w_ukv[0].astype(bf16)
    wout = w_out[0].astype(bf16)
    bs = jnp.broadcast_to(b_spatial[0][:, :, None], (GM_HEADS, GM_CHUNK, GM_HEAD_DIM))
    wr = jnp.concatenate([w_router_group[0], jnp.zeros((D_MODEL, SUBLANES - N_GROUPS), jnp.float32),
                          w_router_expert[0],
                          jnp.zeros((D_MODEL, LANES - ROUTER_ROWS), jnp.float32)],
                         axis=1).astype(bf16)
    br = jnp.concatenate([b_router_group[0], jnp.zeros((SUBLANES - N_GROUPS,), jnp.float32),
                          b_router_expert[0]]).reshape(ROUTER_ROWS, 1)

    inv_freq = ROPE_THETA ** (-jnp.arange(0, QK_ROPE_DIM, 2, dtype=jnp.float32) / QK_ROPE_DIM)
    freq = jnp.tile(inv_freq, 4).reshape(1, LANES)
    quarter = QK_ROPE_DIM // 2
    phase = jnp.concatenate([jnp.zeros((2 * quarter,), jnp.float32),
                             jnp.full((2 * quarter,), math.pi / 2, jnp.float32)]).reshape(1, LANES)
    sign = jnp.concatenate([jnp.ones((2 * quarter,), jnp.float32), -jnp.ones((quarter,), jnp.float32),
                            jnp.ones((quarter,), jnp.float32)]).reshape(1, LANES)

    x2 = x.reshape(N_TOK, D_MODEL)
    pos2 = positions.reshape(N_TOK // PREP_TOKENS, 1, PREP_TOKENS)

    h, outa, q, k, v = _prep(x2, pos2, row(ln0_g), row(ln0_b), win, row(gm_ln_g[0]), row(gm_ln_b[0]),
                             w_spatial[0], bs, row(q_norm_g[0]), wuq, row(kv_norm_g[0]), wukv,
                             freq, phase, sign)
    ob = _attn(q, k, v)
    h1, h1b, ri, rf = _proj(outa, ob, h, wout, row(ln1_g[0]), row(ln1_b[0]), wr, br)
    ldest, runs, meta = _plan(ri)
    buf = _dispatch(meta, runs, ldest, h1b)
    eout = _experts(meta, buf, w_gate[0], w_up[0], w_down[0])
    out = _combine(runs, ldest, rf, h1, eout, row(ln2_g[0]), row(ln2_b[0]))
    return out.reshape(BATCH, SEQ, D_MODEL)
```

```python
import functools
import math

import jax
import jax.numpy as jnp
from jax import lax
from jax.experimental import pallas as pl
from jax.experimental.pallas import tpu as pltpu

D_MODEL = 1024
BATCH = 16
SEQ = 2048
N_TOK = BATCH * SEQ
CHUNK = 64
GM_WIDTH = 512
GM_HEADS = 4
GM_HEAD_DIM = 128
GM_CHUNK = 128
MLA_HEADS = 4
QK_NOPE_DIM = 128
QK_ROPE_DIM = 64
V_HEAD_DIM = 128
Q_LORA_RANK = 384
KV_LORA_RANK = 256
ROPE_THETA = 10000.0
N_GROUPS = 4
EXPERTS_PER_GROUP = 8
N_EXPERTS = 32
TOP_K = 2
EXPERT_FF = 256
ALPHA = 2.0 ** 0.25
QK_SCALE = (QK_NOPE_DIM + QK_ROPE_DIM) ** -0.5 * math.log2(math.e)

LANES = 128
SUBLANES = 8
FEAT_TILES = D_MODEL // LANES
PREP_TOKENS = 512
ATT_Q = 512
ATT_K = 512
N_QBLK = SEQ // ATT_Q
PROJ_TOKENS = 1024
MOE_TOKENS = 512
N_MOE_BLOCKS = N_TOK // MOE_TOKENS
RUN_BITS = (TOP_K * MOE_TOKENS).bit_length()
SORT_CHUNK = 256
EXPERT_ROWS = 256
EXPERT_SUB = 2
EXPERT_RING = 3
INPUT_RING = 3
N_ROWS = N_TOK * TOP_K + N_EXPERTS * EXPERT_ROWS
N_ROW_BLOCKS = N_ROWS // EXPERT_ROWS
META_LANES = 384
IN_COLS = 2 * GM_WIDTH + Q_LORA_RANK + KV_LORA_RANK + 2 * QK_ROPE_DIM
O_Q = 2 * GM_WIDTH
O_KV = O_Q + Q_LORA_RANK
O_KR = O_KV + KV_LORA_RANK
ROUTER_ROWS = 40
VMEM_LIMIT = 48 * 1024 * 1024

assert N_ROW_BLOCKS <= META_LANES and N_ROW_BLOCKS % EXPERT_SUB == 0


def _layer_norm(x, g, b, eps=1e-5):
    mu = jnp.mean(x, axis=-1, keepdims=True)
    xc = x - mu
    var = jnp.mean(xc * xc, axis=-1, keepdims=True)
    return xc * lax.rsqrt(var + eps) * g + b


def _rms_norm(x, g, eps=1e-6):
    return x * lax.rsqrt(jnp.mean(x * x, axis=-1, keepdims=True) + eps) * g


def _gelu_tanh(x):
    c = math.sqrt(2.0 / math.pi)
    return 0.5 * x * (1.0 + jnp.tanh(c * (x + 0.044715 * (x * x * x))))


def _to_row_tiles(ref, x):
    n = x.shape[0]
    for s in range(FEAT_TILES):
        ref[pl.ds(s, n, stride=FEAT_TILES), :] = x[:, s * LANES:(s + 1) * LANES]


def _from_row_tiles(ref, n):
    return jnp.concatenate(
        [ref[pl.ds(s, n, stride=FEAT_TILES), :] for s in range(FEAT_TILES)], axis=-1)


def _ring_block(hbm_ref, ring_ref, sems, block_of_step, rows):
    i = pl.program_id(0)
    n_steps = pl.num_programs(0)

    def copy(step):
        start = pl.multiple_of(block_of_step(step) * rows, rows)
        slot = step % INPUT_RING
        return pltpu.make_async_copy(hbm_ref.at[pl.ds(start, rows)], ring_ref.at[slot], sems.at[slot])

    @pl.when(i == 0)
    def _():
        for step in range(INPUT_RING - 1):
            copy(step).start()

    @pl.when(i + (INPUT_RING - 1) < n_steps)
    def _():
        copy(i + (INPUT_RING - 1)).start()

    copy(i).wait()
    return ring_ref.at[i % INPUT_RING]


def _ring_scratch(rows, cols, dtype):
    return [pltpu.VMEM((INPUT_RING, rows, cols), dtype), pltpu.SemaphoreType.DMA((INPUT_RING,))]


def _prep_kernel(x_ref, pos_ref, ln0g_ref, ln0b_ref, win_ref, gmg_ref, gmb_ref, ws_ref, bs_ref,
                 qg_ref, wuq_ref, kvg_ref, wukv_ref, freq_ref, phase_ref, sign_ref,
                 h_ref, outa_ref, q_ref, k_ref, v_ref, tabc_ref, tabs_ref, rot_ref):
    tb = x_ref.shape[0]

    @pl.when(pl.program_id(0) == 0)
    def _():
        d = lax.broadcasted_iota(jnp.int32, (tb, LANES), 0).astype(jnp.float32) * freq_ref[...]
        tabc_ref[...] = jnp.cos(d)
        tabs_ref[...] = jnp.sin(d)

    pos_row = pos_ref[0]
    p0 = pos_row[:, 0:1]
    offset = lax.broadcasted_iota(jnp.int32, (1, tb), 1)
    consecutive = jnp.max(jnp.abs((pos_row - p0 - offset).astype(jnp.float32))) == 0.0

    @pl.when(consecutive)
    def _():
        a0 = p0.astype(jnp.float32) * freq_ref[...]
        c0, s0 = jnp.cos(a0), jnp.sin(a0)
        lane = lax.broadcasted_iota(jnp.int32, (1, LANES), 1)
        coef_c = jnp.where(lane < 2 * 32, c0, jnp.where(lane < 3 * 32, -s0, s0))
        coef_s = jnp.where(lane < 2 * 32, -s0, jnp.where(lane < 3 * 32, -c0, c0))
        rot_ref[...] = coef_c * tabc_ref[...] + coef_s * tabs_ref[...]

    @pl.when(jnp.logical_not(consecutive))
    def _():
        pos_col = jnp.broadcast_to(pos_row.astype(jnp.float32), (SUBLANES, tb)).T[:, 0:1]
        ang = pos_col * freq_ref[...]
        rot_ref[...] = jnp.cos(ang - phase_ref[...]) * sign_ref[...]

    rot = rot_ref[...]

    h = _layer_norm(x_ref[...], ln0g_ref[...], ln0b_ref[...])
    h_ref[...] = h
    z = jnp.dot(h.astype(jnp.bfloat16), win_ref[...], preferred_element_type=jnp.float32)

    u = _gelu_tanh(z[:, :GM_WIDTH])
    v = _gelu_tanh(z[:, GM_WIDTH:2 * GM_WIDTH])
    row_chunk = lax.broadcasted_iota(jnp.int32, (GM_CHUNK, GM_CHUNK), 0) // CHUNK
    col_chunk = lax.broadcasted_iota(jnp.int32, (GM_CHUNK, GM_CHUNK), 1) // CHUNK
    allowed = col_chunk <= row_chunk
    for hd in range(GM_HEADS):
        lo, hi = hd * GM_HEAD_DIM, (hd + 1) * GM_HEAD_DIM
        vln = _layer_norm(v[:, lo:hi], gmg_ref[:, lo:hi], gmb_ref[:, lo:hi]).astype(jnp.bfloat16)
        wm = jnp.where(allowed, ws_ref[hd], 0.0).astype(jnp.bfloat16)
        for c in range(tb // GM_CHUNK):
            r0, r1 = c * GM_CHUNK, (c + 1) * GM_CHUNK
            f = jnp.dot(wm, vln[r0:r1], preferred_element_type=jnp.float32) + bs_ref[hd]
            outa_ref[r0:r1, lo:hi] = (u[r0:r1, lo:hi] * f).astype(jnp.bfloat16)

    ql = _rms_norm(z[:, O_Q:O_KV], qg_ref[...]).astype(jnp.bfloat16)
    qf = jnp.dot(ql, wuq_ref[...], preferred_element_type=jnp.float32)
    rot_s = rot * QK_SCALE
    q_parts = []
    for hd in range(MLA_HEADS):
        base = hd * 2 * LANES
        q_parts.append(qf[:, base:base + LANES] * QK_SCALE)
        q_parts.append(qf[:, base + LANES:base + 2 * LANES] * rot_s)
    q_ref[...] = jnp.concatenate(q_parts, axis=-1).astype(jnp.bfloat16)

    kvl = _rms_norm(z[:, O_KV:O_KR], kvg_ref[...]).astype(jnp.bfloat16)
    kv = jnp.dot(kvl, wukv_ref[...], preferred_element_type=jnp.float32)
    t = z[:, O_KR:O_KR + LANES] * rot
    krr = t + pltpu.roll(t, 2 * QK_ROPE_DIM // 2, axis=1)
    k_parts, v_parts = [], []
    for hd in range(MLA_HEADS):
        base = hd * 2 * LANES
        k_parts.append(kv[:, base:base + LANES])
        k_parts.append(krr)
        v_parts.append(kv[:, base + LANES:base + 2 * LANES])
    k_ref[...] = jnp.concatenate(k_parts, axis=-1).astype(jnp.bfloat16)
    v_ref[...] = jnp.concatenate(v_parts, axis=-1).astype(jnp.bfloat16)


def _prep(x2, pos2, ln0g, ln0b, win, gmg, gmb, ws, bs, qg, wuq, kvg, wukv, freq, phase, sign):
    tb = PREP_TOKENS
    full = lambda shape: pl.BlockSpec(shape, lambda i: (0,) * len(shape))
    tok = lambda cols: pl.BlockSpec((tb, cols), lambda i: (i, 0))
    return pl.pallas_call(
        _prep_kernel,
        grid=(N_TOK // tb,),
        in_specs=[tok(D_MODEL), pl.BlockSpec((1, 1, tb), lambda i: (i, 0, 0)), full((1, D_MODEL)), full((1, D_MODEL)),
                  full((D_MODEL, IN_COLS)), full((1, GM_WIDTH)), full((1, GM_WIDTH)),
                  full((GM_HEADS, GM_CHUNK, GM_CHUNK)), full((GM_HEADS, GM_CHUNK, GM_HEAD_DIM)),
                  full((1, Q_LORA_RANK)), full((Q_LORA_RANK, D_MODEL)),
                  full((1, KV_LORA_RANK)), full((KV_LORA_RANK, D_MODEL)),
                  full((1, LANES)), full((1, LANES)), full((1, LANES))],
        out_specs=[tok(D_MODEL), tok(GM_WIDTH), tok(D_MODEL), tok(D_MODEL), tok(GM_WIDTH)],
        out_shape=[jax.ShapeDtypeStruct((N_TOK, D_MODEL), jnp.float32),
                   jax.ShapeDtypeStruct((N_TOK, GM_WIDTH), jnp.bfloat16),
                   jax.ShapeDtypeStruct((N_TOK, D_MODEL), jnp.bfloat16),
                   jax.ShapeDtypeStruct((N_TOK, D_MODEL), jnp.bfloat16),
                   jax.ShapeDtypeStruct((N_TOK, GM_WIDTH), jnp.bfloat16)],
        scratch_shapes=[pltpu.VMEM((tb, LANES), jnp.float32)] * 3,
        compiler_params=pltpu.CompilerParams(
            dimension_semantics=("arbitrary",), vmem_limit_bytes=VMEM_LIMIT),
        name="prep",
    )(x2, pos2, ln0g, ln0b, win, gmg, gmb, ws, bs, qg, wuq, kvg, wukv, freq, phase, sign)


def _attn_kernel(q_ref, k_ref, v_ref, o_ref, s_ref, mx_ref, ls_ref, acc_ref):
    qi = pl.program_id(1)
    tq = q_ref.shape[0]
    nt = (((1,), (1,)), ((), ()))
    n_kv = SEQ // ATT_K

    half = tq // 2
    row_chunk = (lax.broadcasted_iota(jnp.int32, (half, half), 0)) // CHUNK
    col_chunk = (lax.broadcasted_iota(jnp.int32, (half, half), 1)) // CHUNK
    quad_allowed = col_chunk <= row_chunk
    diag_start = pl.multiple_of(qi * ATT_K, ATT_K)
    masked = jnp.float32(-1e30)

    def lane_tiles(x):
        return [x[:, t * LANES:(t + 1) * LANES] for t in range(x.shape[1] // LANES)]

    def lane_fold(x, op):
        return functools.reduce(op, lane_tiles(x))

    def scores(hd, j):
        start = pl.multiple_of(j * ATT_K, ATT_K)
        q_h = q_ref[:, hd * 2 * LANES:(hd + 1) * 2 * LANES]
        kb = k_ref[pl.ds(start, ATT_K), hd * 2 * LANES:(hd + 1) * 2 * LANES]
        return lax.dot_general(q_h, kb, nt, preferred_element_type=jnp.float32)

    for hd in range(MLA_HEADS):
        cols = slice(hd * 2 * LANES, (hd + 1) * 2 * LANES)
        k_lo = k_ref[pl.ds(diag_start, half), cols]
        k_hi = k_ref[pl.ds(diag_start + half, half), cols]
        qk = lambda q, k: lax.dot_general(q, k, nt, preferred_element_type=jnp.float32)
        s_tl = jnp.where(quad_allowed, qk(q_ref[:half, cols], k_lo), masked)
        s_bl = qk(q_ref[half:, cols], k_lo)
        s_br = jnp.where(quad_allowed, qk(q_ref[half:, cols], k_hi), masked)
        diag = s_ref.at[hd * n_kv + qi]
        diag[:half, :half] = s_tl
        diag[half:, :half] = s_bl
        diag[half:, half:] = s_br
        mx_ref[hd, :half] = lane_fold(s_tl, jnp.maximum)
        mx_ref[hd, half:] = jnp.maximum(lane_fold(s_bl, jnp.maximum), lane_fold(s_br, jnp.maximum))

    def pass_a(j, c):
        for hd in range(MLA_HEADS):
            s = scores(hd, j)
            s_ref[hd * n_kv + j] = s
            mx_ref[hd] = jnp.maximum(mx_ref[hd], lane_fold(s, jnp.maximum))
        return c

    lax.fori_loop(0, qi, pass_a, 0)

    for hd in range(MLA_HEADS):
        mx_ref[hd] = jnp.broadcast_to(jnp.max(mx_ref[hd], axis=-1, keepdims=True), (tq, LANES))

    def probs(hd, j):
        s = s_ref[hd * n_kv + j]
        mb = mx_ref[hd]
        p = jnp.exp2(jnp.concatenate([t - mb for t in lane_tiles(s)], axis=-1))
        start = pl.multiple_of(j * ATT_K, ATT_K)
        vb = v_ref[pl.ds(start, ATT_K), hd * LANES:(hd + 1) * LANES]
        pv = jnp.dot(p.astype(jnp.bfloat16), vb, preferred_element_type=jnp.float32)
        return lane_fold(p, jnp.add), pv

    for hd in range(MLA_HEADS):
        diag = s_ref.at[hd * n_kv + qi]
        vcols = slice(hd * LANES, (hd + 1) * LANES)
        p_top = jnp.exp2(jnp.concatenate([t - mx_ref[hd, :half] for t in lane_tiles(diag[:half, :half])], axis=-1))
        p_bot = jnp.exp2(jnp.concatenate([t - mx_ref[hd, half:] for t in lane_tiles(diag[half:, :])], axis=-1))
        ls_ref[hd, :half] = lane_fold(p_top, jnp.add)
        ls_ref[hd, half:] = lane_fold(p_bot, jnp.add)
        acc_ref[hd, :half] = jnp.dot(p_top.astype(jnp.bfloat16), v_ref[pl.ds(diag_start, half), vcols],
                                     preferred_element_type=jnp.float32)
        acc_ref[hd, half:] = jnp.dot(p_bot.astype(jnp.bfloat16), v_ref[pl.ds(diag_start, ATT_K), vcols],
                                     preferred_element_type=jnp.float32)

    def pass_b(j, c):
        for hd in range(MLA_HEADS):
            ls, pv = probs(hd, j)
            ls_ref[hd] = ls_ref[hd] + ls
            acc_ref[hd] = acc_ref[hd] + pv
        return c

    lax.fori_loop(0, qi, pass_b, 0)

    for hd in range(MLA_HEADS):
        l = jnp.sum(ls_ref[hd], axis=-1, keepdims=True)
        o_ref[:, hd * V_HEAD_DIM:(hd + 1) * V_HEAD_DIM] = (acc_ref[hd] / l).astype(jnp.bfloat16)


def _attn(q, k, v):
    tq = ATT_Q
    tokblk = lambda cols: pl.BlockSpec((tq, cols), lambda b, i: (b * N_QBLK + i, 0))
    seqblk = lambda cols: pl.BlockSpec((SEQ, cols), lambda b, i: (b, 0))
    return pl.pallas_call(
        _attn_kernel,
        grid=(BATCH, N_QBLK),
        in_specs=[tokblk(D_MODEL), seqblk(D_MODEL), seqblk(GM_WIDTH)],
        out_specs=tokblk(GM_WIDTH),
        out_shape=jax.ShapeDtypeStruct((N_TOK, MLA_HEADS * V_HEAD_DIM), jnp.bfloat16),
        scratch_shapes=[pltpu.VMEM((MLA_HEADS * (SEQ // ATT_K), tq, ATT_K), jnp.float32),
                        pltpu.VMEM((MLA_HEADS, tq, LANES), jnp.float32),
                        pltpu.VMEM((MLA_HEADS, tq, LANES), jnp.float32),
                        pltpu.VMEM((MLA_HEADS, tq, V_HEAD_DIM), jnp.float32)],
        compiler_params=pltpu.CompilerParams(
            dimension_semantics=("arbitrary", "arbitrary"), vmem_limit_bytes=VMEM_LIMIT),
        name="attn",
    )(q, k, v)


def _proj_kernel(outa_ref, ob_ref, h_hbm_ref, wout_ref, ln1g_ref, ln1b_ref, wr_ref, br_ref,
                 h1_ref, h1b_ref, ri_ref, rf_ref, proj_ref, hring_ref, hsems):
    i = pl.program_id(0)
    tb = outa_ref.shape[0]
    h_ref = _ring_block(h_hbm_ref, hring_ref, hsems, lambda step: jnp.maximum(step - 1, 0), tb)

    @pl.when(i == 0)
    def _():
        proj_ref[...] = jnp.zeros_like(proj_ref)

    h1 = _layer_norm(ALPHA * h_ref[...] + proj_ref[(i + 1) % 2], ln1g_ref[...], ln1b_ref[...])
    h1_ref[...] = h1
    h1b_ref[...] = h1.astype(jnp.bfloat16)

    logits_tm = jnp.dot(h1b_ref[...], wr_ref[...], preferred_element_type=jnp.float32)
    logits = logits_tm.T[0:ROUTER_ROWS] + br_ref[...]

    sub_i = lax.broadcasted_iota(jnp.int32, (SUBLANES, tb), 0)
    sub = sub_i.astype(jnp.float32)
    neg = jnp.float32(-jnp.inf)
    g = jnp.where(sub_i < N_GROUPS, logits[0:SUBLANES], neg)
    gmax = jnp.max(g, axis=0, keepdims=True)
    g_top = jnp.min(jnp.where(g == gmax, sub, float(SUBLANES)), axis=0, keepdims=True)
    p_group = 1.0 / jnp.sum(jnp.exp(g - gmax), axis=0, keepdims=True)
    sel = logits[SUBLANES:2 * SUBLANES]
    for grp in range(1, N_GROUPS):
        sel = jnp.where(g_top == float(grp), logits[(grp + 1) * SUBLANES:(grp + 2) * SUBLANES], sel)
    v1 = jnp.max(sel, axis=0, keepdims=True)
    i1 = jnp.min(jnp.where(sel == v1, sub, float(SUBLANES)), axis=0, keepdims=True)
    sel2 = jnp.where(sub == i1, neg, sel)
    v2 = jnp.max(sel2, axis=0, keepdims=True)
    i2 = jnp.min(jnp.where(sel2 == v2, sub, float(SUBLANES)), axis=0, keepdims=True)
    e21 = jnp.exp(v2 - v1)
    w1 = 1.0 / (1.0 + e21)
    gate1 = p_group * w1
    gate2 = p_group * (e21 * w1)
    e1 = g_top * EXPERTS_PER_GROUP + i1
    e2 = g_top * EXPERTS_PER_GROUP + i2
    ri_ref[...] = jnp.where(sub_i == 0, e1, jnp.where(sub_i == 1, e2, 0.0)).astype(jnp.int32)
    rf_ref[...] = jnp.where(sub_i == 0, gate1, jnp.where(sub_i == 1, gate2, 0.0))

    proj_ref[i % 2] = (jnp.dot(outa_ref[...], wout_ref[:GM_WIDTH, :], preferred_element_type=jnp.float32)
                       + jnp.dot(ob_ref[...], wout_ref[GM_WIDTH:, :], preferred_element_type=jnp.float32))


def _proj(outa, ob, h, wout, ln1g, ln1b, wr, br):
    tb = PROJ_TOKENS
    n_blk = N_TOK // tb
    cur = lambda i: jnp.minimum(i, n_blk - 1)
    prev = lambda i: jnp.maximum(i - 1, 0)
    full = lambda shape: pl.BlockSpec(shape, lambda i: (0,) * len(shape))
    return pl.pallas_call(
        _proj_kernel,
        grid=(n_blk + 1,),
        in_specs=[pl.BlockSpec((tb, GM_WIDTH), lambda i: (cur(i), 0)),
                  pl.BlockSpec((tb, GM_WIDTH), lambda i: (cur(i), 0)),
                  pl.BlockSpec(memory_space=pl.ANY),
                  full((D_MODEL, D_MODEL)), full((1, D_MODEL)), full((1, D_MODEL)),
                  full((D_MODEL, LANES)), full((ROUTER_ROWS, 1))],
        out_specs=[pl.BlockSpec((tb, D_MODEL), lambda i: (prev(i), 0)),
                   pl.BlockSpec((tb, D_MODEL), lambda i: (prev(i), 0)),
                   pl.BlockSpec((SUBLANES, tb), lambda i: (0, prev(i))),
                   pl.BlockSpec((SUBLANES, tb), lambda i: (0, prev(i)))],
        out_shape=[jax.ShapeDtypeStruct((N_TOK, D_MODEL), jnp.float32),
                   jax.ShapeDtypeStruct((N_TOK, D_MODEL), jnp.bfloat16),
                   jax.ShapeDtypeStruct((SUBLANES, N_TOK), jnp.int32),
                   jax.ShapeDtypeStruct((SUBLANES, N_TOK), jnp.float32)],
        scratch_shapes=[pltpu.VMEM((2, tb, D_MODEL), jnp.float32)] + _ring_scratch(tb, D_MODEL, jnp.float32),
        compiler_params=pltpu.CompilerParams(
            dimension_semantics=("arbitrary",), vmem_limit_bytes=VMEM_LIMIT),
        name="proj",
    )(outa, ob, h, wout, ln1g, ln1b, wr, br)


def _plan_kernel(ri_all_ref, ri_ref, ldest_ref, runs_ref, meta_ref, run_ref, start_ref, upper_ref):
    step = pl.program_id(0)
    tb = ri_ref.shape[1]
    f32 = jnp.float32
    er = lax.broadcasted_iota(jnp.int32, (N_EXPERTS, LANES), 0)
    ec = lax.broadcasted_iota(jnp.int32, (N_EXPERTS, LANES), 1)
    to_row = lambda col: jnp.sum(jnp.where(er == ec, col, 0.0), axis=0, keepdims=True)

    def expert_one_hot(ref):
        e_sub = lax.broadcasted_iota(jnp.int32, (N_EXPERTS, ref.shape[1]), 0)
        return e_sub == ref[0:1, :], e_sub == ref[1:2, :]

    @pl.when(step == 0)
    def _():
        oh1, oh2 = expert_one_hot(ri_all_ref)
        counts = jnp.sum(jnp.where(oh1 | oh2, 1.0, 0.0), axis=1, keepdims=True)
        padded = jnp.floor((counts + (EXPERT_ROWS - 1)) * (1.0 / EXPERT_ROWS)) * EXPERT_ROWS
        pad_end = jnp.sum(jnp.where(ec <= er, to_row(padded), 0.0), axis=1, keepdims=True)
        start_ref[...] = jnp.broadcast_to(pad_end - padded, start_ref.shape)
        run_ref[...] = jnp.zeros_like(run_ref)
        bstart = (lax.broadcasted_iota(jnp.int32, (N_EXPERTS, META_LANES), 1) * EXPERT_ROWS).astype(f32)
        blk_e = jnp.sum(jnp.where(pad_end <= bstart, 1.0, 0.0), axis=0, keepdims=True)
        blk_e = jnp.minimum(blk_e, N_EXPERTS - 1.0)
        n_used = pad_end[N_EXPERTS - 1:N_EXPERTS, :] * (1.0 / EXPERT_ROWS)
        pad3 = lambda r: jnp.concatenate(
            [r, jnp.zeros((1, META_LANES - LANES), f32)], axis=1)
        msub = lax.broadcasted_iota(jnp.int32, (SUBLANES, META_LANES), 0)
        meta = jnp.where(msub == 0, blk_e,
                         jnp.where(msub == 1, pad3(to_row(pad_end)),
                                   jnp.where(msub == 2, pad3(to_row(counts)),
                                             jnp.where(msub == 3, n_used, 0.0))))
        meta_ref[...] = meta.astype(jnp.int32)
        tr = lax.broadcasted_iota(jnp.int32, (tb, tb), 0)
        tc = lax.broadcasted_iota(jnp.int32, (tb, tb), 1)
        upper_ref[...] = jnp.where(tr < tc, 1.0, 0.0).astype(jnp.bfloat16)

    @pl.when(step > 0)
    def _():
        oh1, oh2 = expert_one_hot(ri_ref)
        oh = jnp.where(oh1 | oh2, 1.0, 0.0).astype(f32)
        blk_count = jnp.sum(oh, axis=1, keepdims=True)
        prefix = jnp.dot(oh.astype(jnp.bfloat16), upper_ref[...], preferred_element_type=f32)
        cnt_row = to_row(blk_count)
        lstart = jnp.sum(jnp.where(ec < er, cnt_row, 0.0), axis=1, keepdims=True)
        base = prefix + lstart
        d1 = jnp.sum(jnp.where(oh1, base, 0.0), axis=0, keepdims=True)
        d2 = jnp.sum(jnp.where(oh2, base, 0.0), axis=0, keepdims=True)
        sub = lax.broadcasted_iota(jnp.int32, (SUBLANES, tb), 0)
        ldest_ref[...] = jnp.where(sub == 0, d1, jnp.where(sub == 1, d2, 0.0)).astype(jnp.int32)
        gstart = start_ref[:, 0:1] + run_ref[:, 0:1]
        rsub = lax.broadcasted_iota(jnp.int32, (SUBLANES, LANES), 0)
        runs = jnp.where(rsub == 0, cnt_row,
                         jnp.where(rsub == 1, to_row(lstart), jnp.where(rsub == 2, to_row(gstart), 0.0)))
        runs_ref[...] = runs.astype(jnp.int32)
        run_ref[...] = run_ref[...] + blk_count


def _plan(ri):
    tb = MOE_TOKENS
    blk = lambda i: jnp.maximum(i - 1, 0)
    return pl.pallas_call(
        _plan_kernel,
        grid=(N_MOE_BLOCKS + 1,),
        in_specs=[pl.BlockSpec((SUBLANES, N_TOK), lambda i: (0, 0)),
                  pl.BlockSpec((SUBLANES, tb), lambda i: (0, blk(i)))],
        out_specs=[pl.BlockSpec((SUBLANES, tb), lambda i: (0, blk(i))),
                   pl.BlockSpec((SUBLANES, LANES), lambda i: (blk(i), 0)),
                   pl.BlockSpec((SUBLANES, META_LANES), lambda i: (0, 0))],
        out_shape=[jax.ShapeDtypeStruct((SUBLANES, N_TOK), jnp.int32),
                   jax.ShapeDtypeStruct((N_MOE_BLOCKS * SUBLANES, LANES), jnp.int32),
                   jax.ShapeDtypeStruct((SUBLANES, META_LANES), jnp.int32)],
        scratch_shapes=[pltpu.VMEM((N_EXPERTS, LANES), jnp.float32),
                        pltpu.VMEM((N_EXPERTS, LANES), jnp.float32),
                        pltpu.VMEM((tb, tb), jnp.bfloat16)],
        compiler_params=pltpu.CompilerParams(
            dimension_semantics=("arbitrary",), vmem_limit_bytes=VMEM_LIMIT),
        name="plan",
    )(ri, ri)


def _for_each_run_piece(runs_ref, fn):
    for e in range(N_EXPERTS):
        n, lstart, gstart = runs_ref[0, e], runs_ref[1, e], runs_ref[2, e]
        for bit in range(RUN_BITS):
            @pl.when((n & (1 << bit)) != 0)
            def _(n=n, lstart=lstart, gstart=gstart, bit=bit):
                off = (n >> (bit + 1)) << (bit + 1)
                fn(lstart + off, gstart + off, 1 << bit)


def _tile_rows(ref, row, rows):
    return ref.at[pl.ds(pl.multiple_of(row * FEAT_TILES, FEAT_TILES), rows * FEAT_TILES)]


def _dispatch_kernel(meta_ref, runs_ref, ldest_ref, h1b_ref, buf_ref, sorted_ref, zero_ref, sems, zsem):
    i = pl.program_id(0)
    n_steps = pl.num_programs(0)
    tb = ldest_ref.shape[1]
    slot = i % 2
    block_tiles = 2 * tb * FEAT_TILES

    def wait_slot(s):
        pltpu.make_async_copy(sorted_ref.at[s], buf_ref.at[pl.ds(0, block_tiles)], sems.at[s]).wait()

    @pl.when(i == 0)
    def _():
        zero_ref[...] = jnp.zeros_like(zero_ref)

        def zero_copy(e):
            start = pl.multiple_of((meta_ref[1, e] - EXPERT_ROWS) * FEAT_TILES, EXPERT_ROWS * FEAT_TILES)
            return pltpu.make_async_copy(
                zero_ref, buf_ref.at[pl.ds(start, EXPERT_ROWS * FEAT_TILES)], zsem)

        def start_zero(e, c):
            @pl.when(meta_ref[2, e] > 0)
            def _():
                zero_copy(e).start()
            return c

        def wait_zero(e, c):
            @pl.when(meta_ref[2, e] > 0)
            def _():
                zero_copy(e).wait()
            return c

        def tail_copy(b):
            start = pl.multiple_of(b * (EXPERT_ROWS * FEAT_TILES), EXPERT_ROWS * FEAT_TILES)
            return pltpu.make_async_copy(
                zero_ref, buf_ref.at[pl.ds(start, EXPERT_ROWS * FEAT_TILES)], zsem)

        def start_tail(b, c):
            tail_copy(b).start()
            return c

        def wait_tail(b, c):
            tail_copy(b).wait()
            return c

        lax.fori_loop(0, N_EXPERTS, start_zero, 0)
        lax.fori_loop(meta_ref[3, 0], N_ROW_BLOCKS, start_tail, 0)
        lax.fori_loop(0, N_EXPERTS, wait_zero, 0)
        lax.fori_loop(meta_ref[3, 0], N_ROW_BLOCKS, wait_tail, 0)

    @pl.when(i >= 2)
    def _():
        wait_slot(slot)

    x = h1b_ref[...]
    ld0 = ldest_ref[0:1, :]
    ld1 = ldest_ref[1:2, :]
    for c in range(2 * tb // SORT_CHUNK):
        r = lax.broadcasted_iota(jnp.int32, (SORT_CHUNK, tb), 0) + c * SORT_CHUNK
        perm = jnp.where((r == ld0) | (r == ld1), 1.0, 0.0).astype(jnp.bfloat16)
        rows = jnp.dot(perm, x, preferred_element_type=jnp.float32)
        _to_row_tiles(sorted_ref.at[slot, pl.ds(c * SORT_CHUNK * FEAT_TILES, SORT_CHUNK * FEAT_TILES)], rows)

    def send(lrow, grow, rows):
        pltpu.make_async_copy(_tile_rows(sorted_ref.at[slot], lrow, rows),
                              _tile_rows(buf_ref, grow, rows), sems.at[slot]).start()

    _for_each_run_piece(runs_ref, send)

    @pl.when(i == n_steps - 1)
    def _():
        wait_slot(slot)
        wait_slot(1 - slot)


def _dispatch(meta, runs, ldest, h1b):
    tb = MOE_TOKENS
    return pl.pallas_call(
        _dispatch_kernel,
        grid_spec=pltpu.PrefetchScalarGridSpec(
            num_scalar_prefetch=1,
            grid=(N_MOE_BLOCKS,),
            in_specs=[pl.BlockSpec((SUBLANES, LANES), lambda i, m: (i, 0), memory_space=pltpu.SMEM),
                      pl.BlockSpec((SUBLANES, tb), lambda i, m: (0, i)),
                      pl.BlockSpec((tb, D_MODEL), lambda i, m: (i, 0))],
            out_specs=pl.BlockSpec(memory_space=pl.ANY),
            scratch_shapes=[pltpu.VMEM((2, 2 * tb * FEAT_TILES, LANES), jnp.float32),
                            pltpu.VMEM((EXPERT_ROWS * FEAT_TILES, LANES), jnp.float32),
                            pltpu.SemaphoreType.DMA((2,)), pltpu.SemaphoreType.DMA]),
        out_shape=jax.ShapeDtypeStruct((N_ROWS * FEAT_TILES, LANES), jnp.float32),
        compiler_params=pltpu.CompilerParams(
            dimension_semantics=("arbitrary",), vmem_limit_bytes=VMEM_LIMIT),
        name="dispatch",
    )(meta, runs, ldest, h1b)


def _sub_block_expert(meta, i, sub):
    return meta[0, jnp.minimum(i + sub * (N_ROW_BLOCKS // EXPERT_SUB), meta[3, 0] - 1)]


def _experts_kernel(meta_ref, x_hbm_ref, wg0_ref, wu0_ref, wd0_ref, wg1_ref, wu1_ref, wd1_ref, o_ref,
                    wgb_ref, wub_ref, wdb_ref, cached_ref, xring_ref, xsems):
    n = EXPERT_ROWS
    i = pl.program_id(0)
    n_steps = pl.num_programs(0)
    weights = ((wg0_ref, wu0_ref, wd0_ref), (wg1_ref, wu1_ref, wd1_ref))

    def rows_copy(step):
        slot = step % EXPERT_RING
        start = pl.multiple_of(step * (n * FEAT_TILES), n * FEAT_TILES)
        return pltpu.make_async_copy(x_hbm_ref.at[:, pl.ds(start, n * FEAT_TILES), :], xring_ref.at[slot],
                                     xsems.at[slot])

    @pl.when(i == 0)
    def _():
        for step in range(EXPERT_RING - 1):
            rows_copy(step).start()

    @pl.when(i + (EXPERT_RING - 1) < n_steps)
    def _():
        rows_copy(i + (EXPERT_RING - 1)).start()

    rows_copy(i).wait()
    x_ref = xring_ref.at[i % EXPERT_RING]

    @pl.when(i == 0)
    def _():
        for sub in range(EXPERT_SUB):
            cached_ref[sub] = -1

    for sub in range(EXPERT_SUB):
        expert = _sub_block_expert(meta_ref, i, sub)

        @pl.when(cached_ref[sub] != expert)
        def _(sub=sub, expert=expert):
            wg_ref, wu_ref, wd_ref = weights[sub]
            wgb_ref[sub] = wg_ref[...].astype(jnp.bfloat16)
            wub_ref[sub] = wu_ref[...].astype(jnp.bfloat16)
            wdb_ref[sub] = wd_ref[...].astype(jnp.bfloat16)
            cached_ref[sub] = expert

    for sub in range(EXPERT_SUB):
        x = _from_row_tiles(x_ref.at[sub], n).astype(jnp.bfloat16)
        gate = jnp.dot(x, wgb_ref[sub], preferred_element_type=jnp.float32)
        up = jnp.dot(x, wub_ref[sub], preferred_element_type=jnp.float32)
        act = (gate * jax.nn.sigmoid(gate) * up).astype(jnp.bfloat16)
        _to_row_tiles(o_ref.at[sub], jnp.dot(act, wdb_ref[sub], preferred_element_type=jnp.float32))


def _experts(meta, buf, wg, wu, wd):
    sub_tiles = (N_ROW_BLOCKS // EXPERT_SUB) * EXPERT_ROWS * FEAT_TILES

    def weight_spec(shape, sub):
        return pl.BlockSpec((None,) + shape, lambda i, m: (_sub_block_expert(m, i, sub), 0, 0))

    rows_shape = (EXPERT_SUB, EXPERT_ROWS * FEAT_TILES, LANES)
    up_shape, down_shape = (D_MODEL, EXPERT_FF), (EXPERT_FF, D_MODEL)
    eout = pl.pallas_call(
        _experts_kernel,
        grid_spec=pltpu.PrefetchScalarGridSpec(
            num_scalar_prefetch=1,
            grid=(N_ROW_BLOCKS // EXPERT_SUB,),
            in_specs=[pl.BlockSpec(memory_space=pl.ANY),
                      weight_spec(up_shape, 0), weight_spec(up_shape, 0), weight_spec(down_shape, 0),
                      weight_spec(up_shape, 1), weight_spec(up_shape, 1), weight_spec(down_shape, 1)],
            out_specs=pl.BlockSpec(rows_shape, lambda i, m: (0, i, 0)),
            scratch_shapes=[pltpu.VMEM((EXPERT_SUB,) + up_shape, jnp.bfloat16),
                            pltpu.VMEM((EXPERT_SUB,) + up_shape, jnp.bfloat16),
                            pltpu.VMEM((EXPERT_SUB,) + down_shape, jnp.bfloat16),
                            pltpu.SMEM((EXPERT_SUB,), jnp.int32),
                            pltpu.VMEM((EXPERT_RING,) + rows_shape, jnp.float32),
                            pltpu.SemaphoreType.DMA((EXPERT_RING,))]),
        out_shape=jax.ShapeDtypeStruct((EXPERT_SUB, sub_tiles, LANES), jnp.float32),
        compiler_params=pltpu.CompilerParams(
            dimension_semantics=("arbitrary",), vmem_limit_bytes=VMEM_LIMIT),
        name="experts",
    )(meta, buf.reshape(EXPERT_SUB, sub_tiles, LANES), wg, wu, wd, wg, wu, wd)
    return eout.reshape(N_ROWS * FEAT_TILES, LANES)


def _combine_kernel(runs_ref, runs_next_ref, ldest_ref, rf_ref, h1_ref, eout_ref, ln2g_ref, ln2b_ref,
                    o_ref, y_ref, sems):
    i = pl.program_id(0)
    n_steps = pl.num_programs(0)
    tb = ldest_ref.shape[1]
    slot = i % 2
    block_tiles = 2 * tb * FEAT_TILES

    def fetch(table_ref, s):
        def recv(lrow, grow, rows):
            pltpu.make_async_copy(_tile_rows(eout_ref, grow, rows),
                                  _tile_rows(y_ref.at[s], lrow, rows), sems.at[s]).start()
        _for_each_run_piece(table_ref, recv)

    @pl.when(i == 0)
    def _():
        fetch(runs_ref, slot)

    @pl.when(i + 1 < n_steps)
    def _():
        fetch(runs_next_ref, 1 - slot)

    pltpu.make_async_copy(eout_ref.at[pl.ds(0, block_tiles)], y_ref.at[slot], sems.at[slot]).wait()

    ld = ldest_ref[...].astype(jnp.float32).T
    gates = rf_ref[...].T
    y = None
    col = lax.broadcasted_iota(jnp.int32, (tb, SORT_CHUNK), 1).astype(jnp.float32)
    for c in range(2 * tb // SORT_CHUNK):
        ld_c = ld - float(c * SORT_CHUNK)
        g = jnp.where(col == ld_c[:, 0:1], gates[:, 0:1],
                      jnp.where(col == ld_c[:, 1:2], gates[:, 1:2], 0.0)).astype(jnp.bfloat16)
        rows = _from_row_tiles(
            y_ref.at[slot, pl.ds(c * SORT_CHUNK * FEAT_TILES, SORT_CHUNK * FEAT_TILES)], SORT_CHUNK)
        part = jnp.dot(g, rows.astype(jnp.bfloat16), preferred_element_type=jnp.float32)
        y = part if y is None else y + part
    o_ref[...] = _layer_norm(ALPHA * h1_ref[...] + y, ln2g_ref[...], ln2b_ref[...])


def _combine(runs, ldest, rf, h1, eout, ln2g, ln2b):
    tb = MOE_TOKENS
    last = N_MOE_BLOCKS - 1
    return pl.pallas_call(
        _combine_kernel,
        grid=(N_MOE_BLOCKS,),
        in_specs=[pl.BlockSpec((SUBLANES, LANES), lambda i: (i, 0), memory_space=pltpu.SMEM),
                  pl.BlockSpec((SUBLANES, LANES), lambda i: (jnp.minimum(i + 1, last), 0),
                               memory_space=pltpu.SMEM),
                  pl.BlockSpec((SUBLANES, tb), lambda i: (0, i)),
                  pl.BlockSpec((SUBLANES, tb), lambda i: (0, i)),
                  pl.BlockSpec((tb, D_MODEL), lambda i: (i, 0)),
                  pl.BlockSpec(memory_space=pl.ANY),
                  pl.BlockSpec((1, D_MODEL), lambda i: (0, 0)),
                  pl.BlockSpec((1, D_MODEL), lambda i: (0, 0))],
        out_specs=pl.BlockSpec((tb, D_MODEL), lambda i: (i, 0)),
        out_shape=jax.ShapeDtypeStruct((N_TOK, D_MODEL), jnp.float32),
        scratch_shapes=[pltpu.VMEM((2, 2 * tb * FEAT_TILES, LANES), jnp.float32),
                        pltpu.SemaphoreType.DMA((2,))],
        compiler_params=pltpu.CompilerParams(
            dimension_semantics=("arbitrary",), vmem_limit_bytes=VMEM_LIMIT),
        name="combine",
    )(runs, runs, ldest, rf, h1, eout, ln2g, ln2b)


def _swap_halves(w):
    half = w.shape[-1] // 2
    return jnp.concatenate([w[..., half:], w[..., :half]], axis=-1)


def kernel(x, positions, ln0_g, ln0_b, w_in, gm_ln_g, gm_ln_b, w_spatial, b_spatial, q_norm_g, w_uq, kv_norm_g, w_ukv, w_out, ln1_g, ln1_b, w_router_group, b_router_group, w_router_expert, b_router_expert, w_gate, w_up, w_down, ln2_g, ln2_b):
    bf16 = jnp.bfloat16
    row = lambda a: a.reshape(1, -1)

    w_in0 = w_in[0]
    kr_cols = w_in0[:, O_KR:O_KR + QK_ROPE_DIM]
    win = jnp.concatenate([w_in0, _swap_halves(kr_cols)], axis=1).astype(bf16)
    wuq3 = w_uq[0].reshape(Q_LORA_RANK, MLA_HEADS, QK_NOPE_DIM + QK_ROPE_DIM)
    rope_cols = wuq3[:, :, QK_NOPE_DIM:]
    wuq = jnp.concatenate([wuq3, _swap_halves(rope_cols)], axis=-1).reshape(Q_LORA_RANK, D_MODEL).astype(bf16)
    wukv = w_ukv[0].astype(bf16)
    wout = w_out[0].astype(bf16)
    bs = jnp.broadcast_to(b_spatial[0][:, :, None], (GM_HEADS, GM_CHUNK, GM_HEAD_DIM))
    wr = jnp.concatenate([w_router_group[0], jnp.zeros((D_MODEL, SUBLANES - N_GROUPS), jnp.float32),
                          w_router_expert[0],
                          jnp.zeros((D_MODEL, LANES - ROUTER_ROWS), jnp.float32)],
                         axis=1).astype(bf16)
    br = jnp.concatenate([b_router_group[0], jnp.zeros((SUBLANES - N_GROUPS,), jnp.float32),
                          b_router_expert[0]]).reshape(ROUTER_ROWS, 1)

    inv_freq = ROPE_THETA ** (-jnp.arange(0, QK_ROPE_DIM, 2, dtype=jnp.float32) / QK_ROPE_DIM)
    freq = jnp.tile(inv_freq, 4).reshape(1, LANES)
    quarter = QK_ROPE_DIM // 2
    phase = jnp.concatenate([jnp.zeros((2 * quarter,), jnp.float32),
                             jnp.full((2 * quarter,), math.pi / 2, jnp.float32)]).reshape(1, LANES)
    sign = jnp.concatenate([jnp.ones((2 * quarter,), jnp.float32), -jnp.ones((quarter,), jnp.float32),
                            jnp.ones((quarter,), jnp.float32)]).reshape(1, LANES)

    x2 = x.reshape(N_TOK, D_MODEL)
    pos2 = positions.reshape(N_TOK // PREP_TOKENS, 1, PREP_TOKENS)

    h, outa, q, k, v = _prep(x2, pos2, row(ln0_g), row(ln0_b), win, row(gm_ln_g[0]), row(gm_ln_b[0]),
                             w_spatial[0], bs, row(q_norm_g[0]), wuq, row(kv_norm_g[0]), wukv,
                             freq, phase, sign)
    ob = _attn(q, k, v)
    h1, h1b, ri, rf = _proj(outa, ob, h, wout, row(ln1_g[0]), row(ln1_b[0]), wr, br)
    ldest, runs, meta = _plan(ri)
    buf = _dispatch(meta, runs, ldest, h1b)
    eout = _experts(meta, buf, w_gate[0], w_up[0], w_down[0])
    out = _combine(runs, ldest, rf, h1, eout, row(ln2_g[0]), row(ln2_b[0]))
    return out.reshape(BATCH, SEQ, D_MODEL)
```

```python
import functools
import math

import jax
import jax.numpy as jnp
from jax import lax
from jax.experimental import pallas as pl
from jax.experimental.pallas import tpu as pltpu

D_MODEL = 1024
BATCH = 16
SEQ = 2048
N_TOK = BATCH * SEQ
CHUNK = 64
GM_WIDTH = 512
GM_HEADS = 4
GM_HEAD_DIM = 128
GM_CHUNK = 128
MLA_HEADS = 4
QK_NOPE_DIM = 128
QK_ROPE_DIM = 64
V_HEAD_DIM = 128
Q_LORA_RANK = 384
KV_LORA_RANK = 256
ROPE_THETA = 10000.0
N_GROUPS = 4
EXPERTS_PER_GROUP = 8
N_EXPERTS = 32
TOP_K = 2
EXPERT_FF = 256
ALPHA = 2.0 ** 0.25
QK_SCALE = (QK_NOPE_DIM + QK_ROPE_DIM) ** -0.5 * math.log2(math.e)

LANES = 128
SUBLANES = 8
FEAT_TILES = D_MODEL // LANES
PREP_TOKENS = 512
ATT_Q = 512
ATT_K = 512
N_QBLK = SEQ // ATT_Q
PROJ_TOKENS = 1024
MOE_TOKENS = 512
N_MOE_BLOCKS = N_TOK // MOE_TOKENS
RUN_BITS = (TOP_K * MOE_TOKENS).bit_length()
SORT_CHUNK = 256
EXPERT_ROWS = 256
EXPERT_SUB = 3
EXPERT_RING = 3
INPUT_RING = 3
N_ROWS = N_TOK * TOP_K + N_EXPERTS * EXPERT_ROWS
N_ROW_BLOCKS = N_ROWS // EXPERT_ROWS
META_LANES = 384
IN_COLS = 2 * GM_WIDTH + Q_LORA_RANK + KV_LORA_RANK + 2 * QK_ROPE_DIM
O_Q = 2 * GM_WIDTH
O_KV = O_Q + Q_LORA_RANK
O_KR = O_KV + KV_LORA_RANK
ROUTER_ROWS = 40
VMEM_LIMIT = 48 * 1024 * 1024

assert N_ROW_BLOCKS <= META_LANES and N_ROW_BLOCKS % EXPERT_SUB == 0


def _layer_norm(x, g, b, eps=1e-5):
    mu = jnp.mean(x, axis=-1, keepdims=True)
    xc = x - mu
    var = jnp.mean(xc * xc, axis=-1, keepdims=True)
    return xc * lax.rsqrt(var + eps) * g + b


def _rms_norm(x, g, eps=1e-6):
    return x * lax.rsqrt(jnp.mean(x * x, axis=-1, keepdims=True) + eps) * g


def _gelu_tanh(x):
    c = math.sqrt(2.0 / math.pi)
    return 0.5 * x * (1.0 + jnp.tanh(c * (x + 0.044715 * (x * x * x))))


def _to_row_tiles(ref, x):
    n = x.shape[0]
    for s in range(FEAT_TILES):
        ref[pl.ds(s, n, stride=FEAT_TILES), :] = x[:, s * LANES:(s + 1) * LANES]


def _from_row_tiles(ref, n):
    return jnp.concatenate(
        [ref[pl.ds(s, n, stride=FEAT_TILES), :] for s in range(FEAT_TILES)], axis=-1)


def _ring_block(hbm_ref, ring_ref, sems, block_of_step, rows):
    i = pl.program_id(0)
    n_steps = pl.num_programs(0)

    def copy(step):
        start = pl.multiple_of(block_of_step(step) * rows, rows)
        slot = step % INPUT_RING
        return pltpu.make_async_copy(hbm_ref.at[pl.ds(start, rows)], ring_ref.at[slot], sems.at[slot])

    @pl.when(i == 0)
    def _():
        for step in range(INPUT_RING - 1):
            copy(step).start()

    @pl.when(i + (INPUT_RING - 1) < n_steps)
    def _():
        copy(i + (INPUT_RING - 1)).start()

    copy(i).wait()
    return ring_ref.at[i % INPUT_RING]


def _ring_scratch(rows, cols, dtype):
    return [pltpu.VMEM((INPUT_RING, rows, cols), dtype), pltpu.SemaphoreType.DMA((INPUT_RING,))]


def _prep_kernel(x_ref, pos_ref, ln0g_ref, ln0b_ref, win_ref, gmg_ref, gmb_ref, ws_ref, bs_ref,
                 qg_ref, wuq_ref, kvg_ref, wukv_ref, freq_ref, phase_ref, sign_ref,
                 h_ref, outa_ref, q_ref, k_ref, v_ref, tabc_ref, tabs_ref, rot_ref):
    tb = x_ref.shape[0]

    @pl.when(pl.program_id(0) == 0)
    def _():
        d = lax.broadcasted_iota(jnp.int32, (tb, LANES), 0).astype(jnp.float32) * freq_ref[...]
        tabc_ref[...] = jnp.cos(d)
        tabs_ref[...] = jnp.sin(d)

    pos_row = pos_ref[0]
    p0 = pos_row[:, 0:1]
    offset = lax.broadcasted_iota(jnp.int32, (1, tb), 1)
    consecutive = jnp.max(jnp.abs((pos_row - p0 - offset).astype(jnp.float32))) == 0.0

    @pl.when(consecutive)
    def _():
        a0 = p0.astype(jnp.float32) * freq_ref[...]
        c0, s0 = jnp.cos(a0), jnp.sin(a0)
        lane = lax.broadcasted_iota(jnp.int32, (1, LANES), 1)
        coef_c = jnp.where(lane < 2 * 32, c0, jnp.where(lane < 3 * 32, -s0, s0))
        coef_s = jnp.where(lane < 2 * 32, -s0, jnp.where(lane < 3 * 32, -c0, c0))
        rot_ref[...] = coef_c * tabc_ref[...] + coef_s * tabs_ref[...]

    @pl.when(jnp.logical_not(consecutive))
    def _():
        pos_col = jnp.broadcast_to(pos_row.astype(jnp.float32), (SUBLANES, tb)).T[:, 0:1]
        ang = pos_col * freq_ref[...]
        rot_ref[...] = jnp.cos(ang - phase_ref[...]) * sign_ref[...]

    rot = rot_ref[...]

    h = _layer_norm(x_ref[...], ln0g_ref[...], ln0b_ref[...])
    h_ref[...] = h
    z = jnp.dot(h.astype(jnp.bfloat16), win_ref[...], preferred_element_type=jnp.float32)

    u = _gelu_tanh(z[:, :GM_WIDTH])
    v = _gelu_tanh(z[:, GM_WIDTH:2 * GM_WIDTH])
    row_chunk = lax.broadcasted_iota(jnp.int32, (GM_CHUNK, GM_CHUNK), 0) // CHUNK
    col_chunk = lax.broadcasted_iota(jnp.int32, (GM_CHUNK, GM_CHUNK), 1) // CHUNK
    allowed = col_chunk <= row_chunk
    for hd in range(GM_HEADS):
        lo, hi = hd * GM_HEAD_DIM, (hd + 1) * GM_HEAD_DIM
        vln = _layer_norm(v[:, lo:hi], gmg_ref[:, lo:hi], gmb_ref[:, lo:hi]).astype(jnp.bfloat16)
        wm = jnp.where(allowed, ws_ref[hd], 0.0).astype(jnp.bfloat16)
        for c in range(tb // GM_CHUNK):
            r0, r1 = c * GM_CHUNK, (c + 1) * GM_CHUNK
            f = jnp.dot(wm, vln[r0:r1], preferred_element_type=jnp.float32) + bs_ref[hd]
            outa_ref[r0:r1, lo:hi] = (u[r0:r1, lo:hi] * f).astype(jnp.bfloat16)

    ql = _rms_norm(z[:, O_Q:O_KV], qg_ref[...]).astype(jnp.bfloat16)
    qf = jnp.dot(ql, wuq_ref[...], preferred_element_type=jnp.float32)
    rot_s = rot * QK_SCALE
    q_parts = []
    for hd in range(MLA_HEADS):
        base = hd * 2 * LANES
        q_parts.append(qf[:, base:base + LANES] * QK_SCALE)
        q_parts.append(qf[:, base + LANES:base + 2 * LANES] * rot_s)
    q_ref[...] = jnp.concatenate(q_parts, axis=-1).astype(jnp.bfloat16)

    kvl = _rms_norm(z[:, O_KV:O_KR], kvg_ref[...]).astype(jnp.bfloat16)
    kv = jnp.dot(kvl, wukv_ref[...], preferred_element_type=jnp.float32)
    t = z[:, O_KR:O_KR + LANES] * rot
    krr = t + pltpu.roll(t, 2 * QK_ROPE_DIM // 2, axis=1)
    k_parts, v_parts = [], []
    for hd in range(MLA_HEADS):
        base = hd * 2 * LANES
        k_parts.append(kv[:, base:base + LANES])
        k_parts.append(krr)
        v_parts.append(kv[:, base + LANES:base + 2 * LANES])
    k_ref[...] = jnp.concatenate(k_parts, axis=-1).astype(jnp.bfloat16)
    v_ref[...] = jnp.concatenate(v_parts, axis=-1).astype(jnp.bfloat16)


def _prep(x2, pos2, ln0g, ln0b, win, gmg, gmb, ws, bs, qg, wuq, kvg, wukv, freq, phase, sign):
    tb = PREP_TOKENS
    full = lambda shape: pl.BlockSpec(shape, lambda i: (0,) * len(shape))
    tok = lambda cols: pl.BlockSpec((tb, cols), lambda i: (i, 0))
    return pl.pallas_call(
        _prep_kernel,
        grid=(N_TOK // tb,),
        in_specs=[tok(D_MODEL), pl.BlockSpec((1, 1, tb), lambda i: (i, 0, 0)), full((1, D_MODEL)), full((1, D_MODEL)),
                  full((D_MODEL, IN_COLS)), full((1, GM_WIDTH)), full((1, GM_WIDTH)),
                  full((GM_HEADS, GM_CHUNK, GM_CHUNK)), full((GM_HEADS, GM_CHUNK, GM_HEAD_DIM)),
                  full((1, Q_LORA_RANK)), full((Q_LORA_RANK, D_MODEL)),
                  full((1, KV_LORA_RANK)), full((KV_LORA_RANK, D_MODEL)),
                  full((1, LANES)), full((1, LANES)), full((1, LANES))],
        out_specs=[tok(D_MODEL), tok(GM_WIDTH), tok(D_MODEL), tok(D_MODEL), tok(GM_WIDTH)],
        out_shape=[jax.ShapeDtypeStruct((N_TOK, D_MODEL), jnp.float32),
                   jax.ShapeDtypeStruct((N_TOK, GM_WIDTH), jnp.bfloat16),
                   jax.ShapeDtypeStruct((N_TOK, D_MODEL), jnp.bfloat16),
                   jax.ShapeDtypeStruct((N_TOK, D_MODEL), jnp.bfloat16),
                   jax.ShapeDtypeStruct((N_TOK, GM_WIDTH), jnp.bfloat16)],
        scratch_shapes=[pltpu.VMEM((tb, LANES), jnp.float32)] * 3,
        compiler_params=pltpu.CompilerParams(
            dimension_semantics=("arbitrary",), vmem_limit_bytes=VMEM_LIMIT),
        name="prep",
    )(x2, pos2, ln0g, ln0b, win, gmg, gmb, ws, bs, qg, wuq, kvg, wukv, freq, phase, sign)


def _attn_kernel(q_ref, k_ref, v_ref, o_ref, s_ref, mx_ref, ls_ref, acc_ref):
    qi = pl.program_id(1)
    tq = q_ref.shape[0]
    nt = (((1,), (1,)), ((), ()))
    n_kv = SEQ // ATT_K

    half = tq // 2
    row_chunk = (lax.broadcasted_iota(jnp.int32, (half, half), 0)) // CHUNK
    col_chunk = (lax.broadcasted_iota(jnp.int32, (half, half), 1)) // CHUNK
    quad_allowed = col_chunk <= row_chunk
    diag_start = pl.multiple_of(qi * ATT_K, ATT_K)
    masked = jnp.float32(-1e30)

    def lane_tiles(x):
        return [x[:, t * LANES:(t + 1) * LANES] for t in range(x.shape[1] // LANES)]

    def lane_fold(x, op):
        return functools.reduce(op, lane_tiles(x))

    def scores(hd, j):
        start = pl.multiple_of(j * ATT_K, ATT_K)
        q_h = q_ref[:, hd * 2 * LANES:(hd + 1) * 2 * LANES]
        kb = k_ref[pl.ds(start, ATT_K), hd * 2 * LANES:(hd + 1) * 2 * LANES]
        return lax.dot_general(q_h, kb, nt, preferred_element_type=jnp.float32)

    for hd in range(MLA_HEADS):
        cols = slice(hd * 2 * LANES, (hd + 1) * 2 * LANES)
        k_lo = k_ref[pl.ds(diag_start, half), cols]
        k_hi = k_ref[pl.ds(diag_start + half, half), cols]
        qk = lambda q, k: lax.dot_general(q, k, nt, preferred_element_type=jnp.float32)
        s_tl = jnp.where(quad_allowed, qk(q_ref[:half, cols], k_lo), masked)
        s_bl = qk(q_ref[half:, cols], k_lo)
        s_br = jnp.where(quad_allowed, qk(q_ref[half:, cols], k_hi), masked)
        diag = s_ref.at[hd * n_kv + qi]
        diag[:half, :half] = s_tl
        diag[half:, :half] = s_bl
        diag[half:, half:] = s_br
        mx_ref[hd, :half] = lane_fold(s_tl, jnp.maximum)
        mx_ref[hd, half:] = jnp.maximum(lane_fold(s_bl, jnp.maximum), lane_fold(s_br, jnp.maximum))

    def pass_a(j, c):
        for hd in range(MLA_HEADS):
            s = scores(hd, j)
            s_ref[hd * n_kv + j] = s
            mx_ref[hd] = jnp.maximum(mx_ref[hd], lane_fold(s, jnp.maximum))
        return c

    lax.fori_loop(0, qi, pass_a, 0)

    for hd in range(MLA_HEADS):
        mx_ref[hd] = jnp.broadcast_to(jnp.max(mx_ref[hd], axis=-1, keepdims=True), (tq, LANES))

    def probs(hd, j):
        s = s_ref[hd * n_kv + j]
        mb = mx_ref[hd]
        p = jnp.exp2(jnp.concatenate([t - mb for t in lane_tiles(s)], axis=-1))
        start = pl.multiple_of(j * ATT_K, ATT_K)
        vb = v_ref[pl.ds(start, ATT_K), hd * LANES:(hd + 1) * LANES]
        pv = jnp.dot(p.astype(jnp.bfloat16), vb, preferred_element_type=jnp.float32)
        return lane_fold(p, jnp.add), pv

    for hd in range(MLA_HEADS):
        diag = s_ref.at[hd * n_kv + qi]
        vcols = slice(hd * LANES, (hd + 1) * LANES)
        p_top = jnp.exp2(jnp.concatenate([t - mx_ref[hd, :half] for t in lane_tiles(diag[:half, :half])], axis=-1))
        p_bot = jnp.exp2(jnp.concatenate([t - mx_ref[hd, half:] for t in lane_tiles(diag[half:, :])], axis=-1))
        ls_ref[hd, :half] = lane_fold(p_top, jnp.add)
        ls_ref[hd, half:] = lane_fold(p_bot, jnp.add)
        acc_ref[hd, :half] = jnp.dot(p_top.astype(jnp.bfloat16), v_ref[pl.ds(diag_start, half), vcols],
                                     preferred_element_type=jnp.float32)
        acc_ref[hd, half:] = jnp.dot(p_bot.astype(jnp.bfloat16), v_ref[pl.ds(diag_start, ATT_K), vcols],
                                     preferred_element_type=jnp.float32)

    def pass_b(j, c):
        for hd in range(MLA_HEADS):
            ls, pv = probs(hd, j)
            ls_ref[hd] = ls_ref[hd] + ls
            acc_ref[hd] = acc_ref[hd] + pv
        return c

    lax.fori_loop(0, qi, pass_b, 0)

    for hd in range(MLA_HEADS):
        l = jnp.sum(ls_ref[hd], axis=-1, keepdims=True)
        o_ref[:, hd * V_HEAD_DIM:(hd + 1) * V_HEAD_DIM] = (acc_ref[hd] / l).astype(jnp.bfloat16)


def _attn(q, k, v):
    tq = ATT_Q
    tokblk = lambda cols: pl.BlockSpec((tq, cols), lambda b, i: (b * N_QBLK + i, 0))
    seqblk = lambda cols: pl.BlockSpec((SEQ, cols), lambda b, i: (b, 0))
    return pl.pallas_call(
        _attn_kernel,
        grid=(BATCH, N_QBLK),
        in_specs=[tokblk(D_MODEL), seqblk(D_MODEL), seqblk(GM_WIDTH)],
        out_specs=tokblk(GM_WIDTH),
        out_shape=jax.ShapeDtypeStruct((N_TOK, MLA_HEADS * V_HEAD_DIM), jnp.bfloat16),
        scratch_shapes=[pltpu.VMEM((MLA_HEADS * (SEQ // ATT_K), tq, ATT_K), jnp.float32),
                        pltpu.VMEM((MLA_HEADS, tq, LANES), jnp.float32),
                        pltpu.VMEM((MLA_HEADS, tq, LANES), jnp.float32),
                        pltpu.VMEM((MLA_HEADS, tq, V_HEAD_DIM), jnp.float32)],
        compiler_params=pltpu.CompilerParams(
            dimension_semantics=("arbitrary", "arbitrary"), vmem_limit_bytes=VMEM_LIMIT),
        name="attn",
    )(q, k, v)


def _proj_kernel(outa_ref, ob_ref, h_hbm_ref, wout_ref, ln1g_ref, ln1b_ref, wr_ref, br_ref,
                 h1_ref, h1b_ref, ri_ref, rf_ref, proj_ref, hring_ref, hsems):
    i = pl.program_id(0)
    tb = outa_ref.shape[0]
    h_ref = _ring_block(h_hbm_ref, hring_ref, hsems, lambda step: jnp.maximum(step - 1, 0), tb)

    @pl.when(i == 0)
    def _():
        proj_ref[...] = jnp.zeros_like(proj_ref)

    h1 = _layer_norm(ALPHA * h_ref[...] + proj_ref[(i + 1) % 2], ln1g_ref[...], ln1b_ref[...])
    h1_ref[...] = h1
    h1b_ref[...] = h1.astype(jnp.bfloat16)

    logits_tm = jnp.dot(h1b_ref[...], wr_ref[...], preferred_element_type=jnp.float32)
    logits = logits_tm.T[0:ROUTER_ROWS] + br_ref[...]

    sub_i = lax.broadcasted_iota(jnp.int32, (SUBLANES, tb), 0)
    sub = sub_i.astype(jnp.float32)
    neg = jnp.float32(-jnp.inf)
    g = jnp.where(sub_i < N_GROUPS, logits[0:SUBLANES], neg)
    gmax = jnp.max(g, axis=0, keepdims=True)
    g_top = jnp.min(jnp.where(g == gmax, sub, float(SUBLANES)), axis=0, keepdims=True)
    p_group = 1.0 / jnp.sum(jnp.exp(g - gmax), axis=0, keepdims=True)
    sel = logits[SUBLANES:2 * SUBLANES]
    for grp in range(1, N_GROUPS):
        sel = jnp.where(g_top == float(grp), logits[(grp + 1) * SUBLANES:(grp + 2) * SUBLANES], sel)
    v1 = jnp.max(sel, axis=0, keepdims=True)
    i1 = jnp.min(jnp.where(sel == v1, sub, float(SUBLANES)), axis=0, keepdims=True)
    sel2 = jnp.where(sub == i1, neg, sel)
    v2 = jnp.max(sel2, axis=0, keepdims=True)
    i2 = jnp.min(jnp.where(sel2 == v2, sub, float(SUBLANES)), axis=0, keepdims=True)
    e21 = jnp.exp(v2 - v1)
    w1 = 1.0 / (1.0 + e21)
    gate1 = p_group * w1
    gate2 = p_group * (e21 * w1)
    e1 = g_top * EXPERTS_PER_GROUP + i1
    e2 = g_top * EXPERTS_PER_GROUP + i2
    ri_ref[...] = jnp.where(sub_i == 0, e1, jnp.where(sub_i == 1, e2, 0.0)).astype(jnp.int32)
    rf_ref[...] = jnp.where(sub_i == 0, gate1, jnp.where(sub_i == 1, gate2, 0.0))

    proj_ref[i % 2] = (jnp.dot(outa_ref[...], wout_ref[:GM_WIDTH, :], preferred_element_type=jnp.float32)
                       + jnp.dot(ob_ref[...], wout_ref[GM_WIDTH:, :], preferred_element_type=jnp.float32))


def _proj(outa, ob, h, wout, ln1g, ln1b, wr, br):
    tb = PROJ_TOKENS
    n_blk = N_TOK // tb
    cur = lambda i: jnp.minimum(i, n_blk - 1)
    prev = lambda i: jnp.maximum(i - 1, 0)
    full = lambda shape: pl.BlockSpec(shape, lambda i: (0,) * len(shape))
    return pl.pallas_call(
        _proj_kernel,
        grid=(n_blk + 1,),
        in_specs=[pl.BlockSpec((tb, GM_WIDTH), lambda i: (cur(i), 0)),
                  pl.BlockSpec((tb, GM_WIDTH), lambda i: (cur(i), 0)),
                  pl.BlockSpec(memory_space=pl.ANY),
                  full((D_MODEL, D_MODEL)), full((1, D_MODEL)), full((1, D_MODEL)),
                  full((D_MODEL, LANES)), full((ROUTER_ROWS, 1))],
        out_specs=[pl.BlockSpec((tb, D_MODEL), lambda i: (prev(i), 0)),
                   pl.BlockSpec((tb, D_MODEL), lambda i: (prev(i), 0)),
                   pl.BlockSpec((SUBLANES, tb), lambda i: (0, prev(i))),
                   pl.BlockSpec((SUBLANES, tb), lambda i: (0, prev(i)))],
        out_shape=[jax.ShapeDtypeStruct((N_TOK, D_MODEL), jnp.float32),
                   jax.ShapeDtypeStruct((N_TOK, D_MODEL), jnp.bfloat16),
                   jax.ShapeDtypeStruct((SUBLANES, N_TOK), jnp.int32),
                   jax.ShapeDtypeStruct((SUBLANES, N_TOK), jnp.float32)],
        scratch_shapes=[pltpu.VMEM((2, tb, D_MODEL), jnp.float32)] + _ring_scratch(tb, D_MODEL, jnp.float32),
        compiler_params=pltpu.CompilerParams(
            dimension_semantics=("arbitrary",), vmem_limit_bytes=VMEM_LIMIT),
        name="proj",
    )(outa, ob, h, wout, ln1g, ln1b, wr, br)


def _plan_kernel(ri_all_ref, ri_ref, ldest_ref, runs_ref, meta_ref, run_ref, start_ref, upper_ref):
    step = pl.program_id(0)
    tb = ri_ref.shape[1]
    f32 = jnp.float32
    er = lax.broadcasted_iota(jnp.int32, (N_EXPERTS, LANES), 0)
    ec = lax.broadcasted_iota(jnp.int32, (N_EXPERTS, LANES), 1)
    to_row = lambda col: jnp.sum(jnp.where(er == ec, col, 0.0), axis=0, keepdims=True)

    def expert_one_hot(ref):
        e_sub = lax.broadcasted_iota(jnp.int32, (N_EXPERTS, ref.shape[1]), 0)
        return e_sub == ref[0:1, :], e_sub == ref[1:2, :]

    @pl.when(step == 0)
    def _():
        oh1, oh2 = expert_one_hot(ri_all_ref)
        counts = jnp.sum(jnp.where(oh1 | oh2, 1.0, 0.0), axis=1, keepdims=True)
        padded = jnp.floor((counts + (EXPERT_ROWS - 1)) * (1.0 / EXPERT_ROWS)) * EXPERT_ROWS
        pad_end = jnp.sum(jnp.where(ec <= er, to_row(padded), 0.0), axis=1, keepdims=True)
        start_ref[...] = jnp.broadcast_to(pad_end - padded, start_ref.shape)
        run_ref[...] = jnp.zeros_like(run_ref)
        bstart = (lax.broadcasted_iota(jnp.int32, (N_EXPERTS, META_LANES), 1) * EXPERT_ROWS).astype(f32)
        blk_e = jnp.sum(jnp.where(pad_end <= bstart, 1.0, 0.0), axis=0, keepdims=True)
        blk_e = jnp.minimum(blk_e, N_EXPERTS - 1.0)
        n_used = pad_end[N_EXPERTS - 1:N_EXPERTS, :] * (1.0 / EXPERT_ROWS)
        pad3 = lambda r: jnp.concatenate(
            [r, jnp.zeros((1, META_LANES - LANES), f32)], axis=1)
        msub = lax.broadcasted_iota(jnp.int32, (SUBLANES, META_LANES), 0)
        meta = jnp.where(msub == 0, blk_e,
                         jnp.where(msub == 1, pad3(to_row(pad_end)),
                                   jnp.where(msub == 2, pad3(to_row(counts)),
                                             jnp.where(msub == 3, n_used, 0.0))))
        meta_ref[...] = meta.astype(jnp.int32)
        tr = lax.broadcasted_iota(jnp.int32, (tb, tb), 0)
        tc = lax.broadcasted_iota(jnp.int32, (tb, tb), 1)
        upper_ref[...] = jnp.where(tr < tc, 1.0, 0.0).astype(jnp.bfloat16)

    @pl.when(step > 0)
    def _():
        oh1, oh2 = expert_one_hot(ri_ref)
        oh = jnp.where(oh1 | oh2, 1.0, 0.0).astype(f32)
        blk_count = jnp.sum(oh, axis=1, keepdims=True)
        prefix = jnp.dot(oh.astype(jnp.bfloat16), upper_ref[...], preferred_element_type=f32)
        cnt_row = to_row(blk_count)
        lstart = jnp.sum(jnp.where(ec < er, cnt_row, 0.0), axis=1, keepdims=True)
        base = prefix + lstart
        d1 = jnp.sum(jnp.where(oh1, base, 0.0), axis=0, keepdims=True)
        d2 = jnp.sum(jnp.where(oh2, base, 0.0), axis=0, keepdims=True)
        sub = lax.broadcasted_iota(jnp.int32, (SUBLANES, tb), 0)
        ldest_ref[...] = jnp.where(sub == 0, d1, jnp.where(sub == 1, d2, 0.0)).astype(jnp.int32)
        gstart = start_ref[:, 0:1] + run_ref[:, 0:1]
        rsub = lax.broadcasted_iota(jnp.int32, (SUBLANES, LANES), 0)
        runs = jnp.where(rsub == 0, cnt_row,
                         jnp.where(rsub == 1, to_row(lstart), jnp.where(rsub == 2, to_row(gstart), 0.0)))
        runs_ref[...] = runs.astype(jnp.int32)
        run_ref[...] = run_ref[...] + blk_count


def _plan(ri):
    tb = MOE_TOKENS
    blk = lambda i: jnp.maximum(i - 1, 0)
    return pl.pallas_call(
        _plan_kernel,
        grid=(N_MOE_BLOCKS + 1,),
        in_specs=[pl.BlockSpec((SUBLANES, N_TOK), lambda i: (0, 0)),
                  pl.BlockSpec((SUBLANES, tb), lambda i: (0, blk(i)))],
        out_specs=[pl.BlockSpec((SUBLANES, tb), lambda i: (0, blk(i))),
                   pl.BlockSpec((SUBLANES, LANES), lambda i: (blk(i), 0)),
                   pl.BlockSpec((SUBLANES, META_LANES), lambda i: (0, 0))],
        out_shape=[jax.ShapeDtypeStruct((SUBLANES, N_TOK), jnp.int32),
                   jax.ShapeDtypeStruct((N_MOE_BLOCKS * SUBLANES, LANES), jnp.int32),
                   jax.ShapeDtypeStruct((SUBLANES, META_LANES), jnp.int32)],
        scratch_shapes=[pltpu.VMEM((N_EXPERTS, LANES), jnp.float32),
                        pltpu.VMEM((N_EXPERTS, LANES), jnp.float32),
                        pltpu.VMEM((tb, tb), jnp.bfloat16)],
        compiler_params=pltpu.CompilerParams(
            dimension_semantics=("arbitrary",), vmem_limit_bytes=VMEM_LIMIT),
        name="plan",
    )(ri, ri)


def _for_each_run_piece(runs_ref, fn):
    for e in range(N_EXPERTS):
        n, lstart, gstart = runs_ref[0, e], runs_ref[1, e], runs_ref[2, e]
        for bit in range(RUN_BITS):
            @pl.when((n & (1 << bit)) != 0)
            def _(n=n, lstart=lstart, gstart=gstart, bit=bit):
                off = (n >> (bit + 1)) << (bit + 1)
                fn(lstart + off, gstart + off, 1 << bit)


def _tile_rows(ref, row, rows):
    return ref.at[pl.ds(pl.multiple_of(row * FEAT_TILES, FEAT_TILES), rows * FEAT_TILES)]


def _dispatch_kernel(meta_ref, runs_ref, ldest_ref, h1b_ref, buf_ref, sorted_ref, zero_ref, sems, zsem):
    i = pl.program_id(0)
    n_steps = pl.num_programs(0)
    tb = ldest_ref.shape[1]
    slot = i % 2
    block_tiles = 2 * tb * FEAT_TILES

    def wait_slot(s):
        pltpu.make_async_copy(sorted_ref.at[s], buf_ref.at[pl.ds(0, block_tiles)], sems.at[s]).wait()

    @pl.when(i == 0)
    def _():
        zero_ref[...] = jnp.zeros_like(zero_ref)

        def zero_copy(e):
            start = pl.multiple_of((meta_ref[1, e] - EXPERT_ROWS) * FEAT_TILES, EXPERT_ROWS * FEAT_TILES)
            return pltpu.make_async_copy(
                zero_ref, buf_ref.at[pl.ds(start, EXPERT_ROWS * FEAT_TILES)], zsem)

        def start_zero(e, c):
            @pl.when(meta_ref[2, e] > 0)
            def _():
                zero_copy(e).start()
            return c

        def wait_zero(e, c):
            @pl.when(meta_ref[2, e] > 0)
            def _():
                zero_copy(e).wait()
            return c

        def tail_copy(b):
            start = pl.multiple_of(b * (EXPERT_ROWS * FEAT_TILES), EXPERT_ROWS * FEAT_TILES)
            return pltpu.make_async_copy(
                zero_ref, buf_ref.at[pl.ds(start, EXPERT_ROWS * FEAT_TILES)], zsem)

        def start_tail(b, c):
            tail_copy(b).start()
            return c

        def wait_tail(b, c):
            tail_copy(b).wait()
            return c

        lax.fori_loop(0, N_EXPERTS, start_zero, 0)
        lax.fori_loop(meta_ref[3, 0], N_ROW_BLOCKS, start_tail, 0)
        lax.fori_loop(0, N_EXPERTS, wait_zero, 0)
        lax.fori_loop(meta_ref[3, 0], N_ROW_BLOCKS, wait_tail, 0)

    @pl.when(i >= 2)
    def _():
        wait_slot(slot)

    x = h1b_ref[...]
    ld0 = ldest_ref[0:1, :]
    ld1 = ldest_ref[1:2, :]
    for c in range(2 * tb // SORT_CHUNK):
        r = lax.broadcasted_iota(jnp.int32, (SORT_CHUNK, tb), 0) + c * SORT_CHUNK
        perm = jnp.where((r == ld0) | (r == ld1), 1.0, 0.0).astype(jnp.bfloat16)
        rows = jnp.dot(perm, x, preferred_element_type=jnp.float32)
        _to_row_tiles(sorted_ref.at[slot, pl.ds(c * SORT_CHUNK * FEAT_TILES, SORT_CHUNK * FEAT_TILES)], rows)

    def send(lrow, grow, rows):
        pltpu.make_async_copy(_tile_rows(sorted_ref.at[slot], lrow, rows),
                              _tile_rows(buf_ref, grow, rows), sems.at[slot]).start()

    _for_each_run_piece(runs_ref, send)

    @pl.when(i == n_steps - 1)
    def _():
        wait_slot(slot)
        wait_slot(1 - slot)


def _dispatch(meta, runs, ldest, h1b):
    tb = MOE_TOKENS
    return pl.pallas_call(
        _dispatch_kernel,
        grid_spec=pltpu.PrefetchScalarGridSpec(
            num_scalar_prefetch=1,
            grid=(N_MOE_BLOCKS,),
            in_specs=[pl.BlockSpec((SUBLANES, LANES), lambda i, m: (i, 0), memory_space=pltpu.SMEM),
                      pl.BlockSpec((SUBLANES, tb), lambda i, m: (0, i)),
                      pl.BlockSpec((tb, D_MODEL), lambda i, m: (i, 0))],
            out_specs=pl.BlockSpec(memory_space=pl.ANY),
            scratch_shapes=[pltpu.VMEM((2, 2 * tb * FEAT_TILES, LANES), jnp.float32),
                            pltpu.VMEM((EXPERT_ROWS * FEAT_TILES, LANES), jnp.float32),
                            pltpu.SemaphoreType.DMA((2,)), pltpu.SemaphoreType.DMA]),
        out_shape=jax.ShapeDtypeStruct((N_ROWS * FEAT_TILES, LANES), jnp.float32),
        compiler_params=pltpu.CompilerParams(
            dimension_semantics=("arbitrary",), vmem_limit_bytes=VMEM_LIMIT),
        name="dispatch",
    )(meta, runs, ldest, h1b)


def _sub_block_expert(meta, i, sub):
    return meta[0, jnp.minimum(i + sub * (N_ROW_BLOCKS // EXPERT_SUB), meta[3, 0] - 1)]


def _experts_kernel(meta_ref, x_hbm_ref, *refs):
    weights = [refs[3 * sub:3 * sub + 3] for sub in range(EXPERT_SUB)]
    o_ref, wgb_ref, wub_ref, wdb_ref, cached_ref, xring_ref, xsems = refs[3 * EXPERT_SUB:]
    n = EXPERT_ROWS
    i = pl.program_id(0)
    n_steps = pl.num_programs(0)

    def rows_copy(step):
        slot = step % EXPERT_RING
        start = pl.multiple_of(step * (n * FEAT_TILES), n * FEAT_TILES)
        return pltpu.make_async_copy(x_hbm_ref.at[:, pl.ds(start, n * FEAT_TILES), :], xring_ref.at[slot],
                                     xsems.at[slot])

    @pl.when(i == 0)
    def _():
        for step in range(EXPERT_RING - 1):
            rows_copy(step).start()

    @pl.when(i + (EXPERT_RING - 1) < n_steps)
    def _():
        rows_copy(i + (EXPERT_RING - 1)).start()

    rows_copy(i).wait()
    x_ref = xring_ref.at[i % EXPERT_RING]

    @pl.when(i == 0)
    def _():
        for sub in range(EXPERT_SUB):
            cached_ref[sub] = -1

    for sub in range(EXPERT_SUB):
        expert = _sub_block_expert(meta_ref, i, sub)

        @pl.when(cached_ref[sub] != expert)
        def _(sub=sub, expert=expert):
            wg_ref, wu_ref, wd_ref = weights[sub]
            wgb_ref[sub] = wg_ref[...].astype(jnp.bfloat16)
            wub_ref[sub] = wu_ref[...].astype(jnp.bfloat16)
            wdb_ref[sub] = wd_ref[...].astype(jnp.bfloat16)
            cached_ref[sub] = expert

    for sub in range(EXPERT_SUB):
        x = _from_row_tiles(x_ref.at[sub], n).astype(jnp.bfloat16)
        gate = jnp.dot(x, wgb_ref[sub], preferred_element_type=jnp.float32)
        up = jnp.dot(x, wub_ref[sub], preferred_element_type=jnp.float32)
        act = (gate * jax.nn.sigmoid(gate) * up).astype(jnp.bfloat16)
        _to_row_tiles(o_ref.at[sub], jnp.dot(act, wdb_ref[sub], preferred_element_type=jnp.float32))


def _experts(meta, buf, wg, wu, wd):
    sub_tiles = (N_ROW_BLOCKS // EXPERT_SUB) * EXPERT_ROWS * FEAT_TILES

    def weight_spec(shape, sub):
        return pl.BlockSpec((None,) + shape, lambda i, m: (_sub_block_expert(m, i, sub), 0, 0))

    rows_shape = (EXPERT_SUB, EXPERT_ROWS * FEAT_TILES, LANES)
    up_shape, down_shape = (D_MODEL, EXPERT_FF), (EXPERT_FF, D_MODEL)
    eout = pl.pallas_call(
        _experts_kernel,
        grid_spec=pltpu.PrefetchScalarGridSpec(
            num_scalar_prefetch=1,
            grid=(N_ROW_BLOCKS // EXPERT_SUB,),
            in_specs=[pl.BlockSpec(memory_space=pl.ANY)] + [
                weight_spec(shape, sub) for sub in range(EXPERT_SUB) for shape in (up_shape, up_shape, down_shape)],
            out_specs=pl.BlockSpec(rows_shape, lambda i, m: (0, i, 0)),
            scratch_shapes=[pltpu.VMEM((EXPERT_SUB,) + up_shape, jnp.bfloat16),
                            pltpu.VMEM((EXPERT_SUB,) + up_shape, jnp.bfloat16),
                            pltpu.VMEM((EXPERT_SUB,) + down_shape, jnp.bfloat16),
                            pltpu.SMEM((EXPERT_SUB,), jnp.int32),
                            pltpu.VMEM((EXPERT_RING,) + rows_shape, jnp.float32),
                            pltpu.SemaphoreType.DMA((EXPERT_RING,))]),
        out_shape=jax.ShapeDtypeStruct((EXPERT_SUB, sub_tiles, LANES), jnp.float32),
        compiler_params=pltpu.CompilerParams(
            dimension_semantics=("arbitrary",), vmem_limit_bytes=VMEM_LIMIT),
        name="experts",
    )(meta, buf.reshape(EXPERT_SUB, sub_tiles, LANES), *([wg, wu, wd] * EXPERT_SUB))
    return eout.reshape(N_ROWS * FEAT_TILES, LANES)


def _combine_kernel(runs_ref, runs_next_ref, ldest_ref, rf_ref, h1_ref, eout_ref, ln2g_ref, ln2b_ref,
                    o_ref, y_ref, sems):
    i = pl.program_id(0)
    n_steps = pl.num_programs(0)
    tb = ldest_ref.shape[1]
    slot = i % 2
    block_tiles = 2 * tb * FEAT_TILES

    def fetch(table_ref, s):
        def recv(lrow, grow, rows):
            pltpu.make_async_copy(_tile_rows(eout_ref, grow, rows),
                                  _tile_rows(y_ref.at[s], lrow, rows), sems.at[s]).start()
        _for_each_run_piece(table_ref, recv)

    @pl.when(i == 0)
    def _():
        fetch(runs_ref, slot)

    @pl.when(i + 1 < n_steps)
    def _():
        fetch(runs_next_ref, 1 - slot)

    pltpu.make_async_copy(eout_ref.at[pl.ds(0, block_tiles)], y_ref.at[slot], sems.at[slot]).wait()

    ld = ldest_ref[...].astype(jnp.float32).T
    gates = rf_ref[...].T
    y = None
    col = lax.broadcasted_iota(jnp.int32, (tb, SORT_CHUNK), 1).astype(jnp.float32)
    for c in range(2 * tb // SORT_CHUNK):
        ld_c = ld - float(c * SORT_CHUNK)
        g = jnp.where(col == ld_c[:, 0:1], gates[:, 0:1],
                      jnp.where(col == ld_c[:, 1:2], gates[:, 1:2], 0.0)).astype(jnp.bfloat16)
        rows = _from_row_tiles(
            y_ref.at[slot, pl.ds(c * SORT_CHUNK * FEAT_TILES, SORT_CHUNK * FEAT_TILES)], SORT_CHUNK)
        part = jnp.dot(g, rows.astype(jnp.bfloat16), preferred_element_type=jnp.float32)
        y = part if y is None else y + part
    o_ref[...] = _layer_norm(ALPHA * h1_ref[...] + y, ln2g_ref[...], ln2b_ref[...])


def _combine(runs, ldest, rf, h1, eout, ln2g, ln2b):
    tb = MOE_TOKENS
    last = N_MOE_BLOCKS - 1
    return pl.pallas_call(
        _combine_kernel,
        grid=(N_MOE_BLOCKS,),
        in_specs=[pl.BlockSpec((SUBLANES, LANES), lambda i: (i, 0), memory_space=pltpu.SMEM),
                  pl.BlockSpec((SUBLANES, LANES), lambda i: (jnp.minimum(i + 1, last), 0),
                               memory_space=pltpu.SMEM),
                  pl.BlockSpec((SUBLANES, tb), lambda i: (0, i)),
                  pl.BlockSpec((SUBLANES, tb), lambda i: (0, i)),
                  pl.BlockSpec((tb, D_MODEL), lambda i: (i, 0)),
                  pl.BlockSpec(memory_space=pl.ANY),
                  pl.BlockSpec((1, D_MODEL), lambda i: (0, 0)),
                  pl.BlockSpec((1, D_MODEL), lambda i: (0, 0))],
        out_specs=pl.BlockSpec((tb, D_MODEL), lambda i: (i, 0)),
        out_shape=jax.ShapeDtypeStruct((N_TOK, D_MODEL), jnp.float32),
        scratch_shapes=[pltpu.VMEM((2, 2 * tb * FEAT_TILES, LANES), jnp.float32),
                        pltpu.SemaphoreType.DMA((2,))],
        compiler_params=pltpu.CompilerParams(
            dimension_semantics=("arbitrary",), vmem_limit_bytes=VMEM_LIMIT),
        name="combine",
    )(runs, runs, ldest, rf, h1, eout, ln2g, ln2b)


def _swap_halves(w):
    half = w.shape[-1] // 2
    return jnp.concatenate([w[..., half:], w[..., :half]], axis=-1)


def kernel(x, positions, ln0_g, ln0_b, w_in, gm_ln_g, gm_ln_b, w_spatial, b_spatial, q_norm_g, w_uq, kv_norm_g, w_ukv, w_out, ln1_g, ln1_b, w_router_group, b_router_group, w_router_expert, b_router_expert, w_gate, w_up, w_down, ln2_g, ln2_b):
    bf16 = jnp.bfloat16
    row = lambda a: a.reshape(1, -1)

    w_in0 = w_in[0]
    kr_cols = w_in0[:, O_KR:O_KR + QK_ROPE_DIM]
    win = jnp.concatenate([w_in0, _swap_halves(kr_cols)], axis=1).astype(bf16)
    wuq3 = w_uq[0].reshape(Q_LORA_RANK, MLA_HEADS, QK_NOPE_DIM + QK_ROPE_DIM)
    rope_cols = wuq3[:, :, QK_NOPE_DIM:]
    wuq = jnp.concatenate([wuq3, _swap_halves(rope_cols)], axis=-1).reshape(Q_LORA_RANK, D_MODEL).astype(bf16)
    wukv = w_ukv[0].astype(bf16)
    wout = w_out[0].astype(bf16)
    bs = jnp.broadcast_to(b_spatial[0][:, :, None], (GM_HEADS, GM_CHUNK, GM_HEAD_DIM))
    wr = jnp.concatenate([w_router_group[0], jnp.zeros((D_MODEL, SUBLANES - N_GROUPS), jnp.float32),
                          w_router_expert[0],
                          jnp.zeros((D_MODEL, LANES - ROUTER_ROWS), jnp.float32)],
                         axis=1).astype(bf16)
    br = jnp.concatenate([b_router_group[0], jnp.zeros((SUBLANES - N_GROUPS,), jnp.float32),
                          b_router_expert[0]]).reshape(ROUTER_ROWS, 1)

    inv_freq = ROPE_THETA ** (-jnp.arange(0, QK_ROPE_DIM, 2, dtype=jnp.float32) / QK_ROPE_DIM)
    freq = jnp.tile(inv_freq, 4).reshape(1, LANES)
    quarter = QK_ROPE_DIM // 2
    phase = jnp.concatenate([jnp.zeros((2 * quarter,), jnp.float32),
                             jnp.full((2 * quarter,), math.pi / 2, jnp.float32)]).reshape(1, LANES)
    sign = jnp.concatenate([jnp.ones((2 * quarter,), jnp.float32), -jnp.ones((quarter,), jnp.float32),
                            jnp.ones((quarter,), jnp.float32)]).reshape(1, LANES)

    x2 = x.reshape(N_TOK, D_MODEL)
    pos2 = positions.reshape(N_TOK // PREP_TOKENS, 1, PREP_TOKENS)

    h, outa, q, k, v = _prep(x2, pos2, row(ln0_g), row(ln0_b), win, row(gm_ln_g[0]), row(gm_ln_b[0]),
                             w_spatial[0], bs, row(q_norm_g[0]), wuq, row(kv_norm_g[0]), wukv,
                             freq, phase, sign)
    ob = _attn(q, k, v)
    h1, h1b, ri, rf = _proj(outa, ob, h, wout, row(ln1_g[0]), row(ln1_b[0]), wr, br)
    ldest, runs, meta = _plan(ri)
    buf = _dispatch(meta, runs, ldest, h1b)
    eout = _experts(meta, buf, w_gate[0], w_up[0], w_down[0])
    out = _combine(runs, ldest, rf, h1, eout, row(ln2_g[0]), row(ln2_b[0]))
    return out.reshape(BATCH, SEQ, D_MODEL)
```

```python
import functools
import math

import jax
import jax.numpy as jnp
from jax import lax
from jax.experimental import pallas as pl
from jax.experimental.pallas import tpu as pltpu

D_MODEL = 1024
BATCH = 16
SEQ = 2048
N_TOK = BATCH * SEQ
CHUNK = 64
GM_WIDTH = 512
GM_HEADS = 4
GM_HEAD_DIM = 128
GM_CHUNK = 128
MLA_HEADS = 4
QK_NOPE_DIM = 128
QK_ROPE_DIM = 64
V_HEAD_DIM = 128
Q_LORA_RANK = 384
KV_LORA_RANK = 256
ROPE_THETA = 10000.0
N_GROUPS = 4
EXPERTS_PER_GROUP = 8
N_EXPERTS = 32
TOP_K = 2
EXPERT_FF = 256
ALPHA = 2.0 ** 0.25
QK_SCALE = (QK_NOPE_DIM + QK_ROPE_DIM) ** -0.5 * math.log2(math.e)

LANES = 128
SUBLANES = 8
FEAT_TILES = D_MODEL // LANES
PREP_TOKENS = 1024
ATT_Q = 512
ATT_K = 512
N_QBLK = SEQ // ATT_Q
PROJ_TOKENS = 1024
MOE_TOKENS = 512
N_MOE_BLOCKS = N_TOK // MOE_TOKENS
RUN_BITS = (TOP_K * MOE_TOKENS).bit_length()
SORT_CHUNK = 256
EXPERT_ROWS = 256
EXPERT_SUB = 3
EXPERT_RING = 3
INPUT_RING = 3
N_ROWS = N_TOK * TOP_K + N_EXPERTS * EXPERT_ROWS
N_ROW_BLOCKS = N_ROWS // EXPERT_ROWS
META_LANES = 384
IN_COLS = 2 * GM_WIDTH + Q_LORA_RANK + KV_LORA_RANK + 2 * QK_ROPE_DIM
O_Q = 2 * GM_WIDTH
O_KV = O_Q + Q_LORA_RANK
O_KR = O_KV + KV_LORA_RANK
ROUTER_ROWS = 40
VMEM_LIMIT = 48 * 1024 * 1024

assert N_ROW_BLOCKS <= META_LANES and N_ROW_BLOCKS % EXPERT_SUB == 0


def _layer_norm(x, g, b, eps=1e-5):
    mu = jnp.mean(x, axis=-1, keepdims=True)
    xc = x - mu
    var = jnp.mean(xc * xc, axis=-1, keepdims=True)
    return xc * lax.rsqrt(var + eps) * g + b


def _rms_norm(x, g, eps=1e-6):
    return x * lax.rsqrt(jnp.mean(x * x, axis=-1, keepdims=True) + eps) * g


def _gelu_tanh(x):
    c = math.sqrt(2.0 / math.pi)
    return 0.5 * x * (1.0 + jnp.tanh(c * (x + 0.044715 * (x * x * x))))


def _to_row_tiles(ref, x):
    n = x.shape[0]
    for s in range(FEAT_TILES):
        ref[pl.ds(s, n, stride=FEAT_TILES), :] = x[:, s * LANES:(s + 1) * LANES]


def _from_row_tiles(ref, n):
    return jnp.concatenate(
        [ref[pl.ds(s, n, stride=FEAT_TILES), :] for s in range(FEAT_TILES)], axis=-1)


def _ring_block(hbm_ref, ring_ref, sems, block_of_step, rows):
    i = pl.program_id(0)
    n_steps = pl.num_programs(0)

    def copy(step):
        start = pl.multiple_of(block_of_step(step) * rows, rows)
        slot = step % INPUT_RING
        return pltpu.make_async_copy(hbm_ref.at[pl.ds(start, rows)], ring_ref.at[slot], sems.at[slot])

    @pl.when(i == 0)
    def _():
        for step in range(INPUT_RING - 1):
            copy(step).start()

    @pl.when(i + (INPUT_RING - 1) < n_steps)
    def _():
        copy(i + (INPUT_RING - 1)).start()

    copy(i).wait()
    return ring_ref.at[i % INPUT_RING]


def _ring_scratch(rows, cols, dtype):
    return [pltpu.VMEM((INPUT_RING, rows, cols), dtype), pltpu.SemaphoreType.DMA((INPUT_RING,))]


def _prep_kernel(x_ref, pos_ref, ln0g_ref, ln0b_ref, win_ref, gmg_ref, gmb_ref, ws_ref, bs_ref,
                 qg_ref, wuq_ref, kvg_ref, wukv_ref, freq_ref, phase_ref, sign_ref,
                 h_ref, outa_ref, q_ref, k_ref, v_ref, tabc_ref, tabs_ref, rot_ref):
    tb = x_ref.shape[0]

    @pl.when(pl.program_id(0) == 0)
    def _():
        d = lax.broadcasted_iota(jnp.int32, (tb, LANES), 0).astype(jnp.float32) * freq_ref[...]
        tabc_ref[...] = jnp.cos(d)
        tabs_ref[...] = jnp.sin(d)

    pos_row = pos_ref[0]
    p0 = pos_row[:, 0:1]
    offset = lax.broadcasted_iota(jnp.int32, (1, tb), 1)
    consecutive = jnp.max(jnp.abs((pos_row - p0 - offset).astype(jnp.float32))) == 0.0

    @pl.when(consecutive)
    def _():
        a0 = p0.astype(jnp.float32) * freq_ref[...]
        c0, s0 = jnp.cos(a0), jnp.sin(a0)
        lane = lax.broadcasted_iota(jnp.int32, (1, LANES), 1)
        coef_c = jnp.where(lane < 2 * 32, c0, jnp.where(lane < 3 * 32, -s0, s0))
        coef_s = jnp.where(lane < 2 * 32, -s0, jnp.where(lane < 3 * 32, -c0, c0))
        rot_ref[...] = coef_c * tabc_ref[...] + coef_s * tabs_ref[...]

    @pl.when(jnp.logical_not(consecutive))
    def _():
        pos_col = jnp.broadcast_to(pos_row.astype(jnp.float32), (SUBLANES, tb)).T[:, 0:1]
        ang = pos_col * freq_ref[...]
        rot_ref[...] = jnp.cos(ang - phase_ref[...]) * sign_ref[...]

    rot = rot_ref[...]

    h = _layer_norm(x_ref[...], ln0g_ref[...], ln0b_ref[...])
    h_ref[...] = h
    z = jnp.dot(h.astype(jnp.bfloat16), win_ref[...], preferred_element_type=jnp.float32)

    u = _gelu_tanh(z[:, :GM_WIDTH])
    v = _gelu_tanh(z[:, GM_WIDTH:2 * GM_WIDTH])
    row_chunk = lax.broadcasted_iota(jnp.int32, (GM_CHUNK, GM_CHUNK), 0) // CHUNK
    col_chunk = lax.broadcasted_iota(jnp.int32, (GM_CHUNK, GM_CHUNK), 1) // CHUNK
    allowed = col_chunk <= row_chunk
    for hd in range(GM_HEADS):
        lo, hi = hd * GM_HEAD_DIM, (hd + 1) * GM_HEAD_DIM
        vln = _layer_norm(v[:, lo:hi], gmg_ref[:, lo:hi], gmb_ref[:, lo:hi]).astype(jnp.bfloat16)
        wm = jnp.where(allowed, ws_ref[hd], 0.0).astype(jnp.bfloat16)
        for c in range(tb // GM_CHUNK):
            r0, r1 = c * GM_CHUNK, (c + 1) * GM_CHUNK
            f = jnp.dot(wm, vln[r0:r1], preferred_element_type=jnp.float32) + bs_ref[hd]
            outa_ref[r0:r1, lo:hi] = (u[r0:r1, lo:hi] * f).astype(jnp.bfloat16)

    ql = _rms_norm(z[:, O_Q:O_KV], qg_ref[...]).astype(jnp.bfloat16)
    qf = jnp.dot(ql, wuq_ref[...], preferred_element_type=jnp.float32)
    rot_s = rot * QK_SCALE
    q_parts = []
    for hd in range(MLA_HEADS):
        base = hd * 2 * LANES
        q_parts.append(qf[:, base:base + LANES] * QK_SCALE)
        q_parts.append(qf[:, base + LANES:base + 2 * LANES] * rot_s)
    q_ref[...] = jnp.concatenate(q_parts, axis=-1).astype(jnp.bfloat16)

    kvl = _rms_norm(z[:, O_KV:O_KR], kvg_ref[...]).astype(jnp.bfloat16)
    kv = jnp.dot(kvl, wukv_ref[...], preferred_element_type=jnp.float32)
    t = z[:, O_KR:O_KR + LANES] * rot
    krr = t + pltpu.roll(t, 2 * QK_ROPE_DIM // 2, axis=1)
    k_parts, v_parts = [], []
    for hd in range(MLA_HEADS):
        base = hd * 2 * LANES
        k_parts.append(kv[:, base:base + LANES])
        k_parts.append(krr)
        v_parts.append(kv[:, base + LANES:base + 2 * LANES])
    k_ref[...] = jnp.concatenate(k_parts, axis=-1).astype(jnp.bfloat16)
    v_ref[...] = jnp.concatenate(v_parts, axis=-1).astype(jnp.bfloat16)


def _prep(x2, pos2, ln0g, ln0b, win, gmg, gmb, ws, bs, qg, wuq, kvg, wukv, freq, phase, sign):
    tb = PREP_TOKENS
    full = lambda shape: pl.BlockSpec(shape, lambda i: (0,) * len(shape))
    tok = lambda cols: pl.BlockSpec((tb, cols), lambda i: (i, 0))
    return pl.pallas_call(
        _prep_kernel,
        grid=(N_TOK // tb,),
        in_specs=[tok(D_MODEL), pl.BlockSpec((1, 1, tb), lambda i: (i, 0, 0)), full((1, D_MODEL)), full((1, D_MODEL)),
                  full((D_MODEL, IN_COLS)), full((1, GM_WIDTH)), full((1, GM_WIDTH)),
                  full((GM_HEADS, GM_CHUNK, GM_CHUNK)), full((GM_HEADS, GM_CHUNK, GM_HEAD_DIM)),
                  full((1, Q_LORA_RANK)), full((Q_LORA_RANK, D_MODEL)),
                  full((1, KV_LORA_RANK)), full((KV_LORA_RANK, D_MODEL)),
                  full((1, LANES)), full((1, LANES)), full((1, LANES))],
        out_specs=[tok(D_MODEL), tok(GM_WIDTH), tok(D_MODEL), tok(D_MODEL), tok(GM_WIDTH)],
        out_shape=[jax.ShapeDtypeStruct((N_TOK, D_MODEL), jnp.float32),
                   jax.ShapeDtypeStruct((N_TOK, GM_WIDTH), jnp.bfloat16),
                   jax.ShapeDtypeStruct((N_TOK, D_MODEL), jnp.bfloat16),
                   jax.ShapeDtypeStruct((N_TOK, D_MODEL), jnp.bfloat16),
                   jax.ShapeDtypeStruct((N_TOK, GM_WIDTH), jnp.bfloat16)],
        scratch_shapes=[pltpu.VMEM((tb, LANES), jnp.float32)] * 3,
        compiler_params=pltpu.CompilerParams(
            dimension_semantics=("arbitrary",), vmem_limit_bytes=VMEM_LIMIT),
        name="prep",
    )(x2, pos2, ln0g, ln0b, win, gmg, gmb, ws, bs, qg, wuq, kvg, wukv, freq, phase, sign)


def _attn_kernel(q_ref, k_ref, v_ref, o_ref, s_ref, mx_ref, ls_ref, acc_ref):
    qi = pl.program_id(1)
    tq = q_ref.shape[0]
    nt = (((1,), (1,)), ((), ()))
    n_kv = SEQ // ATT_K

    half = tq // 2
    row_chunk = (lax.broadcasted_iota(jnp.int32, (half, half), 0)) // CHUNK
    col_chunk = (lax.broadcasted_iota(jnp.int32, (half, half), 1)) // CHUNK
    quad_allowed = col_chunk <= row_chunk
    diag_start = pl.multiple_of(qi * ATT_K, ATT_K)
    masked = jnp.float32(-1e30)

    def lane_tiles(x):
        return [x[:, t * LANES:(t + 1) * LANES] for t in range(x.shape[1] // LANES)]

    def lane_fold(x, op):
        return functools.reduce(op, lane_tiles(x))

    def scores(hd, j):
        start = pl.multiple_of(j * ATT_K, ATT_K)
        q_h = q_ref[:, hd * 2 * LANES:(hd + 1) * 2 * LANES]
        kb = k_ref[pl.ds(start, ATT_K), hd * 2 * LANES:(hd + 1) * 2 * LANES]
        return lax.dot_general(q_h, kb, nt, preferred_element_type=jnp.float32)

    for hd in range(MLA_HEADS):
        cols = slice(hd * 2 * LANES, (hd + 1) * 2 * LANES)
        k_lo = k_ref[pl.ds(diag_start, half), cols]
        k_hi = k_ref[pl.ds(diag_start + half, half), cols]
        qk = lambda q, k: lax.dot_general(q, k, nt, preferred_element_type=jnp.float32)
        s_tl = jnp.where(quad_allowed, qk(q_ref[:half, cols], k_lo), masked)
        s_bl = qk(q_ref[half:, cols], k_lo)
        s_br = jnp.where(quad_allowed, qk(q_ref[half:, cols], k_hi), masked)
        diag = s_ref.at[hd * n_kv + qi]
        diag[:half, :half] = s_tl
        diag[half:, :half] = s_bl
        diag[half:, half:] = s_br
        mx_ref[hd, :half] = lane_fold(s_tl, jnp.maximum)
        mx_ref[hd, half:] = jnp.maximum(lane_fold(s_bl, jnp.maximum), lane_fold(s_br, jnp.maximum))

    def pass_a(j, c):
        for hd in range(MLA_HEADS):
            s = scores(hd, j)
            s_ref[hd * n_kv + j] = s
            mx_ref[hd] = jnp.maximum(mx_ref[hd], lane_fold(s, jnp.maximum))
        return c

    lax.fori_loop(0, qi, pass_a, 0)

    for hd in range(MLA_HEADS):
        mx_ref[hd] = jnp.broadcast_to(jnp.max(mx_ref[hd], axis=-1, keepdims=True), (tq, LANES))

    def probs(hd, j):
        s = s_ref[hd * n_kv + j]
        mb = mx_ref[hd]
        p = jnp.exp2(jnp.concatenate([t - mb for t in lane_tiles(s)], axis=-1))
        start = pl.multiple_of(j * ATT_K, ATT_K)
        vb = v_ref[pl.ds(start, ATT_K), hd * LANES:(hd + 1) * LANES]
        pv = jnp.dot(p.astype(jnp.bfloat16), vb, preferred_element_type=jnp.float32)
        return lane_fold(p, jnp.add), pv

    for hd in range(MLA_HEADS):
        diag = s_ref.at[hd * n_kv + qi]
        vcols = slice(hd * LANES, (hd + 1) * LANES)
        p_top = jnp.exp2(jnp.concatenate([t - mx_ref[hd, :half] for t in lane_tiles(diag[:half, :half])], axis=-1))
        p_bot = jnp.exp2(jnp.concatenate([t - mx_ref[hd, half:] for t in lane_tiles(diag[half:, :])], axis=-1))
        ls_ref[hd, :half] = lane_fold(p_top, jnp.add)
        ls_ref[hd, half:] = lane_fold(p_bot, jnp.add)
        acc_ref[hd, :half] = jnp.dot(p_top.astype(jnp.bfloat16), v_ref[pl.ds(diag_start, half), vcols],
                                     preferred_element_type=jnp.float32)
        acc_ref[hd, half:] = jnp.dot(p_bot.astype(jnp.bfloat16), v_ref[pl.ds(diag_start, ATT_K), vcols],
                                     preferred_element_type=jnp.float32)

    def pass_b(j, c):
        for hd in range(MLA_HEADS):
            ls, pv = probs(hd, j)
            ls_ref[hd] = ls_ref[hd] + ls
            acc_ref[hd] = acc_ref[hd] + pv
        return c

    lax.fori_loop(0, qi, pass_b, 0)

    for hd in range(MLA_HEADS):
        l = jnp.sum(ls_ref[hd], axis=-1, keepdims=True)
        o_ref[:, hd * V_HEAD_DIM:(hd + 1) * V_HEAD_DIM] = (acc_ref[hd] / l).astype(jnp.bfloat16)


def _attn(q, k, v):
    tq = ATT_Q
    tokblk = lambda cols: pl.BlockSpec((tq, cols), lambda b, i: (b * N_QBLK + i, 0))
    seqblk = lambda cols: pl.BlockSpec((SEQ, cols), lambda b, i: (b, 0))
    return pl.pallas_call(
        _attn_kernel,
        grid=(BATCH, N_QBLK),
        in_specs=[tokblk(D_MODEL), seqblk(D_MODEL), seqblk(GM_WIDTH)],
        out_specs=tokblk(GM_WIDTH),
        out_shape=jax.ShapeDtypeStruct((N_TOK, MLA_HEADS * V_HEAD_DIM), jnp.bfloat16),
        scratch_shapes=[pltpu.VMEM((MLA_HEADS * (SEQ // ATT_K), tq, ATT_K), jnp.float32),
                        pltpu.VMEM((MLA_HEADS, tq, LANES), jnp.float32),
                        pltpu.VMEM((MLA_HEADS, tq, LANES), jnp.float32),
                        pltpu.VMEM((MLA_HEADS, tq, V_HEAD_DIM), jnp.float32)],
        compiler_params=pltpu.CompilerParams(
            dimension_semantics=("arbitrary", "arbitrary"), vmem_limit_bytes=VMEM_LIMIT),
        name="attn",
    )(q, k, v)


def _proj_kernel(outa_ref, ob_ref, h_hbm_ref, wout_ref, ln1g_ref, ln1b_ref, wr_ref, br_ref,
                 h1_ref, h1b_ref, ri_ref, rf_ref, proj_ref, hring_ref, hsems):
    i = pl.program_id(0)
    tb = outa_ref.shape[0]
    h_ref = _ring_block(h_hbm_ref, hring_ref, hsems, lambda step: jnp.maximum(step - 1, 0), tb)

    @pl.when(i == 0)
    def _():
        proj_ref[...] = jnp.zeros_like(proj_ref)

    h1 = _layer_norm(ALPHA * h_ref[...] + proj_ref[(i + 1) % 2], ln1g_ref[...], ln1b_ref[...])
    h1_ref[...] = h1
    h1b_ref[...] = h1.astype(jnp.bfloat16)

    logits_tm = jnp.dot(h1b_ref[...], wr_ref[...], preferred_element_type=jnp.float32)
    logits = logits_tm.T[0:ROUTER_ROWS] + br_ref[...]

    sub_i = lax.broadcasted_iota(jnp.int32, (SUBLANES, tb), 0)
    sub = sub_i.astype(jnp.float32)
    neg = jnp.float32(-jnp.inf)
    g = jnp.where(sub_i < N_GROUPS, logits[0:SUBLANES], neg)
    gmax = jnp.max(g, axis=0, keepdims=True)
    g_top = jnp.min(jnp.where(g == gmax, sub, float(SUBLANES)), axis=0, keepdims=True)
    p_group = 1.0 / jnp.sum(jnp.exp(g - gmax), axis=0, keepdims=True)
    sel = logits[SUBLANES:2 * SUBLANES]
    for grp in range(1, N_GROUPS):
        sel = jnp.where(g_top == float(grp), logits[(grp + 1) * SUBLANES:(grp + 2) * SUBLANES], sel)
    v1 = jnp.max(sel, axis=0, keepdims=True)
    i1 = jnp.min(jnp.where(sel == v1, sub, float(SUBLANES)), axis=0, keepdims=True)
    sel2 = jnp.where(sub == i1, neg, sel)
    v2 = jnp.max(sel2, axis=0, keepdims=True)
    i2 = jnp.min(jnp.where(sel2 == v2, sub, float(SUBLANES)), axis=0, keepdims=True)
    e21 = jnp.exp(v2 - v1)
    w1 = 1.0 / (1.0 + e21)
    gate1 = p_group * w1
    gate2 = p_group * (e21 * w1)
    e1 = g_top * EXPERTS_PER_GROUP + i1
    e2 = g_top * EXPERTS_PER_GROUP + i2
    ri_ref[...] = jnp.where(sub_i == 0, e1, jnp.where(sub_i == 1, e2, 0.0)).astype(jnp.int32)
    rf_ref[...] = jnp.where(sub_i == 0, gate1, jnp.where(sub_i == 1, gate2, 0.0))

    proj_ref[i % 2] = (jnp.dot(outa_ref[...], wout_ref[:GM_WIDTH, :], preferred_element_type=jnp.float32)
                       + jnp.dot(ob_ref[...], wout_ref[GM_WIDTH:, :], preferred_element_type=jnp.float32))


def _proj(outa, ob, h, wout, ln1g, ln1b, wr, br):
    tb = PROJ_TOKENS
    n_blk = N_TOK // tb
    cur = lambda i: jnp.minimum(i, n_blk - 1)
    prev = lambda i: jnp.maximum(i - 1, 0)
    full = lambda shape: pl.BlockSpec(shape, lambda i: (0,) * len(shape))
    return pl.pallas_call(
        _proj_kernel,
        grid=(n_blk + 1,),
        in_specs=[pl.BlockSpec((tb, GM_WIDTH), lambda i: (cur(i), 0)),
                  pl.BlockSpec((tb, GM_WIDTH), lambda i: (cur(i), 0)),
                  pl.BlockSpec(memory_space=pl.ANY),
                  full((D_MODEL, D_MODEL)), full((1, D_MODEL)), full((1, D_MODEL)),
                  full((D_MODEL, LANES)), full((ROUTER_ROWS, 1))],
        out_specs=[pl.BlockSpec((tb, D_MODEL), lambda i: (prev(i), 0)),
                   pl.BlockSpec((tb, D_MODEL), lambda i: (prev(i), 0)),
                   pl.BlockSpec((SUBLANES, tb), lambda i: (0, prev(i))),
                   pl.BlockSpec((SUBLANES, tb), lambda i: (0, prev(i)))],
        out_shape=[jax.ShapeDtypeStruct((N_TOK, D_MODEL), jnp.float32),
                   jax.ShapeDtypeStruct((N_TOK, D_MODEL), jnp.bfloat16),
                   jax.ShapeDtypeStruct((SUBLANES, N_TOK), jnp.int32),
                   jax.ShapeDtypeStruct((SUBLANES, N_TOK), jnp.float32)],
        scratch_shapes=[pltpu.VMEM((2, tb, D_MODEL), jnp.float32)] + _ring_scratch(tb, D_MODEL, jnp.float32),
        compiler_params=pltpu.CompilerParams(
            dimension_semantics=("arbitrary",), vmem_limit_bytes=VMEM_LIMIT),
        name="proj",
    )(outa, ob, h, wout, ln1g, ln1b, wr, br)


def _plan_kernel(ri_all_ref, ri_ref, ldest_ref, runs_ref, meta_ref, run_ref, start_ref, upper_ref):
    step = pl.program_id(0)
    tb = ri_ref.shape[1]
    f32 = jnp.float32
    er = lax.broadcasted_iota(jnp.int32, (N_EXPERTS, LANES), 0)
    ec = lax.broadcasted_iota(jnp.int32, (N_EXPERTS, LANES), 1)
    to_row = lambda col: jnp.sum(jnp.where(er == ec, col, 0.0), axis=0, keepdims=True)

    def expert_one_hot(ref):
        e_sub = lax.broadcasted_iota(jnp.int32, (N_EXPERTS, ref.shape[1]), 0)
        return e_sub == ref[0:1, :], e_sub == ref[1:2, :]

    @pl.when(step == 0)
    def _():
        oh1, oh2 = expert_one_hot(ri_all_ref)
        counts = jnp.sum(jnp.where(oh1 | oh2, 1.0, 0.0), axis=1, keepdims=True)
        padded = jnp.floor((counts + (EXPERT_ROWS - 1)) * (1.0 / EXPERT_ROWS)) * EXPERT_ROWS
        pad_end = jnp.sum(jnp.where(ec <= er, to_row(padded), 0.0), axis=1, keepdims=True)
        start_ref[...] = jnp.broadcast_to(pad_end - padded, start_ref.shape)
        run_ref[...] = jnp.zeros_like(run_ref)
        bstart = (lax.broadcasted_iota(jnp.int32, (N_EXPERTS, META_LANES), 1) * EXPERT_ROWS).astype(f32)
        blk_e = jnp.sum(jnp.where(pad_end <= bstart, 1.0, 0.0), axis=0, keepdims=True)
        blk_e = jnp.minimum(blk_e, N_EXPERTS - 1.0)
        n_used = pad_end[N_EXPERTS - 1:N_EXPERTS, :] * (1.0 / EXPERT_ROWS)
        pad3 = lambda r: jnp.concatenate(
            [r, jnp.zeros((1, META_LANES - LANES), f32)], axis=1)
        msub = lax.broadcasted_iota(jnp.int32, (SUBLANES, META_LANES), 0)
        meta = jnp.where(msub == 0, blk_e,
                         jnp.where(msub == 1, pad3(to_row(pad_end)),
                                   jnp.where(msub == 2, pad3(to_row(counts)),
                                             jnp.where(msub == 3, n_used, 0.0))))
        meta_ref[...] = meta.astype(jnp.int32)
        tr = lax.broadcasted_iota(jnp.int32, (tb, tb), 0)
        tc = lax.broadcasted_iota(jnp.int32, (tb, tb), 1)
        upper_ref[...] = jnp.where(tr < tc, 1.0, 0.0).astype(jnp.bfloat16)

    @pl.when(step > 0)
    def _():
        oh1, oh2 = expert_one_hot(ri_ref)
        oh = jnp.where(oh1 | oh2, 1.0, 0.0).astype(f32)
        blk_count = jnp.sum(oh, axis=1, keepdims=True)
        prefix = jnp.dot(oh.astype(jnp.bfloat16), upper_ref[...], preferred_element_type=f32)
        cnt_row = to_row(blk_count)
        lstart = jnp.sum(jnp.where(ec < er, cnt_row, 0.0), axis=1, keepdims=True)
        base = prefix + lstart
        d1 = jnp.sum(jnp.where(oh1, base, 0.0), axis=0, keepdims=True)
        d2 = jnp.sum(jnp.where(oh2, base, 0.0), axis=0, keepdims=True)
        sub = lax.broadcasted_iota(jnp.int32, (SUBLANES, tb), 0)
        ldest_ref[...] = jnp.where(sub == 0, d1, jnp.where(sub == 1, d2, 0.0)).astype(jnp.int32)
        gstart = start_ref[:, 0:1] + run_ref[:, 0:1]
        rsub = lax.broadcasted_iota(jnp.int32, (SUBLANES, LANES), 0)
        runs = jnp.where(rsub == 0, cnt_row,
                         jnp.where(rsub == 1, to_row(lstart), jnp.where(rsub == 2, to_row(gstart), 0.0)))
        runs_ref[...] = runs.astype(jnp.int32)
        run_ref[...] = run_ref[...] + blk_count


def _plan(ri):
    tb = MOE_TOKENS
    blk = lambda i: jnp.maximum(i - 1, 0)
    return pl.pallas_call(
        _plan_kernel,
        grid=(N_MOE_BLOCKS + 1,),
        in_specs=[pl.BlockSpec((SUBLANES, N_TOK), lambda i: (0, 0)),
                  pl.BlockSpec((SUBLANES, tb), lambda i: (0, blk(i)))],
        out_specs=[pl.BlockSpec((SUBLANES, tb), lambda i: (0, blk(i))),
                   pl.BlockSpec((SUBLANES, LANES), lambda i: (blk(i), 0)),
                   pl.BlockSpec((SUBLANES, META_LANES), lambda i: (0, 0))],
        out_shape=[jax.ShapeDtypeStruct((SUBLANES, N_TOK), jnp.int32),
                   jax.ShapeDtypeStruct((N_MOE_BLOCKS * SUBLANES, LANES), jnp.int32),
                   jax.ShapeDtypeStruct((SUBLANES, META_LANES), jnp.int32)],
        scratch_shapes=[pltpu.VMEM((N_EXPERTS, LANES), jnp.float32),
                        pltpu.VMEM((N_EXPERTS, LANES), jnp.float32),
                        pltpu.VMEM((tb, tb), jnp.bfloat16)],
        compiler_params=pltpu.CompilerParams(
            dimension_semantics=("arbitrary",), vmem_limit_bytes=VMEM_LIMIT),
        name="plan",
    )(ri, ri)


def _for_each_run_piece(runs_ref, fn):
    for e in range(N_EXPERTS):
        n, lstart, gstart = runs_ref[0, e], runs_ref[1, e], runs_ref[2, e]
        for bit in range(RUN_BITS):
            @pl.when((n & (1 << bit)) != 0)
            def _(n=n, lstart=lstart, gstart=gstart, bit=bit):
                off = (n >> (bit + 1)) << (bit + 1)
                fn(lstart + off, gstart + off, 1 << bit)


def _tile_rows(ref, row, rows):
    return ref.at[pl.ds(pl.multiple_of(row * FEAT_TILES, FEAT_TILES), rows * FEAT_TILES)]


def _dispatch_kernel(meta_ref, runs_ref, ldest_ref, h1b_ref, buf_ref, sorted_ref, zero_ref, sems, zsem):
    i = pl.program_id(0)
    n_steps = pl.num_programs(0)
    tb = ldest_ref.shape[1]
    slot = i % 2
    block_tiles = 2 * tb * FEAT_TILES

    def wait_slot(s):
        pltpu.make_async_copy(sorted_ref.at[s], buf_ref.at[pl.ds(0, block_tiles)], sems.at[s]).wait()

    @pl.when(i == 0)
    def _():
        zero_ref[...] = jnp.zeros_like(zero_ref)

        def zero_copy(e):
            start = pl.multiple_of((meta_ref[1, e] - EXPERT_ROWS) * FEAT_TILES, EXPERT_ROWS * FEAT_TILES)
            return pltpu.make_async_copy(
                zero_ref, buf_ref.at[pl.ds(start, EXPERT_ROWS * FEAT_TILES)], zsem)

        def start_zero(e, c):
            @pl.when(meta_ref[2, e] > 0)
            def _():
                zero_copy(e).start()
            return c

        def wait_zero(e, c):
            @pl.when(meta_ref[2, e] > 0)
            def _():
                zero_copy(e).wait()
            return c

        def tail_copy(b):
            start = pl.multiple_of(b * (EXPERT_ROWS * FEAT_TILES), EXPERT_ROWS * FEAT_TILES)
            return pltpu.make_async_copy(
                zero_ref, buf_ref.at[pl.ds(start, EXPERT_ROWS * FEAT_TILES)], zsem)

        def start_tail(b, c):
            tail_copy(b).start()
            return c

        def wait_tail(b, c):
            tail_copy(b).wait()
            return c

        lax.fori_loop(0, N_EXPERTS, start_zero, 0)
        lax.fori_loop(meta_ref[3, 0], N_ROW_BLOCKS, start_tail, 0)
        lax.fori_loop(0, N_EXPERTS, wait_zero, 0)
        lax.fori_loop(meta_ref[3, 0], N_ROW_BLOCKS, wait_tail, 0)

    @pl.when(i >= 2)
    def _():
        wait_slot(slot)

    x = h1b_ref[...]
    ld0 = ldest_ref[0:1, :]
    ld1 = ldest_ref[1:2, :]
    for c in range(2 * tb // SORT_CHUNK):
        r = lax.broadcasted_iota(jnp.int32, (SORT_CHUNK, tb), 0) + c * SORT_CHUNK
        perm = jnp.where((r == ld0) | (r == ld1), 1.0, 0.0).astype(jnp.bfloat16)
        rows = jnp.dot(perm, x, preferred_element_type=jnp.float32)
        _to_row_tiles(sorted_ref.at[slot, pl.ds(c * SORT_CHUNK * FEAT_TILES, SORT_CHUNK * FEAT_TILES)], rows)

    def send(lrow, grow, rows):
        pltpu.make_async_copy(_tile_rows(sorted_ref.at[slot], lrow, rows),
                              _tile_rows(buf_ref, grow, rows), sems.at[slot]).start()

    _for_each_run_piece(runs_ref, send)

    @pl.when(i == n_steps - 1)
    def _():
        wait_slot(slot)
        wait_slot(1 - slot)


def _dispatch(meta, runs, ldest, h1b):
    tb = MOE_TOKENS
    return pl.pallas_call(
        _dispatch_kernel,
        grid_spec=pltpu.PrefetchScalarGridSpec(
            num_scalar_prefetch=1,
            grid=(N_MOE_BLOCKS,),
            in_specs=[pl.BlockSpec((SUBLANES, LANES), lambda i, m: (i, 0), memory_space=pltpu.SMEM),
                      pl.BlockSpec((SUBLANES, tb), lambda i, m: (0, i)),
                      pl.BlockSpec((tb, D_MODEL), lambda i, m: (i, 0))],
            out_specs=pl.BlockSpec(memory_space=pl.ANY),
            scratch_shapes=[pltpu.VMEM((2, 2 * tb * FEAT_TILES, LANES), jnp.float32),
                            pltpu.VMEM((EXPERT_ROWS * FEAT_TILES, LANES), jnp.float32),
                            pltpu.SemaphoreType.DMA((2,)), pltpu.SemaphoreType.DMA]),
        out_shape=jax.ShapeDtypeStruct((N_ROWS * FEAT_TILES, LANES), jnp.float32),
        compiler_params=pltpu.CompilerParams(
            dimension_semantics=("arbitrary",), vmem_limit_bytes=VMEM_LIMIT),
        name="dispatch",
    )(meta, runs, ldest, h1b)


def _sub_block_expert(meta, i, sub):
    return meta[0, jnp.minimum(i + sub * (N_ROW_BLOCKS // EXPERT_SUB), meta[3, 0] - 1)]


def _experts_kernel(meta_ref, x_hbm_ref, *refs):
    weights = [refs[3 * sub:3 * sub + 3] for sub in range(EXPERT_SUB)]
    o_ref, wgb_ref, wub_ref, wdb_ref, cached_ref, xring_ref, xsems = refs[3 * EXPERT_SUB:]
    n = EXPERT_ROWS
    i = pl.program_id(0)
    n_steps = pl.num_programs(0)

    def rows_copy(step):
        slot = step % EXPERT_RING
        start = pl.multiple_of(step * (n * FEAT_TILES), n * FEAT_TILES)
        return pltpu.make_async_copy(x_hbm_ref.at[:, pl.ds(start, n * FEAT_TILES), :], xring_ref.at[slot],
                                     xsems.at[slot])

    @pl.when(i == 0)
    def _():
        for step in range(EXPERT_RING - 1):
            rows_copy(step).start()

    @pl.when(i + (EXPERT_RING - 1) < n_steps)
    def _():
        rows_copy(i + (EXPERT_RING - 1)).start()

    rows_copy(i).wait()
    x_ref = xring_ref.at[i % EXPERT_RING]

    @pl.when(i == 0)
    def _():
        for sub in range(EXPERT_SUB):
            cached_ref[sub] = -1

    for sub in range(EXPERT_SUB):
        expert = _sub_block_expert(meta_ref, i, sub)

        @pl.when(cached_ref[sub] != expert)
        def _(sub=sub, expert=expert):
            wg_ref, wu_ref, wd_ref = weights[sub]
            wgb_ref[sub] = wg_ref[...].astype(jnp.bfloat16)
            wub_ref[sub] = wu_ref[...].astype(jnp.bfloat16)
            wdb_ref[sub] = wd_ref[...].astype(jnp.bfloat16)
            cached_ref[sub] = expert

    for sub in range(EXPERT_SUB):
        x = _from_row_tiles(x_ref.at[sub], n).astype(jnp.bfloat16)
        gate = jnp.dot(x, wgb_ref[sub], preferred_element_type=jnp.float32)
        up = jnp.dot(x, wub_ref[sub], preferred_element_type=jnp.float32)
        act = (gate * jax.nn.sigmoid(gate) * up).astype(jnp.bfloat16)
        _to_row_tiles(o_ref.at[sub], jnp.dot(act, wdb_ref[sub], preferred_element_type=jnp.float32))


def _experts(meta, buf, wg, wu, wd):
    sub_tiles = (N_ROW_BLOCKS // EXPERT_SUB) * EXPERT_ROWS * FEAT_TILES

    def weight_spec(shape, sub):
        return pl.BlockSpec((None,) + shape, lambda i, m: (_sub_block_expert(m, i, sub), 0, 0))

    rows_shape = (EXPERT_SUB, EXPERT_ROWS * FEAT_TILES, LANES)
    up_shape, down_shape = (D_MODEL, EXPERT_FF), (EXPERT_FF, D_MODEL)
    eout = pl.pallas_call(
        _experts_kernel,
        grid_spec=pltpu.PrefetchScalarGridSpec(
            num_scalar_prefetch=1,
            grid=(N_ROW_BLOCKS // EXPERT_SUB,),
            in_specs=[pl.BlockSpec(memory_space=pl.ANY)] + [
                weight_spec(shape, sub) for sub in range(EXPERT_SUB) for shape in (up_shape, up_shape, down_shape)],
            out_specs=pl.BlockSpec(rows_shape, lambda i, m: (0, i, 0)),
            scratch_shapes=[pltpu.VMEM((EXPERT_SUB,) + up_shape, jnp.bfloat16),
                            pltpu.VMEM((EXPERT_SUB,) + up_shape, jnp.bfloat16),
                            pltpu.VMEM((EXPERT_SUB,) + down_shape, jnp.bfloat16),
                            pltpu.SMEM((EXPERT_SUB,), jnp.int32),
                            pltpu.VMEM((EXPERT_RING,) + rows_shape, jnp.float32),
                            pltpu.SemaphoreType.DMA((EXPERT_RING,))]),
        out_shape=jax.ShapeDtypeStruct((EXPERT_SUB, sub_tiles, LANES), jnp.float32),
        compiler_params=pltpu.CompilerParams(
            dimension_semantics=("arbitrary",), vmem_limit_bytes=VMEM_LIMIT),
        name="experts",
    )(meta, buf.reshape(EXPERT_SUB, sub_tiles, LANES), *([wg, wu, wd] * EXPERT_SUB))
    return eout.reshape(N_ROWS * FEAT_TILES, LANES)


def _combine_kernel(runs_ref, runs_next_ref, ldest_ref, rf_ref, h1_ref, eout_ref, ln2g_ref, ln2b_ref,
                    o_ref, y_ref, sems):
    i = pl.program_id(0)
    n_steps = pl.num_programs(0)
    tb = ldest_ref.shape[1]
    slot = i % 2
    block_tiles = 2 * tb * FEAT_TILES

    def fetch(table_ref, s):
        def recv(lrow, grow, rows):
            pltpu.make_async_copy(_tile_rows(eout_ref, grow, rows),
                                  _tile_rows(y_ref.at[s], lrow, rows), sems.at[s]).start()
        _for_each_run_piece(table_ref, recv)

    @pl.when(i == 0)
    def _():
        fetch(runs_ref, slot)

    @pl.when(i + 1 < n_steps)
    def _():
        fetch(runs_next_ref, 1 - slot)

    pltpu.make_async_copy(eout_ref.at[pl.ds(0, block_tiles)], y_ref.at[slot], sems.at[slot]).wait()

    ld = ldest_ref[...].astype(jnp.float32).T
    gates = rf_ref[...].T
    y = None
    col = lax.broadcasted_iota(jnp.int32, (tb, SORT_CHUNK), 1).astype(jnp.float32)
    for c in range(2 * tb // SORT_CHUNK):
        ld_c = ld - float(c * SORT_CHUNK)
        g = jnp.where(col == ld_c[:, 0:1], gates[:, 0:1],
                      jnp.where(col == ld_c[:, 1:2], gates[:, 1:2], 0.0)).astype(jnp.bfloat16)
        rows = _from_row_tiles(
            y_ref.at[slot, pl.ds(c * SORT_CHUNK * FEAT_TILES, SORT_CHUNK * FEAT_TILES)], SORT_CHUNK)
        part = jnp.dot(g, rows.astype(jnp.bfloat16), preferred_element_type=jnp.float32)
        y = part if y is None else y + part
    o_ref[...] = _layer_norm(ALPHA * h1_ref[...] + y, ln2g_ref[...], ln2b_ref[...])


def _combine(runs, ldest, rf, h1, eout, ln2g, ln2b):
    tb = MOE_TOKENS
    last = N_MOE_BLOCKS - 1
    return pl.pallas_call(
        _combine_kernel,
        grid=(N_MOE_BLOCKS,),
        in_specs=[pl.BlockSpec((SUBLANES, LANES), lambda i: (i, 0), memory_space=pltpu.SMEM),
                  pl.BlockSpec((SUBLANES, LANES), lambda i: (jnp.minimum(i + 1, last), 0),
                               memory_space=pltpu.SMEM),
                  pl.BlockSpec((SUBLANES, tb), lambda i: (0, i)),
                  pl.BlockSpec((SUBLANES, tb), lambda i: (0, i)),
                  pl.BlockSpec((tb, D_MODEL), lambda i: (i, 0)),
                  pl.BlockSpec(memory_space=pl.ANY),
                  pl.BlockSpec((1, D_MODEL), lambda i: (0, 0)),
                  pl.BlockSpec((1, D_MODEL), lambda i: (0, 0))],
        out_specs=pl.BlockSpec((tb, D_MODEL), lambda i: (i, 0)),
        out_shape=jax.ShapeDtypeStruct((N_TOK, D_MODEL), jnp.float32),
        scratch_shapes=[pltpu.VMEM((2, 2 * tb * FEAT_TILES, LANES), jnp.float32),
                        pltpu.SemaphoreType.DMA((2,))],
        compiler_params=pltpu.CompilerParams(
            dimension_semantics=("arbitrary",), vmem_limit_bytes=VMEM_LIMIT),
        name="combine",
    )(runs, runs, ldest, rf, h1, eout, ln2g, ln2b)


def _swap_halves(w):
    half = w.shape[-1] // 2
    return jnp.concatenate([w[..., half:], w[..., :half]], axis=-1)


def kernel(x, positions, ln0_g, ln0_b, w_in, gm_ln_g, gm_ln_b, w_spatial, b_spatial, q_norm_g, w_uq, kv_norm_g, w_ukv, w_out, ln1_g, ln1_b, w_router_group, b_router_group, w_router_expert, b_router_expert, w_gate, w_up, w_down, ln2_g, ln2_b):
    bf16 = jnp.bfloat16
    row = lambda a: a.reshape(1, -1)

    w_in0 = w_in[0]
    kr_cols = w_in0[:, O_KR:O_KR + QK_ROPE_DIM]
    win = jnp.concatenate([w_in0, _swap_halves(kr_cols)], axis=1).astype(bf16)
    wuq3 = w_uq[0].reshape(Q_LORA_RANK, MLA_HEADS, QK_NOPE_DIM + QK_ROPE_DIM)
    rope_cols = wuq3[:, :, QK_NOPE_DIM:]
    wuq = jnp.concatenate([wuq3, _swap_halves(rope_cols)], axis=-1).reshape(Q_LORA_RANK, D_MODEL).astype(bf16)
    wukv = w_ukv[0].astype(bf16)
    wout = w_out[0].astype(bf16)
    bs = jnp.broadcast_to(b_spatial[0][:, :, None], (GM_HEADS, GM_CHUNK, GM_HEAD_DIM))
    wr = jnp.concatenate([w_router_group[0], jnp.zeros((D_MODEL, SUBLANES - N_GROUPS), jnp.float32),
                          w_router_expert[0],
                          jnp.zeros((D_MODEL, LANES - ROUTER_ROWS), jnp.float32)],
                         axis=1).astype(bf16)
    br = jnp.concatenate([b_router_group[0], jnp.zeros((SUBLANES - N_GROUPS,), jnp.float32),
                          b_router_expert[0]]).reshape(ROUTER_ROWS, 1)

    inv_freq = ROPE_THETA ** (-jnp.arange(0, QK_ROPE_DIM, 2, dtype=jnp.float32) / QK_ROPE_DIM)
    freq = jnp.tile(inv_freq, 4).reshape(1, LANES)
    quarter = QK_ROPE_DIM // 2
    phase = jnp.concatenate([jnp.zeros((2 * quarter,), jnp.float32),
                             jnp.full((2 * quarter,), math.pi / 2, jnp.float32)]).reshape(1, LANES)
    sign = jnp.concatenate([jnp.ones((2 * quarter,), jnp.float32), -jnp.ones((quarter,), jnp.float32),
                            jnp.ones((quarter,), jnp.float32)]).reshape(1, LANES)

    x2 = x.reshape(N_TOK, D_MODEL)
    pos2 = positions.reshape(N_TOK // PREP_TOKENS, 1, PREP_TOKENS)

    h, outa, q, k, v = _prep(x2, pos2, row(ln0_g), row(ln0_b), win, row(gm_ln_g[0]), row(gm_ln_b[0]),
                             w_spatial[0], bs, row(q_norm_g[0]), wuq, row(kv_norm_g[0]), wukv,
                             freq, phase, sign)
    ob = _attn(q, k, v)
    h1, h1b, ri, rf = _proj(outa, ob, h, wout, row(ln1_g[0]), row(ln1_b[0]), wr, br)
    ldest, runs, meta = _plan(ri)
    buf = _dispatch(meta, runs, ldest, h1b)
    eout = _experts(meta, buf, w_gate[0], w_up[0], w_down[0])
    out = _combine(runs, ldest, rf, h1, eout, row(ln2_g[0]), row(ln2_b[0]))
    return out.reshape(BATCH, SEQ, D_MODEL)
```

```python
import functools
import math

import jax
import jax.numpy as jnp
from jax import lax
from jax.experimental import pallas as pl
from jax.experimental.pallas import tpu as pltpu

D_MODEL = 1024
BATCH = 16
SEQ = 2048
N_TOK = BATCH * SEQ
CHUNK = 64
GM_WIDTH = 512
GM_HEADS = 4
GM_HEAD_DIM = 128
GM_CHUNK = 128
MLA_HEADS = 4
QK_NOPE_DIM = 128
QK_ROPE_DIM = 64
V_HEAD_DIM = 128
Q_LORA_RANK = 384
KV_LORA_RANK = 256
ROPE_THETA = 10000.0
N_GROUPS = 4
EXPERTS_PER_GROUP = 8
N_EXPERTS = 32
TOP_K = 2
EXPERT_FF = 256
ALPHA = 2.0 ** 0.25
QK_SCALE = (QK_NOPE_DIM + QK_ROPE_DIM) ** -0.5 * math.log2(math.e)

LANES = 128
SUBLANES = 8
FEAT_TILES = D_MODEL // LANES
PREP_TOKENS = 1024
ATT_Q = 512
ATT_K = 512
N_QBLK = SEQ // ATT_Q
PROJ_TOKENS = 1024
MOE_TOKENS = 512
N_MOE_BLOCKS = N_TOK // MOE_TOKENS
MOE_GROUP = 2
RUN_BITS = (TOP_K * MOE_TOKENS).bit_length()
SORT_CHUNK = 256
EXPERT_ROWS = 256
EXPERT_SUB = 3
EXPERT_RING = 3
INPUT_RING = 3
N_ROWS = N_TOK * TOP_K + N_EXPERTS * EXPERT_ROWS
N_ROW_BLOCKS = N_ROWS // EXPERT_ROWS
META_LANES = 384
IN_COLS = 2 * GM_WIDTH + Q_LORA_RANK + KV_LORA_RANK + 2 * QK_ROPE_DIM
O_Q = 2 * GM_WIDTH
O_KV = O_Q + Q_LORA_RANK
O_KR = O_KV + KV_LORA_RANK
ROUTER_ROWS = 40
VMEM_LIMIT = 48 * 1024 * 1024

assert N_ROW_BLOCKS <= META_LANES and N_ROW_BLOCKS % EXPERT_SUB == 0


def _layer_norm(x, g, b, eps=1e-5):
    mu = jnp.mean(x, axis=-1, keepdims=True)
    xc = x - mu
    var = jnp.mean(xc * xc, axis=-1, keepdims=True)
    return xc * lax.rsqrt(var + eps) * g + b


def _rms_norm(x, g, eps=1e-6):
    return x * lax.rsqrt(jnp.mean(x * x, axis=-1, keepdims=True) + eps) * g


def _gelu_tanh(x):
    c = math.sqrt(2.0 / math.pi)
    return 0.5 * x * (1.0 + jnp.tanh(c * (x + 0.044715 * (x * x * x))))


def _to_row_tiles(ref, x):
    n = x.shape[0]
    for s in range(FEAT_TILES):
        ref[pl.ds(s, n, stride=FEAT_TILES), :] = x[:, s * LANES:(s + 1) * LANES]


def _from_row_tiles(ref, n):
    return jnp.concatenate(
        [ref[pl.ds(s, n, stride=FEAT_TILES), :] for s in range(FEAT_TILES)], axis=-1)


def _ring_block(hbm_ref, ring_ref, sems, block_of_step, rows):
    i = pl.program_id(0)
    n_steps = pl.num_programs(0)

    def copy(step):
        start = pl.multiple_of(block_of_step(step) * rows, rows)
        slot = step % INPUT_RING
        return pltpu.make_async_copy(hbm_ref.at[pl.ds(start, rows)], ring_ref.at[slot], sems.at[slot])

    @pl.when(i == 0)
    def _():
        for step in range(INPUT_RING - 1):
            copy(step).start()

    @pl.when(i + (INPUT_RING - 1) < n_steps)
    def _():
        copy(i + (INPUT_RING - 1)).start()

    copy(i).wait()
    return ring_ref.at[i % INPUT_RING]


def _ring_scratch(rows, cols, dtype):
    return [pltpu.VMEM((INPUT_RING, rows, cols), dtype), pltpu.SemaphoreType.DMA((INPUT_RING,))]


def _prep_kernel(x_ref, pos_ref, ln0g_ref, ln0b_ref, win_ref, gmg_ref, gmb_ref, ws_ref, bs_ref,
                 qg_ref, wuq_ref, kvg_ref, wukv_ref, freq_ref, phase_ref, sign_ref,
                 h_ref, outa_ref, q_ref, k_ref, v_ref, tabc_ref, tabs_ref, rot_ref):
    tb = x_ref.shape[0]

    @pl.when(pl.program_id(0) == 0)
    def _():
        d = lax.broadcasted_iota(jnp.int32, (tb, LANES), 0).astype(jnp.float32) * freq_ref[...]
        tabc_ref[...] = jnp.cos(d)
        tabs_ref[...] = jnp.sin(d)

    pos_row = pos_ref[0]
    p0 = pos_row[:, 0:1]
    offset = lax.broadcasted_iota(jnp.int32, (1, tb), 1)
    consecutive = jnp.max(jnp.abs((pos_row - p0 - offset).astype(jnp.float32))) == 0.0

    @pl.when(consecutive)
    def _():
        a0 = p0.astype(jnp.float32) * freq_ref[...]
        c0, s0 = jnp.cos(a0), jnp.sin(a0)
        lane = lax.broadcasted_iota(jnp.int32, (1, LANES), 1)
        coef_c = jnp.where(lane < 2 * 32, c0, jnp.where(lane < 3 * 32, -s0, s0))
        coef_s = jnp.where(lane < 2 * 32, -s0, jnp.where(lane < 3 * 32, -c0, c0))
        rot_ref[...] = coef_c * tabc_ref[...] + coef_s * tabs_ref[...]

    @pl.when(jnp.logical_not(consecutive))
    def _():
        pos_col = jnp.broadcast_to(pos_row.astype(jnp.float32), (SUBLANES, tb)).T[:, 0:1]
        ang = pos_col * freq_ref[...]
        rot_ref[...] = jnp.cos(ang - phase_ref[...]) * sign_ref[...]

    rot = rot_ref[...]

    h = _layer_norm(x_ref[...], ln0g_ref[...], ln0b_ref[...])
    h_ref[...] = h
    z = jnp.dot(h.astype(jnp.bfloat16), win_ref[...], preferred_element_type=jnp.float32)

    u = _gelu_tanh(z[:, :GM_WIDTH])
    v = _gelu_tanh(z[:, GM_WIDTH:2 * GM_WIDTH])
    row_chunk = lax.broadcasted_iota(jnp.int32, (GM_CHUNK, GM_CHUNK), 0) // CHUNK
    col_chunk = lax.broadcasted_iota(jnp.int32, (GM_CHUNK, GM_CHUNK), 1) // CHUNK
    allowed = col_chunk <= row_chunk
    for hd in range(GM_HEADS):
        lo, hi = hd * GM_HEAD_DIM, (hd + 1) * GM_HEAD_DIM
        vln = _layer_norm(v[:, lo:hi], gmg_ref[:, lo:hi], gmb_ref[:, lo:hi]).astype(jnp.bfloat16)
        wm = jnp.where(allowed, ws_ref[hd], 0.0).astype(jnp.bfloat16)
        for c in range(tb // GM_CHUNK):
            r0, r1 = c * GM_CHUNK, (c + 1) * GM_CHUNK
            f = jnp.dot(wm, vln[r0:r1], preferred_element_type=jnp.float32) + bs_ref[hd]
            outa_ref[r0:r1, lo:hi] = (u[r0:r1, lo:hi] * f).astype(jnp.bfloat16)

    ql = _rms_norm(z[:, O_Q:O_KV], qg_ref[...]).astype(jnp.bfloat16)
    qf = jnp.dot(ql, wuq_ref[...], preferred_element_type=jnp.float32)
    rot_s = rot * QK_SCALE
    q_parts = []
    for hd in range(MLA_HEADS):
        base = hd * 2 * LANES
        q_parts.append(qf[:, base:base + LANES] * QK_SCALE)
        q_parts.append(qf[:, base + LANES:base + 2 * LANES] * rot_s)
    q_ref[...] = jnp.concatenate(q_parts, axis=-1).astype(jnp.bfloat16)

    kvl = _rms_norm(z[:, O_KV:O_KR], kvg_ref[...]).astype(jnp.bfloat16)
    kv = jnp.dot(kvl, wukv_ref[...], preferred_element_type=jnp.float32)
    t = z[:, O_KR:O_KR + LANES] * rot
    krr = t + pltpu.roll(t, 2 * QK_ROPE_DIM // 2, axis=1)
    k_parts, v_parts = [], []
    for hd in range(MLA_HEADS):
        base = hd * 2 * LANES
        k_parts.append(kv[:, base:base + LANES])
        k_parts.append(krr)
        v_parts.append(kv[:, base + LANES:base + 2 * LANES])
    k_ref[...] = jnp.concatenate(k_parts, axis=-1).astype(jnp.bfloat16)
    v_ref[...] = jnp.concatenate(v_parts, axis=-1).astype(jnp.bfloat16)


def _prep(x2, pos2, ln0g, ln0b, win, gmg, gmb, ws, bs, qg, wuq, kvg, wukv, freq, phase, sign):
    tb = PREP_TOKENS
    full = lambda shape: pl.BlockSpec(shape, lambda i: (0,) * len(shape))
    tok = lambda cols: pl.BlockSpec((tb, cols), lambda i: (i, 0))
    return pl.pallas_call(
        _prep_kernel,
        grid=(N_TOK // tb,),
        in_specs=[tok(D_MODEL), pl.BlockSpec((1, 1, tb), lambda i: (i, 0, 0)), full((1, D_MODEL)), full((1, D_MODEL)),
                  full((D_MODEL, IN_COLS)), full((1, GM_WIDTH)), full((1, GM_WIDTH)),
                  full((GM_HEADS, GM_CHUNK, GM_CHUNK)), full((GM_HEADS, GM_CHUNK, GM_HEAD_DIM)),
                  full((1, Q_LORA_RANK)), full((Q_LORA_RANK, D_MODEL)),
                  full((1, KV_LORA_RANK)), full((KV_LORA_RANK, D_MODEL)),
                  full((1, LANES)), full((1, LANES)), full((1, LANES))],
        out_specs=[tok(D_MODEL), tok(GM_WIDTH), tok(D_MODEL), tok(D_MODEL), tok(GM_WIDTH)],
        out_shape=[jax.ShapeDtypeStruct((N_TOK, D_MODEL), jnp.float32),
                   jax.ShapeDtypeStruct((N_TOK, GM_WIDTH), jnp.bfloat16),
                   jax.ShapeDtypeStruct((N_TOK, D_MODEL), jnp.bfloat16),
                   jax.ShapeDtypeStruct((N_TOK, D_MODEL), jnp.bfloat16),
                   jax.ShapeDtypeStruct((N_TOK, GM_WIDTH), jnp.bfloat16)],
        scratch_shapes=[pltpu.VMEM((tb, LANES), jnp.float32)] * 3,
        compiler_params=pltpu.CompilerParams(
            dimension_semantics=("arbitrary",), vmem_limit_bytes=VMEM_LIMIT),
        name="prep",
    )(x2, pos2, ln0g, ln0b, win, gmg, gmb, ws, bs, qg, wuq, kvg, wukv, freq, phase, sign)


def _attn_kernel(q_ref, k_ref, v_ref, o_ref, s_ref, mx_ref, ls_ref, acc_ref):
    qi = pl.program_id(1)
    tq = q_ref.shape[0]
    nt = (((1,), (1,)), ((), ()))
    n_kv = SEQ // ATT_K

    half = tq // 2
    row_chunk = (lax.broadcasted_iota(jnp.int32, (half, half), 0)) // CHUNK
    col_chunk = (lax.broadcasted_iota(jnp.int32, (half, half), 1)) // CHUNK
    quad_allowed = col_chunk <= row_chunk
    diag_start = pl.multiple_of(qi * ATT_K, ATT_K)
    masked = jnp.float32(-1e30)

    def lane_tiles(x):
        return [x[:, t * LANES:(t + 1) * LANES] for t in range(x.shape[1] // LANES)]

    def lane_fold(x, op):
        return functools.reduce(op, lane_tiles(x))

    def scores(hd, j):
        start = pl.multiple_of(j * ATT_K, ATT_K)
        q_h = q_ref[:, hd * 2 * LANES:(hd + 1) * 2 * LANES]
        kb = k_ref[pl.ds(start, ATT_K), hd * 2 * LANES:(hd + 1) * 2 * LANES]
        return lax.dot_general(q_h, kb, nt, preferred_element_type=jnp.float32)

    for hd in range(MLA_HEADS):
        cols = slice(hd * 2 * LANES, (hd + 1) * 2 * LANES)
        k_lo = k_ref[pl.ds(diag_start, half), cols]
        k_hi = k_ref[pl.ds(diag_start + half, half), cols]
        qk = lambda q, k: lax.dot_general(q, k, nt, preferred_element_type=jnp.float32)
        s_tl = jnp.where(quad_allowed, qk(q_ref[:half, cols], k_lo), masked)
        s_bl = qk(q_ref[half:, cols], k_lo)
        s_br = jnp.where(quad_allowed, qk(q_ref[half:, cols], k_hi), masked)
        diag = s_ref.at[hd * n_kv + qi]
        diag[:half, :half] = s_tl
        diag[half:, :half] = s_bl
        diag[half:, half:] = s_br
        mx_ref[hd, :half] = lane_fold(s_tl, jnp.maximum)
        mx_ref[hd, half:] = jnp.maximum(lane_fold(s_bl, jnp.maximum), lane_fold(s_br, jnp.maximum))

    def pass_a(j, c):
        for hd in range(MLA_HEADS):
            s = scores(hd, j)
            s_ref[hd * n_kv + j] = s
            mx_ref[hd] = jnp.maximum(mx_ref[hd], lane_fold(s, jnp.maximum))
        return c

    lax.fori_loop(0, qi, pass_a, 0)

    for hd in range(MLA_HEADS):
        mx_ref[hd] = jnp.broadcast_to(jnp.max(mx_ref[hd], axis=-1, keepdims=True), (tq, LANES))

    def probs(hd, j):
        s = s_ref[hd * n_kv + j]
        mb = mx_ref[hd]
        p = jnp.exp2(jnp.concatenate([t - mb for t in lane_tiles(s)], axis=-1))
        start = pl.multiple_of(j * ATT_K, ATT_K)
        vb = v_ref[pl.ds(start, ATT_K), hd * LANES:(hd + 1) * LANES]
        pv = jnp.dot(p.astype(jnp.bfloat16), vb, preferred_element_type=jnp.float32)
        return lane_fold(p, jnp.add), pv

    for hd in range(MLA_HEADS):
        diag = s_ref.at[hd * n_kv + qi]
        vcols = slice(hd * LANES, (hd + 1) * LANES)
        p_top = jnp.exp2(jnp.concatenate([t - mx_ref[hd, :half] for t in lane_tiles(diag[:half, :half])], axis=-1))
        p_bot = jnp.exp2(jnp.concatenate([t - mx_ref[hd, half:] for t in lane_tiles(diag[half:, :])], axis=-1))
        ls_ref[hd, :half] = lane_fold(p_top, jnp.add)
        ls_ref[hd, half:] = lane_fold(p_bot, jnp.add)
        acc_ref[hd, :half] = jnp.dot(p_top.astype(jnp.bfloat16), v_ref[pl.ds(diag_start, half), vcols],
                                     preferred_element_type=jnp.float32)
        acc_ref[hd, half:] = jnp.dot(p_bot.astype(jnp.bfloat16), v_ref[pl.ds(diag_start, ATT_K), vcols],
                                     preferred_element_type=jnp.float32)

    def pass_b(j, c):
        for hd in range(MLA_HEADS):
            ls, pv = probs(hd, j)
            ls_ref[hd] = ls_ref[hd] + ls
            acc_ref[hd] = acc_ref[hd] + pv
        return c

    lax.fori_loop(0, qi, pass_b, 0)

    for hd in range(MLA_HEADS):
        l = jnp.sum(ls_ref[hd], axis=-1, keepdims=True)
        o_ref[:, hd * V_HEAD_DIM:(hd + 1) * V_HEAD_DIM] = (acc_ref[hd] / l).astype(jnp.bfloat16)


def _attn(q, k, v):
    tq = ATT_Q
    tokblk = lambda cols: pl.BlockSpec((tq, cols), lambda b, i: (b * N_QBLK + i, 0))
    seqblk = lambda cols: pl.BlockSpec((SEQ, cols), lambda b, i: (b, 0))
    return pl.pallas_call(
        _attn_kernel,
        grid=(BATCH, N_QBLK),
        in_specs=[tokblk(D_MODEL), seqblk(D_MODEL), seqblk(GM_WIDTH)],
        out_specs=tokblk(GM_WIDTH),
        out_shape=jax.ShapeDtypeStruct((N_TOK, MLA_HEADS * V_HEAD_DIM), jnp.bfloat16),
        scratch_shapes=[pltpu.VMEM((MLA_HEADS * (SEQ // ATT_K), tq, ATT_K), jnp.float32),
                        pltpu.VMEM((MLA_HEADS, tq, LANES), jnp.float32),
                        pltpu.VMEM((MLA_HEADS, tq, LANES), jnp.float32),
                        pltpu.VMEM((MLA_HEADS, tq, V_HEAD_DIM), jnp.float32)],
        compiler_params=pltpu.CompilerParams(
            dimension_semantics=("arbitrary", "arbitrary"), vmem_limit_bytes=VMEM_LIMIT),
        name="attn",
    )(q, k, v)


def _proj_kernel(outa_ref, ob_ref, h_hbm_ref, wout_ref, ln1g_ref, ln1b_ref, wr_ref, br_ref,
                 h1_ref, h1b_ref, ri_ref, rf_ref, proj_ref, hring_ref, hsems):
    i = pl.program_id(0)
    tb = outa_ref.shape[0]
    h_ref = _ring_block(h_hbm_ref, hring_ref, hsems, lambda step: jnp.maximum(step - 1, 0), tb)

    @pl.when(i == 0)
    def _():
        proj_ref[...] = jnp.zeros_like(proj_ref)

    h1 = _layer_norm(ALPHA * h_ref[...] + proj_ref[(i + 1) % 2], ln1g_ref[...], ln1b_ref[...])
    h1_ref[...] = h1
    h1b_ref[...] = h1.astype(jnp.bfloat16)

    logits_tm = jnp.dot(h1b_ref[...], wr_ref[...], preferred_element_type=jnp.float32)
    logits = logits_tm.T[0:ROUTER_ROWS] + br_ref[...]

    sub_i = lax.broadcasted_iota(jnp.int32, (SUBLANES, tb), 0)
    sub = sub_i.astype(jnp.float32)
    neg = jnp.float32(-jnp.inf)
    g = jnp.where(sub_i < N_GROUPS, logits[0:SUBLANES], neg)
    gmax = jnp.max(g, axis=0, keepdims=True)
    g_top = jnp.min(jnp.where(g == gmax, sub, float(SUBLANES)), axis=0, keepdims=True)
    p_group = 1.0 / jnp.sum(jnp.exp(g - gmax), axis=0, keepdims=True)
    sel = logits[SUBLANES:2 * SUBLANES]
    for grp in range(1, N_GROUPS):
        sel = jnp.where(g_top == float(grp), logits[(grp + 1) * SUBLANES:(grp + 2) * SUBLANES], sel)
    v1 = jnp.max(sel, axis=0, keepdims=True)
    i1 = jnp.min(jnp.where(sel == v1, sub, float(SUBLANES)), axis=0, keepdims=True)
    sel2 = jnp.where(sub == i1, neg, sel)
    v2 = jnp.max(sel2, axis=0, keepdims=True)
    i2 = jnp.min(jnp.where(sel2 == v2, sub, float(SUBLANES)), axis=0, keepdims=True)
    e21 = jnp.exp(v2 - v1)
    w1 = 1.0 / (1.0 + e21)
    gate1 = p_group * w1
    gate2 = p_group * (e21 * w1)
    e1 = g_top * EXPERTS_PER_GROUP + i1
    e2 = g_top * EXPERTS_PER_GROUP + i2
    ri_ref[...] = jnp.where(sub_i == 0, e1, jnp.where(sub_i == 1, e2, 0.0)).astype(jnp.int32)
    rf_ref[...] = jnp.where(sub_i == 0, gate1, jnp.where(sub_i == 1, gate2, 0.0))

    proj_ref[i % 2] = (jnp.dot(outa_ref[...], wout_ref[:GM_WIDTH, :], preferred_element_type=jnp.float32)
                       + jnp.dot(ob_ref[...], wout_ref[GM_WIDTH:, :], preferred_element_type=jnp.float32))


def _proj(outa, ob, h, wout, ln1g, ln1b, wr, br):
    tb = PROJ_TOKENS
    n_blk = N_TOK // tb
    cur = lambda i: jnp.minimum(i, n_blk - 1)
    prev = lambda i: jnp.maximum(i - 1, 0)
    full = lambda shape: pl.BlockSpec(shape, lambda i: (0,) * len(shape))
    return pl.pallas_call(
        _proj_kernel,
        grid=(n_blk + 1,),
        in_specs=[pl.BlockSpec((tb, GM_WIDTH), lambda i: (cur(i), 0)),
                  pl.BlockSpec((tb, GM_WIDTH), lambda i: (cur(i), 0)),
                  pl.BlockSpec(memory_space=pl.ANY),
                  full((D_MODEL, D_MODEL)), full((1, D_MODEL)), full((1, D_MODEL)),
                  full((D_MODEL, LANES)), full((ROUTER_ROWS, 1))],
        out_specs=[pl.BlockSpec((tb, D_MODEL), lambda i: (prev(i), 0)),
                   pl.BlockSpec((tb, D_MODEL), lambda i: (prev(i), 0)),
                   pl.BlockSpec((SUBLANES, tb), lambda i: (0, prev(i))),
                   pl.BlockSpec((SUBLANES, tb), lambda i: (0, prev(i)))],
        out_shape=[jax.ShapeDtypeStruct((N_TOK, D_MODEL), jnp.float32),
                   jax.ShapeDtypeStruct((N_TOK, D_MODEL), jnp.bfloat16),
                   jax.ShapeDtypeStruct((SUBLANES, N_TOK), jnp.int32),
                   jax.ShapeDtypeStruct((SUBLANES, N_TOK), jnp.float32)],
        scratch_shapes=[pltpu.VMEM((2, tb, D_MODEL), jnp.float32)] + _ring_scratch(tb, D_MODEL, jnp.float32),
        compiler_params=pltpu.CompilerParams(
            dimension_semantics=("arbitrary",), vmem_limit_bytes=VMEM_LIMIT),
        name="proj",
    )(outa, ob, h, wout, ln1g, ln1b, wr, br)


def _plan_kernel(ri_all_ref, ri_ref, ldest_ref, runs_ref, meta_ref, run_ref, start_ref, upper_ref):
    step = pl.program_id(0)
    tb = ri_ref.shape[1]
    f32 = jnp.float32
    er = lax.broadcasted_iota(jnp.int32, (N_EXPERTS, LANES), 0)
    ec = lax.broadcasted_iota(jnp.int32, (N_EXPERTS, LANES), 1)
    to_row = lambda col: jnp.sum(jnp.where(er == ec, col, 0.0), axis=0, keepdims=True)

    def expert_one_hot(ref):
        e_sub = lax.broadcasted_iota(jnp.int32, (N_EXPERTS, ref.shape[1]), 0)
        return e_sub == ref[0:1, :], e_sub == ref[1:2, :]

    @pl.when(step == 0)
    def _():
        oh1, oh2 = expert_one_hot(ri_all_ref)
        counts = jnp.sum(jnp.where(oh1 | oh2, 1.0, 0.0), axis=1, keepdims=True)
        padded = jnp.floor((counts + (EXPERT_ROWS - 1)) * (1.0 / EXPERT_ROWS)) * EXPERT_ROWS
        pad_end = jnp.sum(jnp.where(ec <= er, to_row(padded), 0.0), axis=1, keepdims=True)
        start_ref[...] = jnp.broadcast_to(pad_end - padded, start_ref.shape)
        run_ref[...] = jnp.zeros_like(run_ref)
        bstart = (lax.broadcasted_iota(jnp.int32, (N_EXPERTS, META_LANES), 1) * EXPERT_ROWS).astype(f32)
        blk_e = jnp.sum(jnp.where(pad_end <= bstart, 1.0, 0.0), axis=0, keepdims=True)
        blk_e = jnp.minimum(blk_e, N_EXPERTS - 1.0)
        n_used = pad_end[N_EXPERTS - 1:N_EXPERTS, :] * (1.0 / EXPERT_ROWS)
        pad3 = lambda r: jnp.concatenate(
            [r, jnp.zeros((1, META_LANES - LANES), f32)], axis=1)
        msub = lax.broadcasted_iota(jnp.int32, (SUBLANES, META_LANES), 0)
        meta = jnp.where(msub == 0, blk_e,
                         jnp.where(msub == 1, pad3(to_row(pad_end)),
                                   jnp.where(msub == 2, pad3(to_row(counts)),
                                             jnp.where(msub == 3, n_used, 0.0))))
        meta_ref[...] = meta.astype(jnp.int32)
        tr = lax.broadcasted_iota(jnp.int32, (tb, tb), 0)
        tc = lax.broadcasted_iota(jnp.int32, (tb, tb), 1)
        upper_ref[...] = jnp.where(tr < tc, 1.0, 0.0).astype(jnp.bfloat16)

    @pl.when(step > 0)
    def _():
        oh1, oh2 = expert_one_hot(ri_ref)
        oh = jnp.where(oh1 | oh2, 1.0, 0.0).astype(f32)
        blk_count = jnp.sum(oh, axis=1, keepdims=True)
        prefix = jnp.dot(oh.astype(jnp.bfloat16), upper_ref[...], preferred_element_type=f32)
        cnt_row = to_row(blk_count)
        lstart = jnp.sum(jnp.where(ec < er, cnt_row, 0.0), axis=1, keepdims=True)
        base = prefix + lstart
        d1 = jnp.sum(jnp.where(oh1, base, 0.0), axis=0, keepdims=True)
        d2 = jnp.sum(jnp.where(oh2, base, 0.0), axis=0, keepdims=True)
        sub = lax.broadcasted_iota(jnp.int32, (SUBLANES, tb), 0)
        ldest_ref[...] = jnp.where(sub == 0, d1, jnp.where(sub == 1, d2, 0.0)).astype(jnp.int32)
        gstart = start_ref[:, 0:1] + run_ref[:, 0:1]
        rsub = lax.broadcasted_iota(jnp.int32, (SUBLANES, LANES), 0)
        runs = jnp.where(rsub == 0, cnt_row,
                         jnp.where(rsub == 1, to_row(lstart), jnp.where(rsub == 2, to_row(gstart), 0.0)))
        runs_ref[...] = runs.astype(jnp.int32)
        run_ref[...] = run_ref[...] + blk_count


def _plan(ri):
    tb = MOE_TOKENS
    blk = lambda i: jnp.maximum(i - 1, 0)
    return pl.pallas_call(
        _plan_kernel,
        grid=(N_MOE_BLOCKS + 1,),
        in_specs=[pl.BlockSpec((SUBLANES, N_TOK), lambda i: (0, 0)),
                  pl.BlockSpec((SUBLANES, tb), lambda i: (0, blk(i)))],
        out_specs=[pl.BlockSpec((SUBLANES, tb), lambda i: (0, blk(i))),
                   pl.BlockSpec((SUBLANES, LANES), lambda i: (blk(i), 0)),
                   pl.BlockSpec((SUBLANES, META_LANES), lambda i: (0, 0))],
        out_shape=[jax.ShapeDtypeStruct((SUBLANES, N_TOK), jnp.int32),
                   jax.ShapeDtypeStruct((N_MOE_BLOCKS * SUBLANES, LANES), jnp.int32),
                   jax.ShapeDtypeStruct((SUBLANES, META_LANES), jnp.int32)],
        scratch_shapes=[pltpu.VMEM((N_EXPERTS, LANES), jnp.float32),
                        pltpu.VMEM((N_EXPERTS, LANES), jnp.float32),
                        pltpu.VMEM((tb, tb), jnp.bfloat16)],
        compiler_params=pltpu.CompilerParams(
            dimension_semantics=("arbitrary",), vmem_limit_bytes=VMEM_LIMIT),
        name="plan",
    )(ri, ri)


def _for_each_run_piece(runs_ref, row0, fn):
    for e in range(N_EXPERTS):
        n, lstart, gstart = runs_ref[row0, e], runs_ref[row0 + 1, e], runs_ref[row0 + 2, e]
        for bit in range(RUN_BITS):
            @pl.when((n & (1 << bit)) != 0)
            def _(n=n, lstart=lstart, gstart=gstart, bit=bit):
                off = (n >> (bit + 1)) << (bit + 1)
                fn(lstart + off, gstart + off, 1 << bit)


def _tile_rows(ref, row, rows):
    return ref.at[pl.ds(pl.multiple_of(row * FEAT_TILES, FEAT_TILES), rows * FEAT_TILES)]


def _dispatch_kernel(meta_ref, runs_ref, ldest_ref, h1b_ref, buf_ref, *scratch):
    sorted_refs = scratch[:MOE_GROUP]
    zero_ref, sems, zsem = scratch[MOE_GROUP:]
    i = pl.program_id(0)
    n_steps = pl.num_programs(0)
    tb = MOE_TOKENS
    slot = i % 2
    block_tiles = 2 * tb * FEAT_TILES

    def wait_slot(g, s):
        pltpu.make_async_copy(sorted_refs[g].at[s], buf_ref.at[pl.ds(0, block_tiles)], sems.at[g, s]).wait()

    @pl.when(i == 0)
    def _():
        zero_ref[...] = jnp.zeros_like(zero_ref)

        def zero_copy(e):
            start = pl.multiple_of((meta_ref[1, e] - EXPERT_ROWS) * FEAT_TILES, EXPERT_ROWS * FEAT_TILES)
            return pltpu.make_async_copy(
                zero_ref, buf_ref.at[pl.ds(start, EXPERT_ROWS * FEAT_TILES)], zsem)

        def start_zero(e, c):
            @pl.when(meta_ref[2, e] > 0)
            def _():
                zero_copy(e).start()
            return c

        def wait_zero(e, c):
            @pl.when(meta_ref[2, e] > 0)
            def _():
                zero_copy(e).wait()
            return c

        def tail_copy(b):
            start = pl.multiple_of(b * (EXPERT_ROWS * FEAT_TILES), EXPERT_ROWS * FEAT_TILES)
            return pltpu.make_async_copy(
                zero_ref, buf_ref.at[pl.ds(start, EXPERT_ROWS * FEAT_TILES)], zsem)

        def start_tail(b, c):
            tail_copy(b).start()
            return c

        def wait_tail(b, c):
            tail_copy(b).wait()
            return c

        lax.fori_loop(0, N_EXPERTS, start_zero, 0)
        lax.fori_loop(meta_ref[3, 0], N_ROW_BLOCKS, start_tail, 0)
        lax.fori_loop(0, N_EXPERTS, wait_zero, 0)
        lax.fori_loop(meta_ref[3, 0], N_ROW_BLOCKS, wait_tail, 0)

    @pl.when(i >= 2)
    def _():
        for g in range(MOE_GROUP):
            wait_slot(g, slot)

    for g in range(MOE_GROUP):
        x = h1b_ref[g * tb:(g + 1) * tb, :]
        ld0 = ldest_ref[0:1, g * tb:(g + 1) * tb]
        ld1 = ldest_ref[1:2, g * tb:(g + 1) * tb]
        for c in range(2 * tb // SORT_CHUNK):
            r = lax.broadcasted_iota(jnp.int32, (SORT_CHUNK, tb), 0) + c * SORT_CHUNK
            perm = jnp.where((r == ld0) | (r == ld1), 1.0, 0.0).astype(jnp.bfloat16)
            rows = jnp.dot(perm, x, preferred_element_type=jnp.float32)
            _to_row_tiles(sorted_refs[g].at[slot, pl.ds(c * SORT_CHUNK * FEAT_TILES, SORT_CHUNK * FEAT_TILES)],
                          rows)

        def send(lrow, grow, rows, g=g):
            pltpu.make_async_copy(_tile_rows(sorted_refs[g].at[slot], lrow, rows),
                                  _tile_rows(buf_ref, grow, rows), sems.at[g, slot]).start()

        _for_each_run_piece(runs_ref, g * SUBLANES, send)

    @pl.when(i == n_steps - 1)
    def _():
        for g in range(MOE_GROUP):
            wait_slot(g, slot)
            wait_slot(g, 1 - slot)


def _dispatch(meta, runs, ldest, h1b):
    tb = MOE_GROUP * MOE_TOKENS
    return pl.pallas_call(
        _dispatch_kernel,
        grid_spec=pltpu.PrefetchScalarGridSpec(
            num_scalar_prefetch=1,
            grid=(N_MOE_BLOCKS // MOE_GROUP,),
            in_specs=[pl.BlockSpec((MOE_GROUP * SUBLANES, LANES), lambda i, m: (i, 0), memory_space=pltpu.SMEM),
                      pl.BlockSpec((SUBLANES, tb), lambda i, m: (0, i)),
                      pl.BlockSpec((tb, D_MODEL), lambda i, m: (i, 0))],
            out_specs=pl.BlockSpec(memory_space=pl.ANY),
            scratch_shapes=[pltpu.VMEM((2, 2 * MOE_TOKENS * FEAT_TILES, LANES), jnp.float32)] * MOE_GROUP + [
                pltpu.VMEM((EXPERT_ROWS * FEAT_TILES, LANES), jnp.float32),
                pltpu.SemaphoreType.DMA((MOE_GROUP, 2)), pltpu.SemaphoreType.DMA]),
        out_shape=jax.ShapeDtypeStruct((N_ROWS * FEAT_TILES, LANES), jnp.float32),
        compiler_params=pltpu.CompilerParams(
            dimension_semantics=("arbitrary",), vmem_limit_bytes=VMEM_LIMIT),
        name="dispatch",
    )(meta, runs, ldest, h1b)


def _sub_block_expert(meta, i, sub):
    return meta[0, jnp.minimum(i + sub * (N_ROW_BLOCKS // EXPERT_SUB), meta[3, 0] - 1)]


def _experts_kernel(meta_ref, x_hbm_ref, *refs):
    weights = [refs[3 * sub:3 * sub + 3] for sub in range(EXPERT_SUB)]
    o_ref, wgb_ref, wub_ref, wdb_ref, cached_ref, xring_ref, xsems = refs[3 * EXPERT_SUB:]
    n = EXPERT_ROWS
    i = pl.program_id(0)
    n_steps = pl.num_programs(0)

    def rows_copy(step):
        slot = step % EXPERT_RING
        start = pl.multiple_of(step * (n * FEAT_TILES), n * FEAT_TILES)
        return pltpu.make_async_copy(x_hbm_ref.at[:, pl.ds(start, n * FEAT_TILES), :], xring_ref.at[slot],
                                     xsems.at[slot])

    @pl.when(i == 0)
    def _():
        for step in range(EXPERT_RING - 1):
            rows_copy(step).start()

    @pl.when(i + (EXPERT_RING - 1) < n_steps)
    def _():
        rows_copy(i + (EXPERT_RING - 1)).start()

    rows_copy(i).wait()
    x_ref = xring_ref.at[i % EXPERT_RING]

    @pl.when(i == 0)
    def _():
        for sub in range(EXPERT_SUB):
            cached_ref[sub] = -1

    for sub in range(EXPERT_SUB):
        expert = _sub_block_expert(meta_ref, i, sub)

        @pl.when(cached_ref[sub] != expert)
        def _(sub=sub, expert=expert):
            wg_ref, wu_ref, wd_ref = weights[sub]
            wgb_ref[sub] = wg_ref[...].astype(jnp.bfloat16)
            wub_ref[sub] = wu_ref[...].astype(jnp.bfloat16)
            wdb_ref[sub] = wd_ref[...].astype(jnp.bfloat16)
            cached_ref[sub] = expert

    for sub in range(EXPERT_SUB):
        x = _from_row_tiles(x_ref.at[sub], n).astype(jnp.bfloat16)
        gate = jnp.dot(x, wgb_ref[sub], preferred_element_type=jnp.float32)
        up = jnp.dot(x, wub_ref[sub], preferred_element_type=jnp.float32)
        act = (gate * jax.nn.sigmoid(gate) * up).astype(jnp.bfloat16)
        _to_row_tiles(o_ref.at[sub], jnp.dot(act, wdb_ref[sub], preferred_element_type=jnp.float32))


def _experts(meta, buf, wg, wu, wd):
    sub_tiles = (N_ROW_BLOCKS // EXPERT_SUB) * EXPERT_ROWS * FEAT_TILES

    def weight_spec(shape, sub):
        return pl.BlockSpec((None,) + shape, lambda i, m: (_sub_block_expert(m, i, sub), 0, 0))

    rows_shape = (EXPERT_SUB, EXPERT_ROWS * FEAT_TILES, LANES)
    up_shape, down_shape = (D_MODEL, EXPERT_FF), (EXPERT_FF, D_MODEL)
    eout = pl.pallas_call(
        _experts_kernel,
        grid_spec=pltpu.PrefetchScalarGridSpec(
            num_scalar_prefetch=1,
            grid=(N_ROW_BLOCKS // EXPERT_SUB,),
            in_specs=[pl.BlockSpec(memory_space=pl.ANY)] + [
                weight_spec(shape, sub) for sub in range(EXPERT_SUB) for shape in (up_shape, up_shape, down_shape)],
            out_specs=pl.BlockSpec(rows_shape, lambda i, m: (0, i, 0)),
            scratch_shapes=[pltpu.VMEM((EXPERT_SUB,) + up_shape, jnp.bfloat16),
                            pltpu.VMEM((EXPERT_SUB,) + up_shape, jnp.bfloat16),
                            pltpu.VMEM((EXPERT_SUB,) + down_shape, jnp.bfloat16),
                            pltpu.SMEM((EXPERT_SUB,), jnp.int32),
                            pltpu.VMEM((EXPERT_RING,) + rows_shape, jnp.float32),
                            pltpu.SemaphoreType.DMA((EXPERT_RING,))]),
        out_shape=jax.ShapeDtypeStruct((EXPERT_SUB, sub_tiles, LANES), jnp.float32),
        compiler_params=pltpu.CompilerParams(
            dimension_semantics=("arbitrary",), vmem_limit_bytes=VMEM_LIMIT),
        name="experts",
    )(meta, buf.reshape(EXPERT_SUB, sub_tiles, LANES), *([wg, wu, wd] * EXPERT_SUB))
    return eout.reshape(N_ROWS * FEAT_TILES, LANES)


def _combine_kernel(runs_ref, runs_next_ref, ldest_ref, rf_ref, h1_ref, eout_ref, ln2g_ref, ln2b_ref,
                    o_ref, *scratch):
    y_refs = scratch[:MOE_GROUP]
    sems = scratch[MOE_GROUP]
    i = pl.program_id(0)
    n_steps = pl.num_programs(0)
    tb = MOE_TOKENS
    slot = i % 2
    block_tiles = 2 * tb * FEAT_TILES

    def fetch(table_ref, s):
        for g in range(MOE_GROUP):
            def recv(lrow, grow, rows, g=g):
                pltpu.make_async_copy(_tile_rows(eout_ref, grow, rows),
                                      _tile_rows(y_refs[g].at[s], lrow, rows), sems.at[g, s]).start()
            _for_each_run_piece(table_ref, g * SUBLANES, recv)

    @pl.when(i == 0)
    def _():
        fetch(runs_ref, slot)

    @pl.when(i + 1 < n_steps)
    def _():
        fetch(runs_next_ref, 1 - slot)

    for g in range(MOE_GROUP):
        pltpu.make_async_copy(eout_ref.at[pl.ds(0, block_tiles)], y_refs[g].at[slot], sems.at[g, slot]).wait()

        ld = ldest_ref[:, g * tb:(g + 1) * tb].astype(jnp.float32).T
        gates = rf_ref[:, g * tb:(g + 1) * tb].T
        y = None
        col = lax.broadcasted_iota(jnp.int32, (tb, SORT_CHUNK), 1).astype(jnp.float32)
        for c in range(2 * tb // SORT_CHUNK):
            ld_c = ld - float(c * SORT_CHUNK)
            gm = jnp.where(col == ld_c[:, 0:1], gates[:, 0:1],
                           jnp.where(col == ld_c[:, 1:2], gates[:, 1:2], 0.0)).astype(jnp.bfloat16)
            rows = _from_row_tiles(
                y_refs[g].at[slot, pl.ds(c * SORT_CHUNK * FEAT_TILES, SORT_CHUNK * FEAT_TILES)], SORT_CHUNK)
            part = jnp.dot(gm, rows.astype(jnp.bfloat16), preferred_element_type=jnp.float32)
            y = part if y is None else y + part
        tok = slice(g * tb, (g + 1) * tb)
        o_ref[tok, :] = _layer_norm(ALPHA * h1_ref[tok, :] + y, ln2g_ref[...], ln2b_ref[...])


def _combine(runs, ldest, rf, h1, eout, ln2g, ln2b):
    tb = MOE_GROUP * MOE_TOKENS
    n_steps = N_MOE_BLOCKS // MOE_GROUP
    table = lambda index_map: pl.BlockSpec((MOE_GROUP * SUBLANES, LANES), index_map, memory_space=pltpu.SMEM)
    return pl.pallas_call(
        _combine_kernel,
        grid=(n_steps,),
        in_specs=[table(lambda i: (i, 0)), table(lambda i: (jnp.minimum(i + 1, n_steps - 1), 0)),
                  pl.BlockSpec((SUBLANES, tb), lambda i: (0, i)),
                  pl.BlockSpec((SUBLANES, tb), lambda i: (0, i)),
                  pl.BlockSpec((tb, D_MODEL), lambda i: (i, 0)),
                  pl.BlockSpec(memory_space=pl.ANY),
                  pl.BlockSpec((1, D_MODEL), lambda i: (0, 0)),
                  pl.BlockSpec((1, D_MODEL), lambda i: (0, 0))],
        out_specs=pl.BlockSpec((tb, D_MODEL), lambda i: (i, 0)),
        out_shape=jax.ShapeDtypeStruct((N_TOK, D_MODEL), jnp.float32),
        scratch_shapes=[pltpu.VMEM((2, 2 * MOE_TOKENS * FEAT_TILES, LANES), jnp.float32)] * MOE_GROUP + [
            pltpu.SemaphoreType.DMA((MOE_GROUP, 2))],
        compiler_params=pltpu.CompilerParams(
            dimension_semantics=("arbitrary",), vmem_limit_bytes=VMEM_LIMIT),
        name="combine",
    )(runs, runs, ldest, rf, h1, eout, ln2g, ln2b)


def _swap_halves(w):
    half = w.shape[-1] // 2
    return jnp.concatenate([w[..., half:], w[..., :half]], axis=-1)


def kernel(x, positions, ln0_g, ln0_b, w_in, gm_ln_g, gm_ln_b, w_spatial, b_spatial, q_norm_g, w_uq, kv_norm_g, w_ukv, w_out, ln1_g, ln1_b, w_router_group, b_router_group, w_router_expert, b_router_expert, w_gate, w_up, w_down, ln2_g, ln2_b):
    bf16 = jnp.bfloat16
    row = lambda a: a.reshape(1, -1)

    w_in0 = w_in[0]
    kr_cols = w_in0[:, O_KR:O_KR + QK_ROPE_DIM]
    win = jnp.concatenate([w_in0, _swap_halves(kr_cols)], axis=1).astype(bf16)
    wuq3 = w_uq[0].reshape(Q_LORA_RANK, MLA_HEADS, QK_NOPE_DIM + QK_ROPE_DIM)
    rope_cols = wuq3[:, :, QK_NOPE_DIM:]
    wuq = jnp.concatenate([wuq3, _swap_halves(rope_cols)], axis=-1).reshape(Q_LORA_RANK, D_MODEL).astype(bf16)
    wukv = w_ukv[0].astype(bf16)
    wout = w_out[0].astype(bf16)
    bs = jnp.broadcast_to(b_spatial[0][:, :, None], (GM_HEADS, GM_CHUNK, GM_HEAD_DIM))
    wr = jnp.concatenate([w_router_group[0], jnp.zeros((D_MODEL, SUBLANES - N_GROUPS), jnp.float32),
                          w_router_expert[0],
                          jnp.zeros((D_MODEL, LANES - ROUTER_ROWS), jnp.float32)],
                         axis=1).astype(bf16)
    br = jnp.concatenate([b_router_group[0], jnp.zeros((SUBLANES - N_GROUPS,), jnp.float32),
                          b_router_expert[0]]).reshape(ROUTER_ROWS, 1)

    inv_freq = ROPE_THETA ** (-jnp.arange(0, QK_ROPE_DIM, 2, dtype=jnp.float32) / QK_ROPE_DIM)
    freq = jnp.tile(inv_freq, 4).reshape(1, LANES)
    quarter = QK_ROPE_DIM // 2
    phase = jnp.concatenate([jnp.zeros((2 * quarter,), jnp.float32),
                             jnp.full((2 * quarter,), math.pi / 2, jnp.float32)]).reshape(1, LANES)
    sign = jnp.concatenate([jnp.ones((2 * quarter,), jnp.float32), -jnp.ones((quarter,), jnp.float32),
                            jnp.ones((quarter,), jnp.float32)]).reshape(1, LANES)

    x2 = x.reshape(N_TOK, D_MODEL)
    pos2 = positions.reshape(N_TOK // PREP_TOKENS, 1, PREP_TOKENS)

    h, outa, q, k, v = _prep(x2, pos2, row(ln0_g), row(ln0_b), win, row(gm_ln_g[0]), row(gm_ln_b[0]),
                             w_spatial[0], bs, row(q_norm_g[0]), wuq, row(kv_norm_g[0]), wukv,
                             freq, phase, sign)
    ob = _attn(q, k, v)
    h1, h1b, ri, rf = _proj(outa, ob, h, wout, row(ln1_g[0]), row(ln1_b[0]), wr, br)
    ldest, runs, meta = _plan(ri)
    buf = _dispatch(meta, runs, ldest, h1b)
    eout = _experts(meta, buf, w_gate[0], w_up[0], w_down[0])
    out = _combine(runs, ldest, rf, h1, eout, row(ln2_g[0]), row(ln2_b[0]))
    return out.reshape(BATCH, SEQ, D_MODEL)
```

```python
import functools
import math

import jax
import jax.numpy as jnp
from jax import lax
from jax.experimental import pallas as pl
from jax.experimental.pallas import tpu as pltpu

D_MODEL = 1024
BATCH = 16
SEQ = 2048
N_TOK = BATCH * SEQ
CHUNK = 64
GM_WIDTH = 512
GM_HEADS = 4
GM_HEAD_DIM = 128
GM_CHUNK = 128
MLA_HEADS = 4
QK_NOPE_DIM = 128
QK_ROPE_DIM = 64
V_HEAD_DIM = 128
Q_LORA_RANK = 384
KV_LORA_RANK = 256
ROPE_THETA = 10000.0
N_GROUPS = 4
EXPERTS_PER_GROUP = 8
N_EXPERTS = 32
TOP_K = 2
EXPERT_FF = 256
ALPHA = 2.0 ** 0.25
QK_SCALE = (QK_NOPE_DIM + QK_ROPE_DIM) ** -0.5 * math.log2(math.e)

LANES = 128
SUBLANES = 8
FEAT_TILES = D_MODEL // LANES
PREP_TOKENS = 1024
ATT_Q = 512
ATT_K = 512
N_QBLK = SEQ // ATT_Q
PROJ_TOKENS = 1024
MOE_TOKENS = 512
N_MOE_BLOCKS = N_TOK // MOE_TOKENS
MOE_GROUP = 2
RUN_BITS = (TOP_K * MOE_TOKENS).bit_length()
SORT_CHUNK = 256
EXPERT_ROWS = 256
EXPERT_SUB = 4
EXPERT_RING = 3
INPUT_RING = 3
N_ROWS = N_TOK * TOP_K + N_EXPERTS * EXPERT_ROWS
N_ROW_BLOCKS = N_ROWS // EXPERT_ROWS
META_LANES = 384
IN_COLS = 2 * GM_WIDTH + Q_LORA_RANK + KV_LORA_RANK + 2 * QK_ROPE_DIM
O_Q = 2 * GM_WIDTH
O_KV = O_Q + Q_LORA_RANK
O_KR = O_KV + KV_LORA_RANK
ROUTER_ROWS = 40
VMEM_LIMIT = 48 * 1024 * 1024
EXPERT_VMEM_LIMIT = 56 * 1024 * 1024

assert N_ROW_BLOCKS <= META_LANES and N_ROW_BLOCKS % EXPERT_SUB == 0


def _layer_norm(x, g, b, eps=1e-5):
    mu = jnp.mean(x, axis=-1, keepdims=True)
    xc = x - mu
    var = jnp.mean(xc * xc, axis=-1, keepdims=True)
    return xc * lax.rsqrt(var + eps) * g + b


def _rms_norm(x, g, eps=1e-6):
    return x * lax.rsqrt(jnp.mean(x * x, axis=-1, keepdims=True) + eps) * g


def _gelu_tanh(x):
    c = math.sqrt(2.0 / math.pi)
    return 0.5 * x * (1.0 + jnp.tanh(c * (x + 0.044715 * (x * x * x))))


def _to_row_tiles(ref, x):
    n = x.shape[0]
    for s in range(FEAT_TILES):
        ref[pl.ds(s, n, stride=FEAT_TILES), :] = x[:, s * LANES:(s + 1) * LANES]


def _from_row_tiles(ref, n):
    return jnp.concatenate(
        [ref[pl.ds(s, n, stride=FEAT_TILES), :] for s in range(FEAT_TILES)], axis=-1)


def _ring_block(hbm_ref, ring_ref, sems, block_of_step, rows):
    i = pl.program_id(0)
    n_steps = pl.num_programs(0)

    def copy(step):
        start = pl.multiple_of(block_of_step(step) * rows, rows)
        slot = step % INPUT_RING
        return pltpu.make_async_copy(hbm_ref.at[pl.ds(start, rows)], ring_ref.at[slot], sems.at[slot])

    @pl.when(i == 0)
    def _():
        for step in range(INPUT_RING - 1):
            copy(step).start()

    @pl.when(i + (INPUT_RING - 1) < n_steps)
    def _():
        copy(i + (INPUT_RING - 1)).start()

    copy(i).wait()
    return ring_ref.at[i % INPUT_RING]


def _ring_scratch(rows, cols, dtype):
    return [pltpu.VMEM((INPUT_RING, rows, cols), dtype), pltpu.SemaphoreType.DMA((INPUT_RING,))]


def _prep_kernel(x_ref, pos_ref, ln0g_ref, ln0b_ref, win_ref, gmg_ref, gmb_ref, ws_ref, bs_ref,
                 qg_ref, wuq_ref, kvg_ref, wukv_ref, freq_ref, phase_ref, sign_ref,
                 h_ref, outa_ref, q_ref, k_ref, v_ref, tabc_ref, tabs_ref, rot_ref):
    tb = x_ref.shape[0]

    @pl.when(pl.program_id(0) == 0)
    def _():
        d = lax.broadcasted_iota(jnp.int32, (tb, LANES), 0).astype(jnp.float32) * freq_ref[...]
        tabc_ref[...] = jnp.cos(d)
        tabs_ref[...] = jnp.sin(d)

    pos_row = pos_ref[0]
    p0 = pos_row[:, 0:1]
    offset = lax.broadcasted_iota(jnp.int32, (1, tb), 1)
    consecutive = jnp.max(jnp.abs((pos_row - p0 - offset).astype(jnp.float32))) == 0.0

    @pl.when(consecutive)
    def _():
        a0 = p0.astype(jnp.float32) * freq_ref[...]
        c0, s0 = jnp.cos(a0), jnp.sin(a0)
        lane = lax.broadcasted_iota(jnp.int32, (1, LANES), 1)
        coef_c = jnp.where(lane < 2 * 32, c0, jnp.where(lane < 3 * 32, -s0, s0))
        coef_s = jnp.where(lane < 2 * 32, -s0, jnp.where(lane < 3 * 32, -c0, c0))
        rot_ref[...] = coef_c * tabc_ref[...] + coef_s * tabs_ref[...]

    @pl.when(jnp.logical_not(consecutive))
    def _():
        pos_col = jnp.broadcast_to(pos_row.astype(jnp.float32), (SUBLANES, tb)).T[:, 0:1]
        ang = pos_col * freq_ref[...]
        rot_ref[...] = jnp.cos(ang - phase_ref[...]) * sign_ref[...]

    rot = rot_ref[...]

    h = _layer_norm(x_ref[...], ln0g_ref[...], ln0b_ref[...])
    h_ref[...] = h
    z = jnp.dot(h.astype(jnp.bfloat16), win_ref[...], preferred_element_type=jnp.float32)

    u = _gelu_tanh(z[:, :GM_WIDTH])
    v = _gelu_tanh(z[:, GM_WIDTH:2 * GM_WIDTH])
    row_chunk = lax.broadcasted_iota(jnp.int32, (GM_CHUNK, GM_CHUNK), 0) // CHUNK
    col_chunk = lax.broadcasted_iota(jnp.int32, (GM_CHUNK, GM_CHUNK), 1) // CHUNK
    allowed = col_chunk <= row_chunk
    for hd in range(GM_HEADS):
        lo, hi = hd * GM_HEAD_DIM, (hd + 1) * GM_HEAD_DIM
        vln = _layer_norm(v[:, lo:hi], gmg_ref[:, lo:hi], gmb_ref[:, lo:hi]).astype(jnp.bfloat16)
        wm = jnp.where(allowed, ws_ref[hd], 0.0).astype(jnp.bfloat16)
        for c in range(tb // GM_CHUNK):
            r0, r1 = c * GM_CHUNK, (c + 1) * GM_CHUNK
            f = jnp.dot(wm, vln[r0:r1], preferred_element_type=jnp.float32) + bs_ref[hd]
            outa_ref[r0:r1, lo:hi] = (u[r0:r1, lo:hi] * f).astype(jnp.bfloat16)

    ql = _rms_norm(z[:, O_Q:O_KV], qg_ref[...]).astype(jnp.bfloat16)
    qf = jnp.dot(ql, wuq_ref[...], preferred_element_type=jnp.float32)
    rot_s = rot * QK_SCALE
    q_parts = []
    for hd in range(MLA_HEADS):
        base = hd * 2 * LANES
        q_parts.append(qf[:, base:base + LANES] * QK_SCALE)
        q_parts.append(qf[:, base + LANES:base + 2 * LANES] * rot_s)
    q_ref[...] = jnp.concatenate(q_parts, axis=-1).astype(jnp.bfloat16)

    kvl = _rms_norm(z[:, O_KV:O_KR], kvg_ref[...]).astype(jnp.bfloat16)
    kv = jnp.dot(kvl, wukv_ref[...], preferred_element_type=jnp.float32)
    t = z[:, O_KR:O_KR + LANES] * rot
    krr = t + pltpu.roll(t, 2 * QK_ROPE_DIM // 2, axis=1)
    k_parts, v_parts = [], []
    for hd in range(MLA_HEADS):
        base = hd * 2 * LANES
        k_parts.append(kv[:, base:base + LANES])
        k_parts.append(krr)
        v_parts.append(kv[:, base + LANES:base + 2 * LANES])
    k_ref[...] = jnp.concatenate(k_parts, axis=-1).astype(jnp.bfloat16)
    v_ref[...] = jnp.concatenate(v_parts, axis=-1).astype(jnp.bfloat16)


def _prep(x2, pos2, ln0g, ln0b, win, gmg, gmb, ws, bs, qg, wuq, kvg, wukv, freq, phase, sign):
    tb = PREP_TOKENS
    full = lambda shape: pl.BlockSpec(shape, lambda i: (0,) * len(shape))
    tok = lambda cols: pl.BlockSpec((tb, cols), lambda i: (i, 0))
    return pl.pallas_call(
        _prep_kernel,
        grid=(N_TOK // tb,),
        in_specs=[tok(D_MODEL), pl.BlockSpec((1, 1, tb), lambda i: (i, 0, 0)), full((1, D_MODEL)), full((1, D_MODEL)),
                  full((D_MODEL, IN_COLS)), full((1, GM_WIDTH)), full((1, GM_WIDTH)),
                  full((GM_HEADS, GM_CHUNK, GM_CHUNK)), full((GM_HEADS, GM_CHUNK, GM_HEAD_DIM)),
                  full((1, Q_LORA_RANK)), full((Q_LORA_RANK, D_MODEL)),
                  full((1, KV_LORA_RANK)), full((KV_LORA_RANK, D_MODEL)),
                  full((1, LANES)), full((1, LANES)), full((1, LANES))],
        out_specs=[tok(D_MODEL), tok(GM_WIDTH), tok(D_MODEL), tok(D_MODEL), tok(GM_WIDTH)],
        out_shape=[jax.ShapeDtypeStruct((N_TOK, D_MODEL), jnp.float32),
                   jax.ShapeDtypeStruct((N_TOK, GM_WIDTH), jnp.bfloat16),
                   jax.ShapeDtypeStruct((N_TOK, D_MODEL), jnp.bfloat16),
                   jax.ShapeDtypeStruct((N_TOK, D_MODEL), jnp.bfloat16),
                   jax.ShapeDtypeStruct((N_TOK, GM_WIDTH), jnp.bfloat16)],
        scratch_shapes=[pltpu.VMEM((tb, LANES), jnp.float32)] * 3,
        compiler_params=pltpu.CompilerParams(
            dimension_semantics=("arbitrary",), vmem_limit_bytes=VMEM_LIMIT),
        name="prep",
    )(x2, pos2, ln0g, ln0b, win, gmg, gmb, ws, bs, qg, wuq, kvg, wukv, freq, phase, sign)


def _attn_kernel(q_ref, k_ref, v_ref, o_ref, s_ref, mx_ref, ls_ref, acc_ref):
    qi = pl.program_id(1)
    tq = q_ref.shape[0]
    nt = (((1,), (1,)), ((), ()))
    n_kv = SEQ // ATT_K

    half = tq // 2
    row_chunk = (lax.broadcasted_iota(jnp.int32, (half, half), 0)) // CHUNK
    col_chunk = (lax.broadcasted_iota(jnp.int32, (half, half), 1)) // CHUNK
    quad_allowed = col_chunk <= row_chunk
    diag_start = pl.multiple_of(qi * ATT_K, ATT_K)
    masked = jnp.float32(-1e30)

    def lane_tiles(x):
        return [x[:, t * LANES:(t + 1) * LANES] for t in range(x.shape[1] // LANES)]

    def lane_fold(x, op):
        return functools.reduce(op, lane_tiles(x))

    def scores(hd, j):
        start = pl.multiple_of(j * ATT_K, ATT_K)
        q_h = q_ref[:, hd * 2 * LANES:(hd + 1) * 2 * LANES]
        kb = k_ref[pl.ds(start, ATT_K), hd * 2 * LANES:(hd + 1) * 2 * LANES]
        return lax.dot_general(q_h, kb, nt, preferred_element_type=jnp.float32)

    for hd in range(MLA_HEADS):
        cols = slice(hd * 2 * LANES, (hd + 1) * 2 * LANES)
        k_lo = k_ref[pl.ds(diag_start, half), cols]
        k_hi = k_ref[pl.ds(diag_start + half, half), cols]
        qk = lambda q, k: lax.dot_general(q, k, nt, preferred_element_type=jnp.float32)
        s_tl = jnp.where(quad_allowed, qk(q_ref[:half, cols], k_lo), masked)
        s_bl = qk(q_ref[half:, cols], k_lo)
        s_br = jnp.where(quad_allowed, qk(q_ref[half:, cols], k_hi), masked)
        diag = s_ref.at[hd * n_kv + qi]
        diag[:half, :half] = s_tl
        diag[half:, :half] = s_bl
        diag[half:, half:] = s_br
        mx_ref[hd, :half] = lane_fold(s_tl, jnp.maximum)
        mx_ref[hd, half:] = jnp.maximum(lane_fold(s_bl, jnp.maximum), lane_fold(s_br, jnp.maximum))

    def pass_a(j, c):
        for hd in range(MLA_HEADS):
            s = scores(hd, j)
            s_ref[hd * n_kv + j] = s
            mx_ref[hd] = jnp.maximum(mx_ref[hd], lane_fold(s, jnp.maximum))
        return c

    lax.fori_loop(0, qi, pass_a, 0)

    for hd in range(MLA_HEADS):
        mx_ref[hd] = jnp.broadcast_to(jnp.max(mx_ref[hd], axis=-1, keepdims=True), (tq, LANES))

    def probs(hd, j):
        s = s_ref[hd * n_kv + j]
        mb = mx_ref[hd]
        p = jnp.exp2(jnp.concatenate([t - mb for t in lane_tiles(s)], axis=-1))
        start = pl.multiple_of(j * ATT_K, ATT_K)
        vb = v_ref[pl.ds(start, ATT_K), hd * LANES:(hd + 1) * LANES]
        pv = jnp.dot(p.astype(jnp.bfloat16), vb, preferred_element_type=jnp.float32)
        return lane_fold(p, jnp.add), pv

    for hd in range(MLA_HEADS):
        diag = s_ref.at[hd * n_kv + qi]
        vcols = slice(hd * LANES, (hd + 1) * LANES)
        p_top = jnp.exp2(jnp.concatenate([t - mx_ref[hd, :half] for t in lane_tiles(diag[:half, :half])], axis=-1))
        p_bot = jnp.exp2(jnp.concatenate([t - mx_ref[hd, half:] for t in lane_tiles(diag[half:, :])], axis=-1))
        ls_ref[hd, :half] = lane_fold(p_top, jnp.add)
        ls_ref[hd, half:] = lane_fold(p_bot, jnp.add)
        acc_ref[hd, :half] = jnp.dot(p_top.astype(jnp.bfloat16), v_ref[pl.ds(diag_start, half), vcols],
                                     preferred_element_type=jnp.float32)
        acc_ref[hd, half:] = jnp.dot(p_bot.astype(jnp.bfloat16), v_ref[pl.ds(diag_start, ATT_K), vcols],
                                     preferred_element_type=jnp.float32)

    def pass_b(j, c):
        for hd in range(MLA_HEADS):
            ls, pv = probs(hd, j)
            ls_ref[hd] = ls_ref[hd] + ls
            acc_ref[hd] = acc_ref[hd] + pv
        return c

    lax.fori_loop(0, qi, pass_b, 0)

    for hd in range(MLA_HEADS):
        l = jnp.sum(ls_ref[hd], axis=-1, keepdims=True)
        o_ref[:, hd * V_HEAD_DIM:(hd + 1) * V_HEAD_DIM] = (acc_ref[hd] / l).astype(jnp.bfloat16)


def _attn(q, k, v):
    tq = ATT_Q
    tokblk = lambda cols: pl.BlockSpec((tq, cols), lambda b, i: (b * N_QBLK + i, 0))
    seqblk = lambda cols: pl.BlockSpec((SEQ, cols), lambda b, i: (b, 0))
    return pl.pallas_call(
        _attn_kernel,
        grid=(BATCH, N_QBLK),
        in_specs=[tokblk(D_MODEL), seqblk(D_MODEL), seqblk(GM_WIDTH)],
        out_specs=tokblk(GM_WIDTH),
        out_shape=jax.ShapeDtypeStruct((N_TOK, MLA_HEADS * V_HEAD_DIM), jnp.bfloat16),
        scratch_shapes=[pltpu.VMEM((MLA_HEADS * (SEQ // ATT_K), tq, ATT_K), jnp.float32),
                        pltpu.VMEM((MLA_HEADS, tq, LANES), jnp.float32),
                        pltpu.VMEM((MLA_HEADS, tq, LANES), jnp.float32),
                        pltpu.VMEM((MLA_HEADS, tq, V_HEAD_DIM), jnp.float32)],
        compiler_params=pltpu.CompilerParams(
            dimension_semantics=("arbitrary", "arbitrary"), vmem_limit_bytes=VMEM_LIMIT),
        name="attn",
    )(q, k, v)


def _proj_kernel(outa_ref, ob_ref, h_hbm_ref, wout_ref, ln1g_ref, ln1b_ref, wr_ref, br_ref,
                 h1_ref, h1b_ref, ri_ref, rf_ref, proj_ref, hring_ref, hsems):
    i = pl.program_id(0)
    tb = outa_ref.shape[0]
    h_ref = _ring_block(h_hbm_ref, hring_ref, hsems, lambda step: jnp.maximum(step - 1, 0), tb)

    @pl.when(i == 0)
    def _():
        proj_ref[...] = jnp.zeros_like(proj_ref)

    h1 = _layer_norm(ALPHA * h_ref[...] + proj_ref[(i + 1) % 2], ln1g_ref[...], ln1b_ref[...])
    h1_ref[...] = h1
    h1b_ref[...] = h1.astype(jnp.bfloat16)

    logits_tm = jnp.dot(h1b_ref[...], wr_ref[...], preferred_element_type=jnp.float32)
    logits = logits_tm.T[0:ROUTER_ROWS] + br_ref[...]

    sub_i = lax.broadcasted_iota(jnp.int32, (SUBLANES, tb), 0)
    sub = sub_i.astype(jnp.float32)
    neg = jnp.float32(-jnp.inf)
    g = jnp.where(sub_i < N_GROUPS, logits[0:SUBLANES], neg)
    gmax = jnp.max(g, axis=0, keepdims=True)
    g_top = jnp.min(jnp.where(g == gmax, sub, float(SUBLANES)), axis=0, keepdims=True)
    p_group = 1.0 / jnp.sum(jnp.exp(g - gmax), axis=0, keepdims=True)
    sel = logits[SUBLANES:2 * SUBLANES]
    for grp in range(1, N_GROUPS):
        sel = jnp.where(g_top == float(grp), logits[(grp + 1) * SUBLANES:(grp + 2) * SUBLANES], sel)
    v1 = jnp.max(sel, axis=0, keepdims=True)
    i1 = jnp.min(jnp.where(sel == v1, sub, float(SUBLANES)), axis=0, keepdims=True)
    sel2 = jnp.where(sub == i1, neg, sel)
    v2 = jnp.max(sel2, axis=0, keepdims=True)
    i2 = jnp.min(jnp.where(sel2 == v2, sub, float(SUBLANES)), axis=0, keepdims=True)
    e21 = jnp.exp(v2 - v1)
    w1 = 1.0 / (1.0 + e21)
    gate1 = p_group * w1
    gate2 = p_group * (e21 * w1)
    e1 = g_top * EXPERTS_PER_GROUP + i1
    e2 = g_top * EXPERTS_PER_GROUP + i2
    ri_ref[...] = jnp.where(sub_i == 0, e1, jnp.where(sub_i == 1, e2, 0.0)).astype(jnp.int32)
    rf_ref[...] = jnp.where(sub_i == 0, gate1, jnp.where(sub_i == 1, gate2, 0.0))

    proj_ref[i % 2] = (jnp.dot(outa_ref[...], wout_ref[:GM_WIDTH, :], preferred_element_type=jnp.float32)
                       + jnp.dot(ob_ref[...], wout_ref[GM_WIDTH:, :], preferred_element_type=jnp.float32))


def _proj(outa, ob, h, wout, ln1g, ln1b, wr, br):
    tb = PROJ_TOKENS
    n_blk = N_TOK // tb
    cur = lambda i: jnp.minimum(i, n_blk - 1)
    prev = lambda i: jnp.maximum(i - 1, 0)
    full = lambda shape: pl.BlockSpec(shape, lambda i: (0,) * len(shape))
    return pl.pallas_call(
        _proj_kernel,
        grid=(n_blk + 1,),
        in_specs=[pl.BlockSpec((tb, GM_WIDTH), lambda i: (cur(i), 0)),
                  pl.BlockSpec((tb, GM_WIDTH), lambda i: (cur(i), 0)),
                  pl.BlockSpec(memory_space=pl.ANY),
                  full((D_MODEL, D_MODEL)), full((1, D_MODEL)), full((1, D_MODEL)),
                  full((D_MODEL, LANES)), full((ROUTER_ROWS, 1))],
        out_specs=[pl.BlockSpec((tb, D_MODEL), lambda i: (prev(i), 0)),
                   pl.BlockSpec((tb, D_MODEL), lambda i: (prev(i), 0)),
                   pl.BlockSpec((SUBLANES, tb), lambda i: (0, prev(i))),
                   pl.BlockSpec((SUBLANES, tb), lambda i: (0, prev(i)))],
        out_shape=[jax.ShapeDtypeStruct((N_TOK, D_MODEL), jnp.float32),
                   jax.ShapeDtypeStruct((N_TOK, D_MODEL), jnp.bfloat16),
                   jax.ShapeDtypeStruct((SUBLANES, N_TOK), jnp.int32),
                   jax.ShapeDtypeStruct((SUBLANES, N_TOK), jnp.float32)],
        scratch_shapes=[pltpu.VMEM((2, tb, D_MODEL), jnp.float32)] + _ring_scratch(tb, D_MODEL, jnp.float32),
        compiler_params=pltpu.CompilerParams(
            dimension_semantics=("arbitrary",), vmem_limit_bytes=VMEM_LIMIT),
        name="proj",
    )(outa, ob, h, wout, ln1g, ln1b, wr, br)


def _plan_kernel(ri_all_ref, ri_ref, ldest_ref, runs_ref, meta_ref, run_ref, start_ref, upper_ref):
    step = pl.program_id(0)
    tb = ri_ref.shape[1]
    f32 = jnp.float32
    er = lax.broadcasted_iota(jnp.int32, (N_EXPERTS, LANES), 0)
    ec = lax.broadcasted_iota(jnp.int32, (N_EXPERTS, LANES), 1)
    to_row = lambda col: jnp.sum(jnp.where(er == ec, col, 0.0), axis=0, keepdims=True)

    def expert_one_hot(ref):
        e_sub = lax.broadcasted_iota(jnp.int32, (N_EXPERTS, ref.shape[1]), 0)
        return e_sub == ref[0:1, :], e_sub == ref[1:2, :]

    @pl.when(step == 0)
    def _():
        oh1, oh2 = expert_one_hot(ri_all_ref)
        counts = jnp.sum(jnp.where(oh1 | oh2, 1.0, 0.0), axis=1, keepdims=True)
        padded = jnp.floor((counts + (EXPERT_ROWS - 1)) * (1.0 / EXPERT_ROWS)) * EXPERT_ROWS
        pad_end = jnp.sum(jnp.where(ec <= er, to_row(padded), 0.0), axis=1, keepdims=True)
        start_ref[...] = jnp.broadcast_to(pad_end - padded, start_ref.shape)
        run_ref[...] = jnp.zeros_like(run_ref)
        bstart = (lax.broadcasted_iota(jnp.int32, (N_EXPERTS, META_LANES), 1) * EXPERT_ROWS).astype(f32)
        blk_e = jnp.sum(jnp.where(pad_end <= bstart, 1.0, 0.0), axis=0, keepdims=True)
        blk_e = jnp.minimum(blk_e, N_EXPERTS - 1.0)
        n_used = pad_end[N_EXPERTS - 1:N_EXPERTS, :] * (1.0 / EXPERT_ROWS)
        pad3 = lambda r: jnp.concatenate(
            [r, jnp.zeros((1, META_LANES - LANES), f32)], axis=1)
        msub = lax.broadcasted_iota(jnp.int32, (SUBLANES, META_LANES), 0)
        meta = jnp.where(msub == 0, blk_e,
                         jnp.where(msub == 1, pad3(to_row(pad_end)),
                                   jnp.where(msub == 2, pad3(to_row(counts)),
                                             jnp.where(msub == 3, n_used, 0.0))))
        meta_ref[...] = meta.astype(jnp.int32)
        tr = lax.broadcasted_iota(jnp.int32, (tb, tb), 0)
        tc = lax.broadcasted_iota(jnp.int32, (tb, tb), 1)
        upper_ref[...] = jnp.where(tr < tc, 1.0, 0.0).astype(jnp.bfloat16)

    @pl.when(step > 0)
    def _():
        oh1, oh2 = expert_one_hot(ri_ref)
        oh = jnp.where(oh1 | oh2, 1.0, 0.0).astype(f32)
        blk_count = jnp.sum(oh, axis=1, keepdims=True)
        prefix = jnp.dot(oh.astype(jnp.bfloat16), upper_ref[...], preferred_element_type=f32)
        cnt_row = to_row(blk_count)
        lstart = jnp.sum(jnp.where(ec < er, cnt_row, 0.0), axis=1, keepdims=True)
        base = prefix + lstart
        d1 = jnp.sum(jnp.where(oh1, base, 0.0), axis=0, keepdims=True)
        d2 = jnp.sum(jnp.where(oh2, base, 0.0), axis=0, keepdims=True)
        sub = lax.broadcasted_iota(jnp.int32, (SUBLANES, tb), 0)
        ldest_ref[...] = jnp.where(sub == 0, d1, jnp.where(sub == 1, d2, 0.0)).astype(jnp.int32)
        gstart = start_ref[:, 0:1] + run_ref[:, 0:1]
        rsub = lax.broadcasted_iota(jnp.int32, (SUBLANES, LANES), 0)
        runs = jnp.where(rsub == 0, cnt_row,
                         jnp.where(rsub == 1, to_row(lstart), jnp.where(rsub == 2, to_row(gstart), 0.0)))
        runs_ref[...] = runs.astype(jnp.int32)
        run_ref[...] = run_ref[...] + blk_count


def _plan(ri):
    tb = MOE_TOKENS
    blk = lambda i: jnp.maximum(i - 1, 0)
    return pl.pallas_call(
        _plan_kernel,
        grid=(N_MOE_BLOCKS + 1,),
        in_specs=[pl.BlockSpec((SUBLANES, N_TOK), lambda i: (0, 0)),
                  pl.BlockSpec((SUBLANES, tb), lambda i: (0, blk(i)))],
        out_specs=[pl.BlockSpec((SUBLANES, tb), lambda i: (0, blk(i))),
                   pl.BlockSpec((SUBLANES, LANES), lambda i: (blk(i), 0)),
                   pl.BlockSpec((SUBLANES, META_LANES), lambda i: (0, 0))],
        out_shape=[jax.ShapeDtypeStruct((SUBLANES, N_TOK), jnp.int32),
                   jax.ShapeDtypeStruct((N_MOE_BLOCKS * SUBLANES, LANES), jnp.int32),
                   jax.ShapeDtypeStruct((SUBLANES, META_LANES), jnp.int32)],
        scratch_shapes=[pltpu.VMEM((N_EXPERTS, LANES), jnp.float32),
                        pltpu.VMEM((N_EXPERTS, LANES), jnp.float32),
                        pltpu.VMEM((tb, tb), jnp.bfloat16)],
        compiler_params=pltpu.CompilerParams(
            dimension_semantics=("arbitrary",), vmem_limit_bytes=VMEM_LIMIT),
        name="plan",
    )(ri, ri)


def _for_each_run_piece(runs_ref, row0, fn, live=True):
    for e in range(N_EXPERTS):
        n = jnp.where(live, runs_ref[row0, e], 0)
        lstart, gstart = runs_ref[row0 + 1, e], runs_ref[row0 + 2, e]
        for bit in range(RUN_BITS):
            @pl.when((n & (1 << bit)) != 0)
            def _(n=n, lstart=lstart, gstart=gstart, bit=bit):
                off = (n >> (bit + 1)) << (bit + 1)
                fn(lstart + off, gstart + off, 1 << bit)


def _tile_rows(ref, row, rows):
    return ref.at[pl.ds(pl.multiple_of(row * FEAT_TILES, FEAT_TILES), rows * FEAT_TILES)]


def _dispatch_kernel(meta_ref, runs_ref, ldest_ref, h1b_ref, buf_ref, *scratch):
    sorted_refs = scratch[:MOE_GROUP]
    zero_ref, sems, zsem = scratch[MOE_GROUP:]
    i = pl.program_id(0)
    n_steps = pl.num_programs(0)
    tb = MOE_TOKENS
    slot = i % 2
    block_tiles = 2 * tb * FEAT_TILES

    def wait_slot(g, s):
        pltpu.make_async_copy(sorted_refs[g].at[s], buf_ref.at[pl.ds(0, block_tiles)], sems.at[g, s]).wait()

    @pl.when(i == 0)
    def _():
        zero_ref[...] = jnp.zeros_like(zero_ref)

        def zero_copy(e):
            start = pl.multiple_of((meta_ref[1, e] - EXPERT_ROWS) * FEAT_TILES, EXPERT_ROWS * FEAT_TILES)
            return pltpu.make_async_copy(
                zero_ref, buf_ref.at[pl.ds(start, EXPERT_ROWS * FEAT_TILES)], zsem)

        def start_zero(e, c):
            @pl.when(meta_ref[2, e] > 0)
            def _():
                zero_copy(e).start()
            return c

        def wait_zero(e, c):
            @pl.when(meta_ref[2, e] > 0)
            def _():
                zero_copy(e).wait()
            return c

        def tail_copy(b):
            start = pl.multiple_of(b * (EXPERT_ROWS * FEAT_TILES), EXPERT_ROWS * FEAT_TILES)
            return pltpu.make_async_copy(
                zero_ref, buf_ref.at[pl.ds(start, EXPERT_ROWS * FEAT_TILES)], zsem)

        def start_tail(b, c):
            tail_copy(b).start()
            return c

        def wait_tail(b, c):
            tail_copy(b).wait()
            return c

        lax.fori_loop(0, N_EXPERTS, start_zero, 0)
        lax.fori_loop(meta_ref[3, 0], N_ROW_BLOCKS, start_tail, 0)
        lax.fori_loop(0, N_EXPERTS, wait_zero, 0)
        lax.fori_loop(meta_ref[3, 0], N_ROW_BLOCKS, wait_tail, 0)

    @pl.when(i >= 2)
    def _():
        for g in range(MOE_GROUP):
            wait_slot(g, slot)

    for g in range(MOE_GROUP):
        x = h1b_ref[g * tb:(g + 1) * tb, :]
        ld0 = ldest_ref[0:1, g * tb:(g + 1) * tb]
        ld1 = ldest_ref[1:2, g * tb:(g + 1) * tb]
        for c in range(2 * tb // SORT_CHUNK):
            r = lax.broadcasted_iota(jnp.int32, (SORT_CHUNK, tb), 0) + c * SORT_CHUNK
            perm = jnp.where((r == ld0) | (r == ld1), 1.0, 0.0).astype(jnp.bfloat16)
            rows = jnp.dot(perm, x, preferred_element_type=jnp.float32)
            _to_row_tiles(sorted_refs[g].at[slot, pl.ds(c * SORT_CHUNK * FEAT_TILES, SORT_CHUNK * FEAT_TILES)],
                          rows)

        def send(lrow, grow, rows, g=g):
            pltpu.make_async_copy(_tile_rows(sorted_refs[g].at[slot], lrow, rows),
                                  _tile_rows(buf_ref, grow, rows), sems.at[g, slot]).start()

        _for_each_run_piece(runs_ref, g * SUBLANES, send)

    @pl.when(i == n_steps - 1)
    def _():
        for g in range(MOE_GROUP):
            wait_slot(g, slot)
            wait_slot(g, 1 - slot)


def _dispatch(meta, runs, ldest, h1b):
    tb = MOE_GROUP * MOE_TOKENS
    return pl.pallas_call(
        _dispatch_kernel,
        grid_spec=pltpu.PrefetchScalarGridSpec(
            num_scalar_prefetch=1,
            grid=(N_MOE_BLOCKS // MOE_GROUP,),
            in_specs=[pl.BlockSpec((MOE_GROUP * SUBLANES, LANES), lambda i, m: (i, 0), memory_space=pltpu.SMEM),
                      pl.BlockSpec((SUBLANES, tb), lambda i, m: (0, i)),
                      pl.BlockSpec((tb, D_MODEL), lambda i, m: (i, 0))],
            out_specs=pl.BlockSpec(memory_space=pl.ANY),
            scratch_shapes=[pltpu.VMEM((2, 2 * MOE_TOKENS * FEAT_TILES, LANES), jnp.float32)] * MOE_GROUP + [
                pltpu.VMEM((EXPERT_ROWS * FEAT_TILES, LANES), jnp.float32),
                pltpu.SemaphoreType.DMA((MOE_GROUP, 2)), pltpu.SemaphoreType.DMA]),
        out_shape=jax.ShapeDtypeStruct((N_ROWS * FEAT_TILES, LANES), jnp.float32),
        compiler_params=pltpu.CompilerParams(
            dimension_semantics=("arbitrary",), vmem_limit_bytes=VMEM_LIMIT),
        name="dispatch",
    )(meta, runs, ldest, h1b)


def _sub_block_expert(meta, i, sub):
    return meta[0, jnp.minimum(i + sub * (N_ROW_BLOCKS // EXPERT_SUB), meta[3, 0] - 1)]


def _experts_kernel(meta_ref, x_hbm_ref, *refs):
    weights = [refs[3 * sub:3 * sub + 3] for sub in range(EXPERT_SUB)]
    o_ref, wgb_ref, wub_ref, wdb_ref, cached_ref, xring_ref, xsems = refs[3 * EXPERT_SUB:]
    n = EXPERT_ROWS
    i = pl.program_id(0)
    n_steps = pl.num_programs(0)

    def rows_copy(step):
        slot = step % EXPERT_RING
        start = pl.multiple_of(step * (n * FEAT_TILES), n * FEAT_TILES)
        return pltpu.make_async_copy(x_hbm_ref.at[:, pl.ds(start, n * FEAT_TILES), :], xring_ref.at[slot],
                                     xsems.at[slot])

    @pl.when(i == 0)
    def _():
        for step in range(EXPERT_RING - 1):
            rows_copy(step).start()

    @pl.when(i + (EXPERT_RING - 1) < n_steps)
    def _():
        rows_copy(i + (EXPERT_RING - 1)).start()

    rows_copy(i).wait()
    x_ref = xring_ref.at[i % EXPERT_RING]

    @pl.when(i == 0)
    def _():
        for sub in range(EXPERT_SUB):
            cached_ref[sub] = -1

    for sub in range(EXPERT_SUB):
        expert = _sub_block_expert(meta_ref, i, sub)

        @pl.when(cached_ref[sub] != expert)
        def _(sub=sub, expert=expert):
            wg_ref, wu_ref, wd_ref = weights[sub]
            wgb_ref[sub] = wg_ref[...].astype(jnp.bfloat16)
            wub_ref[sub] = wu_ref[...].astype(jnp.bfloat16)
            wdb_ref[sub] = wd_ref[...].astype(jnp.bfloat16)
            cached_ref[sub] = expert

    for sub in range(EXPERT_SUB):
        x = _from_row_tiles(x_ref.at[sub], n).astype(jnp.bfloat16)
        gate = jnp.dot(x, wgb_ref[sub], preferred_element_type=jnp.float32)
        up = jnp.dot(x, wub_ref[sub], preferred_element_type=jnp.float32)
        act = (gate * jax.nn.sigmoid(gate) * up).astype(jnp.bfloat16)
        _to_row_tiles(o_ref.at[sub], jnp.dot(act, wdb_ref[sub], preferred_element_type=jnp.float32))


def _experts(meta, buf, wg, wu, wd):
    sub_tiles = (N_ROW_BLOCKS // EXPERT_SUB) * EXPERT_ROWS * FEAT_TILES

    def weight_spec(shape, sub):
        return pl.BlockSpec((None,) + shape, lambda i, m: (_sub_block_expert(m, i, sub), 0, 0))

    rows_shape = (EXPERT_SUB, EXPERT_ROWS * FEAT_TILES, LANES)
    up_shape, down_shape = (D_MODEL, EXPERT_FF), (EXPERT_FF, D_MODEL)
    eout = pl.pallas_call(
        _experts_kernel,
        grid_spec=pltpu.PrefetchScalarGridSpec(
            num_scalar_prefetch=1,
            grid=(N_ROW_BLOCKS // EXPERT_SUB,),
            in_specs=[pl.BlockSpec(memory_space=pl.ANY)] + [
                weight_spec(shape, sub) for sub in range(EXPERT_SUB) for shape in (up_shape, up_shape, down_shape)],
            out_specs=pl.BlockSpec(rows_shape, lambda i, m: (0, i, 0)),
            scratch_shapes=[pltpu.VMEM((EXPERT_SUB,) + up_shape, jnp.bfloat16),
                            pltpu.VMEM((EXPERT_SUB,) + up_shape, jnp.bfloat16),
                            pltpu.VMEM((EXPERT_SUB,) + down_shape, jnp.bfloat16),
                            pltpu.SMEM((EXPERT_SUB,), jnp.int32),
                            pltpu.VMEM((EXPERT_RING,) + rows_shape, jnp.float32),
                            pltpu.SemaphoreType.DMA((EXPERT_RING,))]),
        out_shape=jax.ShapeDtypeStruct((EXPERT_SUB, sub_tiles, LANES), jnp.float32),
        compiler_params=pltpu.CompilerParams(
            dimension_semantics=("arbitrary",), vmem_limit_bytes=EXPERT_VMEM_LIMIT),
        name="experts",
    )(meta, buf.reshape(EXPERT_SUB, sub_tiles, LANES), *([wg, wu, wd] * EXPERT_SUB))
    return eout.reshape(N_ROWS * FEAT_TILES, LANES)


def _combine_kernel(runs_ref, runs_next_ref, ldest_ref, rf_ref, h1_ref, eout_ref, ln2g_ref, ln2b_ref,
                    o_ref, *scratch):
    y_refs = scratch[:MOE_GROUP]
    sems = scratch[MOE_GROUP]
    i = pl.program_id(0)
    n_steps = pl.num_programs(0)
    tb = MOE_TOKENS
    slot = i % 2
    block_tiles = 2 * tb * FEAT_TILES

    def fetch(table_ref, g, s, live):
        def recv(lrow, grow, rows):
            pltpu.make_async_copy(_tile_rows(eout_ref, grow, rows),
                                  _tile_rows(y_refs[g].at[s], lrow, rows), sems.at[g, s]).start()
        _for_each_run_piece(table_ref, g * SUBLANES, recv, live)

    @pl.when(i == 0)
    def _():
        for g in range(MOE_GROUP):
            fetch(runs_ref, g, slot, True)

    for g in range(MOE_GROUP):
        pltpu.make_async_copy(eout_ref.at[pl.ds(0, block_tiles)], y_refs[g].at[slot], sems.at[g, slot]).wait()
        fetch(runs_next_ref, g, 1 - slot, i + 1 < n_steps)

        ld = ldest_ref[:, g * tb:(g + 1) * tb].astype(jnp.float32).T
        gates = rf_ref[:, g * tb:(g + 1) * tb].T
        y = None
        col = lax.broadcasted_iota(jnp.int32, (tb, SORT_CHUNK), 1).astype(jnp.float32)
        for c in range(2 * tb // SORT_CHUNK):
            ld_c = ld - float(c * SORT_CHUNK)
            gm = jnp.where(col == ld_c[:, 0:1], gates[:, 0:1],
                           jnp.where(col == ld_c[:, 1:2], gates[:, 1:2], 0.0)).astype(jnp.bfloat16)
            rows = _from_row_tiles(
                y_refs[g].at[slot, pl.ds(c * SORT_CHUNK * FEAT_TILES, SORT_CHUNK * FEAT_TILES)], SORT_CHUNK)
            part = jnp.dot(gm, rows.astype(jnp.bfloat16), preferred_element_type=jnp.float32)
            y = part if y is None else y + part
        tok = slice(g * tb, (g + 1) * tb)
        o_ref[tok, :] = _layer_norm(ALPHA * h1_ref[tok, :] + y, ln2g_ref[...], ln2b_ref[...])


def _combine(runs, ldest, rf, h1, eout, ln2g, ln2b):
    tb = MOE_GROUP * MOE_TOKENS
    n_steps = N_MOE_BLOCKS // MOE_GROUP
    table = lambda index_map: pl.BlockSpec((MOE_GROUP * SUBLANES, LANES), index_map, memory_space=pltpu.SMEM)
    return pl.pallas_call(
        _combine_kernel,
        grid=(n_steps,),
        in_specs=[table(lambda i: (i, 0)), table(lambda i: (jnp.minimum(i + 1, n_steps - 1), 0)),
                  pl.BlockSpec((SUBLANES, tb), lambda i: (0, i)),
                  pl.BlockSpec((SUBLANES, tb), lambda i: (0, i)),
                  pl.BlockSpec((tb, D_MODEL), lambda i: (i, 0)),
                  pl.BlockSpec(memory_space=pl.ANY),
                  pl.BlockSpec((1, D_MODEL), lambda i: (0, 0)),
                  pl.BlockSpec((1, D_MODEL), lambda i: (0, 0))],
        out_specs=pl.BlockSpec((tb, D_MODEL), lambda i: (i, 0)),
        out_shape=jax.ShapeDtypeStruct((N_TOK, D_MODEL), jnp.float32),
        scratch_shapes=[pltpu.VMEM((2, 2 * MOE_TOKENS * FEAT_TILES, LANES), jnp.float32)] * MOE_GROUP + [
            pltpu.SemaphoreType.DMA((MOE_GROUP, 2))],
        compiler_params=pltpu.CompilerParams(
            dimension_semantics=("arbitrary",), vmem_limit_bytes=VMEM_LIMIT),
        name="combine",
    )(runs, runs, ldest, rf, h1, eout, ln2g, ln2b)


def _swap_halves(w):
    half = w.shape[-1] // 2
    return jnp.concatenate([w[..., half:], w[..., :half]], axis=-1)


def kernel(x, positions, ln0_g, ln0_b, w_in, gm_ln_g, gm_ln_b, w_spatial, b_spatial, q_norm_g, w_uq, kv_norm_g, w_ukv, w_out, ln1_g, ln1_b, w_router_group, b_router_group, w_router_expert, b_router_expert, w_gate, w_up, w_down, ln2_g, ln2_b):
    bf16 = jnp.bfloat16
    row = lambda a: a.reshape(1, -1)

    w_in0 = w_in[0]
    kr_cols = w_in0[:, O_KR:O_KR + QK_ROPE_DIM]
    win = jnp.concatenate([w_in0, _swap_halves(kr_cols)], axis=1).astype(bf16)
    wuq3 = w_uq[0].reshape(Q_LORA_RANK, MLA_HEADS, QK_NOPE_DIM + QK_ROPE_DIM)
    rope_cols = wuq3[:, :, QK_NOPE_DIM:]
    wuq = jnp.concatenate([wuq3, _swap_halves(rope_cols)], axis=-1).reshape(Q_LORA_RANK, D_MODEL).astype(bf16)
    wukv = w_ukv[0].astype(bf16)
    wout = w_out[0].astype(bf16)
    bs = jnp.broadcast_to(b_spatial[0][:, :, None], (GM_HEADS, GM_CHUNK, GM_HEAD_DIM))
    wr = jnp.concatenate([w_router_group[0], jnp.zeros((D_MODEL, SUBLANES - N_GROUPS), jnp.float32),
                          w_router_expert[0],
                          jnp.zeros((D_MODEL, LANES - ROUTER_ROWS), jnp.float32)],
                         axis=1).astype(bf16)
    br = jnp.concatenate([b_router_group[0], jnp.zeros((SUBLANES - N_GROUPS,), jnp.float32),
                          b_router_expert[0]]).reshape(ROUTER_ROWS, 1)

    inv_freq = ROPE_THETA ** (-jnp.arange(0, QK_ROPE_DIM, 2, dtype=jnp.float32) / QK_ROPE_DIM)
    freq = jnp.tile(inv_freq, 4).reshape(1, LANES)
    quarter = QK_ROPE_DIM // 2
    phase = jnp.concatenate([jnp.zeros((2 * quarter,), jnp.float32),
                             jnp.full((2 * quarter,), math.pi / 2, jnp.float32)]).reshape(1, LANES)
    sign = jnp.concatenate([jnp.ones((2 * quarter,), jnp.float32), -jnp.ones((quarter,), jnp.float32),
                            jnp.ones((quarter,), jnp.float32)]).reshape(1, LANES)

    x2 = x.reshape(N_TOK, D_MODEL)
    pos2 = positions.reshape(N_TOK // PREP_TOKENS, 1, PREP_TOKENS)

    h, outa, q, k, v = _prep(x2, pos2, row(ln0_g), row(ln0_b), win, row(gm_ln_g[0]), row(gm_ln_b[0]),
                             w_spatial[0], bs, row(q_norm_g[0]), wuq, row(kv_norm_g[0]), wukv,
                             freq, phase, sign)
    ob = _attn(q, k, v)
    h1, h1b, ri, rf = _proj(outa, ob, h, wout, row(ln1_g[0]), row(ln1_b[0]), wr, br)
    ldest, runs, meta = _plan(ri)
    buf = _dispatch(meta, runs, ldest, h1b)
    eout = _experts(meta, buf, w_gate[0], w_up[0], w_down[0])
    out = _combine(runs, ldest, rf, h1, eout, row(ln2_g[0]), row(ln2_b[0]))
    return out.reshape(BATCH, SEQ, D_MODEL)
```

```python
import functools
import math

import jax
import jax.numpy as jnp
from jax import lax
from jax.experimental import pallas as pl
from jax.experimental.pallas import tpu as pltpu

D_MODEL = 1024
BATCH = 16
SEQ = 2048
N_TOK = BATCH * SEQ
CHUNK = 64
GM_WIDTH = 512
GM_HEADS = 4
GM_HEAD_DIM = 128
GM_CHUNK = 128
MLA_HEADS = 4
QK_NOPE_DIM = 128
QK_ROPE_DIM = 64
V_HEAD_DIM = 128
Q_LORA_RANK = 384
KV_LORA_RANK = 256
ROPE_THETA = 10000.0
N_GROUPS = 4
EXPERTS_PER_GROUP = 8
N_EXPERTS = 32
TOP_K = 2
EXPERT_FF = 256
ALPHA = 2.0 ** 0.25
QK_SCALE = (QK_NOPE_DIM + QK_ROPE_DIM) ** -0.5 * math.log2(math.e)

LANES = 128
SUBLANES = 8
FEAT_TILES = D_MODEL // LANES
PREP_TOKENS = 1024
ATT_Q = 512
ATT_K = 512
N_QBLK = SEQ // ATT_Q
PROJ_TOKENS = 1024
MOE_TOKENS = 512
N_MOE_BLOCKS = N_TOK // MOE_TOKENS
MOE_GROUP = 2
RUN_BITS = (TOP_K * MOE_TOKENS).bit_length()
SORT_CHUNK = 256
EXPERT_ROWS = 256
EXPERT_SUB = 4
EXPERT_RING = 3
INPUT_RING = 3
N_ROWS = N_TOK * TOP_K + N_EXPERTS * EXPERT_ROWS
N_ROW_BLOCKS = N_ROWS // EXPERT_ROWS
META_LANES = 384
IN_COLS = 2 * GM_WIDTH + Q_LORA_RANK + KV_LORA_RANK + 2 * QK_ROPE_DIM
O_Q = 2 * GM_WIDTH
O_KV = O_Q + Q_LORA_RANK
O_KR = O_KV + KV_LORA_RANK
ROUTER_ROWS = 40
VMEM_LIMIT = 48 * 1024 * 1024
EXPERT_VMEM_LIMIT = 56 * 1024 * 1024

assert N_ROW_BLOCKS <= META_LANES and N_ROW_BLOCKS % EXPERT_SUB == 0


def _layer_norm(x, g, b, eps=1e-5):
    mu = jnp.mean(x, axis=-1, keepdims=True)
    xc = x - mu
    var = jnp.mean(xc * xc, axis=-1, keepdims=True)
    return xc * lax.rsqrt(var + eps) * g + b


def _rms_norm(x, g, eps=1e-6):
    return x * lax.rsqrt(jnp.mean(x * x, axis=-1, keepdims=True) + eps) * g


def _gelu_tanh(x):
    c = math.sqrt(2.0 / math.pi)
    return 0.5 * x * (1.0 + jnp.tanh(c * (x + 0.044715 * (x * x * x))))


def _to_row_tiles(ref, x):
    n = x.shape[0]
    for s in range(FEAT_TILES):
        ref[pl.ds(s, n, stride=FEAT_TILES), :] = x[:, s * LANES:(s + 1) * LANES]


def _from_row_tiles(ref, n):
    return jnp.concatenate(
        [ref[pl.ds(s, n, stride=FEAT_TILES), :] for s in range(FEAT_TILES)], axis=-1)


def _ring_block(hbm_ref, ring_ref, sems, block_of_step, rows):
    i = pl.program_id(0)
    n_steps = pl.num_programs(0)

    def copy(step):
        start = pl.multiple_of(block_of_step(step) * rows, rows)
        slot = step % INPUT_RING
        return pltpu.make_async_copy(hbm_ref.at[pl.ds(start, rows)], ring_ref.at[slot], sems.at[slot])

    @pl.when(i == 0)
    def _():
        for step in range(INPUT_RING - 1):
            copy(step).start()

    @pl.when(i + (INPUT_RING - 1) < n_steps)
    def _():
        copy(i + (INPUT_RING - 1)).start()

    copy(i).wait()
    return ring_ref.at[i % INPUT_RING]


def _ring_scratch(rows, cols, dtype):
    return [pltpu.VMEM((INPUT_RING, rows, cols), dtype), pltpu.SemaphoreType.DMA((INPUT_RING,))]


def _prep_kernel(x_ref, pos_ref, ln0g_ref, ln0b_ref, win_ref, gmg_ref, gmb_ref, ws_ref, bs_ref,
                 qg_ref, wuq_ref, kvg_ref, wukv_ref, freq_ref, phase_ref, sign_ref,
                 h_ref, outa_ref, q_ref, k_ref, v_ref, tabc_ref, tabs_ref, rot_ref):
    tb = x_ref.shape[0]

    @pl.when(pl.program_id(0) == 0)
    def _():
        d = lax.broadcasted_iota(jnp.int32, (tb, LANES), 0).astype(jnp.float32) * freq_ref[...]
        tabc_ref[...] = jnp.cos(d)
        tabs_ref[...] = jnp.sin(d)

    pos_row = pos_ref[0]
    p0 = pos_row[:, 0:1]
    offset = lax.broadcasted_iota(jnp.int32, (1, tb), 1)
    consecutive = jnp.max(jnp.abs((pos_row - p0 - offset).astype(jnp.float32))) == 0.0

    @pl.when(consecutive)
    def _():
        a0 = p0.astype(jnp.float32) * freq_ref[...]
        c0, s0 = jnp.cos(a0), jnp.sin(a0)
        lane = lax.broadcasted_iota(jnp.int32, (1, LANES), 1)
        coef_c = jnp.where(lane < 2 * 32, c0, jnp.where(lane < 3 * 32, -s0, s0))
        coef_s = jnp.where(lane < 2 * 32, -s0, jnp.where(lane < 3 * 32, -c0, c0))
        rot_ref[...] = coef_c * tabc_ref[...] + coef_s * tabs_ref[...]

    @pl.when(jnp.logical_not(consecutive))
    def _():
        pos_col = jnp.broadcast_to(pos_row.astype(jnp.float32), (SUBLANES, tb)).T[:, 0:1]
        ang = pos_col * freq_ref[...]
        rot_ref[...] = jnp.cos(ang - phase_ref[...]) * sign_ref[...]

    rot = rot_ref[...]

    h = _layer_norm(x_ref[...], ln0g_ref[...], ln0b_ref[...])
    h_ref[...] = h
    z = jnp.dot(h.astype(jnp.bfloat16), win_ref[...], preferred_element_type=jnp.float32)

    u = _gelu_tanh(z[:, :GM_WIDTH])
    v = _gelu_tanh(z[:, GM_WIDTH:2 * GM_WIDTH])
    row_chunk = lax.broadcasted_iota(jnp.int32, (GM_CHUNK, GM_CHUNK), 0) // CHUNK
    col_chunk = lax.broadcasted_iota(jnp.int32, (GM_CHUNK, GM_CHUNK), 1) // CHUNK
    allowed = col_chunk <= row_chunk
    for hd in range(GM_HEADS):
        lo, hi = hd * GM_HEAD_DIM, (hd + 1) * GM_HEAD_DIM
        vln = _layer_norm(v[:, lo:hi], gmg_ref[:, lo:hi], gmb_ref[:, lo:hi]).astype(jnp.bfloat16)
        wm = jnp.where(allowed, ws_ref[hd], 0.0).astype(jnp.bfloat16)
        for c in range(tb // GM_CHUNK):
            r0, r1 = c * GM_CHUNK, (c + 1) * GM_CHUNK
            f = jnp.dot(wm, vln[r0:r1], preferred_element_type=jnp.float32) + bs_ref[hd]
            outa_ref[r0:r1, lo:hi] = (u[r0:r1, lo:hi] * f).astype(jnp.bfloat16)

    ql = _rms_norm(z[:, O_Q:O_KV], qg_ref[...]).astype(jnp.bfloat16)
    qf = jnp.dot(ql, wuq_ref[...], preferred_element_type=jnp.float32)
    rot_s = rot * QK_SCALE
    q_parts = []
    for hd in range(MLA_HEADS):
        base = hd * 2 * LANES
        q_parts.append(qf[:, base:base + LANES] * QK_SCALE)
        q_parts.append(qf[:, base + LANES:base + 2 * LANES] * rot_s)
    q_ref[...] = jnp.concatenate(q_parts, axis=-1).astype(jnp.bfloat16)

    kvl = _rms_norm(z[:, O_KV:O_KR], kvg_ref[...]).astype(jnp.bfloat16)
    kv = jnp.dot(kvl, wukv_ref[...], preferred_element_type=jnp.float32)
    t = z[:, O_KR:O_KR + LANES] * rot
    krr = t + pltpu.roll(t, 2 * QK_ROPE_DIM // 2, axis=1)
    k_parts, v_parts = [], []
    for hd in range(MLA_HEADS):
        base = hd * 2 * LANES
        k_parts.append(kv[:, base:base + LANES])
        k_parts.append(krr)
        v_parts.append(kv[:, base + LANES:base + 2 * LANES])
    k_ref[...] = jnp.concatenate(k_parts, axis=-1).astype(jnp.bfloat16)
    v_ref[...] = jnp.concatenate(v_parts, axis=-1).astype(jnp.bfloat16)


def _prep(x2, pos2, ln0g, ln0b, win, gmg, gmb, ws, bs, qg, wuq, kvg, wukv, freq, phase, sign):
    tb = PREP_TOKENS
    full = lambda shape: pl.BlockSpec(shape, lambda i: (0,) * len(shape))
    tok = lambda cols: pl.BlockSpec((tb, cols), lambda i: (i, 0))
    return pl.pallas_call(
        _prep_kernel,
        grid=(N_TOK // tb,),
        in_specs=[tok(D_MODEL), pl.BlockSpec((1, 1, tb), lambda i: (i, 0, 0)), full((1, D_MODEL)), full((1, D_MODEL)),
                  full((D_MODEL, IN_COLS)), full((1, GM_WIDTH)), full((1, GM_WIDTH)),
                  full((GM_HEADS, GM_CHUNK, GM_CHUNK)), full((GM_HEADS, GM_CHUNK, GM_HEAD_DIM)),
                  full((1, Q_LORA_RANK)), full((Q_LORA_RANK, D_MODEL)),
                  full((1, KV_LORA_RANK)), full((KV_LORA_RANK, D_MODEL)),
                  full((1, LANES)), full((1, LANES)), full((1, LANES))],
        out_specs=[tok(D_MODEL), tok(GM_WIDTH), tok(D_MODEL), tok(D_MODEL), tok(GM_WIDTH)],
        out_shape=[jax.ShapeDtypeStruct((N_TOK, D_MODEL), jnp.float32),
                   jax.ShapeDtypeStruct((N_TOK, GM_WIDTH), jnp.bfloat16),
                   jax.ShapeDtypeStruct((N_TOK, D_MODEL), jnp.bfloat16),
                   jax.ShapeDtypeStruct((N_TOK, D_MODEL), jnp.bfloat16),
                   jax.ShapeDtypeStruct((N_TOK, GM_WIDTH), jnp.bfloat16)],
        scratch_shapes=[pltpu.VMEM((tb, LANES), jnp.float32)] * 3,
        compiler_params=pltpu.CompilerParams(
            dimension_semantics=("arbitrary",), vmem_limit_bytes=VMEM_LIMIT),
        name="prep",
    )(x2, pos2, ln0g, ln0b, win, gmg, gmb, ws, bs, qg, wuq, kvg, wukv, freq, phase, sign)


def _attn_kernel(q_ref, k_ref, v_ref, o_ref, s_ref, mx_ref, ls_ref, acc_ref):
    qi = pl.program_id(1)
    tq = q_ref.shape[0]
    nt = (((1,), (1,)), ((), ()))
    n_kv = SEQ // ATT_K

    half = tq // 2
    row_chunk = (lax.broadcasted_iota(jnp.int32, (half, half), 0)) // CHUNK
    col_chunk = (lax.broadcasted_iota(jnp.int32, (half, half), 1)) // CHUNK
    quad_allowed = col_chunk <= row_chunk
    diag_start = pl.multiple_of(qi * ATT_K, ATT_K)
    masked = jnp.float32(-1e30)

    def lane_tiles(x):
        return [x[:, t * LANES:(t + 1) * LANES] for t in range(x.shape[1] // LANES)]

    def lane_fold(x, op):
        return functools.reduce(op, lane_tiles(x))

    def scores(hd, j):
        start = pl.multiple_of(j * ATT_K, ATT_K)
        q_h = q_ref[:, hd * 2 * LANES:(hd + 1) * 2 * LANES]
        kb = k_ref[pl.ds(start, ATT_K), hd * 2 * LANES:(hd + 1) * 2 * LANES]
        return lax.dot_general(q_h, kb, nt, preferred_element_type=jnp.float32)

    for hd in range(MLA_HEADS):
        cols = slice(hd * 2 * LANES, (hd + 1) * 2 * LANES)
        k_lo = k_ref[pl.ds(diag_start, half), cols]
        k_hi = k_ref[pl.ds(diag_start + half, half), cols]
        qk = lambda q, k: lax.dot_general(q, k, nt, preferred_element_type=jnp.float32)
        s_tl = jnp.where(quad_allowed, qk(q_ref[:half, cols], k_lo), masked)
        s_bl = qk(q_ref[half:, cols], k_lo)
        s_br = jnp.where(quad_allowed, qk(q_ref[half:, cols], k_hi), masked)
        diag = s_ref.at[hd * n_kv + qi]
        diag[:half, :half] = s_tl
        diag[half:, :half] = s_bl
        diag[half:, half:] = s_br
        mx_ref[hd, :half] = lane_fold(s_tl, jnp.maximum)
        mx_ref[hd, half:] = jnp.maximum(lane_fold(s_bl, jnp.maximum), lane_fold(s_br, jnp.maximum))

    def pass_a(j, c):
        for hd in range(MLA_HEADS):
            s = scores(hd, j)
            s_ref[hd * n_kv + j] = s
            mx_ref[hd] = jnp.maximum(mx_ref[hd], lane_fold(s, jnp.maximum))
        return c

    lax.fori_loop(0, qi, pass_a, 0)

    for hd in range(MLA_HEADS):
        mx_ref[hd] = jnp.broadcast_to(jnp.max(mx_ref[hd], axis=-1, keepdims=True), (tq, LANES))

    def probs(hd, j):
        s = s_ref[hd * n_kv + j]
        mb = mx_ref[hd]
        p = jnp.exp2(jnp.concatenate([t - mb for t in lane_tiles(s)], axis=-1))
        start = pl.multiple_of(j * ATT_K, ATT_K)
        vb = v_ref[pl.ds(start, ATT_K), hd * LANES:(hd + 1) * LANES]
        pv = jnp.dot(p.astype(jnp.bfloat16), vb, preferred_element_type=jnp.float32)
        return lane_fold(p, jnp.add), pv

    for hd in range(MLA_HEADS):
        diag = s_ref.at[hd * n_kv + qi]
        vcols = slice(hd * LANES, (hd + 1) * LANES)
        p_top = jnp.exp2(jnp.concatenate([t - mx_ref[hd, :half] for t in lane_tiles(diag[:half, :half])], axis=-1))
        p_bot = jnp.exp2(jnp.concatenate([t - mx_ref[hd, half:] for t in lane_tiles(diag[half:, :])], axis=-1))
        ls_ref[hd, :half] = lane_fold(p_top, jnp.add)
        ls_ref[hd, half:] = lane_fold(p_bot, jnp.add)
        acc_ref[hd, :half] = jnp.dot(p_top.astype(jnp.bfloat16), v_ref[pl.ds(diag_start, half), vcols],
                                     preferred_element_type=jnp.float32)
        acc_ref[hd, half:] = jnp.dot(p_bot.astype(jnp.bfloat16), v_ref[pl.ds(diag_start, ATT_K), vcols],
                                     preferred_element_type=jnp.float32)

    def pass_b(j, c):
        for hd in range(MLA_HEADS):
            ls, pv = probs(hd, j)
            ls_ref[hd] = ls_ref[hd] + ls
            acc_ref[hd] = acc_ref[hd] + pv
        return c

    lax.fori_loop(0, qi, pass_b, 0)

    for hd in range(MLA_HEADS):
        l = jnp.sum(ls_ref[hd], axis=-1, keepdims=True)
        o_ref[:, hd * V_HEAD_DIM:(hd + 1) * V_HEAD_DIM] = (acc_ref[hd] / l).astype(jnp.bfloat16)


def _attn(q, k, v):
    tq = ATT_Q
    tokblk = lambda cols: pl.BlockSpec((tq, cols), lambda b, i: (b * N_QBLK + i, 0))
    seqblk = lambda cols: pl.BlockSpec((SEQ, cols), lambda b, i: (b, 0))
    return pl.pallas_call(
        _attn_kernel,
        grid=(BATCH, N_QBLK),
        in_specs=[tokblk(D_MODEL), seqblk(D_MODEL), seqblk(GM_WIDTH)],
        out_specs=tokblk(GM_WIDTH),
        out_shape=jax.ShapeDtypeStruct((N_TOK, MLA_HEADS * V_HEAD_DIM), jnp.bfloat16),
        scratch_shapes=[pltpu.VMEM((MLA_HEADS * (SEQ // ATT_K), tq, ATT_K), jnp.float32),
                        pltpu.VMEM((MLA_HEADS, tq, LANES), jnp.float32),
                        pltpu.VMEM((MLA_HEADS, tq, LANES), jnp.float32),
                        pltpu.VMEM((MLA_HEADS, tq, V_HEAD_DIM), jnp.float32)],
        compiler_params=pltpu.CompilerParams(
            dimension_semantics=("arbitrary", "arbitrary"), vmem_limit_bytes=VMEM_LIMIT),
        name="attn",
    )(q, k, v)


def _proj_kernel(outa_ref, ob_ref, h_hbm_ref, wout_ref, ln1g_ref, ln1b_ref, wr_ref, br_ref,
                 h1_ref, h1b_ref, ri_ref, rf_ref, proj_ref, hring_ref, hsems):
    i = pl.program_id(0)
    tb = outa_ref.shape[0]
    h_ref = _ring_block(h_hbm_ref, hring_ref, hsems, lambda step: jnp.maximum(step - 1, 0), tb)

    @pl.when(i == 0)
    def _():
        proj_ref[...] = jnp.zeros_like(proj_ref)

    h1 = _layer_norm(ALPHA * h_ref[...] + proj_ref[(i + 1) % 2], ln1g_ref[...], ln1b_ref[...])
    h1_ref[...] = h1
    h1b_ref[...] = h1.astype(jnp.bfloat16)

    logits_tm = jnp.dot(h1b_ref[...], wr_ref[...], preferred_element_type=jnp.float32)
    logits = logits_tm.T[0:ROUTER_ROWS] + br_ref[...]

    sub_i = lax.broadcasted_iota(jnp.int32, (SUBLANES, tb), 0)
    sub = sub_i.astype(jnp.float32)
    neg = jnp.float32(-jnp.inf)
    g = jnp.where(sub_i < N_GROUPS, logits[0:SUBLANES], neg)
    gmax = jnp.max(g, axis=0, keepdims=True)
    g_top = jnp.min(jnp.where(g == gmax, sub, float(SUBLANES)), axis=0, keepdims=True)
    p_group = 1.0 / jnp.sum(jnp.exp(g - gmax), axis=0, keepdims=True)
    sel = logits[SUBLANES:2 * SUBLANES]
    for grp in range(1, N_GROUPS):
        sel = jnp.where(g_top == float(grp), logits[(grp + 1) * SUBLANES:(grp + 2) * SUBLANES], sel)
    v1 = jnp.max(sel, axis=0, keepdims=True)
    i1 = jnp.min(jnp.where(sel == v1, sub, float(SUBLANES)), axis=0, keepdims=True)
    sel2 = jnp.where(sub == i1, neg, sel)
    v2 = jnp.max(sel2, axis=0, keepdims=True)
    i2 = jnp.min(jnp.where(sel2 == v2, sub, float(SUBLANES)), axis=0, keepdims=True)
    e21 = jnp.exp(v2 - v1)
    w1 = 1.0 / (1.0 + e21)
    gate1 = p_group * w1
    gate2 = p_group * (e21 * w1)
    e1 = g_top * EXPERTS_PER_GROUP + i1
    e2 = g_top * EXPERTS_PER_GROUP + i2
    ri_ref[...] = jnp.where(sub_i == 0, e1, jnp.where(sub_i == 1, e2, 0.0)).astype(jnp.int32)
    rf_ref[...] = jnp.where(sub_i == 0, gate1, jnp.where(sub_i == 1, gate2, 0.0))

    proj_ref[i % 2] = (jnp.dot(outa_ref[...], wout_ref[:GM_WIDTH, :], preferred_element_type=jnp.float32)
                       + jnp.dot(ob_ref[...], wout_ref[GM_WIDTH:, :], preferred_element_type=jnp.float32))


def _proj(outa, ob, h, wout, ln1g, ln1b, wr, br):
    tb = PROJ_TOKENS
    n_blk = N_TOK // tb
    cur = lambda i: jnp.minimum(i, n_blk - 1)
    prev = lambda i: jnp.maximum(i - 1, 0)
    full = lambda shape: pl.BlockSpec(shape, lambda i: (0,) * len(shape))
    return pl.pallas_call(
        _proj_kernel,
        grid=(n_blk + 1,),
        in_specs=[pl.BlockSpec((tb, GM_WIDTH), lambda i: (cur(i), 0)),
                  pl.BlockSpec((tb, GM_WIDTH), lambda i: (cur(i), 0)),
                  pl.BlockSpec(memory_space=pl.ANY),
                  full((D_MODEL, D_MODEL)), full((1, D_MODEL)), full((1, D_MODEL)),
                  full((D_MODEL, LANES)), full((ROUTER_ROWS, 1))],
        out_specs=[pl.BlockSpec((tb, D_MODEL), lambda i: (prev(i), 0)),
                   pl.BlockSpec((tb, D_MODEL), lambda i: (prev(i), 0)),
                   pl.BlockSpec((SUBLANES, tb), lambda i: (0, prev(i))),
                   pl.BlockSpec((SUBLANES, tb), lambda i: (0, prev(i)))],
        out_shape=[jax.ShapeDtypeStruct((N_TOK, D_MODEL), jnp.float32),
                   jax.ShapeDtypeStruct((N_TOK, D_MODEL), jnp.bfloat16),
                   jax.ShapeDtypeStruct((SUBLANES, N_TOK), jnp.int32),
                   jax.ShapeDtypeStruct((SUBLANES, N_TOK), jnp.float32)],
        scratch_shapes=[pltpu.VMEM((2, tb, D_MODEL), jnp.float32)] + _ring_scratch(tb, D_MODEL, jnp.float32),
        compiler_params=pltpu.CompilerParams(
            dimension_semantics=("arbitrary",), vmem_limit_bytes=VMEM_LIMIT),
        name="proj",
    )(outa, ob, h, wout, ln1g, ln1b, wr, br)


def _plan_kernel(ri_all_ref, ri_ref, ldest_ref, runs_ref, meta_ref, run_ref, start_ref, upper_ref):
    step = pl.program_id(0)
    tb = ri_ref.shape[1]
    f32 = jnp.float32
    er = lax.broadcasted_iota(jnp.int32, (N_EXPERTS, LANES), 0)
    ec = lax.broadcasted_iota(jnp.int32, (N_EXPERTS, LANES), 1)
    to_row = lambda col: jnp.sum(jnp.where(er == ec, col, 0.0), axis=0, keepdims=True)

    def expert_one_hot(ref):
        e_sub = lax.broadcasted_iota(jnp.int32, (N_EXPERTS, ref.shape[1]), 0)
        return e_sub == ref[0:1, :], e_sub == ref[1:2, :]

    @pl.when(step == 0)
    def _():
        oh1, oh2 = expert_one_hot(ri_all_ref)
        counts = jnp.sum(jnp.where(oh1 | oh2, 1.0, 0.0), axis=1, keepdims=True)
        padded = jnp.floor((counts + (EXPERT_ROWS - 1)) * (1.0 / EXPERT_ROWS)) * EXPERT_ROWS
        pad_end = jnp.sum(jnp.where(ec <= er, to_row(padded), 0.0), axis=1, keepdims=True)
        start_ref[...] = jnp.broadcast_to(pad_end - padded, start_ref.shape)
        run_ref[...] = jnp.zeros_like(run_ref)
        bstart = (lax.broadcasted_iota(jnp.int32, (N_EXPERTS, META_LANES), 1) * EXPERT_ROWS).astype(f32)
        blk_e = jnp.sum(jnp.where(pad_end <= bstart, 1.0, 0.0), axis=0, keepdims=True)
        blk_e = jnp.minimum(blk_e, N_EXPERTS - 1.0)
        n_used = pad_end[N_EXPERTS - 1:N_EXPERTS, :] * (1.0 / EXPERT_ROWS)
        pad3 = lambda r: jnp.concatenate(
            [r, jnp.zeros((1, META_LANES - LANES), f32)], axis=1)
        msub = lax.broadcasted_iota(jnp.int32, (SUBLANES, META_LANES), 0)
        meta = jnp.where(msub == 0, blk_e,
                         jnp.where(msub == 1, pad3(to_row(pad_end)),
                                   jnp.where(msub == 2, pad3(to_row(counts)),
                                             jnp.where(msub == 3, n_used, 0.0))))
        meta_ref[...] = meta.astype(jnp.int32)
        tr = lax.broadcasted_iota(jnp.int32, (tb, tb), 0)
        tc = lax.broadcasted_iota(jnp.int32, (tb, tb), 1)
        upper_ref[...] = jnp.where(tr < tc, 1.0, 0.0).astype(jnp.bfloat16)

    @pl.when(step > 0)
    def _():
        oh1, oh2 = expert_one_hot(ri_ref)
        oh = jnp.where(oh1 | oh2, 1.0, 0.0).astype(f32)
        blk_count = jnp.sum(oh, axis=1, keepdims=True)
        prefix = jnp.dot(oh.astype(jnp.bfloat16), upper_ref[...], preferred_element_type=f32)
        cnt_row = to_row(blk_count)
        lstart = jnp.sum(jnp.where(ec < er, cnt_row, 0.0), axis=1, keepdims=True)
        base = prefix + lstart
        d1 = jnp.sum(jnp.where(oh1, base, 0.0), axis=0, keepdims=True)
        d2 = jnp.sum(jnp.where(oh2, base, 0.0), axis=0, keepdims=True)
        sub = lax.broadcasted_iota(jnp.int32, (SUBLANES, tb), 0)
        ldest_ref[...] = jnp.where(sub == 0, d1, jnp.where(sub == 1, d2, 0.0)).astype(jnp.int32)
        gstart = start_ref[:, 0:1] + run_ref[:, 0:1]
        rsub = lax.broadcasted_iota(jnp.int32, (SUBLANES, LANES), 0)
        runs = jnp.where(rsub == 0, cnt_row,
                         jnp.where(rsub == 1, to_row(lstart * FEAT_TILES),
                                   jnp.where(rsub == 2, to_row(gstart * FEAT_TILES), 0.0)))
        runs_ref[...] = runs.astype(jnp.int32)
        run_ref[...] = run_ref[...] + blk_count


def _plan(ri):
    tb = MOE_TOKENS
    blk = lambda i: jnp.maximum(i - 1, 0)
    return pl.pallas_call(
        _plan_kernel,
        grid=(N_MOE_BLOCKS + 1,),
        in_specs=[pl.BlockSpec((SUBLANES, N_TOK), lambda i: (0, 0)),
                  pl.BlockSpec((SUBLANES, tb), lambda i: (0, blk(i)))],
        out_specs=[pl.BlockSpec((SUBLANES, tb), lambda i: (0, blk(i))),
                   pl.BlockSpec((SUBLANES, LANES), lambda i: (blk(i), 0)),
                   pl.BlockSpec((SUBLANES, META_LANES), lambda i: (0, 0))],
        out_shape=[jax.ShapeDtypeStruct((SUBLANES, N_TOK), jnp.int32),
                   jax.ShapeDtypeStruct((N_MOE_BLOCKS * SUBLANES, LANES), jnp.int32),
                   jax.ShapeDtypeStruct((SUBLANES, META_LANES), jnp.int32)],
        scratch_shapes=[pltpu.VMEM((N_EXPERTS, LANES), jnp.float32),
                        pltpu.VMEM((N_EXPERTS, LANES), jnp.float32),
                        pltpu.VMEM((tb, tb), jnp.bfloat16)],
        compiler_params=pltpu.CompilerParams(
            dimension_semantics=("arbitrary",), vmem_limit_bytes=VMEM_LIMIT),
        name="plan",
    )(ri, ri)


def _for_each_run_piece(runs_ref, row0, fn, live=True):
    for e in range(N_EXPERTS):
        n = jnp.where(live, runs_ref[row0, e], 0)
        lstart, gstart = runs_ref[row0 + 1, e], runs_ref[row0 + 2, e]
        for bit in range(RUN_BITS):
            @pl.when((n & (1 << bit)) != 0)
            def _(n=n, lstart=lstart, gstart=gstart, bit=bit):
                off = ((n >> (bit + 1)) << (bit + 1)) * FEAT_TILES
                fn(lstart + off, gstart + off, 1 << bit)


def _tile_rows(ref, tile_row, rows):
    return ref.at[pl.ds(pl.multiple_of(tile_row, FEAT_TILES), rows * FEAT_TILES)]


def _dispatch_kernel(meta_ref, runs_ref, ldest_ref, h1b_ref, buf_ref, *scratch):
    sorted_refs = scratch[:MOE_GROUP]
    zero_ref, sems, zsem = scratch[MOE_GROUP:]
    i = pl.program_id(0)
    n_steps = pl.num_programs(0)
    tb = MOE_TOKENS
    slot = i % 2
    block_tiles = 2 * tb * FEAT_TILES

    def wait_slot(g, s):
        pltpu.make_async_copy(sorted_refs[g].at[s], buf_ref.at[pl.ds(0, block_tiles)], sems.at[g, s]).wait()

    @pl.when(i == 0)
    def _():
        zero_ref[...] = jnp.zeros_like(zero_ref)

        def zero_copy(e):
            start = pl.multiple_of((meta_ref[1, e] - EXPERT_ROWS) * FEAT_TILES, EXPERT_ROWS * FEAT_TILES)
            return pltpu.make_async_copy(
                zero_ref, buf_ref.at[pl.ds(start, EXPERT_ROWS * FEAT_TILES)], zsem)

        def start_zero(e, c):
            @pl.when(meta_ref[2, e] > 0)
            def _():
                zero_copy(e).start()
            return c

        def wait_zero(e, c):
            @pl.when(meta_ref[2, e] > 0)
            def _():
                zero_copy(e).wait()
            return c

        def tail_copy(b):
            start = pl.multiple_of(b * (EXPERT_ROWS * FEAT_TILES), EXPERT_ROWS * FEAT_TILES)
            return pltpu.make_async_copy(
                zero_ref, buf_ref.at[pl.ds(start, EXPERT_ROWS * FEAT_TILES)], zsem)

        def start_tail(b, c):
            tail_copy(b).start()
            return c

        def wait_tail(b, c):
            tail_copy(b).wait()
            return c

        lax.fori_loop(0, N_EXPERTS, start_zero, 0)
        lax.fori_loop(meta_ref[3, 0], N_ROW_BLOCKS, start_tail, 0)
        lax.fori_loop(0, N_EXPERTS, wait_zero, 0)
        lax.fori_loop(meta_ref[3, 0], N_ROW_BLOCKS, wait_tail, 0)

    @pl.when(i >= 2)
    def _():
        for g in range(MOE_GROUP):
            wait_slot(g, slot)

    for g in range(MOE_GROUP):
        x = h1b_ref[g * tb:(g + 1) * tb, :]
        ld0 = ldest_ref[0:1, g * tb:(g + 1) * tb]
        ld1 = ldest_ref[1:2, g * tb:(g + 1) * tb]
        for c in range(2 * tb // SORT_CHUNK):
            r = lax.broadcasted_iota(jnp.int32, (SORT_CHUNK, tb), 0) + c * SORT_CHUNK
            perm = jnp.where((r == ld0) | (r == ld1), 1.0, 0.0).astype(jnp.bfloat16)
            rows = jnp.dot(perm, x, preferred_element_type=jnp.float32)
            _to_row_tiles(sorted_refs[g].at[slot, pl.ds(c * SORT_CHUNK * FEAT_TILES, SORT_CHUNK * FEAT_TILES)],
                          rows)

        def send(lrow, grow, rows, g=g):
            pltpu.make_async_copy(_tile_rows(sorted_refs[g].at[slot], lrow, rows),
                                  _tile_rows(buf_ref, grow, rows), sems.at[g, slot]).start()

        _for_each_run_piece(runs_ref, g * SUBLANES, send)

    @pl.when(i == n_steps - 1)
    def _():
        for g in range(MOE_GROUP):
            wait_slot(g, slot)
            wait_slot(g, 1 - slot)


def _dispatch(meta, runs, ldest, h1b):
    tb = MOE_GROUP * MOE_TOKENS
    return pl.pallas_call(
        _dispatch_kernel,
        grid_spec=pltpu.PrefetchScalarGridSpec(
            num_scalar_prefetch=1,
            grid=(N_MOE_BLOCKS // MOE_GROUP,),
            in_specs=[pl.BlockSpec((MOE_GROUP * SUBLANES, LANES), lambda i, m: (i, 0), memory_space=pltpu.SMEM),
                      pl.BlockSpec((SUBLANES, tb), lambda i, m: (0, i)),
                      pl.BlockSpec((tb, D_MODEL), lambda i, m: (i, 0))],
            out_specs=pl.BlockSpec(memory_space=pl.ANY),
            scratch_shapes=[pltpu.VMEM((2, 2 * MOE_TOKENS * FEAT_TILES, LANES), jnp.float32)] * MOE_GROUP + [
                pltpu.VMEM((EXPERT_ROWS * FEAT_TILES, LANES), jnp.float32),
                pltpu.SemaphoreType.DMA((MOE_GROUP, 2)), pltpu.SemaphoreType.DMA]),
        out_shape=jax.ShapeDtypeStruct((N_ROWS * FEAT_TILES, LANES), jnp.float32),
        compiler_params=pltpu.CompilerParams(
            dimension_semantics=("arbitrary",), vmem_limit_bytes=VMEM_LIMIT),
        name="dispatch",
    )(meta, runs, ldest, h1b)


def _sub_block_expert(meta, i, sub):
    return meta[0, jnp.minimum(i + sub * (N_ROW_BLOCKS // EXPERT_SUB), meta[3, 0] - 1)]


def _experts_kernel(meta_ref, x_hbm_ref, *refs):
    weights = [refs[3 * sub:3 * sub + 3] for sub in range(EXPERT_SUB)]
    o_ref, wgb_ref, wub_ref, wdb_ref, cached_ref, xring_ref, xsems = refs[3 * EXPERT_SUB:]
    n = EXPERT_ROWS
    i = pl.program_id(0)
    n_steps = pl.num_programs(0)

    def rows_copy(step):
        slot = step % EXPERT_RING
        start = pl.multiple_of(step * (n * FEAT_TILES), n * FEAT_TILES)
        return pltpu.make_async_copy(x_hbm_ref.at[:, pl.ds(start, n * FEAT_TILES), :], xring_ref.at[slot],
                                     xsems.at[slot])

    @pl.when(i == 0)
    def _():
        for step in range(EXPERT_RING - 1):
            rows_copy(step).start()

    @pl.when(i + (EXPERT_RING - 1) < n_steps)
    def _():
        rows_copy(i + (EXPERT_RING - 1)).start()

    rows_copy(i).wait()
    x_ref = xring_ref.at[i % EXPERT_RING]

    @pl.when(i == 0)
    def _():
        for sub in range(EXPERT_SUB):
            cached_ref[sub] = -1

    for sub in range(EXPERT_SUB):
        expert = _sub_block_expert(meta_ref, i, sub)

        @pl.when(cached_ref[sub] != expert)
        def _(sub=sub, expert=expert):
            wg_ref, wu_ref, wd_ref = weights[sub]
            wgb_ref[sub] = wg_ref[...].astype(jnp.bfloat16)
            wub_ref[sub] = wu_ref[...].astype(jnp.bfloat16)
            wdb_ref[sub] = wd_ref[...].astype(jnp.bfloat16)
            cached_ref[sub] = expert

    for sub in range(EXPERT_SUB):
        x = _from_row_tiles(x_ref.at[sub], n).astype(jnp.bfloat16)
        gate = jnp.dot(x, wgb_ref[sub], preferred_element_type=jnp.float32)
        up = jnp.dot(x, wub_ref[sub], preferred_element_type=jnp.float32)
        act = (gate * jax.nn.sigmoid(gate) * up).astype(jnp.bfloat16)
        _to_row_tiles(o_ref.at[sub], jnp.dot(act, wdb_ref[sub], preferred_element_type=jnp.float32))


def _experts(meta, buf, wg, wu, wd):
    sub_tiles = (N_ROW_BLOCKS // EXPERT_SUB) * EXPERT_ROWS * FEAT_TILES

    def weight_spec(shape, sub):
        return pl.BlockSpec((None,) + shape, lambda i, m: (_sub_block_expert(m, i, sub), 0, 0))

    rows_shape = (EXPERT_SUB, EXPERT_ROWS * FEAT_TILES, LANES)
    up_shape, down_shape = (D_MODEL, EXPERT_FF), (EXPERT_FF, D_MODEL)
    eout = pl.pallas_call(
        _experts_kernel,
        grid_spec=pltpu.PrefetchScalarGridSpec(
            num_scalar_prefetch=1,
            grid=(N_ROW_BLOCKS // EXPERT_SUB,),
            in_specs=[pl.BlockSpec(memory_space=pl.ANY)] + [
                weight_spec(shape, sub) for sub in range(EXPERT_SUB) for shape in (up_shape, up_shape, down_shape)],
            out_specs=pl.BlockSpec(rows_shape, lambda i, m: (0, i, 0)),
            scratch_shapes=[pltpu.VMEM((EXPERT_SUB,) + up_shape, jnp.bfloat16),
                            pltpu.VMEM((EXPERT_SUB,) + up_shape, jnp.bfloat16),
                            pltpu.VMEM((EXPERT_SUB,) + down_shape, jnp.bfloat16),
                            pltpu.SMEM((EXPERT_SUB,), jnp.int32),
                            pltpu.VMEM((EXPERT_RING,) + rows_shape, jnp.float32),
                            pltpu.SemaphoreType.DMA((EXPERT_RING,))]),
        out_shape=jax.ShapeDtypeStruct((EXPERT_SUB, sub_tiles, LANES), jnp.float32),
        compiler_params=pltpu.CompilerParams(
            dimension_semantics=("arbitrary",), vmem_limit_bytes=EXPERT_VMEM_LIMIT),
        name="experts",
    )(meta, buf.reshape(EXPERT_SUB, sub_tiles, LANES), *([wg, wu, wd] * EXPERT_SUB))
    return eout.reshape(N_ROWS * FEAT_TILES, LANES)


def _combine_kernel(runs_ref, runs_next_ref, ldest_ref, rf_ref, h1_ref, eout_ref, ln2g_ref, ln2b_ref,
                    o_ref, *scratch):
    y_refs = scratch[:MOE_GROUP]
    sems = scratch[MOE_GROUP]
    i = pl.program_id(0)
    n_steps = pl.num_programs(0)
    tb = MOE_TOKENS
    slot = i % 2
    block_tiles = 2 * tb * FEAT_TILES

    def fetch(table_ref, g, s, live):
        def recv(lrow, grow, rows):
            pltpu.make_async_copy(_tile_rows(eout_ref, grow, rows),
                                  _tile_rows(y_refs[g].at[s], lrow, rows), sems.at[g, s]).start()
        _for_each_run_piece(table_ref, g * SUBLANES, recv, live)

    @pl.when(i == 0)
    def _():
        for g in range(MOE_GROUP):
            fetch(runs_ref, g, slot, True)

    for g in range(MOE_GROUP):
        pltpu.make_async_copy(eout_ref.at[pl.ds(0, block_tiles)], y_refs[g].at[slot], sems.at[g, slot]).wait()
        fetch(runs_next_ref, g, 1 - slot, i + 1 < n_steps)

        ld = ldest_ref[:, g * tb:(g + 1) * tb].astype(jnp.float32).T
        gates = rf_ref[:, g * tb:(g + 1) * tb].T
        y = None
        col = lax.broadcasted_iota(jnp.int32, (tb, SORT_CHUNK), 1).astype(jnp.float32)
        for c in range(2 * tb // SORT_CHUNK):
            ld_c = ld - float(c * SORT_CHUNK)
            gm = jnp.where(col == ld_c[:, 0:1], gates[:, 0:1],
                           jnp.where(col == ld_c[:, 1:2], gates[:, 1:2], 0.0)).astype(jnp.bfloat16)
            rows = _from_row_tiles(
                y_refs[g].at[slot, pl.ds(c * SORT_CHUNK * FEAT_TILES, SORT_CHUNK * FEAT_TILES)], SORT_CHUNK)
            part = jnp.dot(gm, rows.astype(jnp.bfloat16), preferred_element_type=jnp.float32)
            y = part if y is None else y + part
        tok = slice(g * tb, (g + 1) * tb)
        o_ref[tok, :] = _layer_norm(ALPHA * h1_ref[tok, :] + y, ln2g_ref[...], ln2b_ref[...])


def _combine(runs, ldest, rf, h1, eout, ln2g, ln2b):
    tb = MOE_GROUP * MOE_TOKENS
    n_steps = N_MOE_BLOCKS // MOE_GROUP
    table = lambda index_map: pl.BlockSpec((MOE_GROUP * SUBLANES, LANES), index_map, memory_space=pltpu.SMEM)
    return pl.pallas_call(
        _combine_kernel,
        grid=(n_steps,),
        in_specs=[table(lambda i: (i, 0)), table(lambda i: (jnp.minimum(i + 1, n_steps - 1), 0)),
                  pl.BlockSpec((SUBLANES, tb), lambda i: (0, i)),
                  pl.BlockSpec((SUBLANES, tb), lambda i: (0, i)),
                  pl.BlockSpec((tb, D_MODEL), lambda i: (i, 0)),
                  pl.BlockSpec(memory_space=pl.ANY),
                  pl.BlockSpec((1, D_MODEL), lambda i: (0, 0)),
                  pl.BlockSpec((1, D_MODEL), lambda i: (0, 0))],
        out_specs=pl.BlockSpec((tb, D_MODEL), lambda i: (i, 0)),
        out_shape=jax.ShapeDtypeStruct((N_TOK, D_MODEL), jnp.float32),
        scratch_shapes=[pltpu.VMEM((2, 2 * MOE_TOKENS * FEAT_TILES, LANES), jnp.float32)] * MOE_GROUP + [
            pltpu.SemaphoreType.DMA((MOE_GROUP, 2))],
        compiler_params=pltpu.CompilerParams(
            dimension_semantics=("arbitrary",), vmem_limit_bytes=VMEM_LIMIT),
        name="combine",
    )(runs, runs, ldest, rf, h1, eout, ln2g, ln2b)


def _swap_halves(w):
    half = w.shape[-1] // 2
    return jnp.concatenate([w[..., half:], w[..., :half]], axis=-1)


def kernel(x, positions, ln0_g, ln0_b, w_in, gm_ln_g, gm_ln_b, w_spatial, b_spatial, q_norm_g, w_uq, kv_norm_g, w_ukv, w_out, ln1_g, ln1_b, w_router_group, b_router_group, w_router_expert, b_router_expert, w_gate, w_up, w_down, ln2_g, ln2_b):
    bf16 = jnp.bfloat16
    row = lambda a: a.reshape(1, -1)

    w_in0 = w_in[0]
    kr_cols = w_in0[:, O_KR:O_KR + QK_ROPE_DIM]
    win = jnp.concatenate([w_in0, _swap_halves(kr_cols)], axis=1).astype(bf16)
    wuq3 = w_uq[0].reshape(Q_LORA_RANK, MLA_HEADS, QK_NOPE_DIM + QK_ROPE_DIM)
    rope_cols = wuq3[:, :, QK_NOPE_DIM:]
    wuq = jnp.concatenate([wuq3, _swap_halves(rope_cols)], axis=-1).reshape(Q_LORA_RANK, D_MODEL).astype(bf16)
    wukv = w_ukv[0].astype(bf16)
    wout = w_out[0].astype(bf16)
    bs = jnp.broadcast_to(b_spatial[0][:, :, None], (GM_HEADS, GM_CHUNK, GM_HEAD_DIM))
    wr = jnp.concatenate([w_router_group[0], jnp.zeros((D_MODEL, SUBLANES - N_GROUPS), jnp.float32),
                          w_router_expert[0],
                          jnp.zeros((D_MODEL, LANES - ROUTER_ROWS), jnp.float32)],
                         axis=1).astype(bf16)
    br = jnp.concatenate([b_router_group[0], jnp.zeros((SUBLANES - N_GROUPS,), jnp.float32),
                          b_router_expert[0]]).reshape(ROUTER_ROWS, 1)

    inv_freq = ROPE_THETA ** (-jnp.arange(0, QK_ROPE_DIM, 2, dtype=jnp.float32) / QK_ROPE_DIM)
    freq = jnp.tile(inv_freq, 4).reshape(1, LANES)
    quarter = QK_ROPE_DIM // 2
    phase = jnp.concatenate([jnp.zeros((2 * quarter,), jnp.float32),
                             jnp.full((2 * quarter,), math.pi / 2, jnp.float32)]).reshape(1, LANES)
    sign = jnp.concatenate([jnp.ones((2 * quarter,), jnp.float32), -jnp.ones((quarter,), jnp.float32),
                            jnp.ones((quarter,), jnp.float32)]).reshape(1, LANES)

    x2 = x.reshape(N_TOK, D_MODEL)
    pos2 = positions.reshape(N_TOK // PREP_TOKENS, 1, PREP_TOKENS)

    h, outa, q, k, v = _prep(x2, pos2, row(ln0_g), row(ln0_b), win, row(gm_ln_g[0]), row(gm_ln_b[0]),
                             w_spatial[0], bs, row(q_norm_g[0]), wuq, row(kv_norm_g[0]), wukv,
                             freq, phase, sign)
    ob = _attn(q, k, v)
    h1, h1b, ri, rf = _proj(outa, ob, h, wout, row(ln1_g[0]), row(ln1_b[0]), wr, br)
    ldest, runs, meta = _plan(ri)
    buf = _dispatch(meta, runs, ldest, h1b)
    eout = _experts(meta, buf, w_gate[0], w_up[0], w_down[0])
    out = _combine(runs, ldest, rf, h1, eout, row(ln2_g[0]), row(ln2_b[0]))
    return out.reshape(BATCH, SEQ, D_MODEL)
```

```python
import functools
import math

import jax
import jax.numpy as jnp
from jax import lax
from jax.experimental import pallas as pl
from jax.experimental.pallas import tpu as pltpu

D_MODEL = 1024
BATCH = 16
SEQ = 2048
N_TOK = BATCH * SEQ
CHUNK = 64
GM_WIDTH = 512
GM_HEADS = 4
GM_HEAD_DIM = 128
GM_CHUNK = 128
MLA_HEADS = 4
QK_NOPE_DIM = 128
QK_ROPE_DIM = 64
V_HEAD_DIM = 128
Q_LORA_RANK = 384
KV_LORA_RANK = 256
ROPE_THETA = 10000.0
N_GROUPS = 4
EXPERTS_PER_GROUP = 8
N_EXPERTS = 32
TOP_K = 2
EXPERT_FF = 256
ALPHA = 2.0 ** 0.25
QK_SCALE = (QK_NOPE_DIM + QK_ROPE_DIM) ** -0.5 * math.log2(math.e)

LANES = 128
SUBLANES = 8
FEAT_TILES = D_MODEL // LANES
PREP_TOKENS = 1024
ATT_Q = 512
ATT_K = 512
N_QBLK = SEQ // ATT_Q
PROJ_TOKENS = 1024
MOE_TOKENS = 512
N_MOE_BLOCKS = N_TOK // MOE_TOKENS
MOE_GROUP = 2
RUN_BITS = (TOP_K * MOE_TOKENS).bit_length()
SORT_CHUNK = 256
EXPERT_ROWS = 256
EXPERT_SUB = 4
EXPERT_RING = 3
INPUT_RING = 3
N_ROWS = N_TOK * TOP_K + N_EXPERTS * EXPERT_ROWS
N_ROW_BLOCKS = N_ROWS // EXPERT_ROWS
META_LANES = 384
IN_COLS = 2 * GM_WIDTH + Q_LORA_RANK + KV_LORA_RANK + 2 * QK_ROPE_DIM
O_Q = 2 * GM_WIDTH
O_KV = O_Q + Q_LORA_RANK
O_KR = O_KV + KV_LORA_RANK
ROUTER_ROWS = 40
VMEM_LIMIT = 48 * 1024 * 1024
EXPERT_VMEM_LIMIT = 56 * 1024 * 1024

assert N_ROW_BLOCKS <= META_LANES and N_ROW_BLOCKS % EXPERT_SUB == 0


def _layer_norm(x, g, b, eps=1e-5):
    mu = jnp.mean(x, axis=-1, keepdims=True)
    xc = x - mu
    var = jnp.mean(xc * xc, axis=-1, keepdims=True)
    return xc * lax.rsqrt(var + eps) * g + b


def _rms_norm(x, g, eps=1e-6):
    return x * lax.rsqrt(jnp.mean(x * x, axis=-1, keepdims=True) + eps) * g


def _gelu_tanh(x):
    c = math.sqrt(2.0 / math.pi)
    return 0.5 * x * (1.0 + jnp.tanh(c * (x + 0.044715 * (x * x * x))))


def _to_row_tiles(ref, x):
    n = x.shape[0]
    for s in range(FEAT_TILES):
        ref[pl.ds(s, n, stride=FEAT_TILES), :] = x[:, s * LANES:(s + 1) * LANES]


def _from_row_tiles(ref, n):
    return jnp.concatenate(
        [ref[pl.ds(s, n, stride=FEAT_TILES), :] for s in range(FEAT_TILES)], axis=-1)


def _ring_block(hbm_ref, ring_ref, sems, block_of_step, rows):
    i = pl.program_id(0)
    n_steps = pl.num_programs(0)

    def copy(step):
        start = pl.multiple_of(block_of_step(step) * rows, rows)
        slot = step % INPUT_RING
        return pltpu.make_async_copy(hbm_ref.at[pl.ds(start, rows)], ring_ref.at[slot], sems.at[slot])

    @pl.when(i == 0)
    def _():
        for step in range(INPUT_RING - 1):
            copy(step).start()

    @pl.when(i + (INPUT_RING - 1) < n_steps)
    def _():
        copy(i + (INPUT_RING - 1)).start()

    copy(i).wait()
    return ring_ref.at[i % INPUT_RING]


def _ring_scratch(rows, cols, dtype):
    return [pltpu.VMEM((INPUT_RING, rows, cols), dtype), pltpu.SemaphoreType.DMA((INPUT_RING,))]


def _prep_kernel(x_ref, pos_ref, ln0g_ref, ln0b_ref, win_ref, gmg_ref, gmb_ref, ws_ref, bs_ref,
                 qg_ref, wuq_ref, kvg_ref, wukv_ref, freq_ref, phase_ref, sign_ref,
                 h_ref, outa_ref, q_ref, k_ref, v_ref, tabc_ref, tabs_ref, rot_ref):
    tb = x_ref.shape[0]

    @pl.when(pl.program_id(0) == 0)
    def _():
        d = lax.broadcasted_iota(jnp.int32, (tb, LANES), 0).astype(jnp.float32) * freq_ref[...]
        tabc_ref[...] = jnp.cos(d)
        tabs_ref[...] = jnp.sin(d)

    pos_row = pos_ref[0]
    p0 = pos_row[:, 0:1]
    offset = lax.broadcasted_iota(jnp.int32, (1, tb), 1)
    consecutive = jnp.max(jnp.abs((pos_row - p0 - offset).astype(jnp.float32))) == 0.0

    @pl.when(consecutive)
    def _():
        a0 = p0.astype(jnp.float32) * freq_ref[...]
        c0, s0 = jnp.cos(a0), jnp.sin(a0)
        lane = lax.broadcasted_iota(jnp.int32, (1, LANES), 1)
        coef_c = jnp.where(lane < 2 * 32, c0, jnp.where(lane < 3 * 32, -s0, s0))
        coef_s = jnp.where(lane < 2 * 32, -s0, jnp.where(lane < 3 * 32, -c0, c0))
        rot_ref[...] = coef_c * tabc_ref[...] + coef_s * tabs_ref[...]

    @pl.when(jnp.logical_not(consecutive))
    def _():
        pos_col = jnp.broadcast_to(pos_row.astype(jnp.float32), (SUBLANES, tb)).T[:, 0:1]
        ang = pos_col * freq_ref[...]
        rot_ref[...] = jnp.cos(ang - phase_ref[...]) * sign_ref[...]

    rot = rot_ref[...]

    h = _layer_norm(x_ref[...], ln0g_ref[...], ln0b_ref[...])
    h_ref[...] = h
    z = jnp.dot(h.astype(jnp.bfloat16), win_ref[...], preferred_element_type=jnp.float32)

    u = _gelu_tanh(z[:, :GM_WIDTH])
    v = _gelu_tanh(z[:, GM_WIDTH:2 * GM_WIDTH])
    row_chunk = lax.broadcasted_iota(jnp.int32, (GM_CHUNK, GM_CHUNK), 0) // CHUNK
    col_chunk = lax.broadcasted_iota(jnp.int32, (GM_CHUNK, GM_CHUNK), 1) // CHUNK
    allowed = col_chunk <= row_chunk
    for hd in range(GM_HEADS):
        lo, hi = hd * GM_HEAD_DIM, (hd + 1) * GM_HEAD_DIM
        vln = _layer_norm(v[:, lo:hi], gmg_ref[:, lo:hi], gmb_ref[:, lo:hi]).astype(jnp.bfloat16)
        wm = jnp.where(allowed, ws_ref[hd], 0.0).astype(jnp.bfloat16)
        for c in range(tb // GM_CHUNK):
            r0, r1 = c * GM_CHUNK, (c + 1) * GM_CHUNK
            f = jnp.dot(wm, vln[r0:r1], preferred_element_type=jnp.float32) + bs_ref[hd]
            outa_ref[r0:r1, lo:hi] = (u[r0:r1, lo:hi] * f).astype(jnp.bfloat16)

    ql = _rms_norm(z[:, O_Q:O_KV], qg_ref[...]).astype(jnp.bfloat16)
    qf = jnp.dot(ql, wuq_ref[...], preferred_element_type=jnp.float32)
    rot_s = rot * QK_SCALE
    q_parts = []
    for hd in range(MLA_HEADS):
        base = hd * 2 * LANES
        q_parts.append(qf[:, base:base + LANES] * QK_SCALE)
        q_parts.append(qf[:, base + LANES:base + 2 * LANES] * rot_s)
    q_ref[...] = jnp.concatenate(q_parts, axis=-1).astype(jnp.bfloat16)

    kvl = _rms_norm(z[:, O_KV:O_KR], kvg_ref[...]).astype(jnp.bfloat16)
    kv = jnp.dot(kvl, wukv_ref[...], preferred_element_type=jnp.float32)
    t = z[:, O_KR:O_KR + LANES] * rot
    krr = t + pltpu.roll(t, 2 * QK_ROPE_DIM // 2, axis=1)
    k_parts, v_parts = [], []
    for hd in range(MLA_HEADS):
        base = hd * 2 * LANES
        k_parts.append(kv[:, base:base + LANES])
        k_parts.append(krr)
        v_parts.append(kv[:, base + LANES:base + 2 * LANES])
    k_ref[...] = jnp.concatenate(k_parts, axis=-1).astype(jnp.bfloat16)
    v_ref[...] = jnp.concatenate(v_parts, axis=-1).astype(jnp.bfloat16)


def _prep(x2, pos2, ln0g, ln0b, win, gmg, gmb, ws, bs, qg, wuq, kvg, wukv, freq, phase, sign):
    tb = PREP_TOKENS
    full = lambda shape: pl.BlockSpec(shape, lambda i: (0,) * len(shape))
    tok = lambda cols: pl.BlockSpec((tb, cols), lambda i: (i, 0))
    return pl.pallas_call(
        _prep_kernel,
        grid=(N_TOK // tb,),
        in_specs=[tok(D_MODEL), pl.BlockSpec((1, 1, tb), lambda i: (i, 0, 0)), full((1, D_MODEL)), full((1, D_MODEL)),
                  full((D_MODEL, IN_COLS)), full((1, GM_WIDTH)), full((1, GM_WIDTH)),
                  full((GM_HEADS, GM_CHUNK, GM_CHUNK)), full((GM_HEADS, GM_CHUNK, GM_HEAD_DIM)),
                  full((1, Q_LORA_RANK)), full((Q_LORA_RANK, D_MODEL)),
                  full((1, KV_LORA_RANK)), full((KV_LORA_RANK, D_MODEL)),
                  full((1, LANES)), full((1, LANES)), full((1, LANES))],
        out_specs=[tok(D_MODEL), tok(GM_WIDTH), tok(D_MODEL), tok(D_MODEL), tok(GM_WIDTH)],
        out_shape=[jax.ShapeDtypeStruct((N_TOK, D_MODEL), jnp.float32),
                   jax.ShapeDtypeStruct((N_TOK, GM_WIDTH), jnp.bfloat16),
                   jax.ShapeDtypeStruct((N_TOK, D_MODEL), jnp.bfloat16),
                   jax.ShapeDtypeStruct((N_TOK, D_MODEL), jnp.bfloat16),
                   jax.ShapeDtypeStruct((N_TOK, GM_WIDTH), jnp.bfloat16)],
        scratch_shapes=[pltpu.VMEM((tb, LANES), jnp.float32)] * 3,
        compiler_params=pltpu.CompilerParams(
            dimension_semantics=("arbitrary",), vmem_limit_bytes=VMEM_LIMIT),
        name="prep",
    )(x2, pos2, ln0g, ln0b, win, gmg, gmb, ws, bs, qg, wuq, kvg, wukv, freq, phase, sign)


def _attn_kernel(q_ref, k_ref, v_ref, o_ref, s_ref, mx_ref, ls_ref, acc_ref):
    qi = pl.program_id(1)
    tq = q_ref.shape[0]
    nt = (((1,), (1,)), ((), ()))
    n_kv = SEQ // ATT_K

    half = tq // 2
    row_chunk = (lax.broadcasted_iota(jnp.int32, (half, half), 0)) // CHUNK
    col_chunk = (lax.broadcasted_iota(jnp.int32, (half, half), 1)) // CHUNK
    quad_allowed = col_chunk <= row_chunk
    diag_start = pl.multiple_of(qi * ATT_K, ATT_K)
    masked = jnp.float32(-1e30)

    def lane_tiles(x):
        return [x[:, t * LANES:(t + 1) * LANES] for t in range(x.shape[1] // LANES)]

    def lane_fold(x, op):
        return functools.reduce(op, lane_tiles(x))

    def scores(hd, j):
        start = pl.multiple_of(j * ATT_K, ATT_K)
        q_h = q_ref[:, hd * 2 * LANES:(hd + 1) * 2 * LANES]
        kb = k_ref[pl.ds(start, ATT_K), hd * 2 * LANES:(hd + 1) * 2 * LANES]
        return lax.dot_general(q_h, kb, nt, preferred_element_type=jnp.float32)

    for hd in range(MLA_HEADS):
        cols = slice(hd * 2 * LANES, (hd + 1) * 2 * LANES)
        k_lo = k_ref[pl.ds(diag_start, half), cols]
        k_hi = k_ref[pl.ds(diag_start + half, half), cols]
        qk = lambda q, k: lax.dot_general(q, k, nt, preferred_element_type=jnp.float32)
        s_tl = jnp.where(quad_allowed, qk(q_ref[:half, cols], k_lo), masked)
        s_bl = qk(q_ref[half:, cols], k_lo)
        s_br = jnp.where(quad_allowed, qk(q_ref[half:, cols], k_hi), masked)
        diag = s_ref.at[hd * n_kv + qi]
        diag[:half, :half] = s_tl
        diag[half:, :half] = s_bl
        diag[half:, half:] = s_br
        mx_ref[hd, :half] = lane_fold(s_tl, jnp.maximum)
        mx_ref[hd, half:] = jnp.maximum(lane_fold(s_bl, jnp.maximum), lane_fold(s_br, jnp.maximum))

    def pass_a(j, c):
        for hd in range(MLA_HEADS):
            s = scores(hd, j)
            s_ref[hd * n_kv + j] = s
            mx_ref[hd] = jnp.maximum(mx_ref[hd], lane_fold(s, jnp.maximum))
        return c

    lax.fori_loop(0, qi, pass_a, 0)

    for hd in range(MLA_HEADS):
        mx_ref[hd] = jnp.broadcast_to(jnp.max(mx_ref[hd], axis=-1, keepdims=True), (tq, LANES))

    def probs(hd, j):
        s = s_ref[hd * n_kv + j]
        mb = mx_ref[hd]
        p = jnp.exp2(jnp.concatenate([t - mb for t in lane_tiles(s)], axis=-1))
        start = pl.multiple_of(j * ATT_K, ATT_K)
        vb = v_ref[pl.ds(start, ATT_K), hd * LANES:(hd + 1) * LANES]
        pv = jnp.dot(p.astype(jnp.bfloat16), vb, preferred_element_type=jnp.float32)
        return lane_fold(p, jnp.add), pv

    for hd in range(MLA_HEADS):
        diag = s_ref.at[hd * n_kv + qi]
        vcols = slice(hd * LANES, (hd + 1) * LANES)
        p_top = jnp.exp2(jnp.concatenate([t - mx_ref[hd, :half] for t in lane_tiles(diag[:half, :half])], axis=-1))
        p_bot = jnp.exp2(jnp.concatenate([t - mx_ref[hd, half:] for t in lane_tiles(diag[half:, :])], axis=-1))
        ls_ref[hd, :half] = lane_fold(p_top, jnp.add)
        ls_ref[hd, half:] = lane_fold(p_bot, jnp.add)
        acc_ref[hd, :half] = jnp.dot(p_top.astype(jnp.bfloat16), v_ref[pl.ds(diag_start, half), vcols],
                                     preferred_element_type=jnp.float32)
        acc_ref[hd, half:] = jnp.dot(p_bot.astype(jnp.bfloat16), v_ref[pl.ds(diag_start, ATT_K), vcols],
                                     preferred_element_type=jnp.float32)

    def pass_b(j, c):
        for hd in range(MLA_HEADS):
            ls, pv = probs(hd, j)
            ls_ref[hd] = ls_ref[hd] + ls
            acc_ref[hd] = acc_ref[hd] + pv
        return c

    lax.fori_loop(0, qi, pass_b, 0)

    for hd in range(MLA_HEADS):
        l = jnp.sum(ls_ref[hd], axis=-1, keepdims=True)
        o_ref[:, hd * V_HEAD_DIM:(hd + 1) * V_HEAD_DIM] = (acc_ref[hd] / l).astype(jnp.bfloat16)


def _attn(q, k, v):
    tq = ATT_Q
    tokblk = lambda cols: pl.BlockSpec((tq, cols), lambda b, i: (b * N_QBLK + i, 0))
    seqblk = lambda cols: pl.BlockSpec((SEQ, cols), lambda b, i: (b, 0))
    return pl.pallas_call(
        _attn_kernel,
        grid=(BATCH, N_QBLK),
        in_specs=[tokblk(D_MODEL), seqblk(D_MODEL), seqblk(GM_WIDTH)],
        out_specs=tokblk(GM_WIDTH),
        out_shape=jax.ShapeDtypeStruct((N_TOK, MLA_HEADS * V_HEAD_DIM), jnp.bfloat16),
        scratch_shapes=[pltpu.VMEM((MLA_HEADS * (SEQ // ATT_K), tq, ATT_K), jnp.float32),
                        pltpu.VMEM((MLA_HEADS, tq, LANES), jnp.float32),
                        pltpu.VMEM((MLA_HEADS, tq, LANES), jnp.float32),
                        pltpu.VMEM((MLA_HEADS, tq, V_HEAD_DIM), jnp.float32)],
        compiler_params=pltpu.CompilerParams(
            dimension_semantics=("arbitrary", "arbitrary"), vmem_limit_bytes=VMEM_LIMIT),
        name="attn",
    )(q, k, v)


def _proj_kernel(outa_ref, ob_ref, h_hbm_ref, wout_ref, ln1g_ref, ln1b_ref, wr_ref, br_ref,
                 h1_ref, h1b_ref, ri_ref, rf_ref, proj_ref, hring_ref, hsems):
    i = pl.program_id(0)
    tb = outa_ref.shape[0]
    h_ref = _ring_block(h_hbm_ref, hring_ref, hsems, lambda step: jnp.maximum(step - 1, 0), tb)

    @pl.when(i == 0)
    def _():
        proj_ref[...] = jnp.zeros_like(proj_ref)

    h1 = _layer_norm(ALPHA * h_ref[...] + proj_ref[(i + 1) % 2], ln1g_ref[...], ln1b_ref[...])
    h1_ref[...] = h1
    h1b_ref[...] = h1.astype(jnp.bfloat16)

    logits_tm = jnp.dot(h1b_ref[...], wr_ref[...], preferred_element_type=jnp.float32)
    logits = logits_tm.T[0:ROUTER_ROWS] + br_ref[...]

    sub_i = lax.broadcasted_iota(jnp.int32, (SUBLANES, tb), 0)
    sub = sub_i.astype(jnp.float32)
    neg = jnp.float32(-jnp.inf)
    g = jnp.where(sub_i < N_GROUPS, logits[0:SUBLANES], neg)
    gmax = jnp.max(g, axis=0, keepdims=True)
    g_top = jnp.min(jnp.where(g == gmax, sub, float(SUBLANES)), axis=0, keepdims=True)
    p_group = 1.0 / jnp.sum(jnp.exp(g - gmax), axis=0, keepdims=True)
    sel = logits[SUBLANES:2 * SUBLANES]
    for grp in range(1, N_GROUPS):
        sel = jnp.where(g_top == float(grp), logits[(grp + 1) * SUBLANES:(grp + 2) * SUBLANES], sel)
    v1 = jnp.max(sel, axis=0, keepdims=True)
    i1 = jnp.min(jnp.where(sel == v1, sub, float(SUBLANES)), axis=0, keepdims=True)
    sel2 = jnp.where(sub == i1, neg, sel)
    v2 = jnp.max(sel2, axis=0, keepdims=True)
    i2 = jnp.min(jnp.where(sel2 == v2, sub, float(SUBLANES)), axis=0, keepdims=True)
    e21 = jnp.exp(v2 - v1)
    w1 = 1.0 / (1.0 + e21)
    gate1 = p_group * w1
    gate2 = p_group * (e21 * w1)
    e1 = g_top * EXPERTS_PER_GROUP + i1
    e2 = g_top * EXPERTS_PER_GROUP + i2
    ri_ref[...] = jnp.where(sub_i == 0, e1, jnp.where(sub_i == 1, e2, 0.0)).astype(jnp.int32)
    rf_ref[...] = jnp.where(sub_i == 0, gate1, jnp.where(sub_i == 1, gate2, 0.0))

    proj_ref[i % 2] = (jnp.dot(outa_ref[...], wout_ref[:GM_WIDTH, :], preferred_element_type=jnp.float32)
                       + jnp.dot(ob_ref[...], wout_ref[GM_WIDTH:, :], preferred_element_type=jnp.float32))


def _proj(outa, ob, h, wout, ln1g, ln1b, wr, br):
    tb = PROJ_TOKENS
    n_blk = N_TOK // tb
    cur = lambda i: jnp.minimum(i, n_blk - 1)
    prev = lambda i: jnp.maximum(i - 1, 0)
    full = lambda shape: pl.BlockSpec(shape, lambda i: (0,) * len(shape))
    return pl.pallas_call(
        _proj_kernel,
        grid=(n_blk + 1,),
        in_specs=[pl.BlockSpec((tb, GM_WIDTH), lambda i: (cur(i), 0)),
                  pl.BlockSpec((tb, GM_WIDTH), lambda i: (cur(i), 0)),
                  pl.BlockSpec(memory_space=pl.ANY),
                  full((D_MODEL, D_MODEL)), full((1, D_MODEL)), full((1, D_MODEL)),
                  full((D_MODEL, LANES)), full((ROUTER_ROWS, 1))],
        out_specs=[pl.BlockSpec((tb, D_MODEL), lambda i: (prev(i), 0)),
                   pl.BlockSpec((tb, D_MODEL), lambda i: (prev(i), 0)),
                   pl.BlockSpec((SUBLANES, tb), lambda i: (0, prev(i))),
                   pl.BlockSpec((SUBLANES, tb), lambda i: (0, prev(i)))],
        out_shape=[jax.ShapeDtypeStruct((N_TOK, D_MODEL), jnp.float32),
                   jax.ShapeDtypeStruct((N_TOK, D_MODEL), jnp.bfloat16),
                   jax.ShapeDtypeStruct((SUBLANES, N_TOK), jnp.int32),
                   jax.ShapeDtypeStruct((SUBLANES, N_TOK), jnp.float32)],
        scratch_shapes=[pltpu.VMEM((2, tb, D_MODEL), jnp.float32)] + _ring_scratch(tb, D_MODEL, jnp.float32),
        compiler_params=pltpu.CompilerParams(
            dimension_semantics=("arbitrary",), vmem_limit_bytes=VMEM_LIMIT),
        name="proj",
    )(outa, ob, h, wout, ln1g, ln1b, wr, br)


def _plan_kernel(ri_all_ref, ri_ref, ldest_ref, runs_ref, meta_ref, run_ref, start_ref, upper_ref):
    step = pl.program_id(0)
    tb = ri_ref.shape[1]
    f32 = jnp.float32
    er = lax.broadcasted_iota(jnp.int32, (N_EXPERTS, LANES), 0)
    ec = lax.broadcasted_iota(jnp.int32, (N_EXPERTS, LANES), 1)
    to_row = lambda col: jnp.sum(jnp.where(er == ec, col, 0.0), axis=0, keepdims=True)

    def expert_one_hot(ref):
        e_sub = lax.broadcasted_iota(jnp.int32, (N_EXPERTS, ref.shape[1]), 0)
        return e_sub == ref[0:1, :], e_sub == ref[1:2, :]

    @pl.when(step == 0)
    def _():
        oh1, oh2 = expert_one_hot(ri_all_ref)
        counts = jnp.sum(jnp.where(oh1 | oh2, 1.0, 0.0), axis=1, keepdims=True)
        padded = jnp.floor((counts + (EXPERT_ROWS - 1)) * (1.0 / EXPERT_ROWS)) * EXPERT_ROWS
        pad_end = jnp.sum(jnp.where(ec <= er, to_row(padded), 0.0), axis=1, keepdims=True)
        start_ref[...] = jnp.broadcast_to(pad_end - padded, start_ref.shape)
        run_ref[...] = jnp.zeros_like(run_ref)
        bstart = (lax.broadcasted_iota(jnp.int32, (N_EXPERTS, META_LANES), 1) * EXPERT_ROWS).astype(f32)
        blk_e = jnp.sum(jnp.where(pad_end <= bstart, 1.0, 0.0), axis=0, keepdims=True)
        blk_e = jnp.minimum(blk_e, N_EXPERTS - 1.0)
        n_used = pad_end[N_EXPERTS - 1:N_EXPERTS, :] * (1.0 / EXPERT_ROWS)
        pad3 = lambda r: jnp.concatenate(
            [r, jnp.zeros((1, META_LANES - LANES), f32)], axis=1)
        msub = lax.broadcasted_iota(jnp.int32, (SUBLANES, META_LANES), 0)
        meta = jnp.where(msub == 0, blk_e,
                         jnp.where(msub == 1, pad3(to_row(pad_end)),
                                   jnp.where(msub == 2, pad3(to_row(counts)),
                                             jnp.where(msub == 3, n_used, 0.0))))
        meta_ref[...] = meta.astype(jnp.int32)
        tr = lax.broadcasted_iota(jnp.int32, (tb, tb), 0)
        tc = lax.broadcasted_iota(jnp.int32, (tb, tb), 1)
        upper_ref[...] = jnp.where(tr < tc, 1.0, 0.0).astype(jnp.bfloat16)

    @pl.when(step > 0)
    def _():
        oh1, oh2 = expert_one_hot(ri_ref)
        oh = jnp.where(oh1 | oh2, 1.0, 0.0).astype(f32)
        blk_count = jnp.sum(oh, axis=1, keepdims=True)
        prefix = jnp.dot(oh.astype(jnp.bfloat16), upper_ref[...], preferred_element_type=f32)
        cnt_row = to_row(blk_count)
        lstart = jnp.sum(jnp.where(ec < er, cnt_row, 0.0), axis=1, keepdims=True)
        base = prefix + lstart
        d1 = jnp.sum(jnp.where(oh1, base, 0.0), axis=0, keepdims=True)
        d2 = jnp.sum(jnp.where(oh2, base, 0.0), axis=0, keepdims=True)
        sub = lax.broadcasted_iota(jnp.int32, (SUBLANES, tb), 0)
        ldest_ref[...] = jnp.where(sub == 0, d1, jnp.where(sub == 1, d2, 0.0)).astype(jnp.int32)
        gstart = start_ref[:, 0:1] + run_ref[:, 0:1]
        rsub = lax.broadcasted_iota(jnp.int32, (SUBLANES, LANES), 0)
        runs = jnp.where(rsub == 0, cnt_row,
                         jnp.where(rsub == 1, to_row(lstart * FEAT_TILES),
                                   jnp.where(rsub == 2, to_row(gstart * FEAT_TILES), 0.0)))
        runs_ref[...] = runs.astype(jnp.int32)
        run_ref[...] = run_ref[...] + blk_count


def _plan(ri):
    tb = MOE_TOKENS
    blk = lambda i: jnp.maximum(i - 1, 0)
    return pl.pallas_call(
        _plan_kernel,
        grid=(N_MOE_BLOCKS + 1,),
        in_specs=[pl.BlockSpec((SUBLANES, N_TOK), lambda i: (0, 0)),
                  pl.BlockSpec((SUBLANES, tb), lambda i: (0, blk(i)))],
        out_specs=[pl.BlockSpec((SUBLANES, tb), lambda i: (0, blk(i))),
                   pl.BlockSpec((SUBLANES, LANES), lambda i: (blk(i), 0)),
                   pl.BlockSpec((SUBLANES, META_LANES), lambda i: (0, 0))],
        out_shape=[jax.ShapeDtypeStruct((SUBLANES, N_TOK), jnp.int32),
                   jax.ShapeDtypeStruct((N_MOE_BLOCKS * SUBLANES, LANES), jnp.int32),
                   jax.ShapeDtypeStruct((SUBLANES, META_LANES), jnp.int32)],
        scratch_shapes=[pltpu.VMEM((N_EXPERTS, LANES), jnp.float32),
                        pltpu.VMEM((N_EXPERTS, LANES), jnp.float32),
                        pltpu.VMEM((tb, tb), jnp.bfloat16)],
        compiler_params=pltpu.CompilerParams(
            dimension_semantics=("arbitrary",), vmem_limit_bytes=VMEM_LIMIT),
        name="plan",
    )(ri, ri)


def _for_each_run_piece(runs_ref, row0, fn, live=True):
    for e in range(N_EXPERTS):
        n_tiles = jnp.where(live, runs_ref[row0, e], 0) * FEAT_TILES
        lstart, gstart = runs_ref[row0 + 1, e], runs_ref[row0 + 2, e]
        taken = 0
        for bit in reversed(range(RUN_BITS)):
            piece = n_tiles & (FEAT_TILES << bit)

            @pl.when(piece != 0)
            def _(lstart=lstart, gstart=gstart, bit=bit, taken=taken):
                fn(lstart + taken, gstart + taken, 1 << bit)

            taken = taken + piece


def _tile_rows(ref, tile_row, rows):
    return ref.at[pl.ds(pl.multiple_of(tile_row, FEAT_TILES), rows * FEAT_TILES)]


def _dispatch_kernel(meta_ref, runs_ref, ldest_ref, h1b_ref, buf_ref, *scratch):
    sorted_refs = scratch[:MOE_GROUP]
    zero_ref, sems, zsem = scratch[MOE_GROUP:]
    i = pl.program_id(0)
    n_steps = pl.num_programs(0)
    tb = MOE_TOKENS
    slot = i % 2
    block_tiles = 2 * tb * FEAT_TILES

    def wait_slot(g, s):
        pltpu.make_async_copy(sorted_refs[g].at[s], buf_ref.at[pl.ds(0, block_tiles)], sems.at[g, s]).wait()

    @pl.when(i == 0)
    def _():
        zero_ref[...] = jnp.zeros_like(zero_ref)

        def zero_copy(e):
            start = pl.multiple_of((meta_ref[1, e] - EXPERT_ROWS) * FEAT_TILES, EXPERT_ROWS * FEAT_TILES)
            return pltpu.make_async_copy(
                zero_ref, buf_ref.at[pl.ds(start, EXPERT_ROWS * FEAT_TILES)], zsem)

        def start_zero(e, c):
            @pl.when(meta_ref[2, e] > 0)
            def _():
                zero_copy(e).start()
            return c

        def wait_zero(e, c):
            @pl.when(meta_ref[2, e] > 0)
            def _():
                zero_copy(e).wait()
            return c

        def tail_copy(b):
            start = pl.multiple_of(b * (EXPERT_ROWS * FEAT_TILES), EXPERT_ROWS * FEAT_TILES)
            return pltpu.make_async_copy(
                zero_ref, buf_ref.at[pl.ds(start, EXPERT_ROWS * FEAT_TILES)], zsem)

        def start_tail(b, c):
            tail_copy(b).start()
            return c

        def wait_tail(b, c):
            tail_copy(b).wait()
            return c

        lax.fori_loop(0, N_EXPERTS, start_zero, 0)
        lax.fori_loop(meta_ref[3, 0], N_ROW_BLOCKS, start_tail, 0)
        lax.fori_loop(0, N_EXPERTS, wait_zero, 0)
        lax.fori_loop(meta_ref[3, 0], N_ROW_BLOCKS, wait_tail, 0)

    @pl.when(i >= 2)
    def _():
        for g in range(MOE_GROUP):
            wait_slot(g, slot)

    for g in range(MOE_GROUP):
        x = h1b_ref[g * tb:(g + 1) * tb, :]
        ld0 = ldest_ref[0:1, g * tb:(g + 1) * tb]
        ld1 = ldest_ref[1:2, g * tb:(g + 1) * tb]
        for c in range(2 * tb // SORT_CHUNK):
            r = lax.broadcasted_iota(jnp.int32, (SORT_CHUNK, tb), 0) + c * SORT_CHUNK
            perm = jnp.where((r == ld0) | (r == ld1), 1.0, 0.0).astype(jnp.bfloat16)
            rows = jnp.dot(perm, x, preferred_element_type=jnp.float32)
            _to_row_tiles(sorted_refs[g].at[slot, pl.ds(c * SORT_CHUNK * FEAT_TILES, SORT_CHUNK * FEAT_TILES)],
                          rows)

        def send(lrow, grow, rows, g=g):
            pltpu.make_async_copy(_tile_rows(sorted_refs[g].at[slot], lrow, rows),
                                  _tile_rows(buf_ref, grow, rows), sems.at[g, slot]).start()

        _for_each_run_piece(runs_ref, g * SUBLANES, send)

    @pl.when(i == n_steps - 1)
    def _():
        for g in range(MOE_GROUP):
            wait_slot(g, slot)
            wait_slot(g, 1 - slot)


def _dispatch(meta, runs, ldest, h1b):
    tb = MOE_GROUP * MOE_TOKENS
    return pl.pallas_call(
        _dispatch_kernel,
        grid_spec=pltpu.PrefetchScalarGridSpec(
            num_scalar_prefetch=1,
            grid=(N_MOE_BLOCKS // MOE_GROUP,),
            in_specs=[pl.BlockSpec((MOE_GROUP * SUBLANES, LANES), lambda i, m: (i, 0), memory_space=pltpu.SMEM),
                      pl.BlockSpec((SUBLANES, tb), lambda i, m: (0, i)),
                      pl.BlockSpec((tb, D_MODEL), lambda i, m: (i, 0))],
            out_specs=pl.BlockSpec(memory_space=pl.ANY),
            scratch_shapes=[pltpu.VMEM((2, 2 * MOE_TOKENS * FEAT_TILES, LANES), jnp.float32)] * MOE_GROUP + [
                pltpu.VMEM((EXPERT_ROWS * FEAT_TILES, LANES), jnp.float32),
                pltpu.SemaphoreType.DMA((MOE_GROUP, 2)), pltpu.SemaphoreType.DMA]),
        out_shape=jax.ShapeDtypeStruct((N_ROWS * FEAT_TILES, LANES), jnp.float32),
        compiler_params=pltpu.CompilerParams(
            dimension_semantics=("arbitrary",), vmem_limit_bytes=VMEM_LIMIT),
        name="dispatch",
    )(meta, runs, ldest, h1b)


def _sub_block_expert(meta, i, sub):
    return meta[0, jnp.minimum(i + sub * (N_ROW_BLOCKS // EXPERT_SUB), meta[3, 0] - 1)]


def _experts_kernel(meta_ref, x_hbm_ref, *refs):
    weights = [refs[3 * sub:3 * sub + 3] for sub in range(EXPERT_SUB)]
    o_ref, wgb_ref, wub_ref, wdb_ref, cached_ref, xring_ref, xsems = refs[3 * EXPERT_SUB:]
    n = EXPERT_ROWS
    i = pl.program_id(0)
    n_steps = pl.num_programs(0)

    def rows_copy(step):
        slot = step % EXPERT_RING
        start = pl.multiple_of(step * (n * FEAT_TILES), n * FEAT_TILES)
        return pltpu.make_async_copy(x_hbm_ref.at[:, pl.ds(start, n * FEAT_TILES), :], xring_ref.at[slot],
                                     xsems.at[slot])

    @pl.when(i == 0)
    def _():
        for step in range(EXPERT_RING - 1):
            rows_copy(step).start()

    @pl.when(i + (EXPERT_RING - 1) < n_steps)
    def _():
        rows_copy(i + (EXPERT_RING - 1)).start()

    rows_copy(i).wait()
    x_ref = xring_ref.at[i % EXPERT_RING]

    @pl.when(i == 0)
    def _():
        for sub in range(EXPERT_SUB):
            cached_ref[sub] = -1

    for sub in range(EXPERT_SUB):
        expert = _sub_block_expert(meta_ref, i, sub)

        @pl.when(cached_ref[sub] != expert)
        def _(sub=sub, expert=expert):
            wg_ref, wu_ref, wd_ref = weights[sub]
            wgb_ref[sub] = wg_ref[...].astype(jnp.bfloat16)
            wub_ref[sub] = wu_ref[...].astype(jnp.bfloat16)
            wdb_ref[sub] = wd_ref[...].astype(jnp.bfloat16)
            cached_ref[sub] = expert

    for sub in range(EXPERT_SUB):
        x = _from_row_tiles(x_ref.at[sub], n).astype(jnp.bfloat16)
        gate = jnp.dot(x, wgb_ref[sub], preferred_element_type=jnp.float32)
        up = jnp.dot(x, wub_ref[sub], preferred_element_type=jnp.float32)
        act = (gate * jax.nn.sigmoid(gate) * up).astype(jnp.bfloat16)
        _to_row_tiles(o_ref.at[sub], jnp.dot(act, wdb_ref[sub], preferred_element_type=jnp.float32))


def _experts(meta, buf, wg, wu, wd):
    sub_tiles = (N_ROW_BLOCKS // EXPERT_SUB) * EXPERT_ROWS * FEAT_TILES

    def weight_spec(shape, sub):
        return pl.BlockSpec((None,) + shape, lambda i, m: (_sub_block_expert(m, i, sub), 0, 0))

    rows_shape = (EXPERT_SUB, EXPERT_ROWS * FEAT_TILES, LANES)
    up_shape, down_shape = (D_MODEL, EXPERT_FF), (EXPERT_FF, D_MODEL)
    eout = pl.pallas_call(
        _experts_kernel,
        grid_spec=pltpu.PrefetchScalarGridSpec(
            num_scalar_prefetch=1,
            grid=(N_ROW_BLOCKS // EXPERT_SUB,),
            in_specs=[pl.BlockSpec(memory_space=pl.ANY)] + [
                weight_spec(shape, sub) for sub in range(EXPERT_SUB) for shape in (up_shape, up_shape, down_shape)],
            out_specs=pl.BlockSpec(rows_shape, lambda i, m: (0, i, 0)),
            scratch_shapes=[pltpu.VMEM((EXPERT_SUB,) + up_shape, jnp.bfloat16),
                            pltpu.VMEM((EXPERT_SUB,) + up_shape, jnp.bfloat16),
                            pltpu.VMEM((EXPERT_SUB,) + down_shape, jnp.bfloat16),
                            pltpu.SMEM((EXPERT_SUB,), jnp.int32),
                            pltpu.VMEM((EXPERT_RING,) + rows_shape, jnp.float32),
                            pltpu.SemaphoreType.DMA((EXPERT_RING,))]),
        out_shape=jax.ShapeDtypeStruct((EXPERT_SUB, sub_tiles, LANES), jnp.float32),
        compiler_params=pltpu.CompilerParams(
            dimension_semantics=("arbitrary",), vmem_limit_bytes=EXPERT_VMEM_LIMIT),
        name="experts",
    )(meta, buf.reshape(EXPERT_SUB, sub_tiles, LANES), *([wg, wu, wd] * EXPERT_SUB))
    return eout.reshape(N_ROWS * FEAT_TILES, LANES)


def _combine_kernel(runs_ref, runs_next_ref, ldest_ref, rf_ref, h1_ref, eout_ref, ln2g_ref, ln2b_ref,
                    o_ref, *scratch):
    y_refs = scratch[:MOE_GROUP]
    sems = scratch[MOE_GROUP]
    i = pl.program_id(0)
    n_steps = pl.num_programs(0)
    tb = MOE_TOKENS
    slot = i % 2
    block_tiles = 2 * tb * FEAT_TILES

    def fetch(table_ref, g, s, live):
        def recv(lrow, grow, rows):
            pltpu.make_async_copy(_tile_rows(eout_ref, grow, rows),
                                  _tile_rows(y_refs[g].at[s], lrow, rows), sems.at[g, s]).start()
        _for_each_run_piece(table_ref, g * SUBLANES, recv, live)

    @pl.when(i == 0)
    def _():
        for g in range(MOE_GROUP):
            fetch(runs_ref, g, slot, True)

    for g in range(MOE_GROUP):
        pltpu.make_async_copy(eout_ref.at[pl.ds(0, block_tiles)], y_refs[g].at[slot], sems.at[g, slot]).wait()
        fetch(runs_next_ref, g, 1 - slot, i + 1 < n_steps)

        ld = ldest_ref[:, g * tb:(g + 1) * tb].astype(jnp.float32).T
        gates = rf_ref[:, g * tb:(g + 1) * tb].T
        y = None
        col = lax.broadcasted_iota(jnp.int32, (tb, SORT_CHUNK), 1).astype(jnp.float32)
        for c in range(2 * tb // SORT_CHUNK):
            ld_c = ld - float(c * SORT_CHUNK)
            gm = jnp.where(col == ld_c[:, 0:1], gates[:, 0:1],
                           jnp.where(col == ld_c[:, 1:2], gates[:, 1:2], 0.0)).astype(jnp.bfloat16)
            rows = _from_row_tiles(
                y_refs[g].at[slot, pl.ds(c * SORT_CHUNK * FEAT_TILES, SORT_CHUNK * FEAT_TILES)], SORT_CHUNK)
            part = jnp.dot(gm, rows.astype(jnp.bfloat16), preferred_element_type=jnp.float32)
            y = part if y is None else y + part
        tok = slice(g * tb, (g + 1) * tb)
        o_ref[tok, :] = _layer_norm(ALPHA * h1_ref[tok, :] + y, ln2g_ref[...], ln2b_ref[...])


def _combine(runs, ldest, rf, h1, eout, ln2g, ln2b):
    tb = MOE_GROUP * MOE_TOKENS
    n_steps = N_MOE_BLOCKS // MOE_GROUP
    table = lambda index_map: pl.BlockSpec((MOE_GROUP * SUBLANES, LANES), index_map, memory_space=pltpu.SMEM)
    return pl.pallas_call(
        _combine_kernel,
        grid=(n_steps,),
        in_specs=[table(lambda i: (i, 0)), table(lambda i: (jnp.minimum(i + 1, n_steps - 1), 0)),
                  pl.BlockSpec((SUBLANES, tb), lambda i: (0, i)),
                  pl.BlockSpec((SUBLANES, tb), lambda i: (0, i)),
                  pl.BlockSpec((tb, D_MODEL), lambda i: (i, 0)),
                  pl.BlockSpec(memory_space=pl.ANY),
                  pl.BlockSpec((1, D_MODEL), lambda i: (0, 0)),
                  pl.BlockSpec((1, D_MODEL), lambda i: (0, 0))],
        out_specs=pl.BlockSpec((tb, D_MODEL), lambda i: (i, 0)),
        out_shape=jax.ShapeDtypeStruct((N_TOK, D_MODEL), jnp.float32),
        scratch_shapes=[pltpu.VMEM((2, 2 * MOE_TOKENS * FEAT_TILES, LANES), jnp.float32)] * MOE_GROUP + [
            pltpu.SemaphoreType.DMA((MOE_GROUP, 2))],
        compiler_params=pltpu.CompilerParams(
            dimension_semantics=("arbitrary",), vmem_limit_bytes=VMEM_LIMIT),
        name="combine",
    )(runs, runs, ldest, rf, h1, eout, ln2g, ln2b)


def _swap_halves(w):
    half = w.shape[-1] // 2
    return jnp.concatenate([w[..., half:], w[..., :half]], axis=-1)


def kernel(x, positions, ln0_g, ln0_b, w_in, gm_ln_g, gm_ln_b, w_spatial, b_spatial, q_norm_g, w_uq, kv_norm_g, w_ukv, w_out, ln1_g, ln1_b, w_router_group, b_router_group, w_router_expert, b_router_expert, w_gate, w_up, w_down, ln2_g, ln2_b):
    bf16 = jnp.bfloat16
    row = lambda a: a.reshape(1, -1)

    w_in0 = w_in[0]
    kr_cols = w_in0[:, O_KR:O_KR + QK_ROPE_DIM]
    win = jnp.concatenate([w_in0, _swap_halves(kr_cols)], axis=1).astype(bf16)
    wuq3 = w_uq[0].reshape(Q_LORA_RANK, MLA_HEADS, QK_NOPE_DIM + QK_ROPE_DIM)
    rope_cols = wuq3[:, :, QK_NOPE_DIM:]
    wuq = jnp.concatenate([wuq3, _swap_halves(rope_cols)], axis=-1).reshape(Q_LORA_RANK, D_MODEL).astype(bf16)
    wukv = w_ukv[0].astype(bf16)
    wout = w_out[0].astype(bf16)
    bs = jnp.broadcast_to(b_spatial[0][:, :, None], (GM_HEADS, GM_CHUNK, GM_HEAD_DIM))
    wr = jnp.concatenate([w_router_group[0], jnp.zeros((D_MODEL, SUBLANES - N_GROUPS), jnp.float32),
                          w_router_expert[0],
                          jnp.zeros((D_MODEL, LANES - ROUTER_ROWS), jnp.float32)],
                         axis=1).astype(bf16)
    br = jnp.concatenate([b_router_group[0], jnp.zeros((SUBLANES - N_GROUPS,), jnp.float32),
                          b_router_expert[0]]).reshape(ROUTER_ROWS, 1)

    inv_freq = ROPE_THETA ** (-jnp.arange(0, QK_ROPE_DIM, 2, dtype=jnp.float32) / QK_ROPE_DIM)
    freq = jnp.tile(inv_freq, 4).reshape(1, LANES)
    quarter = QK_ROPE_DIM // 2
    phase = jnp.concatenate([jnp.zeros((2 * quarter,), jnp.float32),
                             jnp.full((2 * quarter,), math.pi / 2, jnp.float32)]).reshape(1, LANES)
    sign = jnp.concatenate([jnp.ones((2 * quarter,), jnp.float32), -jnp.ones((quarter,), jnp.float32),
                            jnp.ones((quarter,), jnp.float32)]).reshape(1, LANES)

    x2 = x.reshape(N_TOK, D_MODEL)
    pos2 = positions.reshape(N_TOK // PREP_TOKENS, 1, PREP_TOKENS)

    h, outa, q, k, v = _prep(x2, pos2, row(ln0_g), row(ln0_b), win, row(gm_ln_g[0]), row(gm_ln_b[0]),
                             w_spatial[0], bs, row(q_norm_g[0]), wuq, row(kv_norm_g[0]), wukv,
                             freq, phase, sign)
    ob = _attn(q, k, v)
    h1, h1b, ri, rf = _proj(outa, ob, h, wout, row(ln1_g[0]), row(ln1_b[0]), wr, br)
    ldest, runs, meta = _plan(ri)
    buf = _dispatch(meta, runs, ldest, h1b)
    eout = _experts(meta, buf, w_gate[0], w_up[0], w_down[0])
    out = _combine(runs, ldest, rf, h1, eout, row(ln2_g[0]), row(ln2_b[0]))
    return out.reshape(BATCH, SEQ, D_MODEL)
```

```python
import functools
import math

import jax
import jax.numpy as jnp
from jax import lax
from jax.experimental import pallas as pl
from jax.experimental.pallas import tpu as pltpu

D_MODEL = 1024
BATCH = 16
SEQ = 2048
N_TOK = BATCH * SEQ
CHUNK = 64
GM_WIDTH = 512
GM_HEADS = 4
GM_HEAD_DIM = 128
GM_CHUNK = 128
MLA_HEADS = 4
QK_NOPE_DIM = 128
QK_ROPE_DIM = 64
V_HEAD_DIM = 128
Q_LORA_RANK = 384
KV_LORA_RANK = 256
ROPE_THETA = 10000.0
N_GROUPS = 4
EXPERTS_PER_GROUP = 8
N_EXPERTS = 32
TOP_K = 2
EXPERT_FF = 256
ALPHA = 2.0 ** 0.25
QK_SCALE = (QK_NOPE_DIM + QK_ROPE_DIM) ** -0.5 * math.log2(math.e)

LANES = 128
SUBLANES = 8
FEAT_TILES = D_MODEL // LANES
PREP_TOKENS = 1024
ATT_Q = 512
ATT_K = 512
N_QBLK = SEQ // ATT_Q
PROJ_TOKENS = 1024
MOE_TOKENS = 512
N_MOE_BLOCKS = N_TOK // MOE_TOKENS
MOE_GROUP = 2
RUN_BITS = (TOP_K * MOE_TOKENS).bit_length()
SORT_CHUNK = 256
EXPERT_ROWS = 256
EXPERT_SUB = 4
EXPERT_RING = 3
INPUT_RING = 3
N_ROWS = N_TOK * TOP_K + N_EXPERTS * EXPERT_ROWS
N_ROW_BLOCKS = N_ROWS // EXPERT_ROWS
META_LANES = 384
IN_COLS = 2 * GM_WIDTH + Q_LORA_RANK + KV_LORA_RANK + 2 * QK_ROPE_DIM
O_Q = 2 * GM_WIDTH
O_KV = O_Q + Q_LORA_RANK
O_KR = O_KV + KV_LORA_RANK
ROUTER_ROWS = 40
VMEM_LIMIT = 48 * 1024 * 1024
EXPERT_VMEM_LIMIT = 56 * 1024 * 1024

assert N_ROW_BLOCKS <= META_LANES and N_ROW_BLOCKS % EXPERT_SUB == 0


def _layer_norm(x, g, b, eps=1e-5):
    mu = jnp.mean(x, axis=-1, keepdims=True)
    xc = x - mu
    var = jnp.mean(xc * xc, axis=-1, keepdims=True)
    return xc * lax.rsqrt(var + eps) * g + b


def _rms_norm(x, g, eps=1e-6):
    return x * lax.rsqrt(jnp.mean(x * x, axis=-1, keepdims=True) + eps) * g


def _gelu_tanh(x):
    c = math.sqrt(2.0 / math.pi)
    return 0.5 * x * (1.0 + jnp.tanh(c * (x + 0.044715 * (x * x * x))))


def _to_row_tiles(ref, x):
    n = x.shape[0]
    for s in range(FEAT_TILES):
        ref[pl.ds(s, n, stride=FEAT_TILES), :] = x[:, s * LANES:(s + 1) * LANES]


def _from_row_tiles(ref, n):
    return jnp.concatenate(
        [ref[pl.ds(s, n, stride=FEAT_TILES), :] for s in range(FEAT_TILES)], axis=-1)


def _ring_block(hbm_ref, ring_ref, sems, block_of_step, rows):
    i = pl.program_id(0)
    n_steps = pl.num_programs(0)

    def copy(step):
        start = pl.multiple_of(block_of_step(step) * rows, rows)
        slot = step % INPUT_RING
        return pltpu.make_async_copy(hbm_ref.at[pl.ds(start, rows)], ring_ref.at[slot], sems.at[slot])

    @pl.when(i == 0)
    def _():
        for step in range(INPUT_RING - 1):
            copy(step).start()

    @pl.when(i + (INPUT_RING - 1) < n_steps)
    def _():
        copy(i + (INPUT_RING - 1)).start()

    copy(i).wait()
    return ring_ref.at[i % INPUT_RING]


def _ring_scratch(rows, cols, dtype):
    return [pltpu.VMEM((INPUT_RING, rows, cols), dtype), pltpu.SemaphoreType.DMA((INPUT_RING,))]


def _prep_kernel(x_ref, pos_ref, ln0g_ref, ln0b_ref, win_ref, gmg_ref, gmb_ref, ws_ref, bs_ref,
                 qg_ref, wuq_ref, kvg_ref, wukv_ref, freq_ref, phase_ref, sign_ref,
                 h_ref, outa_ref, q_ref, k_ref, v_ref, tabc_ref, tabs_ref, rot_ref):
    tb = x_ref.shape[0]

    @pl.when(pl.program_id(0) == 0)
    def _():
        d = lax.broadcasted_iota(jnp.int32, (tb, LANES), 0).astype(jnp.float32) * freq_ref[...]
        tabc_ref[...] = jnp.cos(d)
        tabs_ref[...] = jnp.sin(d)

    pos_row = pos_ref[0]
    p0 = pos_row[:, 0:1]
    offset = lax.broadcasted_iota(jnp.int32, (1, tb), 1)
    consecutive = jnp.max(jnp.abs((pos_row - p0 - offset).astype(jnp.float32))) == 0.0

    @pl.when(consecutive)
    def _():
        a0 = p0.astype(jnp.float32) * freq_ref[...]
        c0, s0 = jnp.cos(a0), jnp.sin(a0)
        lane = lax.broadcasted_iota(jnp.int32, (1, LANES), 1)
        coef_c = jnp.where(lane < 2 * 32, c0, jnp.where(lane < 3 * 32, -s0, s0))
        coef_s = jnp.where(lane < 2 * 32, -s0, jnp.where(lane < 3 * 32, -c0, c0))
        rot_ref[...] = coef_c * tabc_ref[...] + coef_s * tabs_ref[...]

    @pl.when(jnp.logical_not(consecutive))
    def _():
        pos_col = jnp.broadcast_to(pos_row.astype(jnp.float32), (SUBLANES, tb)).T[:, 0:1]
        ang = pos_col * freq_ref[...]
        rot_ref[...] = jnp.cos(ang - phase_ref[...]) * sign_ref[...]

    rot = rot_ref[...]

    h = _layer_norm(x_ref[...], ln0g_ref[...], ln0b_ref[...])
    h_ref[...] = h
    z = jnp.dot(h.astype(jnp.bfloat16), win_ref[...], preferred_element_type=jnp.float32)

    u = _gelu_tanh(z[:, :GM_WIDTH])
    v = _gelu_tanh(z[:, GM_WIDTH:2 * GM_WIDTH])
    row_chunk = lax.broadcasted_iota(jnp.int32, (GM_CHUNK, GM_CHUNK), 0) // CHUNK
    col_chunk = lax.broadcasted_iota(jnp.int32, (GM_CHUNK, GM_CHUNK), 1) // CHUNK
    allowed = col_chunk <= row_chunk
    for hd in range(GM_HEADS):
        lo, hi = hd * GM_HEAD_DIM, (hd + 1) * GM_HEAD_DIM
        vln = _layer_norm(v[:, lo:hi], gmg_ref[:, lo:hi], gmb_ref[:, lo:hi]).astype(jnp.bfloat16)
        wm = jnp.where(allowed, ws_ref[hd], 0.0).astype(jnp.bfloat16)
        for c in range(tb // GM_CHUNK):
            r0, r1 = c * GM_CHUNK, (c + 1) * GM_CHUNK
            f = jnp.dot(wm, vln[r0:r1], preferred_element_type=jnp.float32) + bs_ref[hd]
            outa_ref[r0:r1, lo:hi] = (u[r0:r1, lo:hi] * f).astype(jnp.bfloat16)

    ql = _rms_norm(z[:, O_Q:O_KV], qg_ref[...]).astype(jnp.bfloat16)
    qf = jnp.dot(ql, wuq_ref[...], preferred_element_type=jnp.float32)
    rot_s = rot * QK_SCALE
    q_parts = []
    for hd in range(MLA_HEADS):
        base = hd * 2 * LANES
        q_parts.append(qf[:, base:base + LANES] * QK_SCALE)
        q_parts.append(qf[:, base + LANES:base + 2 * LANES] * rot_s)
    q_ref[...] = jnp.concatenate(q_parts, axis=-1).astype(jnp.bfloat16)

    kvl = _rms_norm(z[:, O_KV:O_KR], kvg_ref[...]).astype(jnp.bfloat16)
    kv = jnp.dot(kvl, wukv_ref[...], preferred_element_type=jnp.float32)
    t = z[:, O_KR:O_KR + LANES] * rot
    krr = t + pltpu.roll(t, 2 * QK_ROPE_DIM // 2, axis=1)
    k_parts, v_parts = [], []
    for hd in range(MLA_HEADS):
        base = hd * 2 * LANES
        k_parts.append(kv[:, base:base + LANES])
        k_parts.append(krr)
        v_parts.append(kv[:, base + LANES:base + 2 * LANES])
    k_ref[...] = jnp.concatenate(k_parts, axis=-1).astype(jnp.bfloat16)
    v_ref[...] = jnp.concatenate(v_parts, axis=-1).astype(jnp.bfloat16)


def _prep(x2, pos2, ln0g, ln0b, win, gmg, gmb, ws, bs, qg, wuq, kvg, wukv, freq, phase, sign):
    tb = PREP_TOKENS
    full = lambda shape: pl.BlockSpec(shape, lambda i: (0,) * len(shape))
    tok = lambda cols: pl.BlockSpec((tb, cols), lambda i: (i, 0))
    return pl.pallas_call(
        _prep_kernel,
        grid=(N_TOK // tb,),
        in_specs=[tok(D_MODEL), pl.BlockSpec((1, 1, tb), lambda i: (i, 0, 0)), full((1, D_MODEL)), full((1, D_MODEL)),
                  full((D_MODEL, IN_COLS)), full((1, GM_WIDTH)), full((1, GM_WIDTH)),
                  full((GM_HEADS, GM_CHUNK, GM_CHUNK)), full((GM_HEADS, GM_CHUNK, GM_HEAD_DIM)),
                  full((1, Q_LORA_RANK)), full((Q_LORA_RANK, D_MODEL)),
                  full((1, KV_LORA_RANK)), full((KV_LORA_RANK, D_MODEL)),
                  full((1, LANES)), full((1, LANES)), full((1, LANES))],
        out_specs=[tok(D_MODEL), tok(GM_WIDTH), tok(D_MODEL), tok(D_MODEL), tok(GM_WIDTH)],
        out_shape=[jax.ShapeDtypeStruct((N_TOK, D_MODEL), jnp.float32),
                   jax.ShapeDtypeStruct((N_TOK, GM_WIDTH), jnp.bfloat16),
                   jax.ShapeDtypeStruct((N_TOK, D_MODEL), jnp.bfloat16),
                   jax.ShapeDtypeStruct((N_TOK, D_MODEL), jnp.bfloat16),
                   jax.ShapeDtypeStruct((N_TOK, GM_WIDTH), jnp.bfloat16)],
        scratch_shapes=[pltpu.VMEM((tb, LANES), jnp.float32)] * 3,
        compiler_params=pltpu.CompilerParams(
            dimension_semantics=("arbitrary",), vmem_limit_bytes=VMEM_LIMIT),
        name="prep",
    )(x2, pos2, ln0g, ln0b, win, gmg, gmb, ws, bs, qg, wuq, kvg, wukv, freq, phase, sign)


def _attn_kernel(q_ref, k_ref, v_ref, o_ref, s_ref, mx_ref, ls_ref, acc_ref):
    qi = pl.program_id(1)
    tq = q_ref.shape[0]
    nt = (((1,), (1,)), ((), ()))
    n_kv = SEQ // ATT_K

    half = tq // 2
    row_chunk = (lax.broadcasted_iota(jnp.int32, (half, half), 0)) // CHUNK
    col_chunk = (lax.broadcasted_iota(jnp.int32, (half, half), 1)) // CHUNK
    quad_allowed = col_chunk <= row_chunk
    diag_start = pl.multiple_of(qi * ATT_K, ATT_K)
    masked = jnp.float32(-1e30)

    def lane_tiles(x):
        return [x[:, t * LANES:(t + 1) * LANES] for t in range(x.shape[1] // LANES)]

    def lane_fold(x, op):
        return functools.reduce(op, lane_tiles(x))

    def scores(hd, j):
        start = pl.multiple_of(j * ATT_K, ATT_K)
        q_h = q_ref[:, hd * 2 * LANES:(hd + 1) * 2 * LANES]
        kb = k_ref[pl.ds(start, ATT_K), hd * 2 * LANES:(hd + 1) * 2 * LANES]
        return lax.dot_general(q_h, kb, nt, preferred_element_type=jnp.float32)

    for hd in range(MLA_HEADS):
        cols = slice(hd * 2 * LANES, (hd + 1) * 2 * LANES)
        k_lo = k_ref[pl.ds(diag_start, half), cols]
        k_hi = k_ref[pl.ds(diag_start + half, half), cols]
        qk = lambda q, k: lax.dot_general(q, k, nt, preferred_element_type=jnp.float32)
        s_tl = jnp.where(quad_allowed, qk(q_ref[:half, cols], k_lo), masked)
        s_bl = qk(q_ref[half:, cols], k_lo)
        s_br = jnp.where(quad_allowed, qk(q_ref[half:, cols], k_hi), masked)
        diag = s_ref.at[hd * n_kv + qi]
        diag[:half, :half] = s_tl
        diag[half:, :half] = s_bl
        diag[half:, half:] = s_br
        mx_ref[hd, :half] = lane_fold(s_tl, jnp.maximum)
        mx_ref[hd, half:] = jnp.maximum(lane_fold(s_bl, jnp.maximum), lane_fold(s_br, jnp.maximum))

    def pass_a(j, c):
        for hd in range(MLA_HEADS):
            s = scores(hd, j)
            s_ref[hd * n_kv + j] = s
            mx_ref[hd] = jnp.maximum(mx_ref[hd], lane_fold(s, jnp.maximum))
        return c

    lax.fori_loop(0, qi, pass_a, 0)

    for hd in range(MLA_HEADS):
        mx_ref[hd] = jnp.broadcast_to(jnp.max(mx_ref[hd], axis=-1, keepdims=True), (tq, LANES))

    def probs(hd, j):
        s = s_ref[hd * n_kv + j]
        mb = mx_ref[hd]
        p = jnp.exp2(jnp.concatenate([t - mb for t in lane_tiles(s)], axis=-1))
        start = pl.multiple_of(j * ATT_K, ATT_K)
        vb = v_ref[pl.ds(start, ATT_K), hd * LANES:(hd + 1) * LANES]
        pv = jnp.dot(p.astype(jnp.bfloat16), vb, preferred_element_type=jnp.float32)
        return lane_fold(p, jnp.add), pv

    for hd in range(MLA_HEADS):
        diag = s_ref.at[hd * n_kv + qi]
        vcols = slice(hd * LANES, (hd + 1) * LANES)
        p_top = jnp.exp2(jnp.concatenate([t - mx_ref[hd, :half] for t in lane_tiles(diag[:half, :half])], axis=-1))
        p_bot = jnp.exp2(jnp.concatenate([t - mx_ref[hd, half:] for t in lane_tiles(diag[half:, :])], axis=-1))
        ls_ref[hd, :half] = lane_fold(p_top, jnp.add)
        ls_ref[hd, half:] = lane_fold(p_bot, jnp.add)
        acc_ref[hd, :half] = jnp.dot(p_top.astype(jnp.bfloat16), v_ref[pl.ds(diag_start, half), vcols],
                                     preferred_element_type=jnp.float32)
        acc_ref[hd, half:] = jnp.dot(p_bot.astype(jnp.bfloat16), v_ref[pl.ds(diag_start, ATT_K), vcols],
                                     preferred_element_type=jnp.float32)

    def pass_b(j, c):
        for hd in range(MLA_HEADS):
            ls, pv = probs(hd, j)
            ls_ref[hd] = ls_ref[hd] + ls
            acc_ref[hd] = acc_ref[hd] + pv
        return c

    lax.fori_loop(0, qi, pass_b, 0)

    for hd in range(MLA_HEADS):
        l = jnp.sum(ls_ref[hd], axis=-1, keepdims=True)
        o_ref[:, hd * V_HEAD_DIM:(hd + 1) * V_HEAD_DIM] = (acc_ref[hd] / l).astype(jnp.bfloat16)


def _attn(q, k, v):
    tq = ATT_Q
    tokblk = lambda cols: pl.BlockSpec((tq, cols), lambda b, i: (b * N_QBLK + i, 0))
    seqblk = lambda cols: pl.BlockSpec((SEQ, cols), lambda b, i: (b, 0))
    return pl.pallas_call(
        _attn_kernel,
        grid=(BATCH, N_QBLK),
        in_specs=[tokblk(D_MODEL), seqblk(D_MODEL), seqblk(GM_WIDTH)],
        out_specs=tokblk(GM_WIDTH),
        out_shape=jax.ShapeDtypeStruct((N_TOK, MLA_HEADS * V_HEAD_DIM), jnp.bfloat16),
        scratch_shapes=[pltpu.VMEM((MLA_HEADS * (SEQ // ATT_K), tq, ATT_K), jnp.float32),
                        pltpu.VMEM((MLA_HEADS, tq, LANES), jnp.float32),
                        pltpu.VMEM((MLA_HEADS, tq, LANES), jnp.float32),
                        pltpu.VMEM((MLA_HEADS, tq, V_HEAD_DIM), jnp.float32)],
        compiler_params=pltpu.CompilerParams(
            dimension_semantics=("arbitrary", "arbitrary"), vmem_limit_bytes=VMEM_LIMIT),
        name="attn",
    )(q, k, v)


def _proj_kernel(outa_ref, ob_ref, h_hbm_ref, wout_ref, ln1g_ref, ln1b_ref, wr_ref, br_ref,
                 h1_ref, h1b_ref, ri_ref, rf_ref, proj_ref, hring_ref, hsems):
    i = pl.program_id(0)
    tb = outa_ref.shape[0]
    h_ref = _ring_block(h_hbm_ref, hring_ref, hsems, lambda step: jnp.maximum(step - 1, 0), tb)

    @pl.when(i == 0)
    def _():
        proj_ref[...] = jnp.zeros_like(proj_ref)

    h1 = _layer_norm(ALPHA * h_ref[...] + proj_ref[(i + 1) % 2], ln1g_ref[...], ln1b_ref[...])
    h1_ref[...] = h1
    h1b_ref[...] = h1.astype(jnp.bfloat16)

    logits_tm = jnp.dot(h1b_ref[...], wr_ref[...], preferred_element_type=jnp.float32)
    logits = logits_tm.T[0:ROUTER_ROWS] + br_ref[...]

    sub_i = lax.broadcasted_iota(jnp.int32, (SUBLANES, tb), 0)
    sub = sub_i.astype(jnp.float32)
    neg = jnp.float32(-jnp.inf)
    g = jnp.where(sub_i < N_GROUPS, logits[0:SUBLANES], neg)
    gmax = jnp.max(g, axis=0, keepdims=True)
    g_top = jnp.min(jnp.where(g == gmax, sub, float(SUBLANES)), axis=0, keepdims=True)
    p_group = 1.0 / jnp.sum(jnp.exp(g - gmax), axis=0, keepdims=True)
    sel = logits[SUBLANES:2 * SUBLANES]
    for grp in range(1, N_GROUPS):
        sel = jnp.where(g_top == float(grp), logits[(grp + 1) * SUBLANES:(grp + 2) * SUBLANES], sel)
    v1 = jnp.max(sel, axis=0, keepdims=True)
    i1 = jnp.min(jnp.where(sel == v1, sub, float(SUBLANES)), axis=0, keepdims=True)
    sel2 = jnp.where(sub == i1, neg, sel)
    v2 = jnp.max(sel2, axis=0, keepdims=True)
    i2 = jnp.min(jnp.where(sel2 == v2, sub, float(SUBLANES)), axis=0, keepdims=True)
    e21 = jnp.exp(v2 - v1)
    w1 = 1.0 / (1.0 + e21)
    gate1 = p_group * w1
    gate2 = p_group * (e21 * w1)
    e1 = g_top * EXPERTS_PER_GROUP + i1
    e2 = g_top * EXPERTS_PER_GROUP + i2
    ri_ref[...] = jnp.where(sub_i == 0, e1, jnp.where(sub_i == 1, e2, 0.0)).astype(jnp.int32)
    rf_ref[...] = jnp.where(sub_i == 0, gate1, jnp.where(sub_i == 1, gate2, 0.0))

    proj_ref[i % 2] = (jnp.dot(outa_ref[...], wout_ref[:GM_WIDTH, :], preferred_element_type=jnp.float32)
                       + jnp.dot(ob_ref[...], wout_ref[GM_WIDTH:, :], preferred_element_type=jnp.float32))


def _proj(outa, ob, h, wout, ln1g, ln1b, wr, br):
    tb = PROJ_TOKENS
    n_blk = N_TOK // tb
    cur = lambda i: jnp.minimum(i, n_blk - 1)
    prev = lambda i: jnp.maximum(i - 1, 0)
    full = lambda shape: pl.BlockSpec(shape, lambda i: (0,) * len(shape))
    return pl.pallas_call(
        _proj_kernel,
        grid=(n_blk + 1,),
        in_specs=[pl.BlockSpec((tb, GM_WIDTH), lambda i: (cur(i), 0)),
                  pl.BlockSpec((tb, GM_WIDTH), lambda i: (cur(i), 0)),
                  pl.BlockSpec(memory_space=pl.ANY),
                  full((D_MODEL, D_MODEL)), full((1, D_MODEL)), full((1, D_MODEL)),
                  full((D_MODEL, LANES)), full((ROUTER_ROWS, 1))],
        out_specs=[pl.BlockSpec((tb, D_MODEL), lambda i: (prev(i), 0)),
                   pl.BlockSpec((tb, D_MODEL), lambda i: (prev(i), 0)),
                   pl.BlockSpec((SUBLANES, tb), lambda i: (0, prev(i))),
                   pl.BlockSpec((SUBLANES, tb), lambda i: (0, prev(i)))],
        out_shape=[jax.ShapeDtypeStruct((N_TOK, D_MODEL), jnp.float32),
                   jax.ShapeDtypeStruct((N_TOK, D_MODEL), jnp.bfloat16),
                   jax.ShapeDtypeStruct((SUBLANES, N_TOK), jnp.int32),
                   jax.ShapeDtypeStruct((SUBLANES, N_TOK), jnp.float32)],
        scratch_shapes=[pltpu.VMEM((2, tb, D_MODEL), jnp.float32)] + _ring_scratch(tb, D_MODEL, jnp.float32),
        compiler_params=pltpu.CompilerParams(
            dimension_semantics=("arbitrary",), vmem_limit_bytes=VMEM_LIMIT),
        name="proj",
    )(outa, ob, h, wout, ln1g, ln1b, wr, br)


def _plan_kernel(ri_all_ref, ri_ref, ldest_ref, runs_ref, meta_ref, run_ref, start_ref, upper_ref):
    step = pl.program_id(0)
    tb = ri_ref.shape[1]
    f32 = jnp.float32
    er = lax.broadcasted_iota(jnp.int32, (N_EXPERTS, LANES), 0)
    ec = lax.broadcasted_iota(jnp.int32, (N_EXPERTS, LANES), 1)
    to_row = lambda col: jnp.sum(jnp.where(er == ec, col, 0.0), axis=0, keepdims=True)

    def expert_one_hot(ref):
        e_sub = lax.broadcasted_iota(jnp.int32, (N_EXPERTS, ref.shape[1]), 0)
        return e_sub == ref[0:1, :], e_sub == ref[1:2, :]

    @pl.when(step == 0)
    def _():
        oh1, oh2 = expert_one_hot(ri_all_ref)
        counts = jnp.sum(jnp.where(oh1 | oh2, 1.0, 0.0), axis=1, keepdims=True)
        padded = jnp.floor((counts + (EXPERT_ROWS - 1)) * (1.0 / EXPERT_ROWS)) * EXPERT_ROWS
        pad_end = jnp.sum(jnp.where(ec <= er, to_row(padded), 0.0), axis=1, keepdims=True)
        start_ref[...] = jnp.broadcast_to(pad_end - padded, start_ref.shape)
        run_ref[...] = jnp.zeros_like(run_ref)
        bstart = (lax.broadcasted_iota(jnp.int32, (N_EXPERTS, META_LANES), 1) * EXPERT_ROWS).astype(f32)
        blk_e = jnp.sum(jnp.where(pad_end <= bstart, 1.0, 0.0), axis=0, keepdims=True)
        blk_e = jnp.minimum(blk_e, N_EXPERTS - 1.0)
        n_used = pad_end[N_EXPERTS - 1:N_EXPERTS, :] * (1.0 / EXPERT_ROWS)
        pad3 = lambda r: jnp.concatenate(
            [r, jnp.zeros((1, META_LANES - LANES), f32)], axis=1)
        msub = lax.broadcasted_iota(jnp.int32, (SUBLANES, META_LANES), 0)
        meta = jnp.where(msub == 0, blk_e,
                         jnp.where(msub == 1, pad3(to_row(pad_end)),
                                   jnp.where(msub == 2, pad3(to_row(counts)),
                                             jnp.where(msub == 3, n_used, 0.0))))
        meta_ref[...] = meta.astype(jnp.int32)
        tr = lax.broadcasted_iota(jnp.int32, (tb, tb), 0)
        tc = lax.broadcasted_iota(jnp.int32, (tb, tb), 1)
        upper_ref[...] = jnp.where(tr < tc, 1.0, 0.0).astype(jnp.bfloat16)

    @pl.when(step > 0)
    def _():
        oh1, oh2 = expert_one_hot(ri_ref)
        oh = jnp.where(oh1 | oh2, 1.0, 0.0).astype(f32)
        blk_count = jnp.sum(oh, axis=1, keepdims=True)
        prefix = jnp.dot(oh.astype(jnp.bfloat16), upper_ref[...], preferred_element_type=f32)
        cnt_row = to_row(blk_count)
        lstart = jnp.sum(jnp.where(ec < er, cnt_row, 0.0), axis=1, keepdims=True)
        base = prefix + lstart
        d1 = jnp.sum(jnp.where(oh1, base, 0.0), axis=0, keepdims=True)
        d2 = jnp.sum(jnp.where(oh2, base, 0.0), axis=0, keepdims=True)
        sub = lax.broadcasted_iota(jnp.int32, (SUBLANES, tb), 0)
        ldest_ref[...] = jnp.where(sub == 0, d1, jnp.where(sub == 1, d2, 0.0)).astype(jnp.int32)
        gstart = start_ref[:, 0:1] + run_ref[:, 0:1]
        rsub = lax.broadcasted_iota(jnp.int32, (SUBLANES, LANES), 0)
        runs = jnp.where(rsub == 0, cnt_row,
                         jnp.where(rsub == 1, to_row(lstart * FEAT_TILES),
                                   jnp.where(rsub == 2, to_row(gstart * FEAT_TILES), 0.0)))
        runs_ref[...] = runs.astype(jnp.int32)
        run_ref[...] = run_ref[...] + blk_count


def _plan(ri):
    tb = MOE_TOKENS
    blk = lambda i: jnp.maximum(i - 1, 0)
    return pl.pallas_call(
        _plan_kernel,
        grid=(N_MOE_BLOCKS + 1,),
        in_specs=[pl.BlockSpec((SUBLANES, N_TOK), lambda i: (0, 0)),
                  pl.BlockSpec((SUBLANES, tb), lambda i: (0, blk(i)))],
        out_specs=[pl.BlockSpec((SUBLANES, tb), lambda i: (0, blk(i))),
                   pl.BlockSpec((SUBLANES, LANES), lambda i: (blk(i), 0)),
                   pl.BlockSpec((SUBLANES, META_LANES), lambda i: (0, 0))],
        out_shape=[jax.ShapeDtypeStruct((SUBLANES, N_TOK), jnp.int32),
                   jax.ShapeDtypeStruct((N_MOE_BLOCKS * SUBLANES, LANES), jnp.int32),
                   jax.ShapeDtypeStruct((SUBLANES, META_LANES), jnp.int32)],
        scratch_shapes=[pltpu.VMEM((N_EXPERTS, LANES), jnp.float32),
                        pltpu.VMEM((N_EXPERTS, LANES), jnp.float32),
                        pltpu.VMEM((tb, tb), jnp.bfloat16)],
        compiler_params=pltpu.CompilerParams(
            dimension_semantics=("arbitrary",), vmem_limit_bytes=VMEM_LIMIT),
        name="plan",
    )(ri, ri)


def _for_each_run_piece(runs_ref, row0, fn, live=True):
    for e in range(N_EXPERTS):
        n_tiles = jnp.where(live, runs_ref[row0, e], 0) * FEAT_TILES
        lrow, grow = runs_ref[row0 + 1, e], runs_ref[row0 + 2, e]
        for bit in reversed(range(RUN_BITS)):
            piece = n_tiles & (FEAT_TILES << bit)

            @pl.when(piece != 0)
            def _(lrow=lrow, grow=grow, bit=bit):
                fn(lrow, grow, 1 << bit)

            lrow, grow = lrow + piece, grow + piece


def _tile_rows(ref, tile_row, rows):
    return ref.at[pl.ds(pl.multiple_of(tile_row, FEAT_TILES), rows * FEAT_TILES)]


def _dispatch_kernel(meta_ref, runs_ref, ldest_ref, h1b_ref, buf_ref, *scratch):
    sorted_refs = scratch[:MOE_GROUP]
    zero_ref, sems, zsem = scratch[MOE_GROUP:]
    i = pl.program_id(0)
    n_steps = pl.num_programs(0)
    tb = MOE_TOKENS
    slot = i % 2
    block_tiles = 2 * tb * FEAT_TILES

    def wait_slot(g, s):
        pltpu.make_async_copy(sorted_refs[g].at[s], buf_ref.at[pl.ds(0, block_tiles)], sems.at[g, s]).wait()

    @pl.when(i == 0)
    def _():
        zero_ref[...] = jnp.zeros_like(zero_ref)

        def zero_copy(e):
            start = pl.multiple_of((meta_ref[1, e] - EXPERT_ROWS) * FEAT_TILES, EXPERT_ROWS * FEAT_TILES)
            return pltpu.make_async_copy(
                zero_ref, buf_ref.at[pl.ds(start, EXPERT_ROWS * FEAT_TILES)], zsem)

        def start_zero(e, c):
            @pl.when(meta_ref[2, e] > 0)
            def _():
                zero_copy(e).start()
            return c

        def wait_zero(e, c):
            @pl.when(meta_ref[2, e] > 0)
            def _():
                zero_copy(e).wait()
            return c

        def tail_copy(b):
            start = pl.multiple_of(b * (EXPERT_ROWS * FEAT_TILES), EXPERT_ROWS * FEAT_TILES)
            return pltpu.make_async_copy(
                zero_ref, buf_ref.at[pl.ds(start, EXPERT_ROWS * FEAT_TILES)], zsem)

        def start_tail(b, c):
            tail_copy(b).start()
            return c

        def wait_tail(b, c):
            tail_copy(b).wait()
            return c

        lax.fori_loop(0, N_EXPERTS, start_zero, 0)
        lax.fori_loop(meta_ref[3, 0], N_ROW_BLOCKS, start_tail, 0)
        lax.fori_loop(0, N_EXPERTS, wait_zero, 0)
        lax.fori_loop(meta_ref[3, 0], N_ROW_BLOCKS, wait_tail, 0)

    @pl.when(i >= 2)
    def _():
        for g in range(MOE_GROUP):
            wait_slot(g, slot)

    for g in range(MOE_GROUP):
        x = h1b_ref[g * tb:(g + 1) * tb, :]
        ld0 = ldest_ref[0:1, g * tb:(g + 1) * tb]
        ld1 = ldest_ref[1:2, g * tb:(g + 1) * tb]
        for c in range(2 * tb // SORT_CHUNK):
            r = lax.broadcasted_iota(jnp.int32, (SORT_CHUNK, tb), 0) + c * SORT_CHUNK
            perm = jnp.where((r == ld0) | (r == ld1), 1.0, 0.0).astype(jnp.bfloat16)
            rows = jnp.dot(perm, x, preferred_element_type=jnp.float32)
            _to_row_tiles(sorted_refs[g].at[slot, pl.ds(c * SORT_CHUNK * FEAT_TILES, SORT_CHUNK * FEAT_TILES)],
                          rows)

        def send(lrow, grow, rows, g=g):
            pltpu.make_async_copy(_tile_rows(sorted_refs[g].at[slot], lrow, rows),
                                  _tile_rows(buf_ref, grow, rows), sems.at[g, slot]).start()

        _for_each_run_piece(runs_ref, g * SUBLANES, send)

    @pl.when(i == n_steps - 1)
    def _():
        for g in range(MOE_GROUP):
            wait_slot(g, slot)
            wait_slot(g, 1 - slot)


def _dispatch(meta, runs, ldest, h1b):
    tb = MOE_GROUP * MOE_TOKENS
    return pl.pallas_call(
        _dispatch_kernel,
        grid_spec=pltpu.PrefetchScalarGridSpec(
            num_scalar_prefetch=1,
            grid=(N_MOE_BLOCKS // MOE_GROUP,),
            in_specs=[pl.BlockSpec((MOE_GROUP * SUBLANES, LANES), lambda i, m: (i, 0), memory_space=pltpu.SMEM),
                      pl.BlockSpec((SUBLANES, tb), lambda i, m: (0, i)),
                      pl.BlockSpec((tb, D_MODEL), lambda i, m: (i, 0))],
            out_specs=pl.BlockSpec(memory_space=pl.ANY),
            scratch_shapes=[pltpu.VMEM((2, 2 * MOE_TOKENS * FEAT_TILES, LANES), jnp.float32)] * MOE_GROUP + [
                pltpu.VMEM((EXPERT_ROWS * FEAT_TILES, LANES), jnp.float32),
                pltpu.SemaphoreType.DMA((MOE_GROUP, 2)), pltpu.SemaphoreType.DMA]),
        out_shape=jax.ShapeDtypeStruct((N_ROWS * FEAT_TILES, LANES), jnp.float32),
        compiler_params=pltpu.CompilerParams(
            dimension_semantics=("arbitrary",), vmem_limit_bytes=VMEM_LIMIT),
        name="dispatch",
    )(meta, runs, ldest, h1b)


def _sub_block_expert(meta, i, sub):
    return meta[0, jnp.minimum(i + sub * (N_ROW_BLOCKS // EXPERT_SUB), meta[3, 0] - 1)]


def _experts_kernel(meta_ref, x_hbm_ref, *refs):
    weights = [refs[3 * sub:3 * sub + 3] for sub in range(EXPERT_SUB)]
    o_ref, wgb_ref, wub_ref, wdb_ref, cached_ref, xring_ref, xsems = refs[3 * EXPERT_SUB:]
    n = EXPERT_ROWS
    i = pl.program_id(0)
    n_steps = pl.num_programs(0)

    def rows_copy(step):
        slot = step % EXPERT_RING
        start = pl.multiple_of(step * (n * FEAT_TILES), n * FEAT_TILES)
        return pltpu.make_async_copy(x_hbm_ref.at[:, pl.ds(start, n * FEAT_TILES), :], xring_ref.at[slot],
                                     xsems.at[slot])

    @pl.when(i == 0)
    def _():
        for step in range(EXPERT_RING - 1):
            rows_copy(step).start()

    @pl.when(i + (EXPERT_RING - 1) < n_steps)
    def _():
        rows_copy(i + (EXPERT_RING - 1)).start()

    rows_copy(i).wait()
    x_ref = xring_ref.at[i % EXPERT_RING]

    @pl.when(i == 0)
    def _():
        for sub in range(EXPERT_SUB):
            cached_ref[sub] = -1

    for sub in range(EXPERT_SUB):
        expert = _sub_block_expert(meta_ref, i, sub)

        @pl.when(cached_ref[sub] != expert)
        def _(sub=sub, expert=expert):
            wg_ref, wu_ref, wd_ref = weights[sub]
            wgb_ref[sub] = wg_ref[...].astype(jnp.bfloat16)
            wub_ref[sub] = wu_ref[...].astype(jnp.bfloat16)
            wdb_ref[sub] = wd_ref[...].astype(jnp.bfloat16)
            cached_ref[sub] = expert

    for sub in range(EXPERT_SUB):
        x = _from_row_tiles(x_ref.at[sub], n).astype(jnp.bfloat16)
        gate = jnp.dot(x, wgb_ref[sub], preferred_element_type=jnp.float32)
        up = jnp.dot(x, wub_ref[sub], preferred_element_type=jnp.float32)
        act = (gate * jax.nn.sigmoid(gate) * up).astype(jnp.bfloat16)
        _to_row_tiles(o_ref.at[sub], jnp.dot(act, wdb_ref[sub], preferred_element_type=jnp.float32))


def _experts(meta, buf, wg, wu, wd):
    sub_tiles = (N_ROW_BLOCKS // EXPERT_SUB) * EXPERT_ROWS * FEAT_TILES

    def weight_spec(shape, sub):
        return pl.BlockSpec((None,) + shape, lambda i, m: (_sub_block_expert(m, i, sub), 0, 0))

    rows_shape = (EXPERT_SUB, EXPERT_ROWS * FEAT_TILES, LANES)
    up_shape, down_shape = (D_MODEL, EXPERT_FF), (EXPERT_FF, D_MODEL)
    eout = pl.pallas_call(
        _experts_kernel,
        grid_spec=pltpu.PrefetchScalarGridSpec(
            num_scalar_prefetch=1,
            grid=(N_ROW_BLOCKS // EXPERT_SUB,),
            in_specs=[pl.BlockSpec(memory_space=pl.ANY)] + [
                weight_spec(shape, sub) for sub in range(EXPERT_SUB) for shape in (up_shape, up_shape, down_shape)],
            out_specs=pl.BlockSpec(rows_shape, lambda i, m: (0, i, 0)),
            scratch_shapes=[pltpu.VMEM((EXPERT_SUB,) + up_shape, jnp.bfloat16),
                            pltpu.VMEM((EXPERT_SUB,) + up_shape, jnp.bfloat16),
                            pltpu.VMEM((EXPERT_SUB,) + down_shape, jnp.bfloat16),
                            pltpu.SMEM((EXPERT_SUB,), jnp.int32),
                            pltpu.VMEM((EXPERT_RING,) + rows_shape, jnp.float32),
                            pltpu.SemaphoreType.DMA((EXPERT_RING,))]),
        out_shape=jax.ShapeDtypeStruct((EXPERT_SUB, sub_tiles, LANES), jnp.float32),
        compiler_params=pltpu.CompilerParams(
            dimension_semantics=("arbitrary",), vmem_limit_bytes=EXPERT_VMEM_LIMIT),
        name="experts",
    )(meta, buf.reshape(EXPERT_SUB, sub_tiles, LANES), *([wg, wu, wd] * EXPERT_SUB))
    return eout.reshape(N_ROWS * FEAT_TILES, LANES)


def _combine_kernel(runs_ref, runs_next_ref, ldest_ref, rf_ref, h1_ref, eout_ref, ln2g_ref, ln2b_ref,
                    o_ref, *scratch):
    y_refs = scratch[:MOE_GROUP]
    sems = scratch[MOE_GROUP]
    i = pl.program_id(0)
    n_steps = pl.num_programs(0)
    tb = MOE_TOKENS
    slot = i % 2
    block_tiles = 2 * tb * FEAT_TILES

    def fetch(table_ref, g, s, live):
        def recv(lrow, grow, rows):
            pltpu.make_async_copy(_tile_rows(eout_ref, grow, rows),
                                  _tile_rows(y_refs[g].at[s], lrow, rows), sems.at[g, s]).start()
        _for_each_run_piece(table_ref, g * SUBLANES, recv, live)

    @pl.when(i == 0)
    def _():
        for g in range(MOE_GROUP):
            fetch(runs_ref, g, slot, True)

    for g in range(MOE_GROUP):
        pltpu.make_async_copy(eout_ref.at[pl.ds(0, block_tiles)], y_refs[g].at[slot], sems.at[g, slot]).wait()
        fetch(runs_next_ref, g, 1 - slot, i + 1 < n_steps)

        ld = ldest_ref[:, g * tb:(g + 1) * tb].astype(jnp.float32).T
        gates = rf_ref[:, g * tb:(g + 1) * tb].T
        y = None
        col = lax.broadcasted_iota(jnp.int32, (tb, SORT_CHUNK), 1).astype(jnp.float32)
        for c in range(2 * tb // SORT_CHUNK):
            ld_c = ld - float(c * SORT_CHUNK)
            gm = jnp.where(col == ld_c[:, 0:1], gates[:, 0:1],
                           jnp.where(col == ld_c[:, 1:2], gates[:, 1:2], 0.0)).astype(jnp.bfloat16)
            rows = _from_row_tiles(
                y_refs[g].at[slot, pl.ds(c * SORT_CHUNK * FEAT_TILES, SORT_CHUNK * FEAT_TILES)], SORT_CHUNK)
            part = jnp.dot(gm, rows.astype(jnp.bfloat16), preferred_element_type=jnp.float32)
            y = part if y is None else y + part
        tok = slice(g * tb, (g + 1) * tb)
        o_ref[tok, :] = _layer_norm(ALPHA * h1_ref[tok, :] + y, ln2g_ref[...], ln2b_ref[...])


def _combine(runs, ldest, rf, h1, eout, ln2g, ln2b):
    tb = MOE_GROUP * MOE_TOKENS
    n_steps = N_MOE_BLOCKS // MOE_GROUP
    table = lambda index_map: pl.BlockSpec((MOE_GROUP * SUBLANES, LANES), index_map, memory_space=pltpu.SMEM)
    return pl.pallas_call(
        _combine_kernel,
        grid=(n_steps,),
        in_specs=[table(lambda i: (i, 0)), table(lambda i: (jnp.minimum(i + 1, n_steps - 1), 0)),
                  pl.BlockSpec((SUBLANES, tb), lambda i: (0, i)),
                  pl.BlockSpec((SUBLANES, tb), lambda i: (0, i)),
                  pl.BlockSpec((tb, D_MODEL), lambda i: (i, 0)),
                  pl.BlockSpec(memory_space=pl.ANY),
                  pl.BlockSpec((1, D_MODEL), lambda i: (0, 0)),
                  pl.BlockSpec((1, D_MODEL), lambda i: (0, 0))],
        out_specs=pl.BlockSpec((tb, D_MODEL), lambda i: (i, 0)),
        out_shape=jax.ShapeDtypeStruct((N_TOK, D_MODEL), jnp.float32),
        scratch_shapes=[pltpu.VMEM((2, 2 * MOE_TOKENS * FEAT_TILES, LANES), jnp.float32)] * MOE_GROUP + [
            pltpu.SemaphoreType.DMA((MOE_GROUP, 2))],
        compiler_params=pltpu.CompilerParams(
            dimension_semantics=("arbitrary",), vmem_limit_bytes=VMEM_LIMIT),
        name="combine",
    )(runs, runs, ldest, rf, h1, eout, ln2g, ln2b)


def _swap_halves(w):
    half = w.shape[-1] // 2
    return jnp.concatenate([w[..., half:], w[..., :half]], axis=-1)


def kernel(x, positions, ln0_g, ln0_b, w_in, gm_ln_g, gm_ln_b, w_spatial, b_spatial, q_norm_g, w_uq, kv_norm_g, w_ukv, w_out, ln1_g, ln1_b, w_router_group, b_router_group, w_router_expert, b_router_expert, w_gate, w_up, w_down, ln2_g, ln2_b):
    bf16 = jnp.bfloat16
    row = lambda a: a.reshape(1, -1)

    w_in0 = w_in[0]
    kr_cols = w_in0[:, O_KR:O_KR + QK_ROPE_DIM]
    win = jnp.concatenate([w_in0, _swap_halves(kr_cols)], axis=1).astype(bf16)
    wuq3 = w_uq[0].reshape(Q_LORA_RANK, MLA_HEADS, QK_NOPE_DIM + QK_ROPE_DIM)
    rope_cols = wuq3[:, :, QK_NOPE_DIM:]
    wuq = jnp.concatenate([wuq3, _swap_halves(rope_cols)], axis=-1).reshape(Q_LORA_RANK, D_MODEL).astype(bf16)
    wukv = w_ukv[0].astype(bf16)
    wout = w_out[0].astype(bf16)
    bs = jnp.broadcast_to(b_spatial[0][:, :, None], (GM_HEADS, GM_CHUNK, GM_HEAD_DIM))
    wr = jnp.concatenate([w_router_group[0], jnp.zeros((D_MODEL, SUBLANES - N_GROUPS), jnp.float32),
                          w_router_expert[0],
                          jnp.zeros((D_MODEL, LANES - ROUTER_ROWS), jnp.float32)],
                         axis=1).astype(bf16)
    br = jnp.concatenate([b_router_group[0], jnp.zeros((SUBLANES - N_GROUPS,), jnp.float32),
                          b_router_expert[0]]).reshape(ROUTER_ROWS, 1)

    inv_freq = ROPE_THETA ** (-jnp.arange(0, QK_ROPE_DIM, 2, dtype=jnp.float32) / QK_ROPE_DIM)
    freq = jnp.tile(inv_freq, 4).reshape(1, LANES)
    quarter = QK_ROPE_DIM // 2
    phase = jnp.concatenate([jnp.zeros((2 * quarter,), jnp.float32),
                             jnp.full((2 * quarter,), math.pi / 2, jnp.float32)]).reshape(1, LANES)
    sign = jnp.concatenate([jnp.ones((2 * quarter,), jnp.float32), -jnp.ones((quarter,), jnp.float32),
                            jnp.ones((quarter,), jnp.float32)]).reshape(1, LANES)

    x2 = x.reshape(N_TOK, D_MODEL)
    pos2 = positions.reshape(N_TOK // PREP_TOKENS, 1, PREP_TOKENS)

    h, outa, q, k, v = _prep(x2, pos2, row(ln0_g), row(ln0_b), win, row(gm_ln_g[0]), row(gm_ln_b[0]),
                             w_spatial[0], bs, row(q_norm_g[0]), wuq, row(kv_norm_g[0]), wukv,
                             freq, phase, sign)
    ob = _attn(q, k, v)
    h1, h1b, ri, rf = _proj(outa, ob, h, wout, row(ln1_g[0]), row(ln1_b[0]), wr, br)
    ldest, runs, meta = _plan(ri)
    buf = _dispatch(meta, runs, ldest, h1b)
    eout = _experts(meta, buf, w_gate[0], w_up[0], w_down[0])
    out = _combine(runs, ldest, rf, h1, eout, row(ln2_g[0]), row(ln2_b[0]))
    return out.reshape(BATCH, SEQ, D_MODEL)
```

```python
import functools
import math

import jax
import jax.numpy as jnp
from jax import lax
from jax.experimental import pallas as pl
from jax.experimental.pallas import tpu as pltpu

D_MODEL = 1024
BATCH = 16
SEQ = 2048
N_TOK = BATCH * SEQ
CHUNK = 64
GM_WIDTH = 512
GM_HEADS = 4
GM_HEAD_DIM = 128
GM_CHUNK = 128
MLA_HEADS = 4
QK_NOPE_DIM = 128
QK_ROPE_DIM = 64
V_HEAD_DIM = 128
Q_LORA_RANK = 384
KV_LORA_RANK = 256
ROPE_THETA = 10000.0
N_GROUPS = 4
EXPERTS_PER_GROUP = 8
N_EXPERTS = 32
TOP_K = 2
EXPERT_FF = 256
ALPHA = 2.0 ** 0.25
QK_SCALE = (QK_NOPE_DIM + QK_ROPE_DIM) ** -0.5 * math.log2(math.e)

LANES = 128
SUBLANES = 8
FEAT_TILES = D_MODEL // LANES
PREP_TOKENS = 1024
ATT_Q = 512
ATT_K = 512
N_QBLK = SEQ // ATT_Q
PROJ_TOKENS = 1024
MOE_TOKENS = 512
N_MOE_BLOCKS = N_TOK // MOE_TOKENS
MOE_GROUP = 2
RUN_BITS = (TOP_K * MOE_TOKENS).bit_length()
SORT_CHUNK = 256
EXPERT_ROWS = 256
EXPERT_SUB = 4
EXPERT_RING = 3
INPUT_RING = 3
N_ROWS = N_TOK * TOP_K + N_EXPERTS * EXPERT_ROWS
N_ROW_BLOCKS = N_ROWS // EXPERT_ROWS
META_LANES = 384
IN_COLS = 2 * GM_WIDTH + Q_LORA_RANK + KV_LORA_RANK + 2 * QK_ROPE_DIM
O_Q = 2 * GM_WIDTH
O_KV = O_Q + Q_LORA_RANK
O_KR = O_KV + KV_LORA_RANK
ROUTER_ROWS = 40
VMEM_LIMIT = 48 * 1024 * 1024
EXPERT_VMEM_LIMIT = 56 * 1024 * 1024

assert N_ROW_BLOCKS <= META_LANES and N_ROW_BLOCKS % EXPERT_SUB == 0


def _layer_norm(x, g, b, eps=1e-5):
    mu = jnp.mean(x, axis=-1, keepdims=True)
    xc = x - mu
    var = jnp.mean(xc * xc, axis=-1, keepdims=True)
    return xc * lax.rsqrt(var + eps) * g + b


def _rms_norm(x, g, eps=1e-6):
    return x * lax.rsqrt(jnp.mean(x * x, axis=-1, keepdims=True) + eps) * g


def _gelu_tanh(x):
    c = math.sqrt(2.0 / math.pi)
    return 0.5 * x * (1.0 + jnp.tanh(c * (x + 0.044715 * (x * x * x))))


def _to_row_tiles(ref, x):
    n = x.shape[0]
    for s in range(FEAT_TILES):
        ref[pl.ds(s, n, stride=FEAT_TILES), :] = x[:, s * LANES:(s + 1) * LANES]


def _from_row_tiles(ref, n):
    return jnp.concatenate(
        [ref[pl.ds(s, n, stride=FEAT_TILES), :] for s in range(FEAT_TILES)], axis=-1)


def _ring_block(hbm_ref, ring_ref, sems, block_of_step, rows):
    i = pl.program_id(0)
    n_steps = pl.num_programs(0)

    def copy(step):
        start = pl.multiple_of(block_of_step(step) * rows, rows)
        slot = step % INPUT_RING
        return pltpu.make_async_copy(hbm_ref.at[pl.ds(start, rows)], ring_ref.at[slot], sems.at[slot])

    @pl.when(i == 0)
    def _():
        for step in range(INPUT_RING - 1):
            copy(step).start()

    @pl.when(i + (INPUT_RING - 1) < n_steps)
    def _():
        copy(i + (INPUT_RING - 1)).start()

    copy(i).wait()
    return ring_ref.at[i % INPUT_RING]


def _ring_scratch(rows, cols, dtype):
    return [pltpu.VMEM((INPUT_RING, rows, cols), dtype), pltpu.SemaphoreType.DMA((INPUT_RING,))]


def _prep_kernel(x_ref, pos_ref, ln0g_ref, ln0b_ref, win_ref, gmg_ref, gmb_ref, ws_ref, bs_ref,
                 qg_ref, wuq_ref, kvg_ref, wukv_ref, freq_ref, phase_ref, sign_ref,
                 h_ref, outa_ref, q_ref, k_ref, v_ref, tabc_ref, tabs_ref, rot_ref):
    tb = x_ref.shape[0]

    @pl.when(pl.program_id(0) == 0)
    def _():
        d = lax.broadcasted_iota(jnp.int32, (tb, LANES), 0).astype(jnp.float32) * freq_ref[...]
        tabc_ref[...] = jnp.cos(d)
        tabs_ref[...] = jnp.sin(d)

    pos_row = pos_ref[0]
    p0 = pos_row[:, 0:1]
    offset = lax.broadcasted_iota(jnp.int32, (1, tb), 1)
    consecutive = jnp.max(jnp.abs((pos_row - p0 - offset).astype(jnp.float32))) == 0.0

    @pl.when(consecutive)
    def _():
        a0 = p0.astype(jnp.float32) * freq_ref[...]
        c0, s0 = jnp.cos(a0), jnp.sin(a0)
        lane = lax.broadcasted_iota(jnp.int32, (1, LANES), 1)
        n_freq = QK_ROPE_DIM // 2
        coef_c = jnp.where(lane < 2 * n_freq, c0, jnp.where(lane < 3 * n_freq, -s0, s0))
        coef_s = jnp.where(lane < 2 * n_freq, -s0, jnp.where(lane < 3 * n_freq, -c0, c0))
        rot_ref[...] = coef_c * tabc_ref[...] + coef_s * tabs_ref[...]

    @pl.when(jnp.logical_not(consecutive))
    def _():
        pos_col = jnp.broadcast_to(pos_row.astype(jnp.float32), (SUBLANES, tb)).T[:, 0:1]
        ang = pos_col * freq_ref[...]
        rot_ref[...] = jnp.cos(ang - phase_ref[...]) * sign_ref[...]

    rot = rot_ref[...]

    h = _layer_norm(x_ref[...], ln0g_ref[...], ln0b_ref[...])
    h_ref[...] = h
    z = jnp.dot(h.astype(jnp.bfloat16), win_ref[...], preferred_element_type=jnp.float32)

    u = _gelu_tanh(z[:, :GM_WIDTH])
    v = _gelu_tanh(z[:, GM_WIDTH:2 * GM_WIDTH])
    row_chunk = lax.broadcasted_iota(jnp.int32, (GM_CHUNK, GM_CHUNK), 0) // CHUNK
    col_chunk = lax.broadcasted_iota(jnp.int32, (GM_CHUNK, GM_CHUNK), 1) // CHUNK
    allowed = col_chunk <= row_chunk
    for hd in range(GM_HEADS):
        lo, hi = hd * GM_HEAD_DIM, (hd + 1) * GM_HEAD_DIM
        vln = _layer_norm(v[:, lo:hi], gmg_ref[:, lo:hi], gmb_ref[:, lo:hi]).astype(jnp.bfloat16)
        wm = jnp.where(allowed, ws_ref[hd], 0.0).astype(jnp.bfloat16)
        for c in range(tb // GM_CHUNK):
            r0, r1 = c * GM_CHUNK, (c + 1) * GM_CHUNK
            f = jnp.dot(wm, vln[r0:r1], preferred_element_type=jnp.float32) + bs_ref[hd]
            outa_ref[r0:r1, lo:hi] = (u[r0:r1, lo:hi] * f).astype(jnp.bfloat16)

    ql = _rms_norm(z[:, O_Q:O_KV], qg_ref[...]).astype(jnp.bfloat16)
    qf = jnp.dot(ql, wuq_ref[...], preferred_element_type=jnp.float32)
    rot_s = rot * QK_SCALE
    q_parts = []
    for hd in range(MLA_HEADS):
        base = hd * 2 * LANES
        q_parts.append(qf[:, base:base + LANES] * QK_SCALE)
        q_parts.append(qf[:, base + LANES:base + 2 * LANES] * rot_s)
    q_ref[...] = jnp.concatenate(q_parts, axis=-1).astype(jnp.bfloat16)

    kvl = _rms_norm(z[:, O_KV:O_KR], kvg_ref[...]).astype(jnp.bfloat16)
    kv = jnp.dot(kvl, wukv_ref[...], preferred_element_type=jnp.float32)
    t = z[:, O_KR:O_KR + LANES] * rot
    krr = t + pltpu.roll(t, 2 * QK_ROPE_DIM // 2, axis=1)
    k_parts, v_parts = [], []
    for hd in range(MLA_HEADS):
        base = hd * 2 * LANES
        k_parts.append(kv[:, base:base + LANES])
        k_parts.append(krr)
        v_parts.append(kv[:, base + LANES:base + 2 * LANES])
    k_ref[...] = jnp.concatenate(k_parts, axis=-1).astype(jnp.bfloat16)
    v_ref[...] = jnp.concatenate(v_parts, axis=-1).astype(jnp.bfloat16)


def _prep(x2, pos2, ln0g, ln0b, win, gmg, gmb, ws, bs, qg, wuq, kvg, wukv, freq, phase, sign):
    tb = PREP_TOKENS
    full = lambda shape: pl.BlockSpec(shape, lambda i: (0,) * len(shape))
    tok = lambda cols: pl.BlockSpec((tb, cols), lambda i: (i, 0))
    return pl.pallas_call(
        _prep_kernel,
        grid=(N_TOK // tb,),
        in_specs=[tok(D_MODEL), pl.BlockSpec((1, 1, tb), lambda i: (i, 0, 0)), full((1, D_MODEL)), full((1, D_MODEL)),
                  full((D_MODEL, IN_COLS)), full((1, GM_WIDTH)), full((1, GM_WIDTH)),
                  full((GM_HEADS, GM_CHUNK, GM_CHUNK)), full((GM_HEADS, GM_CHUNK, GM_HEAD_DIM)),
                  full((1, Q_LORA_RANK)), full((Q_LORA_RANK, D_MODEL)),
                  full((1, KV_LORA_RANK)), full((KV_LORA_RANK, D_MODEL)),
                  full((1, LANES)), full((1, LANES)), full((1, LANES))],
        out_specs=[tok(D_MODEL), tok(GM_WIDTH), tok(D_MODEL), tok(D_MODEL), tok(GM_WIDTH)],
        out_shape=[jax.ShapeDtypeStruct((N_TOK, D_MODEL), jnp.float32),
                   jax.ShapeDtypeStruct((N_TOK, GM_WIDTH), jnp.bfloat16),
                   jax.ShapeDtypeStruct((N_TOK, D_MODEL), jnp.bfloat16),
                   jax.ShapeDtypeStruct((N_TOK, D_MODEL), jnp.bfloat16),
                   jax.ShapeDtypeStruct((N_TOK, GM_WIDTH), jnp.bfloat16)],
        scratch_shapes=[pltpu.VMEM((tb, LANES), jnp.float32)] * 3,
        compiler_params=pltpu.CompilerParams(
            dimension_semantics=("arbitrary",), vmem_limit_bytes=VMEM_LIMIT),
        name="prep",
    )(x2, pos2, ln0g, ln0b, win, gmg, gmb, ws, bs, qg, wuq, kvg, wukv, freq, phase, sign)


def _attn_kernel(q_ref, k_ref, v_ref, o_ref, s_ref, mx_ref, ls_ref, acc_ref):
    qi = pl.program_id(1)
    tq = q_ref.shape[0]
    nt = (((1,), (1,)), ((), ()))
    n_kv = SEQ // ATT_K

    half = tq // 2
    row_chunk = (lax.broadcasted_iota(jnp.int32, (half, half), 0)) // CHUNK
    col_chunk = (lax.broadcasted_iota(jnp.int32, (half, half), 1)) // CHUNK
    quad_allowed = col_chunk <= row_chunk
    diag_start = pl.multiple_of(qi * ATT_K, ATT_K)
    masked = jnp.float32(-1e30)

    def lane_tiles(x):
        return [x[:, t * LANES:(t + 1) * LANES] for t in range(x.shape[1] // LANES)]

    def lane_fold(x, op):
        return functools.reduce(op, lane_tiles(x))

    def scores(hd, j):
        start = pl.multiple_of(j * ATT_K, ATT_K)
        q_h = q_ref[:, hd * 2 * LANES:(hd + 1) * 2 * LANES]
        kb = k_ref[pl.ds(start, ATT_K), hd * 2 * LANES:(hd + 1) * 2 * LANES]
        return lax.dot_general(q_h, kb, nt, preferred_element_type=jnp.float32)

    for hd in range(MLA_HEADS):
        cols = slice(hd * 2 * LANES, (hd + 1) * 2 * LANES)
        k_lo = k_ref[pl.ds(diag_start, half), cols]
        k_hi = k_ref[pl.ds(diag_start + half, half), cols]
        qk = lambda q, k: lax.dot_general(q, k, nt, preferred_element_type=jnp.float32)
        s_tl = jnp.where(quad_allowed, qk(q_ref[:half, cols], k_lo), masked)
        s_bl = qk(q_ref[half:, cols], k_lo)
        s_br = jnp.where(quad_allowed, qk(q_ref[half:, cols], k_hi), masked)
        diag = s_ref.at[hd * n_kv + qi]
        diag[:half, :half] = s_tl
        diag[half:, :half] = s_bl
        diag[half:, half:] = s_br
        mx_ref[hd, :half] = lane_fold(s_tl, jnp.maximum)
        mx_ref[hd, half:] = jnp.maximum(lane_fold(s_bl, jnp.maximum), lane_fold(s_br, jnp.maximum))

    def pass_a(j, c):
        for hd in range(MLA_HEADS):
            s = scores(hd, j)
            s_ref[hd * n_kv + j] = s
            mx_ref[hd] = jnp.maximum(mx_ref[hd], lane_fold(s, jnp.maximum))
        return c

    lax.fori_loop(0, qi, pass_a, 0)

    for hd in range(MLA_HEADS):
        mx_ref[hd] = jnp.broadcast_to(jnp.max(mx_ref[hd], axis=-1, keepdims=True), (tq, LANES))

    def probs(hd, j):
        s = s_ref[hd * n_kv + j]
        mb = mx_ref[hd]
        p = jnp.exp2(jnp.concatenate([t - mb for t in lane_tiles(s)], axis=-1))
        start = pl.multiple_of(j * ATT_K, ATT_K)
        vb = v_ref[pl.ds(start, ATT_K), hd * LANES:(hd + 1) * LANES]
        pv = jnp.dot(p.astype(jnp.bfloat16), vb, preferred_element_type=jnp.float32)
        return lane_fold(p, jnp.add), pv

    for hd in range(MLA_HEADS):
        diag = s_ref.at[hd * n_kv + qi]
        vcols = slice(hd * LANES, (hd + 1) * LANES)
        p_top = jnp.exp2(jnp.concatenate([t - mx_ref[hd, :half] for t in lane_tiles(diag[:half, :half])], axis=-1))
        p_bot = jnp.exp2(jnp.concatenate([t - mx_ref[hd, half:] for t in lane_tiles(diag[half:, :])], axis=-1))
        ls_ref[hd, :half] = lane_fold(p_top, jnp.add)
        ls_ref[hd, half:] = lane_fold(p_bot, jnp.add)
        acc_ref[hd, :half] = jnp.dot(p_top.astype(jnp.bfloat16), v_ref[pl.ds(diag_start, half), vcols],
                                     preferred_element_type=jnp.float32)
        acc_ref[hd, half:] = jnp.dot(p_bot.astype(jnp.bfloat16), v_ref[pl.ds(diag_start, ATT_K), vcols],
                                     preferred_element_type=jnp.float32)

    def pass_b(j, c):
        for hd in range(MLA_HEADS):
            ls, pv = probs(hd, j)
            ls_ref[hd] = ls_ref[hd] + ls
            acc_ref[hd] = acc_ref[hd] + pv
        return c

    lax.fori_loop(0, qi, pass_b, 0)

    for hd in range(MLA_HEADS):
        l = jnp.sum(ls_ref[hd], axis=-1, keepdims=True)
        o_ref[:, hd * V_HEAD_DIM:(hd + 1) * V_HEAD_DIM] = (acc_ref[hd] / l).astype(jnp.bfloat16)


def _attn(q, k, v):
    tq = ATT_Q
    tokblk = lambda cols: pl.BlockSpec((tq, cols), lambda b, i: (b * N_QBLK + i, 0))
    seqblk = lambda cols: pl.BlockSpec((SEQ, cols), lambda b, i: (b, 0))
    return pl.pallas_call(
        _attn_kernel,
        grid=(BATCH, N_QBLK),
        in_specs=[tokblk(D_MODEL), seqblk(D_MODEL), seqblk(GM_WIDTH)],
        out_specs=tokblk(GM_WIDTH),
        out_shape=jax.ShapeDtypeStruct((N_TOK, MLA_HEADS * V_HEAD_DIM), jnp.bfloat16),
        scratch_shapes=[pltpu.VMEM((MLA_HEADS * (SEQ // ATT_K), tq, ATT_K), jnp.float32),
                        pltpu.VMEM((MLA_HEADS, tq, LANES), jnp.float32),
                        pltpu.VMEM((MLA_HEADS, tq, LANES), jnp.float32),
                        pltpu.VMEM((MLA_HEADS, tq, V_HEAD_DIM), jnp.float32)],
        compiler_params=pltpu.CompilerParams(
            dimension_semantics=("arbitrary", "arbitrary"), vmem_limit_bytes=VMEM_LIMIT),
        name="attn",
    )(q, k, v)


def _proj_kernel(outa_ref, ob_ref, h_hbm_ref, wout_ref, ln1g_ref, ln1b_ref, wr_ref, br_ref,
                 h1_ref, h1b_ref, ri_ref, rf_ref, proj_ref, hring_ref, hsems):
    i = pl.program_id(0)
    tb = outa_ref.shape[0]
    h_ref = _ring_block(h_hbm_ref, hring_ref, hsems, lambda step: jnp.maximum(step - 1, 0), tb)

    @pl.when(i == 0)
    def _():
        proj_ref[...] = jnp.zeros_like(proj_ref)

    h1 = _layer_norm(ALPHA * h_ref[...] + proj_ref[(i + 1) % 2], ln1g_ref[...], ln1b_ref[...])
    h1_ref[...] = h1
    h1b_ref[...] = h1.astype(jnp.bfloat16)

    logits_tm = jnp.dot(h1b_ref[...], wr_ref[...], preferred_element_type=jnp.float32)
    logits = logits_tm.T[0:ROUTER_ROWS] + br_ref[...]

    sub_i = lax.broadcasted_iota(jnp.int32, (SUBLANES, tb), 0)
    sub = sub_i.astype(jnp.float32)
    neg = jnp.float32(-jnp.inf)
    g = jnp.where(sub_i < N_GROUPS, logits[0:SUBLANES], neg)
    gmax = jnp.max(g, axis=0, keepdims=True)
    g_top = jnp.min(jnp.where(g == gmax, sub, float(SUBLANES)), axis=0, keepdims=True)
    p_group = 1.0 / jnp.sum(jnp.exp(g - gmax), axis=0, keepdims=True)
    sel = logits[SUBLANES:2 * SUBLANES]
    for grp in range(1, N_GROUPS):
        sel = jnp.where(g_top == float(grp), logits[(grp + 1) * SUBLANES:(grp + 2) * SUBLANES], sel)
    v1 = jnp.max(sel, axis=0, keepdims=True)
    i1 = jnp.min(jnp.where(sel == v1, sub, float(SUBLANES)), axis=0, keepdims=True)
    sel2 = jnp.where(sub == i1, neg, sel)
    v2 = jnp.max(sel2, axis=0, keepdims=True)
    i2 = jnp.min(jnp.where(sel2 == v2, sub, float(SUBLANES)), axis=0, keepdims=True)
    e21 = jnp.exp(v2 - v1)
    w1 = 1.0 / (1.0 + e21)
    gate1 = p_group * w1
    gate2 = p_group * (e21 * w1)
    e1 = g_top * EXPERTS_PER_GROUP + i1
    e2 = g_top * EXPERTS_PER_GROUP + i2
    ri_ref[...] = jnp.where(sub_i == 0, e1, jnp.where(sub_i == 1, e2, 0.0)).astype(jnp.int32)
    rf_ref[...] = jnp.where(sub_i == 0, gate1, jnp.where(sub_i == 1, gate2, 0.0))

    proj_ref[i % 2] = (jnp.dot(outa_ref[...], wout_ref[:GM_WIDTH, :], preferred_element_type=jnp.float32)
                       + jnp.dot(ob_ref[...], wout_ref[GM_WIDTH:, :], preferred_element_type=jnp.float32))


def _proj(outa, ob, h, wout, ln1g, ln1b, wr, br):
    tb = PROJ_TOKENS
    n_blk = N_TOK // tb
    cur = lambda i: jnp.minimum(i, n_blk - 1)
    prev = lambda i: jnp.maximum(i - 1, 0)
    full = lambda shape: pl.BlockSpec(shape, lambda i: (0,) * len(shape))
    return pl.pallas_call(
        _proj_kernel,
        grid=(n_blk + 1,),
        in_specs=[pl.BlockSpec((tb, GM_WIDTH), lambda i: (cur(i), 0)),
                  pl.BlockSpec((tb, GM_WIDTH), lambda i: (cur(i), 0)),
                  pl.BlockSpec(memory_space=pl.ANY),
                  full((D_MODEL, D_MODEL)), full((1, D_MODEL)), full((1, D_MODEL)),
                  full((D_MODEL, LANES)), full((ROUTER_ROWS, 1))],
        out_specs=[pl.BlockSpec((tb, D_MODEL), lambda i: (prev(i), 0)),
                   pl.BlockSpec((tb, D_MODEL), lambda i: (prev(i), 0)),
                   pl.BlockSpec((SUBLANES, tb), lambda i: (0, prev(i))),
                   pl.BlockSpec((SUBLANES, tb), lambda i: (0, prev(i)))],
        out_shape=[jax.ShapeDtypeStruct((N_TOK, D_MODEL), jnp.float32),
                   jax.ShapeDtypeStruct((N_TOK, D_MODEL), jnp.bfloat16),
                   jax.ShapeDtypeStruct((SUBLANES, N_TOK), jnp.int32),
                   jax.ShapeDtypeStruct((SUBLANES, N_TOK), jnp.float32)],
        scratch_shapes=[pltpu.VMEM((2, tb, D_MODEL), jnp.float32)] + _ring_scratch(tb, D_MODEL, jnp.float32),
        compiler_params=pltpu.CompilerParams(
            dimension_semantics=("arbitrary",), vmem_limit_bytes=VMEM_LIMIT),
        name="proj",
    )(outa, ob, h, wout, ln1g, ln1b, wr, br)


def _plan_kernel(ri_all_ref, ri_ref, ldest_ref, runs_ref, meta_ref, run_ref, start_ref, upper_ref):
    step = pl.program_id(0)
    tb = ri_ref.shape[1]
    f32 = jnp.float32
    er = lax.broadcasted_iota(jnp.int32, (N_EXPERTS, LANES), 0)
    ec = lax.broadcasted_iota(jnp.int32, (N_EXPERTS, LANES), 1)
    to_row = lambda col: jnp.sum(jnp.where(er == ec, col, 0.0), axis=0, keepdims=True)

    def expert_one_hot(ref):
        e_sub = lax.broadcasted_iota(jnp.int32, (N_EXPERTS, ref.shape[1]), 0)
        return e_sub == ref[0:1, :], e_sub == ref[1:2, :]

    @pl.when(step == 0)
    def _():
        oh1, oh2 = expert_one_hot(ri_all_ref)
        counts = jnp.sum(jnp.where(oh1 | oh2, 1.0, 0.0), axis=1, keepdims=True)
        padded = jnp.floor((counts + (EXPERT_ROWS - 1)) * (1.0 / EXPERT_ROWS)) * EXPERT_ROWS
        pad_end = jnp.sum(jnp.where(ec <= er, to_row(padded), 0.0), axis=1, keepdims=True)
        start_ref[...] = jnp.broadcast_to(pad_end - padded, start_ref.shape)
        run_ref[...] = jnp.zeros_like(run_ref)
        bstart = (lax.broadcasted_iota(jnp.int32, (N_EXPERTS, META_LANES), 1) * EXPERT_ROWS).astype(f32)
        blk_e = jnp.sum(jnp.where(pad_end <= bstart, 1.0, 0.0), axis=0, keepdims=True)
        blk_e = jnp.minimum(blk_e, N_EXPERTS - 1.0)
        n_used = pad_end[N_EXPERTS - 1:N_EXPERTS, :] * (1.0 / EXPERT_ROWS)
        pad3 = lambda r: jnp.concatenate(
            [r, jnp.zeros((1, META_LANES - LANES), f32)], axis=1)
        msub = lax.broadcasted_iota(jnp.int32, (SUBLANES, META_LANES), 0)
        meta = jnp.where(msub == 0, blk_e,
                         jnp.where(msub == 1, pad3(to_row(pad_end)),
                                   jnp.where(msub == 2, pad3(to_row(counts)),
                                             jnp.where(msub == 3, n_used, 0.0))))
        meta_ref[...] = meta.astype(jnp.int32)
        tr = lax.broadcasted_iota(jnp.int32, (tb, tb), 0)
        tc = lax.broadcasted_iota(jnp.int32, (tb, tb), 1)
        upper_ref[...] = jnp.where(tr < tc, 1.0, 0.0).astype(jnp.bfloat16)

    @pl.when(step > 0)
    def _():
        oh1, oh2 = expert_one_hot(ri_ref)
        oh = jnp.where(oh1 | oh2, 1.0, 0.0).astype(f32)
        blk_count = jnp.sum(oh, axis=1, keepdims=True)
        prefix = jnp.dot(oh.astype(jnp.bfloat16), upper_ref[...], preferred_element_type=f32)
        cnt_row = to_row(blk_count)
        lstart = jnp.sum(jnp.where(ec < er, cnt_row, 0.0), axis=1, keepdims=True)
        base = prefix + lstart
        d1 = jnp.sum(jnp.where(oh1, base, 0.0), axis=0, keepdims=True)
        d2 = jnp.sum(jnp.where(oh2, base, 0.0), axis=0, keepdims=True)
        sub = lax.broadcasted_iota(jnp.int32, (SUBLANES, tb), 0)
        ldest_ref[...] = jnp.where(sub == 0, d1, jnp.where(sub == 1, d2, 0.0)).astype(jnp.int32)
        gstart = start_ref[:, 0:1] + run_ref[:, 0:1]
        rsub = lax.broadcasted_iota(jnp.int32, (SUBLANES, LANES), 0)
        runs = jnp.where(rsub == 0, cnt_row,
                         jnp.where(rsub == 1, to_row(lstart * FEAT_TILES),
                                   jnp.where(rsub == 2, to_row(gstart * FEAT_TILES), 0.0)))
        runs_ref[...] = runs.astype(jnp.int32)
        run_ref[...] = run_ref[...] + blk_count


def _plan(ri):
    tb = MOE_TOKENS
    blk = lambda i: jnp.maximum(i - 1, 0)
    return pl.pallas_call(
        _plan_kernel,
        grid=(N_MOE_BLOCKS + 1,),
        in_specs=[pl.BlockSpec((SUBLANES, N_TOK), lambda i: (0, 0)),
                  pl.BlockSpec((SUBLANES, tb), lambda i: (0, blk(i)))],
        out_specs=[pl.BlockSpec((SUBLANES, tb), lambda i: (0, blk(i))),
                   pl.BlockSpec((SUBLANES, LANES), lambda i: (blk(i), 0)),
                   pl.BlockSpec((SUBLANES, META_LANES), lambda i: (0, 0))],
        out_shape=[jax.ShapeDtypeStruct((SUBLANES, N_TOK), jnp.int32),
                   jax.ShapeDtypeStruct((N_MOE_BLOCKS * SUBLANES, LANES), jnp.int32),
                   jax.ShapeDtypeStruct((SUBLANES, META_LANES), jnp.int32)],
        scratch_shapes=[pltpu.VMEM((N_EXPERTS, LANES), jnp.float32),
                        pltpu.VMEM((N_EXPERTS, LANES), jnp.float32),
                        pltpu.VMEM((tb, tb), jnp.bfloat16)],
        compiler_params=pltpu.CompilerParams(
            dimension_semantics=("arbitrary",), vmem_limit_bytes=VMEM_LIMIT),
        name="plan",
    )(ri, ri)


def _for_each_run_piece(runs_ref, row0, fn, live=True):
    for e in range(N_EXPERTS):
        n_tiles = jnp.where(live, runs_ref[row0, e], 0) * FEAT_TILES
        lrow, grow = runs_ref[row0 + 1, e], runs_ref[row0 + 2, e]
        for bit in reversed(range(RUN_BITS)):
            piece = n_tiles & (FEAT_TILES << bit)

            @pl.when(piece != 0)
            def _(lrow=lrow, grow=grow, bit=bit):
                fn(lrow, grow, 1 << bit)

            lrow, grow = lrow + piece, grow + piece


def _tile_rows(ref, tile_row, rows):
    return ref.at[pl.ds(pl.multiple_of(tile_row, FEAT_TILES), rows * FEAT_TILES)]


def _dispatch_kernel(meta_ref, runs_ref, ldest_ref, h1b_ref, buf_ref, *scratch):
    sorted_refs = scratch[:MOE_GROUP]
    zero_ref, sems, zsem = scratch[MOE_GROUP:]
    i = pl.program_id(0)
    n_steps = pl.num_programs(0)
    tb = MOE_TOKENS
    slot = i % 2
    block_tiles = 2 * tb * FEAT_TILES

    def wait_slot(g, s):
        pltpu.make_async_copy(sorted_refs[g].at[s], buf_ref.at[pl.ds(0, block_tiles)], sems.at[g, s]).wait()

    @pl.when(i == 0)
    def _():
        zero_ref[...] = jnp.zeros_like(zero_ref)

        def zero_copy(e):
            start = pl.multiple_of((meta_ref[1, e] - EXPERT_ROWS) * FEAT_TILES, EXPERT_ROWS * FEAT_TILES)
            return pltpu.make_async_copy(
                zero_ref, buf_ref.at[pl.ds(start, EXPERT_ROWS * FEAT_TILES)], zsem)

        def start_zero(e, c):
            @pl.when(meta_ref[2, e] > 0)
            def _():
                zero_copy(e).start()
            return c

        def wait_zero(e, c):
            @pl.when(meta_ref[2, e] > 0)
            def _():
                zero_copy(e).wait()
            return c

        def tail_copy(b):
            start = pl.multiple_of(b * (EXPERT_ROWS * FEAT_TILES), EXPERT_ROWS * FEAT_TILES)
            return pltpu.make_async_copy(
                zero_ref, buf_ref.at[pl.ds(start, EXPERT_ROWS * FEAT_TILES)], zsem)

        def start_tail(b, c):
            tail_copy(b).start()
            return c

        def wait_tail(b, c):
            tail_copy(b).wait()
            return c

        lax.fori_loop(0, N_EXPERTS, start_zero, 0)
        lax.fori_loop(meta_ref[3, 0], N_ROW_BLOCKS, start_tail, 0)
        lax.fori_loop(0, N_EXPERTS, wait_zero, 0)
        lax.fori_loop(meta_ref[3, 0], N_ROW_BLOCKS, wait_tail, 0)

    @pl.when(i >= 2)
    def _():
        for g in range(MOE_GROUP):
            wait_slot(g, slot)

    for g in range(MOE_GROUP):
        x = h1b_ref[g * tb:(g + 1) * tb, :]
        ld0 = ldest_ref[0:1, g * tb:(g + 1) * tb]
        ld1 = ldest_ref[1:2, g * tb:(g + 1) * tb]
        for c in range(2 * tb // SORT_CHUNK):
            r = lax.broadcasted_iota(jnp.int32, (SORT_CHUNK, tb), 0) + c * SORT_CHUNK
            perm = jnp.where((r == ld0) | (r == ld1), 1.0, 0.0).astype(jnp.bfloat16)
            rows = jnp.dot(perm, x, preferred_element_type=jnp.float32)
            _to_row_tiles(sorted_refs[g].at[slot, pl.ds(c * SORT_CHUNK * FEAT_TILES, SORT_CHUNK * FEAT_TILES)],
                          rows)

        def send(lrow, grow, rows, g=g):
            pltpu.make_async_copy(_tile_rows(sorted_refs[g].at[slot], lrow, rows),
                                  _tile_rows(buf_ref, grow, rows), sems.at[g, slot]).start()

        _for_each_run_piece(runs_ref, g * SUBLANES, send)

    @pl.when(i == n_steps - 1)
    def _():
        for g in range(MOE_GROUP):
            wait_slot(g, slot)
            wait_slot(g, 1 - slot)


def _dispatch(meta, runs, ldest, h1b):
    tb = MOE_GROUP * MOE_TOKENS
    return pl.pallas_call(
        _dispatch_kernel,
        grid_spec=pltpu.PrefetchScalarGridSpec(
            num_scalar_prefetch=1,
            grid=(N_MOE_BLOCKS // MOE_GROUP,),
            in_specs=[pl.BlockSpec((MOE_GROUP * SUBLANES, LANES), lambda i, m: (i, 0), memory_space=pltpu.SMEM),
                      pl.BlockSpec((SUBLANES, tb), lambda i, m: (0, i)),
                      pl.BlockSpec((tb, D_MODEL), lambda i, m: (i, 0))],
            out_specs=pl.BlockSpec(memory_space=pl.ANY),
            scratch_shapes=[pltpu.VMEM((2, 2 * MOE_TOKENS * FEAT_TILES, LANES), jnp.float32)] * MOE_GROUP + [
                pltpu.VMEM((EXPERT_ROWS * FEAT_TILES, LANES), jnp.float32),
                pltpu.SemaphoreType.DMA((MOE_GROUP, 2)), pltpu.SemaphoreType.DMA]),
        out_shape=jax.ShapeDtypeStruct((N_ROWS * FEAT_TILES, LANES), jnp.float32),
        compiler_params=pltpu.CompilerParams(
            dimension_semantics=("arbitrary",), vmem_limit_bytes=VMEM_LIMIT),
        name="dispatch",
    )(meta, runs, ldest, h1b)


def _sub_block_expert(meta, i, sub):
    return meta[0, jnp.minimum(i + sub * (N_ROW_BLOCKS // EXPERT_SUB), meta[3, 0] - 1)]


def _experts_kernel(meta_ref, x_hbm_ref, *refs):
    weights = [refs[3 * sub:3 * sub + 3] for sub in range(EXPERT_SUB)]
    o_ref, wgb_ref, wub_ref, wdb_ref, cached_ref, xring_ref, xsems = refs[3 * EXPERT_SUB:]
    n = EXPERT_ROWS
    i = pl.program_id(0)
    n_steps = pl.num_programs(0)

    def rows_copy(step):
        slot = step % EXPERT_RING
        start = pl.multiple_of(step * (n * FEAT_TILES), n * FEAT_TILES)
        return pltpu.make_async_copy(x_hbm_ref.at[:, pl.ds(start, n * FEAT_TILES), :], xring_ref.at[slot],
                                     xsems.at[slot])

    @pl.when(i == 0)
    def _():
        for step in range(EXPERT_RING - 1):
            rows_copy(step).start()

    @pl.when(i + (EXPERT_RING - 1) < n_steps)
    def _():
        rows_copy(i + (EXPERT_RING - 1)).start()

    rows_copy(i).wait()
    x_ref = xring_ref.at[i % EXPERT_RING]

    @pl.when(i == 0)
    def _():
        for sub in range(EXPERT_SUB):
            cached_ref[sub] = -1

    for sub in range(EXPERT_SUB):
        expert = _sub_block_expert(meta_ref, i, sub)

        @pl.when(cached_ref[sub] != expert)
        def _(sub=sub, expert=expert):
            wg_ref, wu_ref, wd_ref = weights[sub]
            wgb_ref[sub] = wg_ref[...].astype(jnp.bfloat16)
            wub_ref[sub] = wu_ref[...].astype(jnp.bfloat16)
            wdb_ref[sub] = wd_ref[...].astype(jnp.bfloat16)
            cached_ref[sub] = expert

    for sub in range(EXPERT_SUB):
        x = _from_row_tiles(x_ref.at[sub], n).astype(jnp.bfloat16)
        gate = jnp.dot(x, wgb_ref[sub], preferred_element_type=jnp.float32)
        up = jnp.dot(x, wub_ref[sub], preferred_element_type=jnp.float32)
        act = (gate * jax.nn.sigmoid(gate) * up).astype(jnp.bfloat16)
        _to_row_tiles(o_ref.at[sub], jnp.dot(act, wdb_ref[sub], preferred_element_type=jnp.float32))


def _experts(meta, buf, wg, wu, wd):
    sub_tiles = (N_ROW_BLOCKS // EXPERT_SUB) * EXPERT_ROWS * FEAT_TILES

    def weight_spec(shape, sub):
        return pl.BlockSpec((None,) + shape, lambda i, m: (_sub_block_expert(m, i, sub), 0, 0))

    rows_shape = (EXPERT_SUB, EXPERT_ROWS * FEAT_TILES, LANES)
    up_shape, down_shape = (D_MODEL, EXPERT_FF), (EXPERT_FF, D_MODEL)
    eout = pl.pallas_call(
        _experts_kernel,
        grid_spec=pltpu.PrefetchScalarGridSpec(
            num_scalar_prefetch=1,
            grid=(N_ROW_BLOCKS // EXPERT_SUB,),
            in_specs=[pl.BlockSpec(memory_space=pl.ANY)] + [
                weight_spec(shape, sub) for sub in range(EXPERT_SUB) for shape in (up_shape, up_shape, down_shape)],
            out_specs=pl.BlockSpec(rows_shape, lambda i, m: (0, i, 0)),
            scratch_shapes=[pltpu.VMEM((EXPERT_SUB,) + up_shape, jnp.bfloat16),
                            pltpu.VMEM((EXPERT_SUB,) + up_shape, jnp.bfloat16),
                            pltpu.VMEM((EXPERT_SUB,) + down_shape, jnp.bfloat16),
                            pltpu.SMEM((EXPERT_SUB,), jnp.int32),
                            pltpu.VMEM((EXPERT_RING,) + rows_shape, jnp.float32),
                            pltpu.SemaphoreType.DMA((EXPERT_RING,))]),
        out_shape=jax.ShapeDtypeStruct((EXPERT_SUB, sub_tiles, LANES), jnp.float32),
        compiler_params=pltpu.CompilerParams(
            dimension_semantics=("arbitrary",), vmem_limit_bytes=EXPERT_VMEM_LIMIT),
        name="experts",
    )(meta, buf.reshape(EXPERT_SUB, sub_tiles, LANES), *([wg, wu, wd] * EXPERT_SUB))
    return eout.reshape(N_ROWS * FEAT_TILES, LANES)


def _combine_kernel(runs_ref, runs_next_ref, ldest_ref, rf_ref, h1_ref, eout_ref, ln2g_ref, ln2b_ref,
                    o_ref, *scratch):
    y_refs = scratch[:MOE_GROUP]
    sems = scratch[MOE_GROUP]
    i = pl.program_id(0)
    n_steps = pl.num_programs(0)
    tb = MOE_TOKENS
    slot = i % 2
    block_tiles = 2 * tb * FEAT_TILES

    def fetch(table_ref, g, s, live):
        def recv(lrow, grow, rows):
            pltpu.make_async_copy(_tile_rows(eout_ref, grow, rows),
                                  _tile_rows(y_refs[g].at[s], lrow, rows), sems.at[g, s]).start()
        _for_each_run_piece(table_ref, g * SUBLANES, recv, live)

    @pl.when(i == 0)
    def _():
        for g in range(MOE_GROUP):
            fetch(runs_ref, g, slot, True)

    for g in range(MOE_GROUP):
        pltpu.make_async_copy(eout_ref.at[pl.ds(0, block_tiles)], y_refs[g].at[slot], sems.at[g, slot]).wait()
        fetch(runs_next_ref, g, 1 - slot, i + 1 < n_steps)

        ld = ldest_ref[:, g * tb:(g + 1) * tb].astype(jnp.float32).T
        gates = rf_ref[:, g * tb:(g + 1) * tb].T
        y = None
        col = lax.broadcasted_iota(jnp.int32, (tb, SORT_CHUNK), 1).astype(jnp.float32)
        for c in range(2 * tb // SORT_CHUNK):
            ld_c = ld - float(c * SORT_CHUNK)
            gm = jnp.where(col == ld_c[:, 0:1], gates[:, 0:1],
                           jnp.where(col == ld_c[:, 1:2], gates[:, 1:2], 0.0)).astype(jnp.bfloat16)
            rows = _from_row_tiles(
                y_refs[g].at[slot, pl.ds(c * SORT_CHUNK * FEAT_TILES, SORT_CHUNK * FEAT_TILES)], SORT_CHUNK)
            part = jnp.dot(gm, rows.astype(jnp.bfloat16), preferred_element_type=jnp.float32)
            y = part if y is None else y + part
        tok = slice(g * tb, (g + 1) * tb)
        o_ref[tok, :] = _layer_norm(ALPHA * h1_ref[tok, :] + y, ln2g_ref[...], ln2b_ref[...])


def _combine(runs, ldest, rf, h1, eout, ln2g, ln2b):
    tb = MOE_GROUP * MOE_TOKENS
    n_steps = N_MOE_BLOCKS // MOE_GROUP
    table = lambda index_map: pl.BlockSpec((MOE_GROUP * SUBLANES, LANES), index_map, memory_space=pltpu.SMEM)
    return pl.pallas_call(
        _combine_kernel,
        grid=(n_steps,),
        in_specs=[table(lambda i: (i, 0)), table(lambda i: (jnp.minimum(i + 1, n_steps - 1), 0)),
                  pl.BlockSpec((SUBLANES, tb), lambda i: (0, i)),
                  pl.BlockSpec((SUBLANES, tb), lambda i: (0, i)),
                  pl.BlockSpec((tb, D_MODEL), lambda i: (i, 0)),
                  pl.BlockSpec(memory_space=pl.ANY),
                  pl.BlockSpec((1, D_MODEL), lambda i: (0, 0)),
                  pl.BlockSpec((1, D_MODEL), lambda i: (0, 0))],
        out_specs=pl.BlockSpec((tb, D_MODEL), lambda i: (i, 0)),
        out_shape=jax.ShapeDtypeStruct((N_TOK, D_MODEL), jnp.float32),
        scratch_shapes=[pltpu.VMEM((2, 2 * MOE_TOKENS * FEAT_TILES, LANES), jnp.float32)] * MOE_GROUP + [
            pltpu.SemaphoreType.DMA((MOE_GROUP, 2))],
        compiler_params=pltpu.CompilerParams(
            dimension_semantics=("arbitrary",), vmem_limit_bytes=VMEM_LIMIT),
        name="combine",
    )(runs, runs, ldest, rf, h1, eout, ln2g, ln2b)


def _swap_halves(w):
    half = w.shape[-1] // 2
    return jnp.concatenate([w[..., half:], w[..., :half]], axis=-1)


def kernel(x, positions, ln0_g, ln0_b, w_in, gm_ln_g, gm_ln_b, w_spatial, b_spatial, q_norm_g, w_uq, kv_norm_g, w_ukv, w_out, ln1_g, ln1_b, w_router_group, b_router_group, w_router_expert, b_router_expert, w_gate, w_up, w_down, ln2_g, ln2_b):
    bf16 = jnp.bfloat16
    row = lambda a: a.reshape(1, -1)

    w_in0 = w_in[0]
    kr_cols = w_in0[:, O_KR:O_KR + QK_ROPE_DIM]
    win = jnp.concatenate([w_in0, _swap_halves(kr_cols)], axis=1).astype(bf16)
    wuq3 = w_uq[0].reshape(Q_LORA_RANK, MLA_HEADS, QK_NOPE_DIM + QK_ROPE_DIM)
    rope_cols = wuq3[:, :, QK_NOPE_DIM:]
    wuq = jnp.concatenate([wuq3, _swap_halves(rope_cols)], axis=-1).reshape(Q_LORA_RANK, D_MODEL).astype(bf16)
    wukv = w_ukv[0].astype(bf16)
    wout = w_out[0].astype(bf16)
    bs = jnp.broadcast_to(b_spatial[0][:, :, None], (GM_HEADS, GM_CHUNK, GM_HEAD_DIM))
    wr = jnp.concatenate([w_router_group[0], jnp.zeros((D_MODEL, SUBLANES - N_GROUPS), jnp.float32),
                          w_router_expert[0],
                          jnp.zeros((D_MODEL, LANES - ROUTER_ROWS), jnp.float32)],
                         axis=1).astype(bf16)
    br = jnp.concatenate([b_router_group[0], jnp.zeros((SUBLANES - N_GROUPS,), jnp.float32),
                          b_router_expert[0]]).reshape(ROUTER_ROWS, 1)

    inv_freq = ROPE_THETA ** (-jnp.arange(0, QK_ROPE_DIM, 2, dtype=jnp.float32) / QK_ROPE_DIM)
    freq = jnp.tile(inv_freq, 4).reshape(1, LANES)
    quarter = QK_ROPE_DIM // 2
    phase = jnp.concatenate([jnp.zeros((2 * quarter,), jnp.float32),
                             jnp.full((2 * quarter,), math.pi / 2, jnp.float32)]).reshape(1, LANES)
    sign = jnp.concatenate([jnp.ones((2 * quarter,), jnp.float32), -jnp.ones((quarter,), jnp.float32),
                            jnp.ones((quarter,), jnp.float32)]).reshape(1, LANES)

    x2 = x.reshape(N_TOK, D_MODEL)
    pos2 = positions.reshape(N_TOK // PREP_TOKENS, 1, PREP_TOKENS)

    h, outa, q, k, v = _prep(x2, pos2, row(ln0_g), row(ln0_b), win, row(gm_ln_g[0]), row(gm_ln_b[0]),
                             w_spatial[0], bs, row(q_norm_g[0]), wuq, row(kv_norm_g[0]), wukv,
                             freq, phase, sign)
    ob = _attn(q, k, v)
    h1, h1b, ri, rf = _proj(outa, ob, h, wout, row(ln1_g[0]), row(ln1_b[0]), wr, br)
    ldest, runs, meta = _plan(ri)
    buf = _dispatch(meta, runs, ldest, h1b)
    eout = _experts(meta, buf, w_gate[0], w_up[0], w_down[0])
    out = _combine(runs, ldest, rf, h1, eout, row(ln2_g[0]), row(ln2_b[0]))
    return out.reshape(BATCH, SEQ, D_MODEL)
```

```python
import functools
import math

import jax
import jax.numpy as jnp
from jax import lax
from jax.experimental import pallas as pl
from jax.experimental.pallas import tpu as pltpu

D_MODEL = 1024
BATCH = 16
SEQ = 2048
N_TOK = BATCH * SEQ
CHUNK = 64
GM_WIDTH = 512
GM_HEADS = 4
GM_HEAD_DIM = 128
GM_CHUNK = 128
MLA_HEADS = 4
QK_NOPE_DIM = 128
QK_ROPE_DIM = 64
V_HEAD_DIM = 128
Q_LORA_RANK = 384
KV_LORA_RANK = 256
ROPE_THETA = 10000.0
N_GROUPS = 4
EXPERTS_PER_GROUP = 8
N_EXPERTS = 32
TOP_K = 2
EXPERT_FF = 256
ALPHA = 2.0 ** 0.25
QK_SCALE = (QK_NOPE_DIM + QK_ROPE_DIM) ** -0.5 * math.log2(math.e)

LANES = 128
SUBLANES = 8
FEAT_TILES = D_MODEL // LANES
PREP_TOKENS = 1024
PREP_ROWS = 256
ATT_Q = 512
ATT_K = 512
N_QBLK = SEQ // ATT_Q
PROJ_TOKENS = 1024
MOE_TOKENS = 512
N_MOE_BLOCKS = N_TOK // MOE_TOKENS
MOE_GROUP = 2
RUN_BITS = (TOP_K * MOE_TOKENS).bit_length()
SORT_CHUNK = 256
EXPERT_ROWS = 256
EXPERT_SUB = 4
EXPERT_RING = 3
INPUT_RING = 3
N_ROWS = N_TOK * TOP_K + N_EXPERTS * EXPERT_ROWS
N_ROW_BLOCKS = N_ROWS // EXPERT_ROWS
META_LANES = 384
IN_COLS = 2 * GM_WIDTH + Q_LORA_RANK + KV_LORA_RANK + 2 * QK_ROPE_DIM
O_Q = 2 * GM_WIDTH
O_KV = O_Q + Q_LORA_RANK
O_KR = O_KV + KV_LORA_RANK
ROUTER_ROWS = 40
VMEM_LIMIT = 48 * 1024 * 1024
EXPERT_VMEM_LIMIT = 56 * 1024 * 1024

assert N_ROW_BLOCKS <= META_LANES and N_ROW_BLOCKS % EXPERT_SUB == 0


def _layer_norm(x, g, b, eps=1e-5):
    mu = jnp.mean(x, axis=-1, keepdims=True)
    xc = x - mu
    var = jnp.mean(xc * xc, axis=-1, keepdims=True)
    return xc * lax.rsqrt(var + eps) * g + b


def _rms_norm(x, g, eps=1e-6):
    return x * lax.rsqrt(jnp.mean(x * x, axis=-1, keepdims=True) + eps) * g


def _gelu_tanh(x):
    c = math.sqrt(2.0 / math.pi)
    return 0.5 * x * (1.0 + jnp.tanh(c * (x + 0.044715 * (x * x * x))))


def _to_row_tiles(ref, x):
    n = x.shape[0]
    for s in range(FEAT_TILES):
        ref[pl.ds(s, n, stride=FEAT_TILES), :] = x[:, s * LANES:(s + 1) * LANES]


def _from_row_tiles(ref, n):
    return jnp.concatenate(
        [ref[pl.ds(s, n, stride=FEAT_TILES), :] for s in range(FEAT_TILES)], axis=-1)


def _ring_block(hbm_ref, ring_ref, sems, block_of_step, rows):
    i = pl.program_id(0)
    n_steps = pl.num_programs(0)

    def copy(step):
        start = pl.multiple_of(block_of_step(step) * rows, rows)
        slot = step % INPUT_RING
        return pltpu.make_async_copy(hbm_ref.at[pl.ds(start, rows)], ring_ref.at[slot], sems.at[slot])

    @pl.when(i == 0)
    def _():
        for step in range(INPUT_RING - 1):
            copy(step).start()

    @pl.when(i + (INPUT_RING - 1) < n_steps)
    def _():
        copy(i + (INPUT_RING - 1)).start()

    copy(i).wait()
    return ring_ref.at[i % INPUT_RING]


def _ring_scratch(rows, cols, dtype):
    return [pltpu.VMEM((INPUT_RING, rows, cols), dtype), pltpu.SemaphoreType.DMA((INPUT_RING,))]


def _prep_kernel(x_ref, pos_ref, ln0g_ref, ln0b_ref, win_ref, gmg_ref, gmb_ref, ws_ref, bs_ref,
                 qg_ref, wuq_ref, kvg_ref, wukv_ref, freq_ref, phase_ref, sign_ref,
                 h_ref, outa_ref, q_ref, k_ref, v_ref, tabc_ref, tabs_ref, rot_ref):
    tb = x_ref.shape[0]

    @pl.when(pl.program_id(0) == 0)
    def _():
        d = lax.broadcasted_iota(jnp.int32, (tb, LANES), 0).astype(jnp.float32) * freq_ref[...]
        tabc_ref[...] = jnp.cos(d)
        tabs_ref[...] = jnp.sin(d)

    pos_row = pos_ref[0]
    p0 = pos_row[:, 0:1]
    offset = lax.broadcasted_iota(jnp.int32, (1, tb), 1)
    consecutive = jnp.max(jnp.abs((pos_row - p0 - offset).astype(jnp.float32))) == 0.0

    @pl.when(consecutive)
    def _():
        a0 = p0.astype(jnp.float32) * freq_ref[...]
        c0, s0 = jnp.cos(a0), jnp.sin(a0)
        lane = lax.broadcasted_iota(jnp.int32, (1, LANES), 1)
        n_freq = QK_ROPE_DIM // 2
        coef_c = jnp.where(lane < 2 * n_freq, c0, jnp.where(lane < 3 * n_freq, -s0, s0))
        coef_s = jnp.where(lane < 2 * n_freq, -s0, jnp.where(lane < 3 * n_freq, -c0, c0))
        rot_ref[...] = coef_c * tabc_ref[...] + coef_s * tabs_ref[...]

    @pl.when(jnp.logical_not(consecutive))
    def _():
        pos_col = jnp.broadcast_to(pos_row.astype(jnp.float32), (SUBLANES, tb)).T[:, 0:1]
        ang = pos_col * freq_ref[...]
        rot_ref[...] = jnp.cos(ang - phase_ref[...]) * sign_ref[...]

    row_chunk = lax.broadcasted_iota(jnp.int32, (GM_CHUNK, GM_CHUNK), 0) // CHUNK
    col_chunk = lax.broadcasted_iota(jnp.int32, (GM_CHUNK, GM_CHUNK), 1) // CHUNK
    allowed = col_chunk <= row_chunk
    wms = [jnp.where(allowed, ws_ref[hd], 0.0).astype(jnp.bfloat16) for hd in range(GM_HEADS)]

    def chunk(r, carry):
        rows = pl.ds(pl.multiple_of(r * PREP_ROWS, PREP_ROWS), PREP_ROWS)
        rot = rot_ref[rows, :]
        h = _layer_norm(x_ref[rows, :], ln0g_ref[...], ln0b_ref[...])
        h_ref[rows, :] = h
        z = jnp.dot(h.astype(jnp.bfloat16), win_ref[...], preferred_element_type=jnp.float32)

        u = _gelu_tanh(z[:, :GM_WIDTH])
        v = _gelu_tanh(z[:, GM_WIDTH:2 * GM_WIDTH])
        a_parts = []
        for hd in range(GM_HEADS):
            lo, hi = hd * GM_HEAD_DIM, (hd + 1) * GM_HEAD_DIM
            vln = _layer_norm(v[:, lo:hi], gmg_ref[:, lo:hi], gmb_ref[:, lo:hi]).astype(jnp.bfloat16)
            fs = []
            for c in range(PREP_ROWS // GM_CHUNK):
                r0, r1 = c * GM_CHUNK, (c + 1) * GM_CHUNK
                fs.append(jnp.dot(wms[hd], vln[r0:r1], preferred_element_type=jnp.float32) + bs_ref[hd])
            a_parts.append(u[:, lo:hi] * jnp.concatenate(fs, axis=0))
        outa_ref[rows, :] = jnp.concatenate(a_parts, axis=-1).astype(jnp.bfloat16)

        ql = _rms_norm(z[:, O_Q:O_KV], qg_ref[...]).astype(jnp.bfloat16)
        qf = jnp.dot(ql, wuq_ref[...], preferred_element_type=jnp.float32)
        rot_s = rot * QK_SCALE
        q_parts = []
        for hd in range(MLA_HEADS):
            base = hd * 2 * LANES
            q_parts.append(qf[:, base:base + LANES] * QK_SCALE)
            q_parts.append(qf[:, base + LANES:base + 2 * LANES] * rot_s)
        q_ref[rows, :] = jnp.concatenate(q_parts, axis=-1).astype(jnp.bfloat16)

        kvl = _rms_norm(z[:, O_KV:O_KR], kvg_ref[...]).astype(jnp.bfloat16)
        kv = jnp.dot(kvl, wukv_ref[...], preferred_element_type=jnp.float32)
        t = z[:, O_KR:O_KR + LANES] * rot
        krr = t + pltpu.roll(t, 2 * QK_ROPE_DIM // 2, axis=1)
        k_parts, v_parts = [], []
        for hd in range(MLA_HEADS):
            base = hd * 2 * LANES
            k_parts.append(kv[:, base:base + LANES])
            k_parts.append(krr)
            v_parts.append(kv[:, base + LANES:base + 2 * LANES])
        k_ref[rows, :] = jnp.concatenate(k_parts, axis=-1).astype(jnp.bfloat16)
        v_ref[rows, :] = jnp.concatenate(v_parts, axis=-1).astype(jnp.bfloat16)
        return carry

    lax.fori_loop(0, tb // PREP_ROWS, chunk, 0, unroll=True)


def _prep(x2, pos2, ln0g, ln0b, win, gmg, gmb, ws, bs, qg, wuq, kvg, wukv, freq, phase, sign):
    tb = PREP_TOKENS
    full = lambda shape: pl.BlockSpec(shape, lambda i: (0,) * len(shape))
    tok = lambda cols: pl.BlockSpec((tb, cols), lambda i: (i, 0))
    return pl.pallas_call(
        _prep_kernel,
        grid=(N_TOK // tb,),
        in_specs=[tok(D_MODEL), pl.BlockSpec((1, 1, tb), lambda i: (i, 0, 0)), full((1, D_MODEL)), full((1, D_MODEL)),
                  full((D_MODEL, IN_COLS)), full((1, GM_WIDTH)), full((1, GM_WIDTH)),
                  full((GM_HEADS, GM_CHUNK, GM_CHUNK)), full((GM_HEADS, GM_CHUNK, GM_HEAD_DIM)),
                  full((1, Q_LORA_RANK)), full((Q_LORA_RANK, D_MODEL)),
                  full((1, KV_LORA_RANK)), full((KV_LORA_RANK, D_MODEL)),
                  full((1, LANES)), full((1, LANES)), full((1, LANES))],
        out_specs=[tok(D_MODEL), tok(GM_WIDTH), tok(D_MODEL), tok(D_MODEL), tok(GM_WIDTH)],
        out_shape=[jax.ShapeDtypeStruct((N_TOK, D_MODEL), jnp.float32),
                   jax.ShapeDtypeStruct((N_TOK, GM_WIDTH), jnp.bfloat16),
                   jax.ShapeDtypeStruct((N_TOK, D_MODEL), jnp.bfloat16),
                   jax.ShapeDtypeStruct((N_TOK, D_MODEL), jnp.bfloat16),
                   jax.ShapeDtypeStruct((N_TOK, GM_WIDTH), jnp.bfloat16)],
        scratch_shapes=[pltpu.VMEM((tb, LANES), jnp.float32)] * 3,
        compiler_params=pltpu.CompilerParams(
            dimension_semantics=("arbitrary",), vmem_limit_bytes=VMEM_LIMIT),
        name="prep",
    )(x2, pos2, ln0g, ln0b, win, gmg, gmb, ws, bs, qg, wuq, kvg, wukv, freq, phase, sign)


def _attn_kernel(q_ref, k_ref, v_ref, o_ref, s_ref, mx_ref, ls_ref, acc_ref):
    qi = pl.program_id(1)
    tq = q_ref.shape[0]
    nt = (((1,), (1,)), ((), ()))
    n_kv = SEQ // ATT_K

    half = tq // 2
    row_chunk = (lax.broadcasted_iota(jnp.int32, (half, half), 0)) // CHUNK
    col_chunk = (lax.broadcasted_iota(jnp.int32, (half, half), 1)) // CHUNK
    quad_allowed = col_chunk <= row_chunk
    diag_start = pl.multiple_of(qi * ATT_K, ATT_K)
    masked = jnp.float32(-1e30)

    def lane_tiles(x):
        return [x[:, t * LANES:(t + 1) * LANES] for t in range(x.shape[1] // LANES)]

    def lane_fold(x, op):
        return functools.reduce(op, lane_tiles(x))

    def scores(hd, j):
        start = pl.multiple_of(j * ATT_K, ATT_K)
        q_h = q_ref[:, hd * 2 * LANES:(hd + 1) * 2 * LANES]
        kb = k_ref[pl.ds(start, ATT_K), hd * 2 * LANES:(hd + 1) * 2 * LANES]
        return lax.dot_general(q_h, kb, nt, preferred_element_type=jnp.float32)

    for hd in range(MLA_HEADS):
        cols = slice(hd * 2 * LANES, (hd + 1) * 2 * LANES)
        k_lo = k_ref[pl.ds(diag_start, half), cols]
        k_hi = k_ref[pl.ds(diag_start + half, half), cols]
        qk = lambda q, k: lax.dot_general(q, k, nt, preferred_element_type=jnp.float32)
        s_tl = jnp.where(quad_allowed, qk(q_ref[:half, cols], k_lo), masked)
        s_bl = qk(q_ref[half:, cols], k_lo)
        s_br = jnp.where(quad_allowed, qk(q_ref[half:, cols], k_hi), masked)
        diag = s_ref.at[hd * n_kv + qi]
        diag[:half, :half] = s_tl
        diag[half:, :half] = s_bl
        diag[half:, half:] = s_br
        mx_ref[hd, :half] = lane_fold(s_tl, jnp.maximum)
        mx_ref[hd, half:] = jnp.maximum(lane_fold(s_bl, jnp.maximum), lane_fold(s_br, jnp.maximum))

    def pass_a(j, c):
        for hd in range(MLA_HEADS):
            s = scores(hd, j)
            s_ref[hd * n_kv + j] = s
            mx_ref[hd] = jnp.maximum(mx_ref[hd], lane_fold(s, jnp.maximum))
        return c

    lax.fori_loop(0, qi, pass_a, 0)

    for hd in range(MLA_HEADS):
        mx_ref[hd] = jnp.broadcast_to(jnp.max(mx_ref[hd], axis=-1, keepdims=True), (tq, LANES))

    def probs(hd, j):
        s = s_ref[hd * n_kv + j]
        mb = mx_ref[hd]
        p = jnp.exp2(jnp.concatenate([t - mb for t in lane_tiles(s)], axis=-1))
        start = pl.multiple_of(j * ATT_K, ATT_K)
        vb = v_ref[pl.ds(start, ATT_K), hd * LANES:(hd + 1) * LANES]
        pv = jnp.dot(p.astype(jnp.bfloat16), vb, preferred_element_type=jnp.float32)
        return lane_fold(p, jnp.add), pv

    for hd in range(MLA_HEADS):
        diag = s_ref.at[hd * n_kv + qi]
        vcols = slice(hd * LANES, (hd + 1) * LANES)
        p_top = jnp.exp2(jnp.concatenate([t - mx_ref[hd, :half] for t in lane_tiles(diag[:half, :half])], axis=-1))
        p_bot = jnp.exp2(jnp.concatenate([t - mx_ref[hd, half:] for t in lane_tiles(diag[half:, :])], axis=-1))
        ls_ref[hd, :half] = lane_fold(p_top, jnp.add)
        ls_ref[hd, half:] = lane_fold(p_bot, jnp.add)
        acc_ref[hd, :half] = jnp.dot(p_top.astype(jnp.bfloat16), v_ref[pl.ds(diag_start, half), vcols],
                                     preferred_element_type=jnp.float32)
        acc_ref[hd, half:] = jnp.dot(p_bot.astype(jnp.bfloat16), v_ref[pl.ds(diag_start, ATT_K), vcols],
                                     preferred_element_type=jnp.float32)

    def pass_b(j, c):
        for hd in range(MLA_HEADS):
            ls, pv = probs(hd, j)
            ls_ref[hd] = ls_ref[hd] + ls
            acc_ref[hd] = acc_ref[hd] + pv
        return c

    lax.fori_loop(0, qi, pass_b, 0)

    for hd in range(MLA_HEADS):
        l = jnp.sum(ls_ref[hd], axis=-1, keepdims=True)
        o_ref[:, hd * V_HEAD_DIM:(hd + 1) * V_HEAD_DIM] = (acc_ref[hd] / l).astype(jnp.bfloat16)


def _attn(q, k, v):
    tq = ATT_Q
    tokblk = lambda cols: pl.BlockSpec((tq, cols), lambda b, i: (b * N_QBLK + i, 0))
    seqblk = lambda cols: pl.BlockSpec((SEQ, cols), lambda b, i: (b, 0))
    return pl.pallas_call(
        _attn_kernel,
        grid=(BATCH, N_QBLK),
        in_specs=[tokblk(D_MODEL), seqblk(D_MODEL), seqblk(GM_WIDTH)],
        out_specs=tokblk(GM_WIDTH),
        out_shape=jax.ShapeDtypeStruct((N_TOK, MLA_HEADS * V_HEAD_DIM), jnp.bfloat16),
        scratch_shapes=[pltpu.VMEM((MLA_HEADS * (SEQ // ATT_K), tq, ATT_K), jnp.float32),
                        pltpu.VMEM((MLA_HEADS, tq, LANES), jnp.float32),
                        pltpu.VMEM((MLA_HEADS, tq, LANES), jnp.float32),
                        pltpu.VMEM((MLA_HEADS, tq, V_HEAD_DIM), jnp.float32)],
        compiler_params=pltpu.CompilerParams(
            dimension_semantics=("arbitrary", "arbitrary"), vmem_limit_bytes=VMEM_LIMIT),
        name="attn",
    )(q, k, v)


def _proj_kernel(outa_ref, ob_ref, h_hbm_ref, wout_ref, ln1g_ref, ln1b_ref, wr_ref, br_ref,
                 h1_ref, h1b_ref, ri_ref, rf_ref, proj_ref, hring_ref, hsems):
    i = pl.program_id(0)
    tb = outa_ref.shape[0]
    h_ref = _ring_block(h_hbm_ref, hring_ref, hsems, lambda step: jnp.maximum(step - 1, 0), tb)

    @pl.when(i == 0)
    def _():
        proj_ref[...] = jnp.zeros_like(proj_ref)

    h1 = _layer_norm(ALPHA * h_ref[...] + proj_ref[(i + 1) % 2], ln1g_ref[...], ln1b_ref[...])
    h1_ref[...] = h1
    h1b_ref[...] = h1.astype(jnp.bfloat16)

    logits_tm = jnp.dot(h1b_ref[...], wr_ref[...], preferred_element_type=jnp.float32)
    logits = logits_tm.T[0:ROUTER_ROWS] + br_ref[...]

    sub_i = lax.broadcasted_iota(jnp.int32, (SUBLANES, tb), 0)
    sub = sub_i.astype(jnp.float32)
    neg = jnp.float32(-jnp.inf)
    g = jnp.where(sub_i < N_GROUPS, logits[0:SUBLANES], neg)
    gmax = jnp.max(g, axis=0, keepdims=True)
    g_top = jnp.min(jnp.where(g == gmax, sub, float(SUBLANES)), axis=0, keepdims=True)
    p_group = 1.0 / jnp.sum(jnp.exp(g - gmax), axis=0, keepdims=True)
    sel = logits[SUBLANES:2 * SUBLANES]
    for grp in range(1, N_GROUPS):
        sel = jnp.where(g_top == float(grp), logits[(grp + 1) * SUBLANES:(grp + 2) * SUBLANES], sel)
    v1 = jnp.max(sel, axis=0, keepdims=True)
    i1 = jnp.min(jnp.where(sel == v1, sub, float(SUBLANES)), axis=0, keepdims=True)
    sel2 = jnp.where(sub == i1, neg, sel)
    v2 = jnp.max(sel2, axis=0, keepdims=True)
    i2 = jnp.min(jnp.where(sel2 == v2, sub, float(SUBLANES)), axis=0, keepdims=True)
    e21 = jnp.exp(v2 - v1)
    w1 = 1.0 / (1.0 + e21)
    gate1 = p_group * w1
    gate2 = p_group * (e21 * w1)
    e1 = g_top * EXPERTS_PER_GROUP + i1
    e2 = g_top * EXPERTS_PER_GROUP + i2
    ri_ref[...] = jnp.where(sub_i == 0, e1, jnp.where(sub_i == 1, e2, 0.0)).astype(jnp.int32)
    rf_ref[...] = jnp.where(sub_i == 0, gate1, jnp.where(sub_i == 1, gate2, 0.0))

    proj_ref[i % 2] = (jnp.dot(outa_ref[...], wout_ref[:GM_WIDTH, :], preferred_element_type=jnp.float32)
                       + jnp.dot(ob_ref[...], wout_ref[GM_WIDTH:, :], preferred_element_type=jnp.float32))


def _proj(outa, ob, h, wout, ln1g, ln1b, wr, br):
    tb = PROJ_TOKENS
    n_blk = N_TOK // tb
    cur = lambda i: jnp.minimum(i, n_blk - 1)
    prev = lambda i: jnp.maximum(i - 1, 0)
    full = lambda shape: pl.BlockSpec(shape, lambda i: (0,) * len(shape))
    return pl.pallas_call(
        _proj_kernel,
        grid=(n_blk + 1,),
        in_specs=[pl.BlockSpec((tb, GM_WIDTH), lambda i: (cur(i), 0)),
                  pl.BlockSpec((tb, GM_WIDTH), lambda i: (cur(i), 0)),
                  pl.BlockSpec(memory_space=pl.ANY),
                  full((D_MODEL, D_MODEL)), full((1, D_MODEL)), full((1, D_MODEL)),
                  full((D_MODEL, LANES)), full((ROUTER_ROWS, 1))],
        out_specs=[pl.BlockSpec((tb, D_MODEL), lambda i: (prev(i), 0)),
                   pl.BlockSpec((tb, D_MODEL), lambda i: (prev(i), 0)),
                   pl.BlockSpec((SUBLANES, tb), lambda i: (0, prev(i))),
                   pl.BlockSpec((SUBLANES, tb), lambda i: (0, prev(i)))],
        out_shape=[jax.ShapeDtypeStruct((N_TOK, D_MODEL), jnp.float32),
                   jax.ShapeDtypeStruct((N_TOK, D_MODEL), jnp.bfloat16),
                   jax.ShapeDtypeStruct((SUBLANES, N_TOK), jnp.int32),
                   jax.ShapeDtypeStruct((SUBLANES, N_TOK), jnp.float32)],
        scratch_shapes=[pltpu.VMEM((2, tb, D_MODEL), jnp.float32)] + _ring_scratch(tb, D_MODEL, jnp.float32),
        compiler_params=pltpu.CompilerParams(
            dimension_semantics=("arbitrary",), vmem_limit_bytes=VMEM_LIMIT),
        name="proj",
    )(outa, ob, h, wout, ln1g, ln1b, wr, br)


def _plan_kernel(ri_all_ref, ri_ref, ldest_ref, runs_ref, meta_ref, run_ref, start_ref, upper_ref):
    step = pl.program_id(0)
    tb = ri_ref.shape[1]
    f32 = jnp.float32
    er = lax.broadcasted_iota(jnp.int32, (N_EXPERTS, LANES), 0)
    ec = lax.broadcasted_iota(jnp.int32, (N_EXPERTS, LANES), 1)
    to_row = lambda col: jnp.sum(jnp.where(er == ec, col, 0.0), axis=0, keepdims=True)

    def expert_one_hot(ref):
        e_sub = lax.broadcasted_iota(jnp.int32, (N_EXPERTS, ref.shape[1]), 0)
        return e_sub == ref[0:1, :], e_sub == ref[1:2, :]

    @pl.when(step == 0)
    def _():
        oh1, oh2 = expert_one_hot(ri_all_ref)
        counts = jnp.sum(jnp.where(oh1 | oh2, 1.0, 0.0), axis=1, keepdims=True)
        padded = jnp.floor((counts + (EXPERT_ROWS - 1)) * (1.0 / EXPERT_ROWS)) * EXPERT_ROWS
        pad_end = jnp.sum(jnp.where(ec <= er, to_row(padded), 0.0), axis=1, keepdims=True)
        start_ref[...] = jnp.broadcast_to(pad_end - padded, start_ref.shape)
        run_ref[...] = jnp.zeros_like(run_ref)
        bstart = (lax.broadcasted_iota(jnp.int32, (N_EXPERTS, META_LANES), 1) * EXPERT_ROWS).astype(f32)
        blk_e = jnp.sum(jnp.where(pad_end <= bstart, 1.0, 0.0), axis=0, keepdims=True)
        blk_e = jnp.minimum(blk_e, N_EXPERTS - 1.0)
        n_used = pad_end[N_EXPERTS - 1:N_EXPERTS, :] * (1.0 / EXPERT_ROWS)
        pad3 = lambda r: jnp.concatenate(
            [r, jnp.zeros((1, META_LANES - LANES), f32)], axis=1)
        msub = lax.broadcasted_iota(jnp.int32, (SUBLANES, META_LANES), 0)
        meta = jnp.where(msub == 0, blk_e,
                         jnp.where(msub == 1, pad3(to_row(pad_end)),
                                   jnp.where(msub == 2, pad3(to_row(counts)),
                                             jnp.where(msub == 3, n_used, 0.0))))
        meta_ref[...] = meta.astype(jnp.int32)
        tr = lax.broadcasted_iota(jnp.int32, (tb, tb), 0)
        tc = lax.broadcasted_iota(jnp.int32, (tb, tb), 1)
        upper_ref[...] = jnp.where(tr < tc, 1.0, 0.0).astype(jnp.bfloat16)

    @pl.when(step > 0)
    def _():
        oh1, oh2 = expert_one_hot(ri_ref)
        oh = jnp.where(oh1 | oh2, 1.0, 0.0).astype(f32)
        blk_count = jnp.sum(oh, axis=1, keepdims=True)
        prefix = jnp.dot(oh.astype(jnp.bfloat16), upper_ref[...], preferred_element_type=f32)
        cnt_row = to_row(blk_count)
        lstart = jnp.sum(jnp.where(ec < er, cnt_row, 0.0), axis=1, keepdims=True)
        base = prefix + lstart
        d1 = jnp.sum(jnp.where(oh1, base, 0.0), axis=0, keepdims=True)
        d2 = jnp.sum(jnp.where(oh2, base, 0.0), axis=0, keepdims=True)
        sub = lax.broadcasted_iota(jnp.int32, (SUBLANES, tb), 0)
        ldest_ref[...] = jnp.where(sub == 0, d1, jnp.where(sub == 1, d2, 0.0)).astype(jnp.int32)
        gstart = start_ref[:, 0:1] + run_ref[:, 0:1]
        rsub = lax.broadcasted_iota(jnp.int32, (SUBLANES, LANES), 0)
        runs = jnp.where(rsub == 0, cnt_row,
                         jnp.where(rsub == 1, to_row(lstart * FEAT_TILES),
                                   jnp.where(rsub == 2, to_row(gstart * FEAT_TILES), 0.0)))
        runs_ref[...] = runs.astype(jnp.int32)
        run_ref[...] = run_ref[...] + blk_count


def _plan(ri):
    tb = MOE_TOKENS
    blk = lambda i: jnp.maximum(i - 1, 0)
    return pl.pallas_call(
        _plan_kernel,
        grid=(N_MOE_BLOCKS + 1,),
        in_specs=[pl.BlockSpec((SUBLANES, N_TOK), lambda i: (0, 0)),
                  pl.BlockSpec((SUBLANES, tb), lambda i: (0, blk(i)))],
        out_specs=[pl.BlockSpec((SUBLANES, tb), lambda i: (0, blk(i))),
                   pl.BlockSpec((SUBLANES, LANES), lambda i: (blk(i), 0)),
                   pl.BlockSpec((SUBLANES, META_LANES), lambda i: (0, 0))],
        out_shape=[jax.ShapeDtypeStruct((SUBLANES, N_TOK), jnp.int32),
                   jax.ShapeDtypeStruct((N_MOE_BLOCKS * SUBLANES, LANES), jnp.int32),
                   jax.ShapeDtypeStruct((SUBLANES, META_LANES), jnp.int32)],
        scratch_shapes=[pltpu.VMEM((N_EXPERTS, LANES), jnp.float32),
                        pltpu.VMEM((N_EXPERTS, LANES), jnp.float32),
                        pltpu.VMEM((tb, tb), jnp.bfloat16)],
        compiler_params=pltpu.CompilerParams(
            dimension_semantics=("arbitrary",), vmem_limit_bytes=VMEM_LIMIT),
        name="plan",
    )(ri, ri)


def _for_each_run_piece(runs_ref, row0, fn, live=True):
    for e in range(N_EXPERTS):
        n_tiles = jnp.where(live, runs_ref[row0, e], 0) * FEAT_TILES
        lrow, grow = runs_ref[row0 + 1, e], runs_ref[row0 + 2, e]
        for bit in reversed(range(RUN_BITS)):
            piece = n_tiles & (FEAT_TILES << bit)

            @pl.when(piece != 0)
            def _(lrow=lrow, grow=grow, bit=bit):
                fn(lrow, grow, 1 << bit)

            lrow, grow = lrow + piece, grow + piece


def _tile_rows(ref, tile_row, rows):
    return ref.at[pl.ds(pl.multiple_of(tile_row, FEAT_TILES), rows * FEAT_TILES)]


def _dispatch_kernel(meta_ref, runs_ref, ldest_ref, h1b_ref, buf_ref, *scratch):
    sorted_refs = scratch[:MOE_GROUP]
    zero_ref, sems, zsem = scratch[MOE_GROUP:]
    i = pl.program_id(0)
    n_steps = pl.num_programs(0)
    tb = MOE_TOKENS
    slot = i % 2
    block_tiles = 2 * tb * FEAT_TILES

    def wait_slot(g, s):
        pltpu.make_async_copy(sorted_refs[g].at[s], buf_ref.at[pl.ds(0, block_tiles)], sems.at[g, s]).wait()

    @pl.when(i == 0)
    def _():
        zero_ref[...] = jnp.zeros_like(zero_ref)

        def zero_copy(e):
            start = pl.multiple_of((meta_ref[1, e] - EXPERT_ROWS) * FEAT_TILES, EXPERT_ROWS * FEAT_TILES)
            return pltpu.make_async_copy(
                zero_ref, buf_ref.at[pl.ds(start, EXPERT_ROWS * FEAT_TILES)], zsem)

        def start_zero(e, c):
            @pl.when(meta_ref[2, e] > 0)
            def _():
                zero_copy(e).start()
            return c

        def wait_zero(e, c):
            @pl.when(meta_ref[2, e] > 0)
            def _():
                zero_copy(e).wait()
            return c

        def tail_copy(b):
            start = pl.multiple_of(b * (EXPERT_ROWS * FEAT_TILES), EXPERT_ROWS * FEAT_TILES)
            return pltpu.make_async_copy(
                zero_ref, buf_ref.at[pl.ds(start, EXPERT_ROWS * FEAT_TILES)], zsem)

        def start_tail(b, c):
            tail_copy(b).start()
            return c

        def wait_tail(b, c):
            tail_copy(b).wait()
            return c

        lax.fori_loop(0, N_EXPERTS, start_zero, 0)
        lax.fori_loop(meta_ref[3, 0], N_ROW_BLOCKS, start_tail, 0)
        lax.fori_loop(0, N_EXPERTS, wait_zero, 0)
        lax.fori_loop(meta_ref[3, 0], N_ROW_BLOCKS, wait_tail, 0)

    @pl.when(i >= 2)
    def _():
        for g in range(MOE_GROUP):
            wait_slot(g, slot)

    for g in range(MOE_GROUP):
        x = h1b_ref[g * tb:(g + 1) * tb, :]
        ld0 = ldest_ref[0:1, g * tb:(g + 1) * tb]
        ld1 = ldest_ref[1:2, g * tb:(g + 1) * tb]
        for c in range(2 * tb // SORT_CHUNK):
            r = lax.broadcasted_iota(jnp.int32, (SORT_CHUNK, tb), 0) + c * SORT_CHUNK
            perm = jnp.where((r == ld0) | (r == ld1), 1.0, 0.0).astype(jnp.bfloat16)
            rows = jnp.dot(perm, x, preferred_element_type=jnp.float32)
            _to_row_tiles(sorted_refs[g].at[slot, pl.ds(c * SORT_CHUNK * FEAT_TILES, SORT_CHUNK * FEAT_TILES)],
                          rows)

        def send(lrow, grow, rows, g=g):
            pltpu.make_async_copy(_tile_rows(sorted_refs[g].at[slot], lrow, rows),
                                  _tile_rows(buf_ref, grow, rows), sems.at[g, slot]).start()

        _for_each_run_piece(runs_ref, g * SUBLANES, send)

    @pl.when(i == n_steps - 1)
    def _():
        for g in range(MOE_GROUP):
            wait_slot(g, slot)
            wait_slot(g, 1 - slot)


def _dispatch(meta, runs, ldest, h1b):
    tb = MOE_GROUP * MOE_TOKENS
    return pl.pallas_call(
        _dispatch_kernel,
        grid_spec=pltpu.PrefetchScalarGridSpec(
            num_scalar_prefetch=1,
            grid=(N_MOE_BLOCKS // MOE_GROUP,),
            in_specs=[pl.BlockSpec((MOE_GROUP * SUBLANES, LANES), lambda i, m: (i, 0), memory_space=pltpu.SMEM),
                      pl.BlockSpec((SUBLANES, tb), lambda i, m: (0, i)),
                      pl.BlockSpec((tb, D_MODEL), lambda i, m: (i, 0))],
            out_specs=pl.BlockSpec(memory_space=pl.ANY),
            scratch_shapes=[pltpu.VMEM((2, 2 * MOE_TOKENS * FEAT_TILES, LANES), jnp.float32)] * MOE_GROUP + [
                pltpu.VMEM((EXPERT_ROWS * FEAT_TILES, LANES), jnp.float32),
                pltpu.SemaphoreType.DMA((MOE_GROUP, 2)), pltpu.SemaphoreType.DMA]),
        out_shape=jax.ShapeDtypeStruct((N_ROWS * FEAT_TILES, LANES), jnp.float32),
        compiler_params=pltpu.CompilerParams(
            dimension_semantics=("arbitrary",), vmem_limit_bytes=VMEM_LIMIT),
        name="dispatch",
    )(meta, runs, ldest, h1b)


def _sub_block_expert(meta, i, sub):
    return meta[0, jnp.minimum(i + sub * (N_ROW_BLOCKS // EXPERT_SUB), meta[3, 0] - 1)]


def _experts_kernel(meta_ref, x_hbm_ref, *refs):
    weights = [refs[3 * sub:3 * sub + 3] for sub in range(EXPERT_SUB)]
    o_ref, wgb_ref, wub_ref, wdb_ref, cached_ref, xring_ref, xsems = refs[3 * EXPERT_SUB:]
    n = EXPERT_ROWS
    i = pl.program_id(0)
    n_steps = pl.num_programs(0)

    def rows_copy(step):
        slot = step % EXPERT_RING
        start = pl.multiple_of(step * (n * FEAT_TILES), n * FEAT_TILES)
        return pltpu.make_async_copy(x_hbm_ref.at[:, pl.ds(start, n * FEAT_TILES), :], xring_ref.at[slot],
                                     xsems.at[slot])

    @pl.when(i == 0)
    def _():
        for step in range(EXPERT_RING - 1):
            rows_copy(step).start()

    @pl.when(i + (EXPERT_RING - 1) < n_steps)
    def _():
        rows_copy(i + (EXPERT_RING - 1)).start()

    rows_copy(i).wait()
    x_ref = xring_ref.at[i % EXPERT_RING]

    @pl.when(i == 0)
    def _():
        for sub in range(EXPERT_SUB):
            cached_ref[sub] = -1

    for sub in range(EXPERT_SUB):
        expert = _sub_block_expert(meta_ref, i, sub)

        @pl.when(cached_ref[sub] != expert)
        def _(sub=sub, expert=expert):
            wg_ref, wu_ref, wd_ref = weights[sub]
            wgb_ref[sub] = wg_ref[...].astype(jnp.bfloat16)
            wub_ref[sub] = wu_ref[...].astype(jnp.bfloat16)
            wdb_ref[sub] = wd_ref[...].astype(jnp.bfloat16)
            cached_ref[sub] = expert

    for sub in range(EXPERT_SUB):
        x = _from_row_tiles(x_ref.at[sub], n).astype(jnp.bfloat16)
        gate = jnp.dot(x, wgb_ref[sub], preferred_element_type=jnp.float32)
        up = jnp.dot(x, wub_ref[sub], preferred_element_type=jnp.float32)
        act = (gate * jax.nn.sigmoid(gate) * up).astype(jnp.bfloat16)
        _to_row_tiles(o_ref.at[sub], jnp.dot(act, wdb_ref[sub], preferred_element_type=jnp.float32))


def _experts(meta, buf, wg, wu, wd):
    sub_tiles = (N_ROW_BLOCKS // EXPERT_SUB) * EXPERT_ROWS * FEAT_TILES

    def weight_spec(shape, sub):
        return pl.BlockSpec((None,) + shape, lambda i, m: (_sub_block_expert(m, i, sub), 0, 0))

    rows_shape = (EXPERT_SUB, EXPERT_ROWS * FEAT_TILES, LANES)
    up_shape, down_shape = (D_MODEL, EXPERT_FF), (EXPERT_FF, D_MODEL)
    eout = pl.pallas_call(
        _experts_kernel,
        grid_spec=pltpu.PrefetchScalarGridSpec(
            num_scalar_prefetch=1,
            grid=(N_ROW_BLOCKS // EXPERT_SUB,),
            in_specs=[pl.BlockSpec(memory_space=pl.ANY)] + [
                weight_spec(shape, sub) for sub in range(EXPERT_SUB) for shape in (up_shape, up_shape, down_shape)],
            out_specs=pl.BlockSpec(rows_shape, lambda i, m: (0, i, 0)),
            scratch_shapes=[pltpu.VMEM((EXPERT_SUB,) + up_shape, jnp.bfloat16),
                            pltpu.VMEM((EXPERT_SUB,) + up_shape, jnp.bfloat16),
                            pltpu.VMEM((EXPERT_SUB,) + down_shape, jnp.bfloat16),
                            pltpu.SMEM((EXPERT_SUB,), jnp.int32),
                            pltpu.VMEM((EXPERT_RING,) + rows_shape, jnp.float32),
                            pltpu.SemaphoreType.DMA((EXPERT_RING,))]),
        out_shape=jax.ShapeDtypeStruct((EXPERT_SUB, sub_tiles, LANES), jnp.float32),
        compiler_params=pltpu.CompilerParams(
            dimension_semantics=("arbitrary",), vmem_limit_bytes=EXPERT_VMEM_LIMIT),
        name="experts",
    )(meta, buf.reshape(EXPERT_SUB, sub_tiles, LANES), *([wg, wu, wd] * EXPERT_SUB))
    return eout.reshape(N_ROWS * FEAT_TILES, LANES)


def _combine_kernel(runs_ref, runs_next_ref, ldest_ref, rf_ref, h1_ref, eout_ref, ln2g_ref, ln2b_ref,
                    o_ref, *scratch):
    y_refs = scratch[:MOE_GROUP]
    sems = scratch[MOE_GROUP]
    i = pl.program_id(0)
    n_steps = pl.num_programs(0)
    tb = MOE_TOKENS
    slot = i % 2
    block_tiles = 2 * tb * FEAT_TILES

    def fetch(table_ref, g, s, live):
        def recv(lrow, grow, rows):
            pltpu.make_async_copy(_tile_rows(eout_ref, grow, rows),
                                  _tile_rows(y_refs[g].at[s], lrow, rows), sems.at[g, s]).start()
        _for_each_run_piece(table_ref, g * SUBLANES, recv, live)

    @pl.when(i == 0)
    def _():
        for g in range(MOE_GROUP):
            fetch(runs_ref, g, slot, True)

    for g in range(MOE_GROUP):
        pltpu.make_async_copy(eout_ref.at[pl.ds(0, block_tiles)], y_refs[g].at[slot], sems.at[g, slot]).wait()
        fetch(runs_next_ref, g, 1 - slot, i + 1 < n_steps)

        ld = ldest_ref[:, g * tb:(g + 1) * tb].astype(jnp.float32).T
        gates = rf_ref[:, g * tb:(g + 1) * tb].T
        y = None
        col = lax.broadcasted_iota(jnp.int32, (tb, SORT_CHUNK), 1).astype(jnp.float32)
        for c in range(2 * tb // SORT_CHUNK):
            ld_c = ld - float(c * SORT_CHUNK)
            gm = jnp.where(col == ld_c[:, 0:1], gates[:, 0:1],
                           jnp.where(col == ld_c[:, 1:2], gates[:, 1:2], 0.0)).astype(jnp.bfloat16)
            rows = _from_row_tiles(
                y_refs[g].at[slot, pl.ds(c * SORT_CHUNK * FEAT_TILES, SORT_CHUNK * FEAT_TILES)], SORT_CHUNK)
            part = jnp.dot(gm, rows.astype(jnp.bfloat16), preferred_element_type=jnp.float32)
            y = part if y is None else y + part
        tok = slice(g * tb, (g + 1) * tb)
        o_ref[tok, :] = _layer_norm(ALPHA * h1_ref[tok, :] + y, ln2g_ref[...], ln2b_ref[...])


def _combine(runs, ldest, rf, h1, eout, ln2g, ln2b):
    tb = MOE_GROUP * MOE_TOKENS
    n_steps = N_MOE_BLOCKS // MOE_GROUP
    table = lambda index_map: pl.BlockSpec((MOE_GROUP * SUBLANES, LANES), index_map, memory_space=pltpu.SMEM)
    return pl.pallas_call(
        _combine_kernel,
        grid=(n_steps,),
        in_specs=[table(lambda i: (i, 0)), table(lambda i: (jnp.minimum(i + 1, n_steps - 1), 0)),
                  pl.BlockSpec((SUBLANES, tb), lambda i: (0, i)),
                  pl.BlockSpec((SUBLANES, tb), lambda i: (0, i)),
                  pl.BlockSpec((tb, D_MODEL), lambda i: (i, 0)),
                  pl.BlockSpec(memory_space=pl.ANY),
                  pl.BlockSpec((1, D_MODEL), lambda i: (0, 0)),
                  pl.BlockSpec((1, D_MODEL), lambda i: (0, 0))],
        out_specs=pl.BlockSpec((tb, D_MODEL), lambda i: (i, 0)),
        out_shape=jax.ShapeDtypeStruct((N_TOK, D_MODEL), jnp.float32),
        scratch_shapes=[pltpu.VMEM((2, 2 * MOE_TOKENS * FEAT_TILES, LANES), jnp.float32)] * MOE_GROUP + [
            pltpu.SemaphoreType.DMA((MOE_GROUP, 2))],
        compiler_params=pltpu.CompilerParams(
            dimension_semantics=("arbitrary",), vmem_limit_bytes=VMEM_LIMIT),
        name="combine",
    )(runs, runs, ldest, rf, h1, eout, ln2g, ln2b)


def _swap_halves(w):
    half = w.shape[-1] // 2
    return jnp.concatenate([w[..., half:], w[..., :half]], axis=-1)


def kernel(x, positions, ln0_g, ln0_b, w_in, gm_ln_g, gm_ln_b, w_spatial, b_spatial, q_norm_g, w_uq, kv_norm_g, w_ukv, w_out, ln1_g, ln1_b, w_router_group, b_router_group, w_router_expert, b_router_expert, w_gate, w_up, w_down, ln2_g, ln2_b):
    bf16 = jnp.bfloat16
    row = lambda a: a.reshape(1, -1)

    w_in0 = w_in[0]
    kr_cols = w_in0[:, O_KR:O_KR + QK_ROPE_DIM]
    win = jnp.concatenate([w_in0, _swap_halves(kr_cols)], axis=1).astype(bf16)
    wuq3 = w_uq[0].reshape(Q_LORA_RANK, MLA_HEADS, QK_NOPE_DIM + QK_ROPE_DIM)
    rope_cols = wuq3[:, :, QK_NOPE_DIM:]
    wuq = jnp.concatenate([wuq3, _swap_halves(rope_cols)], axis=-1).reshape(Q_LORA_RANK, D_MODEL).astype(bf16)
    wukv = w_ukv[0].astype(bf16)
    wout = w_out[0].astype(bf16)
    bs = jnp.broadcast_to(b_spatial[0][:, :, None], (GM_HEADS, GM_CHUNK, GM_HEAD_DIM))
    wr = jnp.concatenate([w_router_group[0], jnp.zeros((D_MODEL, SUBLANES - N_GROUPS), jnp.float32),
                          w_router_expert[0],
                          jnp.zeros((D_MODEL, LANES - ROUTER_ROWS), jnp.float32)],
                         axis=1).astype(bf16)
    br = jnp.concatenate([b_router_group[0], jnp.zeros((SUBLANES - N_GROUPS,), jnp.float32),
                          b_router_expert[0]]).reshape(ROUTER_ROWS, 1)

    inv_freq = ROPE_THETA ** (-jnp.arange(0, QK_ROPE_DIM, 2, dtype=jnp.float32) / QK_ROPE_DIM)
    freq = jnp.tile(inv_freq, 4).reshape(1, LANES)
    quarter = QK_ROPE_DIM // 2
    phase = jnp.concatenate([jnp.zeros((2 * quarter,), jnp.float32),
                             jnp.full((2 * quarter,), math.pi / 2, jnp.float32)]).reshape(1, LANES)
    sign = jnp.concatenate([jnp.ones((2 * quarter,), jnp.float32), -jnp.ones((quarter,), jnp.float32),
                            jnp.ones((quarter,), jnp.float32)]).reshape(1, LANES)

    x2 = x.reshape(N_TOK, D_MODEL)
    pos2 = positions.reshape(N_TOK // PREP_TOKENS, 1, PREP_TOKENS)

    h, outa, q, k, v = _prep(x2, pos2, row(ln0_g), row(ln0_b), win, row(gm_ln_g[0]), row(gm_ln_b[0]),
                             w_spatial[0], bs, row(q_norm_g[0]), wuq, row(kv_norm_g[0]), wukv,
                             freq, phase, sign)
    ob = _attn(q, k, v)
    h1, h1b, ri, rf = _proj(outa, ob, h, wout, row(ln1_g[0]), row(ln1_b[0]), wr, br)
    ldest, runs, meta = _plan(ri)
    buf = _dispatch(meta, runs, ldest, h1b)
    eout = _experts(meta, buf, w_gate[0], w_up[0], w_down[0])
    out = _combine(runs, ldest, rf, h1, eout, row(ln2_g[0]), row(ln2_b[0]))
    return out.reshape(BATCH, SEQ, D_MODEL)
```
